```python
import math
import jax, jax.numpy as jnp
from jax import lax
import numpy as np

D_MODEL = 1024
BATCH = 8
SEQ = 4096
DEPTH = 2

A_WIDTH = D_MODEL // 2
A_GROUPS = 4
A_KERNEL = 31
B_WIDTH = D_MODEL // 2
B_GROUPS = 4
B_GROUP_DIM = B_WIDTH // B_GROUPS
B_CHUNK = 128
HEAD_DIM = 64
N_Q_HEADS = D_MODEL // HEAD_DIM
N_KV_HEADS = N_Q_HEADS // 8
Q_PER_KV = N_Q_HEADS // N_KV_HEADS
WINDOW = 128
ATT_BLOCK = 128
D_FF = 2816
FFN_KERNEL = 3
N_EVEN = (DEPTH + 1) // 2
N_ODD = DEPTH // 2
DEEPNORM_ALPHA = (2.0 * DEPTH) ** 0.25
DEEPNORM_BETA = (8.0 * DEPTH) ** -0.25
LN_EPS = 1e-5

kernel_name = "hybrid_conv_gmlp_swa_sink_deepnorm"


def layer_norm(x, g, b):
    xf = x.astype(jnp.float32)
    mu = jnp.mean(xf, axis=-1, keepdims=True)
    var = jnp.mean(jnp.square(xf - mu), axis=-1, keepdims=True)
    y = (xf - mu) * lax.rsqrt(var + LN_EPS)
    return (y * g.astype(jnp.float32) + b.astype(jnp.float32)).astype(x.dtype)


def causal_dwconv(x, w, b):
    k = w.shape[0]
    c = x.shape[-1]
    y = lax.conv_general_dilated(
        x, w[:, None, :].astype(x.dtype), window_strides=(1,), padding=[(k - 1, 0)],
        dimension_numbers=("NWC", "WIO", "NWC"), feature_group_count=c)
    return y + b.astype(x.dtype)


def conv_gmlp_mixer(x, w_in, a_conv_w, a_conv_b, a_norm_g, a_norm_b,
                    b_norm_g, b_norm_b, b_spatial_w, b_spatial_b, w_out):
    bsz, seq, _ = x.shape
    h = x @ w_in
    a_val, a_gate, b_u, b_v = jnp.split(h, 4, axis=-1)
    a = a_val * jax.nn.sigmoid(a_gate)
    a = causal_dwconv(a, a_conv_w, a_conv_b)
    a = jax.nn.silu(layer_norm(a, a_norm_g, a_norm_b))
    b_u = jax.nn.gelu(b_u)
    b_v = layer_norm(jax.nn.gelu(b_v), b_norm_g, b_norm_b)
    n_chunks = seq // B_CHUNK
    vb = b_v.reshape(bsz, n_chunks, B_CHUNK, B_GROUPS, B_GROUP_DIM)
    causal = jnp.tril(jnp.ones((B_CHUNK, B_CHUNK), dtype=bool))
    ws = jnp.where(causal[None], b_spatial_w, jnp.zeros((), b_spatial_w.dtype))
    mixed = jnp.einsum("gij,bcjgd->bcigd", ws, vb) + b_spatial_b.T[None, None, :, :, None]
    b = b_u * mixed.reshape(bsz, seq, B_WIDTH)
    return jnp.concatenate([a, b], axis=-1) @ w_out


def swa_sink_attention(x, w_qkv, b_qkv, sinks, w_o):
    bsz, seq, _ = x.shape
    nb = seq // ATT_BLOCK
    qkv = x @ w_qkv + b_qkv
    q, k, v = jnp.split(qkv, [N_Q_HEADS * HEAD_DIM, (N_Q_HEADS + N_KV_HEADS) * HEAD_DIM], axis=-1)
    q = q.reshape(bsz, nb, ATT_BLOCK, N_KV_HEADS, Q_PER_KV, HEAD_DIM)
    k = k.reshape(bsz, nb, ATT_BLOCK, N_KV_HEADS, HEAD_DIM)
    v = v.reshape(bsz, nb, ATT_BLOCK, N_KV_HEADS, HEAD_DIM)

    def with_prev(t):
        prev = jnp.concatenate([jnp.zeros_like(t[:, :1]), t[:, :-1]], axis=1)
        return jnp.concatenate([prev, t], axis=2)

    kk, vv = with_prev(k), with_prev(v)
    scale = 1.0 / math.sqrt(HEAD_DIM)
    scores = jnp.einsum("bnqkgd,bnskd->bnkgqs", q, kk).astype(jnp.float32) * scale
    qi = jnp.arange(ATT_BLOCK)[:, None]
    sj = jnp.arange(2 * ATT_BLOCK)[None, :]
    diff = qi + ATT_BLOCK - sj
    in_window = (diff >= 0) & (diff < WINDOW)
    blk = jnp.arange(nb)[:, None, None]
    valid = in_window[None] & ((blk > 0) | (sj[None] >= ATT_BLOCK))
    scores = jnp.where(valid[None, :, None, None], scores, -jnp.inf)
    sink = jnp.broadcast_to(sinks.astype(jnp.float32).reshape(1, 1, N_KV_HEADS, Q_PER_KV, 1, 1),
                            scores.shape[:-1] + (1,))
    probs = jax.nn.softmax(jnp.concatenate([scores, sink], axis=-1), axis=-1)[..., :-1]
    out = jnp.einsum("bnkgqs,bnskd->bnqkgd", probs.astype(vv.dtype), vv)
    return out.reshape(bsz, seq, N_Q_HEADS * HEAD_DIM) @ w_o


def conv_ffn(x, w_up, conv_w, conv_b, w_down):
    h = causal_dwconv(x @ w_up, conv_w, conv_b)
    gate, val = jnp.split(h, 2, axis=-1)
    return (jax.nn.gelu(gate) * val) @ w_down


def _fwd_setup_inputs(seed: int = 0) -> dict:
    key = jax.random.key(seed)
    ks = jax.random.split(key, 24)
    f32 = jnp.float32
    nrm = lambda k, shape, s: jax.random.normal(k, shape, f32) * s
    d_in_ab = 2 * A_WIDTH + 2 * B_WIDTH
    d_qkv = (N_Q_HEADS + 2 * N_KV_HEADS) * HEAD_DIM
    return {
        "x": nrm(ks[0], (BATCH, SEQ, D_MODEL), 1.0),
        "ab_w_in": nrm(ks[1], (N_EVEN, D_MODEL, d_in_ab), D_MODEL ** -0.5),
        "a_conv_w": nrm(ks[2], (N_EVEN, A_KERNEL, A_WIDTH), A_KERNEL ** -0.5),
        "a_conv_b": nrm(ks[3], (N_EVEN, A_WIDTH), 0.02),
        "a_norm_g": 1.0 + nrm(ks[4], (N_EVEN, A_WIDTH), 0.02),
        "a_norm_b": nrm(ks[5], (N_EVEN, A_WIDTH), 0.02),
        "b_norm_g": 1.0 + nrm(ks[6], (N_EVEN, B_WIDTH), 0.02),
        "b_norm_b": nrm(ks[7], (N_EVEN, B_WIDTH), 0.02),
        "b_spatial_w": nrm(ks[8], (N_EVEN, B_GROUPS, B_CHUNK, B_CHUNK), B_CHUNK ** -0.5),
        "b_spatial_b": 1.0 + nrm(ks[9], (N_EVEN, B_GROUPS, B_CHUNK), 0.02),
        "ab_w_out": nrm(ks[10], (N_EVEN, A_WIDTH + B_WIDTH, D_MODEL), (A_WIDTH + B_WIDTH) ** -0.5 * DEEPNORM_BETA),
        "c_w_qkv": nrm(ks[11], (N_ODD, D_MODEL, d_qkv), D_MODEL ** -0.5),
        "c_b_qkv": nrm(ks[12], (N_ODD, d_qkv), 0.02),
        "c_sinks": nrm(ks[13], (N_ODD, N_Q_HEADS), 0.5),
        "c_w_o": nrm(ks[14], (N_ODD, N_Q_HEADS * HEAD_DIM, D_MODEL), (N_Q_HEADS * HEAD_DIM) ** -0.5 * DEEPNORM_BETA),
        "ffn_w_up": nrm(ks[15], (DEPTH, D_MODEL, 2 * D_FF), D_MODEL ** -0.5),
        "ffn_conv_w": nrm(ks[16], (DEPTH, FFN_KERNEL, 2 * D_FF), FFN_KERNEL ** -0.5),
        "ffn_conv_b": nrm(ks[17], (DEPTH, 2 * D_FF), 0.02),
        "ffn_w_down": nrm(ks[18], (DEPTH, D_FF, D_MODEL), D_FF ** -0.5 * DEEPNORM_BETA),
        "ln_g": 1.0 + nrm(ks[19], (DEPTH, 2, D_MODEL), 0.02),
        "ln_b": nrm(ks[20], (DEPTH, 2, D_MODEL), 0.02),
    }


def _fwd_reference(x, ab_w_in, a_conv_w, a_conv_b, a_norm_g, a_norm_b, b_norm_g, b_norm_b,
              b_spatial_w, b_spatial_b, ab_w_out, c_w_qkv, c_b_qkv, c_sinks, c_w_o,
              ffn_w_up, ffn_conv_w, ffn_conv_b, ffn_w_down, ln_g, ln_b):
    alpha = jnp.asarray(DEEPNORM_ALPHA, dtype=x.dtype)
    for i in range(DEPTH):
        j = i // 2
        if i % 2 == 0:
            mix = conv_gmlp_mixer(x, ab_w_in[j], a_conv_w[j], a_conv_b[j], a_norm_g[j], a_norm_b[j],
                                  b_norm_g[j], b_norm_b[j], b_spatial_w[j], b_spatial_b[j], ab_w_out[j])
        else:
            mix = swa_sink_attention(x, c_w_qkv[j], c_b_qkv[j], c_sinks[j], c_w_o[j])
        x = layer_norm(alpha * x + mix, ln_g[i, 0], ln_b[i, 0])
        ffn = conv_ffn(x, ffn_w_up[i], ffn_conv_w[i], ffn_conv_b[i], ffn_w_down[i])
        x = layer_norm(alpha * x + ffn, ln_g[i, 1], ln_b[i, 1])
    return x


import jax as _jax
import jax.numpy as _jnp

TWIN_FORMAT = 'train_step'
FWD_PARAMS = ['x', 'ab_w_in', 'a_conv_w', 'a_conv_b', 'a_norm_g', 'a_norm_b', 'b_norm_g', 'b_norm_b', 'b_spatial_w', 'b_spatial_b', 'ab_w_out', 'c_w_qkv', 'c_b_qkv', 'c_sinks', 'c_w_o', 'ffn_w_up', 'ffn_conv_w', 'ffn_conv_b', 'ffn_w_down', 'ln_g', 'ln_b']
TWIN_WEIGHTS = ['ab_w_in', 'a_conv_w', 'a_conv_b', 'a_norm_g', 'a_norm_b', 'b_norm_g', 'b_norm_b', 'b_spatial_w', 'b_spatial_b', 'ab_w_out', 'c_w_qkv', 'c_b_qkv', 'c_sinks', 'c_w_o', 'ffn_w_up', 'ffn_conv_w', 'ffn_conv_b', 'ffn_w_down', 'ln_g', 'ln_b']
TWIN_DIFF_INPUT = 'x'
TWIN_INPUTS = ['x', 'ab_w_in', 'a_conv_w', 'a_conv_b', 'a_norm_g', 'a_norm_b', 'b_norm_g', 'b_norm_b', 'b_spatial_w', 'b_spatial_b', 'ab_w_out', 'c_w_qkv', 'c_b_qkv', 'c_sinks', 'c_w_o', 'ffn_w_up', 'ffn_conv_w', 'ffn_conv_b', 'ffn_w_down', 'ln_g', 'ln_b', 'loss_target', 'm_ab_w_in', 'm_a_conv_w', 'm_a_conv_b', 'm_a_norm_g', 'm_a_norm_b', 'm_b_norm_g', 'm_b_norm_b', 'm_b_spatial_w', 'm_b_spatial_b', 'm_ab_w_out', 'm_c_w_qkv', 'm_c_b_qkv', 'm_c_sinks', 'm_c_w_o', 'm_ffn_w_up', 'm_ffn_conv_w', 'm_ffn_conv_b', 'm_ffn_w_down', 'm_ln_g', 'm_ln_b', 'v_ab_w_in', 'v_a_conv_w', 'v_a_conv_b', 'v_a_norm_g', 'v_a_norm_b', 'v_b_norm_g', 'v_b_norm_b', 'v_b_spatial_w', 'v_b_spatial_b', 'v_ab_w_out', 'v_c_w_qkv', 'v_c_b_qkv', 'v_c_sinks', 'v_c_w_o', 'v_ffn_w_up', 'v_ffn_conv_w', 'v_ffn_conv_b', 'v_ffn_w_down', 'v_ln_g', 'v_ln_b']
TWIN_OUTPUTS = ['loss', 'grad_x', 'grad_ab_w_in', 'grad_a_conv_w', 'grad_a_conv_b', 'grad_a_norm_g', 'grad_a_norm_b', 'grad_b_norm_g', 'grad_b_norm_b', 'grad_b_spatial_w', 'grad_b_spatial_b', 'grad_ab_w_out', 'grad_c_w_qkv', 'grad_c_b_qkv', 'grad_c_sinks', 'grad_c_w_o', 'grad_ffn_w_up', 'grad_ffn_conv_w', 'grad_ffn_conv_b', 'grad_ffn_w_down', 'grad_ln_g', 'grad_ln_b', 'delta_ab_w_in', 'delta_a_conv_w', 'delta_a_conv_b', 'delta_a_norm_g', 'delta_a_norm_b', 'delta_b_norm_g', 'delta_b_norm_b', 'delta_b_spatial_w', 'delta_b_spatial_b', 'delta_ab_w_out', 'delta_c_w_qkv', 'delta_c_b_qkv', 'delta_c_sinks', 'delta_c_w_o', 'delta_ffn_w_up', 'delta_ffn_conv_w', 'delta_ffn_conv_b', 'delta_ffn_w_down', 'delta_ln_g', 'delta_ln_b', 'new_m_ab_w_in', 'new_m_a_conv_w', 'new_m_a_conv_b', 'new_m_a_norm_g', 'new_m_a_norm_b', 'new_m_b_norm_g', 'new_m_b_norm_b', 'new_m_b_spatial_w', 'new_m_b_spatial_b', 'new_m_ab_w_out', 'new_m_c_w_qkv', 'new_m_c_b_qkv', 'new_m_c_sinks', 'new_m_c_w_o', 'new_m_ffn_w_up', 'new_m_ffn_conv_w', 'new_m_ffn_conv_b', 'new_m_ffn_w_down', 'new_m_ln_g', 'new_m_ln_b', 'new_v_ab_w_in', 'new_v_a_conv_w', 'new_v_a_conv_b', 'new_v_a_norm_g', 'new_v_a_norm_b', 'new_v_b_norm_g', 'new_v_b_norm_b', 'new_v_b_spatial_w', 'new_v_b_spatial_b', 'new_v_ab_w_out', 'new_v_c_w_qkv', 'new_v_c_b_qkv', 'new_v_c_sinks', 'new_v_c_w_o', 'new_v_ffn_w_up', 'new_v_ffn_conv_w', 'new_v_ffn_conv_b', 'new_v_ffn_w_down', 'new_v_ln_g', 'new_v_ln_b']
TWIN_LEAF_KINDS = {'loss': 'loss', 'grad_x': 'grad_x', 'grad_ab_w_in': 'grad_w', 'grad_a_conv_w': 'grad_w', 'grad_a_conv_b': 'grad_w', 'grad_a_norm_g': 'grad_w', 'grad_a_norm_b': 'grad_w', 'grad_b_norm_g': 'grad_w', 'grad_b_norm_b': 'grad_w', 'grad_b_spatial_w': 'grad_w', 'grad_b_spatial_b': 'grad_w', 'grad_ab_w_out': 'grad_w', 'grad_c_w_qkv': 'grad_w', 'grad_c_b_qkv': 'grad_w', 'grad_c_sinks': 'grad_w', 'grad_c_w_o': 'grad_w', 'grad_ffn_w_up': 'grad_w', 'grad_ffn_conv_w': 'grad_w', 'grad_ffn_conv_b': 'grad_w', 'grad_ffn_w_down': 'grad_w', 'grad_ln_g': 'grad_w', 'grad_ln_b': 'grad_w', 'delta_ab_w_in': 'delta_w', 'delta_a_conv_w': 'delta_w', 'delta_a_conv_b': 'delta_w', 'delta_a_norm_g': 'delta_w', 'delta_a_norm_b': 'delta_w', 'delta_b_norm_g': 'delta_w', 'delta_b_norm_b': 'delta_w', 'delta_b_spatial_w': 'delta_w', 'delta_b_spatial_b': 'delta_w', 'delta_ab_w_out': 'delta_w', 'delta_c_w_qkv': 'delta_w', 'delta_c_b_qkv': 'delta_w', 'delta_c_sinks': 'delta_w', 'delta_c_w_o': 'delta_w', 'delta_ffn_w_up': 'delta_w', 'delta_ffn_conv_w': 'delta_w', 'delta_ffn_conv_b': 'delta_w', 'delta_ffn_w_down': 'delta_w', 'delta_ln_g': 'delta_w', 'delta_ln_b': 'delta_w', 'new_m_ab_w_in': 'new_m', 'new_m_a_conv_w': 'new_m', 'new_m_a_conv_b': 'new_m', 'new_m_a_norm_g': 'new_m', 'new_m_a_norm_b': 'new_m', 'new_m_b_norm_g': 'new_m', 'new_m_b_norm_b': 'new_m', 'new_m_b_spatial_w': 'new_m', 'new_m_b_spatial_b': 'new_m', 'new_m_ab_w_out': 'new_m', 'new_m_c_w_qkv': 'new_m', 'new_m_c_b_qkv': 'new_m', 'new_m_c_sinks': 'new_m', 'new_m_c_w_o': 'new_m', 'new_m_ffn_w_up': 'new_m', 'new_m_ffn_conv_w': 'new_m', 'new_m_ffn_conv_b': 'new_m', 'new_m_ffn_w_down': 'new_m', 'new_m_ln_g': 'new_m', 'new_m_ln_b': 'new_m', 'new_v_ab_w_in': 'new_v', 'new_v_a_conv_w': 'new_v', 'new_v_a_conv_b': 'new_v', 'new_v_a_norm_g': 'new_v', 'new_v_a_norm_b': 'new_v', 'new_v_b_norm_g': 'new_v', 'new_v_b_norm_b': 'new_v', 'new_v_b_spatial_w': 'new_v', 'new_v_b_spatial_b': 'new_v', 'new_v_ab_w_out': 'new_v', 'new_v_c_w_qkv': 'new_v', 'new_v_c_b_qkv': 'new_v', 'new_v_c_sinks': 'new_v', 'new_v_c_w_o': 'new_v', 'new_v_ffn_w_up': 'new_v', 'new_v_ffn_conv_w': 'new_v', 'new_v_ffn_conv_b': 'new_v', 'new_v_ffn_w_down': 'new_v', 'new_v_ln_g': 'new_v', 'new_v_ln_b': 'new_v'}


def _forward(args):
    return _fwd_reference(*[args[k] for k in FWD_PARAMS])


def _output_shape():
    def fwd():
        inp = _fwd_setup_inputs(0)
        return _fwd_reference(*[inp[k] for k in FWD_PARAMS])
    out = _jax.eval_shape(fwd)
    return out.shape, out.dtype

N_MICROBATCH = 1
ADAM_LR = 0.001
ADAM_B1 = 0.9
ADAM_B2 = 0.999
ADAM_EPS = 1e-08
ADAM_WD = 0.01
ADAM_STEP = 10
PER_EXAMPLE_BATCH_AXIS = {'x': 0, 'loss_target': 0}
SHARED_INPUTS = []
_WEIGHT_DTYPES = {'ab_w_in': _jnp.float32, 'a_conv_w': _jnp.float32, 'a_conv_b': _jnp.float32, 'a_norm_g': _jnp.float32, 'a_norm_b': _jnp.float32, 'b_norm_g': _jnp.float32, 'b_norm_b': _jnp.float32, 'b_spatial_w': _jnp.float32, 'b_spatial_b': _jnp.float32, 'ab_w_out': _jnp.float32, 'c_w_qkv': _jnp.float32, 'c_b_qkv': _jnp.float32, 'c_sinks': _jnp.float32, 'c_w_o': _jnp.float32, 'ffn_w_up': _jnp.float32, 'ffn_conv_w': _jnp.float32, 'ffn_conv_b': _jnp.float32, 'ffn_w_down': _jnp.float32, 'ln_g': _jnp.float32, 'ln_b': _jnp.float32}
MOMENT_SCALE = {'ab_w_in': 3.816514e-02, 'a_conv_w': 4.118570e-02, 'a_conv_b': 1.875567e-01, 'a_norm_g': 8.021900e-02, 'a_norm_b': 1.088048e-01, 'b_norm_g': 2.858459e-02, 'b_norm_b': 3.013121e-02, 'b_spatial_w': 2.971108e-02, 'b_spatial_b': 4.067466e-02, 'ab_w_out': 1.229794e-01, 'c_w_qkv': 1.985170e-02, 'c_b_qkv': 1.717866e-01, 'c_sinks': 9.327780e-03, 'c_w_o': 3.134693e-02, 'ffn_w_up': 2.495075e-02, 'ffn_conv_w': 2.518125e-02, 'ffn_conv_b': 3.246270e-02, 'ffn_w_down': 8.167523e-02, 'ln_g': 1.607701e+01, 'ln_b': 1.544013e+00}


def _to_microbatches(a, axis):
    t = _jnp.moveaxis(a, axis, 0)
    t = t.reshape((N_MICROBATCH, t.shape[0] // N_MICROBATCH) + t.shape[1:])
    return _jnp.moveaxis(t, 1, axis + 1)


def setup_inputs(seed: int = 0) -> dict:
    inp = _fwd_setup_inputs(seed)
    key = _jax.random.fold_in(_jax.random.key(seed), 7919)
    shape, _ = _output_shape()
    out = dict(inp)
    out["loss_target"] = _jax.random.normal(_jax.random.fold_in(key, 0), shape, _jnp.float32)
    for i, name in enumerate(TWIN_WEIGHTS):
        w = inp[name].astype(_jnp.float32)
        if MOMENT_SCALE is None:
            s = _jnp.sqrt(_jnp.mean(_jnp.square(w)) + 1e-30)
        else:
            s = MOMENT_SCALE[name]
        km, kv = _jax.random.split(_jax.random.fold_in(key, i + 1))
        out[name] = w
        out["m_" + name] = s * _jax.random.normal(km, w.shape, _jnp.float32)
        out["v_" + name] = (s * s) * _jax.random.uniform(kv, w.shape, _jnp.float32, 0.5, 1.5)
    if N_MICROBATCH > 1:
        for name, axis in PER_EXAMPLE_BATCH_AXIS.items():
            out[name] = _to_microbatches(out[name], axis)
    return {'x': out['x'], 'ab_w_in': out['ab_w_in'], 'a_conv_w': out['a_conv_w'], 'a_conv_b': out['a_conv_b'], 'a_norm_g': out['a_norm_g'], 'a_norm_b': out['a_norm_b'], 'b_norm_g': out['b_norm_g'], 'b_norm_b': out['b_norm_b'], 'b_spatial_w': out['b_spatial_w'], 'b_spatial_b': out['b_spatial_b'], 'ab_w_out': out['ab_w_out'], 'c_w_qkv': out['c_w_qkv'], 'c_b_qkv': out['c_b_qkv'], 'c_sinks': out['c_sinks'], 'c_w_o': out['c_w_o'], 'ffn_w_up': out['ffn_w_up'], 'ffn_conv_w': out['ffn_conv_w'], 'ffn_conv_b': out['ffn_conv_b'], 'ffn_w_down': out['ffn_w_down'], 'ln_g': out['ln_g'], 'ln_b': out['ln_b'], 'loss_target': out['loss_target'], 'm_ab_w_in': out['m_ab_w_in'], 'm_a_conv_w': out['m_a_conv_w'], 'm_a_conv_b': out['m_a_conv_b'], 'm_a_norm_g': out['m_a_norm_g'], 'm_a_norm_b': out['m_a_norm_b'], 'm_b_norm_g': out['m_b_norm_g'], 'm_b_norm_b': out['m_b_norm_b'], 'm_b_spatial_w': out['m_b_spatial_w'], 'm_b_spatial_b': out['m_b_spatial_b'], 'm_ab_w_out': out['m_ab_w_out'], 'm_c_w_qkv': out['m_c_w_qkv'], 'm_c_b_qkv': out['m_c_b_qkv'], 'm_c_sinks': out['m_c_sinks'], 'm_c_w_o': out['m_c_w_o'], 'm_ffn_w_up': out['m_ffn_w_up'], 'm_ffn_conv_w': out['m_ffn_conv_w'], 'm_ffn_conv_b': out['m_ffn_conv_b'], 'm_ffn_w_down': out['m_ffn_w_down'], 'm_ln_g': out['m_ln_g'], 'm_ln_b': out['m_ln_b'], 'v_ab_w_in': out['v_ab_w_in'], 'v_a_conv_w': out['v_a_conv_w'], 'v_a_conv_b': out['v_a_conv_b'], 'v_a_norm_g': out['v_a_norm_g'], 'v_a_norm_b': out['v_a_norm_b'], 'v_b_norm_g': out['v_b_norm_g'], 'v_b_norm_b': out['v_b_norm_b'], 'v_b_spatial_w': out['v_b_spatial_w'], 'v_b_spatial_b': out['v_b_spatial_b'], 'v_ab_w_out': out['v_ab_w_out'], 'v_c_w_qkv': out['v_c_w_qkv'], 'v_c_b_qkv': out['v_c_b_qkv'], 'v_c_sinks': out['v_c_sinks'], 'v_c_w_o': out['v_c_w_o'], 'v_ffn_w_up': out['v_ffn_w_up'], 'v_ffn_conv_w': out['v_ffn_conv_w'], 'v_ffn_conv_b': out['v_ffn_conv_b'], 'v_ffn_w_down': out['v_ffn_w_down'], 'v_ln_g': out['v_ln_g'], 'v_ln_b': out['v_ln_b']}


def _loss(weights, diff, rest, loss_target):
    with _jax.named_scope("forward"):
        args = {**rest, TWIN_DIFF_INPUT: diff, **{k: w.astype(_WEIGHT_DTYPES[k]) for k, w in weights.items()}}
        y = _forward(args)
    with _jax.named_scope("loss_head"):
        err = _jnp.square(y.astype(_jnp.float32) - loss_target)
        return 0.5 * _jnp.sum(_jnp.mean(err, axis=-1)) if err.ndim else 0.5 * err


def _adamw(w, g, m, v):
    m = ADAM_B1 * m + (1.0 - ADAM_B1) * g
    v = ADAM_B2 * v + (1.0 - ADAM_B2) * _jnp.square(g)
    m_hat = m / (1.0 - ADAM_B1 ** ADAM_STEP)
    v_hat = v / (1.0 - ADAM_B2 ** ADAM_STEP)
    delta = -ADAM_LR * (m_hat / (_jnp.sqrt(v_hat) + ADAM_EPS) + ADAM_WD * w)
    return delta, m, v


def reference(x, ab_w_in, a_conv_w, a_conv_b, a_norm_g, a_norm_b, b_norm_g, b_norm_b, b_spatial_w, b_spatial_b, ab_w_out, c_w_qkv, c_b_qkv, c_sinks, c_w_o, ffn_w_up, ffn_conv_w, ffn_conv_b, ffn_w_down, ln_g, ln_b, loss_target, m_ab_w_in, m_a_conv_w, m_a_conv_b, m_a_norm_g, m_a_norm_b, m_b_norm_g, m_b_norm_b, m_b_spatial_w, m_b_spatial_b, m_ab_w_out, m_c_w_qkv, m_c_b_qkv, m_c_sinks, m_c_w_o, m_ffn_w_up, m_ffn_conv_w, m_ffn_conv_b, m_ffn_w_down, m_ln_g, m_ln_b, v_ab_w_in, v_a_conv_w, v_a_conv_b, v_a_norm_g, v_a_norm_b, v_b_norm_g, v_b_norm_b, v_b_spatial_w, v_b_spatial_b, v_ab_w_out, v_c_w_qkv, v_c_b_qkv, v_c_sinks, v_c_w_o, v_ffn_w_up, v_ffn_conv_w, v_ffn_conv_b, v_ffn_w_down, v_ln_g, v_ln_b):
    given = dict(x=x, ab_w_in=ab_w_in, a_conv_w=a_conv_w, a_conv_b=a_conv_b, a_norm_g=a_norm_g, a_norm_b=a_norm_b, b_norm_g=b_norm_g, b_norm_b=b_norm_b, b_spatial_w=b_spatial_w, b_spatial_b=b_spatial_b, ab_w_out=ab_w_out, c_w_qkv=c_w_qkv, c_b_qkv=c_b_qkv, c_sinks=c_sinks, c_w_o=c_w_o, ffn_w_up=ffn_w_up, ffn_conv_w=ffn_conv_w, ffn_conv_b=ffn_conv_b, ffn_w_down=ffn_w_down, ln_g=ln_g, ln_b=ln_b, loss_target=loss_target, m_ab_w_in=m_ab_w_in, m_a_conv_w=m_a_conv_w, m_a_conv_b=m_a_conv_b, m_a_norm_g=m_a_norm_g, m_a_norm_b=m_a_norm_b, m_b_norm_g=m_b_norm_g, m_b_norm_b=m_b_norm_b, m_b_spatial_w=m_b_spatial_w, m_b_spatial_b=m_b_spatial_b, m_ab_w_out=m_ab_w_out, m_c_w_qkv=m_c_w_qkv, m_c_b_qkv=m_c_b_qkv, m_c_sinks=m_c_sinks, m_c_w_o=m_c_w_o, m_ffn_w_up=m_ffn_w_up, m_ffn_conv_w=m_ffn_conv_w, m_ffn_conv_b=m_ffn_conv_b, m_ffn_w_down=m_ffn_w_down, m_ln_g=m_ln_g, m_ln_b=m_ln_b, v_ab_w_in=v_ab_w_in, v_a_conv_w=v_a_conv_w, v_a_conv_b=v_a_conv_b, v_a_norm_g=v_a_norm_g, v_a_norm_b=v_a_norm_b, v_b_norm_g=v_b_norm_g, v_b_norm_b=v_b_norm_b, v_b_spatial_w=v_b_spatial_w, v_b_spatial_b=v_b_spatial_b, v_ab_w_out=v_ab_w_out, v_c_w_qkv=v_c_w_qkv, v_c_b_qkv=v_c_b_qkv, v_c_sinks=v_c_sinks, v_c_w_o=v_c_w_o, v_ffn_w_up=v_ffn_w_up, v_ffn_conv_w=v_ffn_conv_w, v_ffn_conv_b=v_ffn_conv_b, v_ffn_w_down=v_ffn_w_down, v_ln_g=v_ln_g, v_ln_b=v_ln_b)
    weights = {n: given[n] for n in TWIN_WEIGHTS}
    shared = {n: given[n] for n in SHARED_INPUTS}
    per_example = {n: given[n] for n in ['x']}
    grad_fn = _jax.value_and_grad(_loss, argnums=(0, 1))

    def one_microbatch(ex, loss_target):
        ex = dict(ex)
        diff = ex.pop(TWIN_DIFF_INPUT)
        return grad_fn(weights, diff, {**shared, **ex}, loss_target)

    if N_MICROBATCH == 1:
        loss, (grad_w, grad_x) = one_microbatch(per_example, given["loss_target"])
    else:
        def body(carry, xs):
            loss_sum, grad_sum = carry
            l_k, (gw_k, gx_k) = one_microbatch(xs[0], xs[1])
            with _jax.named_scope("update"):
                return (loss_sum + l_k, _jax.tree.map(_jnp.add, grad_sum, gw_k)), gx_k

        init = (_jnp.zeros((), _jnp.float32), _jax.tree.map(_jnp.zeros_like, weights))
        (loss, grad_w), grad_x = _jax.lax.scan(body, init, (per_example, given["loss_target"]))
    with _jax.named_scope("update"):
        delta_w, new_m, new_v = {}, {}, {}
        for n in TWIN_WEIGHTS:
            delta_w[n], new_m[n], new_v[n] = _adamw(weights[n], grad_w[n], given["m_" + n], given["v_" + n])
    return (loss, grad_x, *[grad_w[n] for n in TWIN_WEIGHTS], *[delta_w[n] for n in TWIN_WEIGHTS],
            *[new_m[n] for n in TWIN_WEIGHTS], *[new_v[n] for n in TWIN_WEIGHTS])
```

```python
import functools
import math

import jax
import jax.numpy as jnp
from jax import lax
from jax.experimental import pallas as pl
from jax.experimental.pallas import tpu as pltpu

F32 = jnp.float32
BF16 = jnp.bfloat16
MESH = pl.DeviceIdType.MESH

LN_EPS = 1e-5
HEAD_DIM = 64
ATT_BLOCK = 128
Q_PER_KV = 8
A_KERNEL = 31
CONV_HALO = 32
FFN_HALO = 8
B_CHUNK = 128
LANES = 128
GELU_C = math.sqrt(2.0 / math.pi)
ADAM_LR = 0.001
ADAM_B1 = 0.9
ADAM_B2 = 0.999
ADAM_EPS = 1e-08
ADAM_WD = 0.01
ADAM_STEP = 10
VMEM_LIMIT = 56 * 1024 * 1024


def _cp(*dims):
    return pltpu.CompilerParams(dimension_semantics=dims, vmem_limit_bytes=VMEM_LIMIT)


def _pick(n, prefs):
    for p in prefs:
        if n % p == 0:
            return p
    return n


def _sig(x):
    return 1.0 / (1.0 + jnp.exp(-x))


def _gelu(x):
    t = jnp.tanh(GELU_C * (x + 0.044715 * (x * x * x)))
    return x * (0.5 * (1.0 + t)), t


def _gelu_grad(x, t):
    return 0.5 * (1.0 + t) + 0.5 * x * (1.0 - t * t) * (GELU_C * (1.0 + 3.0 * 0.044715 * x * x))


def _ln_stats(z):
    mu = jnp.mean(z, axis=-1, keepdims=True)
    zc = z - mu
    var = jnp.mean(zc * zc, axis=-1, keepdims=True)
    rstd = lax.rsqrt(var + LN_EPS)
    return zc * rstd, rstd


def _ln_bwd(dxh, xh, rstd):
    return rstd * (dxh - jnp.mean(dxh, axis=-1, keepdims=True) - xh * jnp.mean(dxh * xh, axis=-1, keepdims=True))


def _rowsum(a):
    return jnp.sum(a, axis=0, keepdims=True)


def _lshape(a):
    return (a.shape[0], a.shape[1]) if a.ndim == 2 else (a.shape[1], a.shape[0] * a.shape[2])


def _spec2(arr, blk_r, blk_c, ridx, cidx):
    if len(arr.shape) == 2:
        return pl.BlockSpec((blk_r, blk_c), lambda i, j, k: (ridx(i, j, k), cidx(i, j, k)))
    per = arr.shape[2] // blk_c
    assert arr.shape[2] % blk_c == 0
    return pl.BlockSpec((None, blk_r, blk_c), lambda i, j, k: (cidx(i, j, k) // per, ridx(i, j, k), cidx(i, j, k) % per))


def _matmul(a, b, *, name, ta=False, tb=False, tm, tn, tk, out_dtype=F32, out_stack=None, bias=None, addend=None):
    ar, ac = _lshape(a)
    br, bc = _lshape(b)
    m, kdim = (ac, ar) if ta else (ar, ac)
    n = br if tb else bc
    assert (bc if tb else br) == kdim
    tm, tn, tk = min(tm, m), min(tn, n), min(tk, kdim)
    assert m % tm == 0 and n % tn == 0 and kdim % tk == 0, (name, m, n, kdim, tm, tn, tk)
    nk = kdim // tk
    gi, gj, gk = (lambda i, j, k: i), (lambda i, j, k: j), (lambda i, j, k: k)
    a_spec = _spec2(a, tk, tm, gk, gi) if ta else _spec2(a, tm, tk, gi, gk)
    b_spec = _spec2(b, tn, tk, gj, gk) if tb else _spec2(b, tk, tn, gk, gj)
    if out_stack is None:
        out_sds = jax.ShapeDtypeStruct((m, n), out_dtype)
    else:
        out_sds = jax.ShapeDtypeStruct((out_stack, m, n // out_stack), out_dtype)
    o_spec = _spec2(out_sds, tm, tn, gi, gj)
    in_specs = [a_spec, b_spec]
    args = [a, b]
    if bias is not None:
        in_specs.append(pl.BlockSpec((1, tn), lambda i, j, k: (0, j)))
        args.append(bias)
    scale = None
    if addend is not None:
        add_arr, scale = addend
        in_specs.append(pl.BlockSpec((tm, tn), lambda i, j, k: (i, j)))
        args.append(add_arr)
    use_acc = nk > 1 and out_dtype != F32
    dn = (((0 if ta else 1,), (1 if tb else 0,)), ((), ()))

    def body(*refs):
        a_ref, b_ref = refs[0], refs[1]
        pos = 2
        bias_ref = add_ref = None
        if bias is not None:
            bias_ref = refs[pos]
            pos += 1
        if addend is not None:
            add_ref = refs[pos]
            pos += 1
        o_ref = refs[pos]
        acc_ref = refs[pos + 1] if use_acc else o_ref
        p = lax.dot_general(a_ref[...].astype(BF16), b_ref[...].astype(BF16), dn, preferred_element_type=F32)

        def finish(val):
            if bias_ref is not None:
                val = val + bias_ref[...]
            if add_ref is not None:
                val = val + scale * add_ref[...]
            return val.astype(out_dtype)

        if nk == 1:
            o_ref[...] = finish(p)
        else:
            k = pl.program_id(2)

            @pl.when(k == 0)
            def _():
                acc_ref[...] = p

            @pl.when(k > 0)
            def _():
                acc_ref[...] += p

            if use_acc or bias_ref is not None or add_ref is not None:
                @pl.when(k == nk - 1)
                def _():
                    o_ref[...] = finish(acc_ref[...])

    return pl.pallas_call(
        body, name=name, grid=(m // tm, n // tn, nk), in_specs=in_specs, out_specs=o_spec, out_shape=out_sds,
        scratch_shapes=[pltpu.VMEM((tm, tn), F32)] if use_acc else [],
        compiler_params=_cp("parallel", "parallel", "arbitrary"),
    )(*args)


def _add_ln_fwd(x, s, g, b, alpha, *, name):
    rows, d = x.shape
    t = _pick(rows, (512, 256))

    def body(x_ref, s_ref, g_ref, b_ref, y_ref):
        xh, _ = _ln_stats(alpha * x_ref[...] + s_ref[...])
        y_ref[...] = xh * g_ref[...] + b_ref[...]

    row = pl.BlockSpec((t, d), lambda i: (i, 0))
    vec = pl.BlockSpec((1, d), lambda i: (0, 0))
    return pl.pallas_call(body, name=name, grid=(rows // t,), in_specs=[row, row, vec, vec], out_specs=row,
                          out_shape=jax.ShapeDtypeStruct((rows, d), F32), compiler_params=_cp("parallel"))(x, s, g, b)


def _add_ln_bwd(dy, x, s, g, alpha, *, name):
    rows, d = x.shape
    t = _pick(rows, (512, 256))

    def body(dy_ref, x_ref, s_ref, g_ref, dz_ref, dg_ref, db_ref):
        @pl.when(pl.program_id(0) == 0)
        def _():
            dg_ref[...] = jnp.zeros_like(dg_ref)
            db_ref[...] = jnp.zeros_like(db_ref)

        xh, rstd = _ln_stats(alpha * x_ref[...] + s_ref[...])
        dyv = dy_ref[...]
        dz_ref[...] = _ln_bwd(dyv * g_ref[...], xh, rstd)
        dg_ref[...] += _rowsum(dyv * xh)
        db_ref[...] += _rowsum(dyv)

    row = pl.BlockSpec((t, d), lambda i: (i, 0))
    vec = pl.BlockSpec((1, d), lambda i: (0, 0))
    vsds = jax.ShapeDtypeStruct((1, d), F32)
    return pl.pallas_call(body, name=name, grid=(rows // t,), in_specs=[row, row, row, vec], out_specs=[row, vec, vec],
                          out_shape=[jax.ShapeDtypeStruct((rows, d), F32), vsds, vsds],
                          compiler_params=_cp("arbitrary"))(dy, x, s, g)


def _loss_and_grad(y, tgt, *, name):
    rows, d = y.shape
    t = _pick(rows, (512, 256))

    def body(y_ref, t_ref, l_ref, dy_ref):
        @pl.when(pl.program_id(0) == 0)
        def _():
            l_ref[...] = jnp.zeros_like(l_ref)

        e = y_ref[...] - t_ref[...]
        l_ref[...] += _rowsum(e * e)
        dy_ref[...] = e * (1.0 / d)

    row = pl.BlockSpec((t, d), lambda i: (i, 0))
    vec = pl.BlockSpec((1, d), lambda i: (0, 0))
    return pl.pallas_call(body, name=name, grid=(rows // t,), in_specs=[row, row], out_specs=[vec, row],
                          out_shape=[jax.ShapeDtypeStruct((1, d), F32), jax.ShapeDtypeStruct((rows, d), F32)],
                          compiler_params=_cp("arbitrary"))(y, tgt)


def _ffn_tiles(rows, f):
    return _pick(rows, (512, 256)), _pick(f, (256, 128))


def _ffn_mid_fwd(hf, cw, cb, *, name):
    _, rows, f = hf.shape
    t, tc = _ffn_tiles(rows, f)
    hb = t // FFN_HALO

    def body(h_ref, hp_ref, cw_ref, cb_ref, o_ref):
        first = pl.program_id(0) == 0
        hc = []
        for s in range(2):
            e = jnp.concatenate([jnp.where(first, 0.0, hp_ref[s]), h_ref[s]], axis=0)
            w = cw_ref[s]
            y = w[0:1, :] * pltpu.roll(e, 2, 0) + w[1:2, :] * pltpu.roll(e, 1, 0) + w[2:3, :] * e + cb_ref[s]
            hc.append(y[FFN_HALO:])
        gl, _ = _gelu(hc[0])
        o_ref[...] = (gl * hc[1]).astype(BF16)

    in_specs = [
        pl.BlockSpec((2, t, tc), lambda i, j: (0, i, j)),
        pl.BlockSpec((2, FFN_HALO, tc), lambda i, j: (0, jnp.maximum(i * hb - 1, 0), j)),
        pl.BlockSpec((2, 3, tc), lambda i, j: (0, 0, j)),
        pl.BlockSpec((2, 1, tc), lambda i, j: (0, 0, j)),
    ]
    return pl.pallas_call(body, name=name, grid=(rows // t, f // tc), in_specs=in_specs,
                          out_specs=pl.BlockSpec((t, tc), lambda i, j: (i, j)),
                          out_shape=jax.ShapeDtypeStruct((rows, f), BF16),
                          compiler_params=_cp("parallel", "parallel"))(hf, hf, cw, cb)


def _ffn_mid_bwd(hf, df, cw, cb, *, name):
    _, rows, f = hf.shape
    t, tc = _ffn_tiles(rows, f)
    hb = t // FFN_HALO
    ni = rows // t
    last_blk = rows // FFN_HALO - 1
    ext = t + 2 * FFN_HALO
    tile = slice(FFN_HALO, FFN_HALO + t)

    def body(h_ref, hp_ref, hn_ref, d_ref, dn_ref, cw_ref, cb_ref, dh_ref, dcw_ref, dcb_ref):
        i = pl.program_id(1)
        first = i == 0
        last = i == ni - 1

        @pl.when(first)
        def _():
            dcw_ref[...] = jnp.zeros_like(dcw_ref)
            dcb_ref[...] = jnp.zeros_like(dcb_ref)

        de = jnp.concatenate([jnp.zeros((FFN_HALO, tc), F32), d_ref[...], jnp.where(last, 0.0, dn_ref[...])], axis=0)
        taps, hc = [], []
        for s in range(2):
            e = jnp.concatenate([jnp.where(first, 0.0, hp_ref[s]), h_ref[s], hn_ref[s]], axis=0)
            r1 = pltpu.roll(e, 1, 0)
            r2 = pltpu.roll(e, 2, 0)
            w = cw_ref[s]
            hc.append(w[0:1, :] * r2 + w[1:2, :] * r1 + w[2:3, :] * e + cb_ref[s])
            taps.append((r2, r1, e))
        gl, th = _gelu(hc[0])
        dhc = (de * hc[1] * _gelu_grad(hc[0], th), de * gl)
        for s in range(2):
            w = cw_ref[s]
            g = dhc[s]
            dh = w[2:3, :] * g + w[1:2, :] * pltpu.roll(g, ext - 1, 0) + w[0:1, :] * pltpu.roll(g, ext - 2, 0)
            dh_ref[s] = dh[tile].astype(BF16)
            gt = g[tile]
            for k in range(3):
                dcw_ref[s, k:k + 1, :] += _rowsum(gt * taps[s][k][tile])
            dcb_ref[s] += _rowsum(gt)

    in_specs = [
        pl.BlockSpec((2, t, tc), lambda j, i: (0, i, j)),
        pl.BlockSpec((2, FFN_HALO, tc), lambda j, i: (0, jnp.maximum(i * hb - 1, 0), j)),
        pl.BlockSpec((2, FFN_HALO, tc), lambda j, i: (0, jnp.minimum((i + 1) * hb, last_blk), j)),
        pl.BlockSpec((t, tc), lambda j, i: (i, j)),
        pl.BlockSpec((FFN_HALO, tc), lambda j, i: (jnp.minimum((i + 1) * hb, last_blk), j)),
        pl.BlockSpec((2, 3, tc), lambda j, i: (0, 0, j)),
        pl.BlockSpec((2, 1, tc), lambda j, i: (0, 0, j)),
    ]
    out_specs = [
        pl.BlockSpec((2, t, tc), lambda j, i: (0, i, j)),
        pl.BlockSpec((2, 3, tc), lambda j, i: (0, 0, j)),
        pl.BlockSpec((2, 1, tc), lambda j, i: (0, 0, j)),
    ]
    out_shape = [jax.ShapeDtypeStruct((2, rows, f), BF16), jax.ShapeDtypeStruct((2, 3, f), F32),
                 jax.ShapeDtypeStruct((2, 1, f), F32)]
    return pl.pallas_call(body, name=name, grid=(f // tc, ni), in_specs=in_specs, out_specs=out_specs,
                          out_shape=out_shape, compiler_params=_cp("parallel", "arbitrary"))(hf, hf, hf, df, df, cw, cb)


def _mixer_fwd(h0, cw, cb, ga, ba, gb, bb, ws, sbb, *, name):
    _, rows, w = h0.shape
    t = _pick(rows, (256,))
    hb = t // CONV_HALO
    groups = w // B_CHUNK

    def body(h_ref, hp_ref, cw_ref, cb_ref, ga_ref, ba_ref, gb_ref, bb_ref, ws_ref, sb_ref, o_ref):
        first = pl.program_id(0) == 0
        a1 = h_ref[0] * _sig(h_ref[1])
        a1p = jnp.where(first, 0.0, hp_ref[0] * _sig(hp_ref[1]))
        e = jnp.concatenate([a1p, a1], axis=0)
        acc = cw_ref[A_KERNEL - 1:A_KERNEL, :] * e
        for k in range(A_KERNEL - 1):
            acc = acc + cw_ref[k:k + 1, :] * pltpu.roll(e, A_KERNEL - 1 - k, 0)
        xh, _ = _ln_stats(acc[CONV_HALO:] + cb_ref[...])
        a3 = xh * ga_ref[...] + ba_ref[...]
        o_ref[:, 0:w] = (a3 * _sig(a3)).astype(BF16)

        u, _ = _gelu(h_ref[2])
        v1, _ = _gelu(h_ref[3])
        xh2, _ = _ln_stats(v1)
        v2 = (xh2 * gb_ref[...] + bb_ref[...]).astype(BF16)
        for c in range(t // B_CHUNK):
            rs = slice(c * B_CHUNK, (c + 1) * B_CHUNK)
            for g in range(groups):
                cs = slice(g * B_CHUNK, (g + 1) * B_CHUNK)
                mixed = jnp.dot(ws_ref[g], v2[rs, cs], preferred_element_type=F32) + sb_ref[g]
                o_ref[rs, w + g * B_CHUNK:w + (g + 1) * B_CHUNK] = (u[rs, cs] * mixed).astype(BF16)

    vec = pl.BlockSpec((1, w), lambda i: (0, 0))
    grp = pl.BlockSpec((groups, B_CHUNK, B_CHUNK), lambda i: (0, 0, 0))
    in_specs = [
        pl.BlockSpec((4, t, w), lambda i: (0, i, 0)),
        pl.BlockSpec((2, CONV_HALO, w), lambda i: (0, jnp.maximum(i * hb - 1, 0), 0)),
        pl.BlockSpec((A_KERNEL, w), lambda i: (0, 0)),
        vec, vec, vec, vec, vec, grp, grp,
    ]
    return pl.pallas_call(body, name=name, grid=(rows // t,), in_specs=in_specs,
                          out_specs=pl.BlockSpec((t, 2 * w), lambda i: (i, 0)),
                          out_shape=jax.ShapeDtypeStruct((rows, 2 * w), BF16),
                          compiler_params=_cp("parallel"))(h0, h0, cw, cb, ga, ba, gb, bb, ws, sbb)


def _mixer_bwd(h0, dab, cw, cb, ga, ba, gb, bb, ws, wst, sbb, tril, *, name):
    _, rows, w = h0.shape
    t = _pick(rows, (256,))
    hb = t // CONV_HALO
    ni = rows // t
    last_blk = rows // CONV_HALO - 1
    ext = t + 2 * CONV_HALO
    tile = slice(CONV_HALO, CONV_HALO + t)
    groups = w // B_CHUNK
    taps = A_KERNEL - 1

    def body(h_ref, hp_ref, hn_ref, d_ref, dn_ref, cw_ref, cb_ref, ga_ref, ba_ref, gb_ref, bb_ref, ws_ref, wst_ref,
             sb_ref, tril_ref, dh_ref, dcw_ref, dcb_ref, dga_ref, dba_ref, dgb_ref, dbb_ref, dws_ref, dsb_ref):
        i = pl.program_id(0)
        first = i == 0
        last = i == ni - 1

        @pl.when(first)
        def _():
            for r in (dcw_ref, dcb_ref, dga_ref, dba_ref, dgb_ref, dbb_ref, dws_ref, dsb_ref):
                r[...] = jnp.zeros_like(r)

        av_e = jnp.concatenate([hp_ref[0], h_ref[0], hn_ref[0]], axis=0)
        sg_e = _sig(jnp.concatenate([hp_ref[1], h_ref[1], hn_ref[1]], axis=0))
        rows_e = lax.broadcasted_iota(jnp.int32, (ext, 1), 0)
        a1_e = jnp.where(first & (rows_e < CONV_HALO), 0.0, av_e * sg_e)
        acc = cw_ref[taps:taps + 1, :] * a1_e
        for k in range(taps):
            acc = acc + cw_ref[k:k + 1, :] * pltpu.roll(a1_e, taps - k, 0)
        xh, rstd = _ln_stats(acc + cb_ref[...])
        a3 = xh * ga_ref[...] + ba_ref[...]
        s3 = _sig(a3)
        da_e = jnp.concatenate([jnp.zeros((CONV_HALO, w), F32), d_ref[:, 0:w], jnp.where(last, 0.0, dn_ref[...])], axis=0)
        da3 = da_e * (s3 * (1.0 + a3 * (1.0 - s3)))
        da2 = _ln_bwd(da3 * ga_ref[...], xh, rstd)
        dga_ref[...] += _rowsum(da3[tile] * xh[tile])
        dba_ref[...] += _rowsum(da3[tile])
        da2t = da2[tile]
        dcb_ref[...] += _rowsum(da2t)
        dcw_ref[taps:taps + 1, :] += _rowsum(da2t * a1_e[tile])
        da1 = cw_ref[taps:taps + 1, :] * da2
        for k in range(taps):
            sh = taps - k
            dcw_ref[k:k + 1, :] += _rowsum(da2t * pltpu.roll(a1_e, sh, 0)[tile])
            da1 = da1 + cw_ref[k:k + 1, :] * pltpu.roll(da2, ext - sh, 0)
        da1t = da1[tile]
        sgt = sg_e[tile]
        dh_ref[0] = (da1t * sgt).astype(BF16)
        dh_ref[1] = (da1t * h_ref[0] * sgt * (1.0 - sgt)).astype(BF16)

        bu = h_ref[2]
        bv = h_ref[3]
        u, tu = _gelu(bu)
        v1, tv = _gelu(bv)
        xh2, rstd2 = _ln_stats(v1)
        v2 = (xh2 * gb_ref[...] + bb_ref[...]).astype(BF16)
        db = d_ref[:, w:2 * w]
        dmx_all = db * u
        du_parts, dv2_parts = [], []
        for c in range(t // B_CHUNK):
            rs = slice(c * B_CHUNK, (c + 1) * B_CHUNK)
            du_row, dv2_row = [], []
            for g in range(groups):
                cs = slice(g * B_CHUNK, (g + 1) * B_CHUNK)
                v2cg = v2[rs, cs]
                mixed = jnp.dot(ws_ref[g], v2cg, preferred_element_type=F32) + sb_ref[g]
                dmx = dmx_all[rs, cs]
                dmxb = dmx.astype(BF16)
                du_row.append(db[rs, cs] * mixed)
                dv2_row.append(jnp.dot(wst_ref[g], dmxb, preferred_element_type=F32))
                dws_ref[g] += tril_ref[...] * lax.dot_general(dmxb, v2cg, (((1,), (1,)), ((), ())),
                                                               preferred_element_type=F32)
                dsb_ref[g] += jnp.sum(dmx, axis=1, keepdims=True)
            du_parts.append(jnp.concatenate(du_row, axis=1))
            dv2_parts.append(jnp.concatenate(dv2_row, axis=1))
        du = jnp.concatenate(du_parts, axis=0)
        dv2 = jnp.concatenate(dv2_parts, axis=0)
        dgb_ref[...] += _rowsum(dv2 * xh2)
        dbb_ref[...] += _rowsum(dv2)
        dv1 = _ln_bwd(dv2 * gb_ref[...], xh2, rstd2)
        dh_ref[2] = (du * _gelu_grad(bu, tu)).astype(BF16)
        dh_ref[3] = (dv1 * _gelu_grad(bv, tv)).astype(BF16)

    vec = pl.BlockSpec((1, w), lambda i: (0, 0))
    grp = pl.BlockSpec((groups, B_CHUNK, B_CHUNK), lambda i: (0, 0, 0))
    in_specs = [
        pl.BlockSpec((4, t, w), lambda i: (0, i, 0)),
        pl.BlockSpec((2, CONV_HALO, w), lambda i: (0, jnp.maximum(i * hb - 1, 0), 0)),
        pl.BlockSpec((2, CONV_HALO, w), lambda i: (0, jnp.minimum((i + 1) * hb, last_blk), 0)),
        pl.BlockSpec((t, 2 * w), lambda i: (i, 0)),
        pl.BlockSpec((CONV_HALO, w), lambda i: (jnp.minimum((i + 1) * hb, last_blk), 0)),
        pl.BlockSpec((A_KERNEL, w), lambda i: (0, 0)),
        vec, vec, vec, vec, vec, grp, grp, grp,
        pl.BlockSpec((B_CHUNK, B_CHUNK), lambda i: (0, 0)),
    ]
    vsds = jax.ShapeDtypeStruct((1, w), F32)
    out_specs = [
        pl.BlockSpec((4, t, w), lambda i: (0, i, 0)),
        pl.BlockSpec((A_KERNEL, w), lambda i: (0, 0)),
        vec, vec, vec, vec, vec, grp,
        pl.BlockSpec((groups, B_CHUNK, 1), lambda i: (0, 0, 0)),
    ]
    out_shape = [jax.ShapeDtypeStruct((4, rows, w), BF16), jax.ShapeDtypeStruct((A_KERNEL, w), F32),
                 vsds, vsds, vsds, vsds, vsds, jax.ShapeDtypeStruct((groups, B_CHUNK, B_CHUNK), F32),
                 jax.ShapeDtypeStruct((groups, B_CHUNK, 1), F32)]
    return pl.pallas_call(body, name=name, grid=(ni,), in_specs=in_specs, out_specs=out_specs, out_shape=out_shape,
                          compiler_params=_cp("arbitrary"))(h0, h0, h0, dab, dab, cw, cb, ga, ba, gb, bb, ws, wst, sbb, tril)


def _attn_mask(n):
    qi = lax.broadcasted_iota(jnp.int32, (ATT_BLOCK, 2 * ATT_BLOCK), 0)
    sj = lax.broadcasted_iota(jnp.int32, (ATT_BLOCK, 2 * ATT_BLOCK), 1)
    diff = qi + ATT_BLOCK - sj
    return (diff >= 0) & (diff < ATT_BLOCK) & ((n > 0) | (sj >= ATT_BLOCK))


def _attn_specs(rows, n_q):
    dq = n_q * HEAD_DIM
    dkv = 2 * (n_q // Q_PER_KV) * HEAD_DIM
    kv_blk = dq // dkv
    assert dq % dkv == 0
    return dq, dkv, [
        pl.BlockSpec(memory_space=pltpu.SMEM),
        pl.BlockSpec((ATT_BLOCK, dq), lambda n: (n, 0)),
        pl.BlockSpec((ATT_BLOCK, dkv), lambda n: (n, kv_blk)),
        pl.BlockSpec((ATT_BLOCK, dkv), lambda n: (jnp.maximum(n - 1, 0), kv_blk)),
    ]


def _kv_pair(kvc_ref, kvp_ref, kvh, n_kv):
    ks = slice(kvh * HEAD_DIM, (kvh + 1) * HEAD_DIM)
    vs = slice((n_kv + kvh) * HEAD_DIM, (n_kv + kvh + 1) * HEAD_DIM)
    kk = jnp.concatenate([kvp_ref[:, ks], kvc_ref[:, ks]], axis=0).astype(BF16)
    vv = jnp.concatenate([kvp_ref[:, vs], kvc_ref[:, vs]], axis=0).astype(BF16)
    return kk, vv


def _attn_fwd(qkv, sinks, *, name):
    rows = qkv.shape[0]
    n_q = sinks.shape[0]
    n_kv = n_q // Q_PER_KV
    scale = 1.0 / math.sqrt(HEAD_DIM)
    dq, _, in_specs = _attn_specs(rows, n_q)

    def body(sink_ref, q_ref, kvc_ref, kvp_ref, o_ref, lse_ref):
        valid = _attn_mask(pl.program_id(0))
        for kvh in range(n_kv):
            kk, vv = _kv_pair(kvc_ref, kvp_ref, kvh, n_kv)
            for g in range(Q_PER_KV):
                h = kvh * Q_PER_KV + g
                hs = slice(h * HEAD_DIM, (h + 1) * HEAD_DIM)
                s = lax.dot_general(q_ref[:, hs].astype(BF16), kk, (((1,), (1,)), ((), ())), preferred_element_type=F32)
                s = jnp.where(valid, s * scale, -jnp.inf)
                sk = sink_ref[h]
                m = jnp.maximum(jnp.max(s, axis=1, keepdims=True), sk)
                p = jnp.exp(s - m)
                l = jnp.sum(p, axis=1, keepdims=True) + jnp.exp(sk - m)
                o_ref[:, hs] = jnp.dot((p / l).astype(BF16), vv, preferred_element_type=F32)
                lse_ref[:, h:h + 1] = m + jnp.log(l)

    out_specs = [pl.BlockSpec((ATT_BLOCK, dq), lambda n: (n, 0)), pl.BlockSpec((ATT_BLOCK, n_q), lambda n: (n, 0))]
    out_shape = [jax.ShapeDtypeStruct((rows, dq), F32), jax.ShapeDtypeStruct((rows, n_q), F32)]
    return pl.pallas_call(body, name=name, grid=(rows // ATT_BLOCK,), in_specs=in_specs, out_specs=out_specs,
                          out_shape=out_shape, compiler_params=_cp("parallel"))(sinks, qkv, qkv, qkv)


def _attn_bwd(qkv, dout, lse, sinks, *, name):
    rows = qkv.shape[0]
    n_q = sinks.shape[0]
    n_kv = n_q // Q_PER_KV
    scale = 1.0 / math.sqrt(HEAD_DIM)
    dq_w, dkv_w, in_specs = _attn_specs(rows, n_q)
    blk_q = pl.BlockSpec((ATT_BLOCK, dq_w), lambda n: (n, 0))
    blk_kv = pl.BlockSpec((ATT_BLOCK, dkv_w), lambda n: (n, 0))
    in_specs = in_specs + [blk_q, pl.BlockSpec((ATT_BLOCK, n_q), lambda n: (n, 0))]

    def body(sink_ref, q_ref, kvc_ref, kvp_ref, do_ref, lse_ref, dq_ref, dkc_ref, dkp_ref, dsink_ref):
        n = pl.program_id(0)

        @pl.when(n == 0)
        def _():
            dsink_ref[...] = jnp.zeros_like(dsink_ref)

        valid = _attn_mask(n)
        head_ids = lax.broadcasted_iota(jnp.int32, (1, n_q), 1)
        dsink = jnp.zeros((1, n_q), F32)
        for kvh in range(n_kv):
            kk, vv = _kv_pair(kvc_ref, kvp_ref, kvh, n_kv)
            dk = jnp.zeros((2 * ATT_BLOCK, HEAD_DIM), F32)
            dv = jnp.zeros((2 * ATT_BLOCK, HEAD_DIM), F32)
            for g in range(Q_PER_KV):
                h = kvh * Q_PER_KV + g
                hs = slice(h * HEAD_DIM, (h + 1) * HEAD_DIM)
                qh = q_ref[:, hs].astype(BF16)
                s = lax.dot_general(qh, kk, (((1,), (1,)), ((), ())), preferred_element_type=F32)
                s = jnp.where(valid, s * scale, -jnp.inf)
                lse_h = lse_ref[:, h:h + 1]
                p = jnp.exp(s - lse_h)
                doh = do_ref[:, hs].astype(BF16)
                dp = lax.dot_general(doh, vv, (((1,), (1,)), ((), ())), preferred_element_type=F32)
                delta = jnp.sum(p * dp, axis=1, keepdims=True)
                ds = (p * (dp - delta) * scale).astype(BF16)
                dsink = dsink + jnp.where(head_ids == h, -jnp.sum(jnp.exp(sink_ref[h] - lse_h) * delta), 0.0)
                dq_ref[:, hs] = jnp.dot(ds, kk, preferred_element_type=F32)
                dk = dk + lax.dot_general(ds, qh, (((0,), (0,)), ((), ())), preferred_element_type=F32)
                dv = dv + lax.dot_general(p.astype(BF16), doh, (((0,), (0,)), ((), ())), preferred_element_type=F32)
            ks = slice(kvh * HEAD_DIM, (kvh + 1) * HEAD_DIM)
            vs = slice((n_kv + kvh) * HEAD_DIM, (n_kv + kvh + 1) * HEAD_DIM)
            dkp_ref[:, ks] = dk[0:ATT_BLOCK]
            dkc_ref[:, ks] = dk[ATT_BLOCK:]
            dkp_ref[:, vs] = dv[0:ATT_BLOCK]
            dkc_ref[:, vs] = dv[ATT_BLOCK:]
        dsink_ref[...] += dsink

    out_specs = [blk_q, blk_kv, blk_kv, pl.BlockSpec((1, n_q), lambda n: (0, 0))]
    out_shape = [jax.ShapeDtypeStruct((rows, dq_w), F32), jax.ShapeDtypeStruct((rows, dkv_w), F32),
                 jax.ShapeDtypeStruct((rows, dkv_w), F32), jax.ShapeDtypeStruct((1, n_q), F32)]
    return pl.pallas_call(body, name=name, grid=(rows // ATT_BLOCK,), in_specs=in_specs, out_specs=out_specs,
                          out_shape=out_shape, compiler_params=_cp("arbitrary"))(sinks, qkv, qkv, qkv, dout, lse)


def _dqkv_assemble(dq, dkc, dkp, *, name):
    rows, dq_w = dq.shape
    dkv_w = dkc.shape[1]
    nb = rows // ATT_BLOCK

    def body(dq_ref, dkc_ref, dkp_ref, o_ref, db_ref):
        n = pl.program_id(0)

        @pl.when(n == 0)
        def _():
            db_ref[...] = jnp.zeros_like(db_ref)

        dqv = dq_ref[...]
        dkv = dkc_ref[...] + jnp.where(n == nb - 1, 0.0, dkp_ref[...])
        o_ref[:, 0:dq_w] = dqv.astype(BF16)
        o_ref[:, dq_w:dq_w + dkv_w] = dkv.astype(BF16)
        db_ref[:, 0:dq_w] += _rowsum(dqv)
        db_ref[:, dq_w:dq_w + dkv_w] += _rowsum(dkv)

    width = dq_w + dkv_w
    in_specs = [pl.BlockSpec((ATT_BLOCK, dq_w), lambda n: (n, 0)), pl.BlockSpec((ATT_BLOCK, dkv_w), lambda n: (n, 0)),
                pl.BlockSpec((ATT_BLOCK, dkv_w), lambda n: (jnp.minimum(n + 1, nb - 1), 0))]
    out_specs = [pl.BlockSpec((ATT_BLOCK, width), lambda n: (n, 0)), pl.BlockSpec((1, width), lambda n: (0, 0))]
    out_shape = [jax.ShapeDtypeStruct((rows, width), BF16), jax.ShapeDtypeStruct((1, width), F32)]
    return pl.pallas_call(body, name=name, grid=(nb,), in_specs=in_specs, out_specs=out_specs, out_shape=out_shape,
                          compiler_params=_cp("arbitrary"))(dq, dkc, dkp)


def _row_tile(r, c):
    budget = 2 * 1024 * 1024 // (4 * c)
    for cand in (1024, 512, 256, 128, 64, 32, 16):
        if cand <= budget and r % cand == 0:
            return cand
    return r


def _pair_sum(g, recv, c_idx, *, name):
    _, _, r, c = g.shape
    t = _row_tile(r, c)

    def body(c_ref, g_ref, r_ref, o_ref):
        o_ref[...] = (g_ref[...] + r_ref[...]).astype(BF16)

    grid_spec = pltpu.PrefetchScalarGridSpec(
        num_scalar_prefetch=1, grid=(4, r // t),
        in_specs=[pl.BlockSpec((None, None, t, c), lambda q, i, cr: (q, cr[0], i, 0)),
                  pl.BlockSpec((None, t, c), lambda q, i, cr: (q, i, 0))],
        out_specs=pl.BlockSpec((None, t, c), lambda q, i, cr: (q, i, 0)))
    return pl.pallas_call(body, name=name, grid_spec=grid_spec, out_shape=jax.ShapeDtypeStruct((4, r, c), BF16),
                          compiler_params=_cp("parallel", "parallel"))(c_idx, g, recv)


def _quad_sum(parts, *, name):
    _, r, c = parts.shape
    t = _row_tile(r, c)

    def body(p_ref, o_ref):
        acc = p_ref[0].astype(F32) + p_ref[1].astype(F32)
        acc = acc + p_ref[2].astype(F32)
        o_ref[...] = acc + p_ref[3].astype(F32)

    return pl.pallas_call(body, name=name, grid=(r // t,), in_specs=[pl.BlockSpec((4, t, c), lambda i: (0, i, 0))],
                          out_specs=pl.BlockSpec((t, c), lambda i: (i, 0)), out_shape=jax.ShapeDtypeStruct((r, c), F32),
                          compiler_params=_cp("parallel"))(parts)


def _adamw(w, g, m, v, *, name):
    r, c = w.shape
    t = _row_tile(r, c)

    def body(w_ref, g_ref, m_ref, v_ref, d_ref, nm_ref, nv_ref):
        gv = g_ref[...]
        nm = ADAM_B1 * m_ref[...] + (1.0 - ADAM_B1) * gv
        nv = ADAM_B2 * v_ref[...] + (1.0 - ADAM_B2) * (gv * gv)
        m_hat = nm / (1.0 - ADAM_B1 ** ADAM_STEP)
        v_hat = nv / (1.0 - ADAM_B2 ** ADAM_STEP)
        d_ref[...] = -ADAM_LR * (m_hat / (jnp.sqrt(v_hat) + ADAM_EPS) + ADAM_WD * w_ref[...])
        nm_ref[...] = nm
        nv_ref[...] = nv

    blk = pl.BlockSpec((t, c), lambda i: (i, 0))
    sds = jax.ShapeDtypeStruct((r, c), F32)
    return pl.pallas_call(body, name=name, grid=(r // t,), in_specs=[blk] * 4, out_specs=[blk] * 3,
                          out_shape=[sds] * 3, compiler_params=_cp("parallel"))(w, g, m, v)


HBM = pl.BlockSpec(memory_space=pl.ANY)


def _place():
    x, y, c = lax.axis_index("x"), lax.axis_index("y"), lax.axis_index("c")
    chips = [(1 - x, y), (x, 1 - y), (1 - x, 1 - y)]
    return x, y, c, 2 * x + y, (x, y, 1 - c), chips


def _rcopy(src, dst, ssem, rsem, dev):
    return pltpu.make_async_remote_copy(src_ref=src, dst_ref=dst, send_sem=ssem, recv_sem=rsem, device_id=dev,
                                        device_id_type=MESH)


def _all_gather(parts, *, name):
    nt = len(parts)

    def body(*refs):
        ins, outs = refs[:nt], refs[nt:2 * nt]
        ssem, rsem, lsem = refs[2 * nt:]
        x, y, c, q, sib, chips = _place()
        local = [pltpu.make_async_copy(ins[t], outs[t].at[q], lsem.at[t]) for t in range(nt)]
        for cp in local:
            cp.start()
        sends = []
        for t in range(nt):
            for j, (px, py) in enumerate(chips):
                cp = _rcopy(ins[t].at[c], outs[t].at[q, c], ssem.at[t, j], rsem.at[t, j], (px, py, c))
                cp.start()
                sends.append(cp)
        for t in range(nt):
            for j, (px, py) in enumerate(chips):
                landed = outs[t].at[2 * px + py, c]
                _rcopy(landed, landed, ssem.at[t, j], rsem.at[t, j], (px, py, c)).wait_recv()
                cp = _rcopy(landed, landed, ssem.at[t, 3 + j], rsem.at[t, 3 + j], sib)
                cp.start()
                sends.append(cp)
        for t in range(nt):
            for j, (px, py) in enumerate(chips):
                passed = outs[t].at[2 * px + py, 1 - c]
                _rcopy(passed, passed, ssem.at[t, 3 + j], rsem.at[t, 3 + j], sib).wait_recv()
        for cp in sends:
            cp.wait_send()
        for cp in local:
            cp.wait()

    out_shape = [jax.ShapeDtypeStruct((4,) + p.shape, p.dtype) for p in parts]
    return pl.pallas_call(
        body, name=name, in_specs=[HBM] * nt, out_specs=[HBM] * nt, out_shape=out_shape,
        scratch_shapes=[pltpu.SemaphoreType.DMA((nt, 6)), pltpu.SemaphoreType.DMA((nt, 6)), pltpu.SemaphoreType.DMA((nt,))],
    )(*parts)


def _sibling_exchange(grads, *, name):
    nt = len(grads)

    def body(*refs):
        ins, outs = refs[:nt], refs[nt:2 * nt]
        ssem, rsem = refs[2 * nt:]
        x, y, c, q, sib, chips = _place()
        sends = []
        for t in range(nt):
            for k in range(4):
                cp = _rcopy(ins[t].at[k, 1 - c], outs[t].at[k], ssem.at[t, k], rsem.at[t, k], sib)
                cp.start()
                sends.append(cp)
        for t in range(nt):
            for k in range(4):
                _rcopy(ins[t].at[k, c], outs[t].at[k], ssem.at[t, k], rsem.at[t, k], sib).wait_recv()
        for cp in sends:
            cp.wait_send()

    out_shape = [jax.ShapeDtypeStruct((4,) + g.shape[2:], g.dtype) for g in grads]
    return pl.pallas_call(
        body, name=name, in_specs=[HBM] * nt, out_specs=[HBM] * nt, out_shape=out_shape,
        scratch_shapes=[pltpu.SemaphoreType.DMA((nt, 4)), pltpu.SemaphoreType.DMA((nt, 4))],
    )(*grads)


def _chip_exchange(partials, *, name):
    nt = len(partials)

    def body(*refs):
        ins, outs = refs[:nt], refs[nt:2 * nt]
        ssem, rsem, lsem = refs[2 * nt:]
        x, y, c, q, sib, chips = _place()
        local = [pltpu.make_async_copy(ins[t].at[q], outs[t].at[q], lsem.at[t]) for t in range(nt)]
        for cp in local:
            cp.start()
        sends = []
        for t in range(nt):
            for j, (px, py) in enumerate(chips):
                cp = _rcopy(ins[t].at[2 * px + py], outs[t].at[q], ssem.at[t, j], rsem.at[t, j], (px, py, c))
                cp.start()
                sends.append(cp)
        for t in range(nt):
            for j, (px, py) in enumerate(chips):
                slot = outs[t].at[2 * px + py]
                _rcopy(slot, slot, ssem.at[t, j], rsem.at[t, j], (px, py, c)).wait_recv()
        for cp in sends:
            cp.wait_send()
        for cp in local:
            cp.wait()

    out_shape = [jax.ShapeDtypeStruct(p.shape, p.dtype) for p in partials]
    return pl.pallas_call(
        body, name=name, in_specs=[HBM] * nt, out_specs=[HBM] * nt, out_shape=out_shape,
        scratch_shapes=[pltpu.SemaphoreType.DMA((nt, 3)), pltpu.SemaphoreType.DMA((nt, 3)), pltpu.SemaphoreType.DMA((nt,))],
    )(*partials)


def _sibling_share(halves, layout, out_shapes, *, name):
    nt = len(halves)
    no = len(out_shapes)

    def body(*refs):
        ins, outs = refs[:nt], refs[nt:nt + no]
        ssem, rsem, lsem = refs[nt + no:]
        x, y, c, q, sib, chips = _place()

        def slot(t, half):
            o, lead = layout[t]
            return outs[o].at[half] if lead is None else outs[o].at[lead, half]

        local = [pltpu.make_async_copy(ins[t], slot(t, c), lsem.at[t]) for t in range(nt)]
        for cp in local:
            cp.start()
        sends = []
        for t in range(nt):
            cp = _rcopy(ins[t], slot(t, c), ssem.at[t], rsem.at[t], sib)
            cp.start()
            sends.append(cp)
        for t in range(nt):
            _rcopy(ins[t], slot(t, 1 - c), ssem.at[t], rsem.at[t], sib).wait_recv()
        for cp in sends:
            cp.wait_send()
        for cp in local:
            cp.wait()

    out_shape = [jax.ShapeDtypeStruct(s, F32) for s in out_shapes]
    return pl.pallas_call(
        body, name=name, in_specs=[HBM] * nt, out_specs=[HBM] * no, out_shape=out_shape,
        scratch_shapes=[pltpu.SemaphoreType.DMA((nt,)), pltpu.SemaphoreType.DMA((nt,)), pltpu.SemaphoreType.DMA((nt,))],
    )(*halves)


def _small_all_reduce(pack, *, name):
    r, lanes = pack.shape

    def body(in_ref, out_ref, gath, ssem, rsem):
        x, y, c = lax.axis_index("x"), lax.axis_index("y"), lax.axis_index("c")
        me = 4 * x + 2 * y + c
        gath[me] = in_ref[...]
        sends = []
        for mask in range(1, 8):
            px = 1 - x if mask & 4 else x
            py = 1 - y if mask & 2 else y
            pc = 1 - c if mask & 1 else c
            cp = _rcopy(in_ref, gath.at[me], ssem.at[mask - 1], rsem.at[mask - 1], (px, py, pc))
            cp.start()
            sends.append(cp)
        for mask in range(1, 8):
            px = 1 - x if mask & 4 else x
            py = 1 - y if mask & 2 else y
            pc = 1 - c if mask & 1 else c
            peer = gath.at[4 * px + 2 * py + pc]
            _rcopy(peer, peer, ssem.at[mask - 1], rsem.at[mask - 1], (px, py, pc)).wait_recv()
        acc = gath[0]
        for d in range(1, 8):
            acc = acc + gath[d]
        out_ref[...] = acc
        for cp in sends:
            cp.wait_send()

    vm = pl.BlockSpec(memory_space=pltpu.VMEM)
    return pl.pallas_call(
        body, name=name, in_specs=[vm], out_specs=vm, out_shape=jax.ShapeDtypeStruct((r, lanes), F32),
        scratch_shapes=[pltpu.VMEM((8, r, lanes), F32), pltpu.SemaphoreType.DMA((7,)), pltpu.SemaphoreType.DMA((7,))],
        compiler_params=pltpu.CompilerParams(vmem_limit_bytes=VMEM_LIMIT),
    )(pack)


def _pack(arrays, rows_multiple):
    flat = jnp.concatenate([a.reshape(-1) for a in arrays])
    rows = -(-flat.shape[0] // LANES)
    rows = -(-rows // rows_multiple) * rows_multiple
    flat = jnp.pad(flat, (0, rows * LANES - flat.shape[0]))
    return flat.reshape(rows, LANES)


def _unpack(buf, shapes):
    flat = buf.reshape(-1)
    out, pos = [], 0
    for s in shapes:
        n = math.prod(s)
        out.append(flat[pos:pos + n].reshape(s))
        pos += n
    return out


def _unshard_cols(stacked):
    moved = jnp.moveaxis(stacked, 0, -2)
    return moved.reshape(moved.shape[:-2] + (4 * stacked.shape[-1],))


def _shard_cols(full, q):
    n = full.shape[-1] // 4
    return lax.dynamic_slice_in_dim(full, q * n, n, axis=full.ndim - 1)


def kernel(x, ab_w_in, a_conv_w, a_conv_b, a_norm_g, a_norm_b, b_norm_g, b_norm_b, b_spatial_w, b_spatial_b, ab_w_out, c_w_qkv, c_b_qkv, c_sinks, c_w_o, ffn_w_up, ffn_conv_w, ffn_conv_b, ffn_w_down, ln_g, ln_b, loss_target, m_ab_w_in, m_a_conv_w, m_a_conv_b, m_a_norm_g, m_a_norm_b, m_b_norm_g, m_b_norm_b, m_b_spatial_w, m_b_spatial_b, m_ab_w_out, m_c_w_qkv, m_c_b_qkv, m_c_sinks, m_c_w_o, m_ffn_w_up, m_ffn_conv_w, m_ffn_conv_b, m_ffn_w_down, m_ln_g, m_ln_b, v_ab_w_in, v_a_conv_w, v_a_conv_b, v_a_norm_g, v_a_norm_b, v_b_norm_g, v_b_norm_b, v_b_spatial_w, v_b_spatial_b, v_ab_w_out, v_c_w_qkv, v_c_b_qkv, v_c_sinks, v_c_w_o, v_ffn_w_up, v_ffn_conv_w, v_ffn_conv_b, v_ffn_w_down, v_ln_g, v_ln_b):
    rows, d = x.shape[1], x.shape[2]
    depth = ln_g.shape[0]
    assert depth == 2 and x.shape[0] == 1
    alpha = (2.0 * depth) ** 0.25
    w_a = ab_w_in.shape[2] * 4 // 4
    f = ffn_w_down.shape[1] * 4
    n_q = c_sinks.shape[1]
    q_idx = 2 * lax.axis_index("x") + lax.axis_index("y")
    c_idx = lax.axis_index("c")
    xs, tgt = x[0], loss_target[0]

    def halves(wm):
        return wm.astype(BF16).reshape((2, wm.shape[0] // 2) + wm.shape[1:])

    small_sharded = [a_conv_w[0], c_b_qkv[0], ffn_conv_w, ln_g, ln_b]
    small_pack = _pack(small_sharded, 16)
    parts = [halves(ab_w_in[0]), halves(ab_w_out[0]), halves(ffn_w_up[0]), halves(ffn_w_down[0]),
             halves(c_w_qkv[0]), halves(c_w_o[0]), halves(ffn_w_up[1]), halves(ffn_w_down[1]),
             small_pack.reshape(2, small_pack.shape[0] // 2, LANES)]
    gathered = _all_gather(parts, name="gather_weights")

    def whole(g):
        return g.reshape(4, 2 * g.shape[2], g.shape[3])

    w_in = whole(gathered[0])
    w_out = whole(gathered[1]).reshape(-1, d)
    w_up = [whole(gathered[2]), whole(gathered[6])]
    w_down = [whole(gathered[3]).reshape(-1, d), whole(gathered[7]).reshape(-1, d)]
    w_qkv = _unshard_cols(whole(gathered[4]))
    w_o = whole(gathered[5]).reshape(-1, d)
    small_all = gathered[8].reshape(4, -1)
    sh_shapes = [s.shape for s in small_sharded]
    pieces, pos = [], 0
    for s in sh_shapes:
        n = math.prod(s)
        pieces.append(_unshard_cols(small_all[:, pos:pos + n].reshape((4,) + s)))
        pos += n
    conv_w_a, b_qkv, conv_w_f, ln_gf, ln_bf = pieces

    tril = jnp.tril(jnp.ones((B_CHUNK, B_CHUNK), F32))
    ws = (b_spatial_w[0] * tril).astype(BF16)
    wst = jnp.swapaxes(ws, 1, 2)
    sbb = jnp.broadcast_to(b_spatial_b[0][:, :, None], b_spatial_w[0].shape)
    mix_vecs = [a_conv_b, a_norm_g, a_norm_b, b_norm_g, b_norm_b]
    cw_f = [jnp.swapaxes(conv_w_f[l].reshape(3, 2, f), 0, 1) for l in range(depth)]
    cb_f = [ffn_conv_b[l].reshape(2, 1, f) for l in range(depth)]
    lng = lambda i, j: ln_gf[i, j].reshape(1, d)
    lnb = lambda i, j: ln_bf[i, j].reshape(1, d)
    sinks = c_sinks[0]

    def ffn_fwd(xin, l):
        hf = _matmul(xin, w_up[l], name=f"ffn{l}_up", tm=512, tn=1408, tk=1024, out_stack=2)
        fact = _ffn_mid_fwd(hf, cw_f[l], cb_f[l], name=f"ffn{l}_mid")
        out = _matmul(fact, w_down[l], name=f"ffn{l}_down", tm=1024, tn=1024, tk=1408)
        return hf, fact, out

    h0 = _matmul(xs, w_in, name="mix_in", tm=1024, tn=512, tk=1024, out_stack=4)
    ab = _mixer_fwd(h0, conv_w_a, *mix_vecs, ws, sbb, name="mix_mid")
    mix = _matmul(ab, w_out, name="mix_out", tm=1024, tn=1024, tk=1024)
    x1 = _add_ln_fwd(xs, mix, lng(0, 0), lnb(0, 0), alpha, name="ln00")
    hf0, f0, ffn0 = ffn_fwd(x1, 0)
    x2 = _add_ln_fwd(x1, ffn0, lng(0, 1), lnb(0, 1), alpha, name="ln01")
    qkv = _matmul(x2, w_qkv, name="att_qkv", tm=1024, tn=w_qkv.shape[1], tk=1024, bias=b_qkv.reshape(1, -1))
    ao, lse = _attn_fwd(qkv, sinks, name="att_core")
    att = _matmul(ao, w_o, name="att_out", tm=1024, tn=1024, tk=1024)
    x3 = _add_ln_fwd(x2, att, lng(1, 0), lnb(1, 0), alpha, name="ln10")
    hf1, f1, ffn1 = ffn_fwd(x3, 1)
    x4 = _add_ln_fwd(x3, ffn1, lng(1, 1), lnb(1, 1), alpha, name="ln11")
    sq_err, dy = _loss_and_grad(x4, tgt, name="loss")

    def ffn_bwd(dz, xin, hf, fact, l):
        d_wdown = _matmul(fact, dz, name=f"ffn{l}_down_dw", ta=True, tm=1408, tn=1024, tk=512)
        dfa = _matmul(dz, w_down[l], name=f"ffn{l}_down_dx", tb=True, tm=1024, tn=1408, tk=1024)
        dhf, dcw, dcb = _ffn_mid_bwd(hf, dfa, cw_f[l], cb_f[l], name=f"ffn{l}_mid_bwd")
        d_wup = _matmul(xin, dhf, name=f"ffn{l}_up_dw", ta=True, tm=1024, tn=1408, tk=512, out_stack=4)
        dxin = _matmul(dhf, w_up[l], name=f"ffn{l}_up_dx", tb=True, tm=1024, tn=1024, tk=1408, addend=(dz, alpha))
        return dxin, d_wup, d_wdown, dcw, dcb

    dz, dg11, db11 = _add_ln_bwd(dy, x3, ffn1, lng(1, 1), alpha, name="ln11_bwd")
    dx3, d_wup1, d_wdown1, dcw1, dcb1 = ffn_bwd(dz, x3, hf1, f1, 1)
    dz, dg10, db10 = _add_ln_bwd(dx3, x2, att, lng(1, 0), alpha, name="ln10_bwd")
    d_wo = _matmul(ao, dz, name="att_out_dw", ta=True, tm=1024, tn=1024, tk=1024)
    dao = _matmul(dz, w_o, name="att_out_dx", tb=True, tm=1024, tn=1024, tk=1024)
    dq, dkc, dkp, d_sinks = _attn_bwd(qkv, dao, lse, sinks, name="att_core_bwd")
    dqkv, d_bqkv = _dqkv_assemble(dq, dkc, dkp, name="att_dqkv")
    d_wqkv = _matmul(x2, dqkv, name="att_qkv_dw", ta=True, tm=1024, tn=dqkv.shape[1], tk=1024)
    dx2 = _matmul(dqkv, w_qkv, name="att_qkv_dx", tb=True, tm=1024, tn=1024, tk=dqkv.shape[1], addend=(dz, alpha))
    dz, dg01, db01 = _add_ln_bwd(dx2, x1, ffn0, lng(0, 1), alpha, name="ln01_bwd")
    dx1, d_wup0, d_wdown0, dcw0, dcb0 = ffn_bwd(dz, x1, hf0, f0, 0)
    dz, dg00, db00 = _add_ln_bwd(dx1, xs, mix, lng(0, 0), alpha, name="ln00_bwd")
    d_wout = _matmul(ab, dz, name="mix_out_dw", ta=True, tm=1024, tn=1024, tk=1024)
    dab = _matmul(dz, w_out, name="mix_out_dx", tb=True, tm=1024, tn=1024, tk=1024)
    dh0, d_cwa, d_cba, d_ga, d_ba, d_gb, d_bb, d_ws, d_sb = _mixer_bwd(
        h0, dab, conv_w_a, *mix_vecs, ws, wst, sbb, tril, name="mix_mid_bwd")
    d_win = _matmul(xs, dh0, name="mix_in_dw", ta=True, tm=1024, tn=512, tk=1024, out_stack=4)
    grad_x = _matmul(dh0, w_in, name="mix_in_dx", tb=True, tm=1024, tn=1024, tk=512, addend=(dz, alpha))

    def owner_view(g):
        if g.ndim == 3:
            return g.reshape(4, 2, g.shape[1] // 2, g.shape[2])
        return g.reshape(4, 2, g.shape[0] // 8, g.shape[1])

    d_wqkv_st = jnp.moveaxis(d_wqkv.reshape(d_wqkv.shape[0], 4, -1), 1, 0)
    big = [owner_view(g) for g in (d_win, d_wout, d_wqkv_st, d_wo, d_wup0, d_wup1, d_wdown0, d_wdown1)]
    from_sib = _sibling_exchange(big, name="reduce_sibling")
    c_arr = jnp.reshape(c_idx, (1,)).astype(jnp.int32)
    partials = [_pair_sum(g, r, c_arr, name=f"reduce_pair{t}") for t, (g, r) in enumerate(zip(big, from_sib))]
    from_chips = _chip_exchange(partials, name="reduce_chips")
    mine = [_quad_sum(p, name=f"reduce_quad{t}") for t, p in enumerate(from_chips)]
    layout = [(0, None), (1, None), (2, None), (3, None), (4, 0), (4, 1), (5, 0), (5, 1)]
    shard_shapes = [(2,) + mine[0].shape, (2,) + mine[1].shape, (2,) + mine[2].shape, (2,) + mine[3].shape,
                    (2, 2) + mine[4].shape, (2, 2) + mine[6].shape]
    shared = _sibling_share(mine, layout, shard_shapes, name="reduce_share")
    g_win = shared[0].reshape(ab_w_in.shape)
    g_wout = shared[1].reshape(ab_w_out.shape)
    g_wqkv = shared[2].reshape(c_w_qkv.shape)
    g_wo = shared[3].reshape(c_w_o.shape)
    g_wup = shared[4].reshape(ffn_w_up.shape)
    g_wdown = shared[5].reshape(ffn_w_down.shape)

    d_cw_f = jnp.stack([jnp.swapaxes(dcw0, 0, 1).reshape(3, 2 * f), jnp.swapaxes(dcw1, 0, 1).reshape(3, 2 * f)])
    d_cb_f = jnp.stack([dcb0.reshape(2 * f), dcb1.reshape(2 * f)])
    d_lng = jnp.stack([jnp.stack([dg00[0], dg01[0]]), jnp.stack([dg10[0], dg11[0]])])
    d_lnb = jnp.stack([jnp.stack([db00[0], db01[0]]), jnp.stack([db10[0], db11[0]])])
    small_full = [d_cwa, d_cba, d_ga, d_ba, d_gb, d_bb, d_ws, d_sb, d_bqkv, d_sinks, d_cw_f, d_cb_f, d_lng, d_lnb, sq_err]
    reduced = _small_all_reduce(_pack(small_full, 8), name="reduce_small")
    (r_cwa, r_cba, r_ga, r_ba, r_gb, r_bb, r_ws, r_sb, r_bqkv, r_sinks, r_cwf, r_cbf, r_lng, r_lnb, r_err) = _unpack(
        reduced, [a.shape for a in small_full])
    loss = 0.5 * jnp.sum(r_err) / d

    small_names_w = [a_conv_w, a_conv_b, a_norm_g, a_norm_b, b_norm_g, b_norm_b, b_spatial_w, b_spatial_b, c_b_qkv,
                     c_sinks, ffn_conv_w, ffn_conv_b, ln_g, ln_b]
    small_m = [m_a_conv_w, m_a_conv_b, m_a_norm_g, m_a_norm_b, m_b_norm_g, m_b_norm_b, m_b_spatial_w, m_b_spatial_b,
               m_c_b_qkv, m_c_sinks, m_ffn_conv_w, m_ffn_conv_b, m_ln_g, m_ln_b]
    small_v = [v_a_conv_w, v_a_conv_b, v_a_norm_g, v_a_norm_b, v_b_norm_g, v_b_norm_b, v_b_spatial_w, v_b_spatial_b,
               v_c_b_qkv, v_c_sinks, v_ffn_conv_w, v_ffn_conv_b, v_ln_g, v_ln_b]
    small_g = [_shard_cols(r_cwa, q_idx), r_cba, r_ga, r_ba, r_gb, r_bb, r_ws, r_sb, _shard_cols(r_bqkv, q_idx), r_sinks,
               _shard_cols(r_cwf, q_idx), r_cbf, _shard_cols(r_lng, q_idx), _shard_cols(r_lnb, q_idx)]
    small_g = [g.reshape(w.shape) for g, w in zip(small_g, small_names_w)]
    sm_shapes = [w.shape for w in small_names_w]
    sm_delta, sm_m, sm_v = _adamw(_pack(small_names_w, 8), _pack(small_g, 8), _pack(small_m, 8), _pack(small_v, 8),
                                  name="adamw_small")
    sm_delta, sm_m, sm_v = _unpack(sm_delta, sm_shapes), _unpack(sm_m, sm_shapes), _unpack(sm_v, sm_shapes)

    def adamw_big(w, g, m, v, name):
        two_d = lambda a: a.reshape(-1, a.shape[-1])
        outs = _adamw(two_d(w), two_d(g), two_d(m), two_d(v), name=name)
        return [o.reshape(w.shape) for o in outs]

    big_w = [ab_w_in, ab_w_out, c_w_qkv, c_w_o, ffn_w_up, ffn_w_down]
    big_g = [g_win, g_wout, g_wqkv, g_wo, g_wup, g_wdown]
    big_m = [m_ab_w_in, m_ab_w_out, m_c_w_qkv, m_c_w_o, m_ffn_w_up, m_ffn_w_down]
    big_v = [v_ab_w_in, v_ab_w_out, v_c_w_qkv, v_c_w_o, v_ffn_w_up, v_ffn_w_down]
    big_out = [adamw_big(w, g, m, v, f"adamw_big{t}") for t, (w, g, m, v) in enumerate(zip(big_w, big_g, big_m, big_v))]

    order_big = {0: 0, 9: 1, 10: 2, 13: 3, 14: 4, 17: 5}
    order_small = {1: 0, 2: 1, 3: 2, 4: 3, 5: 4, 6: 5, 7: 6, 8: 7, 11: 8, 12: 9, 15: 10, 16: 11, 18: 12, 19: 13}
    grads, deltas, new_m, new_v = [], [], [], []
    for pos_w in range(20):
        if pos_w in order_big:
            t = order_big[pos_w]
            grads.append(big_g[t])
            deltas.append(big_out[t][0])
            new_m.append(big_out[t][1])
            new_v.append(big_out[t][2])
        else:
            t = order_small[pos_w]
            grads.append(small_g[t])
            deltas.append(sm_delta[t])
            new_m.append(sm_m[t])
            new_v.append(sm_v[t])
    return (loss, grad_x[None], *grads, *deltas, *new_m, *new_v)
```

```python
import functools
import math

import jax
import jax.numpy as jnp
from jax import lax
from jax.experimental import pallas as pl
from jax.experimental.pallas import tpu as pltpu

F32 = jnp.float32
BF16 = jnp.bfloat16
MESH = pl.DeviceIdType.MESH

LN_EPS = 1e-5
HEAD_DIM = 64
ATT_BLOCK = 128
Q_PER_KV = 8
A_KERNEL = 31
CONV_HALO = 32
FFN_HALO = 8
B_CHUNK = 128
LANES = 128
MXU_WIDTH = 256
GELU_C = math.sqrt(2.0 / math.pi)
ADAM_LR = 0.001
ADAM_B1 = 0.9
ADAM_B2 = 0.999
ADAM_EPS = 1e-08
ADAM_WD = 0.01
ADAM_STEP = 10
VMEM_LIMIT = 56 * 1024 * 1024


def _cp(*dims):
    return pltpu.CompilerParams(dimension_semantics=dims, vmem_limit_bytes=VMEM_LIMIT)


def _pick(n, prefs):
    for p in prefs:
        if n % p == 0:
            return p
    return n


def _sig(x):
    return 1.0 / (1.0 + jnp.exp(-x))


def _gelu(x):
    t = jnp.tanh(GELU_C * (x + 0.044715 * (x * x * x)))
    return x * (0.5 * (1.0 + t)), t


def _gelu_grad(x, t):
    return 0.5 * (1.0 + t) + 0.5 * x * (1.0 - t * t) * (GELU_C * (1.0 + 3.0 * 0.044715 * x * x))


def _ln_stats(z):
    mu = jnp.mean(z, axis=-1, keepdims=True)
    zc = z - mu
    var = jnp.mean(zc * zc, axis=-1, keepdims=True)
    rstd = lax.rsqrt(var + LN_EPS)
    return zc * rstd, rstd


def _ln_bwd(dxh, xh, rstd):
    return rstd * (dxh - jnp.mean(dxh, axis=-1, keepdims=True) - xh * jnp.mean(dxh * xh, axis=-1, keepdims=True))


def _rowsum(a):
    return jnp.sum(a, axis=0, keepdims=True)


def _lshape(a):
    return (a.shape[0], a.shape[1]) if a.ndim == 2 else (a.shape[1], a.shape[0] * a.shape[2])


def _spec2(arr, blk_r, blk_c, ridx, cidx):
    if len(arr.shape) == 2:
        return pl.BlockSpec((blk_r, blk_c), lambda i, j, k: (ridx(i, j, k), cidx(i, j, k)))
    per = arr.shape[2] // blk_c
    assert arr.shape[2] % blk_c == 0
    return pl.BlockSpec((None, blk_r, blk_c), lambda i, j, k: (cidx(i, j, k) // per, ridx(i, j, k), cidx(i, j, k) % per))


def _matmul(a, b, *, name, ta=False, tb=False, tm, tn, tk, out_dtype=F32, out_stack=None, bias=None, addend=None):
    ar, ac = _lshape(a)
    br, bc = _lshape(b)
    m, kdim = (ac, ar) if ta else (ar, ac)
    n = br if tb else bc
    assert (bc if tb else br) == kdim
    tm, tn, tk = min(tm, m), min(tn, n), min(tk, kdim)
    assert m % tm == 0 and n % tn == 0 and kdim % tk == 0, (name, m, n, kdim, tm, tn, tk)
    nk = kdim // tk
    gi, gj, gk = (lambda i, j, k: i), (lambda i, j, k: j), (lambda i, j, k: k)
    a_spec = _spec2(a, tk, tm, gk, gi) if ta else _spec2(a, tm, tk, gi, gk)
    b_spec = _spec2(b, tn, tk, gj, gk) if tb else _spec2(b, tk, tn, gk, gj)
    if out_stack is None:
        out_sds = jax.ShapeDtypeStruct((m, n), out_dtype)
    else:
        out_sds = jax.ShapeDtypeStruct((out_stack, m, n // out_stack), out_dtype)
    o_spec = _spec2(out_sds, tm, tn, gi, gj)
    in_specs = [a_spec, b_spec]
    args = [a, b]
    if bias is not None:
        in_specs.append(pl.BlockSpec((1, tn), lambda i, j, k: (0, j)))
        args.append(bias)
    scale = None
    if addend is not None:
        add_arr, scale = addend
        in_specs.append(pl.BlockSpec((tm, tn), lambda i, j, k: (i, j)))
        args.append(add_arr)
    use_acc = nk > 1 and out_dtype != F32
    dn = (((0 if ta else 1,), (1 if tb else 0,)), ((), ()))

    def body(*refs):
        a_ref, b_ref = refs[0], refs[1]
        pos = 2
        bias_ref = add_ref = None
        if bias is not None:
            bias_ref = refs[pos]
            pos += 1
        if addend is not None:
            add_ref = refs[pos]
            pos += 1
        o_ref = refs[pos]
        acc_ref = refs[pos + 1] if use_acc else o_ref
        p = lax.dot_general(a_ref[...].astype(BF16), b_ref[...].astype(BF16), dn, preferred_element_type=F32)

        def finish(val):
            if bias_ref is not None:
                val = val + bias_ref[...]
            if add_ref is not None:
                val = val + scale * add_ref[...]
            return val.astype(out_dtype)

        if nk == 1:
            o_ref[...] = finish(p)
        else:
            k = pl.program_id(2)

            @pl.when(k == 0)
            def _():
                acc_ref[...] = p

            @pl.when(k > 0)
            def _():
                acc_ref[...] += p

            if use_acc or bias_ref is not None or add_ref is not None:
                @pl.when(k == nk - 1)
                def _():
                    o_ref[...] = finish(acc_ref[...])

    return pl.pallas_call(
        body, name=name, grid=(m // tm, n // tn, nk), in_specs=in_specs, out_specs=o_spec, out_shape=out_sds,
        scratch_shapes=[pltpu.VMEM((tm, tn), F32)] if use_acc else [],
        compiler_params=_cp("parallel", "parallel", "arbitrary"),
    )(*args)


def _add_ln_fwd(x, s, g, b, alpha, *, name):
    rows, d = x.shape
    t = _pick(rows, (512, 256))

    def body(x_ref, s_ref, g_ref, b_ref, y_ref):
        xh, _ = _ln_stats(alpha * x_ref[...] + s_ref[...])
        y_ref[...] = xh * g_ref[...] + b_ref[...]

    row = pl.BlockSpec((t, d), lambda i: (i, 0))
    vec = pl.BlockSpec((1, d), lambda i: (0, 0))
    return pl.pallas_call(body, name=name, grid=(rows // t,), in_specs=[row, row, vec, vec], out_specs=row,
                          out_shape=jax.ShapeDtypeStruct((rows, d), F32), compiler_params=_cp("parallel"))(x, s, g, b)


def _add_ln_bwd(dy_terms, x, s, g, alpha, *, name):
    rows, d = x.shape
    t = _pick(rows, (512, 256))
    nterm = len(dy_terms)
    scales = [sc for _, sc in dy_terms]
    ranks = [a.ndim for a, _ in dy_terms]

    def body(*refs):
        dy_refs = refs[:nterm]
        x_ref, s_ref, g_ref, dz_ref, dg_ref, db_ref = refs[nterm:]

        @pl.when(pl.program_id(0) == 0)
        def _():
            dg_ref[...] = jnp.zeros_like(dg_ref)
            db_ref[...] = jnp.zeros_like(db_ref)

        dyv = None
        for r, sc, rank in zip(dy_refs, scales, ranks):
            slabs = [r[...]] if rank == 2 else [r[p] for p in range(r.shape[0])]
            for v in slabs:
                v = v if sc == 1.0 else sc * v
                dyv = v if dyv is None else dyv + v
        xh, rstd = _ln_stats(alpha * x_ref[...] + s_ref[...])
        dz_ref[...] = _ln_bwd(dyv * g_ref[...], xh, rstd)
        dg_ref[...] += _rowsum(dyv * xh)
        db_ref[...] += _rowsum(dyv)

    row = pl.BlockSpec((t, d), lambda i: (i, 0))
    vec = pl.BlockSpec((1, d), lambda i: (0, 0))
    vsds = jax.ShapeDtypeStruct((1, d), F32)
    dy_specs = [row if a.ndim == 2 else pl.BlockSpec((a.shape[0], t, d), lambda i: (0, i, 0)) for a, _ in dy_terms]
    return pl.pallas_call(body, name=name, grid=(rows // t,), in_specs=dy_specs + [row, row, vec], out_specs=[row, vec, vec],
                          out_shape=[jax.ShapeDtypeStruct((rows, d), F32), vsds, vsds],
                          compiler_params=_cp("arbitrary"))(*[a for a, _ in dy_terms], x, s, g)


def _loss_and_grad(y, tgt, *, name):
    rows, d = y.shape
    t = _pick(rows, (512, 256))

    def body(y_ref, t_ref, l_ref, dy_ref):
        @pl.when(pl.program_id(0) == 0)
        def _():
            l_ref[...] = jnp.zeros_like(l_ref)

        e = y_ref[...] - t_ref[...]
        l_ref[...] += _rowsum(e * e)
        dy_ref[...] = e * (1.0 / d)

    row = pl.BlockSpec((t, d), lambda i: (i, 0))
    vec = pl.BlockSpec((1, d), lambda i: (0, 0))
    return pl.pallas_call(body, name=name, grid=(rows // t,), in_specs=[row, row], out_specs=[vec, row],
                          out_shape=[jax.ShapeDtypeStruct((1, d), F32), jax.ShapeDtypeStruct((rows, d), F32)],
                          compiler_params=_cp("arbitrary"))(y, tgt)


def _col_blocks(width):
    out, pos = [], 0
    while pos < width:
        w = MXU_WIDTH if width - pos >= MXU_WIDTH else width - pos
        out.append(slice(pos, pos + w))
        pos += w
    return out


def _conv3(e, w, b):
    r1 = pltpu.roll(e, 1, 0)
    r2 = pltpu.roll(e, 2, 0)
    return w[0:1, :] * r2 + w[1:2, :] * r1 + w[2:3, :] * e + b, (r2, r1, e)


def _ffn_up_fwd(x, w_up, cw, cb, *, name):
    rows, d = x.shape
    nq, _, tc = w_up.shape
    nj = nq // 2
    f = tc * nj
    tm = _pick(rows, (512, 256))
    blocks = _col_blocks(tc)

    def body(x_ref, wg_ref, wv_ref, cw_ref, cb_ref, hf_ref, f_ref, prev_ref):
        @pl.when(pl.program_id(1) == 0)
        def _():
            prev_ref[...] = jnp.zeros_like(prev_ref)

        xb = x_ref[...].astype(BF16)
        for cs in blocks:
            hc = []
            for s, w_ref in ((0, wg_ref), (1, wv_ref)):
                h = jnp.dot(xb, w_ref[:, cs], preferred_element_type=F32)
                hf_ref[s, :, cs] = h
                e = jnp.concatenate([prev_ref[s, :, cs], h], axis=0)
                prev_ref[s, :, cs] = h[tm - FFN_HALO:]
                y, _ = _conv3(e, cw_ref[s, :, cs], cb_ref[s, :, cs])
                hc.append(y[FFN_HALO:])
            gl, _ = _gelu(hc[0])
            f_ref[:, cs] = (gl * hc[1]).astype(BF16)

    in_specs = [
        pl.BlockSpec((tm, d), lambda j, i: (i, 0)),
        pl.BlockSpec((None, d, tc), lambda j, i: (j, 0, 0)),
        pl.BlockSpec((None, d, tc), lambda j, i: (nj + j, 0, 0)),
        pl.BlockSpec((2, 3, tc), lambda j, i: (0, 0, j)),
        pl.BlockSpec((2, 1, tc), lambda j, i: (0, 0, j)),
    ]
    out_specs = [pl.BlockSpec((2, tm, tc), lambda j, i: (0, i, j)), pl.BlockSpec((tm, tc), lambda j, i: (i, j))]
    out_shape = [jax.ShapeDtypeStruct((2, rows, f), F32), jax.ShapeDtypeStruct((rows, f), BF16)]
    return pl.pallas_call(body, name=name, grid=(nj, rows // tm), in_specs=in_specs, out_specs=out_specs, out_shape=out_shape,
                          scratch_shapes=[pltpu.VMEM((2, FFN_HALO, tc), F32)],
                          compiler_params=_cp("parallel", "arbitrary"))(x, w_up, w_up, cw, cb)


def _ffn_up_bwd(hf, df, x, w_up, cw, cb, *, name):
    _, rows, f = hf.shape
    d = x.shape[1]
    nq, _, tc = w_up.shape
    nj = nq // 2
    tm = _pick(rows, (512, 256))
    hb = tm // FFN_HALO
    once = pl.Buffered(1)
    ni = rows // tm
    last_blk = rows // FFN_HALO - 1
    ext = tm + 2 * FFN_HALO
    tile = slice(FFN_HALO, FFN_HALO + tm)
    blocks = _col_blocks(tc)

    def body(h_ref, hp_ref, hn_ref, d_ref, dn_ref, x_ref, wg_ref, wv_ref, cw_ref, cb_ref, dx_ref, dw_ref, dcw_ref, dcb_ref):
        i = pl.program_id(1)
        first = i == 0
        last = i == ni - 1

        @pl.when(first)
        def _():
            dw_ref[...] = jnp.zeros_like(dw_ref)
            dcw_ref[...] = jnp.zeros_like(dcw_ref)
            dcb_ref[...] = jnp.zeros_like(dcb_ref)

        xt = x_ref[...].astype(BF16).T
        dx = None
        for cs in blocks:
            wc = cs.stop - cs.start
            de = jnp.concatenate([jnp.zeros((FFN_HALO, wc), F32), d_ref[:, cs], jnp.where(last, 0.0, dn_ref[:, cs])], axis=0)
            taps, hc = [], []
            for s in range(2):
                e = jnp.concatenate([jnp.where(first, 0.0, hp_ref[s, :, cs]), h_ref[s, :, cs], hn_ref[s, :, cs]], axis=0)
                y, tp = _conv3(e, cw_ref[s, :, cs], cb_ref[s, :, cs])
                hc.append(y)
                taps.append(tp)
            gl, th = _gelu(hc[0])
            dhc = (de * hc[1] * _gelu_grad(hc[0], th), de * gl)
            for s, w_ref in ((0, wg_ref), (1, wv_ref)):
                w = cw_ref[s, :, cs]
                g = dhc[s]
                dh = (w[2:3, :] * g + w[1:2, :] * pltpu.roll(g, ext - 1, 0) + w[0:1, :] * pltpu.roll(g, ext - 2, 0))[tile]
                gt = g[tile]
                for k in range(3):
                    dcw_ref[s, k:k + 1, cs] += _rowsum(gt * taps[s][k][tile])
                dcb_ref[s, :, cs] += _rowsum(gt)
                dhb = dh.astype(BF16)
                part = lax.dot_general(dhb, w_ref[:, cs], (((1,), (1,)), ((), ())), preferred_element_type=F32)
                dx = part if dx is None else dx + part
                dw_ref[s, :, cs] += jnp.dot(xt, dhb, preferred_element_type=F32)
        dx_ref[...] = dx

    in_specs = [
        pl.BlockSpec((2, tm, tc), lambda j, i: (0, i, j)),
        pl.BlockSpec((2, FFN_HALO, tc), lambda j, i: (0, jnp.maximum(i * hb - 1, 0), j)),
        pl.BlockSpec((2, FFN_HALO, tc), lambda j, i: (0, jnp.minimum((i + 1) * hb, last_blk), j)),
        pl.BlockSpec((tm, tc), lambda j, i: (i, j)),
        pl.BlockSpec((FFN_HALO, tc), lambda j, i: (jnp.minimum((i + 1) * hb, last_blk), j)),
        pl.BlockSpec((tm, d), lambda j, i: (i, 0)),
        pl.BlockSpec((None, d, tc), lambda j, i: (j, 0, 0), pipeline_mode=once),
        pl.BlockSpec((None, d, tc), lambda j, i: (nj + j, 0, 0), pipeline_mode=once),
        pl.BlockSpec((2, 3, tc), lambda j, i: (0, 0, j)),
        pl.BlockSpec((2, 1, tc), lambda j, i: (0, 0, j)),
    ]
    out_specs = [
        pl.BlockSpec((None, tm, d), lambda j, i: (j, i, 0)),
        pl.BlockSpec((2, None, d, tc), lambda j, i: (0, j, 0, 0), pipeline_mode=once),
        pl.BlockSpec((2, 3, tc), lambda j, i: (0, 0, j)),
        pl.BlockSpec((2, 1, tc), lambda j, i: (0, 0, j)),
    ]
    out_shape = [jax.ShapeDtypeStruct((nj, rows, d), F32), jax.ShapeDtypeStruct((2, nj, d, tc), F32),
                 jax.ShapeDtypeStruct((2, 3, f), F32), jax.ShapeDtypeStruct((2, 1, f), F32)]
    dx, dw, dcw, dcb = pl.pallas_call(body, name=name, grid=(nj, ni), in_specs=in_specs, out_specs=out_specs,
                                      out_shape=out_shape, compiler_params=_cp("parallel", "arbitrary"))(
        hf, hf, hf, df, df, x, w_up, w_up, cw, cb)
    return dx, dw.reshape(nq, d, tc), dcw, dcb


def _mixer_fwd(h0, cw, cb, ga, ba, gb, bb, ws, sbb, *, name):
    _, rows, w = h0.shape
    t = _pick(rows, (256,))
    hb = t // CONV_HALO
    groups = w // B_CHUNK

    def body(h_ref, hp_ref, cw_ref, cb_ref, ga_ref, ba_ref, gb_ref, bb_ref, ws_ref, sb_ref, o_ref):
        first = pl.program_id(0) == 0
        a1 = h_ref[0] * _sig(h_ref[1])
        a1p = jnp.where(first, 0.0, hp_ref[0] * _sig(hp_ref[1]))
        e = jnp.concatenate([a1p, a1], axis=0)
        acc = cw_ref[A_KERNEL - 1:A_KERNEL, :] * e
        for k in range(A_KERNEL - 1):
            acc = acc + cw_ref[k:k + 1, :] * pltpu.roll(e, A_KERNEL - 1 - k, 0)
        xh, _ = _ln_stats(acc[CONV_HALO:] + cb_ref[...])
        a3 = xh * ga_ref[...] + ba_ref[...]
        o_ref[:, 0:w] = (a3 * _sig(a3)).astype(BF16)

        u, _ = _gelu(h_ref[2])
        v1, _ = _gelu(h_ref[3])
        xh2, _ = _ln_stats(v1)
        v2 = (xh2 * gb_ref[...] + bb_ref[...]).astype(BF16)
        for c in range(t // B_CHUNK):
            rs = slice(c * B_CHUNK, (c + 1) * B_CHUNK)
            for g in range(groups):
                cs = slice(g * B_CHUNK, (g + 1) * B_CHUNK)
                mixed = jnp.dot(ws_ref[g], v2[rs, cs], preferred_element_type=F32) + sb_ref[g]
                o_ref[rs, w + g * B_CHUNK:w + (g + 1) * B_CHUNK] = (u[rs, cs] * mixed).astype(BF16)

    vec = pl.BlockSpec((1, w), lambda i: (0, 0))
    grp = pl.BlockSpec((groups, B_CHUNK, B_CHUNK), lambda i: (0, 0, 0))
    in_specs = [
        pl.BlockSpec((4, t, w), lambda i: (0, i, 0)),
        pl.BlockSpec((2, CONV_HALO, w), lambda i: (0, jnp.maximum(i * hb - 1, 0), 0)),
        pl.BlockSpec((A_KERNEL, w), lambda i: (0, 0)),
        vec, vec, vec, vec, vec, grp, grp,
    ]
    return pl.pallas_call(body, name=name, grid=(rows // t,), in_specs=in_specs,
                          out_specs=pl.BlockSpec((t, 2 * w), lambda i: (i, 0)),
                          out_shape=jax.ShapeDtypeStruct((rows, 2 * w), BF16),
                          compiler_params=_cp("parallel"))(h0, h0, cw, cb, ga, ba, gb, bb, ws, sbb)


def _mixer_bwd(h0, dab, cw, cb, ga, ba, gb, bb, ws, wst, sbb, tril, *, name):
    _, rows, w = h0.shape
    t = _pick(rows, (256,))
    hb = t // CONV_HALO
    ni = rows // t
    last_blk = rows // CONV_HALO - 1
    ext = t + 2 * CONV_HALO
    tile = slice(CONV_HALO, CONV_HALO + t)
    groups = w // B_CHUNK
    taps = A_KERNEL - 1

    def body(h_ref, hp_ref, hn_ref, d_ref, dn_ref, cw_ref, cb_ref, ga_ref, ba_ref, gb_ref, bb_ref, ws_ref, wst_ref,
             sb_ref, tril_ref, dh_ref, dcw_ref, dcb_ref, dga_ref, dba_ref, dgb_ref, dbb_ref, dws_ref, dsb_ref):
        i = pl.program_id(0)
        first = i == 0
        last = i == ni - 1

        @pl.when(first)
        def _():
            for r in (dcw_ref, dcb_ref, dga_ref, dba_ref, dgb_ref, dbb_ref, dws_ref, dsb_ref):
                r[...] = jnp.zeros_like(r)

        av_e = jnp.concatenate([hp_ref[0], h_ref[0], hn_ref[0]], axis=0)
        sg_e = _sig(jnp.concatenate([hp_ref[1], h_ref[1], hn_ref[1]], axis=0))
        rows_e = lax.broadcasted_iota(jnp.int32, (ext, 1), 0)
        a1_e = jnp.where(first & (rows_e < CONV_HALO), 0.0, av_e * sg_e)
        acc = cw_ref[taps:taps + 1, :] * a1_e
        for k in range(taps):
            acc = acc + cw_ref[k:k + 1, :] * pltpu.roll(a1_e, taps - k, 0)
        xh, rstd = _ln_stats(acc + cb_ref[...])
        a3 = xh * ga_ref[...] + ba_ref[...]
        s3 = _sig(a3)
        da_e = jnp.concatenate([jnp.zeros((CONV_HALO, w), F32), d_ref[:, 0:w], jnp.where(last, 0.0, dn_ref[...])], axis=0)
        da3 = da_e * (s3 * (1.0 + a3 * (1.0 - s3)))
        da2 = _ln_bwd(da3 * ga_ref[...], xh, rstd)
        dga_ref[...] += _rowsum(da3[tile] * xh[tile])
        dba_ref[...] += _rowsum(da3[tile])
        da2t = da2[tile]
        dcb_ref[...] += _rowsum(da2t)
        dcw_ref[taps:taps + 1, :] += _rowsum(da2t * a1_e[tile])
        da1 = cw_ref[taps:taps + 1, :] * da2
        for k in range(taps):
            sh = taps - k
            dcw_ref[k:k + 1, :] += _rowsum(da2t * pltpu.roll(a1_e, sh, 0)[tile])
            da1 = da1 + cw_ref[k:k + 1, :] * pltpu.roll(da2, ext - sh, 0)
        da1t = da1[tile]
        sgt = sg_e[tile]
        dh_ref[0] = (da1t * sgt).astype(BF16)
        dh_ref[1] = (da1t * h_ref[0] * sgt * (1.0 - sgt)).astype(BF16)

        bu = h_ref[2]
        bv = h_ref[3]
        u, tu = _gelu(bu)
        v1, tv = _gelu(bv)
        xh2, rstd2 = _ln_stats(v1)
        v2 = (xh2 * gb_ref[...] + bb_ref[...]).astype(BF16)
        db = d_ref[:, w:2 * w]
        dmx_all = db * u
        du_parts, dv2_parts = [], []
        for c in range(t // B_CHUNK):
            rs = slice(c * B_CHUNK, (c + 1) * B_CHUNK)
            du_row, dv2_row = [], []
            for g in range(groups):
                cs = slice(g * B_CHUNK, (g + 1) * B_CHUNK)
                v2cg = v2[rs, cs]
                mixed = jnp.dot(ws_ref[g], v2cg, preferred_element_type=F32) + sb_ref[g]
                dmx = dmx_all[rs, cs]
                dmxb = dmx.astype(BF16)
                du_row.append(db[rs, cs] * mixed)
                dv2_row.append(jnp.dot(wst_ref[g], dmxb, preferred_element_type=F32))
                dws_ref[g] += tril_ref[...] * lax.dot_general(dmxb, v2cg, (((1,), (1,)), ((), ())),
                                                               preferred_element_type=F32)
                dsb_ref[g] += jnp.sum(dmx, axis=1, keepdims=True)
            du_parts.append(jnp.concatenate(du_row, axis=1))
            dv2_parts.append(jnp.concatenate(dv2_row, axis=1))
        du = jnp.concatenate(du_parts, axis=0)
        dv2 = jnp.concatenate(dv2_parts, axis=0)
        dgb_ref[...] += _rowsum(dv2 * xh2)
        dbb_ref[...] += _rowsum(dv2)
        dv1 = _ln_bwd(dv2 * gb_ref[...], xh2, rstd2)
        dh_ref[2] = (du * _gelu_grad(bu, tu)).astype(BF16)
        dh_ref[3] = (dv1 * _gelu_grad(bv, tv)).astype(BF16)

    vec = pl.BlockSpec((1, w), lambda i: (0, 0))
    grp = pl.BlockSpec((groups, B_CHUNK, B_CHUNK), lambda i: (0, 0, 0))
    in_specs = [
        pl.BlockSpec((4, t, w), lambda i: (0, i, 0)),
        pl.BlockSpec((2, CONV_HALO, w), lambda i: (0, jnp.maximum(i * hb - 1, 0), 0)),
        pl.BlockSpec((2, CONV_HALO, w), lambda i: (0, jnp.minimum((i + 1) * hb, last_blk), 0)),
        pl.BlockSpec((t, 2 * w), lambda i: (i, 0)),
        pl.BlockSpec((CONV_HALO, w), lambda i: (jnp.minimum((i + 1) * hb, last_blk), 0)),
        pl.BlockSpec((A_KERNEL, w), lambda i: (0, 0)),
        vec, vec, vec, vec, vec, grp, grp, grp,
        pl.BlockSpec((B_CHUNK, B_CHUNK), lambda i: (0, 0)),
    ]
    vsds = jax.ShapeDtypeStruct((1, w), F32)
    out_specs = [
        pl.BlockSpec((4, t, w), lambda i: (0, i, 0)),
        pl.BlockSpec((A_KERNEL, w), lambda i: (0, 0)),
        vec, vec, vec, vec, vec, grp,
        pl.BlockSpec((groups, B_CHUNK, 1), lambda i: (0, 0, 0)),
    ]
    out_shape = [jax.ShapeDtypeStruct((4, rows, w), BF16), jax.ShapeDtypeStruct((A_KERNEL, w), F32),
                 vsds, vsds, vsds, vsds, vsds, jax.ShapeDtypeStruct((groups, B_CHUNK, B_CHUNK), F32),
                 jax.ShapeDtypeStruct((groups, B_CHUNK, 1), F32)]
    return pl.pallas_call(body, name=name, grid=(ni,), in_specs=in_specs, out_specs=out_specs, out_shape=out_shape,
                          compiler_params=_cp("arbitrary"))(h0, h0, h0, dab, dab, cw, cb, ga, ba, gb, bb, ws, wst, sbb, tril)


def _attn_mask(n):
    qi = lax.broadcasted_iota(jnp.int32, (ATT_BLOCK, 2 * ATT_BLOCK), 0)
    sj = lax.broadcasted_iota(jnp.int32, (ATT_BLOCK, 2 * ATT_BLOCK), 1)
    diff = qi + ATT_BLOCK - sj
    return (diff >= 0) & (diff < ATT_BLOCK) & ((n > 0) | (sj >= ATT_BLOCK))


def _attn_specs(rows, n_q):
    dq = n_q * HEAD_DIM
    dkv = 2 * (n_q // Q_PER_KV) * HEAD_DIM
    kv_blk = dq // dkv
    assert dq % dkv == 0
    return dq, dkv, [
        pl.BlockSpec(memory_space=pltpu.SMEM),
        pl.BlockSpec((ATT_BLOCK, dq), lambda n: (n, 0)),
        pl.BlockSpec((ATT_BLOCK, dkv), lambda n: (n, kv_blk)),
        pl.BlockSpec((ATT_BLOCK, dkv), lambda n: (jnp.maximum(n - 1, 0), kv_blk)),
    ]


def _kv_pair(kvc_ref, kvp_ref, kvh, n_kv):
    ks = slice(kvh * HEAD_DIM, (kvh + 1) * HEAD_DIM)
    vs = slice((n_kv + kvh) * HEAD_DIM, (n_kv + kvh + 1) * HEAD_DIM)
    kk = jnp.concatenate([kvp_ref[:, ks], kvc_ref[:, ks]], axis=0).astype(BF16)
    vv = jnp.concatenate([kvp_ref[:, vs], kvc_ref[:, vs]], axis=0).astype(BF16)
    return kk, vv


def _attn_fwd(qkv, sinks, *, name):
    rows = qkv.shape[0]
    n_q = sinks.shape[0]
    n_kv = n_q // Q_PER_KV
    scale = 1.0 / math.sqrt(HEAD_DIM)
    dq, _, in_specs = _attn_specs(rows, n_q)

    def body(sink_ref, q_ref, kvc_ref, kvp_ref, o_ref, lse_ref):
        valid = _attn_mask(pl.program_id(0))
        for kvh in range(n_kv):
            kk, vv = _kv_pair(kvc_ref, kvp_ref, kvh, n_kv)
            for g in range(Q_PER_KV):
                h = kvh * Q_PER_KV + g
                hs = slice(h * HEAD_DIM, (h + 1) * HEAD_DIM)
                s = lax.dot_general(q_ref[:, hs].astype(BF16), kk, (((1,), (1,)), ((), ())), preferred_element_type=F32)
                s = jnp.where(valid, s * scale, -jnp.inf)
                sk = sink_ref[h]
                m = jnp.maximum(jnp.max(s, axis=1, keepdims=True), sk)
                p = jnp.exp(s - m)
                l = jnp.sum(p, axis=1, keepdims=True) + jnp.exp(sk - m)
                o_ref[:, hs] = jnp.dot((p / l).astype(BF16), vv, preferred_element_type=F32)
                lse_ref[:, h:h + 1] = m + jnp.log(l)

    out_specs = [pl.BlockSpec((ATT_BLOCK, dq), lambda n: (n, 0)), pl.BlockSpec((ATT_BLOCK, n_q), lambda n: (n, 0))]
    out_shape = [jax.ShapeDtypeStruct((rows, dq), F32), jax.ShapeDtypeStruct((rows, n_q), F32)]
    return pl.pallas_call(body, name=name, grid=(rows // ATT_BLOCK,), in_specs=in_specs, out_specs=out_specs,
                          out_shape=out_shape, compiler_params=_cp("parallel"))(sinks, qkv, qkv, qkv)


def _attn_bwd(qkv, dout, lse, sinks, *, name):
    rows = qkv.shape[0]
    n_q = sinks.shape[0]
    n_kv = n_q // Q_PER_KV
    scale = 1.0 / math.sqrt(HEAD_DIM)
    dq_w, dkv_w, in_specs = _attn_specs(rows, n_q)
    blk_q = pl.BlockSpec((ATT_BLOCK, dq_w), lambda n: (n, 0))
    blk_kv = pl.BlockSpec((ATT_BLOCK, dkv_w), lambda n: (n, 0))
    in_specs = in_specs + [blk_q, pl.BlockSpec((ATT_BLOCK, n_q), lambda n: (n, 0))]

    def body(sink_ref, q_ref, kvc_ref, kvp_ref, do_ref, lse_ref, dq_ref, dkc_ref, dkp_ref, dsink_ref):
        n = pl.program_id(0)

        @pl.when(n == 0)
        def _():
            dsink_ref[...] = jnp.zeros_like(dsink_ref)

        valid = _attn_mask(n)
        head_ids = lax.broadcasted_iota(jnp.int32, (1, n_q), 1)
        dsink = jnp.zeros((1, n_q), F32)
        for kvh in range(n_kv):
            kk, vv = _kv_pair(kvc_ref, kvp_ref, kvh, n_kv)
            dk = jnp.zeros((2 * ATT_BLOCK, HEAD_DIM), F32)
            dv = jnp.zeros((2 * ATT_BLOCK, HEAD_DIM), F32)
            for g in range(Q_PER_KV):
                h = kvh * Q_PER_KV + g
                hs = slice(h * HEAD_DIM, (h + 1) * HEAD_DIM)
                qh = q_ref[:, hs].astype(BF16)
                s = lax.dot_general(qh, kk, (((1,), (1,)), ((), ())), preferred_element_type=F32)
                s = jnp.where(valid, s * scale, -jnp.inf)
                lse_h = lse_ref[:, h:h + 1]
                p = jnp.exp(s - lse_h)
                doh = do_ref[:, hs].astype(BF16)
                dp = lax.dot_general(doh, vv, (((1,), (1,)), ((), ())), preferred_element_type=F32)
                delta = jnp.sum(p * dp, axis=1, keepdims=True)
                ds = (p * (dp - delta) * scale).astype(BF16)
                dsink = dsink + jnp.where(head_ids == h, -jnp.sum(jnp.exp(sink_ref[h] - lse_h) * delta), 0.0)
                dq_ref[:, hs] = jnp.dot(ds, kk, preferred_element_type=F32)
                dk = dk + lax.dot_general(ds, qh, (((0,), (0,)), ((), ())), preferred_element_type=F32)
                dv = dv + lax.dot_general(p.astype(BF16), doh, (((0,), (0,)), ((), ())), preferred_element_type=F32)
            ks = slice(kvh * HEAD_DIM, (kvh + 1) * HEAD_DIM)
            vs = slice((n_kv + kvh) * HEAD_DIM, (n_kv + kvh + 1) * HEAD_DIM)
            dkp_ref[:, ks] = dk[0:ATT_BLOCK]
            dkc_ref[:, ks] = dk[ATT_BLOCK:]
            dkp_ref[:, vs] = dv[0:ATT_BLOCK]
            dkc_ref[:, vs] = dv[ATT_BLOCK:]
        dsink_ref[...] += dsink

    out_specs = [blk_q, blk_kv, blk_kv, pl.BlockSpec((1, n_q), lambda n: (0, 0))]
    out_shape = [jax.ShapeDtypeStruct((rows, dq_w), F32), jax.ShapeDtypeStruct((rows, dkv_w), F32),
                 jax.ShapeDtypeStruct((rows, dkv_w), F32), jax.ShapeDtypeStruct((1, n_q), F32)]
    return pl.pallas_call(body, name=name, grid=(rows // ATT_BLOCK,), in_specs=in_specs, out_specs=out_specs,
                          out_shape=out_shape, compiler_params=_cp("arbitrary"))(sinks, qkv, qkv, qkv, dout, lse)


def _dqkv_assemble(dq, dkc, dkp, *, name):
    rows, dq_w = dq.shape
    dkv_w = dkc.shape[1]
    nb = rows // ATT_BLOCK

    def body(dq_ref, dkc_ref, dkp_ref, o_ref, db_ref):
        n = pl.program_id(0)

        @pl.when(n == 0)
        def _():
            db_ref[...] = jnp.zeros_like(db_ref)

        dqv = dq_ref[...]
        dkv = dkc_ref[...] + jnp.where(n == nb - 1, 0.0, dkp_ref[...])
        o_ref[:, 0:dq_w] = dqv.astype(BF16)
        o_ref[:, dq_w:dq_w + dkv_w] = dkv.astype(BF16)
        db_ref[:, 0:dq_w] += _rowsum(dqv)
        db_ref[:, dq_w:dq_w + dkv_w] += _rowsum(dkv)

    width = dq_w + dkv_w
    in_specs = [pl.BlockSpec((ATT_BLOCK, dq_w), lambda n: (n, 0)), pl.BlockSpec((ATT_BLOCK, dkv_w), lambda n: (n, 0)),
                pl.BlockSpec((ATT_BLOCK, dkv_w), lambda n: (jnp.minimum(n + 1, nb - 1), 0))]
    out_specs = [pl.BlockSpec((ATT_BLOCK, width), lambda n: (n, 0)), pl.BlockSpec((1, width), lambda n: (0, 0))]
    out_shape = [jax.ShapeDtypeStruct((rows, width), BF16), jax.ShapeDtypeStruct((1, width), F32)]
    return pl.pallas_call(body, name=name, grid=(nb,), in_specs=in_specs, out_specs=out_specs, out_shape=out_shape,
                          compiler_params=_cp("arbitrary"))(dq, dkc, dkp)


def _row_tile(r, c):
    budget = 2 * 1024 * 1024 // (4 * c)
    for cand in (1024, 512, 256, 128, 64, 32, 16):
        if cand <= budget and r % cand == 0:
            return cand
    return r


def _pair_sum(g, recv, c_idx, *, name):
    _, _, r, c = g.shape
    t = _row_tile(r, c)

    def body(c_ref, g_ref, r_ref, o_ref):
        o_ref[...] = (g_ref[...] + r_ref[...]).astype(BF16)

    grid_spec = pltpu.PrefetchScalarGridSpec(
        num_scalar_prefetch=1, grid=(4, r // t),
        in_specs=[pl.BlockSpec((None, None, t, c), lambda q, i, cr: (q, cr[0], i, 0)),
                  pl.BlockSpec((None, t, c), lambda q, i, cr: (q, i, 0))],
        out_specs=pl.BlockSpec((None, t, c), lambda q, i, cr: (q, i, 0)))
    return pl.pallas_call(body, name=name, grid_spec=grid_spec, out_shape=jax.ShapeDtypeStruct((4, r, c), BF16),
                          compiler_params=_cp("parallel", "parallel"))(c_idx, g, recv)


def _quad_sum(own, recv, qc, dest, lead, *, name):
    _, r, c = own.shape
    t = _row_tile(r, c)
    lead_idx, buf_shape = lead

    def body(qc_ref, own_ref, r1_ref, r2_ref, r3_ref, *rest):
        o_ref = rest[-1]
        acc = own_ref[...].astype(F32) + r1_ref[...].astype(F32)
        acc = acc + r2_ref[...].astype(F32)
        o_ref[...] = acc + r3_ref[...].astype(F32)

    def slot(xor):
        return pl.BlockSpec((None, t, c), lambda i, qc_ref: (qc_ref[0] ^ xor, i, 0))

    if lead_idx is None:
        o_spec = pl.BlockSpec((None, t, c), lambda i, qc_ref: (qc_ref[1], i, 0))
    else:
        o_spec = pl.BlockSpec((None, None, t, c), lambda i, qc_ref: (lead_idx, qc_ref[1], i, 0))
    in_specs = [slot(0), slot(1), slot(2), slot(3)]
    args = [qc, own, recv, recv, recv]
    aliases = {}
    if dest is not None:
        in_specs.append(HBM)
        args.append(dest)
        aliases = {5: 0}
    grid_spec = pltpu.PrefetchScalarGridSpec(num_scalar_prefetch=1, grid=(r // t,), in_specs=in_specs, out_specs=o_spec)
    return pl.pallas_call(body, name=name, grid_spec=grid_spec, out_shape=jax.ShapeDtypeStruct(buf_shape, F32),
                          input_output_aliases=aliases, compiler_params=_cp("parallel"))(*args)


def _adamw(w, g, m, v, *, name):
    r, c = w.shape
    t = _row_tile(r, c)

    def body(w_ref, g_ref, m_ref, v_ref, d_ref, nm_ref, nv_ref):
        gv = g_ref[...]
        nm = ADAM_B1 * m_ref[...] + (1.0 - ADAM_B1) * gv
        nv = ADAM_B2 * v_ref[...] + (1.0 - ADAM_B2) * (gv * gv)
        m_hat = nm / (1.0 - ADAM_B1 ** ADAM_STEP)
        v_hat = nv / (1.0 - ADAM_B2 ** ADAM_STEP)
        d_ref[...] = -ADAM_LR * (m_hat / (jnp.sqrt(v_hat) + ADAM_EPS) + ADAM_WD * w_ref[...])
        nm_ref[...] = nm
        nv_ref[...] = nv

    blk = pl.BlockSpec((t, c), lambda i: (i, 0))
    sds = jax.ShapeDtypeStruct((r, c), F32)
    return pl.pallas_call(body, name=name, grid=(r // t,), in_specs=[blk] * 4, out_specs=[blk] * 3,
                          out_shape=[sds] * 3, compiler_params=_cp("parallel"))(w, g, m, v)


HBM = pl.BlockSpec(memory_space=pl.ANY)


def _place():
    x, y, c = lax.axis_index("x"), lax.axis_index("y"), lax.axis_index("c")
    chips = [(1 - x, y), (x, 1 - y), (1 - x, 1 - y)]
    return x, y, c, 2 * x + y, (x, y, 1 - c), chips


def _rcopy(src, dst, ssem, rsem, dev):
    return pltpu.make_async_remote_copy(src_ref=src, dst_ref=dst, send_sem=ssem, recv_sem=rsem, device_id=dev,
                                        device_id_type=MESH)


def _all_gather(bufs, *, name):
    nt = len(bufs)

    def body(*refs):
        outs = refs[nt:2 * nt]
        ssem, rsem = refs[2 * nt:]
        x, y, c, q, sib, chips = _place()
        sends = []
        for t in range(nt):
            for j, (px, py) in enumerate(chips):
                mine = outs[t].at[q, c]
                cp = _rcopy(mine, mine, ssem.at[t, j], rsem.at[t, j], (px, py, c))
                cp.start()
                sends.append(cp)
        for t in range(nt):
            for j, (px, py) in enumerate(chips):
                landed = outs[t].at[2 * px + py, c]
                _rcopy(landed, landed, ssem.at[t, j], rsem.at[t, j], (px, py, c)).wait_recv()
                cp = _rcopy(landed, landed, ssem.at[t, 3 + j], rsem.at[t, 3 + j], sib)
                cp.start()
                sends.append(cp)
        for t in range(nt):
            for j, (px, py) in enumerate(chips):
                passed = outs[t].at[2 * px + py, 1 - c]
                _rcopy(passed, passed, ssem.at[t, 3 + j], rsem.at[t, 3 + j], sib).wait_recv()
        for cp in sends:
            cp.wait_send()

    out_shape = [jax.ShapeDtypeStruct(b.shape, b.dtype) for b in bufs]
    return pl.pallas_call(
        body, name=name, in_specs=[HBM] * nt, out_specs=[HBM] * nt, out_shape=out_shape,
        input_output_aliases={t: t for t in range(nt)},
        scratch_shapes=[pltpu.SemaphoreType.DMA((nt, 6)), pltpu.SemaphoreType.DMA((nt, 6))],
    )(*bufs)


def _sibling_exchange(grads, *, name):
    nt = len(grads)

    def body(*refs):
        ins, outs = refs[:nt], refs[nt:2 * nt]
        ssem, rsem = refs[2 * nt:]
        x, y, c, q, sib, chips = _place()
        sends = []
        for t in range(nt):
            for k in range(4):
                cp = _rcopy(ins[t].at[k, 1 - c], outs[t].at[k], ssem.at[t, k], rsem.at[t, k], sib)
                cp.start()
                sends.append(cp)
        for t in range(nt):
            for k in range(4):
                _rcopy(ins[t].at[k, c], outs[t].at[k], ssem.at[t, k], rsem.at[t, k], sib).wait_recv()
        for cp in sends:
            cp.wait_send()

    out_shape = [jax.ShapeDtypeStruct((4,) + g.shape[2:], g.dtype) for g in grads]
    return pl.pallas_call(
        body, name=name, in_specs=[HBM] * nt, out_specs=[HBM] * nt, out_shape=out_shape,
        scratch_shapes=[pltpu.SemaphoreType.DMA((nt, 4)), pltpu.SemaphoreType.DMA((nt, 4))],
    )(*grads)


def _chip_exchange(partials, *, name):
    nt = len(partials)

    def body(*refs):
        ins, outs = refs[:nt], refs[nt:2 * nt]
        ssem, rsem = refs[2 * nt:]
        x, y, c, q, sib, chips = _place()
        sends = []
        for t in range(nt):
            for j, (px, py) in enumerate(chips):
                cp = _rcopy(ins[t].at[2 * px + py], outs[t].at[q], ssem.at[t, j], rsem.at[t, j], (px, py, c))
                cp.start()
                sends.append(cp)
        for t in range(nt):
            for j, (px, py) in enumerate(chips):
                slot = outs[t].at[2 * px + py]
                _rcopy(slot, slot, ssem.at[t, j], rsem.at[t, j], (px, py, c)).wait_recv()
        for cp in sends:
            cp.wait_send()

    out_shape = [jax.ShapeDtypeStruct(p.shape, p.dtype) for p in partials]
    return pl.pallas_call(
        body, name=name, in_specs=[HBM] * nt, out_specs=[HBM] * nt, out_shape=out_shape,
        scratch_shapes=[pltpu.SemaphoreType.DMA((nt, 3)), pltpu.SemaphoreType.DMA((nt, 3))],
    )(*partials)


def _sibling_share(bufs, layout, *, name):
    no = len(bufs)
    nt = len(layout)

    def body(*refs):
        outs = refs[no:2 * no]
        ssem, rsem = refs[2 * no:]
        x, y, c, q, sib, chips = _place()

        def slot(t, half):
            o, lead = layout[t]
            return outs[o].at[half] if lead is None else outs[o].at[lead, half]

        sends = []
        for t in range(nt):
            cp = _rcopy(slot(t, c), slot(t, c), ssem.at[t], rsem.at[t], sib)
            cp.start()
            sends.append(cp)
        for t in range(nt):
            _rcopy(slot(t, 1 - c), slot(t, 1 - c), ssem.at[t], rsem.at[t], sib).wait_recv()
        for cp in sends:
            cp.wait_send()

    out_shape = [jax.ShapeDtypeStruct(b.shape, b.dtype) for b in bufs]
    return pl.pallas_call(
        body, name=name, in_specs=[HBM] * no, out_specs=[HBM] * no, out_shape=out_shape,
        input_output_aliases={o: o for o in range(no)},
        scratch_shapes=[pltpu.SemaphoreType.DMA((nt,)), pltpu.SemaphoreType.DMA((nt,))],
    )(*bufs)


def _small_all_reduce(pack, *, name):
    r, lanes = pack.shape

    def body(in_ref, out_ref, gath, ssem, rsem):
        x, y, c = lax.axis_index("x"), lax.axis_index("y"), lax.axis_index("c")
        me = 4 * x + 2 * y + c
        gath[me] = in_ref[...]
        sends = []
        for mask in range(1, 8):
            px = 1 - x if mask & 4 else x
            py = 1 - y if mask & 2 else y
            pc = 1 - c if mask & 1 else c
            cp = _rcopy(in_ref, gath.at[me], ssem.at[mask - 1], rsem.at[mask - 1], (px, py, pc))
            cp.start()
            sends.append(cp)
        for mask in range(1, 8):
            px = 1 - x if mask & 4 else x
            py = 1 - y if mask & 2 else y
            pc = 1 - c if mask & 1 else c
            peer = gath.at[4 * px + 2 * py + pc]
            _rcopy(peer, peer, ssem.at[mask - 1], rsem.at[mask - 1], (px, py, pc)).wait_recv()
        acc = gath[0]
        for d in range(1, 8):
            acc = acc + gath[d]
        out_ref[...] = acc
        for cp in sends:
            cp.wait_send()

    vm = pl.BlockSpec(memory_space=pltpu.VMEM)
    return pl.pallas_call(
        body, name=name, in_specs=[vm], out_specs=vm, out_shape=jax.ShapeDtypeStruct((r, lanes), F32),
        scratch_shapes=[pltpu.VMEM((8, r, lanes), F32), pltpu.SemaphoreType.DMA((7,)), pltpu.SemaphoreType.DMA((7,))],
        compiler_params=pltpu.CompilerParams(vmem_limit_bytes=VMEM_LIMIT),
    )(pack)


def _pack(arrays, rows_multiple):
    flat = jnp.concatenate([a.reshape(-1) for a in arrays])
    rows = -(-flat.shape[0] // LANES)
    rows = -(-rows // rows_multiple) * rows_multiple
    flat = jnp.pad(flat, (0, rows * LANES - flat.shape[0]))
    return flat.reshape(rows, LANES)


def _unpack(buf, shapes):
    flat = buf.reshape(-1)
    out, pos = [], 0
    for s in shapes:
        n = math.prod(s)
        out.append(flat[pos:pos + n].reshape(s))
        pos += n
    return out


def _unshard_cols(stacked):
    moved = jnp.moveaxis(stacked, 0, -2)
    return moved.reshape(moved.shape[:-2] + (4 * stacked.shape[-1],))


def _shard_cols(full, q):
    n = full.shape[-1] // 4
    return lax.dynamic_slice_in_dim(full, q * n, n, axis=full.ndim - 1)


def kernel(x, ab_w_in, a_conv_w, a_conv_b, a_norm_g, a_norm_b, b_norm_g, b_norm_b, b_spatial_w, b_spatial_b, ab_w_out, c_w_qkv, c_b_qkv, c_sinks, c_w_o, ffn_w_up, ffn_conv_w, ffn_conv_b, ffn_w_down, ln_g, ln_b, loss_target, m_ab_w_in, m_a_conv_w, m_a_conv_b, m_a_norm_g, m_a_norm_b, m_b_norm_g, m_b_norm_b, m_b_spatial_w, m_b_spatial_b, m_ab_w_out, m_c_w_qkv, m_c_b_qkv, m_c_sinks, m_c_w_o, m_ffn_w_up, m_ffn_conv_w, m_ffn_conv_b, m_ffn_w_down, m_ln_g, m_ln_b, v_ab_w_in, v_a_conv_w, v_a_conv_b, v_a_norm_g, v_a_norm_b, v_b_norm_g, v_b_norm_b, v_b_spatial_w, v_b_spatial_b, v_ab_w_out, v_c_w_qkv, v_c_b_qkv, v_c_sinks, v_c_w_o, v_ffn_w_up, v_ffn_conv_w, v_ffn_conv_b, v_ffn_w_down, v_ln_g, v_ln_b):
    rows, d = x.shape[1], x.shape[2]
    depth = ln_g.shape[0]
    assert depth == 2 and x.shape[0] == 1
    alpha = (2.0 * depth) ** 0.25
    f = ffn_w_down.shape[1] * 4
    n_q = c_sinks.shape[1]
    q_idx = 2 * lax.axis_index("x") + lax.axis_index("y")
    c_idx = lax.axis_index("c")
    xs, tgt = x[0], loss_target[0]

    def own_slot(part):
        buf = lax.empty((4,) + part.shape, part.dtype)
        return lax.dynamic_update_slice(buf, part[None], (q_idx, 0, 0, 0))

    def halves(wm):
        return own_slot(wm.astype(BF16).reshape((2, wm.shape[0] // 2) + wm.shape[1:]))

    small_sharded = [a_conv_w[0], c_b_qkv[0], ffn_conv_w, ln_g, ln_b]
    small_pack = _pack(small_sharded, 16)
    bufs = [halves(ab_w_in[0]), halves(ab_w_out[0]), halves(ffn_w_up[0]), halves(ffn_w_down[0]),
            halves(c_w_qkv[0]), halves(c_w_o[0]), halves(ffn_w_up[1]), halves(ffn_w_down[1]),
            own_slot(small_pack.reshape(2, small_pack.shape[0] // 2, LANES))]
    gathered = _all_gather(bufs, name="gather_weights")

    def whole(g):
        return g.reshape(4, 2 * g.shape[2], g.shape[3])

    w_in = whole(gathered[0])
    w_out = whole(gathered[1]).reshape(-1, d)
    w_up = [whole(gathered[2]), whole(gathered[6])]
    w_down = [whole(gathered[3]).reshape(-1, d), whole(gathered[7]).reshape(-1, d)]
    w_qkv = _unshard_cols(whole(gathered[4]))
    w_o = whole(gathered[5]).reshape(-1, d)
    small_all = gathered[8].reshape(4, -1)
    sh_shapes = [s.shape for s in small_sharded]
    pieces, pos = [], 0
    for s in sh_shapes:
        n = math.prod(s)
        pieces.append(_unshard_cols(small_all[:, pos:pos + n].reshape((4,) + s)))
        pos += n
    conv_w_a, b_qkv, conv_w_f, ln_gf, ln_bf = pieces

    tril = jnp.tril(jnp.ones((B_CHUNK, B_CHUNK), F32))
    ws = (b_spatial_w[0] * tril).astype(BF16)
    wst = jnp.swapaxes(ws, 1, 2)
    sbb = jnp.broadcast_to(b_spatial_b[0][:, :, None], b_spatial_w[0].shape)
    mix_vecs = [a_conv_b, a_norm_g, a_norm_b, b_norm_g, b_norm_b]
    cw_f = [jnp.swapaxes(conv_w_f[l].reshape(3, 2, f), 0, 1) for l in range(depth)]
    cb_f = [ffn_conv_b[l].reshape(2, 1, f) for l in range(depth)]
    lng = lambda i, j: ln_gf[i, j].reshape(1, d)
    lnb = lambda i, j: ln_bf[i, j].reshape(1, d)
    sinks = c_sinks[0]

    def ffn_fwd(xin, l):
        hf, fact = _ffn_up_fwd(xin, w_up[l], cw_f[l], cb_f[l], name=f"ffn{l}_up")
        out = _matmul(fact, w_down[l], name=f"ffn{l}_down", tm=1024, tn=1024, tk=1408)
        return hf, fact, out

    h0 = _matmul(xs, w_in, name="mix_in", tm=1024, tn=512, tk=1024, out_stack=4)
    ab = _mixer_fwd(h0, conv_w_a, *mix_vecs, ws, sbb, name="mix_mid")
    mix = _matmul(ab, w_out, name="mix_out", tm=1024, tn=1024, tk=1024)
    x1 = _add_ln_fwd(xs, mix, lng(0, 0), lnb(0, 0), alpha, name="ln00")
    hf0, f0, ffn0 = ffn_fwd(x1, 0)
    x2 = _add_ln_fwd(x1, ffn0, lng(0, 1), lnb(0, 1), alpha, name="ln01")
    qkv = _matmul(x2, w_qkv, name="att_qkv", tm=1024, tn=w_qkv.shape[1], tk=1024, bias=b_qkv.reshape(1, -1))
    ao, lse = _attn_fwd(qkv, sinks, name="att_core")
    att = _matmul(ao, w_o, name="att_out", tm=1024, tn=1024, tk=1024)
    x3 = _add_ln_fwd(x2, att, lng(1, 0), lnb(1, 0), alpha, name="ln10")
    hf1, f1, ffn1 = ffn_fwd(x3, 1)
    x4 = _add_ln_fwd(x3, ffn1, lng(1, 1), lnb(1, 1), alpha, name="ln11")
    sq_err, dy = _loss_and_grad(x4, tgt, name="loss")

    def ffn_bwd(dz, xin, hf, fact, l):
        d_wdown = _matmul(fact, dz, name=f"ffn{l}_down_dw", ta=True, tm=1408, tn=1024, tk=512)
        dfa = _matmul(dz, w_down[l], name=f"ffn{l}_down_dx", tb=True, tm=1024, tn=1408, tk=1024)
        dx_parts, d_wup, dcw, dcb = _ffn_up_bwd(hf, dfa, xin, w_up[l], cw_f[l], cb_f[l], name=f"ffn{l}_up_bwd")
        return [(dx_parts, 1.0), (dz, alpha)], d_wup, d_wdown, dcw, dcb

    dz, dg11, db11 = _add_ln_bwd([(dy, 1.0)], x3, ffn1, lng(1, 1), alpha, name="ln11_bwd")
    dx3, d_wup1, d_wdown1, dcw1, dcb1 = ffn_bwd(dz, x3, hf1, f1, 1)
    dz, dg10, db10 = _add_ln_bwd(dx3, x2, att, lng(1, 0), alpha, name="ln10_bwd")
    d_wo = _matmul(ao, dz, name="att_out_dw", ta=True, tm=1024, tn=1024, tk=1024)
    dao = _matmul(dz, w_o, name="att_out_dx", tb=True, tm=1024, tn=1024, tk=1024)
    dq, dkc, dkp, d_sinks = _attn_bwd(qkv, dao, lse, sinks, name="att_core_bwd")
    dqkv, d_bqkv = _dqkv_assemble(dq, dkc, dkp, name="att_dqkv")
    d_wqkv = _matmul(x2, dqkv, name="att_qkv_dw", ta=True, tm=1024, tn=dqkv.shape[1], tk=1024)
    dx2 = _matmul(dqkv, w_qkv, name="att_qkv_dx", tb=True, tm=1024, tn=1024, tk=dqkv.shape[1], addend=(dz, alpha))
    dz, dg01, db01 = _add_ln_bwd([(dx2, 1.0)], x1, ffn0, lng(0, 1), alpha, name="ln01_bwd")
    dx1, d_wup0, d_wdown0, dcw0, dcb0 = ffn_bwd(dz, x1, hf0, f0, 0)
    dz, dg00, db00 = _add_ln_bwd(dx1, xs, mix, lng(0, 0), alpha, name="ln00_bwd")
    d_wout = _matmul(ab, dz, name="mix_out_dw", ta=True, tm=1024, tn=1024, tk=1024)
    dab = _matmul(dz, w_out, name="mix_out_dx", tb=True, tm=1024, tn=1024, tk=1024)
    dh0, d_cwa, d_cba, d_ga, d_ba, d_gb, d_bb, d_ws, d_sb = _mixer_bwd(
        h0, dab, conv_w_a, *mix_vecs, ws, wst, sbb, tril, name="mix_mid_bwd")
    d_win = _matmul(xs, dh0, name="mix_in_dw", ta=True, tm=1024, tn=512, tk=1024, out_stack=4)
    grad_x = _matmul(dh0, w_in, name="mix_in_dx", tb=True, tm=1024, tn=1024, tk=512, addend=(dz, alpha))

    def owner_view(g):
        if g.ndim == 3:
            return g.reshape(4, 2, g.shape[1] // 2, g.shape[2])
        return g.reshape(4, 2, g.shape[0] // 8, g.shape[1])

    d_wqkv_st = jnp.moveaxis(d_wqkv.reshape(d_wqkv.shape[0], 4, -1), 1, 0)
    big = [owner_view(g) for g in (d_win, d_wout, d_wqkv_st, d_wo, d_wup0, d_wup1, d_wdown0, d_wdown1)]
    from_sib = _sibling_exchange(big, name="reduce_sibling")
    c_arr = jnp.reshape(c_idx, (1,)).astype(jnp.int32)
    partials = [_pair_sum(g, r, c_arr, name=f"reduce_pair{t}") for t, (g, r) in enumerate(zip(big, from_sib))]
    from_chips = _chip_exchange(partials, name="reduce_chips")
    qc = jnp.stack([q_idx, c_idx]).astype(jnp.int32)
    layout = [(0, None), (1, None), (2, None), (3, None), (4, 0), (4, 1), (5, 0), (5, 1)]
    shard_bufs = [None] * 6
    for t, (o, lead) in enumerate(layout):
        piece = partials[t].shape[1:]
        shape = (2,) + piece if lead is None else (2, 2) + piece
        shard_bufs[o] = _quad_sum(partials[t], from_chips[t], qc, shard_bufs[o], (lead, shape), name=f"reduce_quad{t}")
    shared = _sibling_share(shard_bufs, layout, name="reduce_share")
    g_win = shared[0].reshape(ab_w_in.shape)
    g_wout = shared[1].reshape(ab_w_out.shape)
    g_wqkv = shared[2].reshape(c_w_qkv.shape)
    g_wo = shared[3].reshape(c_w_o.shape)
    g_wup = shared[4].reshape(ffn_w_up.shape)
    g_wdown = shared[5].reshape(ffn_w_down.shape)

    d_cw_f = jnp.stack([jnp.swapaxes(dcw0, 0, 1).reshape(3, 2 * f), jnp.swapaxes(dcw1, 0, 1).reshape(3, 2 * f)])
    d_cb_f = jnp.stack([dcb0.reshape(2 * f), dcb1.reshape(2 * f)])
    d_lng = jnp.stack([jnp.stack([dg00[0], dg01[0]]), jnp.stack([dg10[0], dg11[0]])])
    d_lnb = jnp.stack([jnp.stack([db00[0], db01[0]]), jnp.stack([db10[0], db11[0]])])
    small_full = [d_cwa, d_cba, d_ga, d_ba, d_gb, d_bb, d_ws, d_sb, d_bqkv, d_sinks, d_cw_f, d_cb_f, d_lng, d_lnb, sq_err]
    reduced = _small_all_reduce(_pack(small_full, 8), name="reduce_small")
    (r_cwa, r_cba, r_ga, r_ba, r_gb, r_bb, r_ws, r_sb, r_bqkv, r_sinks, r_cwf, r_cbf, r_lng, r_lnb, r_err) = _unpack(
        reduced, [a.shape for a in small_full])
    loss = 0.5 * jnp.sum(r_err) / d

    small_names_w = [a_conv_w, a_conv_b, a_norm_g, a_norm_b, b_norm_g, b_norm_b, b_spatial_w, b_spatial_b, c_b_qkv,
                     c_sinks, ffn_conv_w, ffn_conv_b, ln_g, ln_b]
    small_m = [m_a_conv_w, m_a_conv_b, m_a_norm_g, m_a_norm_b, m_b_norm_g, m_b_norm_b, m_b_spatial_w, m_b_spatial_b,
               m_c_b_qkv, m_c_sinks, m_ffn_conv_w, m_ffn_conv_b, m_ln_g, m_ln_b]
    small_v = [v_a_conv_w, v_a_conv_b, v_a_norm_g, v_a_norm_b, v_b_norm_g, v_b_norm_b, v_b_spatial_w, v_b_spatial_b,
               v_c_b_qkv, v_c_sinks, v_ffn_conv_w, v_ffn_conv_b, v_ln_g, v_ln_b]
    small_g = [_shard_cols(r_cwa, q_idx), r_cba, r_ga, r_ba, r_gb, r_bb, r_ws, r_sb, _shard_cols(r_bqkv, q_idx), r_sinks,
               _shard_cols(r_cwf, q_idx), r_cbf, _shard_cols(r_lng, q_idx), _shard_cols(r_lnb, q_idx)]
    small_g = [g.reshape(w.shape) for g, w in zip(small_g, small_names_w)]
    sm_shapes = [w.shape for w in small_names_w]
    sm_delta, sm_m, sm_v = _adamw(_pack(small_names_w, 8), _pack(small_g, 8), _pack(small_m, 8), _pack(small_v, 8),
                                  name="adamw_small")
    sm_delta, sm_m, sm_v = _unpack(sm_delta, sm_shapes), _unpack(sm_m, sm_shapes), _unpack(sm_v, sm_shapes)

    def adamw_big(w, g, m, v, name):
        two_d = lambda a: a.reshape(-1, a.shape[-1])
        outs = _adamw(two_d(w), two_d(g), two_d(m), two_d(v), name=name)
        return [o.reshape(w.shape) for o in outs]

    big_w = [ab_w_in, ab_w_out, c_w_qkv, c_w_o, ffn_w_up, ffn_w_down]
    big_g = [g_win, g_wout, g_wqkv, g_wo, g_wup, g_wdown]
    big_m = [m_ab_w_in, m_ab_w_out, m_c_w_qkv, m_c_w_o, m_ffn_w_up, m_ffn_w_down]
    big_v = [v_ab_w_in, v_ab_w_out, v_c_w_qkv, v_c_w_o, v_ffn_w_up, v_ffn_w_down]
    big_out = [adamw_big(w, g, m, v, f"adamw_big{t}") for t, (w, g, m, v) in enumerate(zip(big_w, big_g, big_m, big_v))]

    order_big = {0: 0, 9: 1, 10: 2, 13: 3, 14: 4, 17: 5}
    order_small = {1: 0, 2: 1, 3: 2, 4: 3, 5: 4, 6: 5, 7: 6, 8: 7, 11: 8, 12: 9, 15: 10, 16: 11, 18: 12, 19: 13}
    grads, deltas, new_m, new_v = [], [], [], []
    for pos_w in range(20):
        if pos_w in order_big:
            t = order_big[pos_w]
            grads.append(big_g[t])
            deltas.append(big_out[t][0])
            new_m.append(big_out[t][1])
            new_v.append(big_out[t][2])
        else:
            t = order_small[pos_w]
            grads.append(small_g[t])
            deltas.append(sm_delta[t])
            new_m.append(sm_m[t])
            new_v.append(sm_v[t])
    return (loss, grad_x[None], *grads, *deltas, *new_m, *new_v)
```

```python
import functools
import math

import jax
import jax.numpy as jnp
from jax import lax
from jax.experimental import pallas as pl
from jax.experimental.pallas import tpu as pltpu

F32 = jnp.float32
BF16 = jnp.bfloat16
MESH = pl.DeviceIdType.MESH

LN_EPS = 1e-5
HEAD_DIM = 64
ATT_BLOCK = 128
Q_PER_KV = 8
A_KERNEL = 31
CONV_HALO = 32
FFN_HALO = 8
B_CHUNK = 128
LANES = 128
MXU_WIDTH = 256
GELU_C = math.sqrt(2.0 / math.pi)
ADAM_LR = 0.001
ADAM_B1 = 0.9
ADAM_B2 = 0.999
ADAM_EPS = 1e-08
ADAM_WD = 0.01
ADAM_STEP = 10
VMEM_LIMIT = 56 * 1024 * 1024


def _cp(*dims):
    return pltpu.CompilerParams(dimension_semantics=dims, vmem_limit_bytes=VMEM_LIMIT)


def _pick(n, prefs):
    for p in prefs:
        if n % p == 0:
            return p
    return n


def _sig(x):
    return 1.0 / (1.0 + jnp.exp(-x))


def _gelu(x):
    t = jnp.tanh(GELU_C * (x + 0.044715 * (x * x * x)))
    return x * (0.5 * (1.0 + t)), t


def _gelu_grad(x, t):
    return 0.5 * (1.0 + t) + 0.5 * x * (1.0 - t * t) * (GELU_C * (1.0 + 3.0 * 0.044715 * x * x))


def _ln_stats(z):
    mu = jnp.mean(z, axis=-1, keepdims=True)
    zc = z - mu
    var = jnp.mean(zc * zc, axis=-1, keepdims=True)
    rstd = lax.rsqrt(var + LN_EPS)
    return zc * rstd, rstd


def _ln_bwd(dxh, xh, rstd):
    return rstd * (dxh - jnp.mean(dxh, axis=-1, keepdims=True) - xh * jnp.mean(dxh * xh, axis=-1, keepdims=True))


def _rowsum(a):
    return jnp.sum(a, axis=0, keepdims=True)


def _lshape(a):
    return (a.shape[0], a.shape[1]) if a.ndim == 2 else (a.shape[1], a.shape[0] * a.shape[2])


def _spec2(arr, blk_r, blk_c, ridx, cidx):
    if len(arr.shape) == 2:
        return pl.BlockSpec((blk_r, blk_c), lambda i, j, k: (ridx(i, j, k), cidx(i, j, k)))
    per = arr.shape[2] // blk_c
    assert arr.shape[2] % blk_c == 0
    return pl.BlockSpec((None, blk_r, blk_c), lambda i, j, k: (cidx(i, j, k) // per, ridx(i, j, k), cidx(i, j, k) % per))


def _matmul(a, b, *, name, ta=False, tb=False, tm, tn, tk, out_dtype=F32, out_stack=None, bias=None, addend=None):
    ar, ac = _lshape(a)
    br, bc = _lshape(b)
    m, kdim = (ac, ar) if ta else (ar, ac)
    n = br if tb else bc
    assert (bc if tb else br) == kdim
    tm, tn, tk = min(tm, m), min(tn, n), min(tk, kdim)
    assert m % tm == 0 and n % tn == 0 and kdim % tk == 0, (name, m, n, kdim, tm, tn, tk)
    nk = kdim // tk
    gi, gj, gk = (lambda i, j, k: i), (lambda i, j, k: j), (lambda i, j, k: k)
    a_spec = _spec2(a, tk, tm, gk, gi) if ta else _spec2(a, tm, tk, gi, gk)
    b_spec = _spec2(b, tn, tk, gj, gk) if tb else _spec2(b, tk, tn, gk, gj)
    if out_stack is None:
        out_sds = jax.ShapeDtypeStruct((m, n), out_dtype)
    else:
        out_sds = jax.ShapeDtypeStruct((out_stack, m, n // out_stack), out_dtype)
    o_spec = _spec2(out_sds, tm, tn, gi, gj)
    in_specs = [a_spec, b_spec]
    args = [a, b]
    if bias is not None:
        in_specs.append(pl.BlockSpec((1, tn), lambda i, j, k: (0, j)))
        args.append(bias)
    scale = None
    if addend is not None:
        add_arr, scale = addend
        in_specs.append(pl.BlockSpec((tm, tn), lambda i, j, k: (i, j)))
        args.append(add_arr)
    use_acc = nk > 1 and out_dtype != F32
    dn = (((0 if ta else 1,), (1 if tb else 0,)), ((), ()))

    def body(*refs):
        a_ref, b_ref = refs[0], refs[1]
        pos = 2
        bias_ref = add_ref = None
        if bias is not None:
            bias_ref = refs[pos]
            pos += 1
        if addend is not None:
            add_ref = refs[pos]
            pos += 1
        o_ref = refs[pos]
        acc_ref = refs[pos + 1] if use_acc else o_ref
        p = lax.dot_general(a_ref[...].astype(BF16), b_ref[...].astype(BF16), dn, preferred_element_type=F32)

        def finish(val):
            if bias_ref is not None:
                val = val + bias_ref[...]
            if add_ref is not None:
                val = val + scale * add_ref[...]
            return val.astype(out_dtype)

        if nk == 1:
            o_ref[...] = finish(p)
        else:
            k = pl.program_id(2)

            @pl.when(k == 0)
            def _():
                acc_ref[...] = p

            @pl.when(k > 0)
            def _():
                acc_ref[...] += p

            if use_acc or bias_ref is not None or add_ref is not None:
                @pl.when(k == nk - 1)
                def _():
                    o_ref[...] = finish(acc_ref[...])

    return pl.pallas_call(
        body, name=name, grid=(m // tm, n // tn, nk), in_specs=in_specs, out_specs=o_spec, out_shape=out_sds,
        scratch_shapes=[pltpu.VMEM((tm, tn), F32)] if use_acc else [],
        compiler_params=_cp("parallel", "parallel", "arbitrary"),
    )(*args)


def _add_ln_fwd(x, s, g, b, alpha, *, name):
    rows, d = x.shape
    t = _pick(rows, (512, 256))

    def body(x_ref, s_ref, g_ref, b_ref, y_ref):
        xh, _ = _ln_stats(alpha * x_ref[...] + s_ref[...])
        y_ref[...] = xh * g_ref[...] + b_ref[...]

    row = pl.BlockSpec((t, d), lambda i: (i, 0))
    vec = pl.BlockSpec((1, d), lambda i: (0, 0))
    return pl.pallas_call(body, name=name, grid=(rows // t,), in_specs=[row, row, vec, vec], out_specs=row,
                          out_shape=jax.ShapeDtypeStruct((rows, d), F32), compiler_params=_cp("parallel"))(x, s, g, b)


def _add_ln_bwd(dy_terms, x, s, g, alpha, *, name):
    rows, d = x.shape
    t = _pick(rows, (512, 256))
    nterm = len(dy_terms)
    scales = [sc for _, sc in dy_terms]
    ranks = [a.ndim for a, _ in dy_terms]

    def body(*refs):
        dy_refs = refs[:nterm]
        x_ref, s_ref, g_ref, dz_ref, dg_ref, db_ref = refs[nterm:]

        @pl.when(pl.program_id(0) == 0)
        def _():
            dg_ref[...] = jnp.zeros_like(dg_ref)
            db_ref[...] = jnp.zeros_like(db_ref)

        dyv = None
        for r, sc, rank in zip(dy_refs, scales, ranks):
            slabs = [r[...]] if rank == 2 else [r[p] for p in range(r.shape[0])]
            for v in slabs:
                v = v if sc == 1.0 else sc * v
                dyv = v if dyv is None else dyv + v
        xh, rstd = _ln_stats(alpha * x_ref[...] + s_ref[...])
        dz_ref[...] = _ln_bwd(dyv * g_ref[...], xh, rstd)
        dg_ref[...] += _rowsum(dyv * xh)
        db_ref[...] += _rowsum(dyv)

    row = pl.BlockSpec((t, d), lambda i: (i, 0))
    vec = pl.BlockSpec((1, d), lambda i: (0, 0))
    vsds = jax.ShapeDtypeStruct((1, d), F32)
    dy_specs = [row if a.ndim == 2 else pl.BlockSpec((a.shape[0], t, d), lambda i: (0, i, 0)) for a, _ in dy_terms]
    return pl.pallas_call(body, name=name, grid=(rows // t,), in_specs=dy_specs + [row, row, vec], out_specs=[row, vec, vec],
                          out_shape=[jax.ShapeDtypeStruct((rows, d), F32), vsds, vsds],
                          compiler_params=_cp("arbitrary"))(*[a for a, _ in dy_terms], x, s, g)


def _loss_and_grad(y, tgt, *, name):
    rows, d = y.shape
    t = _pick(rows, (512, 256))

    def body(y_ref, t_ref, l_ref, dy_ref):
        @pl.when(pl.program_id(0) == 0)
        def _():
            l_ref[...] = jnp.zeros_like(l_ref)

        e = y_ref[...] - t_ref[...]
        l_ref[...] += _rowsum(e * e)
        dy_ref[...] = e * (1.0 / d)

    row = pl.BlockSpec((t, d), lambda i: (i, 0))
    vec = pl.BlockSpec((1, d), lambda i: (0, 0))
    return pl.pallas_call(body, name=name, grid=(rows // t,), in_specs=[row, row], out_specs=[vec, row],
                          out_shape=[jax.ShapeDtypeStruct((1, d), F32), jax.ShapeDtypeStruct((rows, d), F32)],
                          compiler_params=_cp("arbitrary"))(y, tgt)


def _col_blocks(width):
    out, pos = [], 0
    while pos < width:
        w = MXU_WIDTH if width - pos >= MXU_WIDTH else width - pos
        out.append(slice(pos, pos + w))
        pos += w
    return out


def _conv3(e, w, b):
    r1 = pltpu.roll(e, 1, 0)
    r2 = pltpu.roll(e, 2, 0)
    return w[0:1, :] * r2 + w[1:2, :] * r1 + w[2:3, :] * e + b, (r2, r1, e)


def _ffn_up_fwd(x, w_up, cw, cb, *, name):
    rows, d = x.shape
    nq, _, tc = w_up.shape
    nj = nq // 2
    f = tc * nj
    tm = _pick(rows, (512, 256))
    blocks = _col_blocks(tc)

    def body(x_ref, wg_ref, wv_ref, cw_ref, cb_ref, hf_ref, f_ref, prev_ref):
        @pl.when(pl.program_id(1) == 0)
        def _():
            prev_ref[...] = jnp.zeros_like(prev_ref)

        xb = x_ref[...].astype(BF16)
        for cs in blocks:
            hc = []
            for s, w_ref in ((0, wg_ref), (1, wv_ref)):
                h = jnp.dot(xb, w_ref[:, cs], preferred_element_type=F32)
                hf_ref[s, :, cs] = h
                e = jnp.concatenate([prev_ref[s, :, cs], h], axis=0)
                prev_ref[s, :, cs] = h[tm - FFN_HALO:]
                y, _ = _conv3(e, cw_ref[s, :, cs], cb_ref[s, :, cs])
                hc.append(y[FFN_HALO:])
            gl, _ = _gelu(hc[0])
            f_ref[:, cs] = (gl * hc[1]).astype(BF16)

    in_specs = [
        pl.BlockSpec((tm, d), lambda j, i: (i, 0)),
        pl.BlockSpec((None, d, tc), lambda j, i: (j, 0, 0)),
        pl.BlockSpec((None, d, tc), lambda j, i: (nj + j, 0, 0)),
        pl.BlockSpec((2, 3, tc), lambda j, i: (0, 0, j)),
        pl.BlockSpec((2, 1, tc), lambda j, i: (0, 0, j)),
    ]
    out_specs = [pl.BlockSpec((2, tm, tc), lambda j, i: (0, i, j)), pl.BlockSpec((tm, tc), lambda j, i: (i, j))]
    out_shape = [jax.ShapeDtypeStruct((2, rows, f), F32), jax.ShapeDtypeStruct((rows, f), BF16)]
    return pl.pallas_call(body, name=name, grid=(nj, rows // tm), in_specs=in_specs, out_specs=out_specs, out_shape=out_shape,
                          scratch_shapes=[pltpu.VMEM((2, FFN_HALO, tc), F32)],
                          compiler_params=_cp("parallel", "arbitrary"))(x, w_up, w_up, cw, cb)


def _ffn_up_bwd(hf, df, x, w_up, cw, cb, *, name):
    _, rows, f = hf.shape
    d = x.shape[1]
    nq, _, tc = w_up.shape
    nj = nq // 2
    tm = _pick(rows, (512, 256))
    hb = tm // FFN_HALO
    once = pl.Buffered(1)
    ni = rows // tm
    last_blk = rows // FFN_HALO - 1
    ext = tm + 2 * FFN_HALO
    tile = slice(FFN_HALO, FFN_HALO + tm)
    blocks = _col_blocks(tc)

    def body(h_ref, hp_ref, hn_ref, d_ref, dn_ref, x_ref, wg_ref, wv_ref, cw_ref, cb_ref, dx_ref, dw_ref, dcw_ref, dcb_ref):
        i = pl.program_id(1)
        first = i == 0
        last = i == ni - 1

        @pl.when(first)
        def _():
            dw_ref[...] = jnp.zeros_like(dw_ref)
            dcw_ref[...] = jnp.zeros_like(dcw_ref)
            dcb_ref[...] = jnp.zeros_like(dcb_ref)

        xt = x_ref[...].astype(BF16).T
        dx = None
        for cs in blocks:
            wc = cs.stop - cs.start
            de = jnp.concatenate([jnp.zeros((FFN_HALO, wc), F32), d_ref[:, cs], jnp.where(last, 0.0, dn_ref[:, cs])], axis=0)
            taps, hc = [], []
            for s in range(2):
                e = jnp.concatenate([jnp.where(first, 0.0, hp_ref[s, :, cs]), h_ref[s, :, cs], hn_ref[s, :, cs]], axis=0)
                y, tp = _conv3(e, cw_ref[s, :, cs], cb_ref[s, :, cs])
                hc.append(y)
                taps.append(tp)
            gl, th = _gelu(hc[0])
            dhc = (de * hc[1] * _gelu_grad(hc[0], th), de * gl)
            for s, w_ref in ((0, wg_ref), (1, wv_ref)):
                w = cw_ref[s, :, cs]
                g = dhc[s]
                dh = (w[2:3, :] * g + w[1:2, :] * pltpu.roll(g, ext - 1, 0) + w[0:1, :] * pltpu.roll(g, ext - 2, 0))[tile]
                gt = g[tile]
                for k in range(3):
                    dcw_ref[s, k:k + 1, cs] += _rowsum(gt * taps[s][k][tile])
                dcb_ref[s, :, cs] += _rowsum(gt)
                dhb = dh.astype(BF16)
                part = lax.dot_general(dhb, w_ref[:, cs], (((1,), (1,)), ((), ())), preferred_element_type=F32)
                dx = part if dx is None else dx + part
                dw_ref[s, :, cs] += jnp.dot(xt, dhb, preferred_element_type=F32)
        dx_ref[...] = dx

    in_specs = [
        pl.BlockSpec((2, tm, tc), lambda j, i: (0, i, j)),
        pl.BlockSpec((2, FFN_HALO, tc), lambda j, i: (0, jnp.maximum(i * hb - 1, 0), j)),
        pl.BlockSpec((2, FFN_HALO, tc), lambda j, i: (0, jnp.minimum((i + 1) * hb, last_blk), j)),
        pl.BlockSpec((tm, tc), lambda j, i: (i, j)),
        pl.BlockSpec((FFN_HALO, tc), lambda j, i: (jnp.minimum((i + 1) * hb, last_blk), j)),
        pl.BlockSpec((tm, d), lambda j, i: (i, 0)),
        pl.BlockSpec((None, d, tc), lambda j, i: (j, 0, 0), pipeline_mode=once),
        pl.BlockSpec((None, d, tc), lambda j, i: (nj + j, 0, 0), pipeline_mode=once),
        pl.BlockSpec((2, 3, tc), lambda j, i: (0, 0, j)),
        pl.BlockSpec((2, 1, tc), lambda j, i: (0, 0, j)),
    ]
    out_specs = [
        pl.BlockSpec((None, tm, d), lambda j, i: (j, i, 0)),
        pl.BlockSpec((2, None, d, tc), lambda j, i: (0, j, 0, 0), pipeline_mode=once),
        pl.BlockSpec((2, 3, tc), lambda j, i: (0, 0, j)),
        pl.BlockSpec((2, 1, tc), lambda j, i: (0, 0, j)),
    ]
    out_shape = [jax.ShapeDtypeStruct((nj, rows, d), F32), jax.ShapeDtypeStruct((2, nj, d, tc), F32),
                 jax.ShapeDtypeStruct((2, 3, f), F32), jax.ShapeDtypeStruct((2, 1, f), F32)]
    dx, dw, dcw, dcb = pl.pallas_call(body, name=name, grid=(nj, ni), in_specs=in_specs, out_specs=out_specs,
                                      out_shape=out_shape, compiler_params=_cp("parallel", "arbitrary"))(
        hf, hf, hf, df, df, x, w_up, w_up, cw, cb)
    return dx, dw.reshape(nq, d, tc), dcw, dcb


def _mixer_fwd(h0, cw, cb, ga, ba, gb, bb, ws, sbb, *, name):
    _, rows, w = h0.shape
    t = _pick(rows, (256,))
    hb = t // CONV_HALO
    groups = w // B_CHUNK

    def body(h_ref, hp_ref, cw_ref, cb_ref, ga_ref, ba_ref, gb_ref, bb_ref, ws_ref, sb_ref, o_ref):
        first = pl.program_id(0) == 0
        a1 = h_ref[0] * _sig(h_ref[1])
        a1p = jnp.where(first, 0.0, hp_ref[0] * _sig(hp_ref[1]))
        e = jnp.concatenate([a1p, a1], axis=0)
        acc = cw_ref[A_KERNEL - 1:A_KERNEL, :] * e
        for k in range(A_KERNEL - 1):
            acc = acc + cw_ref[k:k + 1, :] * pltpu.roll(e, A_KERNEL - 1 - k, 0)
        xh, _ = _ln_stats(acc[CONV_HALO:] + cb_ref[...])
        a3 = xh * ga_ref[...] + ba_ref[...]
        o_ref[:, 0:w] = (a3 * _sig(a3)).astype(BF16)

        u, _ = _gelu(h_ref[2])
        v1, _ = _gelu(h_ref[3])
        xh2, _ = _ln_stats(v1)
        v2 = (xh2 * gb_ref[...] + bb_ref[...]).astype(BF16)
        for c in range(t // B_CHUNK):
            rs = slice(c * B_CHUNK, (c + 1) * B_CHUNK)
            for g in range(groups):
                cs = slice(g * B_CHUNK, (g + 1) * B_CHUNK)
                mixed = jnp.dot(ws_ref[g], v2[rs, cs], preferred_element_type=F32) + sb_ref[g]
                o_ref[rs, w + g * B_CHUNK:w + (g + 1) * B_CHUNK] = (u[rs, cs] * mixed).astype(BF16)

    vec = pl.BlockSpec((1, w), lambda i: (0, 0))
    grp = pl.BlockSpec((groups, B_CHUNK, B_CHUNK), lambda i: (0, 0, 0))
    in_specs = [
        pl.BlockSpec((4, t, w), lambda i: (0, i, 0)),
        pl.BlockSpec((2, CONV_HALO, w), lambda i: (0, jnp.maximum(i * hb - 1, 0), 0)),
        pl.BlockSpec((A_KERNEL, w), lambda i: (0, 0)),
        vec, vec, vec, vec, vec, grp, grp,
    ]
    return pl.pallas_call(body, name=name, grid=(rows // t,), in_specs=in_specs,
                          out_specs=pl.BlockSpec((t, 2 * w), lambda i: (i, 0)),
                          out_shape=jax.ShapeDtypeStruct((rows, 2 * w), BF16),
                          compiler_params=_cp("parallel"))(h0, h0, cw, cb, ga, ba, gb, bb, ws, sbb)


def _mixer_bwd(h0, dab, cw, cb, ga, ba, gb, bb, ws, wst, sbb, tril, *, name):
    _, rows, w = h0.shape
    t = _pick(rows, (256,))
    hb = t // CONV_HALO
    ni = rows // t
    last_blk = rows // CONV_HALO - 1
    ext = t + 2 * CONV_HALO
    tile = slice(CONV_HALO, CONV_HALO + t)
    groups = w // B_CHUNK
    taps = A_KERNEL - 1

    def body(h_ref, hp_ref, hn_ref, d_ref, dn_ref, cw_ref, cb_ref, ga_ref, ba_ref, gb_ref, bb_ref, ws_ref, wst_ref,
             sb_ref, tril_ref, dh_ref, dcw_ref, dcb_ref, dga_ref, dba_ref, dgb_ref, dbb_ref, dws_ref, dsb_ref):
        i = pl.program_id(0)
        first = i == 0
        last = i == ni - 1

        @pl.when(first)
        def _():
            for r in (dcw_ref, dcb_ref, dga_ref, dba_ref, dgb_ref, dbb_ref, dws_ref, dsb_ref):
                r[...] = jnp.zeros_like(r)

        av_e = jnp.concatenate([hp_ref[0], h_ref[0], hn_ref[0]], axis=0)
        sg_e = _sig(jnp.concatenate([hp_ref[1], h_ref[1], hn_ref[1]], axis=0))
        rows_e = lax.broadcasted_iota(jnp.int32, (ext, 1), 0)
        a1_e = jnp.where(first & (rows_e < CONV_HALO), 0.0, av_e * sg_e)
        acc = cw_ref[taps:taps + 1, :] * a1_e
        for k in range(taps):
            acc = acc + cw_ref[k:k + 1, :] * pltpu.roll(a1_e, taps - k, 0)
        xh, rstd = _ln_stats(acc + cb_ref[...])
        a3 = xh * ga_ref[...] + ba_ref[...]
        s3 = _sig(a3)
        da_e = jnp.concatenate([jnp.zeros((CONV_HALO, w), F32), d_ref[:, 0:w], jnp.where(last, 0.0, dn_ref[...])], axis=0)
        da3 = da_e * (s3 * (1.0 + a3 * (1.0 - s3)))
        da2 = _ln_bwd(da3 * ga_ref[...], xh, rstd)
        dga_ref[...] += _rowsum(da3[tile] * xh[tile])
        dba_ref[...] += _rowsum(da3[tile])
        da2t = da2[tile]
        dcb_ref[...] += _rowsum(da2t)
        dcw_ref[taps:taps + 1, :] += _rowsum(da2t * a1_e[tile])
        da1 = cw_ref[taps:taps + 1, :] * da2
        for k in range(taps):
            sh = taps - k
            dcw_ref[k:k + 1, :] += _rowsum(da2t * pltpu.roll(a1_e, sh, 0)[tile])
            da1 = da1 + cw_ref[k:k + 1, :] * pltpu.roll(da2, ext - sh, 0)
        da1t = da1[tile]
        sgt = sg_e[tile]
        dh_ref[0] = (da1t * sgt).astype(BF16)
        dh_ref[1] = (da1t * h_ref[0] * sgt * (1.0 - sgt)).astype(BF16)

        bu = h_ref[2]
        bv = h_ref[3]
        u, tu = _gelu(bu)
        v1, tv = _gelu(bv)
        xh2, rstd2 = _ln_stats(v1)
        v2 = (xh2 * gb_ref[...] + bb_ref[...]).astype(BF16)
        db = d_ref[:, w:2 * w]
        dmx_all = db * u
        du_parts, dv2_parts = [], []
        for c in range(t // B_CHUNK):
            rs = slice(c * B_CHUNK, (c + 1) * B_CHUNK)
            du_row, dv2_row = [], []
            for g in range(groups):
                cs = slice(g * B_CHUNK, (g + 1) * B_CHUNK)
                v2cg = v2[rs, cs]
                mixed = jnp.dot(ws_ref[g], v2cg, preferred_element_type=F32) + sb_ref[g]
                dmx = dmx_all[rs, cs]
                dmxb = dmx.astype(BF16)
                du_row.append(db[rs, cs] * mixed)
                dv2_row.append(jnp.dot(wst_ref[g], dmxb, preferred_element_type=F32))
                dws_ref[g] += tril_ref[...] * lax.dot_general(dmxb, v2cg, (((1,), (1,)), ((), ())),
                                                               preferred_element_type=F32)
                dsb_ref[g] += jnp.sum(dmx, axis=1, keepdims=True)
            du_parts.append(jnp.concatenate(du_row, axis=1))
            dv2_parts.append(jnp.concatenate(dv2_row, axis=1))
        du = jnp.concatenate(du_parts, axis=0)
        dv2 = jnp.concatenate(dv2_parts, axis=0)
        dgb_ref[...] += _rowsum(dv2 * xh2)
        dbb_ref[...] += _rowsum(dv2)
        dv1 = _ln_bwd(dv2 * gb_ref[...], xh2, rstd2)
        dh_ref[2] = (du * _gelu_grad(bu, tu)).astype(BF16)
        dh_ref[3] = (dv1 * _gelu_grad(bv, tv)).astype(BF16)

    vec = pl.BlockSpec((1, w), lambda i: (0, 0))
    grp = pl.BlockSpec((groups, B_CHUNK, B_CHUNK), lambda i: (0, 0, 0))
    in_specs = [
        pl.BlockSpec((4, t, w), lambda i: (0, i, 0)),
        pl.BlockSpec((2, CONV_HALO, w), lambda i: (0, jnp.maximum(i * hb - 1, 0), 0)),
        pl.BlockSpec((2, CONV_HALO, w), lambda i: (0, jnp.minimum((i + 1) * hb, last_blk), 0)),
        pl.BlockSpec((t, 2 * w), lambda i: (i, 0)),
        pl.BlockSpec((CONV_HALO, w), lambda i: (jnp.minimum((i + 1) * hb, last_blk), 0)),
        pl.BlockSpec((A_KERNEL, w), lambda i: (0, 0)),
        vec, vec, vec, vec, vec, grp, grp, grp,
        pl.BlockSpec((B_CHUNK, B_CHUNK), lambda i: (0, 0)),
    ]
    vsds = jax.ShapeDtypeStruct((1, w), F32)
    out_specs = [
        pl.BlockSpec((4, t, w), lambda i: (0, i, 0)),
        pl.BlockSpec((A_KERNEL, w), lambda i: (0, 0)),
        vec, vec, vec, vec, vec, grp,
        pl.BlockSpec((groups, B_CHUNK, 1), lambda i: (0, 0, 0)),
    ]
    out_shape = [jax.ShapeDtypeStruct((4, rows, w), BF16), jax.ShapeDtypeStruct((A_KERNEL, w), F32),
                 vsds, vsds, vsds, vsds, vsds, jax.ShapeDtypeStruct((groups, B_CHUNK, B_CHUNK), F32),
                 jax.ShapeDtypeStruct((groups, B_CHUNK, 1), F32)]
    return pl.pallas_call(body, name=name, grid=(ni,), in_specs=in_specs, out_specs=out_specs, out_shape=out_shape,
                          compiler_params=_cp("arbitrary"))(h0, h0, h0, dab, dab, cw, cb, ga, ba, gb, bb, ws, wst, sbb, tril)


def _attn_mask(n):
    qi = lax.broadcasted_iota(jnp.int32, (ATT_BLOCK, 2 * ATT_BLOCK), 0)
    sj = lax.broadcasted_iota(jnp.int32, (ATT_BLOCK, 2 * ATT_BLOCK), 1)
    diff = qi + ATT_BLOCK - sj
    return (diff >= 0) & (diff < ATT_BLOCK) & ((n > 0) | (sj >= ATT_BLOCK))


def _attn_specs(rows, n_q):
    dq = n_q * HEAD_DIM
    dkv = 2 * (n_q // Q_PER_KV) * HEAD_DIM
    kv_blk = dq // dkv
    assert dq % dkv == 0
    return dq, dkv, [
        pl.BlockSpec(memory_space=pltpu.SMEM),
        pl.BlockSpec((ATT_BLOCK, dq), lambda n: (n, 0)),
        pl.BlockSpec((ATT_BLOCK, dkv), lambda n: (n, kv_blk)),
        pl.BlockSpec((ATT_BLOCK, dkv), lambda n: (jnp.maximum(n - 1, 0), kv_blk)),
    ]


def _kv_pair(kvc_ref, kvp_ref, kvh, n_kv):
    ks = slice(kvh * HEAD_DIM, (kvh + 1) * HEAD_DIM)
    vs = slice((n_kv + kvh) * HEAD_DIM, (n_kv + kvh + 1) * HEAD_DIM)
    kk = jnp.concatenate([kvp_ref[:, ks], kvc_ref[:, ks]], axis=0).astype(BF16)
    vv = jnp.concatenate([kvp_ref[:, vs], kvc_ref[:, vs]], axis=0).astype(BF16)
    return kk, vv


def _attn_fwd(qkv, sinks, *, name):
    rows = qkv.shape[0]
    n_q = sinks.shape[0]
    n_kv = n_q // Q_PER_KV
    scale = 1.0 / math.sqrt(HEAD_DIM)
    dq, _, in_specs = _attn_specs(rows, n_q)

    def body(sink_ref, q_ref, kvc_ref, kvp_ref, o_ref, lse_ref):
        valid = _attn_mask(pl.program_id(0))
        for kvh in range(n_kv):
            kk, vv = _kv_pair(kvc_ref, kvp_ref, kvh, n_kv)
            for g in range(Q_PER_KV):
                h = kvh * Q_PER_KV + g
                hs = slice(h * HEAD_DIM, (h + 1) * HEAD_DIM)
                s = lax.dot_general(q_ref[:, hs].astype(BF16), kk, (((1,), (1,)), ((), ())), preferred_element_type=F32)
                s = jnp.where(valid, s * scale, -jnp.inf)
                sk = sink_ref[h]
                m = jnp.maximum(jnp.max(s, axis=1, keepdims=True), sk)
                p = jnp.exp(s - m)
                l = jnp.sum(p, axis=1, keepdims=True) + jnp.exp(sk - m)
                o_ref[:, hs] = jnp.dot((p / l).astype(BF16), vv, preferred_element_type=F32)
                lse_ref[:, h:h + 1] = m + jnp.log(l)

    out_specs = [pl.BlockSpec((ATT_BLOCK, dq), lambda n: (n, 0)), pl.BlockSpec((ATT_BLOCK, n_q), lambda n: (n, 0))]
    out_shape = [jax.ShapeDtypeStruct((rows, dq), F32), jax.ShapeDtypeStruct((rows, n_q), F32)]
    return pl.pallas_call(body, name=name, grid=(rows // ATT_BLOCK,), in_specs=in_specs, out_specs=out_specs,
                          out_shape=out_shape, compiler_params=_cp("parallel"))(sinks, qkv, qkv, qkv)


def _attn_bwd(qkv, dout, lse, sinks, *, name):
    rows = qkv.shape[0]
    n_q = sinks.shape[0]
    n_kv = n_q // Q_PER_KV
    scale = 1.0 / math.sqrt(HEAD_DIM)
    dq_w, dkv_w, in_specs = _attn_specs(rows, n_q)
    blk_q = pl.BlockSpec((ATT_BLOCK, dq_w), lambda n: (n, 0))
    blk_kv = pl.BlockSpec((ATT_BLOCK, dkv_w), lambda n: (n, 0))
    in_specs = in_specs + [blk_q, pl.BlockSpec((ATT_BLOCK, n_q), lambda n: (n, 0))]

    def body(sink_ref, q_ref, kvc_ref, kvp_ref, do_ref, lse_ref, dq_ref, dkc_ref, dkp_ref, dsink_ref):
        n = pl.program_id(0)

        @pl.when(n == 0)
        def _():
            dsink_ref[...] = jnp.zeros_like(dsink_ref)

        valid = _attn_mask(n)
        head_ids = lax.broadcasted_iota(jnp.int32, (1, n_q), 1)
        dsink = jnp.zeros((1, n_q), F32)
        for kvh in range(n_kv):
            kk, vv = _kv_pair(kvc_ref, kvp_ref, kvh, n_kv)
            dk = jnp.zeros((2 * ATT_BLOCK, HEAD_DIM), F32)
            dv = jnp.zeros((2 * ATT_BLOCK, HEAD_DIM), F32)
            for g in range(Q_PER_KV):
                h = kvh * Q_PER_KV + g
                hs = slice(h * HEAD_DIM, (h + 1) * HEAD_DIM)
                qh = q_ref[:, hs].astype(BF16)
                s = lax.dot_general(qh, kk, (((1,), (1,)), ((), ())), preferred_element_type=F32)
                s = jnp.where(valid, s * scale, -jnp.inf)
                lse_h = lse_ref[:, h:h + 1]
                p = jnp.exp(s - lse_h)
                doh = do_ref[:, hs].astype(BF16)
                dp = lax.dot_general(doh, vv, (((1,), (1,)), ((), ())), preferred_element_type=F32)
                delta = jnp.sum(p * dp, axis=1, keepdims=True)
                ds = (p * (dp - delta) * scale).astype(BF16)
                dsink = dsink + jnp.where(head_ids == h, -jnp.sum(jnp.exp(sink_ref[h] - lse_h) * delta), 0.0)
                dq_ref[:, hs] = jnp.dot(ds, kk, preferred_element_type=F32)
                dk = dk + lax.dot_general(ds, qh, (((0,), (0,)), ((), ())), preferred_element_type=F32)
                dv = dv + lax.dot_general(p.astype(BF16), doh, (((0,), (0,)), ((), ())), preferred_element_type=F32)
            ks = slice(kvh * HEAD_DIM, (kvh + 1) * HEAD_DIM)
            vs = slice((n_kv + kvh) * HEAD_DIM, (n_kv + kvh + 1) * HEAD_DIM)
            dkp_ref[:, ks] = dk[0:ATT_BLOCK]
            dkc_ref[:, ks] = dk[ATT_BLOCK:]
            dkp_ref[:, vs] = dv[0:ATT_BLOCK]
            dkc_ref[:, vs] = dv[ATT_BLOCK:]
        dsink_ref[...] += dsink

    out_specs = [blk_q, blk_kv, blk_kv, pl.BlockSpec((1, n_q), lambda n: (0, 0))]
    out_shape = [jax.ShapeDtypeStruct((rows, dq_w), F32), jax.ShapeDtypeStruct((rows, dkv_w), F32),
                 jax.ShapeDtypeStruct((rows, dkv_w), F32), jax.ShapeDtypeStruct((1, n_q), F32)]
    return pl.pallas_call(body, name=name, grid=(rows // ATT_BLOCK,), in_specs=in_specs, out_specs=out_specs,
                          out_shape=out_shape, compiler_params=_cp("arbitrary"))(sinks, qkv, qkv, qkv, dout, lse)


def _dqkv_assemble(dq, dkc, dkp, *, name):
    rows, dq_w = dq.shape
    dkv_w = dkc.shape[1]
    nb = rows // ATT_BLOCK

    def body(dq_ref, dkc_ref, dkp_ref, o_ref, db_ref):
        n = pl.program_id(0)

        @pl.when(n == 0)
        def _():
            db_ref[...] = jnp.zeros_like(db_ref)

        dqv = dq_ref[...]
        dkv = dkc_ref[...] + jnp.where(n == nb - 1, 0.0, dkp_ref[...])
        o_ref[:, 0:dq_w] = dqv.astype(BF16)
        o_ref[:, dq_w:dq_w + dkv_w] = dkv.astype(BF16)
        db_ref[:, 0:dq_w] += _rowsum(dqv)
        db_ref[:, dq_w:dq_w + dkv_w] += _rowsum(dkv)

    width = dq_w + dkv_w
    in_specs = [pl.BlockSpec((ATT_BLOCK, dq_w), lambda n: (n, 0)), pl.BlockSpec((ATT_BLOCK, dkv_w), lambda n: (n, 0)),
                pl.BlockSpec((ATT_BLOCK, dkv_w), lambda n: (jnp.minimum(n + 1, nb - 1), 0))]
    out_specs = [pl.BlockSpec((ATT_BLOCK, width), lambda n: (n, 0)), pl.BlockSpec((1, width), lambda n: (0, 0))]
    out_shape = [jax.ShapeDtypeStruct((rows, width), BF16), jax.ShapeDtypeStruct((1, width), F32)]
    return pl.pallas_call(body, name=name, grid=(nb,), in_specs=in_specs, out_specs=out_specs, out_shape=out_shape,
                          compiler_params=_cp("arbitrary"))(dq, dkc, dkp)


def _row_tile(r, c):
    budget = 2 * 1024 * 1024 // (4 * c)
    for cand in (1024, 512, 256, 128, 64, 32, 16):
        if cand <= budget and r % cand == 0:
            return cand
    return r


def _pair_sum(g, recv, c_idx, *, name):
    _, _, r, c = g.shape
    t = _row_tile(r, c)

    def body(c_ref, g_ref, r_ref, o_ref):
        o_ref[...] = (g_ref[...] + r_ref[...]).astype(BF16)

    grid_spec = pltpu.PrefetchScalarGridSpec(
        num_scalar_prefetch=1, grid=(4, r // t),
        in_specs=[pl.BlockSpec((None, None, t, c), lambda q, i, cr: (q, cr[0], i, 0)),
                  pl.BlockSpec((None, t, c), lambda q, i, cr: (q, i, 0))],
        out_specs=pl.BlockSpec((None, t, c), lambda q, i, cr: (q, i, 0)))
    return pl.pallas_call(body, name=name, grid_spec=grid_spec, out_shape=jax.ShapeDtypeStruct((4, r, c), BF16),
                          compiler_params=_cp("parallel", "parallel"))(c_idx, g, recv)


def _quad_sum(own, recv, qc, dest, lead, *, name):
    _, r, c = own.shape
    t = _row_tile(r, c)
    lead_idx, buf_shape = lead

    def body(qc_ref, own_ref, r1_ref, r2_ref, r3_ref, *rest):
        o_ref = rest[-1]
        acc = own_ref[...].astype(F32) + r1_ref[...].astype(F32)
        acc = acc + r2_ref[...].astype(F32)
        o_ref[...] = acc + r3_ref[...].astype(F32)

    def slot(xor):
        return pl.BlockSpec((None, t, c), lambda i, qc_ref: (qc_ref[0] ^ xor, i, 0))

    if lead_idx is None:
        o_spec = pl.BlockSpec((None, t, c), lambda i, qc_ref: (qc_ref[1], i, 0))
    else:
        o_spec = pl.BlockSpec((None, None, t, c), lambda i, qc_ref: (lead_idx, qc_ref[1], i, 0))
    in_specs = [slot(0), slot(1), slot(2), slot(3)]
    args = [qc, own, recv, recv, recv]
    aliases = {}
    if dest is not None:
        in_specs.append(HBM)
        args.append(dest)
        aliases = {5: 0}
    grid_spec = pltpu.PrefetchScalarGridSpec(num_scalar_prefetch=1, grid=(r // t,), in_specs=in_specs, out_specs=o_spec)
    return pl.pallas_call(body, name=name, grid_spec=grid_spec, out_shape=jax.ShapeDtypeStruct(buf_shape, F32),
                          input_output_aliases=aliases, compiler_params=_cp("parallel"))(*args)


def _adamw(w, g, m, v, *, name):
    r, c = w.shape
    t = _row_tile(r, c)

    def body(w_ref, g_ref, m_ref, v_ref, d_ref, nm_ref, nv_ref):
        gv = g_ref[...]
        nm = ADAM_B1 * m_ref[...] + (1.0 - ADAM_B1) * gv
        nv = ADAM_B2 * v_ref[...] + (1.0 - ADAM_B2) * (gv * gv)
        m_hat = nm / (1.0 - ADAM_B1 ** ADAM_STEP)
        v_hat = nv / (1.0 - ADAM_B2 ** ADAM_STEP)
        d_ref[...] = -ADAM_LR * (m_hat / (jnp.sqrt(v_hat) + ADAM_EPS) + ADAM_WD * w_ref[...])
        nm_ref[...] = nm
        nv_ref[...] = nv

    blk = pl.BlockSpec((t, c), lambda i: (i, 0))
    sds = jax.ShapeDtypeStruct((r, c), F32)
    return pl.pallas_call(body, name=name, grid=(r // t,), in_specs=[blk] * 4, out_specs=[blk] * 3,
                          out_shape=[sds] * 3, compiler_params=_cp("parallel"))(w, g, m, v)


HBM = pl.BlockSpec(memory_space=pl.ANY)


def _place():
    x, y, c = lax.axis_index("x"), lax.axis_index("y"), lax.axis_index("c")
    chips = [(1 - x, y), (x, 1 - y), (1 - x, 1 - y)]
    return x, y, c, 2 * x + y, (x, y, 1 - c), chips


def _rcopy(src, dst, ssem, rsem, dev):
    return pltpu.make_async_remote_copy(src_ref=src, dst_ref=dst, send_sem=ssem, recv_sem=rsem, device_id=dev,
                                        device_id_type=MESH)


def _all_gather(bufs, *, name):
    nt = len(bufs)

    def body(*refs):
        outs = refs[nt:2 * nt]
        ssem, rsem = refs[2 * nt:]
        x, y, c, q, sib, chips = _place()
        sends = []
        for t in range(nt):
            for j, (px, py) in enumerate(chips):
                mine = outs[t].at[q, c]
                cp = _rcopy(mine, mine, ssem.at[t, j], rsem.at[t, j], (px, py, c))
                cp.start()
                sends.append(cp)
        for t in range(nt):
            for j, (px, py) in enumerate(chips):
                landed = outs[t].at[2 * px + py, c]
                _rcopy(landed, landed, ssem.at[t, j], rsem.at[t, j], (px, py, c)).wait_recv()
                cp = _rcopy(landed, landed, ssem.at[t, 3 + j], rsem.at[t, 3 + j], sib)
                cp.start()
                sends.append(cp)
        for t in range(nt):
            for j, (px, py) in enumerate(chips):
                passed = outs[t].at[2 * px + py, 1 - c]
                _rcopy(passed, passed, ssem.at[t, 3 + j], rsem.at[t, 3 + j], sib).wait_recv()
        for cp in sends:
            cp.wait_send()

    out_shape = [jax.ShapeDtypeStruct(b.shape, b.dtype) for b in bufs]
    return pl.pallas_call(
        body, name=name, in_specs=[HBM] * nt, out_specs=[HBM] * nt, out_shape=out_shape,
        input_output_aliases={t: t for t in range(nt)},
        scratch_shapes=[pltpu.SemaphoreType.DMA((nt, 6)), pltpu.SemaphoreType.DMA((nt, 6))],
    )(*bufs)


HBM_ONLY = pl.BlockSpec(memory_space=pltpu.HBM)
SEM = pl.BlockSpec(memory_space=pltpu.SEMAPHORE)
DATAFLOW = pltpu.SideEffectType.DATAFLOW_SIDE_EFFECTING


def _gather_start(bufs, *, name):
    nt = len(bufs)

    def body(*refs):
        ssems, rsems, outs = refs[nt:2 * nt], refs[2 * nt:3 * nt], refs[3 * nt:4 * nt]
        x, y, c, q, sib, chips = _place()
        for t in range(nt):
            for j, (px, py) in enumerate(chips):
                mine = outs[t].at[q]
                _rcopy(mine, mine, ssems[t].at[j], rsems[t].at[j], (px, py, c)).start()

    sems = [pltpu.SemaphoreType.DMA((3,))] * (2 * nt)
    out_shape = sems + [pltpu.HBM(b.shape, b.dtype) for b in bufs]
    res = pl.pallas_call(
        body, name=name, in_specs=[HBM_ONLY] * nt, out_specs=[SEM] * (2 * nt) + [HBM_ONLY] * nt, out_shape=out_shape,
        input_output_aliases={t: 2 * nt + t for t in range(nt)},
        compiler_params=pltpu.CompilerParams(has_side_effects=DATAFLOW),
    )(*[pltpu.with_memory_space_constraint(b, pltpu.HBM) for b in bufs])
    return res[:nt], res[nt:2 * nt], res[2 * nt:]


def _gather_wait(bufs, ssems, rsems, after, *, name):
    nt = len(bufs)

    def body(*refs):
        ssem_refs, rsem_refs = refs[nt:2 * nt], refs[2 * nt:3 * nt]
        outs = refs[3 * nt + 1:]
        x, y, c, q, sib, chips = _place()
        for t in range(nt):
            for j, (px, py) in enumerate(chips):
                mine = outs[t].at[q]
                _rcopy(mine, mine, ssem_refs[t].at[j], rsem_refs[t].at[j], (px, py, c)).wait_send()
        for t in range(nt):
            for j, (px, py) in enumerate(chips):
                theirs = outs[t].at[2 * px + py]
                _rcopy(theirs, theirs, ssem_refs[t].at[j], rsem_refs[t].at[j], (px, py, c)).wait_recv()

    res = pl.pallas_call(
        body, name=name, in_specs=[HBM_ONLY] * nt + [SEM] * (2 * nt) + [HBM], out_specs=[HBM_ONLY] * nt,
        out_shape=[pltpu.HBM(b.shape, b.dtype) for b in bufs], input_output_aliases={t: t for t in range(nt)},
        compiler_params=pltpu.CompilerParams(has_side_effects=DATAFLOW),
    )(*bufs, *ssems, *rsems, after)
    return list(res)


def _sibling_exchange(grads, *, name):
    nt = len(grads)

    def body(*refs):
        ins, outs = refs[:nt], refs[nt:2 * nt]
        ssem, rsem = refs[2 * nt:]
        x, y, c, q, sib, chips = _place()
        sends = []
        for t in range(nt):
            for k in range(4):
                cp = _rcopy(ins[t].at[k, 1 - c], outs[t].at[k], ssem.at[t, k], rsem.at[t, k], sib)
                cp.start()
                sends.append(cp)
        for t in range(nt):
            for k in range(4):
                _rcopy(ins[t].at[k, c], outs[t].at[k], ssem.at[t, k], rsem.at[t, k], sib).wait_recv()
        for cp in sends:
            cp.wait_send()

    out_shape = [jax.ShapeDtypeStruct((4,) + g.shape[2:], g.dtype) for g in grads]
    return pl.pallas_call(
        body, name=name, in_specs=[HBM] * nt, out_specs=[HBM] * nt, out_shape=out_shape,
        scratch_shapes=[pltpu.SemaphoreType.DMA((nt, 4)), pltpu.SemaphoreType.DMA((nt, 4))],
    )(*grads)


def _chip_exchange(partials, *, name):
    nt = len(partials)

    def body(*refs):
        ins, outs = refs[:nt], refs[nt:2 * nt]
        ssem, rsem = refs[2 * nt:]
        x, y, c, q, sib, chips = _place()
        sends = []
        for t in range(nt):
            for j, (px, py) in enumerate(chips):
                cp = _rcopy(ins[t].at[2 * px + py], outs[t].at[q], ssem.at[t, j], rsem.at[t, j], (px, py, c))
                cp.start()
                sends.append(cp)
        for t in range(nt):
            for j, (px, py) in enumerate(chips):
                slot = outs[t].at[2 * px + py]
                _rcopy(slot, slot, ssem.at[t, j], rsem.at[t, j], (px, py, c)).wait_recv()
        for cp in sends:
            cp.wait_send()

    out_shape = [jax.ShapeDtypeStruct(p.shape, p.dtype) for p in partials]
    return pl.pallas_call(
        body, name=name, in_specs=[HBM] * nt, out_specs=[HBM] * nt, out_shape=out_shape,
        scratch_shapes=[pltpu.SemaphoreType.DMA((nt, 3)), pltpu.SemaphoreType.DMA((nt, 3))],
    )(*partials)


def _sibling_share(bufs, layout, *, name):
    no = len(bufs)
    nt = len(layout)

    def body(*refs):
        outs = refs[no:2 * no]
        ssem, rsem = refs[2 * no:]
        x, y, c, q, sib, chips = _place()

        def slot(t, half):
            o, lead = layout[t]
            return outs[o].at[half] if lead is None else outs[o].at[lead, half]

        sends = []
        for t in range(nt):
            cp = _rcopy(slot(t, c), slot(t, c), ssem.at[t], rsem.at[t], sib)
            cp.start()
            sends.append(cp)
        for t in range(nt):
            _rcopy(slot(t, 1 - c), slot(t, 1 - c), ssem.at[t], rsem.at[t], sib).wait_recv()
        for cp in sends:
            cp.wait_send()

    out_shape = [jax.ShapeDtypeStruct(b.shape, b.dtype) for b in bufs]
    return pl.pallas_call(
        body, name=name, in_specs=[HBM] * no, out_specs=[HBM] * no, out_shape=out_shape,
        input_output_aliases={o: o for o in range(no)},
        scratch_shapes=[pltpu.SemaphoreType.DMA((nt,)), pltpu.SemaphoreType.DMA((nt,))],
    )(*bufs)


def _small_all_reduce(pack, *, name):
    r, lanes = pack.shape

    def body(in_ref, out_ref, gath, ssem, rsem):
        x, y, c = lax.axis_index("x"), lax.axis_index("y"), lax.axis_index("c")
        me = 4 * x + 2 * y + c
        gath[me] = in_ref[...]
        sends = []
        for mask in range(1, 8):
            px = 1 - x if mask & 4 else x
            py = 1 - y if mask & 2 else y
            pc = 1 - c if mask & 1 else c
            cp = _rcopy(in_ref, gath.at[me], ssem.at[mask - 1], rsem.at[mask - 1], (px, py, pc))
            cp.start()
            sends.append(cp)
        for mask in range(1, 8):
            px = 1 - x if mask & 4 else x
            py = 1 - y if mask & 2 else y
            pc = 1 - c if mask & 1 else c
            peer = gath.at[4 * px + 2 * py + pc]
            _rcopy(peer, peer, ssem.at[mask - 1], rsem.at[mask - 1], (px, py, pc)).wait_recv()
        acc = gath[0]
        for d in range(1, 8):
            acc = acc + gath[d]
        out_ref[...] = acc
        for cp in sends:
            cp.wait_send()

    vm = pl.BlockSpec(memory_space=pltpu.VMEM)
    return pl.pallas_call(
        body, name=name, in_specs=[vm], out_specs=vm, out_shape=jax.ShapeDtypeStruct((r, lanes), F32),
        scratch_shapes=[pltpu.VMEM((8, r, lanes), F32), pltpu.SemaphoreType.DMA((7,)), pltpu.SemaphoreType.DMA((7,))],
        compiler_params=pltpu.CompilerParams(vmem_limit_bytes=VMEM_LIMIT),
    )(pack)


def _pack(arrays, rows_multiple):
    flat = jnp.concatenate([a.reshape(-1) for a in arrays])
    rows = -(-flat.shape[0] // LANES)
    rows = -(-rows // rows_multiple) * rows_multiple
    flat = jnp.pad(flat, (0, rows * LANES - flat.shape[0]))
    return flat.reshape(rows, LANES)


def _unpack(buf, shapes):
    flat = buf.reshape(-1)
    out, pos = [], 0
    for s in shapes:
        n = math.prod(s)
        out.append(flat[pos:pos + n].reshape(s))
        pos += n
    return out


def _unshard_cols(stacked):
    moved = jnp.moveaxis(stacked, 0, -2)
    return moved.reshape(moved.shape[:-2] + (4 * stacked.shape[-1],))


def _shard_cols(full, q):
    n = full.shape[-1] // 4
    return lax.dynamic_slice_in_dim(full, q * n, n, axis=full.ndim - 1)


def kernel(x, ab_w_in, a_conv_w, a_conv_b, a_norm_g, a_norm_b, b_norm_g, b_norm_b, b_spatial_w, b_spatial_b, ab_w_out, c_w_qkv, c_b_qkv, c_sinks, c_w_o, ffn_w_up, ffn_conv_w, ffn_conv_b, ffn_w_down, ln_g, ln_b, loss_target, m_ab_w_in, m_a_conv_w, m_a_conv_b, m_a_norm_g, m_a_norm_b, m_b_norm_g, m_b_norm_b, m_b_spatial_w, m_b_spatial_b, m_ab_w_out, m_c_w_qkv, m_c_b_qkv, m_c_sinks, m_c_w_o, m_ffn_w_up, m_ffn_conv_w, m_ffn_conv_b, m_ffn_w_down, m_ln_g, m_ln_b, v_ab_w_in, v_a_conv_w, v_a_conv_b, v_a_norm_g, v_a_norm_b, v_b_norm_g, v_b_norm_b, v_b_spatial_w, v_b_spatial_b, v_ab_w_out, v_c_w_qkv, v_c_b_qkv, v_c_sinks, v_c_w_o, v_ffn_w_up, v_ffn_conv_w, v_ffn_conv_b, v_ffn_w_down, v_ln_g, v_ln_b):
    rows, d = x.shape[1], x.shape[2]
    depth = ln_g.shape[0]
    assert depth == 2 and x.shape[0] == 1
    alpha = (2.0 * depth) ** 0.25
    f = ffn_w_down.shape[1] * 4
    n_q = c_sinks.shape[1]
    q_idx = 2 * lax.axis_index("x") + lax.axis_index("y")
    c_idx = lax.axis_index("c")
    xs, tgt = x[0], loss_target[0]

    def own_slot(part):
        buf = lax.empty((4,) + part.shape, part.dtype)
        return lax.dynamic_update_slice(buf, part[None], (q_idx, 0, 0, 0))

    def halves(wm):
        return own_slot(wm.astype(BF16).reshape((2, wm.shape[0] // 2) + wm.shape[1:]))

    small_sharded = [a_conv_w[0], c_b_qkv[0], ffn_conv_w, ln_g, ln_b]
    small_pack = _pack(small_sharded, 16)
    bufs = [halves(ab_w_in[0]), own_slot(small_pack.reshape(2, small_pack.shape[0] // 2, LANES)), halves(ab_w_out[0]),
            halves(ffn_w_up[0]), halves(ffn_w_down[0]), halves(c_w_qkv[0]), halves(c_w_o[0]),
            halves(ffn_w_up[1]), halves(ffn_w_down[1])]
    ssems, rsems, started = _gather_start(bufs, name="gather_start")

    def arrive(idx, after, tag):
        got = _gather_wait([started[i] for i in idx], [ssems[i] for i in idx], [rsems[i] for i in idx], after,
                           name=f"gather_wait_{tag}")
        return [g.reshape(4, 2 * g.shape[2], g.shape[3]) for g in got]

    w_in, small_all = arrive([0, 1], xs, "in")
    small_all = small_all.reshape(4, -1)
    sh_shapes = [s.shape for s in small_sharded]
    pieces, pos = [], 0
    for s in sh_shapes:
        n = math.prod(s)
        pieces.append(_unshard_cols(small_all[:, pos:pos + n].reshape((4,) + s)))
        pos += n
    conv_w_a, b_qkv, conv_w_f, ln_gf, ln_bf = pieces

    tril = jnp.tril(jnp.ones((B_CHUNK, B_CHUNK), F32))
    ws = (b_spatial_w[0] * tril).astype(BF16)
    wst = jnp.swapaxes(ws, 1, 2)
    sbb = jnp.broadcast_to(b_spatial_b[0][:, :, None], b_spatial_w[0].shape)
    mix_vecs = [a_conv_b, a_norm_g, a_norm_b, b_norm_g, b_norm_b]
    cw_f = [jnp.swapaxes(conv_w_f[l].reshape(3, 2, f), 0, 1) for l in range(depth)]
    cb_f = [ffn_conv_b[l].reshape(2, 1, f) for l in range(depth)]
    lng = lambda i, j: ln_gf[i, j].reshape(1, d)
    lnb = lambda i, j: ln_bf[i, j].reshape(1, d)
    sinks = c_sinks[0]

    w_up, w_down = [None, None], [None, None]

    def ffn_fwd(xin, l):
        w_up[l], = arrive([3 + 4 * l], xin, f"up{l}")
        hf, fact = _ffn_up_fwd(xin, w_up[l], cw_f[l], cb_f[l], name=f"ffn{l}_up")
        w_down[l] = arrive([4 + 4 * l], fact, f"down{l}")[0].reshape(-1, d)
        out = _matmul(fact, w_down[l], name=f"ffn{l}_down", tm=1024, tn=1024, tk=1408)
        return hf, fact, out

    h0 = _matmul(xs, w_in, name="mix_in", tm=1024, tn=512, tk=1024, out_stack=4)
    ab = _mixer_fwd(h0, conv_w_a, *mix_vecs, ws, sbb, name="mix_mid")
    w_out = arrive([2], ab, "out")[0].reshape(-1, d)
    mix = _matmul(ab, w_out, name="mix_out", tm=1024, tn=1024, tk=1024)
    x1 = _add_ln_fwd(xs, mix, lng(0, 0), lnb(0, 0), alpha, name="ln00")
    hf0, f0, ffn0 = ffn_fwd(x1, 0)
    x2 = _add_ln_fwd(x1, ffn0, lng(0, 1), lnb(0, 1), alpha, name="ln01")
    w_qkv = _unshard_cols(arrive([5], x2, "qkv")[0])
    qkv = _matmul(x2, w_qkv, name="att_qkv", tm=1024, tn=w_qkv.shape[1], tk=1024, bias=b_qkv.reshape(1, -1))
    ao, lse = _attn_fwd(qkv, sinks, name="att_core")
    w_o = arrive([6], ao, "o")[0].reshape(-1, d)
    att = _matmul(ao, w_o, name="att_out", tm=1024, tn=1024, tk=1024)
    x3 = _add_ln_fwd(x2, att, lng(1, 0), lnb(1, 0), alpha, name="ln10")
    hf1, f1, ffn1 = ffn_fwd(x3, 1)
    x4 = _add_ln_fwd(x3, ffn1, lng(1, 1), lnb(1, 1), alpha, name="ln11")
    sq_err, dy = _loss_and_grad(x4, tgt, name="loss")

    def ffn_bwd(dz, xin, hf, fact, l):
        d_wdown = _matmul(fact, dz, name=f"ffn{l}_down_dw", ta=True, tm=1408, tn=1024, tk=512)
        dfa = _matmul(dz, w_down[l], name=f"ffn{l}_down_dx", tb=True, tm=1024, tn=1408, tk=1024)
        dx_parts, d_wup, dcw, dcb = _ffn_up_bwd(hf, dfa, xin, w_up[l], cw_f[l], cb_f[l], name=f"ffn{l}_up_bwd")
        return [(dx_parts, 1.0), (dz, alpha)], d_wup, d_wdown, dcw, dcb

    dz, dg11, db11 = _add_ln_bwd([(dy, 1.0)], x3, ffn1, lng(1, 1), alpha, name="ln11_bwd")
    dx3, d_wup1, d_wdown1, dcw1, dcb1 = ffn_bwd(dz, x3, hf1, f1, 1)
    dz, dg10, db10 = _add_ln_bwd(dx3, x2, att, lng(1, 0), alpha, name="ln10_bwd")
    d_wo = _matmul(ao, dz, name="att_out_dw", ta=True, tm=1024, tn=1024, tk=1024)
    dao = _matmul(dz, w_o, name="att_out_dx", tb=True, tm=1024, tn=1024, tk=1024)
    dq, dkc, dkp, d_sinks = _attn_bwd(qkv, dao, lse, sinks, name="att_core_bwd")
    dqkv, d_bqkv = _dqkv_assemble(dq, dkc, dkp, name="att_dqkv")
    d_wqkv = _matmul(x2, dqkv, name="att_qkv_dw", ta=True, tm=1024, tn=dqkv.shape[1], tk=1024)
    dx2 = _matmul(dqkv, w_qkv, name="att_qkv_dx", tb=True, tm=1024, tn=1024, tk=dqkv.shape[1], addend=(dz, alpha))
    dz, dg01, db01 = _add_ln_bwd([(dx2, 1.0)], x1, ffn0, lng(0, 1), alpha, name="ln01_bwd")
    dx1, d_wup0, d_wdown0, dcw0, dcb0 = ffn_bwd(dz, x1, hf0, f0, 0)
    dz, dg00, db00 = _add_ln_bwd(dx1, xs, mix, lng(0, 0), alpha, name="ln00_bwd")
    d_wout = _matmul(ab, dz, name="mix_out_dw", ta=True, tm=1024, tn=1024, tk=1024)
    dab = _matmul(dz, w_out, name="mix_out_dx", tb=True, tm=1024, tn=1024, tk=1024)
    dh0, d_cwa, d_cba, d_ga, d_ba, d_gb, d_bb, d_ws, d_sb = _mixer_bwd(
        h0, dab, conv_w_a, *mix_vecs, ws, wst, sbb, tril, name="mix_mid_bwd")
    d_win = _matmul(xs, dh0, name="mix_in_dw", ta=True, tm=1024, tn=512, tk=1024, out_stack=4)
    grad_x = _matmul(dh0, w_in, name="mix_in_dx", tb=True, tm=1024, tn=1024, tk=512, addend=(dz, alpha))

    def owner_view(g):
        if g.ndim == 3:
            return g.reshape(4, 2, g.shape[1] // 2, g.shape[2])
        return g.reshape(4, 2, g.shape[0] // 8, g.shape[1])

    d_wqkv_st = jnp.moveaxis(d_wqkv.reshape(d_wqkv.shape[0], 4, -1), 1, 0)
    big = [owner_view(g) for g in (d_win, d_wout, d_wqkv_st, d_wo, d_wup0, d_wup1, d_wdown0, d_wdown1)]
    from_sib = _sibling_exchange(big, name="reduce_sibling")
    c_arr = jnp.reshape(c_idx, (1,)).astype(jnp.int32)
    partials = [_pair_sum(g, r, c_arr, name=f"reduce_pair{t}") for t, (g, r) in enumerate(zip(big, from_sib))]
    from_chips = _chip_exchange(partials, name="reduce_chips")
    qc = jnp.stack([q_idx, c_idx]).astype(jnp.int32)
    layout = [(0, None), (1, None), (2, None), (3, None), (4, 0), (4, 1), (5, 0), (5, 1)]
    shard_bufs = [None] * 6
    for t, (o, lead) in enumerate(layout):
        piece = partials[t].shape[1:]
        shape = (2,) + piece if lead is None else (2, 2) + piece
        shard_bufs[o] = _quad_sum(partials[t], from_chips[t], qc, shard_bufs[o], (lead, shape), name=f"reduce_quad{t}")
    shared = _sibling_share(shard_bufs, layout, name="reduce_share")
    g_win = shared[0].reshape(ab_w_in.shape)
    g_wout = shared[1].reshape(ab_w_out.shape)
    g_wqkv = shared[2].reshape(c_w_qkv.shape)
    g_wo = shared[3].reshape(c_w_o.shape)
    g_wup = shared[4].reshape(ffn_w_up.shape)
    g_wdown = shared[5].reshape(ffn_w_down.shape)

    d_cw_f = jnp.stack([jnp.swapaxes(dcw0, 0, 1).reshape(3, 2 * f), jnp.swapaxes(dcw1, 0, 1).reshape(3, 2 * f)])
    d_cb_f = jnp.stack([dcb0.reshape(2 * f), dcb1.reshape(2 * f)])
    d_lng = jnp.stack([jnp.stack([dg00[0], dg01[0]]), jnp.stack([dg10[0], dg11[0]])])
    d_lnb = jnp.stack([jnp.stack([db00[0], db01[0]]), jnp.stack([db10[0], db11[0]])])
    small_full = [d_cwa, d_cba, d_ga, d_ba, d_gb, d_bb, d_ws, d_sb, d_bqkv, d_sinks, d_cw_f, d_cb_f, d_lng, d_lnb, sq_err]
    reduced = _small_all_reduce(_pack(small_full, 8), name="reduce_small")
    (r_cwa, r_cba, r_ga, r_ba, r_gb, r_bb, r_ws, r_sb, r_bqkv, r_sinks, r_cwf, r_cbf, r_lng, r_lnb, r_err) = _unpack(
        reduced, [a.shape for a in small_full])
    loss = 0.5 * jnp.sum(r_err) / d

    small_names_w = [a_conv_w, a_conv_b, a_norm_g, a_norm_b, b_norm_g, b_norm_b, b_spatial_w, b_spatial_b, c_b_qkv,
                     c_sinks, ffn_conv_w, ffn_conv_b, ln_g, ln_b]
    small_m = [m_a_conv_w, m_a_conv_b, m_a_norm_g, m_a_norm_b, m_b_norm_g, m_b_norm_b, m_b_spatial_w, m_b_spatial_b,
               m_c_b_qkv, m_c_sinks, m_ffn_conv_w, m_ffn_conv_b, m_ln_g, m_ln_b]
    small_v = [v_a_conv_w, v_a_conv_b, v_a_norm_g, v_a_norm_b, v_b_norm_g, v_b_norm_b, v_b_spatial_w, v_b_spatial_b,
               v_c_b_qkv, v_c_sinks, v_ffn_conv_w, v_ffn_conv_b, v_ln_g, v_ln_b]
    small_g = [_shard_cols(r_cwa, q_idx), r_cba, r_ga, r_ba, r_gb, r_bb, r_ws, r_sb, _shard_cols(r_bqkv, q_idx), r_sinks,
               _shard_cols(r_cwf, q_idx), r_cbf, _shard_cols(r_lng, q_idx), _shard_cols(r_lnb, q_idx)]
    small_g = [g.reshape(w.shape) for g, w in zip(small_g, small_names_w)]
    sm_shapes = [w.shape for w in small_names_w]
    sm_delta, sm_m, sm_v = _adamw(_pack(small_names_w, 8), _pack(small_g, 8), _pack(small_m, 8), _pack(small_v, 8),
                                  name="adamw_small")
    sm_delta, sm_m, sm_v = _unpack(sm_delta, sm_shapes), _unpack(sm_m, sm_shapes), _unpack(sm_v, sm_shapes)

    def adamw_big(w, g, m, v, name):
        two_d = lambda a: a.reshape(-1, a.shape[-1])
        outs = _adamw(two_d(w), two_d(g), two_d(m), two_d(v), name=name)
        return [o.reshape(w.shape) for o in outs]

    big_w = [ab_w_in, ab_w_out, c_w_qkv, c_w_o, ffn_w_up, ffn_w_down]
    big_g = [g_win, g_wout, g_wqkv, g_wo, g_wup, g_wdown]
    big_m = [m_ab_w_in, m_ab_w_out, m_c_w_qkv, m_c_w_o, m_ffn_w_up, m_ffn_w_down]
    big_v = [v_ab_w_in, v_ab_w_out, v_c_w_qkv, v_c_w_o, v_ffn_w_up, v_ffn_w_down]
    big_out = [adamw_big(w, g, m, v, f"adamw_big{t}") for t, (w, g, m, v) in enumerate(zip(big_w, big_g, big_m, big_v))]

    order_big = {0: 0, 9: 1, 10: 2, 13: 3, 14: 4, 17: 5}
    order_small = {1: 0, 2: 1, 3: 2, 4: 3, 5: 4, 6: 5, 7: 6, 8: 7, 11: 8, 12: 9, 15: 10, 16: 11, 18: 12, 19: 13}
    grads, deltas, new_m, new_v = [], [], [], []
    for pos_w in range(20):
        if pos_w in order_big:
            t = order_big[pos_w]
            grads.append(big_g[t])
            deltas.append(big_out[t][0])
            new_m.append(big_out[t][1])
            new_v.append(big_out[t][2])
        else:
            t = order_small[pos_w]
            grads.append(small_g[t])
            deltas.append(sm_delta[t])
            new_m.append(sm_m[t])
            new_v.append(sm_v[t])
    return (loss, grad_x[None], *grads, *deltas, *new_m, *new_v)
```

```python
import functools
import math

import jax
import jax.numpy as jnp
from jax import lax
from jax.experimental import pallas as pl
from jax.experimental.pallas import tpu as pltpu

F32 = jnp.float32
BF16 = jnp.bfloat16
MESH = pl.DeviceIdType.MESH

LN_EPS = 1e-5
HEAD_DIM = 64
ATT_BLOCK = 128
Q_PER_KV = 8
A_KERNEL = 31
CONV_HALO = 32
FFN_HALO = 8
B_CHUNK = 128
LANES = 128
MXU_WIDTH = 256
GELU_C = math.sqrt(2.0 / math.pi)
ADAM_LR = 0.001
ADAM_B1 = 0.9
ADAM_B2 = 0.999
ADAM_EPS = 1e-08
ADAM_WD = 0.01
ADAM_STEP = 10
VMEM_LIMIT = 56 * 1024 * 1024


def _cp(*dims):
    return pltpu.CompilerParams(dimension_semantics=dims, vmem_limit_bytes=VMEM_LIMIT)


def _pick(n, prefs):
    for p in prefs:
        if n % p == 0:
            return p
    return n


def _sig(x):
    return 1.0 / (1.0 + jnp.exp(-x))


def _gelu(x):
    t = jnp.tanh(GELU_C * (x + 0.044715 * (x * x * x)))
    return x * (0.5 * (1.0 + t)), t


def _gelu_grad(x, t):
    return 0.5 * (1.0 + t) + 0.5 * x * (1.0 - t * t) * (GELU_C * (1.0 + 3.0 * 0.044715 * x * x))


def _ln_stats(z):
    mu = jnp.mean(z, axis=-1, keepdims=True)
    zc = z - mu
    var = jnp.mean(zc * zc, axis=-1, keepdims=True)
    rstd = lax.rsqrt(var + LN_EPS)
    return zc * rstd, rstd


def _ln_bwd(dxh, xh, rstd):
    return rstd * (dxh - jnp.mean(dxh, axis=-1, keepdims=True) - xh * jnp.mean(dxh * xh, axis=-1, keepdims=True))


def _rowsum(a):
    return jnp.sum(a, axis=0, keepdims=True)


def _lshape(a):
    return (a.shape[0], a.shape[1]) if a.ndim == 2 else (a.shape[1], a.shape[0] * a.shape[2])


def _spec2(arr, blk_r, blk_c, ridx, cidx):
    if len(arr.shape) == 2:
        return pl.BlockSpec((blk_r, blk_c), lambda i, j, k: (ridx(i, j, k), cidx(i, j, k)))
    per = arr.shape[2] // blk_c
    assert arr.shape[2] % blk_c == 0
    return pl.BlockSpec((None, blk_r, blk_c), lambda i, j, k: (cidx(i, j, k) // per, ridx(i, j, k), cidx(i, j, k) % per))


def _matmul(a, b, *, name, ta=False, tb=False, tm, tn, tk, out_dtype=F32, out_stack=None, bias=None, addend=None):
    ar, ac = _lshape(a)
    br, bc = _lshape(b)
    m, kdim = (ac, ar) if ta else (ar, ac)
    n = br if tb else bc
    assert (bc if tb else br) == kdim
    tm, tn, tk = min(tm, m), min(tn, n), min(tk, kdim)
    assert m % tm == 0 and n % tn == 0 and kdim % tk == 0, (name, m, n, kdim, tm, tn, tk)
    nk = kdim // tk
    gi, gj, gk = (lambda i, j, k: i), (lambda i, j, k: j), (lambda i, j, k: k)
    a_spec = _spec2(a, tk, tm, gk, gi) if ta else _spec2(a, tm, tk, gi, gk)
    b_spec = _spec2(b, tn, tk, gj, gk) if tb else _spec2(b, tk, tn, gk, gj)
    if out_stack is None:
        out_sds = jax.ShapeDtypeStruct((m, n), out_dtype)
    else:
        out_sds = jax.ShapeDtypeStruct((out_stack, m, n // out_stack), out_dtype)
    o_spec = _spec2(out_sds, tm, tn, gi, gj)
    in_specs = [a_spec, b_spec]
    args = [a, b]
    if bias is not None:
        in_specs.append(pl.BlockSpec((1, tn), lambda i, j, k: (0, j)))
        args.append(bias)
    scale = None
    if addend is not None:
        add_arr, scale = addend
        in_specs.append(pl.BlockSpec((tm, tn), lambda i, j, k: (i, j)))
        args.append(add_arr)
    use_acc = nk > 1 and out_dtype != F32
    dn = (((0 if ta else 1,), (1 if tb else 0,)), ((), ()))

    def body(*refs):
        a_ref, b_ref = refs[0], refs[1]
        pos = 2
        bias_ref = add_ref = None
        if bias is not None:
            bias_ref = refs[pos]
            pos += 1
        if addend is not None:
            add_ref = refs[pos]
            pos += 1
        o_ref = refs[pos]
        acc_ref = refs[pos + 1] if use_acc else o_ref
        p = lax.dot_general(a_ref[...].astype(BF16), b_ref[...].astype(BF16), dn, preferred_element_type=F32)

        def finish(val):
            if bias_ref is not None:
                val = val + bias_ref[...]
            if add_ref is not None:
                val = val + scale * add_ref[...]
            return val.astype(out_dtype)

        if nk == 1:
            o_ref[...] = finish(p)
        else:
            k = pl.program_id(2)

            @pl.when(k == 0)
            def _():
                acc_ref[...] = p

            @pl.when(k > 0)
            def _():
                acc_ref[...] += p

            if use_acc or bias_ref is not None or add_ref is not None:
                @pl.when(k == nk - 1)
                def _():
                    o_ref[...] = finish(acc_ref[...])

    return pl.pallas_call(
        body, name=name, grid=(m // tm, n // tn, nk), in_specs=in_specs, out_specs=o_spec, out_shape=out_sds,
        scratch_shapes=[pltpu.VMEM((tm, tn), F32)] if use_acc else [],
        compiler_params=_cp("parallel", "parallel", "arbitrary"),
    )(*args)


def _add_ln_fwd(x, s, g, b, alpha, *, name):
    rows, d = x.shape
    t = _pick(rows, (512, 256))

    def body(x_ref, s_ref, g_ref, b_ref, y_ref):
        xh, _ = _ln_stats(alpha * x_ref[...] + s_ref[...])
        y_ref[...] = xh * g_ref[...] + b_ref[...]

    row = pl.BlockSpec((t, d), lambda i: (i, 0))
    vec = pl.BlockSpec((1, d), lambda i: (0, 0))
    return pl.pallas_call(body, name=name, grid=(rows // t,), in_specs=[row, row, vec, vec], out_specs=row,
                          out_shape=jax.ShapeDtypeStruct((rows, d), F32), compiler_params=_cp("parallel"))(x, s, g, b)


def _add_ln_bwd(dy_terms, x, s, g, alpha, *, name):
    rows, d = x.shape
    t = _pick(rows, (512, 256))
    nterm = len(dy_terms)
    scales = [sc for _, sc in dy_terms]
    ranks = [a.ndim for a, _ in dy_terms]

    def body(*refs):
        dy_refs = refs[:nterm]
        x_ref, s_ref, g_ref, dz_ref, dg_ref, db_ref = refs[nterm:]

        @pl.when(pl.program_id(0) == 0)
        def _():
            dg_ref[...] = jnp.zeros_like(dg_ref)
            db_ref[...] = jnp.zeros_like(db_ref)

        dyv = None
        for r, sc, rank in zip(dy_refs, scales, ranks):
            slabs = [r[...]] if rank == 2 else [r[p] for p in range(r.shape[0])]
            for v in slabs:
                v = v if sc == 1.0 else sc * v
                dyv = v if dyv is None else dyv + v
        xh, rstd = _ln_stats(alpha * x_ref[...] + s_ref[...])
        dz_ref[...] = _ln_bwd(dyv * g_ref[...], xh, rstd)
        dg_ref[...] += _rowsum(dyv * xh)
        db_ref[...] += _rowsum(dyv)

    row = pl.BlockSpec((t, d), lambda i: (i, 0))
    vec = pl.BlockSpec((1, d), lambda i: (0, 0))
    vsds = jax.ShapeDtypeStruct((1, d), F32)
    dy_specs = [row if a.ndim == 2 else pl.BlockSpec((a.shape[0], t, d), lambda i: (0, i, 0)) for a, _ in dy_terms]
    return pl.pallas_call(body, name=name, grid=(rows // t,), in_specs=dy_specs + [row, row, vec], out_specs=[row, vec, vec],
                          out_shape=[jax.ShapeDtypeStruct((rows, d), F32), vsds, vsds],
                          compiler_params=_cp("arbitrary"))(*[a for a, _ in dy_terms], x, s, g)


def _loss_and_grad(y, tgt, *, name):
    rows, d = y.shape
    t = _pick(rows, (512, 256))

    def body(y_ref, t_ref, l_ref, dy_ref):
        @pl.when(pl.program_id(0) == 0)
        def _():
            l_ref[...] = jnp.zeros_like(l_ref)

        e = y_ref[...] - t_ref[...]
        l_ref[...] += _rowsum(e * e)
        dy_ref[...] = e * (1.0 / d)

    row = pl.BlockSpec((t, d), lambda i: (i, 0))
    vec = pl.BlockSpec((1, d), lambda i: (0, 0))
    return pl.pallas_call(body, name=name, grid=(rows // t,), in_specs=[row, row], out_specs=[vec, row],
                          out_shape=[jax.ShapeDtypeStruct((1, d), F32), jax.ShapeDtypeStruct((rows, d), F32)],
                          compiler_params=_cp("arbitrary"))(y, tgt)


def _col_blocks(width):
    out, pos = [], 0
    while pos < width:
        w = MXU_WIDTH if width - pos >= MXU_WIDTH else width - pos
        out.append(slice(pos, pos + w))
        pos += w
    return out


def _conv3(e, w, b):
    r1 = pltpu.roll(e, 1, 0)
    r2 = pltpu.roll(e, 2, 0)
    return w[0:1, :] * r2 + w[1:2, :] * r1 + w[2:3, :] * e + b, (r2, r1, e)


def _ffn_up_fwd(x, w_up, cw, cb, *, name):
    rows, d = x.shape
    nq, _, tc = w_up.shape
    nj = nq // 2
    f = tc * nj
    tm = _pick(rows, (512, 256))
    blocks = _col_blocks(tc)

    def body(x_ref, wg_ref, wv_ref, cw_ref, cb_ref, hf_ref, f_ref, prev_ref):
        @pl.when(pl.program_id(1) == 0)
        def _():
            prev_ref[...] = jnp.zeros_like(prev_ref)

        xb = x_ref[...].astype(BF16)
        for cs in blocks:
            hc = []
            for s, w_ref in ((0, wg_ref), (1, wv_ref)):
                h = jnp.dot(xb, w_ref[:, cs], preferred_element_type=F32)
                hf_ref[s, :, cs] = h
                e = jnp.concatenate([prev_ref[s, :, cs], h], axis=0)
                prev_ref[s, :, cs] = h[tm - FFN_HALO:]
                y, _ = _conv3(e, cw_ref[s, :, cs], cb_ref[s, :, cs])
                hc.append(y[FFN_HALO:])
            gl, _ = _gelu(hc[0])
            f_ref[:, cs] = (gl * hc[1]).astype(BF16)

    in_specs = [
        pl.BlockSpec((tm, d), lambda j, i: (i, 0)),
        pl.BlockSpec((None, d, tc), lambda j, i: (j, 0, 0)),
        pl.BlockSpec((None, d, tc), lambda j, i: (nj + j, 0, 0)),
        pl.BlockSpec((2, 3, tc), lambda j, i: (0, 0, j)),
        pl.BlockSpec((2, 1, tc), lambda j, i: (0, 0, j)),
    ]
    out_specs = [pl.BlockSpec((2, tm, tc), lambda j, i: (0, i, j)), pl.BlockSpec((tm, tc), lambda j, i: (i, j))]
    out_shape = [jax.ShapeDtypeStruct((2, rows, f), F32), jax.ShapeDtypeStruct((rows, f), BF16)]
    return pl.pallas_call(body, name=name, grid=(nj, rows // tm), in_specs=in_specs, out_specs=out_specs, out_shape=out_shape,
                          scratch_shapes=[pltpu.VMEM((2, FFN_HALO, tc), F32)],
                          compiler_params=_cp("parallel", "arbitrary"))(x, w_up, w_up, cw, cb)


def _ffn_up_bwd(hf, df, x, w_up, cw, cb, *, name):
    _, rows, f = hf.shape
    d = x.shape[1]
    nq, _, tc = w_up.shape
    nj = nq // 2
    tm = _pick(rows, (512, 256))
    hb = tm // FFN_HALO
    once = pl.Buffered(1)
    ni = rows // tm
    last_blk = rows // FFN_HALO - 1
    ext = tm + 2 * FFN_HALO
    tile = slice(FFN_HALO, FFN_HALO + tm)
    blocks = _col_blocks(tc)

    def body(h_ref, hp_ref, hn_ref, d_ref, dn_ref, x_ref, wg_ref, wv_ref, cw_ref, cb_ref, dx_ref, dw_ref, dcw_ref, dcb_ref):
        i = pl.program_id(1)
        first = i == 0
        last = i == ni - 1

        @pl.when(first)
        def _():
            dw_ref[...] = jnp.zeros_like(dw_ref)
            dcw_ref[...] = jnp.zeros_like(dcw_ref)
            dcb_ref[...] = jnp.zeros_like(dcb_ref)

        xt = x_ref[...].astype(BF16).T
        dx = None
        for cs in blocks:
            wc = cs.stop - cs.start
            de = jnp.concatenate([jnp.zeros((FFN_HALO, wc), F32), d_ref[:, cs], jnp.where(last, 0.0, dn_ref[:, cs])], axis=0)
            taps, hc = [], []
            for s in range(2):
                e = jnp.concatenate([jnp.where(first, 0.0, hp_ref[s, :, cs]), h_ref[s, :, cs], hn_ref[s, :, cs]], axis=0)
                y, tp = _conv3(e, cw_ref[s, :, cs], cb_ref[s, :, cs])
                hc.append(y)
                taps.append(tp)
            gl, th = _gelu(hc[0])
            dhc = (de * hc[1] * _gelu_grad(hc[0], th), de * gl)
            for s, w_ref in ((0, wg_ref), (1, wv_ref)):
                w = cw_ref[s, :, cs]
                g = dhc[s]
                dh = (w[2:3, :] * g + w[1:2, :] * pltpu.roll(g, ext - 1, 0) + w[0:1, :] * pltpu.roll(g, ext - 2, 0))[tile]
                gt = g[tile]
                for k in range(3):
                    dcw_ref[s, k:k + 1, cs] += _rowsum(gt * taps[s][k][tile])
                dcb_ref[s, :, cs] += _rowsum(gt)
                dhb = dh.astype(BF16)
                part = lax.dot_general(dhb, w_ref[:, cs], (((1,), (1,)), ((), ())), preferred_element_type=F32)
                dx = part if dx is None else dx + part
                dw_ref[s, :, cs] += jnp.dot(xt, dhb, preferred_element_type=F32)
        dx_ref[...] = dx

    in_specs = [
        pl.BlockSpec((2, tm, tc), lambda j, i: (0, i, j)),
        pl.BlockSpec((2, FFN_HALO, tc), lambda j, i: (0, jnp.maximum(i * hb - 1, 0), j)),
        pl.BlockSpec((2, FFN_HALO, tc), lambda j, i: (0, jnp.minimum((i + 1) * hb, last_blk), j)),
        pl.BlockSpec((tm, tc), lambda j, i: (i, j)),
        pl.BlockSpec((FFN_HALO, tc), lambda j, i: (jnp.minimum((i + 1) * hb, last_blk), j)),
        pl.BlockSpec((tm, d), lambda j, i: (i, 0)),
        pl.BlockSpec((None, d, tc), lambda j, i: (j, 0, 0), pipeline_mode=once),
        pl.BlockSpec((None, d, tc), lambda j, i: (nj + j, 0, 0), pipeline_mode=once),
        pl.BlockSpec((2, 3, tc), lambda j, i: (0, 0, j)),
        pl.BlockSpec((2, 1, tc), lambda j, i: (0, 0, j)),
    ]
    out_specs = [
        pl.BlockSpec((None, tm, d), lambda j, i: (j, i, 0)),
        pl.BlockSpec((2, None, d, tc), lambda j, i: (0, j, 0, 0), pipeline_mode=once),
        pl.BlockSpec((2, 3, tc), lambda j, i: (0, 0, j)),
        pl.BlockSpec((2, 1, tc), lambda j, i: (0, 0, j)),
    ]
    out_shape = [jax.ShapeDtypeStruct((nj, rows, d), F32), jax.ShapeDtypeStruct((2, nj, d, tc), F32),
                 jax.ShapeDtypeStruct((2, 3, f), F32), jax.ShapeDtypeStruct((2, 1, f), F32)]
    dx, dw, dcw, dcb = pl.pallas_call(body, name=name, grid=(nj, ni), in_specs=in_specs, out_specs=out_specs,
                                      out_shape=out_shape, compiler_params=_cp("parallel", "arbitrary"))(
        hf, hf, hf, df, df, x, w_up, w_up, cw, cb)
    return dx, dw.reshape(nq, d, tc), dcw, dcb


def _mixer_fwd(h0, cw, cb, ga, ba, gb, bb, ws, sbb, *, name):
    _, rows, w = h0.shape
    t = _pick(rows, (256,))
    hb = t // CONV_HALO
    groups = w // B_CHUNK

    def body(h_ref, hp_ref, cw_ref, cb_ref, ga_ref, ba_ref, gb_ref, bb_ref, ws_ref, sb_ref, o_ref):
        first = pl.program_id(0) == 0
        a1 = h_ref[0] * _sig(h_ref[1])
        a1p = jnp.where(first, 0.0, hp_ref[0] * _sig(hp_ref[1]))
        e = jnp.concatenate([a1p, a1], axis=0)
        acc = cw_ref[A_KERNEL - 1:A_KERNEL, :] * e
        for k in range(A_KERNEL - 1):
            acc = acc + cw_ref[k:k + 1, :] * pltpu.roll(e, A_KERNEL - 1 - k, 0)
        xh, _ = _ln_stats(acc[CONV_HALO:] + cb_ref[...])
        a3 = xh * ga_ref[...] + ba_ref[...]
        o_ref[:, 0:w] = (a3 * _sig(a3)).astype(BF16)

        u, _ = _gelu(h_ref[2])
        v1, _ = _gelu(h_ref[3])
        xh2, _ = _ln_stats(v1)
        v2 = (xh2 * gb_ref[...] + bb_ref[...]).astype(BF16)
        for c in range(t // B_CHUNK):
            rs = slice(c * B_CHUNK, (c + 1) * B_CHUNK)
            for g in range(groups):
                cs = slice(g * B_CHUNK, (g + 1) * B_CHUNK)
                mixed = jnp.dot(ws_ref[g], v2[rs, cs], preferred_element_type=F32) + sb_ref[g]
                o_ref[rs, w + g * B_CHUNK:w + (g + 1) * B_CHUNK] = (u[rs, cs] * mixed).astype(BF16)

    vec = pl.BlockSpec((1, w), lambda i: (0, 0))
    grp = pl.BlockSpec((groups, B_CHUNK, B_CHUNK), lambda i: (0, 0, 0))
    in_specs = [
        pl.BlockSpec((4, t, w), lambda i: (0, i, 0)),
        pl.BlockSpec((2, CONV_HALO, w), lambda i: (0, jnp.maximum(i * hb - 1, 0), 0)),
        pl.BlockSpec((A_KERNEL, w), lambda i: (0, 0)),
        vec, vec, vec, vec, vec, grp, grp,
    ]
    return pl.pallas_call(body, name=name, grid=(rows // t,), in_specs=in_specs,
                          out_specs=pl.BlockSpec((t, 2 * w), lambda i: (i, 0)),
                          out_shape=jax.ShapeDtypeStruct((rows, 2 * w), BF16),
                          compiler_params=_cp("parallel"))(h0, h0, cw, cb, ga, ba, gb, bb, ws, sbb)


def _mixer_bwd(h0, dab, cw, cb, ga, ba, gb, bb, ws, wst, sbb, tril, *, name):
    _, rows, w = h0.shape
    t = _pick(rows, (256,))
    hb = t // CONV_HALO
    ni = rows // t
    last_blk = rows // CONV_HALO - 1
    ext = t + 2 * CONV_HALO
    tile = slice(CONV_HALO, CONV_HALO + t)
    groups = w // B_CHUNK
    taps = A_KERNEL - 1

    def body(h_ref, hp_ref, hn_ref, d_ref, dn_ref, cw_ref, cb_ref, ga_ref, ba_ref, gb_ref, bb_ref, ws_ref, wst_ref,
             sb_ref, tril_ref, dh_ref, dcw_ref, dcb_ref, dga_ref, dba_ref, dgb_ref, dbb_ref, dws_ref, dsb_ref):
        i = pl.program_id(0)
        first = i == 0
        last = i == ni - 1

        @pl.when(first)
        def _():
            for r in (dcw_ref, dcb_ref, dga_ref, dba_ref, dgb_ref, dbb_ref, dws_ref, dsb_ref):
                r[...] = jnp.zeros_like(r)

        av_e = jnp.concatenate([hp_ref[0], h_ref[0], hn_ref[0]], axis=0)
        sg_e = _sig(jnp.concatenate([hp_ref[1], h_ref[1], hn_ref[1]], axis=0))
        rows_e = lax.broadcasted_iota(jnp.int32, (ext, 1), 0)
        a1_e = jnp.where(first & (rows_e < CONV_HALO), 0.0, av_e * sg_e)
        acc = cw_ref[taps:taps + 1, :] * a1_e
        for k in range(taps):
            acc = acc + cw_ref[k:k + 1, :] * pltpu.roll(a1_e, taps - k, 0)
        xh, rstd = _ln_stats(acc + cb_ref[...])
        a3 = xh * ga_ref[...] + ba_ref[...]
        s3 = _sig(a3)
        da_e = jnp.concatenate([jnp.zeros((CONV_HALO, w), F32), d_ref[:, 0:w], jnp.where(last, 0.0, dn_ref[...])], axis=0)
        da3 = da_e * (s3 * (1.0 + a3 * (1.0 - s3)))
        da2 = _ln_bwd(da3 * ga_ref[...], xh, rstd)
        dga_ref[...] += _rowsum(da3[tile] * xh[tile])
        dba_ref[...] += _rowsum(da3[tile])
        da2t = da2[tile]
        dcb_ref[...] += _rowsum(da2t)
        dcw_ref[taps:taps + 1, :] += _rowsum(da2t * a1_e[tile])
        da1 = cw_ref[taps:taps + 1, :] * da2
        for k in range(taps):
            sh = taps - k
            dcw_ref[k:k + 1, :] += _rowsum(da2t * pltpu.roll(a1_e, sh, 0)[tile])
            da1 = da1 + cw_ref[k:k + 1, :] * pltpu.roll(da2, ext - sh, 0)
        da1t = da1[tile]
        sgt = sg_e[tile]
        dh_ref[0] = (da1t * sgt).astype(BF16)
        dh_ref[1] = (da1t * h_ref[0] * sgt * (1.0 - sgt)).astype(BF16)

        bu = h_ref[2]
        bv = h_ref[3]
        u, tu = _gelu(bu)
        v1, tv = _gelu(bv)
        xh2, rstd2 = _ln_stats(v1)
        v2 = (xh2 * gb_ref[...] + bb_ref[...]).astype(BF16)
        db = d_ref[:, w:2 * w]
        dmx_all = db * u
        du_parts, dv2_parts = [], []
        for c in range(t // B_CHUNK):
            rs = slice(c * B_CHUNK, (c + 1) * B_CHUNK)
            du_row, dv2_row = [], []
            for g in range(groups):
                cs = slice(g * B_CHUNK, (g + 1) * B_CHUNK)
                v2cg = v2[rs, cs]
                mixed = jnp.dot(ws_ref[g], v2cg, preferred_element_type=F32) + sb_ref[g]
                dmx = dmx_all[rs, cs]
                dmxb = dmx.astype(BF16)
                du_row.append(db[rs, cs] * mixed)
                dv2_row.append(jnp.dot(wst_ref[g], dmxb, preferred_element_type=F32))
                dws_ref[g] += tril_ref[...] * lax.dot_general(dmxb, v2cg, (((1,), (1,)), ((), ())),
                                                               preferred_element_type=F32)
                dsb_ref[g] += jnp.sum(dmx, axis=1, keepdims=True)
            du_parts.append(jnp.concatenate(du_row, axis=1))
            dv2_parts.append(jnp.concatenate(dv2_row, axis=1))
        du = jnp.concatenate(du_parts, axis=0)
        dv2 = jnp.concatenate(dv2_parts, axis=0)
        dgb_ref[...] += _rowsum(dv2 * xh2)
        dbb_ref[...] += _rowsum(dv2)
        dv1 = _ln_bwd(dv2 * gb_ref[...], xh2, rstd2)
        dh_ref[2] = (du * _gelu_grad(bu, tu)).astype(BF16)
        dh_ref[3] = (dv1 * _gelu_grad(bv, tv)).astype(BF16)

    vec = pl.BlockSpec((1, w), lambda i: (0, 0))
    grp = pl.BlockSpec((groups, B_CHUNK, B_CHUNK), lambda i: (0, 0, 0))
    in_specs = [
        pl.BlockSpec((4, t, w), lambda i: (0, i, 0)),
        pl.BlockSpec((2, CONV_HALO, w), lambda i: (0, jnp.maximum(i * hb - 1, 0), 0)),
        pl.BlockSpec((2, CONV_HALO, w), lambda i: (0, jnp.minimum((i + 1) * hb, last_blk), 0)),
        pl.BlockSpec((t, 2 * w), lambda i: (i, 0)),
        pl.BlockSpec((CONV_HALO, w), lambda i: (jnp.minimum((i + 1) * hb, last_blk), 0)),
        pl.BlockSpec((A_KERNEL, w), lambda i: (0, 0)),
        vec, vec, vec, vec, vec, grp, grp, grp,
        pl.BlockSpec((B_CHUNK, B_CHUNK), lambda i: (0, 0)),
    ]
    vsds = jax.ShapeDtypeStruct((1, w), F32)
    out_specs = [
        pl.BlockSpec((4, t, w), lambda i: (0, i, 0)),
        pl.BlockSpec((A_KERNEL, w), lambda i: (0, 0)),
        vec, vec, vec, vec, vec, grp,
        pl.BlockSpec((groups, B_CHUNK, 1), lambda i: (0, 0, 0)),
    ]
    out_shape = [jax.ShapeDtypeStruct((4, rows, w), BF16), jax.ShapeDtypeStruct((A_KERNEL, w), F32),
                 vsds, vsds, vsds, vsds, vsds, jax.ShapeDtypeStruct((groups, B_CHUNK, B_CHUNK), F32),
                 jax.ShapeDtypeStruct((groups, B_CHUNK, 1), F32)]
    return pl.pallas_call(body, name=name, grid=(ni,), in_specs=in_specs, out_specs=out_specs, out_shape=out_shape,
                          compiler_params=_cp("arbitrary"))(h0, h0, h0, dab, dab, cw, cb, ga, ba, gb, bb, ws, wst, sbb, tril)


def _attn_mask(n):
    qi = lax.broadcasted_iota(jnp.int32, (ATT_BLOCK, 2 * ATT_BLOCK), 0)
    sj = lax.broadcasted_iota(jnp.int32, (ATT_BLOCK, 2 * ATT_BLOCK), 1)
    diff = qi + ATT_BLOCK - sj
    return (diff >= 0) & (diff < ATT_BLOCK) & ((n > 0) | (sj >= ATT_BLOCK))


def _attn_specs(rows, n_q):
    dq = n_q * HEAD_DIM
    dkv = 2 * (n_q // Q_PER_KV) * HEAD_DIM
    kv_blk = dq // dkv
    assert dq % dkv == 0
    return dq, dkv, [
        pl.BlockSpec(memory_space=pltpu.SMEM),
        pl.BlockSpec((ATT_BLOCK, dq), lambda n: (n, 0)),
        pl.BlockSpec((ATT_BLOCK, dkv), lambda n: (n, kv_blk)),
        pl.BlockSpec((ATT_BLOCK, dkv), lambda n: (jnp.maximum(n - 1, 0), kv_blk)),
    ]


def _kv_pair(kvc_ref, kvp_ref, kvh, n_kv):
    ks = slice(kvh * HEAD_DIM, (kvh + 1) * HEAD_DIM)
    vs = slice((n_kv + kvh) * HEAD_DIM, (n_kv + kvh + 1) * HEAD_DIM)
    kk = jnp.concatenate([kvp_ref[:, ks], kvc_ref[:, ks]], axis=0).astype(BF16)
    vv = jnp.concatenate([kvp_ref[:, vs], kvc_ref[:, vs]], axis=0).astype(BF16)
    return kk, vv


def _attn_fwd(qkv, sinks, *, name):
    rows = qkv.shape[0]
    n_q = sinks.shape[0]
    n_kv = n_q // Q_PER_KV
    scale = 1.0 / math.sqrt(HEAD_DIM)
    dq, _, in_specs = _attn_specs(rows, n_q)

    def body(sink_ref, q_ref, kvc_ref, kvp_ref, o_ref, lse_ref):
        valid = _attn_mask(pl.program_id(0))
        for kvh in range(n_kv):
            kk, vv = _kv_pair(kvc_ref, kvp_ref, kvh, n_kv)
            for g in range(Q_PER_KV):
                h = kvh * Q_PER_KV + g
                hs = slice(h * HEAD_DIM, (h + 1) * HEAD_DIM)
                s = lax.dot_general(q_ref[:, hs].astype(BF16), kk, (((1,), (1,)), ((), ())), preferred_element_type=F32)
                s = jnp.where(valid, s * scale, -jnp.inf)
                sk = sink_ref[h]
                m = jnp.maximum(jnp.max(s, axis=1, keepdims=True), sk)
                p = jnp.exp(s - m)
                l = jnp.sum(p, axis=1, keepdims=True) + jnp.exp(sk - m)
                o_ref[:, hs] = jnp.dot((p / l).astype(BF16), vv, preferred_element_type=F32)
                lse_ref[:, h:h + 1] = m + jnp.log(l)

    out_specs = [pl.BlockSpec((ATT_BLOCK, dq), lambda n: (n, 0)), pl.BlockSpec((ATT_BLOCK, n_q), lambda n: (n, 0))]
    out_shape = [jax.ShapeDtypeStruct((rows, dq), F32), jax.ShapeDtypeStruct((rows, n_q), F32)]
    return pl.pallas_call(body, name=name, grid=(rows // ATT_BLOCK,), in_specs=in_specs, out_specs=out_specs,
                          out_shape=out_shape, compiler_params=_cp("parallel"))(sinks, qkv, qkv, qkv)


def _attn_bwd(qkv, dout, lse, sinks, *, name):
    rows = qkv.shape[0]
    n_q = sinks.shape[0]
    n_kv = n_q // Q_PER_KV
    scale = 1.0 / math.sqrt(HEAD_DIM)
    dq_w, dkv_w, in_specs = _attn_specs(rows, n_q)
    blk_q = pl.BlockSpec((ATT_BLOCK, dq_w), lambda n: (n, 0))
    blk_kv = pl.BlockSpec((ATT_BLOCK, dkv_w), lambda n: (n, 0))
    in_specs = in_specs + [blk_q, pl.BlockSpec((ATT_BLOCK, n_q), lambda n: (n, 0))]

    def body(sink_ref, q_ref, kvc_ref, kvp_ref, do_ref, lse_ref, dq_ref, dkc_ref, dkp_ref, dsink_ref):
        n = pl.program_id(0)

        @pl.when(n == 0)
        def _():
            dsink_ref[...] = jnp.zeros_like(dsink_ref)

        valid = _attn_mask(n)
        head_ids = lax.broadcasted_iota(jnp.int32, (1, n_q), 1)
        dsink = jnp.zeros((1, n_q), F32)
        for kvh in range(n_kv):
            kk, vv = _kv_pair(kvc_ref, kvp_ref, kvh, n_kv)
            dk = jnp.zeros((2 * ATT_BLOCK, HEAD_DIM), F32)
            dv = jnp.zeros((2 * ATT_BLOCK, HEAD_DIM), F32)
            for g in range(Q_PER_KV):
                h = kvh * Q_PER_KV + g
                hs = slice(h * HEAD_DIM, (h + 1) * HEAD_DIM)
                qh = q_ref[:, hs].astype(BF16)
                s = lax.dot_general(qh, kk, (((1,), (1,)), ((), ())), preferred_element_type=F32)
                s = jnp.where(valid, s * scale, -jnp.inf)
                lse_h = lse_ref[:, h:h + 1]
                p = jnp.exp(s - lse_h)
                doh = do_ref[:, hs].astype(BF16)
                dp = lax.dot_general(doh, vv, (((1,), (1,)), ((), ())), preferred_element_type=F32)
                delta = jnp.sum(p * dp, axis=1, keepdims=True)
                ds = (p * (dp - delta) * scale).astype(BF16)
                dsink = dsink + jnp.where(head_ids == h, -jnp.sum(jnp.exp(sink_ref[h] - lse_h) * delta), 0.0)
                dq_ref[:, hs] = jnp.dot(ds, kk, preferred_element_type=F32)
                dk = dk + lax.dot_general(ds, qh, (((0,), (0,)), ((), ())), preferred_element_type=F32)
                dv = dv + lax.dot_general(p.astype(BF16), doh, (((0,), (0,)), ((), ())), preferred_element_type=F32)
            ks = slice(kvh * HEAD_DIM, (kvh + 1) * HEAD_DIM)
            vs = slice((n_kv + kvh) * HEAD_DIM, (n_kv + kvh + 1) * HEAD_DIM)
            dkp_ref[:, ks] = dk[0:ATT_BLOCK]
            dkc_ref[:, ks] = dk[ATT_BLOCK:]
            dkp_ref[:, vs] = dv[0:ATT_BLOCK]
            dkc_ref[:, vs] = dv[ATT_BLOCK:]
        dsink_ref[...] += dsink

    out_specs = [blk_q, blk_kv, blk_kv, pl.BlockSpec((1, n_q), lambda n: (0, 0))]
    out_shape = [jax.ShapeDtypeStruct((rows, dq_w), F32), jax.ShapeDtypeStruct((rows, dkv_w), F32),
                 jax.ShapeDtypeStruct((rows, dkv_w), F32), jax.ShapeDtypeStruct((1, n_q), F32)]
    return pl.pallas_call(body, name=name, grid=(rows // ATT_BLOCK,), in_specs=in_specs, out_specs=out_specs,
                          out_shape=out_shape, compiler_params=_cp("arbitrary"))(sinks, qkv, qkv, qkv, dout, lse)


def _dqkv_assemble(dq, dkc, dkp, *, name):
    rows, dq_w = dq.shape
    dkv_w = dkc.shape[1]
    nb = rows // ATT_BLOCK

    def body(dq_ref, dkc_ref, dkp_ref, o_ref, db_ref):
        n = pl.program_id(0)

        @pl.when(n == 0)
        def _():
            db_ref[...] = jnp.zeros_like(db_ref)

        dqv = dq_ref[...]
        dkv = dkc_ref[...] + jnp.where(n == nb - 1, 0.0, dkp_ref[...])
        o_ref[:, 0:dq_w] = dqv.astype(BF16)
        o_ref[:, dq_w:dq_w + dkv_w] = dkv.astype(BF16)
        db_ref[:, 0:dq_w] += _rowsum(dqv)
        db_ref[:, dq_w:dq_w + dkv_w] += _rowsum(dkv)

    width = dq_w + dkv_w
    in_specs = [pl.BlockSpec((ATT_BLOCK, dq_w), lambda n: (n, 0)), pl.BlockSpec((ATT_BLOCK, dkv_w), lambda n: (n, 0)),
                pl.BlockSpec((ATT_BLOCK, dkv_w), lambda n: (jnp.minimum(n + 1, nb - 1), 0))]
    out_specs = [pl.BlockSpec((ATT_BLOCK, width), lambda n: (n, 0)), pl.BlockSpec((1, width), lambda n: (0, 0))]
    out_shape = [jax.ShapeDtypeStruct((rows, width), BF16), jax.ShapeDtypeStruct((1, width), F32)]
    return pl.pallas_call(body, name=name, grid=(nb,), in_specs=in_specs, out_specs=out_specs, out_shape=out_shape,
                          compiler_params=_cp("arbitrary"))(dq, dkc, dkp)


def _row_tile(r, c):
    budget = 2 * 1024 * 1024 // (4 * c)
    for cand in (1024, 512, 256, 128, 64, 32, 16):
        if cand <= budget and r % cand == 0:
            return cand
    return r


def _to_bf16(a, *, name):
    p, r, c = a.shape
    t = _row_tile(r, c)

    def body(a_ref, o_ref):
        o_ref[...] = a_ref[...].astype(BF16)

    blk = pl.BlockSpec((None, t, c), lambda k, i: (k, i, 0))
    return pl.pallas_call(body, name=name, grid=(p, r // t), in_specs=[blk], out_specs=blk,
                          out_shape=jax.ShapeDtypeStruct(a.shape, BF16), compiler_params=_cp("parallel", "parallel"))(a)


def _octo_sum(own, recv, place, dest, lead, *, name):
    _, _, r, c = own.shape
    t = _row_tile(r, c)
    lead_idx, buf_shape = lead

    def body(place_ref, own_ref, *rest):
        o_ref = rest[7] if dest is None else rest[8]
        acc = own_ref[...].astype(F32)
        for k in range(7):
            acc = acc + rest[k][...].astype(F32)
        o_ref[...] = acc

    def peer(mask):
        return pl.BlockSpec((None, t, c), lambda i, pr: (pr[2] ^ mask, i, 0))

    if lead_idx is None:
        o_spec = pl.BlockSpec((None, t, c), lambda i, pr: (pr[1], i, 0))
    else:
        o_spec = pl.BlockSpec((None, None, t, c), lambda i, pr: (lead_idx, pr[1], i, 0))
    in_specs = [pl.BlockSpec((None, None, t, c), lambda i, pr: (pr[0], pr[1], i, 0))] + [peer(m) for m in range(1, 8)]
    args = [place, own] + [recv] * 7
    aliases = {}
    if dest is not None:
        in_specs.append(HBM)
        args.append(dest)
        aliases = {9: 0}
    grid_spec = pltpu.PrefetchScalarGridSpec(num_scalar_prefetch=1, grid=(r // t,), in_specs=in_specs, out_specs=o_spec)
    return pl.pallas_call(body, name=name, grid_spec=grid_spec, out_shape=jax.ShapeDtypeStruct(buf_shape, F32),
                          input_output_aliases=aliases, compiler_params=_cp("parallel"))(*args)


def _adamw(w, g, m, v, *, name):
    r, c = w.shape
    t = _row_tile(r, c)

    def body(w_ref, g_ref, m_ref, v_ref, d_ref, nm_ref, nv_ref):
        gv = g_ref[...]
        nm = ADAM_B1 * m_ref[...] + (1.0 - ADAM_B1) * gv
        nv = ADAM_B2 * v_ref[...] + (1.0 - ADAM_B2) * (gv * gv)
        m_hat = nm / (1.0 - ADAM_B1 ** ADAM_STEP)
        v_hat = nv / (1.0 - ADAM_B2 ** ADAM_STEP)
        d_ref[...] = -ADAM_LR * (m_hat / (jnp.sqrt(v_hat) + ADAM_EPS) + ADAM_WD * w_ref[...])
        nm_ref[...] = nm
        nv_ref[...] = nv

    blk = pl.BlockSpec((t, c), lambda i: (i, 0))
    sds = jax.ShapeDtypeStruct((r, c), F32)
    return pl.pallas_call(body, name=name, grid=(r // t,), in_specs=[blk] * 4, out_specs=[blk] * 3,
                          out_shape=[sds] * 3, compiler_params=_cp("parallel"))(w, g, m, v)


HBM = pl.BlockSpec(memory_space=pl.ANY)


def _place():
    x, y, c = lax.axis_index("x"), lax.axis_index("y"), lax.axis_index("c")
    chips = [(1 - x, y), (x, 1 - y), (1 - x, 1 - y)]
    return x, y, c, 2 * x + y, (x, y, 1 - c), chips


def _rcopy(src, dst, ssem, rsem, dev):
    return pltpu.make_async_remote_copy(src_ref=src, dst_ref=dst, send_sem=ssem, recv_sem=rsem, device_id=dev,
                                        device_id_type=MESH)


HBM_ONLY = pl.BlockSpec(memory_space=pltpu.HBM)
SEM = pl.BlockSpec(memory_space=pltpu.SEMAPHORE)


def _peers():
    x, y, c = lax.axis_index("x"), lax.axis_index("y"), lax.axis_index("c")
    out = []
    for mask in range(1, 8):
        px = 1 - x if mask & 4 else x
        py = 1 - y if mask & 2 else y
        pc = 1 - c if mask & 1 else c
        out.append(((px, py, pc), 2 * px + py, pc, 4 * px + 2 * py + pc))
    return 4 * x + 2 * y + c, out


def _reduce_start(grads, lands, *, name):
    nt = len(grads)

    def body(*refs):
        ssems, rsems = refs[2 * nt:3 * nt], refs[3 * nt:4 * nt]
        g_out, l_out, token = refs[4 * nt:5 * nt], refs[5 * nt:6 * nt], refs[6 * nt]
        me, peers = _peers()
        for t in range(nt):
            for k, (dev, chip, core, _) in enumerate(peers):
                _rcopy(g_out[t].at[chip, core], l_out[t].at[me], ssems[t].at[k], rsems[t].at[k], dev).start()
        token[...] = jnp.zeros_like(token)

    sems = [pltpu.SemaphoreType.DMA((7,))] * (2 * nt)
    out_shape = (sems + [pltpu.HBM(g.shape, g.dtype) for g in grads] + [pltpu.HBM(l.shape, l.dtype) for l in lands]
                 + [jax.ShapeDtypeStruct((8, LANES), F32)])
    res = pl.pallas_call(
        body, name=name, in_specs=[HBM_ONLY] * (2 * nt),
        out_specs=[SEM] * (2 * nt) + [HBM_ONLY] * (2 * nt) + [pl.BlockSpec(memory_space=pltpu.VMEM)], out_shape=out_shape,
        input_output_aliases={t: 2 * nt + t for t in range(2 * nt)},
        compiler_params=pltpu.CompilerParams(has_side_effects=DATAFLOW),
    )(*[pltpu.with_memory_space_constraint(a, pltpu.HBM) for a in list(grads) + list(lands)])
    return res[:nt], res[nt:2 * nt], res[2 * nt:3 * nt], res[3 * nt:4 * nt], res[4 * nt]


def _reduce_wait(grads, lands, ssems, rsems, after, *, name):
    nt = len(grads)

    def body(*refs):
        ssem_refs, rsem_refs = refs[2 * nt:3 * nt], refs[3 * nt:4 * nt]
        g_out, l_out = refs[4 * nt + 1:5 * nt + 1], refs[5 * nt + 1:6 * nt + 1]
        me, peers = _peers()
        for t in range(nt):
            for k, (dev, chip, core, _) in enumerate(peers):
                _rcopy(g_out[t].at[chip, core], l_out[t].at[me], ssem_refs[t].at[k], rsem_refs[t].at[k], dev).wait_send()
        for t in range(nt):
            for k, (dev, _, _, idx) in enumerate(peers):
                slot = l_out[t].at[idx]
                _rcopy(slot, slot, ssem_refs[t].at[k], rsem_refs[t].at[k], dev).wait_recv()

    res = pl.pallas_call(
        body, name=name, in_specs=[HBM_ONLY] * (2 * nt) + [SEM] * (2 * nt) + [HBM], out_specs=[HBM_ONLY] * (2 * nt),
        out_shape=[pltpu.HBM(a.shape, a.dtype) for a in list(grads) + list(lands)],
        input_output_aliases={t: t for t in range(2 * nt)},
        compiler_params=pltpu.CompilerParams(has_side_effects=DATAFLOW),
    )(*grads, *lands, *ssems, *rsems, after)
    return list(res[:nt]), list(res[nt:])
DATAFLOW = pltpu.SideEffectType.DATAFLOW_SIDE_EFFECTING


def _gather_start(bufs, *, name):
    nt = len(bufs)

    def body(*refs):
        ssems, rsems, outs = refs[nt:2 * nt], refs[2 * nt:3 * nt], refs[3 * nt:4 * nt]
        x, y, c, q, sib, chips = _place()
        for t in range(nt):
            for j, (px, py) in enumerate(chips):
                mine = outs[t].at[q]
                _rcopy(mine, mine, ssems[t].at[j], rsems[t].at[j], (px, py, c)).start()

    sems = [pltpu.SemaphoreType.DMA((3,))] * (2 * nt)
    out_shape = sems + [pltpu.HBM(b.shape, b.dtype) for b in bufs]
    res = pl.pallas_call(
        body, name=name, in_specs=[HBM_ONLY] * nt, out_specs=[SEM] * (2 * nt) + [HBM_ONLY] * nt, out_shape=out_shape,
        input_output_aliases={t: 2 * nt + t for t in range(nt)},
        compiler_params=pltpu.CompilerParams(has_side_effects=DATAFLOW),
    )(*[pltpu.with_memory_space_constraint(b, pltpu.HBM) for b in bufs])
    return res[:nt], res[nt:2 * nt], res[2 * nt:]


def _gather_wait(bufs, ssems, rsems, after, *, name):
    nt = len(bufs)

    def body(*refs):
        ssem_refs, rsem_refs = refs[nt:2 * nt], refs[2 * nt:3 * nt]
        outs = refs[3 * nt + 1:]
        x, y, c, q, sib, chips = _place()
        for t in range(nt):
            for j, (px, py) in enumerate(chips):
                mine = outs[t].at[q]
                _rcopy(mine, mine, ssem_refs[t].at[j], rsem_refs[t].at[j], (px, py, c)).wait_send()
        for t in range(nt):
            for j, (px, py) in enumerate(chips):
                theirs = outs[t].at[2 * px + py]
                _rcopy(theirs, theirs, ssem_refs[t].at[j], rsem_refs[t].at[j], (px, py, c)).wait_recv()

    res = pl.pallas_call(
        body, name=name, in_specs=[HBM_ONLY] * nt + [SEM] * (2 * nt) + [HBM], out_specs=[HBM_ONLY] * nt,
        out_shape=[pltpu.HBM(b.shape, b.dtype) for b in bufs], input_output_aliases={t: t for t in range(nt)},
        compiler_params=pltpu.CompilerParams(has_side_effects=DATAFLOW),
    )(*bufs, *ssems, *rsems, after)
    return list(res)


def _sibling_share(bufs, layout, *, name):
    no = len(bufs)
    nt = len(layout)

    def body(*refs):
        outs = refs[no:2 * no]
        ssem, rsem = refs[2 * no:]
        x, y, c, q, sib, chips = _place()

        def slot(t, half):
            o, lead = layout[t]
            return outs[o].at[half] if lead is None else outs[o].at[lead, half]

        sends = []
        for t in range(nt):
            cp = _rcopy(slot(t, c), slot(t, c), ssem.at[t], rsem.at[t], sib)
            cp.start()
            sends.append(cp)
        for t in range(nt):
            _rcopy(slot(t, 1 - c), slot(t, 1 - c), ssem.at[t], rsem.at[t], sib).wait_recv()
        for cp in sends:
            cp.wait_send()

    out_shape = [jax.ShapeDtypeStruct(b.shape, b.dtype) for b in bufs]
    return pl.pallas_call(
        body, name=name, in_specs=[HBM] * no, out_specs=[HBM] * no, out_shape=out_shape,
        input_output_aliases={o: o for o in range(no)},
        scratch_shapes=[pltpu.SemaphoreType.DMA((nt,)), pltpu.SemaphoreType.DMA((nt,))],
    )(*bufs)


def _small_all_reduce(pack, *, name):
    r, lanes = pack.shape

    def body(in_ref, out_ref, gath, ssem, rsem):
        x, y, c = lax.axis_index("x"), lax.axis_index("y"), lax.axis_index("c")
        me = 4 * x + 2 * y + c
        gath[me] = in_ref[...]
        sends = []
        for mask in range(1, 8):
            px = 1 - x if mask & 4 else x
            py = 1 - y if mask & 2 else y
            pc = 1 - c if mask & 1 else c
            cp = _rcopy(in_ref, gath.at[me], ssem.at[mask - 1], rsem.at[mask - 1], (px, py, pc))
            cp.start()
            sends.append(cp)
        for mask in range(1, 8):
            px = 1 - x if mask & 4 else x
            py = 1 - y if mask & 2 else y
            pc = 1 - c if mask & 1 else c
            peer = gath.at[4 * px + 2 * py + pc]
            _rcopy(peer, peer, ssem.at[mask - 1], rsem.at[mask - 1], (px, py, pc)).wait_recv()
        acc = gath[0]
        for d in range(1, 8):
            acc = acc + gath[d]
        out_ref[...] = acc
        for cp in sends:
            cp.wait_send()

    vm = pl.BlockSpec(memory_space=pltpu.VMEM)
    return pl.pallas_call(
        body, name=name, in_specs=[vm], out_specs=vm, out_shape=jax.ShapeDtypeStruct((r, lanes), F32),
        scratch_shapes=[pltpu.VMEM((8, r, lanes), F32), pltpu.SemaphoreType.DMA((7,)), pltpu.SemaphoreType.DMA((7,))],
        compiler_params=pltpu.CompilerParams(vmem_limit_bytes=VMEM_LIMIT),
    )(pack)


def _pack(arrays, rows_multiple):
    flat = jnp.concatenate([a.reshape(-1) for a in arrays])
    rows = -(-flat.shape[0] // LANES)
    rows = -(-rows // rows_multiple) * rows_multiple
    flat = jnp.pad(flat, (0, rows * LANES - flat.shape[0]))
    return flat.reshape(rows, LANES)


def _unpack(buf, shapes):
    flat = buf.reshape(-1)
    out, pos = [], 0
    for s in shapes:
        n = math.prod(s)
        out.append(flat[pos:pos + n].reshape(s))
        pos += n
    return out


def _unshard_cols(stacked):
    moved = jnp.moveaxis(stacked, 0, -2)
    return moved.reshape(moved.shape[:-2] + (4 * stacked.shape[-1],))


def _shard_cols(full, q):
    n = full.shape[-1] // 4
    return lax.dynamic_slice_in_dim(full, q * n, n, axis=full.ndim - 1)


def kernel(x, ab_w_in, a_conv_w, a_conv_b, a_norm_g, a_norm_b, b_norm_g, b_norm_b, b_spatial_w, b_spatial_b, ab_w_out, c_w_qkv, c_b_qkv, c_sinks, c_w_o, ffn_w_up, ffn_conv_w, ffn_conv_b, ffn_w_down, ln_g, ln_b, loss_target, m_ab_w_in, m_a_conv_w, m_a_conv_b, m_a_norm_g, m_a_norm_b, m_b_norm_g, m_b_norm_b, m_b_spatial_w, m_b_spatial_b, m_ab_w_out, m_c_w_qkv, m_c_b_qkv, m_c_sinks, m_c_w_o, m_ffn_w_up, m_ffn_conv_w, m_ffn_conv_b, m_ffn_w_down, m_ln_g, m_ln_b, v_ab_w_in, v_a_conv_w, v_a_conv_b, v_a_norm_g, v_a_norm_b, v_b_norm_g, v_b_norm_b, v_b_spatial_w, v_b_spatial_b, v_ab_w_out, v_c_w_qkv, v_c_b_qkv, v_c_sinks, v_c_w_o, v_ffn_w_up, v_ffn_conv_w, v_ffn_conv_b, v_ffn_w_down, v_ln_g, v_ln_b):
    rows, d = x.shape[1], x.shape[2]
    depth = ln_g.shape[0]
    assert depth == 2 and x.shape[0] == 1
    alpha = (2.0 * depth) ** 0.25
    f = ffn_w_down.shape[1] * 4
    n_q = c_sinks.shape[1]
    q_idx = 2 * lax.axis_index("x") + lax.axis_index("y")
    c_idx = lax.axis_index("c")
    xs, tgt = x[0], loss_target[0]

    def own_slot(part):
        buf = lax.empty((4,) + part.shape, part.dtype)
        return lax.dynamic_update_slice(buf, part[None], (q_idx, 0, 0, 0))

    def halves(wm):
        return own_slot(wm.astype(BF16).reshape((2, wm.shape[0] // 2) + wm.shape[1:]))

    small_sharded = [a_conv_w[0], c_b_qkv[0], ffn_conv_w, ln_g, ln_b]
    small_pack = _pack(small_sharded, 16)
    bufs = [halves(ab_w_in[0]), own_slot(small_pack.reshape(2, small_pack.shape[0] // 2, LANES)), halves(ab_w_out[0]),
            halves(ffn_w_up[0]), halves(ffn_w_down[0]), halves(c_w_qkv[0]), halves(c_w_o[0]),
            halves(ffn_w_up[1]), halves(ffn_w_down[1])]
    ssems, rsems, started = _gather_start(bufs, name="gather_start")

    def arrive(idx, after, tag):
        got = _gather_wait([started[i] for i in idx], [ssems[i] for i in idx], [rsems[i] for i in idx], after,
                           name=f"gather_wait_{tag}")
        return [g.reshape(4, 2 * g.shape[2], g.shape[3]) for g in got]

    w_in, small_all = arrive([0, 1], xs, "in")
    small_all = small_all.reshape(4, -1)
    sh_shapes = [s.shape for s in small_sharded]
    pieces, pos = [], 0
    for s in sh_shapes:
        n = math.prod(s)
        pieces.append(_unshard_cols(small_all[:, pos:pos + n].reshape((4,) + s)))
        pos += n
    conv_w_a, b_qkv, conv_w_f, ln_gf, ln_bf = pieces

    tril = jnp.tril(jnp.ones((B_CHUNK, B_CHUNK), F32))
    ws = (b_spatial_w[0] * tril).astype(BF16)
    wst = jnp.swapaxes(ws, 1, 2)
    sbb = jnp.broadcast_to(b_spatial_b[0][:, :, None], b_spatial_w[0].shape)
    mix_vecs = [a_conv_b, a_norm_g, a_norm_b, b_norm_g, b_norm_b]
    cw_f = [jnp.swapaxes(conv_w_f[l].reshape(3, 2, f), 0, 1) for l in range(depth)]
    cb_f = [ffn_conv_b[l].reshape(2, 1, f) for l in range(depth)]
    lng = lambda i, j: ln_gf[i, j].reshape(1, d)
    lnb = lambda i, j: ln_bf[i, j].reshape(1, d)
    sinks = c_sinks[0]

    w_up, w_down = [None, None], [None, None]

    def ffn_fwd(xin, l):
        w_up[l], = arrive([3 + 4 * l], xin, f"up{l}")
        hf, fact = _ffn_up_fwd(xin, w_up[l], cw_f[l], cb_f[l], name=f"ffn{l}_up")
        w_down[l] = arrive([4 + 4 * l], fact, f"down{l}")[0].reshape(-1, d)
        out = _matmul(fact, w_down[l], name=f"ffn{l}_down", tm=1024, tn=1024, tk=1408)
        return hf, fact, out

    h0 = _matmul(xs, w_in, name="mix_in", tm=1024, tn=512, tk=1024, out_stack=4)
    ab = _mixer_fwd(h0, conv_w_a, *mix_vecs, ws, sbb, name="mix_mid")
    w_out = arrive([2], ab, "out")[0].reshape(-1, d)
    mix = _matmul(ab, w_out, name="mix_out", tm=1024, tn=1024, tk=1024)
    x1 = _add_ln_fwd(xs, mix, lng(0, 0), lnb(0, 0), alpha, name="ln00")
    hf0, f0, ffn0 = ffn_fwd(x1, 0)
    x2 = _add_ln_fwd(x1, ffn0, lng(0, 1), lnb(0, 1), alpha, name="ln01")
    w_qkv = _unshard_cols(arrive([5], x2, "qkv")[0])
    qkv = _matmul(x2, w_qkv, name="att_qkv", tm=1024, tn=w_qkv.shape[1], tk=1024, bias=b_qkv.reshape(1, -1))
    ao, lse = _attn_fwd(qkv, sinks, name="att_core")
    w_o = arrive([6], ao, "o")[0].reshape(-1, d)
    att = _matmul(ao, w_o, name="att_out", tm=1024, tn=1024, tk=1024)
    x3 = _add_ln_fwd(x2, att, lng(1, 0), lnb(1, 0), alpha, name="ln10")
    hf1, f1, ffn1 = ffn_fwd(x3, 1)
    x4 = _add_ln_fwd(x3, ffn1, lng(1, 1), lnb(1, 1), alpha, name="ln11")
    sq_err, dy = _loss_and_grad(x4, tgt, name="loss")

    def owner_view(g):
        if g.ndim == 3:
            return g.reshape(4, 2, g.shape[1] // 2, g.shape[2])
        return g.reshape(4, 2, g.shape[0] // 8, g.shape[1])

    in_flight = []

    def send_grads(tag, grads):
        lands = [lax.empty((8,) + g.shape[2:], BF16) for g in grads]
        ss, rs, g_thru, l_thru, token = _reduce_start(grads, lands, name=f"reduce_start_{tag}")
        in_flight.append((tag, g_thru, l_thru, ss, rs))
        return token[0:1, 0:1]

    def ffn_bwd(dz, xin, hf, fact, l):
        d_wdown = _matmul(fact, dz, name=f"ffn{l}_down_dw", ta=True, tm=1408, tn=1024, tk=512, out_dtype=BF16)
        dfa = _matmul(dz, w_down[l], name=f"ffn{l}_down_dx", tb=True, tm=1024, tn=1408, tk=1024)
        dx_parts, d_wup, dcw, dcb = _ffn_up_bwd(hf, dfa, xin, w_up[l], cw_f[l], cb_f[l], name=f"ffn{l}_up_bwd")
        d_wup = _to_bf16(d_wup, name=f"ffn{l}_up_dw_bf16")
        tok = send_grads(f"ffn{l}", [owner_view(d_wup), owner_view(d_wdown)])
        return [(dx_parts, 1.0), (dz, alpha)], dcw, dcb, tok

    dz, dg11, db11 = _add_ln_bwd([(dy, 1.0)], x3, ffn1, lng(1, 1), alpha, name="ln11_bwd")
    dx3, dcw1, dcb1, tok = ffn_bwd(dz, x3, hf1, f1, 1)
    dz, dg10, db10 = _add_ln_bwd(dx3, x2, att, lng(1, 0) + tok, alpha, name="ln10_bwd")
    d_wo = _matmul(ao, dz, name="att_out_dw", ta=True, tm=1024, tn=1024, tk=1024, out_dtype=BF16)
    dao = _matmul(dz, w_o, name="att_out_dx", tb=True, tm=1024, tn=1024, tk=1024)
    dq, dkc, dkp, d_sinks = _attn_bwd(qkv, dao, lse, sinks, name="att_core_bwd")
    dqkv, d_bqkv = _dqkv_assemble(dq, dkc, dkp, name="att_dqkv")
    d_wqkv = _matmul(x2, dqkv, name="att_qkv_dw", ta=True, tm=1024, tn=dqkv.shape[1], tk=1024, out_dtype=BF16)
    d_wqkv_st = jnp.moveaxis(d_wqkv.reshape(d_wqkv.shape[0], 4, -1), 1, 0)
    tok = send_grads("att", [owner_view(d_wqkv_st), owner_view(d_wo)])
    dx2 = _matmul(dqkv, w_qkv, name="att_qkv_dx", tb=True, tm=1024, tn=1024, tk=dqkv.shape[1], addend=(dz, alpha))
    dz, dg01, db01 = _add_ln_bwd([(dx2, 1.0)], x1, ffn0, lng(0, 1) + tok, alpha, name="ln01_bwd")
    dx1, dcw0, dcb0, tok = ffn_bwd(dz, x1, hf0, f0, 0)
    dz, dg00, db00 = _add_ln_bwd(dx1, xs, mix, lng(0, 0) + tok, alpha, name="ln00_bwd")
    d_wout = _matmul(ab, dz, name="mix_out_dw", ta=True, tm=1024, tn=1024, tk=1024, out_dtype=BF16)
    dab = _matmul(dz, w_out, name="mix_out_dx", tb=True, tm=1024, tn=1024, tk=1024)
    dh0, d_cwa, d_cba, d_ga, d_ba, d_gb, d_bb, d_ws, d_sb = _mixer_bwd(
        h0, dab, conv_w_a, *mix_vecs, ws, wst, sbb, tril, name="mix_mid_bwd")
    d_win = _matmul(xs, dh0, name="mix_in_dw", ta=True, tm=1024, tn=512, tk=1024, out_stack=4, out_dtype=BF16)
    tok = send_grads("mix", [owner_view(d_win), owner_view(d_wout)])
    grad_x = _matmul(dh0, w_in, name="mix_in_dx", tb=True, tm=1024, tn=1024, tk=512, addend=(dz, alpha),
                     bias=jnp.broadcast_to(tok, (1, d)))

    place = jnp.stack([q_idx, c_idx, 4 * lax.axis_index("x") + 2 * lax.axis_index("y") + c_idx]).astype(jnp.int32)
    where = {"mix": [(0, None), (1, None)], "att": [(2, None), (3, None)], "ffn0": [(4, 0), (5, 0)], "ffn1": [(4, 1), (5, 1)]}
    shard_bufs = [None] * 6
    layout = []
    for tag, g_thru, l_thru, ss, rs in in_flight:
        own, landed = _reduce_wait(g_thru, l_thru, ss, rs, grad_x, name=f"reduce_wait_{tag}")
        for k, (o, lead) in enumerate(where[tag]):
            piece = own[k].shape[2:]
            shape = (2,) + piece if lead is None else (2, 2) + piece
            shard_bufs[o] = _octo_sum(own[k], landed[k], place, shard_bufs[o], (lead, shape), name=f"reduce_sum_{tag}{k}")
            layout.append((o, lead))
    shared = _sibling_share(shard_bufs, layout, name="reduce_share")
    g_win = shared[0].reshape(ab_w_in.shape)
    g_wout = shared[1].reshape(ab_w_out.shape)
    g_wqkv = shared[2].reshape(c_w_qkv.shape)
    g_wo = shared[3].reshape(c_w_o.shape)
    g_wup = shared[4].reshape(ffn_w_up.shape)
    g_wdown = shared[5].reshape(ffn_w_down.shape)

    d_cw_f = jnp.stack([jnp.swapaxes(dcw0, 0, 1).reshape(3, 2 * f), jnp.swapaxes(dcw1, 0, 1).reshape(3, 2 * f)])
    d_cb_f = jnp.stack([dcb0.reshape(2 * f), dcb1.reshape(2 * f)])
    d_lng = jnp.stack([jnp.stack([dg00[0], dg01[0]]), jnp.stack([dg10[0], dg11[0]])])
    d_lnb = jnp.stack([jnp.stack([db00[0], db01[0]]), jnp.stack([db10[0], db11[0]])])
    small_full = [d_cwa, d_cba, d_ga, d_ba, d_gb, d_bb, d_ws, d_sb, d_bqkv, d_sinks, d_cw_f, d_cb_f, d_lng, d_lnb, sq_err]
    reduced = _small_all_reduce(_pack(small_full, 8), name="reduce_small")
    (r_cwa, r_cba, r_ga, r_ba, r_gb, r_bb, r_ws, r_sb, r_bqkv, r_sinks, r_cwf, r_cbf, r_lng, r_lnb, r_err) = _unpack(
        reduced, [a.shape for a in small_full])
    loss = 0.5 * jnp.sum(r_err) / d

    small_names_w = [a_conv_w, a_conv_b, a_norm_g, a_norm_b, b_norm_g, b_norm_b, b_spatial_w, b_spatial_b, c_b_qkv,
                     c_sinks, ffn_conv_w, ffn_conv_b, ln_g, ln_b]
    small_m = [m_a_conv_w, m_a_conv_b, m_a_norm_g, m_a_norm_b, m_b_norm_g, m_b_norm_b, m_b_spatial_w, m_b_spatial_b,
               m_c_b_qkv, m_c_sinks, m_ffn_conv_w, m_ffn_conv_b, m_ln_g, m_ln_b]
    small_v = [v_a_conv_w, v_a_conv_b, v_a_norm_g, v_a_norm_b, v_b_norm_g, v_b_norm_b, v_b_spatial_w, v_b_spatial_b,
               v_c_b_qkv, v_c_sinks, v_ffn_conv_w, v_ffn_conv_b, v_ln_g, v_ln_b]
    small_g = [_shard_cols(r_cwa, q_idx), r_cba, r_ga, r_ba, r_gb, r_bb, r_ws, r_sb, _shard_cols(r_bqkv, q_idx), r_sinks,
               _shard_cols(r_cwf, q_idx), r_cbf, _shard_cols(r_lng, q_idx), _shard_cols(r_lnb, q_idx)]
    small_g = [g.reshape(w.shape) for g, w in zip(small_g, small_names_w)]
    sm_shapes = [w.shape for w in small_names_w]
    sm_delta, sm_m, sm_v = _adamw(_pack(small_names_w, 8), _pack(small_g, 8), _pack(small_m, 8), _pack(small_v, 8),
                                  name="adamw_small")
    sm_delta, sm_m, sm_v = _unpack(sm_delta, sm_shapes), _unpack(sm_m, sm_shapes), _unpack(sm_v, sm_shapes)

    def adamw_big(w, g, m, v, name):
        two_d = lambda a: a.reshape(-1, a.shape[-1])
        outs = _adamw(two_d(w), two_d(g), two_d(m), two_d(v), name=name)
        return [o.reshape(w.shape) for o in outs]

    big_w = [ab_w_in, ab_w_out, c_w_qkv, c_w_o, ffn_w_up, ffn_w_down]
    big_g = [g_win, g_wout, g_wqkv, g_wo, g_wup, g_wdown]
    big_m = [m_ab_w_in, m_ab_w_out, m_c_w_qkv, m_c_w_o, m_ffn_w_up, m_ffn_w_down]
    big_v = [v_ab_w_in, v_ab_w_out, v_c_w_qkv, v_c_w_o, v_ffn_w_up, v_ffn_w_down]
    big_out = [adamw_big(w, g, m, v, f"adamw_big{t}") for t, (w, g, m, v) in enumerate(zip(big_w, big_g, big_m, big_v))]

    order_big = {0: 0, 9: 1, 10: 2, 13: 3, 14: 4, 17: 5}
    order_small = {1: 0, 2: 1, 3: 2, 4: 3, 5: 4, 6: 5, 7: 6, 8: 7, 11: 8, 12: 9, 15: 10, 16: 11, 18: 12, 19: 13}
    grads, deltas, new_m, new_v = [], [], [], []
    for pos_w in range(20):
        if pos_w in order_big:
            t = order_big[pos_w]
            grads.append(big_g[t])
            deltas.append(big_out[t][0])
            new_m.append(big_out[t][1])
            new_v.append(big_out[t][2])
        else:
            t = order_small[pos_w]
            grads.append(small_g[t])
            deltas.append(sm_delta[t])
            new_m.append(sm_m[t])
            new_v.append(sm_v[t])
    return (loss, grad_x[None], *grads, *deltas, *new_m, *new_v)
```

```python
import functools
import math

import jax
import jax.numpy as jnp
from jax import lax
from jax.experimental import pallas as pl
from jax.experimental.pallas import tpu as pltpu

F32 = jnp.float32
BF16 = jnp.bfloat16
MESH = pl.DeviceIdType.MESH

LN_EPS = 1e-5
HEAD_DIM = 64
ATT_BLOCK = 128
Q_PER_KV = 8
A_KERNEL = 31
CONV_HALO = 32
FFN_HALO = 8
B_CHUNK = 128
LANES = 128
MXU_WIDTH = 256
GELU_C = math.sqrt(2.0 / math.pi)
ADAM_LR = 0.001
ADAM_B1 = 0.9
ADAM_B2 = 0.999
ADAM_EPS = 1e-08
ADAM_WD = 0.01
ADAM_STEP = 10
VMEM_LIMIT = 56 * 1024 * 1024


def _cp(*dims):
    return pltpu.CompilerParams(dimension_semantics=dims, vmem_limit_bytes=VMEM_LIMIT)


def _pick(n, prefs):
    for p in prefs:
        if n % p == 0:
            return p
    return n


def _sig(x):
    return 1.0 / (1.0 + jnp.exp(-x))


def _gelu(x):
    t = jnp.tanh(GELU_C * (x + 0.044715 * (x * x * x)))
    return x * (0.5 * (1.0 + t)), t


def _gelu_grad(x, t):
    return 0.5 * (1.0 + t) + 0.5 * x * (1.0 - t * t) * (GELU_C * (1.0 + 3.0 * 0.044715 * x * x))


def _ln_stats(z):
    mu = jnp.mean(z, axis=-1, keepdims=True)
    zc = z - mu
    var = jnp.mean(zc * zc, axis=-1, keepdims=True)
    rstd = lax.rsqrt(var + LN_EPS)
    return zc * rstd, rstd


def _ln_bwd(dxh, xh, rstd):
    return rstd * (dxh - jnp.mean(dxh, axis=-1, keepdims=True) - xh * jnp.mean(dxh * xh, axis=-1, keepdims=True))


def _rowsum(a):
    return jnp.sum(a, axis=0, keepdims=True)


def _lshape(a):
    return (a.shape[0], a.shape[1]) if a.ndim == 2 else (a.shape[1], a.shape[0] * a.shape[2])


def _spec2(arr, blk_r, blk_c, ridx, cidx):
    if len(arr.shape) == 2:
        return pl.BlockSpec((blk_r, blk_c), lambda i, j, k: (ridx(i, j, k), cidx(i, j, k)))
    per = arr.shape[2] // blk_c
    assert arr.shape[2] % blk_c == 0
    return pl.BlockSpec((None, blk_r, blk_c), lambda i, j, k: (cidx(i, j, k) // per, ridx(i, j, k), cidx(i, j, k) % per))


def _matmul(a, b, *, name, ta=False, tb=False, tm, tn, tk, out_dtype=F32, out_stack=None, bias=None, addend=None):
    ar, ac = _lshape(a)
    br, bc = _lshape(b)
    m, kdim = (ac, ar) if ta else (ar, ac)
    n = br if tb else bc
    assert (bc if tb else br) == kdim
    tm, tn, tk = min(tm, m), min(tn, n), min(tk, kdim)
    assert m % tm == 0 and n % tn == 0 and kdim % tk == 0, (name, m, n, kdim, tm, tn, tk)
    nk = kdim // tk
    gi, gj, gk = (lambda i, j, k: i), (lambda i, j, k: j), (lambda i, j, k: k)
    a_spec = _spec2(a, tk, tm, gk, gi) if ta else _spec2(a, tm, tk, gi, gk)
    b_spec = _spec2(b, tn, tk, gj, gk) if tb else _spec2(b, tk, tn, gk, gj)
    if out_stack is None:
        out_sds = jax.ShapeDtypeStruct((m, n), out_dtype)
    else:
        out_sds = jax.ShapeDtypeStruct((out_stack, m, n // out_stack), out_dtype)
    o_spec = _spec2(out_sds, tm, tn, gi, gj)
    in_specs = [a_spec, b_spec]
    args = [a, b]
    if bias is not None:
        in_specs.append(pl.BlockSpec((1, tn), lambda i, j, k: (0, j)))
        args.append(bias)
    scale = None
    if addend is not None:
        add_arr, scale = addend
        in_specs.append(pl.BlockSpec((tm, tn), lambda i, j, k: (i, j)))
        args.append(add_arr)
    use_acc = nk > 1 and out_dtype != F32
    dn = (((0 if ta else 1,), (1 if tb else 0,)), ((), ()))

    def body(*refs):
        a_ref, b_ref = refs[0], refs[1]
        pos = 2
        bias_ref = add_ref = None
        if bias is not None:
            bias_ref = refs[pos]
            pos += 1
        if addend is not None:
            add_ref = refs[pos]
            pos += 1
        o_ref = refs[pos]
        acc_ref = refs[pos + 1] if use_acc else o_ref
        p = lax.dot_general(a_ref[...].astype(BF16), b_ref[...].astype(BF16), dn, preferred_element_type=F32)

        def finish(val):
            if bias_ref is not None:
                val = val + bias_ref[...]
            if add_ref is not None:
                val = val + scale * add_ref[...]
            return val.astype(out_dtype)

        if nk == 1:
            o_ref[...] = finish(p)
        else:
            k = pl.program_id(2)

            @pl.when(k == 0)
            def _():
                acc_ref[...] = p

            @pl.when(k > 0)
            def _():
                acc_ref[...] += p

            if use_acc or bias_ref is not None or add_ref is not None:
                @pl.when(k == nk - 1)
                def _():
                    o_ref[...] = finish(acc_ref[...])

    return pl.pallas_call(
        body, name=name, grid=(m // tm, n // tn, nk), in_specs=in_specs, out_specs=o_spec, out_shape=out_sds,
        scratch_shapes=[pltpu.VMEM((tm, tn), F32)] if use_acc else [],
        compiler_params=_cp("parallel", "parallel", "arbitrary"),
    )(*args)


def _add_ln_fwd(x, s, g, b, alpha, *, name):
    rows, d = x.shape
    t = _pick(rows, (512, 256))

    def body(x_ref, s_ref, g_ref, b_ref, y_ref):
        xh, _ = _ln_stats(alpha * x_ref[...] + s_ref[...])
        y_ref[...] = xh * g_ref[...] + b_ref[...]

    row = pl.BlockSpec((t, d), lambda i: (i, 0))
    vec = pl.BlockSpec((1, d), lambda i: (0, 0))
    return pl.pallas_call(body, name=name, grid=(rows // t,), in_specs=[row, row, vec, vec], out_specs=row,
                          out_shape=jax.ShapeDtypeStruct((rows, d), F32), compiler_params=_cp("parallel"))(x, s, g, b)


def _add_ln_bwd(dy_terms, x, s, g, alpha, *, name):
    rows, d = x.shape
    t = _pick(rows, (512, 256))
    nterm = len(dy_terms)
    scales = [sc for _, sc in dy_terms]
    ranks = [a.ndim for a, _ in dy_terms]

    def body(*refs):
        dy_refs = refs[:nterm]
        x_ref, s_ref, g_ref, dz_ref, dg_ref, db_ref = refs[nterm:]

        @pl.when(pl.program_id(0) == 0)
        def _():
            dg_ref[...] = jnp.zeros_like(dg_ref)
            db_ref[...] = jnp.zeros_like(db_ref)

        dyv = None
        for r, sc, rank in zip(dy_refs, scales, ranks):
            slabs = [r[...]] if rank == 2 else [r[p] for p in range(r.shape[0])]
            for v in slabs:
                v = v if sc == 1.0 else sc * v
                dyv = v if dyv is None else dyv + v
        xh, rstd = _ln_stats(alpha * x_ref[...] + s_ref[...])
        dz_ref[...] = _ln_bwd(dyv * g_ref[...], xh, rstd)
        dg_ref[...] += _rowsum(dyv * xh)
        db_ref[...] += _rowsum(dyv)

    row = pl.BlockSpec((t, d), lambda i: (i, 0))
    vec = pl.BlockSpec((1, d), lambda i: (0, 0))
    vsds = jax.ShapeDtypeStruct((1, d), F32)
    dy_specs = [row if a.ndim == 2 else pl.BlockSpec((a.shape[0], t, d), lambda i: (0, i, 0)) for a, _ in dy_terms]
    return pl.pallas_call(body, name=name, grid=(rows // t,), in_specs=dy_specs + [row, row, vec], out_specs=[row, vec, vec],
                          out_shape=[jax.ShapeDtypeStruct((rows, d), F32), vsds, vsds],
                          compiler_params=_cp("arbitrary"))(*[a for a, _ in dy_terms], x, s, g)


def _loss_and_grad(y, tgt, *, name):
    rows, d = y.shape
    t = _pick(rows, (512, 256))

    def body(y_ref, t_ref, l_ref, dy_ref):
        @pl.when(pl.program_id(0) == 0)
        def _():
            l_ref[...] = jnp.zeros_like(l_ref)

        e = y_ref[...] - t_ref[...]
        l_ref[...] += _rowsum(e * e)
        dy_ref[...] = e * (1.0 / d)

    row = pl.BlockSpec((t, d), lambda i: (i, 0))
    vec = pl.BlockSpec((1, d), lambda i: (0, 0))
    return pl.pallas_call(body, name=name, grid=(rows // t,), in_specs=[row, row], out_specs=[vec, row],
                          out_shape=[jax.ShapeDtypeStruct((1, d), F32), jax.ShapeDtypeStruct((rows, d), F32)],
                          compiler_params=_cp("arbitrary"))(y, tgt)


def _col_blocks(width):
    out, pos = [], 0
    while pos < width:
        w = MXU_WIDTH if width - pos >= MXU_WIDTH else width - pos
        out.append(slice(pos, pos + w))
        pos += w
    return out


def _conv3(e, w, b):
    r1 = pltpu.roll(e, 1, 0)
    r2 = pltpu.roll(e, 2, 0)
    return w[0:1, :] * r2 + w[1:2, :] * r1 + w[2:3, :] * e + b, (r2, r1, e)


def _ffn_up_fwd(x, w_up, cw, cb, *, name):
    rows, d = x.shape
    nq, _, tc = w_up.shape
    nj = nq // 2
    f = tc * nj
    tm = _pick(rows, (512, 256))
    blocks = _col_blocks(tc)

    def body(x_ref, wg_ref, wv_ref, cw_ref, cb_ref, hf_ref, f_ref, prev_ref):
        @pl.when(pl.program_id(1) == 0)
        def _():
            prev_ref[...] = jnp.zeros_like(prev_ref)

        xb = x_ref[...].astype(BF16)
        for cs in blocks:
            hc = []
            for s, w_ref in ((0, wg_ref), (1, wv_ref)):
                h = jnp.dot(xb, w_ref[:, cs], preferred_element_type=F32)
                hf_ref[s, :, cs] = h
                e = jnp.concatenate([prev_ref[s, :, cs], h], axis=0)
                prev_ref[s, :, cs] = h[tm - FFN_HALO:]
                y, _ = _conv3(e, cw_ref[s, :, cs], cb_ref[s, :, cs])
                hc.append(y[FFN_HALO:])
            gl, _ = _gelu(hc[0])
            f_ref[:, cs] = (gl * hc[1]).astype(BF16)

    in_specs = [
        pl.BlockSpec((tm, d), lambda j, i: (i, 0)),
        pl.BlockSpec((None, d, tc), lambda j, i: (j, 0, 0)),
        pl.BlockSpec((None, d, tc), lambda j, i: (nj + j, 0, 0)),
        pl.BlockSpec((2, 3, tc), lambda j, i: (0, 0, j)),
        pl.BlockSpec((2, 1, tc), lambda j, i: (0, 0, j)),
    ]
    out_specs = [pl.BlockSpec((2, tm, tc), lambda j, i: (0, i, j)), pl.BlockSpec((tm, tc), lambda j, i: (i, j))]
    out_shape = [jax.ShapeDtypeStruct((2, rows, f), F32), jax.ShapeDtypeStruct((rows, f), BF16)]
    return pl.pallas_call(body, name=name, grid=(nj, rows // tm), in_specs=in_specs, out_specs=out_specs, out_shape=out_shape,
                          scratch_shapes=[pltpu.VMEM((2, FFN_HALO, tc), F32)],
                          compiler_params=_cp("parallel", "arbitrary"))(x, w_up, w_up, cw, cb)


def _ffn_up_bwd(hf, df, x, w_up, cw, cb, *, name):
    _, rows, f = hf.shape
    d = x.shape[1]
    nq, _, tc = w_up.shape
    nj = nq // 2
    tm = _pick(rows, (512, 256))
    hb = tm // FFN_HALO
    once = pl.Buffered(1)
    ni = rows // tm
    last_blk = rows // FFN_HALO - 1
    ext = tm + 2 * FFN_HALO
    tile = slice(FFN_HALO, FFN_HALO + tm)
    blocks = _col_blocks(tc)

    def body(h_ref, hp_ref, hn_ref, d_ref, dn_ref, x_ref, wg_ref, wv_ref, cw_ref, cb_ref, dx_ref, dw_ref, dcw_ref, dcb_ref):
        i = pl.program_id(1)
        first = i == 0
        last = i == ni - 1

        @pl.when(first)
        def _():
            dw_ref[...] = jnp.zeros_like(dw_ref)
            dcw_ref[...] = jnp.zeros_like(dcw_ref)
            dcb_ref[...] = jnp.zeros_like(dcb_ref)

        xt = x_ref[...].astype(BF16).T
        dx = None
        for cs in blocks:
            wc = cs.stop - cs.start
            de = jnp.concatenate([jnp.zeros((FFN_HALO, wc), F32), d_ref[:, cs], jnp.where(last, 0.0, dn_ref[:, cs])], axis=0)
            taps, hc = [], []
            for s in range(2):
                e = jnp.concatenate([jnp.where(first, 0.0, hp_ref[s, :, cs]), h_ref[s, :, cs], hn_ref[s, :, cs]], axis=0)
                y, tp = _conv3(e, cw_ref[s, :, cs], cb_ref[s, :, cs])
                hc.append(y)
                taps.append(tp)
            gl, th = _gelu(hc[0])
            dhc = (de * hc[1] * _gelu_grad(hc[0], th), de * gl)
            for s, w_ref in ((0, wg_ref), (1, wv_ref)):
                w = cw_ref[s, :, cs]
                g = dhc[s]
                dh = (w[2:3, :] * g + w[1:2, :] * pltpu.roll(g, ext - 1, 0) + w[0:1, :] * pltpu.roll(g, ext - 2, 0))[tile]
                gt = g[tile]
                for k in range(3):
                    dcw_ref[s, k:k + 1, cs] += _rowsum(gt * taps[s][k][tile])
                dcb_ref[s, :, cs] += _rowsum(gt)
                dhb = dh.astype(BF16)
                part = lax.dot_general(dhb, w_ref[:, cs], (((1,), (1,)), ((), ())), preferred_element_type=F32)
                dx = part if dx is None else dx + part
                dw_ref[s, :, cs] += jnp.dot(xt, dhb, preferred_element_type=F32)
        dx_ref[...] = dx

    in_specs = [
        pl.BlockSpec((2, tm, tc), lambda j, i: (0, i, j)),
        pl.BlockSpec((2, FFN_HALO, tc), lambda j, i: (0, jnp.maximum(i * hb - 1, 0), j)),
        pl.BlockSpec((2, FFN_HALO, tc), lambda j, i: (0, jnp.minimum((i + 1) * hb, last_blk), j)),
        pl.BlockSpec((tm, tc), lambda j, i: (i, j)),
        pl.BlockSpec((FFN_HALO, tc), lambda j, i: (jnp.minimum((i + 1) * hb, last_blk), j)),
        pl.BlockSpec((tm, d), lambda j, i: (i, 0)),
        pl.BlockSpec((None, d, tc), lambda j, i: (j, 0, 0), pipeline_mode=once),
        pl.BlockSpec((None, d, tc), lambda j, i: (nj + j, 0, 0), pipeline_mode=once),
        pl.BlockSpec((2, 3, tc), lambda j, i: (0, 0, j)),
        pl.BlockSpec((2, 1, tc), lambda j, i: (0, 0, j)),
    ]
    out_specs = [
        pl.BlockSpec((None, tm, d), lambda j, i: (j, i, 0)),
        pl.BlockSpec((2, None, d, tc), lambda j, i: (0, j, 0, 0), pipeline_mode=once),
        pl.BlockSpec((2, 3, tc), lambda j, i: (0, 0, j)),
        pl.BlockSpec((2, 1, tc), lambda j, i: (0, 0, j)),
    ]
    out_shape = [jax.ShapeDtypeStruct((nj, rows, d), F32), jax.ShapeDtypeStruct((2, nj, d, tc), F32),
                 jax.ShapeDtypeStruct((2, 3, f), F32), jax.ShapeDtypeStruct((2, 1, f), F32)]
    dx, dw, dcw, dcb = pl.pallas_call(body, name=name, grid=(nj, ni), in_specs=in_specs, out_specs=out_specs,
                                      out_shape=out_shape, compiler_params=_cp("parallel", "arbitrary"))(
        hf, hf, hf, df, df, x, w_up, w_up, cw, cb)
    return dx, dw.reshape(nq, d, tc), dcw, dcb


def _mixer_fwd(h0, cw, cb, ga, ba, gb, bb, ws, sbb, *, name):
    _, rows, w = h0.shape
    t = _pick(rows, (256,))
    hb = t // CONV_HALO
    groups = w // B_CHUNK

    def body(h_ref, hp_ref, cw_ref, cb_ref, ga_ref, ba_ref, gb_ref, bb_ref, ws_ref, sb_ref, o_ref):
        first = pl.program_id(0) == 0
        a1 = h_ref[0] * _sig(h_ref[1])
        a1p = jnp.where(first, 0.0, hp_ref[0] * _sig(hp_ref[1]))
        e = jnp.concatenate([a1p, a1], axis=0)
        acc = cw_ref[A_KERNEL - 1:A_KERNEL, :] * e
        for k in range(A_KERNEL - 1):
            acc = acc + cw_ref[k:k + 1, :] * pltpu.roll(e, A_KERNEL - 1 - k, 0)
        xh, _ = _ln_stats(acc[CONV_HALO:] + cb_ref[...])
        a3 = xh * ga_ref[...] + ba_ref[...]
        o_ref[:, 0:w] = (a3 * _sig(a3)).astype(BF16)

        u, _ = _gelu(h_ref[2])
        v1, _ = _gelu(h_ref[3])
        xh2, _ = _ln_stats(v1)
        v2 = (xh2 * gb_ref[...] + bb_ref[...]).astype(BF16)
        for c in range(t // B_CHUNK):
            rs = slice(c * B_CHUNK, (c + 1) * B_CHUNK)
            for g in range(groups):
                cs = slice(g * B_CHUNK, (g + 1) * B_CHUNK)
                mixed = jnp.dot(ws_ref[g], v2[rs, cs], preferred_element_type=F32) + sb_ref[g]
                o_ref[rs, w + g * B_CHUNK:w + (g + 1) * B_CHUNK] = (u[rs, cs] * mixed).astype(BF16)

    vec = pl.BlockSpec((1, w), lambda i: (0, 0))
    grp = pl.BlockSpec((groups, B_CHUNK, B_CHUNK), lambda i: (0, 0, 0))
    in_specs = [
        pl.BlockSpec((4, t, w), lambda i: (0, i, 0)),
        pl.BlockSpec((2, CONV_HALO, w), lambda i: (0, jnp.maximum(i * hb - 1, 0), 0)),
        pl.BlockSpec((A_KERNEL, w), lambda i: (0, 0)),
        vec, vec, vec, vec, vec, grp, grp,
    ]
    return pl.pallas_call(body, name=name, grid=(rows // t,), in_specs=in_specs,
                          out_specs=pl.BlockSpec((t, 2 * w), lambda i: (i, 0)),
                          out_shape=jax.ShapeDtypeStruct((rows, 2 * w), BF16),
                          compiler_params=_cp("parallel"))(h0, h0, cw, cb, ga, ba, gb, bb, ws, sbb)


def _mixer_bwd(h0, dab, cw, cb, ga, ba, gb, bb, ws, wst, sbb, tril, *, name):
    _, rows, w = h0.shape
    t = _pick(rows, (256,))
    hb = t // CONV_HALO
    ni = rows // t
    last_blk = rows // CONV_HALO - 1
    ext = t + 2 * CONV_HALO
    tile = slice(CONV_HALO, CONV_HALO + t)
    groups = w // B_CHUNK
    taps = A_KERNEL - 1

    def body(h_ref, hp_ref, hn_ref, d_ref, dn_ref, cw_ref, cb_ref, ga_ref, ba_ref, gb_ref, bb_ref, ws_ref, wst_ref,
             sb_ref, tril_ref, dh_ref, dcw_ref, dcb_ref, dga_ref, dba_ref, dgb_ref, dbb_ref, dws_ref, dsb_ref):
        i = pl.program_id(0)
        first = i == 0
        last = i == ni - 1

        @pl.when(first)
        def _():
            for r in (dcw_ref, dcb_ref, dga_ref, dba_ref, dgb_ref, dbb_ref, dws_ref, dsb_ref):
                r[...] = jnp.zeros_like(r)

        av_e = jnp.concatenate([hp_ref[0], h_ref[0], hn_ref[0]], axis=0)
        sg_e = _sig(jnp.concatenate([hp_ref[1], h_ref[1], hn_ref[1]], axis=0))
        rows_e = lax.broadcasted_iota(jnp.int32, (ext, 1), 0)
        a1_e = jnp.where(first & (rows_e < CONV_HALO), 0.0, av_e * sg_e)
        acc = cw_ref[taps:taps + 1, :] * a1_e
        for k in range(taps):
            acc = acc + cw_ref[k:k + 1, :] * pltpu.roll(a1_e, taps - k, 0)
        xh, rstd = _ln_stats(acc + cb_ref[...])
        a3 = xh * ga_ref[...] + ba_ref[...]
        s3 = _sig(a3)
        da_e = jnp.concatenate([jnp.zeros((CONV_HALO, w), F32), d_ref[:, 0:w], jnp.where(last, 0.0, dn_ref[...])], axis=0)
        da3 = da_e * (s3 * (1.0 + a3 * (1.0 - s3)))
        da2 = _ln_bwd(da3 * ga_ref[...], xh, rstd)
        dga_ref[...] += _rowsum(da3[tile] * xh[tile])
        dba_ref[...] += _rowsum(da3[tile])
        da2t = da2[tile]
        dcb_ref[...] += _rowsum(da2t)
        dcw_ref[taps:taps + 1, :] += _rowsum(da2t * a1_e[tile])
        da1 = cw_ref[taps:taps + 1, :] * da2
        for k in range(taps):
            sh = taps - k
            dcw_ref[k:k + 1, :] += _rowsum(da2t * pltpu.roll(a1_e, sh, 0)[tile])
            da1 = da1 + cw_ref[k:k + 1, :] * pltpu.roll(da2, ext - sh, 0)
        da1t = da1[tile]
        sgt = sg_e[tile]
        dh_ref[0] = (da1t * sgt).astype(BF16)
        dh_ref[1] = (da1t * h_ref[0] * sgt * (1.0 - sgt)).astype(BF16)

        bu = h_ref[2]
        bv = h_ref[3]
        u, tu = _gelu(bu)
        v1, tv = _gelu(bv)
        xh2, rstd2 = _ln_stats(v1)
        v2 = (xh2 * gb_ref[...] + bb_ref[...]).astype(BF16)
        db = d_ref[:, w:2 * w]
        dmx_all = db * u
        du_parts, dv2_parts = [], []
        for c in range(t // B_CHUNK):
            rs = slice(c * B_CHUNK, (c + 1) * B_CHUNK)
            du_row, dv2_row = [], []
            for g in range(groups):
                cs = slice(g * B_CHUNK, (g + 1) * B_CHUNK)
                v2cg = v2[rs, cs]
                mixed = jnp.dot(ws_ref[g], v2cg, preferred_element_type=F32) + sb_ref[g]
                dmx = dmx_all[rs, cs]
                dmxb = dmx.astype(BF16)
                du_row.append(db[rs, cs] * mixed)
                dv2_row.append(jnp.dot(wst_ref[g], dmxb, preferred_element_type=F32))
                dws_ref[g] += tril_ref[...] * lax.dot_general(dmxb, v2cg, (((1,), (1,)), ((), ())),
                                                               preferred_element_type=F32)
                dsb_ref[g:g + 1, :] += _rowsum(dmx.T)
            du_parts.append(jnp.concatenate(du_row, axis=1))
            dv2_parts.append(jnp.concatenate(dv2_row, axis=1))
        du = jnp.concatenate(du_parts, axis=0)
        dv2 = jnp.concatenate(dv2_parts, axis=0)
        dgb_ref[...] += _rowsum(dv2 * xh2)
        dbb_ref[...] += _rowsum(dv2)
        dv1 = _ln_bwd(dv2 * gb_ref[...], xh2, rstd2)
        dh_ref[2] = (du * _gelu_grad(bu, tu)).astype(BF16)
        dh_ref[3] = (dv1 * _gelu_grad(bv, tv)).astype(BF16)

    vec = pl.BlockSpec((1, w), lambda i: (0, 0))
    grp = pl.BlockSpec((groups, B_CHUNK, B_CHUNK), lambda i: (0, 0, 0))
    in_specs = [
        pl.BlockSpec((4, t, w), lambda i: (0, i, 0)),
        pl.BlockSpec((2, CONV_HALO, w), lambda i: (0, jnp.maximum(i * hb - 1, 0), 0)),
        pl.BlockSpec((2, CONV_HALO, w), lambda i: (0, jnp.minimum((i + 1) * hb, last_blk), 0)),
        pl.BlockSpec((t, 2 * w), lambda i: (i, 0)),
        pl.BlockSpec((CONV_HALO, w), lambda i: (jnp.minimum((i + 1) * hb, last_blk), 0)),
        pl.BlockSpec((A_KERNEL, w), lambda i: (0, 0)),
        vec, vec, vec, vec, vec, grp, grp, grp,
        pl.BlockSpec((B_CHUNK, B_CHUNK), lambda i: (0, 0)),
    ]
    vsds = jax.ShapeDtypeStruct((1, w), F32)
    out_specs = [
        pl.BlockSpec((4, t, w), lambda i: (0, i, 0)),
        pl.BlockSpec((A_KERNEL, w), lambda i: (0, 0)),
        vec, vec, vec, vec, vec, grp,
        pl.BlockSpec((groups, B_CHUNK), lambda i: (0, 0)),
    ]
    out_shape = [jax.ShapeDtypeStruct((4, rows, w), BF16), jax.ShapeDtypeStruct((A_KERNEL, w), F32),
                 vsds, vsds, vsds, vsds, vsds, jax.ShapeDtypeStruct((groups, B_CHUNK, B_CHUNK), F32),
                 jax.ShapeDtypeStruct((groups, B_CHUNK), F32)]
    return pl.pallas_call(body, name=name, grid=(ni,), in_specs=in_specs, out_specs=out_specs, out_shape=out_shape,
                          compiler_params=_cp("arbitrary"))(h0, h0, h0, dab, dab, cw, cb, ga, ba, gb, bb, ws, wst, sbb, tril)


def _attn_mask(n):
    qi = lax.broadcasted_iota(jnp.int32, (ATT_BLOCK, 2 * ATT_BLOCK), 0)
    sj = lax.broadcasted_iota(jnp.int32, (ATT_BLOCK, 2 * ATT_BLOCK), 1)
    diff = qi + ATT_BLOCK - sj
    return (diff >= 0) & (diff < ATT_BLOCK) & ((n > 0) | (sj >= ATT_BLOCK))


def _attn_specs(rows, n_q):
    dq = n_q * HEAD_DIM
    dkv = 2 * (n_q // Q_PER_KV) * HEAD_DIM
    kv_blk = dq // dkv
    assert dq % dkv == 0
    return dq, dkv, [
        pl.BlockSpec(memory_space=pltpu.SMEM),
        pl.BlockSpec((ATT_BLOCK, dq), lambda n: (n, 0)),
        pl.BlockSpec((ATT_BLOCK, dkv), lambda n: (n, kv_blk)),
        pl.BlockSpec((ATT_BLOCK, dkv), lambda n: (jnp.maximum(n - 1, 0), kv_blk)),
    ]


def _kv_pair(kvc_ref, kvp_ref, kvh, n_kv):
    ks = slice(kvh * HEAD_DIM, (kvh + 1) * HEAD_DIM)
    vs = slice((n_kv + kvh) * HEAD_DIM, (n_kv + kvh + 1) * HEAD_DIM)
    kk = jnp.concatenate([kvp_ref[:, ks], kvc_ref[:, ks]], axis=0).astype(BF16)
    vv = jnp.concatenate([kvp_ref[:, vs], kvc_ref[:, vs]], axis=0).astype(BF16)
    return kk, vv


def _attn_fwd(qkv, sinks, *, name):
    rows = qkv.shape[0]
    n_q = sinks.shape[0]
    n_kv = n_q // Q_PER_KV
    scale = 1.0 / math.sqrt(HEAD_DIM)
    dq, _, in_specs = _attn_specs(rows, n_q)

    def body(sink_ref, q_ref, kvc_ref, kvp_ref, o_ref, lse_ref):
        valid = _attn_mask(pl.program_id(0))
        for kvh in range(n_kv):
            kk, vv = _kv_pair(kvc_ref, kvp_ref, kvh, n_kv)
            for g in range(Q_PER_KV):
                h = kvh * Q_PER_KV + g
                hs = slice(h * HEAD_DIM, (h + 1) * HEAD_DIM)
                s = lax.dot_general(q_ref[:, hs].astype(BF16), kk, (((1,), (1,)), ((), ())), preferred_element_type=F32)
                s = jnp.where(valid, s * scale, -jnp.inf)
                sk = sink_ref[h]
                m = jnp.maximum(jnp.max(s, axis=1, keepdims=True), sk)
                p = jnp.exp(s - m)
                l = jnp.sum(p, axis=1, keepdims=True) + jnp.exp(sk - m)
                o_ref[:, hs] = jnp.dot((p / l).astype(BF16), vv, preferred_element_type=F32)
                lse_ref[:, h:h + 1] = m + jnp.log(l)

    out_specs = [pl.BlockSpec((ATT_BLOCK, dq), lambda n: (n, 0)), pl.BlockSpec((ATT_BLOCK, n_q), lambda n: (n, 0))]
    out_shape = [jax.ShapeDtypeStruct((rows, dq), F32), jax.ShapeDtypeStruct((rows, n_q), F32)]
    return pl.pallas_call(body, name=name, grid=(rows // ATT_BLOCK,), in_specs=in_specs, out_specs=out_specs,
                          out_shape=out_shape, compiler_params=_cp("parallel"))(sinks, qkv, qkv, qkv)


def _attn_bwd(qkv, dout, lse, sinks, *, name):
    rows = qkv.shape[0]
    n_q = sinks.shape[0]
    n_kv = n_q // Q_PER_KV
    scale = 1.0 / math.sqrt(HEAD_DIM)
    dq_w, dkv_w, in_specs = _attn_specs(rows, n_q)
    blk_q = pl.BlockSpec((ATT_BLOCK, dq_w), lambda n: (n, 0))
    blk_kv = pl.BlockSpec((ATT_BLOCK, dkv_w), lambda n: (n, 0))
    in_specs = in_specs + [blk_q, pl.BlockSpec((ATT_BLOCK, n_q), lambda n: (n, 0))]

    def body(sink_ref, q_ref, kvc_ref, kvp_ref, do_ref, lse_ref, dq_ref, dkc_ref, dkp_ref, dsink_ref):
        n = pl.program_id(0)

        @pl.when(n == 0)
        def _():
            dsink_ref[...] = jnp.zeros_like(dsink_ref)

        valid = _attn_mask(n)
        head_ids = lax.broadcasted_iota(jnp.int32, (1, n_q), 1)
        dsink = jnp.zeros((1, n_q), F32)
        for kvh in range(n_kv):
            kk, vv = _kv_pair(kvc_ref, kvp_ref, kvh, n_kv)
            dk = jnp.zeros((2 * ATT_BLOCK, HEAD_DIM), F32)
            dv = jnp.zeros((2 * ATT_BLOCK, HEAD_DIM), F32)
            for g in range(Q_PER_KV):
                h = kvh * Q_PER_KV + g
                hs = slice(h * HEAD_DIM, (h + 1) * HEAD_DIM)
                qh = q_ref[:, hs].astype(BF16)
                s = lax.dot_general(qh, kk, (((1,), (1,)), ((), ())), preferred_element_type=F32)
                s = jnp.where(valid, s * scale, -jnp.inf)
                lse_h = lse_ref[:, h:h + 1]
                p = jnp.exp(s - lse_h)
                doh = do_ref[:, hs].astype(BF16)
                dp = lax.dot_general(doh, vv, (((1,), (1,)), ((), ())), preferred_element_type=F32)
                delta = jnp.sum(p * dp, axis=1, keepdims=True)
                ds = (p * (dp - delta) * scale).astype(BF16)
                dsink = dsink + jnp.where(head_ids == h, -jnp.sum(jnp.exp(sink_ref[h] - lse_h) * delta), 0.0)
                dq_ref[:, hs] = jnp.dot(ds, kk, preferred_element_type=F32)
                dk = dk + lax.dot_general(ds, qh, (((0,), (0,)), ((), ())), preferred_element_type=F32)
                dv = dv + lax.dot_general(p.astype(BF16), doh, (((0,), (0,)), ((), ())), preferred_element_type=F32)
            ks = slice(kvh * HEAD_DIM, (kvh + 1) * HEAD_DIM)
            vs = slice((n_kv + kvh) * HEAD_DIM, (n_kv + kvh + 1) * HEAD_DIM)
            dkp_ref[:, ks] = dk[0:ATT_BLOCK]
            dkc_ref[:, ks] = dk[ATT_BLOCK:]
            dkp_ref[:, vs] = dv[0:ATT_BLOCK]
            dkc_ref[:, vs] = dv[ATT_BLOCK:]
        dsink_ref[...] += dsink

    out_specs = [blk_q, blk_kv, blk_kv, pl.BlockSpec((1, n_q), lambda n: (0, 0))]
    out_shape = [jax.ShapeDtypeStruct((rows, dq_w), F32), jax.ShapeDtypeStruct((rows, dkv_w), F32),
                 jax.ShapeDtypeStruct((rows, dkv_w), F32), jax.ShapeDtypeStruct((1, n_q), F32)]
    return pl.pallas_call(body, name=name, grid=(rows // ATT_BLOCK,), in_specs=in_specs, out_specs=out_specs,
                          out_shape=out_shape, compiler_params=_cp("arbitrary"))(sinks, qkv, qkv, qkv, dout, lse)


def _dqkv_assemble(dq, dkc, dkp, *, name):
    rows, dq_w = dq.shape
    dkv_w = dkc.shape[1]
    nb = rows // ATT_BLOCK

    def body(dq_ref, dkc_ref, dkp_ref, o_ref, db_ref):
        n = pl.program_id(0)

        @pl.when(n == 0)
        def _():
            db_ref[...] = jnp.zeros_like(db_ref)

        dqv = dq_ref[...]
        dkv = dkc_ref[...] + jnp.where(n == nb - 1, 0.0, dkp_ref[...])
        o_ref[:, 0:dq_w] = dqv.astype(BF16)
        o_ref[:, dq_w:dq_w + dkv_w] = dkv.astype(BF16)
        db_ref[:, 0:dq_w] += _rowsum(dqv)
        db_ref[:, dq_w:dq_w + dkv_w] += _rowsum(dkv)

    width = dq_w + dkv_w
    in_specs = [pl.BlockSpec((ATT_BLOCK, dq_w), lambda n: (n, 0)), pl.BlockSpec((ATT_BLOCK, dkv_w), lambda n: (n, 0)),
                pl.BlockSpec((ATT_BLOCK, dkv_w), lambda n: (jnp.minimum(n + 1, nb - 1), 0))]
    out_specs = [pl.BlockSpec((ATT_BLOCK, width), lambda n: (n, 0)), pl.BlockSpec((1, width), lambda n: (0, 0))]
    out_shape = [jax.ShapeDtypeStruct((rows, width), BF16), jax.ShapeDtypeStruct((1, width), F32)]
    return pl.pallas_call(body, name=name, grid=(nb,), in_specs=in_specs, out_specs=out_specs, out_shape=out_shape,
                          compiler_params=_cp("arbitrary"))(dq, dkc, dkp)


def _row_tile(r, c):
    budget = 2 * 1024 * 1024 // (4 * c)
    for cand in (1024, 512, 256, 128, 64, 32, 16):
        if cand <= budget and r % cand == 0:
            return cand
    return r


def _to_bf16(a, *, name):
    p, r, c = a.shape
    t = _row_tile(r, c)

    def body(a_ref, o_ref):
        o_ref[...] = a_ref[...].astype(BF16)

    blk = pl.BlockSpec((None, t, c), lambda k, i: (k, i, 0))
    return pl.pallas_call(body, name=name, grid=(p, r // t), in_specs=[blk], out_specs=blk,
                          out_shape=jax.ShapeDtypeStruct(a.shape, BF16), compiler_params=_cp("parallel", "parallel"))(a)


def _octo_sum(own, recv, place, dest, lead, *, name):
    _, _, r, c = own.shape
    t = _row_tile(r, c)
    lead_idx, buf_shape = lead

    def body(place_ref, own_ref, *rest):
        o_ref = rest[7] if dest is None else rest[8]
        acc = own_ref[...].astype(F32)
        for k in range(7):
            acc = acc + rest[k][...].astype(F32)
        o_ref[...] = acc

    def peer(mask):
        return pl.BlockSpec((None, t, c), lambda i, pr: (pr[2] ^ mask, i, 0))

    if lead_idx is None:
        o_spec = pl.BlockSpec((None, t, c), lambda i, pr: (pr[1], i, 0))
    else:
        o_spec = pl.BlockSpec((None, None, t, c), lambda i, pr: (lead_idx, pr[1], i, 0))
    in_specs = [pl.BlockSpec((None, None, t, c), lambda i, pr: (pr[0], pr[1], i, 0))] + [peer(m) for m in range(1, 8)]
    args = [place, own] + [recv] * 7
    aliases = {}
    if dest is not None:
        in_specs.append(HBM)
        args.append(dest)
        aliases = {9: 0}
    grid_spec = pltpu.PrefetchScalarGridSpec(num_scalar_prefetch=1, grid=(r // t,), in_specs=in_specs, out_specs=o_spec)
    return pl.pallas_call(body, name=name, grid_spec=grid_spec, out_shape=jax.ShapeDtypeStruct(buf_shape, F32),
                          input_output_aliases=aliases, compiler_params=_cp("parallel"))(*args)


def _adamw_math(w, g, m, v):
    nm = ADAM_B1 * m + (1.0 - ADAM_B1) * g
    nv = ADAM_B2 * v + (1.0 - ADAM_B2) * (g * g)
    m_hat = nm / (1.0 - ADAM_B1 ** ADAM_STEP)
    v_hat = nv / (1.0 - ADAM_B2 ** ADAM_STEP)
    return -ADAM_LR * (m_hat / (jnp.sqrt(v_hat) + ADAM_EPS) + ADAM_WD * w), nm, nv


def _adamw(w, g, m, v, *, name):
    r, c = w.shape
    t = _row_tile(r, c)

    def body(w_ref, g_ref, m_ref, v_ref, d_ref, nm_ref, nv_ref):
        d_ref[...], nm_ref[...], nv_ref[...] = _adamw_math(w_ref[...], g_ref[...], m_ref[...], v_ref[...])

    blk = pl.BlockSpec((t, c), lambda i: (i, 0))
    sds = jax.ShapeDtypeStruct((r, c), F32)
    return pl.pallas_call(body, name=name, grid=(r // t,), in_specs=[blk] * 4, out_specs=[blk] * 3,
                          out_shape=[sds] * 3, compiler_params=_cp("parallel"))(w, g, m, v)


HBM = pl.BlockSpec(memory_space=pl.ANY)


def _place():
    x, y, c = lax.axis_index("x"), lax.axis_index("y"), lax.axis_index("c")
    chips = [(1 - x, y), (x, 1 - y), (1 - x, 1 - y)]
    return x, y, c, 2 * x + y, (x, y, 1 - c), chips


def _rcopy(src, dst, ssem, rsem, dev):
    return pltpu.make_async_remote_copy(src_ref=src, dst_ref=dst, send_sem=ssem, recv_sem=rsem, device_id=dev,
                                        device_id_type=MESH)


HBM_ONLY = pl.BlockSpec(memory_space=pltpu.HBM)
SEM = pl.BlockSpec(memory_space=pltpu.SEMAPHORE)


def _peers():
    x, y, c = lax.axis_index("x"), lax.axis_index("y"), lax.axis_index("c")
    out = []
    for mask in range(1, 8):
        px = 1 - x if mask & 4 else x
        py = 1 - y if mask & 2 else y
        pc = 1 - c if mask & 1 else c
        out.append(((px, py, pc), 2 * px + py, pc, 4 * px + 2 * py + pc))
    return 4 * x + 2 * y + c, out


def _reduce_start(grads, lands, *, name):
    nt = len(grads)

    def body(*refs):
        ssems, rsems = refs[2 * nt:3 * nt], refs[3 * nt:4 * nt]
        g_out, l_out, token = refs[4 * nt:5 * nt], refs[5 * nt:6 * nt], refs[6 * nt]
        me, peers = _peers()
        for t in range(nt):
            for k, (dev, chip, core, _) in enumerate(peers):
                _rcopy(g_out[t].at[chip, core], l_out[t].at[me], ssems[t].at[k], rsems[t].at[k], dev).start()
        token[...] = jnp.zeros_like(token)

    sems = [pltpu.SemaphoreType.DMA((7,))] * (2 * nt)
    out_shape = (sems + [pltpu.HBM(g.shape, g.dtype) for g in grads] + [pltpu.HBM(l.shape, l.dtype) for l in lands]
                 + [jax.ShapeDtypeStruct((8, LANES), F32)])
    res = pl.pallas_call(
        body, name=name, in_specs=[HBM_ONLY] * (2 * nt),
        out_specs=[SEM] * (2 * nt) + [HBM_ONLY] * (2 * nt) + [pl.BlockSpec(memory_space=pltpu.VMEM)], out_shape=out_shape,
        input_output_aliases={t: 2 * nt + t for t in range(2 * nt)},
        compiler_params=pltpu.CompilerParams(has_side_effects=DATAFLOW),
    )(*[pltpu.with_memory_space_constraint(a, pltpu.HBM) for a in list(grads) + list(lands)])
    return res[:nt], res[nt:2 * nt], res[2 * nt:3 * nt], res[3 * nt:4 * nt], res[4 * nt]


def _reduce_wait(grads, lands, ssems, rsems, after, *, name):
    nt = len(grads)

    def body(*refs):
        ssem_refs, rsem_refs = refs[2 * nt:3 * nt], refs[3 * nt:4 * nt]
        g_out, l_out = refs[4 * nt + 1:5 * nt + 1], refs[5 * nt + 1:6 * nt + 1]
        me, peers = _peers()
        for t in range(nt):
            for k, (dev, chip, core, _) in enumerate(peers):
                _rcopy(g_out[t].at[chip, core], l_out[t].at[me], ssem_refs[t].at[k], rsem_refs[t].at[k], dev).wait_send()
        for t in range(nt):
            for k, (dev, _, _, idx) in enumerate(peers):
                slot = l_out[t].at[idx]
                _rcopy(slot, slot, ssem_refs[t].at[k], rsem_refs[t].at[k], dev).wait_recv()

    res = pl.pallas_call(
        body, name=name, in_specs=[HBM_ONLY] * (2 * nt) + [SEM] * (2 * nt) + [HBM], out_specs=[HBM_ONLY] * (2 * nt),
        out_shape=[pltpu.HBM(a.shape, a.dtype) for a in list(grads) + list(lands)],
        input_output_aliases={t: t for t in range(2 * nt)},
        compiler_params=pltpu.CompilerParams(has_side_effects=DATAFLOW),
    )(*grads, *lands, *ssems, *rsems, after)
    return list(res[:nt]), list(res[nt:])
DATAFLOW = pltpu.SideEffectType.DATAFLOW_SIDE_EFFECTING


def _gather_start(bufs, *, name):
    nt = len(bufs)

    def body(*refs):
        ssems, rsems, outs = refs[nt:2 * nt], refs[2 * nt:3 * nt], refs[3 * nt:4 * nt]
        x, y, c, q, sib, chips = _place()
        for t in range(nt):
            for j, (px, py) in enumerate(chips):
                mine = outs[t].at[q]
                _rcopy(mine, mine, ssems[t].at[j], rsems[t].at[j], (px, py, c)).start()

    sems = [pltpu.SemaphoreType.DMA((3,))] * (2 * nt)
    out_shape = sems + [pltpu.HBM(b.shape, b.dtype) for b in bufs]
    res = pl.pallas_call(
        body, name=name, in_specs=[HBM_ONLY] * nt, out_specs=[SEM] * (2 * nt) + [HBM_ONLY] * nt, out_shape=out_shape,
        input_output_aliases={t: 2 * nt + t for t in range(nt)},
        compiler_params=pltpu.CompilerParams(has_side_effects=DATAFLOW),
    )(*[pltpu.with_memory_space_constraint(b, pltpu.HBM) for b in bufs])
    return res[:nt], res[nt:2 * nt], res[2 * nt:]


def _gather_wait(bufs, ssems, rsems, after, *, name):
    nt = len(bufs)

    def body(*refs):
        ssem_refs, rsem_refs = refs[nt:2 * nt], refs[2 * nt:3 * nt]
        outs = refs[3 * nt + 1:]
        x, y, c, q, sib, chips = _place()
        for t in range(nt):
            for j, (px, py) in enumerate(chips):
                mine = outs[t].at[q]
                _rcopy(mine, mine, ssem_refs[t].at[j], rsem_refs[t].at[j], (px, py, c)).wait_send()
        for t in range(nt):
            for j, (px, py) in enumerate(chips):
                theirs = outs[t].at[2 * px + py]
                _rcopy(theirs, theirs, ssem_refs[t].at[j], rsem_refs[t].at[j], (px, py, c)).wait_recv()

    res = pl.pallas_call(
        body, name=name, in_specs=[HBM_ONLY] * nt + [SEM] * (2 * nt) + [HBM], out_specs=[HBM_ONLY] * nt,
        out_shape=[pltpu.HBM(b.shape, b.dtype) for b in bufs], input_output_aliases={t: t for t in range(nt)},
        compiler_params=pltpu.CompilerParams(has_side_effects=DATAFLOW),
    )(*bufs, *ssems, *rsems, after)
    return list(res)


def _sibling_share(bufs, layout, *, name):
    no = len(bufs)
    nt = len(layout)

    def body(*refs):
        outs = refs[no:2 * no]
        ssem, rsem = refs[2 * no:]
        x, y, c, q, sib, chips = _place()

        def slot(t, half):
            o, lead = layout[t]
            return outs[o].at[half] if lead is None else outs[o].at[lead, half]

        sends = []
        for t in range(nt):
            cp = _rcopy(slot(t, c), slot(t, c), ssem.at[t], rsem.at[t], sib)
            cp.start()
            sends.append(cp)
        for t in range(nt):
            _rcopy(slot(t, 1 - c), slot(t, 1 - c), ssem.at[t], rsem.at[t], sib).wait_recv()
        for cp in sends:
            cp.wait_send()

    out_shape = [jax.ShapeDtypeStruct(b.shape, b.dtype) for b in bufs]
    return pl.pallas_call(
        body, name=name, in_specs=[HBM] * no, out_specs=[HBM] * no, out_shape=out_shape,
        input_output_aliases={o: o for o in range(no)},
        scratch_shapes=[pltpu.SemaphoreType.DMA((nt,)), pltpu.SemaphoreType.DMA((nt,))],
    )(*bufs)


def _small_tail(local, params, *, name):
    (cwa, cba, ga, ba, gb, bb, dws, dsb, dbq, dsk, cwf0, cbf0, cwf1, cbf1,
     g00, g01, g10, g11, b00, b01, b10, b11, err) = local
    n_local = len(local)
    kw, wa = cwa.shape
    ng = dws.shape[0]
    nqkv = dbq.shape[1]
    nsk = dsk.shape[1]
    f = cwf0.shape[2]
    dm = err.shape[1]
    row_vec = 8 * (-(-kw // 8))
    shapes = [(row_vec + 8, wa), (ng * B_CHUNK + 8, B_CHUNK), (8, nqkv), (2, 2, 8, f), (16, dm)]
    n_grp = len(shapes)
    flat_params = [a for triple in params for a in triple]
    n_par = len(params)

    def body(*refs):
        loc = refs[:n_local]
        par = refs[n_local:n_local + 3 * n_par]
        outs = refs[n_local + 3 * n_par:n_local + 7 * n_par]
        loss_ref = refs[n_local + 7 * n_par]
        scr = refs[n_local + 7 * n_par + 1:]
        grp, gath, tot = scr[:n_grp], scr[n_grp:2 * n_grp], scr[2 * n_grp:3 * n_grp]
        ssem, rsem = scr[3 * n_grp], scr[3 * n_grp + 1]
        x, y = lax.axis_index("x"), lax.axis_index("y")
        q = 2 * x + y
        me, peers = _peers()

        for gr in grp:
            gr[...] = jnp.zeros_like(gr)
        a, b, c, dd, e = grp
        a[0:kw, :] = loc[0][...]
        for k in range(5):
            a[row_vec + k:row_vec + k + 1, :] = loc[1 + k][...]
        for g in range(ng):
            b[g * B_CHUNK:(g + 1) * B_CHUNK, :] = loc[6][g]
        b[ng * B_CHUNK:ng * B_CHUNK + ng, :] = loc[7][...]
        c[0:1, :] = loc[8][...]
        c[1:2, 0:nsk] = loc[9][...]
        for l in range(2):
            for s in range(2):
                dd[l, s, 0:3, :] = loc[10 + 2 * l][s]
                dd[l, s, 3:4, :] = loc[11 + 2 * l][s]
        for k in range(9):
            e[k:k + 1, :] = loc[14 + k][...]

        sends = []
        for gi in range(n_grp):
            gath[gi][me] = grp[gi][...]
            for k, (dev, _, _, _) in enumerate(peers):
                cp = _rcopy(grp[gi], gath[gi].at[me], ssem.at[gi, k], rsem.at[gi, k], dev)
                cp.start()
                sends.append(cp)
        for gi in range(n_grp):
            for k, (dev, _, _, idx) in enumerate(peers):
                slot = gath[gi].at[idx]
                _rcopy(slot, slot, ssem.at[gi, k], rsem.at[gi, k], dev).wait_recv()
            acc = gath[gi][0]
            for dv in range(1, 8):
                acc = acc + gath[gi][dv]
            tot[gi][...] = acc
        ta, tb, tc, td, te = tot

        def mine(piece):
            out = piece(0)
            for k in range(1, 4):
                out = jnp.where(q == k, piece(k), out)
            return out

        def update(p, grad, index=None):
            at = (lambda r: r[...]) if index is None else (lambda r: r[index])
            w_ref, m_ref, v_ref = par[3 * p:3 * p + 3]
            g_ref, d_ref, nm_ref, nv_ref = outs[4 * p:4 * p + 4]
            delta, nm, nv = _adamw_math(at(w_ref), grad, at(m_ref), at(v_ref))
            for r, val in ((g_ref, grad), (d_ref, delta), (nm_ref, nm), (nv_ref, nv)):
                if index is None:
                    r[...] = val
                else:
                    r[index] = val

        wq = wa // 4
        update(0, mine(lambda k: ta[0:kw, k * wq:(k + 1) * wq]), (0,))
        for k in range(5):
            update(1 + k, ta[row_vec + k:row_vec + k + 1, :])
        for g in range(ng):
            update(6, tb[g * B_CHUNK:(g + 1) * B_CHUNK, :], (0, g))
        update(7, tb[ng * B_CHUNK:ng * B_CHUNK + ng, :], (0,))
        nq4 = nqkv // 4
        update(8, mine(lambda k: tc[0:1, k * nq4:(k + 1) * nq4]))
        update(9, tc[1:2, 0:nsk])
        fh = f // 2
        for l in range(2):
            update(10, mine(lambda k: td[l, k // 2, 0:3, (k % 2) * fh:(k % 2 + 1) * fh]), (l,))
            update(11, jnp.concatenate([td[l, 0, 3:4, :], td[l, 1, 3:4, :]], axis=1), (slice(l, l + 1),))
        dq4 = dm // 4
        for i in range(2):
            for j in range(2):
                for p, base in ((12, 0), (13, 4)):
                    row = base + 2 * i + j
                    update(p, mine(lambda k: te[row:row + 1, k * dq4:(k + 1) * dq4]), (i, slice(j, j + 1)))
        loss_ref[...] = (0.5 / dm) * jnp.sum(te[8:9, :], axis=1, keepdims=True)
        for cp in sends:
            cp.wait_send()

    vm = pl.BlockSpec(memory_space=pltpu.VMEM)
    out_shape = []
    for w, _, _ in params:
        out_shape += [jax.ShapeDtypeStruct(w.shape, F32)] * 4
    out_shape.append(jax.ShapeDtypeStruct((1, 1), F32))
    scratch = ([pltpu.VMEM(s, F32) for s in shapes] + [pltpu.VMEM((8,) + s, F32) for s in shapes]
               + [pltpu.VMEM(s, F32) for s in shapes]
               + [pltpu.SemaphoreType.DMA((n_grp, 7)), pltpu.SemaphoreType.DMA((n_grp, 7))])
    res = pl.pallas_call(
        body, name=name, in_specs=[vm] * (n_local + 3 * n_par), out_specs=[vm] * len(out_shape), out_shape=out_shape,
        scratch_shapes=scratch, compiler_params=pltpu.CompilerParams(vmem_limit_bytes=VMEM_LIMIT),
    )(*local, *flat_params)
    return [res[4 * p:4 * p + 4] for p in range(n_par)], res[-1]


def _pack(arrays, rows_multiple):
    flat = jnp.concatenate([a.reshape(-1) for a in arrays])
    rows = -(-flat.shape[0] // LANES)
    rows = -(-rows // rows_multiple) * rows_multiple
    flat = jnp.pad(flat, (0, rows * LANES - flat.shape[0]))
    return flat.reshape(rows, LANES)


def _unpack(buf, shapes):
    flat = buf.reshape(-1)
    out, pos = [], 0
    for s in shapes:
        n = math.prod(s)
        out.append(flat[pos:pos + n].reshape(s))
        pos += n
    return out


def _unshard_cols(stacked):
    moved = jnp.moveaxis(stacked, 0, -2)
    return moved.reshape(moved.shape[:-2] + (4 * stacked.shape[-1],))


def _shard_cols(full, q):
    n = full.shape[-1] // 4
    return lax.dynamic_slice_in_dim(full, q * n, n, axis=full.ndim - 1)


def kernel(x, ab_w_in, a_conv_w, a_conv_b, a_norm_g, a_norm_b, b_norm_g, b_norm_b, b_spatial_w, b_spatial_b, ab_w_out, c_w_qkv, c_b_qkv, c_sinks, c_w_o, ffn_w_up, ffn_conv_w, ffn_conv_b, ffn_w_down, ln_g, ln_b, loss_target, m_ab_w_in, m_a_conv_w, m_a_conv_b, m_a_norm_g, m_a_norm_b, m_b_norm_g, m_b_norm_b, m_b_spatial_w, m_b_spatial_b, m_ab_w_out, m_c_w_qkv, m_c_b_qkv, m_c_sinks, m_c_w_o, m_ffn_w_up, m_ffn_conv_w, m_ffn_conv_b, m_ffn_w_down, m_ln_g, m_ln_b, v_ab_w_in, v_a_conv_w, v_a_conv_b, v_a_norm_g, v_a_norm_b, v_b_norm_g, v_b_norm_b, v_b_spatial_w, v_b_spatial_b, v_ab_w_out, v_c_w_qkv, v_c_b_qkv, v_c_sinks, v_c_w_o, v_ffn_w_up, v_ffn_conv_w, v_ffn_conv_b, v_ffn_w_down, v_ln_g, v_ln_b):
    rows, d = x.shape[1], x.shape[2]
    depth = ln_g.shape[0]
    assert depth == 2 and x.shape[0] == 1
    alpha = (2.0 * depth) ** 0.25
    f = ffn_w_down.shape[1] * 4
    n_q = c_sinks.shape[1]
    q_idx = 2 * lax.axis_index("x") + lax.axis_index("y")
    c_idx = lax.axis_index("c")
    xs, tgt = x[0], loss_target[0]

    def own_slot(part):
        buf = lax.empty((4,) + part.shape, part.dtype)
        return lax.dynamic_update_slice(buf, part[None], (q_idx, 0, 0, 0))

    def halves(wm):
        return own_slot(wm.astype(BF16).reshape((2, wm.shape[0] // 2) + wm.shape[1:]))

    small_sharded = [a_conv_w[0], c_b_qkv[0], ffn_conv_w, ln_g, ln_b]
    small_pack = _pack(small_sharded, 16)
    bufs = [halves(ab_w_in[0]), own_slot(small_pack.reshape(2, small_pack.shape[0] // 2, LANES)), halves(ab_w_out[0]),
            halves(ffn_w_up[0]), halves(ffn_w_down[0]), halves(c_w_qkv[0]), halves(c_w_o[0]),
            halves(ffn_w_up[1]), halves(ffn_w_down[1])]
    ssems, rsems, started = _gather_start(bufs, name="gather_start")

    def arrive(idx, after, tag):
        got = _gather_wait([started[i] for i in idx], [ssems[i] for i in idx], [rsems[i] for i in idx], after,
                           name=f"gather_wait_{tag}")
        return [g.reshape(4, 2 * g.shape[2], g.shape[3]) for g in got]

    w_in, small_all = arrive([0, 1], xs, "in")
    small_all = small_all.reshape(4, -1)
    sh_shapes = [s.shape for s in small_sharded]
    pieces, pos = [], 0
    for s in sh_shapes:
        n = math.prod(s)
        pieces.append(_unshard_cols(small_all[:, pos:pos + n].reshape((4,) + s)))
        pos += n
    conv_w_a, b_qkv, conv_w_f, ln_gf, ln_bf = pieces

    tril = jnp.tril(jnp.ones((B_CHUNK, B_CHUNK), F32))
    ws = (b_spatial_w[0] * tril).astype(BF16)
    wst = jnp.swapaxes(ws, 1, 2)
    sbb = jnp.broadcast_to(b_spatial_b[0][:, :, None], b_spatial_w[0].shape)
    mix_vecs = [a_conv_b, a_norm_g, a_norm_b, b_norm_g, b_norm_b]
    cw_f = [jnp.swapaxes(conv_w_f[l].reshape(3, 2, f), 0, 1) for l in range(depth)]
    cb_f = [ffn_conv_b[l].reshape(2, 1, f) for l in range(depth)]
    lng = lambda i, j: ln_gf[i, j].reshape(1, d)
    lnb = lambda i, j: ln_bf[i, j].reshape(1, d)
    sinks = c_sinks[0]

    w_up, w_down = [None, None], [None, None]

    def ffn_fwd(xin, l):
        w_up[l], = arrive([3 + 4 * l], xin, f"up{l}")
        hf, fact = _ffn_up_fwd(xin, w_up[l], cw_f[l], cb_f[l], name=f"ffn{l}_up")
        w_down[l] = arrive([4 + 4 * l], fact, f"down{l}")[0].reshape(-1, d)
        out = _matmul(fact, w_down[l], name=f"ffn{l}_down", tm=1024, tn=1024, tk=1408)
        return hf, fact, out

    h0 = _matmul(xs, w_in, name="mix_in", tm=1024, tn=512, tk=1024, out_stack=4)
    ab = _mixer_fwd(h0, conv_w_a, *mix_vecs, ws, sbb, name="mix_mid")
    w_out = arrive([2], ab, "out")[0].reshape(-1, d)
    mix = _matmul(ab, w_out, name="mix_out", tm=1024, tn=1024, tk=1024)
    x1 = _add_ln_fwd(xs, mix, lng(0, 0), lnb(0, 0), alpha, name="ln00")
    hf0, f0, ffn0 = ffn_fwd(x1, 0)
    x2 = _add_ln_fwd(x1, ffn0, lng(0, 1), lnb(0, 1), alpha, name="ln01")
    w_qkv = _unshard_cols(arrive([5], x2, "qkv")[0])
    qkv = _matmul(x2, w_qkv, name="att_qkv", tm=1024, tn=w_qkv.shape[1], tk=1024, bias=b_qkv.reshape(1, -1))
    ao, lse = _attn_fwd(qkv, sinks, name="att_core")
    w_o = arrive([6], ao, "o")[0].reshape(-1, d)
    att = _matmul(ao, w_o, name="att_out", tm=1024, tn=1024, tk=1024)
    x3 = _add_ln_fwd(x2, att, lng(1, 0), lnb(1, 0), alpha, name="ln10")
    hf1, f1, ffn1 = ffn_fwd(x3, 1)
    x4 = _add_ln_fwd(x3, ffn1, lng(1, 1), lnb(1, 1), alpha, name="ln11")
    sq_err, dy = _loss_and_grad(x4, tgt, name="loss")

    def owner_view(g):
        if g.ndim == 3:
            return g.reshape(4, 2, g.shape[1] // 2, g.shape[2])
        return g.reshape(4, 2, g.shape[0] // 8, g.shape[1])

    in_flight = []

    def send_grads(tag, grads):
        lands = [lax.empty((8,) + g.shape[2:], BF16) for g in grads]
        ss, rs, g_thru, l_thru, token = _reduce_start(grads, lands, name=f"reduce_start_{tag}")
        in_flight.append((tag, g_thru, l_thru, ss, rs))
        return token[0:1, 0:1]

    def ffn_bwd(dz, xin, hf, fact, l):
        d_wdown = _matmul(fact, dz, name=f"ffn{l}_down_dw", ta=True, tm=1408, tn=1024, tk=512, out_dtype=BF16)
        dfa = _matmul(dz, w_down[l], name=f"ffn{l}_down_dx", tb=True, tm=1024, tn=1408, tk=1024)
        dx_parts, d_wup, dcw, dcb = _ffn_up_bwd(hf, dfa, xin, w_up[l], cw_f[l], cb_f[l], name=f"ffn{l}_up_bwd")
        d_wup = _to_bf16(d_wup, name=f"ffn{l}_up_dw_bf16")
        tok = send_grads(f"ffn{l}", [owner_view(d_wup), owner_view(d_wdown)])
        return [(dx_parts, 1.0), (dz, alpha)], dcw, dcb, tok

    dz, dg11, db11 = _add_ln_bwd([(dy, 1.0)], x3, ffn1, lng(1, 1), alpha, name="ln11_bwd")
    dx3, dcw1, dcb1, tok = ffn_bwd(dz, x3, hf1, f1, 1)
    dz, dg10, db10 = _add_ln_bwd(dx3, x2, att, lng(1, 0) + tok, alpha, name="ln10_bwd")
    d_wo = _matmul(ao, dz, name="att_out_dw", ta=True, tm=1024, tn=1024, tk=1024, out_dtype=BF16)
    dao = _matmul(dz, w_o, name="att_out_dx", tb=True, tm=1024, tn=1024, tk=1024)
    dq, dkc, dkp, d_sinks = _attn_bwd(qkv, dao, lse, sinks, name="att_core_bwd")
    dqkv, d_bqkv = _dqkv_assemble(dq, dkc, dkp, name="att_dqkv")
    d_wqkv = _matmul(x2, dqkv, name="att_qkv_dw", ta=True, tm=1024, tn=dqkv.shape[1], tk=1024, out_dtype=BF16)
    d_wqkv_st = jnp.moveaxis(d_wqkv.reshape(d_wqkv.shape[0], 4, -1), 1, 0)
    tok = send_grads("att", [owner_view(d_wqkv_st), owner_view(d_wo)])
    dx2 = _matmul(dqkv, w_qkv, name="att_qkv_dx", tb=True, tm=1024, tn=1024, tk=dqkv.shape[1], addend=(dz, alpha))
    dz, dg01, db01 = _add_ln_bwd([(dx2, 1.0)], x1, ffn0, lng(0, 1) + tok, alpha, name="ln01_bwd")
    dx1, dcw0, dcb0, tok = ffn_bwd(dz, x1, hf0, f0, 0)
    dz, dg00, db00 = _add_ln_bwd(dx1, xs, mix, lng(0, 0) + tok, alpha, name="ln00_bwd")
    d_wout = _matmul(ab, dz, name="mix_out_dw", ta=True, tm=1024, tn=1024, tk=1024, out_dtype=BF16)
    dab = _matmul(dz, w_out, name="mix_out_dx", tb=True, tm=1024, tn=1024, tk=1024)
    dh0, d_cwa, d_cba, d_ga, d_ba, d_gb, d_bb, d_ws, d_sb = _mixer_bwd(
        h0, dab, conv_w_a, *mix_vecs, ws, wst, sbb, tril, name="mix_mid_bwd")
    d_win = _matmul(xs, dh0, name="mix_in_dw", ta=True, tm=1024, tn=512, tk=1024, out_stack=4, out_dtype=BF16)
    tok = send_grads("mix", [owner_view(d_win), owner_view(d_wout)])
    grad_x = _matmul(dh0, w_in, name="mix_in_dx", tb=True, tm=1024, tn=1024, tk=512, addend=(dz, alpha),
                     bias=jnp.broadcast_to(tok, (1, d)))

    place = jnp.stack([q_idx, c_idx, 4 * lax.axis_index("x") + 2 * lax.axis_index("y") + c_idx]).astype(jnp.int32)
    where = {"mix": [(0, None), (1, None)], "att": [(2, None), (3, None)], "ffn0": [(4, 0), (5, 0)], "ffn1": [(4, 1), (5, 1)]}
    shard_bufs = [None] * 6
    layout = []
    for tag, g_thru, l_thru, ss, rs in in_flight:
        own, landed = _reduce_wait(g_thru, l_thru, ss, rs, grad_x, name=f"reduce_wait_{tag}")
        for k, (o, lead) in enumerate(where[tag]):
            piece = own[k].shape[2:]
            shape = (2,) + piece if lead is None else (2, 2) + piece
            shard_bufs[o] = _octo_sum(own[k], landed[k], place, shard_bufs[o], (lead, shape), name=f"reduce_sum_{tag}{k}")
            layout.append((o, lead))
    shared = _sibling_share(shard_bufs, layout, name="reduce_share")
    g_win = shared[0].reshape(ab_w_in.shape)
    g_wout = shared[1].reshape(ab_w_out.shape)
    g_wqkv = shared[2].reshape(c_w_qkv.shape)
    g_wo = shared[3].reshape(c_w_o.shape)
    g_wup = shared[4].reshape(ffn_w_up.shape)
    g_wdown = shared[5].reshape(ffn_w_down.shape)

    small_w = [a_conv_w, a_conv_b, a_norm_g, a_norm_b, b_norm_g, b_norm_b, b_spatial_w, b_spatial_b, c_b_qkv,
               c_sinks, ffn_conv_w, ffn_conv_b, ln_g, ln_b]
    small_m = [m_a_conv_w, m_a_conv_b, m_a_norm_g, m_a_norm_b, m_b_norm_g, m_b_norm_b, m_b_spatial_w, m_b_spatial_b,
               m_c_b_qkv, m_c_sinks, m_ffn_conv_w, m_ffn_conv_b, m_ln_g, m_ln_b]
    small_v = [v_a_conv_w, v_a_conv_b, v_a_norm_g, v_a_norm_b, v_b_norm_g, v_b_norm_b, v_b_spatial_w, v_b_spatial_b,
               v_c_b_qkv, v_c_sinks, v_ffn_conv_w, v_ffn_conv_b, v_ln_g, v_ln_b]
    local = [d_cwa, d_cba, d_ga, d_ba, d_gb, d_bb, d_ws, d_sb, d_bqkv, d_sinks, dcw0, dcb0, dcw1, dcb1,
             dg00, dg01, dg10, dg11, db00, db01, db10, db11, sq_err]
    small_out, loss = _small_tail(local, list(zip(small_w, small_m, small_v)), name="small_tail")
    loss = loss[0, 0]
    small_g = [o[0] for o in small_out]
    sm_delta = [o[1] for o in small_out]
    sm_m = [o[2] for o in small_out]
    sm_v = [o[3] for o in small_out]

    def adamw_big(w, g, m, v, name):
        two_d = lambda a: a.reshape(-1, a.shape[-1])
        outs = _adamw(two_d(w), two_d(g), two_d(m), two_d(v), name=name)
        return [o.reshape(w.shape) for o in outs]

    big_w = [ab_w_in, ab_w_out, c_w_qkv, c_w_o, ffn_w_up, ffn_w_down]
    big_g = [g_win, g_wout, g_wqkv, g_wo, g_wup, g_wdown]
    big_m = [m_ab_w_in, m_ab_w_out, m_c_w_qkv, m_c_w_o, m_ffn_w_up, m_ffn_w_down]
    big_v = [v_ab_w_in, v_ab_w_out, v_c_w_qkv, v_c_w_o, v_ffn_w_up, v_ffn_w_down]
    big_out = [adamw_big(w, g, m, v, f"adamw_big{t}") for t, (w, g, m, v) in enumerate(zip(big_w, big_g, big_m, big_v))]

    order_big = {0: 0, 9: 1, 10: 2, 13: 3, 14: 4, 17: 5}
    order_small = {1: 0, 2: 1, 3: 2, 4: 3, 5: 4, 6: 5, 7: 6, 8: 7, 11: 8, 12: 9, 15: 10, 16: 11, 18: 12, 19: 13}
    grads, deltas, new_m, new_v = [], [], [], []
    for pos_w in range(20):
        if pos_w in order_big:
            t = order_big[pos_w]
            grads.append(big_g[t])
            deltas.append(big_out[t][0])
            new_m.append(big_out[t][1])
            new_v.append(big_out[t][2])
        else:
            t = order_small[pos_w]
            grads.append(small_g[t])
            deltas.append(sm_delta[t])
            new_m.append(sm_m[t])
            new_v.append(sm_v[t])
    return (loss, grad_x[None], *grads, *deltas, *new_m, *new_v)
```

```python
import functools
import math

import jax
import jax.numpy as jnp
from jax import lax
from jax.experimental import pallas as pl
from jax.experimental.pallas import tpu as pltpu

F32 = jnp.float32
BF16 = jnp.bfloat16
MESH = pl.DeviceIdType.MESH

LN_EPS = 1e-5
HEAD_DIM = 64
ATT_BLOCK = 128
Q_PER_KV = 8
A_KERNEL = 31
CONV_HALO = 32
FFN_HALO = 8
B_CHUNK = 128
LANES = 128
MXU_WIDTH = 256
GELU_C = math.sqrt(2.0 / math.pi)
ADAM_LR = 0.001
ADAM_B1 = 0.9
ADAM_B2 = 0.999
ADAM_EPS = 1e-08
ADAM_WD = 0.01
ADAM_STEP = 10
VMEM_LIMIT = 56 * 1024 * 1024


def _cp(*dims):
    return pltpu.CompilerParams(dimension_semantics=dims, vmem_limit_bytes=VMEM_LIMIT)


def _pick(n, prefs):
    for p in prefs:
        if n % p == 0:
            return p
    return n


def _sig(x):
    return 1.0 / (1.0 + jnp.exp(-x))


def _gelu(x):
    t = jnp.tanh(GELU_C * (x + 0.044715 * (x * x * x)))
    return x * (0.5 * (1.0 + t)), t


def _gelu_grad(x, t):
    return 0.5 * (1.0 + t) + 0.5 * x * (1.0 - t * t) * (GELU_C * (1.0 + 3.0 * 0.044715 * x * x))


def _ln_stats(z):
    mu = jnp.mean(z, axis=-1, keepdims=True)
    zc = z - mu
    var = jnp.mean(zc * zc, axis=-1, keepdims=True)
    rstd = lax.rsqrt(var + LN_EPS)
    return zc * rstd, rstd


def _ln_bwd(dxh, xh, rstd):
    return rstd * (dxh - jnp.mean(dxh, axis=-1, keepdims=True) - xh * jnp.mean(dxh * xh, axis=-1, keepdims=True))


def _rowsum(a):
    return jnp.sum(a, axis=0, keepdims=True)


def _lshape(a):
    return (a.shape[0], a.shape[1]) if a.ndim == 2 else (a.shape[1], a.shape[0] * a.shape[2])


def _spec2(arr, blk_r, blk_c, ridx, cidx):
    if len(arr.shape) == 2:
        return pl.BlockSpec((blk_r, blk_c), lambda i, j, k: (ridx(i, j, k), cidx(i, j, k)))
    per = arr.shape[2] // blk_c
    assert arr.shape[2] % blk_c == 0
    return pl.BlockSpec((None, blk_r, blk_c), lambda i, j, k: (cidx(i, j, k) // per, ridx(i, j, k), cidx(i, j, k) % per))


def _matmul(a, b, *, name, ta=False, tb=False, tm, tn, tk, out_dtype=F32, out_stack=None, bias=None, addend=None):
    ar, ac = _lshape(a)
    br, bc = _lshape(b)
    m, kdim = (ac, ar) if ta else (ar, ac)
    n = br if tb else bc
    assert (bc if tb else br) == kdim
    tm, tn, tk = min(tm, m), min(tn, n), min(tk, kdim)
    assert m % tm == 0 and n % tn == 0 and kdim % tk == 0, (name, m, n, kdim, tm, tn, tk)
    nk = kdim // tk
    gi, gj, gk = (lambda i, j, k: i), (lambda i, j, k: j), (lambda i, j, k: k)
    a_spec = _spec2(a, tk, tm, gk, gi) if ta else _spec2(a, tm, tk, gi, gk)
    b_spec = _spec2(b, tn, tk, gj, gk) if tb else _spec2(b, tk, tn, gk, gj)
    if out_stack is None:
        out_sds = jax.ShapeDtypeStruct((m, n), out_dtype)
    else:
        out_sds = jax.ShapeDtypeStruct((out_stack, m, n // out_stack), out_dtype)
    o_spec = _spec2(out_sds, tm, tn, gi, gj)
    in_specs = [a_spec, b_spec]
    args = [a, b]
    if bias is not None:
        in_specs.append(pl.BlockSpec((1, tn), lambda i, j, k: (0, j)))
        args.append(bias)
    scale = None
    if addend is not None:
        add_arr, scale = addend
        in_specs.append(pl.BlockSpec((tm, tn), lambda i, j, k: (i, j)))
        args.append(add_arr)
    use_acc = nk > 1 and out_dtype != F32
    dn = (((0 if ta else 1,), (1 if tb else 0,)), ((), ()))

    def body(*refs):
        a_ref, b_ref = refs[0], refs[1]
        pos = 2
        bias_ref = add_ref = None
        if bias is not None:
            bias_ref = refs[pos]
            pos += 1
        if addend is not None:
            add_ref = refs[pos]
            pos += 1
        o_ref = refs[pos]
        acc_ref = refs[pos + 1] if use_acc else o_ref
        p = lax.dot_general(a_ref[...].astype(BF16), b_ref[...].astype(BF16), dn, preferred_element_type=F32)

        def finish(val):
            if bias_ref is not None:
                val = val + bias_ref[...]
            if add_ref is not None:
                val = val + scale * add_ref[...]
            return val.astype(out_dtype)

        if nk == 1:
            o_ref[...] = finish(p)
        else:
            k = pl.program_id(2)

            @pl.when(k == 0)
            def _():
                acc_ref[...] = p

            @pl.when(k > 0)
            def _():
                acc_ref[...] += p

            if use_acc or bias_ref is not None or add_ref is not None:
                @pl.when(k == nk - 1)
                def _():
                    o_ref[...] = finish(acc_ref[...])

    return pl.pallas_call(
        body, name=name, grid=(m // tm, n // tn, nk), in_specs=in_specs, out_specs=o_spec, out_shape=out_sds,
        scratch_shapes=[pltpu.VMEM((tm, tn), F32)] if use_acc else [],
        compiler_params=_cp("parallel", "parallel", "arbitrary"),
    )(*args)


def _add_ln_fwd(x, s, g, b, alpha, *, name):
    rows, d = x.shape
    t = _pick(rows, (512, 256))

    def body(x_ref, s_ref, g_ref, b_ref, y_ref):
        xh, _ = _ln_stats(alpha * x_ref[...] + s_ref[...])
        y_ref[...] = xh * g_ref[...] + b_ref[...]

    row = pl.BlockSpec((t, d), lambda i: (i, 0))
    vec = pl.BlockSpec((1, d), lambda i: (0, 0))
    return pl.pallas_call(body, name=name, grid=(rows // t,), in_specs=[row, row, vec, vec], out_specs=row,
                          out_shape=jax.ShapeDtypeStruct((rows, d), F32), compiler_params=_cp("parallel"))(x, s, g, b)


def _add_ln_bwd(dy_terms, x, s, g, alpha, *, name):
    rows, d = x.shape
    t = _pick(rows, (512, 256))
    nterm = len(dy_terms)
    scales = [sc for _, sc in dy_terms]
    ranks = [a.ndim for a, _ in dy_terms]

    def body(*refs):
        dy_refs = refs[:nterm]
        x_ref, s_ref, g_ref, dz_ref, dg_ref, db_ref = refs[nterm:]

        @pl.when(pl.program_id(0) == 0)
        def _():
            dg_ref[...] = jnp.zeros_like(dg_ref)
            db_ref[...] = jnp.zeros_like(db_ref)

        dyv = None
        for r, sc, rank in zip(dy_refs, scales, ranks):
            slabs = [r[...]] if rank == 2 else [r[p] for p in range(r.shape[0])]
            for v in slabs:
                v = v if sc == 1.0 else sc * v
                dyv = v if dyv is None else dyv + v
        xh, rstd = _ln_stats(alpha * x_ref[...] + s_ref[...])
        dz_ref[...] = _ln_bwd(dyv * g_ref[...], xh, rstd)
        dg_ref[...] += _rowsum(dyv * xh)
        db_ref[...] += _rowsum(dyv)

    row = pl.BlockSpec((t, d), lambda i: (i, 0))
    vec = pl.BlockSpec((1, d), lambda i: (0, 0))
    vsds = jax.ShapeDtypeStruct((1, d), F32)
    dy_specs = [row if a.ndim == 2 else pl.BlockSpec((a.shape[0], t, d), lambda i: (0, i, 0)) for a, _ in dy_terms]
    return pl.pallas_call(body, name=name, grid=(rows // t,), in_specs=dy_specs + [row, row, vec], out_specs=[row, vec, vec],
                          out_shape=[jax.ShapeDtypeStruct((rows, d), F32), vsds, vsds],
                          compiler_params=_cp("arbitrary"))(*[a for a, _ in dy_terms], x, s, g)


def _loss_and_grad(y, tgt, *, name):
    rows, d = y.shape
    t = _pick(rows, (512, 256))

    def body(y_ref, t_ref, l_ref, dy_ref):
        @pl.when(pl.program_id(0) == 0)
        def _():
            l_ref[...] = jnp.zeros_like(l_ref)

        e = y_ref[...] - t_ref[...]
        l_ref[...] += _rowsum(e * e)
        dy_ref[...] = e * (1.0 / d)

    row = pl.BlockSpec((t, d), lambda i: (i, 0))
    vec = pl.BlockSpec((1, d), lambda i: (0, 0))
    return pl.pallas_call(body, name=name, grid=(rows // t,), in_specs=[row, row], out_specs=[vec, row],
                          out_shape=[jax.ShapeDtypeStruct((1, d), F32), jax.ShapeDtypeStruct((rows, d), F32)],
                          compiler_params=_cp("arbitrary"))(y, tgt)


def _col_blocks(width):
    out, pos = [], 0
    while pos < width:
        w = MXU_WIDTH if width - pos >= MXU_WIDTH else width - pos
        out.append(slice(pos, pos + w))
        pos += w
    return out


def _conv3(e, w, b):
    r1 = pltpu.roll(e, 1, 0)
    r2 = pltpu.roll(e, 2, 0)
    return w[0:1, :] * r2 + w[1:2, :] * r1 + w[2:3, :] * e + b, (r2, r1, e)


def _ffn_up_fwd(x, w_up, cw, cb, *, name):
    rows, d = x.shape
    nq, _, tc = w_up.shape
    nj = nq // 2
    f = tc * nj
    tm = _pick(rows, (512, 256))
    blocks = _col_blocks(tc)

    def body(x_ref, wg_ref, wv_ref, cw_ref, cb_ref, hf_ref, f_ref, prev_ref):
        @pl.when(pl.program_id(1) == 0)
        def _():
            prev_ref[...] = jnp.zeros_like(prev_ref)

        xb = x_ref[...].astype(BF16)
        for cs in blocks:
            hc = []
            for s, w_ref in ((0, wg_ref), (1, wv_ref)):
                h = jnp.dot(xb, w_ref[:, cs], preferred_element_type=F32)
                hf_ref[s, :, cs] = h
                e = jnp.concatenate([prev_ref[s, :, cs], h], axis=0)
                prev_ref[s, :, cs] = h[tm - FFN_HALO:]
                y, _ = _conv3(e, cw_ref[s, :, cs], cb_ref[s, :, cs])
                hc.append(y[FFN_HALO:])
            gl, _ = _gelu(hc[0])
            f_ref[:, cs] = (gl * hc[1]).astype(BF16)

    in_specs = [
        pl.BlockSpec((tm, d), lambda j, i: (i, 0)),
        pl.BlockSpec((None, d, tc), lambda j, i: (j, 0, 0)),
        pl.BlockSpec((None, d, tc), lambda j, i: (nj + j, 0, 0)),
        pl.BlockSpec((2, 3, tc), lambda j, i: (0, 0, j)),
        pl.BlockSpec((2, 1, tc), lambda j, i: (0, 0, j)),
    ]
    out_specs = [pl.BlockSpec((2, tm, tc), lambda j, i: (0, i, j)), pl.BlockSpec((tm, tc), lambda j, i: (i, j))]
    out_shape = [jax.ShapeDtypeStruct((2, rows, f), F32), jax.ShapeDtypeStruct((rows, f), BF16)]
    return pl.pallas_call(body, name=name, grid=(nj, rows // tm), in_specs=in_specs, out_specs=out_specs, out_shape=out_shape,
                          scratch_shapes=[pltpu.VMEM((2, FFN_HALO, tc), F32)],
                          compiler_params=_cp("parallel", "arbitrary"))(x, w_up, w_up, cw, cb)


def _ffn_up_bwd(hf, df, x, w_up, cw, cb, *, name):
    _, rows, f = hf.shape
    d = x.shape[1]
    nq, _, tc = w_up.shape
    nj = nq // 2
    tm = _pick(rows, (512, 256))
    hb = tm // FFN_HALO
    once = pl.Buffered(1)
    ni = rows // tm
    last_blk = rows // FFN_HALO - 1
    ext = tm + 2 * FFN_HALO
    tile = slice(FFN_HALO, FFN_HALO + tm)
    blocks = _col_blocks(tc)

    def body(h_ref, hp_ref, hn_ref, d_ref, dn_ref, x_ref, wg_ref, wv_ref, cw_ref, cb_ref, dx_ref, dw_ref, dcw_ref, dcb_ref):
        i = pl.program_id(1)
        first = i == 0
        last = i == ni - 1

        @pl.when(first)
        def _():
            dw_ref[...] = jnp.zeros_like(dw_ref)
            dcw_ref[...] = jnp.zeros_like(dcw_ref)
            dcb_ref[...] = jnp.zeros_like(dcb_ref)

        xt = x_ref[...].astype(BF16).T
        dx = None
        for cs in blocks:
            wc = cs.stop - cs.start
            de = jnp.concatenate([jnp.zeros((FFN_HALO, wc), F32), d_ref[:, cs], jnp.where(last, 0.0, dn_ref[:, cs])], axis=0)
            taps, hc = [], []
            for s in range(2):
                e = jnp.concatenate([jnp.where(first, 0.0, hp_ref[s, :, cs]), h_ref[s, :, cs], hn_ref[s, :, cs]], axis=0)
                y, tp = _conv3(e, cw_ref[s, :, cs], cb_ref[s, :, cs])
                hc.append(y)
                taps.append(tp)
            gl, th = _gelu(hc[0])
            dhc = (de * hc[1] * _gelu_grad(hc[0], th), de * gl)
            for s, w_ref in ((0, wg_ref), (1, wv_ref)):
                w = cw_ref[s, :, cs]
                g = dhc[s]
                dh = (w[2:3, :] * g + w[1:2, :] * pltpu.roll(g, ext - 1, 0) + w[0:1, :] * pltpu.roll(g, ext - 2, 0))[tile]
                gt = g[tile]
                for k in range(3):
                    dcw_ref[s, k:k + 1, cs] += _rowsum(gt * taps[s][k][tile])
                dcb_ref[s, :, cs] += _rowsum(gt)
                dhb = dh.astype(BF16)
                part = lax.dot_general(dhb, w_ref[:, cs], (((1,), (1,)), ((), ())), preferred_element_type=F32)
                dx = part if dx is None else dx + part
                dw_ref[s, :, cs] += jnp.dot(xt, dhb, preferred_element_type=F32)
        dx_ref[...] = dx

    in_specs = [
        pl.BlockSpec((2, tm, tc), lambda j, i: (0, i, j)),
        pl.BlockSpec((2, FFN_HALO, tc), lambda j, i: (0, jnp.maximum(i * hb - 1, 0), j)),
        pl.BlockSpec((2, FFN_HALO, tc), lambda j, i: (0, jnp.minimum((i + 1) * hb, last_blk), j)),
        pl.BlockSpec((tm, tc), lambda j, i: (i, j)),
        pl.BlockSpec((FFN_HALO, tc), lambda j, i: (jnp.minimum((i + 1) * hb, last_blk), j)),
        pl.BlockSpec((tm, d), lambda j, i: (i, 0)),
        pl.BlockSpec((None, d, tc), lambda j, i: (j, 0, 0), pipeline_mode=once),
        pl.BlockSpec((None, d, tc), lambda j, i: (nj + j, 0, 0), pipeline_mode=once),
        pl.BlockSpec((2, 3, tc), lambda j, i: (0, 0, j)),
        pl.BlockSpec((2, 1, tc), lambda j, i: (0, 0, j)),
    ]
    out_specs = [
        pl.BlockSpec((None, tm, d), lambda j, i: (j, i, 0)),
        pl.BlockSpec((2, None, d, tc), lambda j, i: (0, j, 0, 0), pipeline_mode=once),
        pl.BlockSpec((2, 3, tc), lambda j, i: (0, 0, j)),
        pl.BlockSpec((2, 1, tc), lambda j, i: (0, 0, j)),
    ]
    out_shape = [jax.ShapeDtypeStruct((nj, rows, d), F32), jax.ShapeDtypeStruct((2, nj, d, tc), F32),
                 jax.ShapeDtypeStruct((2, 3, f), F32), jax.ShapeDtypeStruct((2, 1, f), F32)]
    dx, dw, dcw, dcb = pl.pallas_call(body, name=name, grid=(nj, ni), in_specs=in_specs, out_specs=out_specs,
                                      out_shape=out_shape, compiler_params=_cp("parallel", "arbitrary"))(
        hf, hf, hf, df, df, x, w_up, w_up, cw, cb)
    return dx, dw.reshape(nq, d, tc), dcw, dcb


def _mixer_fwd(h0, cw, cb, ga, ba, gb, bb, ws, sbb, *, name):
    _, rows, w = h0.shape
    t = _pick(rows, (256,))
    hb = t // CONV_HALO
    groups = w // B_CHUNK

    def body(h_ref, hp_ref, cw_ref, cb_ref, ga_ref, ba_ref, gb_ref, bb_ref, ws_ref, sb_ref, o_ref):
        first = pl.program_id(0) == 0
        a1 = h_ref[0] * _sig(h_ref[1])
        a1p = jnp.where(first, 0.0, hp_ref[0] * _sig(hp_ref[1]))
        e = jnp.concatenate([a1p, a1], axis=0)
        acc = cw_ref[A_KERNEL - 1:A_KERNEL, :] * e
        for k in range(A_KERNEL - 1):
            acc = acc + cw_ref[k:k + 1, :] * pltpu.roll(e, A_KERNEL - 1 - k, 0)
        xh, _ = _ln_stats(acc[CONV_HALO:] + cb_ref[...])
        a3 = xh * ga_ref[...] + ba_ref[...]
        o_ref[:, 0:w] = (a3 * _sig(a3)).astype(BF16)

        u, _ = _gelu(h_ref[2])
        v1, _ = _gelu(h_ref[3])
        xh2, _ = _ln_stats(v1)
        v2 = (xh2 * gb_ref[...] + bb_ref[...]).astype(BF16)
        for c in range(t // B_CHUNK):
            rs = slice(c * B_CHUNK, (c + 1) * B_CHUNK)
            for g in range(groups):
                cs = slice(g * B_CHUNK, (g + 1) * B_CHUNK)
                mixed = jnp.dot(ws_ref[g], v2[rs, cs], preferred_element_type=F32) + sb_ref[g]
                o_ref[rs, w + g * B_CHUNK:w + (g + 1) * B_CHUNK] = (u[rs, cs] * mixed).astype(BF16)

    vec = pl.BlockSpec((1, w), lambda i: (0, 0))
    grp = pl.BlockSpec((groups, B_CHUNK, B_CHUNK), lambda i: (0, 0, 0))
    in_specs = [
        pl.BlockSpec((4, t, w), lambda i: (0, i, 0)),
        pl.BlockSpec((2, CONV_HALO, w), lambda i: (0, jnp.maximum(i * hb - 1, 0), 0)),
        pl.BlockSpec((A_KERNEL, w), lambda i: (0, 0)),
        vec, vec, vec, vec, vec, grp, grp,
    ]
    return pl.pallas_call(body, name=name, grid=(rows // t,), in_specs=in_specs,
                          out_specs=pl.BlockSpec((t, 2 * w), lambda i: (i, 0)),
                          out_shape=jax.ShapeDtypeStruct((rows, 2 * w), BF16),
                          compiler_params=_cp("parallel"))(h0, h0, cw, cb, ga, ba, gb, bb, ws, sbb)


def _mixer_bwd(h0, dab, cw, cb, ga, ba, gb, bb, ws, wst, sbb, tril, *, name):
    _, rows, w = h0.shape
    t = _pick(rows, (256,))
    hb = t // CONV_HALO
    ni = rows // t
    last_blk = rows // CONV_HALO - 1
    ext = t + 2 * CONV_HALO
    tile = slice(CONV_HALO, CONV_HALO + t)
    groups = w // B_CHUNK
    taps = A_KERNEL - 1

    def body(h_ref, hp_ref, hn_ref, d_ref, dn_ref, cw_ref, cb_ref, ga_ref, ba_ref, gb_ref, bb_ref, ws_ref, wst_ref,
             sb_ref, tril_ref, dh_ref, dcw_ref, dcb_ref, dga_ref, dba_ref, dgb_ref, dbb_ref, dws_ref, dsb_ref):
        i = pl.program_id(0)
        first = i == 0
        last = i == ni - 1

        @pl.when(first)
        def _():
            for r in (dcw_ref, dcb_ref, dga_ref, dba_ref, dgb_ref, dbb_ref, dws_ref, dsb_ref):
                r[...] = jnp.zeros_like(r)

        av_e = jnp.concatenate([hp_ref[0], h_ref[0], hn_ref[0]], axis=0)
        sg_e = _sig(jnp.concatenate([hp_ref[1], h_ref[1], hn_ref[1]], axis=0))
        rows_e = lax.broadcasted_iota(jnp.int32, (ext, 1), 0)
        a1_e = jnp.where(first & (rows_e < CONV_HALO), 0.0, av_e * sg_e)
        acc = cw_ref[taps:taps + 1, :] * a1_e
        for k in range(taps):
            acc = acc + cw_ref[k:k + 1, :] * pltpu.roll(a1_e, taps - k, 0)
        xh, rstd = _ln_stats(acc + cb_ref[...])
        a3 = xh * ga_ref[...] + ba_ref[...]
        s3 = _sig(a3)
        da_e = jnp.concatenate([jnp.zeros((CONV_HALO, w), F32), d_ref[:, 0:w], jnp.where(last, 0.0, dn_ref[...])], axis=0)
        da3 = da_e * (s3 * (1.0 + a3 * (1.0 - s3)))
        da2 = _ln_bwd(da3 * ga_ref[...], xh, rstd)
        dga_ref[...] += _rowsum(da3[tile] * xh[tile])
        dba_ref[...] += _rowsum(da3[tile])
        da2t = da2[tile]
        dcb_ref[...] += _rowsum(da2t)
        dcw_ref[taps:taps + 1, :] += _rowsum(da2t * a1_e[tile])
        da1 = cw_ref[taps:taps + 1, :] * da2
        for k in range(taps):
            sh = taps - k
            dcw_ref[k:k + 1, :] += _rowsum(da2t * pltpu.roll(a1_e, sh, 0)[tile])
            da1 = da1 + cw_ref[k:k + 1, :] * pltpu.roll(da2, ext - sh, 0)
        da1t = da1[tile]
        sgt = sg_e[tile]
        dh_ref[0] = (da1t * sgt).astype(BF16)
        dh_ref[1] = (da1t * h_ref[0] * sgt * (1.0 - sgt)).astype(BF16)

        bu = h_ref[2]
        bv = h_ref[3]
        u, tu = _gelu(bu)
        v1, tv = _gelu(bv)
        xh2, rstd2 = _ln_stats(v1)
        v2 = (xh2 * gb_ref[...] + bb_ref[...]).astype(BF16)
        db = d_ref[:, w:2 * w]
        dmx_all = db * u
        du_parts, dv2_parts = [], []
        for c in range(t // B_CHUNK):
            rs = slice(c * B_CHUNK, (c + 1) * B_CHUNK)
            du_row, dv2_row = [], []
            for g in range(groups):
                cs = slice(g * B_CHUNK, (g + 1) * B_CHUNK)
                v2cg = v2[rs, cs]
                mixed = jnp.dot(ws_ref[g], v2cg, preferred_element_type=F32) + sb_ref[g]
                dmx = dmx_all[rs, cs]
                dmxb = dmx.astype(BF16)
                du_row.append(db[rs, cs] * mixed)
                dv2_row.append(jnp.dot(wst_ref[g], dmxb, preferred_element_type=F32))
                dws_ref[g] += tril_ref[...] * lax.dot_general(dmxb, v2cg, (((1,), (1,)), ((), ())),
                                                               preferred_element_type=F32)
                dsb_ref[g:g + 1, :] += _rowsum(dmx.T)
            du_parts.append(jnp.concatenate(du_row, axis=1))
            dv2_parts.append(jnp.concatenate(dv2_row, axis=1))
        du = jnp.concatenate(du_parts, axis=0)
        dv2 = jnp.concatenate(dv2_parts, axis=0)
        dgb_ref[...] += _rowsum(dv2 * xh2)
        dbb_ref[...] += _rowsum(dv2)
        dv1 = _ln_bwd(dv2 * gb_ref[...], xh2, rstd2)
        dh_ref[2] = (du * _gelu_grad(bu, tu)).astype(BF16)
        dh_ref[3] = (dv1 * _gelu_grad(bv, tv)).astype(BF16)

    vec = pl.BlockSpec((1, w), lambda i: (0, 0))
    grp = pl.BlockSpec((groups, B_CHUNK, B_CHUNK), lambda i: (0, 0, 0))
    in_specs = [
        pl.BlockSpec((4, t, w), lambda i: (0, i, 0)),
        pl.BlockSpec((2, CONV_HALO, w), lambda i: (0, jnp.maximum(i * hb - 1, 0), 0)),
        pl.BlockSpec((2, CONV_HALO, w), lambda i: (0, jnp.minimum((i + 1) * hb, last_blk), 0)),
        pl.BlockSpec((t, 2 * w), lambda i: (i, 0)),
        pl.BlockSpec((CONV_HALO, w), lambda i: (jnp.minimum((i + 1) * hb, last_blk), 0)),
        pl.BlockSpec((A_KERNEL, w), lambda i: (0, 0)),
        vec, vec, vec, vec, vec, grp, grp, grp,
        pl.BlockSpec((B_CHUNK, B_CHUNK), lambda i: (0, 0)),
    ]
    vsds = jax.ShapeDtypeStruct((1, w), F32)
    out_specs = [
        pl.BlockSpec((4, t, w), lambda i: (0, i, 0)),
        pl.BlockSpec((A_KERNEL, w), lambda i: (0, 0)),
        vec, vec, vec, vec, vec, grp,
        pl.BlockSpec((groups, B_CHUNK), lambda i: (0, 0)),
    ]
    out_shape = [jax.ShapeDtypeStruct((4, rows, w), BF16), jax.ShapeDtypeStruct((A_KERNEL, w), F32),
                 vsds, vsds, vsds, vsds, vsds, jax.ShapeDtypeStruct((groups, B_CHUNK, B_CHUNK), F32),
                 jax.ShapeDtypeStruct((groups, B_CHUNK), F32)]
    return pl.pallas_call(body, name=name, grid=(ni,), in_specs=in_specs, out_specs=out_specs, out_shape=out_shape,
                          compiler_params=_cp("arbitrary"))(h0, h0, h0, dab, dab, cw, cb, ga, ba, gb, bb, ws, wst, sbb, tril)


GROUP_ROWS = Q_PER_KV * ATT_BLOCK


def _attn_mask(n):
    qi = lax.broadcasted_iota(jnp.int32, (GROUP_ROWS, 2 * ATT_BLOCK), 0) & (ATT_BLOCK - 1)
    sj = lax.broadcasted_iota(jnp.int32, (GROUP_ROWS, 2 * ATT_BLOCK), 1)
    diff = qi + ATT_BLOCK - sj
    return (diff >= 0) & (diff < ATT_BLOCK) & ((n > 0) | (sj >= ATT_BLOCK))


def _stack_heads(ref, kvh, dtype):
    heads = [ref[:, (kvh * Q_PER_KV + g) * HEAD_DIM:(kvh * Q_PER_KV + g + 1) * HEAD_DIM] for g in range(Q_PER_KV)]
    return jnp.concatenate(heads, axis=0).astype(dtype)


def _per_row_sink(sink_ref, kvh):
    head = lax.broadcasted_iota(jnp.int32, (GROUP_ROWS, 1), 0) // ATT_BLOCK
    out = jnp.zeros((GROUP_ROWS, 1), F32)
    for g in range(Q_PER_KV):
        out = jnp.where(head == g, sink_ref[kvh * Q_PER_KV + g], out)
    return out


def _attn_specs(rows, n_q):
    dq = n_q * HEAD_DIM
    dkv = 2 * (n_q // Q_PER_KV) * HEAD_DIM
    kv_blk = dq // dkv
    assert dq % dkv == 0
    return dq, dkv, [
        pl.BlockSpec(memory_space=pltpu.SMEM),
        pl.BlockSpec((ATT_BLOCK, dq), lambda n: (n, 0)),
        pl.BlockSpec((ATT_BLOCK, dkv), lambda n: (n, kv_blk)),
        pl.BlockSpec((ATT_BLOCK, dkv), lambda n: (jnp.maximum(n - 1, 0), kv_blk)),
    ]


def _kv_pair(kvc_ref, kvp_ref, kvh, n_kv):
    ks = slice(kvh * HEAD_DIM, (kvh + 1) * HEAD_DIM)
    vs = slice((n_kv + kvh) * HEAD_DIM, (n_kv + kvh + 1) * HEAD_DIM)
    kk = jnp.concatenate([kvp_ref[:, ks], kvc_ref[:, ks]], axis=0).astype(BF16)
    vv = jnp.concatenate([kvp_ref[:, vs], kvc_ref[:, vs]], axis=0).astype(BF16)
    return kk, vv


def _attn_fwd(qkv, sinks, *, name):
    rows = qkv.shape[0]
    n_q = sinks.shape[0]
    n_kv = n_q // Q_PER_KV
    scale = 1.0 / math.sqrt(HEAD_DIM)
    dq, _, in_specs = _attn_specs(rows, n_q)

    def body(sink_ref, q_ref, kvc_ref, kvp_ref, o_ref, lse_ref):
        valid = _attn_mask(pl.program_id(0))
        for kvh in range(n_kv):
            kk, vv = _kv_pair(kvc_ref, kvp_ref, kvh, n_kv)
            qs = _stack_heads(q_ref, kvh, BF16)
            s = lax.dot_general(qs, kk, (((1,), (1,)), ((), ())), preferred_element_type=F32)
            s = jnp.where(valid, s * scale, -jnp.inf)
            sk = _per_row_sink(sink_ref, kvh)
            m = jnp.maximum(jnp.max(s, axis=1, keepdims=True), sk)
            p = jnp.exp(s - m)
            l = jnp.sum(p, axis=1, keepdims=True) + jnp.exp(sk - m)
            o = jnp.dot((p / l).astype(BF16), vv, preferred_element_type=F32)
            lse = m + jnp.log(l)
            for g in range(Q_PER_KV):
                h = kvh * Q_PER_KV + g
                rs = slice(g * ATT_BLOCK, (g + 1) * ATT_BLOCK)
                o_ref[:, h * HEAD_DIM:(h + 1) * HEAD_DIM] = o[rs]
                lse_ref[:, h:h + 1] = lse[rs]

    out_specs = [pl.BlockSpec((ATT_BLOCK, dq), lambda n: (n, 0)), pl.BlockSpec((ATT_BLOCK, n_q), lambda n: (n, 0))]
    out_shape = [jax.ShapeDtypeStruct((rows, dq), F32), jax.ShapeDtypeStruct((rows, n_q), F32)]
    return pl.pallas_call(body, name=name, grid=(rows // ATT_BLOCK,), in_specs=in_specs, out_specs=out_specs,
                          out_shape=out_shape, compiler_params=_cp("parallel"))(sinks, qkv, qkv, qkv)


def _attn_bwd(qkv, dout, lse, sinks, *, name):
    rows = qkv.shape[0]
    n_q = sinks.shape[0]
    n_kv = n_q // Q_PER_KV
    scale = 1.0 / math.sqrt(HEAD_DIM)
    dq_w, dkv_w, in_specs = _attn_specs(rows, n_q)
    blk_q = pl.BlockSpec((ATT_BLOCK, dq_w), lambda n: (n, 0))
    blk_kv = pl.BlockSpec((ATT_BLOCK, dkv_w), lambda n: (n, 0))
    in_specs = in_specs + [blk_q, pl.BlockSpec((ATT_BLOCK, n_q), lambda n: (n, 0))]

    def body(sink_ref, q_ref, kvc_ref, kvp_ref, do_ref, lse_ref, dq_ref, dkc_ref, dkp_ref, dsink_ref):
        n = pl.program_id(0)

        @pl.when(n == 0)
        def _():
            dsink_ref[...] = jnp.zeros_like(dsink_ref)

        valid = _attn_mask(n)
        head_ids = lax.broadcasted_iota(jnp.int32, (1, n_q), 1)
        dsink = jnp.zeros((1, n_q), F32)
        for kvh in range(n_kv):
            kk, vv = _kv_pair(kvc_ref, kvp_ref, kvh, n_kv)
            qs = _stack_heads(q_ref, kvh, BF16)
            dos = _stack_heads(do_ref, kvh, BF16)
            lse = jnp.concatenate([lse_ref[:, kvh * Q_PER_KV + g:kvh * Q_PER_KV + g + 1] for g in range(Q_PER_KV)], axis=0)
            s = lax.dot_general(qs, kk, (((1,), (1,)), ((), ())), preferred_element_type=F32)
            s = jnp.where(valid, s * scale, -jnp.inf)
            p = jnp.exp(s - lse)
            dp = lax.dot_general(dos, vv, (((1,), (1,)), ((), ())), preferred_element_type=F32)
            delta = jnp.sum(p * dp, axis=1, keepdims=True)
            ds = (p * (dp - delta) * scale).astype(BF16)
            sink_term = jnp.exp(_per_row_sink(sink_ref, kvh) - lse) * delta
            dqs = jnp.dot(ds, kk, preferred_element_type=F32)
            for g in range(Q_PER_KV):
                h = kvh * Q_PER_KV + g
                rs = slice(g * ATT_BLOCK, (g + 1) * ATT_BLOCK)
                dsink = dsink + jnp.where(head_ids == h, -jnp.sum(sink_term[rs]), 0.0)
                dq_ref[:, h * HEAD_DIM:(h + 1) * HEAD_DIM] = dqs[rs]
            dk = lax.dot_general(ds, qs, (((0,), (0,)), ((), ())), preferred_element_type=F32)
            dv = lax.dot_general(p.astype(BF16), dos, (((0,), (0,)), ((), ())), preferred_element_type=F32)
            ks = slice(kvh * HEAD_DIM, (kvh + 1) * HEAD_DIM)
            vs = slice((n_kv + kvh) * HEAD_DIM, (n_kv + kvh + 1) * HEAD_DIM)
            dkp_ref[:, ks] = dk[0:ATT_BLOCK]
            dkc_ref[:, ks] = dk[ATT_BLOCK:]
            dkp_ref[:, vs] = dv[0:ATT_BLOCK]
            dkc_ref[:, vs] = dv[ATT_BLOCK:]
        dsink_ref[...] += dsink

    out_specs = [blk_q, blk_kv, blk_kv, pl.BlockSpec((1, n_q), lambda n: (0, 0))]
    out_shape = [jax.ShapeDtypeStruct((rows, dq_w), F32), jax.ShapeDtypeStruct((rows, dkv_w), F32),
                 jax.ShapeDtypeStruct((rows, dkv_w), F32), jax.ShapeDtypeStruct((1, n_q), F32)]
    return pl.pallas_call(body, name=name, grid=(rows // ATT_BLOCK,), in_specs=in_specs, out_specs=out_specs,
                          out_shape=out_shape, compiler_params=_cp("arbitrary"))(sinks, qkv, qkv, qkv, dout, lse)


def _dqkv_assemble(dq, dkc, dkp, *, name):
    rows, dq_w = dq.shape
    dkv_w = dkc.shape[1]
    nb = rows // ATT_BLOCK

    def body(dq_ref, dkc_ref, dkp_ref, o_ref, db_ref):
        n = pl.program_id(0)

        @pl.when(n == 0)
        def _():
            db_ref[...] = jnp.zeros_like(db_ref)

        dqv = dq_ref[...]
        dkv = dkc_ref[...] + jnp.where(n == nb - 1, 0.0, dkp_ref[...])
        o_ref[:, 0:dq_w] = dqv.astype(BF16)
        o_ref[:, dq_w:dq_w + dkv_w] = dkv.astype(BF16)
        db_ref[:, 0:dq_w] += _rowsum(dqv)
        db_ref[:, dq_w:dq_w + dkv_w] += _rowsum(dkv)

    width = dq_w + dkv_w
    in_specs = [pl.BlockSpec((ATT_BLOCK, dq_w), lambda n: (n, 0)), pl.BlockSpec((ATT_BLOCK, dkv_w), lambda n: (n, 0)),
                pl.BlockSpec((ATT_BLOCK, dkv_w), lambda n: (jnp.minimum(n + 1, nb - 1), 0))]
    out_specs = [pl.BlockSpec((ATT_BLOCK, width), lambda n: (n, 0)), pl.BlockSpec((1, width), lambda n: (0, 0))]
    out_shape = [jax.ShapeDtypeStruct((rows, width), BF16), jax.ShapeDtypeStruct((1, width), F32)]
    return pl.pallas_call(body, name=name, grid=(nb,), in_specs=in_specs, out_specs=out_specs, out_shape=out_shape,
                          compiler_params=_cp("arbitrary"))(dq, dkc, dkp)


def _row_tile(r, c):
    budget = 2 * 1024 * 1024 // (4 * c)
    for cand in (1024, 512, 256, 128, 64, 32, 16):
        if cand <= budget and r % cand == 0:
            return cand
    return r


def _to_bf16(a, *, name):
    p, r, c = a.shape
    t = _row_tile(r, c)

    def body(a_ref, o_ref):
        o_ref[...] = a_ref[...].astype(BF16)

    blk = pl.BlockSpec((None, t, c), lambda k, i: (k, i, 0))
    return pl.pallas_call(body, name=name, grid=(p, r // t), in_specs=[blk], out_specs=blk,
                          out_shape=jax.ShapeDtypeStruct(a.shape, BF16), compiler_params=_cp("parallel", "parallel"))(a)


def _octo_sum(own, recv, place, dest, lead, *, name):
    _, _, r, c = own.shape
    t = _row_tile(r, c)
    lead_idx, buf_shape = lead

    def body(place_ref, own_ref, *rest):
        o_ref = rest[7] if dest is None else rest[8]
        acc = own_ref[...].astype(F32)
        for k in range(7):
            acc = acc + rest[k][...].astype(F32)
        o_ref[...] = acc

    def peer(mask):
        return pl.BlockSpec((None, t, c), lambda i, pr: (pr[2] ^ mask, i, 0))

    if lead_idx is None:
        o_spec = pl.BlockSpec((None, t, c), lambda i, pr: (pr[1], i, 0))
    else:
        o_spec = pl.BlockSpec((None, None, t, c), lambda i, pr: (lead_idx, pr[1], i, 0))
    in_specs = [pl.BlockSpec((None, None, t, c), lambda i, pr: (pr[0], pr[1], i, 0))] + [peer(m) for m in range(1, 8)]
    args = [place, own] + [recv] * 7
    aliases = {}
    if dest is not None:
        in_specs.append(HBM)
        args.append(dest)
        aliases = {9: 0}
    grid_spec = pltpu.PrefetchScalarGridSpec(num_scalar_prefetch=1, grid=(r // t,), in_specs=in_specs, out_specs=o_spec)
    return pl.pallas_call(body, name=name, grid_spec=grid_spec, out_shape=jax.ShapeDtypeStruct(buf_shape, F32),
                          input_output_aliases=aliases, compiler_params=_cp("parallel"))(*args)


def _adamw_math(w, g, m, v):
    nm = ADAM_B1 * m + (1.0 - ADAM_B1) * g
    nv = ADAM_B2 * v + (1.0 - ADAM_B2) * (g * g)
    m_hat = nm / (1.0 - ADAM_B1 ** ADAM_STEP)
    v_hat = nv / (1.0 - ADAM_B2 ** ADAM_STEP)
    return -ADAM_LR * (m_hat / (jnp.sqrt(v_hat) + ADAM_EPS) + ADAM_WD * w), nm, nv


def _adamw(w, g, m, v, *, name):
    r, c = w.shape
    t = _row_tile(r, c)

    def body(w_ref, g_ref, m_ref, v_ref, d_ref, nm_ref, nv_ref):
        d_ref[...], nm_ref[...], nv_ref[...] = _adamw_math(w_ref[...], g_ref[...], m_ref[...], v_ref[...])

    blk = pl.BlockSpec((t, c), lambda i: (i, 0))
    sds = jax.ShapeDtypeStruct((r, c), F32)
    return pl.pallas_call(body, name=name, grid=(r // t,), in_specs=[blk] * 4, out_specs=[blk] * 3,
                          out_shape=[sds] * 3, compiler_params=_cp("parallel"))(w, g, m, v)


HBM = pl.BlockSpec(memory_space=pl.ANY)


def _place():
    x, y, c = lax.axis_index("x"), lax.axis_index("y"), lax.axis_index("c")
    chips = [(1 - x, y), (x, 1 - y), (1 - x, 1 - y)]
    return x, y, c, 2 * x + y, (x, y, 1 - c), chips


def _rcopy(src, dst, ssem, rsem, dev):
    return pltpu.make_async_remote_copy(src_ref=src, dst_ref=dst, send_sem=ssem, recv_sem=rsem, device_id=dev,
                                        device_id_type=MESH)


HBM_ONLY = pl.BlockSpec(memory_space=pltpu.HBM)
SEM = pl.BlockSpec(memory_space=pltpu.SEMAPHORE)


def _peers():
    x, y, c = lax.axis_index("x"), lax.axis_index("y"), lax.axis_index("c")
    out = []
    for mask in range(1, 8):
        px = 1 - x if mask & 4 else x
        py = 1 - y if mask & 2 else y
        pc = 1 - c if mask & 1 else c
        out.append(((px, py, pc), 2 * px + py, pc, 4 * px + 2 * py + pc))
    return 4 * x + 2 * y + c, out


def _reduce_start(grads, lands, *, name):
    nt = len(grads)

    def body(*refs):
        ssems, rsems = refs[2 * nt:3 * nt], refs[3 * nt:4 * nt]
        g_out, l_out, token = refs[4 * nt:5 * nt], refs[5 * nt:6 * nt], refs[6 * nt]
        me, peers = _peers()
        for t in range(nt):
            for k, (dev, chip, core, _) in enumerate(peers):
                _rcopy(g_out[t].at[chip, core], l_out[t].at[me], ssems[t].at[k], rsems[t].at[k], dev).start()
        token[...] = jnp.zeros_like(token)

    sems = [pltpu.SemaphoreType.DMA((7,))] * (2 * nt)
    out_shape = (sems + [pltpu.HBM(g.shape, g.dtype) for g in grads] + [pltpu.HBM(l.shape, l.dtype) for l in lands]
                 + [jax.ShapeDtypeStruct((8, LANES), F32)])
    res = pl.pallas_call(
        body, name=name, in_specs=[HBM_ONLY] * (2 * nt),
        out_specs=[SEM] * (2 * nt) + [HBM_ONLY] * (2 * nt) + [pl.BlockSpec(memory_space=pltpu.VMEM)], out_shape=out_shape,
        input_output_aliases={t: 2 * nt + t for t in range(2 * nt)},
        compiler_params=pltpu.CompilerParams(has_side_effects=DATAFLOW),
    )(*[pltpu.with_memory_space_constraint(a, pltpu.HBM) for a in list(grads) + list(lands)])
    return res[:nt], res[nt:2 * nt], res[2 * nt:3 * nt], res[3 * nt:4 * nt], res[4 * nt]


def _reduce_wait(grads, lands, ssems, rsems, after, *, name):
    nt = len(grads)

    def body(*refs):
        ssem_refs, rsem_refs = refs[2 * nt:3 * nt], refs[3 * nt:4 * nt]
        g_out, l_out = refs[4 * nt + 1:5 * nt + 1], refs[5 * nt + 1:6 * nt + 1]
        me, peers = _peers()
        for t in range(nt):
            for k, (dev, chip, core, _) in enumerate(peers):
                _rcopy(g_out[t].at[chip, core], l_out[t].at[me], ssem_refs[t].at[k], rsem_refs[t].at[k], dev).wait_send()
        for t in range(nt):
            for k, (dev, _, _, idx) in enumerate(peers):
                slot = l_out[t].at[idx]
                _rcopy(slot, slot, ssem_refs[t].at[k], rsem_refs[t].at[k], dev).wait_recv()

    res = pl.pallas_call(
        body, name=name, in_specs=[HBM_ONLY] * (2 * nt) + [SEM] * (2 * nt) + [HBM], out_specs=[HBM_ONLY] * (2 * nt),
        out_shape=[pltpu.HBM(a.shape, a.dtype) for a in list(grads) + list(lands)],
        input_output_aliases={t: t for t in range(2 * nt)},
        compiler_params=pltpu.CompilerParams(has_side_effects=DATAFLOW),
    )(*grads, *lands, *ssems, *rsems, after)
    return list(res[:nt]), list(res[nt:])
DATAFLOW = pltpu.SideEffectType.DATAFLOW_SIDE_EFFECTING


def _gather_start(bufs, *, name):
    nt = len(bufs)

    def body(*refs):
        ssems, rsems, outs = refs[nt:2 * nt], refs[2 * nt:3 * nt], refs[3 * nt:4 * nt]
        x, y, c, q, sib, chips = _place()
        for t in range(nt):
            for j, (px, py) in enumerate(chips):
                mine = outs[t].at[q]
                _rcopy(mine, mine, ssems[t].at[j], rsems[t].at[j], (px, py, c)).start()

    sems = [pltpu.SemaphoreType.DMA((3,))] * (2 * nt)
    out_shape = sems + [pltpu.HBM(b.shape, b.dtype) for b in bufs]
    res = pl.pallas_call(
        body, name=name, in_specs=[HBM_ONLY] * nt, out_specs=[SEM] * (2 * nt) + [HBM_ONLY] * nt, out_shape=out_shape,
        input_output_aliases={t: 2 * nt + t for t in range(nt)},
        compiler_params=pltpu.CompilerParams(has_side_effects=DATAFLOW),
    )(*[pltpu.with_memory_space_constraint(b, pltpu.HBM) for b in bufs])
    return res[:nt], res[nt:2 * nt], res[2 * nt:]


def _gather_wait(bufs, ssems, rsems, after, *, name):
    nt = len(bufs)

    def body(*refs):
        ssem_refs, rsem_refs = refs[nt:2 * nt], refs[2 * nt:3 * nt]
        outs = refs[3 * nt + 1:]
        x, y, c, q, sib, chips = _place()
        for t in range(nt):
            for j, (px, py) in enumerate(chips):
                mine = outs[t].at[q]
                _rcopy(mine, mine, ssem_refs[t].at[j], rsem_refs[t].at[j], (px, py, c)).wait_send()
        for t in range(nt):
            for j, (px, py) in enumerate(chips):
                theirs = outs[t].at[2 * px + py]
                _rcopy(theirs, theirs, ssem_refs[t].at[j], rsem_refs[t].at[j], (px, py, c)).wait_recv()

    res = pl.pallas_call(
        body, name=name, in_specs=[HBM_ONLY] * nt + [SEM] * (2 * nt) + [HBM], out_specs=[HBM_ONLY] * nt,
        out_shape=[pltpu.HBM(b.shape, b.dtype) for b in bufs], input_output_aliases={t: t for t in range(nt)},
        compiler_params=pltpu.CompilerParams(has_side_effects=DATAFLOW),
    )(*bufs, *ssems, *rsems, after)
    return list(res)


def _sibling_share(bufs, layout, *, name):
    no = len(bufs)
    nt = len(layout)

    def body(*refs):
        outs = refs[no:2 * no]
        ssem, rsem = refs[2 * no:]
        x, y, c, q, sib, chips = _place()

        def slot(t, half):
            o, lead = layout[t]
            return outs[o].at[half] if lead is None else outs[o].at[lead, half]

        sends = []
        for t in range(nt):
            cp = _rcopy(slot(t, c), slot(t, c), ssem.at[t], rsem.at[t], sib)
            cp.start()
            sends.append(cp)
        for t in range(nt):
            _rcopy(slot(t, 1 - c), slot(t, 1 - c), ssem.at[t], rsem.at[t], sib).wait_recv()
        for cp in sends:
            cp.wait_send()

    out_shape = [jax.ShapeDtypeStruct(b.shape, b.dtype) for b in bufs]
    return pl.pallas_call(
        body, name=name, in_specs=[HBM] * no, out_specs=[HBM] * no, out_shape=out_shape,
        input_output_aliases={o: o for o in range(no)},
        scratch_shapes=[pltpu.SemaphoreType.DMA((nt,)), pltpu.SemaphoreType.DMA((nt,))],
    )(*bufs)


def _small_tail(local, params, *, name):
    (cwa, cba, ga, ba, gb, bb, dws, dsb, dbq, dsk, cwf0, cbf0, cwf1, cbf1,
     g00, g01, g10, g11, b00, b01, b10, b11, err) = local
    n_local = len(local)
    kw, wa = cwa.shape
    ng = dws.shape[0]
    nqkv = dbq.shape[1]
    nsk = dsk.shape[1]
    f = cwf0.shape[2]
    dm = err.shape[1]
    row_vec = 8 * (-(-kw // 8))
    shapes = [(row_vec + 8, wa), (ng * B_CHUNK + 8, B_CHUNK), (8, nqkv), (2, 2, 8, f), (16, dm)]
    n_grp = len(shapes)
    flat_params = [a for triple in params for a in triple]
    n_par = len(params)

    def reduce_body(*refs):
        loc = refs[:n_local]
        tot = refs[n_local:n_local + n_grp]
        scr = refs[n_local + n_grp:]
        grp, from_sib, pair, gath = (scr[k * n_grp:(k + 1) * n_grp] for k in range(4))
        ssem1, rsem1, ssem2, rsem2 = scr[4 * n_grp:]
        x, y, core, q, sib, chips = _place()

        for gr in grp:
            gr[...] = jnp.zeros_like(gr)
        a, b, c, dd, e = grp
        a[0:kw, :] = loc[0][...]
        for k in range(5):
            a[row_vec + k:row_vec + k + 1, :] = loc[1 + k][...]
        for g in range(ng):
            b[g * B_CHUNK:(g + 1) * B_CHUNK, :] = loc[6][g]
        b[ng * B_CHUNK:ng * B_CHUNK + ng, :] = loc[7][...]
        c[0:1, :] = loc[8][...]
        c[1:2, 0:nsk] = loc[9][...]
        for l in range(2):
            for s in range(2):
                dd[l, s, 0:3, :] = loc[10 + 2 * l][s]
                dd[l, s, 3:4, :] = loc[11 + 2 * l][s]
        for k in range(9):
            e[k:k + 1, :] = loc[14 + k][...]

        sends = []
        for gi in range(n_grp):
            cp = _rcopy(grp[gi], from_sib[gi], ssem1.at[gi], rsem1.at[gi], sib)
            cp.start()
            sends.append(cp)
        for gi in range(n_grp):
            _rcopy(grp[gi], from_sib[gi], ssem1.at[gi], rsem1.at[gi], sib).wait_recv()
            both = grp[gi][...] + from_sib[gi][...]
            pair[gi][...] = both
            gath[gi][q] = both
            for j, (px, py) in enumerate(chips):
                cp = _rcopy(pair[gi], gath[gi].at[q], ssem2.at[gi, j], rsem2.at[gi, j], (px, py, core))
                cp.start()
                sends.append(cp)
        for gi in range(n_grp):
            for j, (px, py) in enumerate(chips):
                slot = gath[gi].at[2 * px + py]
                _rcopy(slot, slot, ssem2.at[gi, j], rsem2.at[gi, j], (px, py, core)).wait_recv()
            acc = gath[gi][0]
            for k in range(1, 4):
                acc = acc + gath[gi][k]
            tot[gi][...] = acc
        for cp in sends:
            cp.wait_send()

    vm = pl.BlockSpec(memory_space=pltpu.VMEM)
    scratch = ([pltpu.VMEM(s, F32) for s in shapes] * 3 + [pltpu.VMEM((4,) + s, F32) for s in shapes]
               + [pltpu.SemaphoreType.DMA((n_grp,)), pltpu.SemaphoreType.DMA((n_grp,)),
                  pltpu.SemaphoreType.DMA((n_grp, 3)), pltpu.SemaphoreType.DMA((n_grp, 3))])
    totals = pl.pallas_call(
        reduce_body, name=name + "_reduce", in_specs=[vm] * n_local, out_specs=[vm] * n_grp,
        out_shape=[jax.ShapeDtypeStruct(s, F32) for s in shapes], scratch_shapes=scratch,
        compiler_params=pltpu.CompilerParams(vmem_limit_bytes=VMEM_LIMIT),
    )(*local)

    def adamw_body(*refs):
        ta, tb, tc, td, te = refs[:n_grp]
        par = refs[n_grp:n_grp + 3 * n_par]
        outs = refs[n_grp + 3 * n_par:n_grp + 7 * n_par]
        loss_ref = refs[n_grp + 7 * n_par]
        q = 2 * lax.axis_index("x") + lax.axis_index("y")

        def mine(piece):
            out = piece(0)
            for k in range(1, 4):
                out = jnp.where(q == k, piece(k), out)
            return out

        def update(p, grad, index=None):
            at = (lambda r: r[...]) if index is None else (lambda r: r[index])
            w_ref, m_ref, v_ref = par[3 * p:3 * p + 3]
            g_ref, d_ref, nm_ref, nv_ref = outs[4 * p:4 * p + 4]
            delta, nm, nv = _adamw_math(at(w_ref), grad, at(m_ref), at(v_ref))
            for r, val in ((g_ref, grad), (d_ref, delta), (nm_ref, nm), (nv_ref, nv)):
                if index is None:
                    r[...] = val
                else:
                    r[index] = val

        wq = wa // 4
        update(0, mine(lambda k: ta[0:kw, k * wq:(k + 1) * wq]), (0,))
        for k in range(5):
            update(1 + k, ta[row_vec + k:row_vec + k + 1, :])
        for g in range(ng):
            update(6, tb[g * B_CHUNK:(g + 1) * B_CHUNK, :], (0, g))
        update(7, tb[ng * B_CHUNK:ng * B_CHUNK + ng, :], (0,))
        nq4 = nqkv // 4
        update(8, mine(lambda k: tc[0:1, k * nq4:(k + 1) * nq4]))
        update(9, tc[1:2, 0:nsk])
        fh = f // 2
        for l in range(2):
            update(10, mine(lambda k: td[l, k // 2, 0:3, (k % 2) * fh:(k % 2 + 1) * fh]), (l,))
            update(11, jnp.concatenate([td[l, 0, 3:4, :], td[l, 1, 3:4, :]], axis=1), (slice(l, l + 1),))
        dq4 = dm // 4
        for i in range(2):
            for j in range(2):
                for p, base in ((12, 0), (13, 4)):
                    row = base + 2 * i + j
                    update(p, mine(lambda k: te[row:row + 1, k * dq4:(k + 1) * dq4]), (i, slice(j, j + 1)))
        loss_ref[...] = (0.5 / dm) * jnp.sum(te[8:9, :], axis=1, keepdims=True)

    out_shape = []
    for w, _, _ in params:
        out_shape += [jax.ShapeDtypeStruct(w.shape, F32)] * 4
    out_shape.append(jax.ShapeDtypeStruct((1, 1), F32))
    res = pl.pallas_call(
        adamw_body, name=name + "_adamw", in_specs=[vm] * (n_grp + 3 * n_par), out_specs=[vm] * len(out_shape),
        out_shape=out_shape, compiler_params=pltpu.CompilerParams(vmem_limit_bytes=VMEM_LIMIT),
    )(*totals, *flat_params)
    return [res[4 * p:4 * p + 4] for p in range(n_par)], res[-1]


def _pack(arrays, rows_multiple):
    flat = jnp.concatenate([a.reshape(-1) for a in arrays])
    rows = -(-flat.shape[0] // LANES)
    rows = -(-rows // rows_multiple) * rows_multiple
    flat = jnp.pad(flat, (0, rows * LANES - flat.shape[0]))
    return flat.reshape(rows, LANES)


def _unpack(buf, shapes):
    flat = buf.reshape(-1)
    out, pos = [], 0
    for s in shapes:
        n = math.prod(s)
        out.append(flat[pos:pos + n].reshape(s))
        pos += n
    return out


def _unshard_cols(stacked):
    moved = jnp.moveaxis(stacked, 0, -2)
    return moved.reshape(moved.shape[:-2] + (4 * stacked.shape[-1],))


def _shard_cols(full, q):
    n = full.shape[-1] // 4
    return lax.dynamic_slice_in_dim(full, q * n, n, axis=full.ndim - 1)


def kernel(x, ab_w_in, a_conv_w, a_conv_b, a_norm_g, a_norm_b, b_norm_g, b_norm_b, b_spatial_w, b_spatial_b, ab_w_out, c_w_qkv, c_b_qkv, c_sinks, c_w_o, ffn_w_up, ffn_conv_w, ffn_conv_b, ffn_w_down, ln_g, ln_b, loss_target, m_ab_w_in, m_a_conv_w, m_a_conv_b, m_a_norm_g, m_a_norm_b, m_b_norm_g, m_b_norm_b, m_b_spatial_w, m_b_spatial_b, m_ab_w_out, m_c_w_qkv, m_c_b_qkv, m_c_sinks, m_c_w_o, m_ffn_w_up, m_ffn_conv_w, m_ffn_conv_b, m_ffn_w_down, m_ln_g, m_ln_b, v_ab_w_in, v_a_conv_w, v_a_conv_b, v_a_norm_g, v_a_norm_b, v_b_norm_g, v_b_norm_b, v_b_spatial_w, v_b_spatial_b, v_ab_w_out, v_c_w_qkv, v_c_b_qkv, v_c_sinks, v_c_w_o, v_ffn_w_up, v_ffn_conv_w, v_ffn_conv_b, v_ffn_w_down, v_ln_g, v_ln_b):
    rows, d = x.shape[1], x.shape[2]
    depth = ln_g.shape[0]
    assert depth == 2 and x.shape[0] == 1
    alpha = (2.0 * depth) ** 0.25
    f = ffn_w_down.shape[1] * 4
    n_q = c_sinks.shape[1]
    q_idx = 2 * lax.axis_index("x") + lax.axis_index("y")
    c_idx = lax.axis_index("c")
    xs, tgt = x[0], loss_target[0]

    def own_slot(part):
        buf = lax.empty((4,) + part.shape, part.dtype)
        return lax.dynamic_update_slice(buf, part[None], (q_idx, 0, 0, 0))

    def halves(wm):
        return own_slot(wm.astype(BF16).reshape((2, wm.shape[0] // 2) + wm.shape[1:]))

    small_sharded = [a_conv_w[0], c_b_qkv[0], ffn_conv_w, ln_g, ln_b]
    small_pack = _pack(small_sharded, 16)
    bufs = [halves(ab_w_in[0]), own_slot(small_pack.reshape(2, small_pack.shape[0] // 2, LANES)), halves(ab_w_out[0]),
            halves(ffn_w_up[0]), halves(ffn_w_down[0]), halves(c_w_qkv[0]), halves(c_w_o[0]),
            halves(ffn_w_up[1]), halves(ffn_w_down[1])]
    ssems, rsems, started = _gather_start(bufs, name="gather_start")

    def arrive(idx, after, tag):
        got = _gather_wait([started[i] for i in idx], [ssems[i] for i in idx], [rsems[i] for i in idx], after,
                           name=f"gather_wait_{tag}")
        return [g.reshape(4, 2 * g.shape[2], g.shape[3]) for g in got]

    w_in, small_all = arrive([0, 1], xs, "in")
    small_all = small_all.reshape(4, -1)
    sh_shapes = [s.shape for s in small_sharded]
    pieces, pos = [], 0
    for s in sh_shapes:
        n = math.prod(s)
        pieces.append(_unshard_cols(small_all[:, pos:pos + n].reshape((4,) + s)))
        pos += n
    conv_w_a, b_qkv, conv_w_f, ln_gf, ln_bf = pieces

    tril = jnp.tril(jnp.ones((B_CHUNK, B_CHUNK), F32))
    ws = (b_spatial_w[0] * tril).astype(BF16)
    wst = jnp.swapaxes(ws, 1, 2)
    sbb = jnp.broadcast_to(b_spatial_b[0][:, :, None], b_spatial_w[0].shape)
    mix_vecs = [a_conv_b, a_norm_g, a_norm_b, b_norm_g, b_norm_b]
    cw_f = [jnp.swapaxes(conv_w_f[l].reshape(3, 2, f), 0, 1) for l in range(depth)]
    cb_f = [ffn_conv_b[l].reshape(2, 1, f) for l in range(depth)]
    lng = lambda i, j: ln_gf[i, j].reshape(1, d)
    lnb = lambda i, j: ln_bf[i, j].reshape(1, d)
    sinks = c_sinks[0]

    w_up, w_down = [None, None], [None, None]

    def ffn_fwd(xin, l):
        w_up[l], = arrive([3 + 4 * l], xin, f"up{l}")
        hf, fact = _ffn_up_fwd(xin, w_up[l], cw_f[l], cb_f[l], name=f"ffn{l}_up")
        w_down[l] = arrive([4 + 4 * l], fact, f"down{l}")[0].reshape(-1, d)
        out = _matmul(fact, w_down[l], name=f"ffn{l}_down", tm=1024, tn=1024, tk=1408)
        return hf, fact, out

    h0 = _matmul(xs, w_in, name="mix_in", tm=1024, tn=512, tk=1024, out_stack=4)
    ab = _mixer_fwd(h0, conv_w_a, *mix_vecs, ws, sbb, name="mix_mid")
    w_out = arrive([2], ab, "out")[0].reshape(-1, d)
    mix = _matmul(ab, w_out, name="mix_out", tm=1024, tn=1024, tk=1024)
    x1 = _add_ln_fwd(xs, mix, lng(0, 0), lnb(0, 0), alpha, name="ln00")
    hf0, f0, ffn0 = ffn_fwd(x1, 0)
    x2 = _add_ln_fwd(x1, ffn0, lng(0, 1), lnb(0, 1), alpha, name="ln01")
    w_qkv = _unshard_cols(arrive([5], x2, "qkv")[0])
    qkv = _matmul(x2, w_qkv, name="att_qkv", tm=1024, tn=w_qkv.shape[1], tk=1024, bias=b_qkv.reshape(1, -1))
    ao, lse = _attn_fwd(qkv, sinks, name="att_core")
    w_o = arrive([6], ao, "o")[0].reshape(-1, d)
    att = _matmul(ao, w_o, name="att_out", tm=1024, tn=1024, tk=1024)
    x3 = _add_ln_fwd(x2, att, lng(1, 0), lnb(1, 0), alpha, name="ln10")
    hf1, f1, ffn1 = ffn_fwd(x3, 1)
    x4 = _add_ln_fwd(x3, ffn1, lng(1, 1), lnb(1, 1), alpha, name="ln11")
    sq_err, dy = _loss_and_grad(x4, tgt, name="loss")

    def owner_view(g):
        if g.ndim == 3:
            return g.reshape(4, 2, g.shape[1] // 2, g.shape[2])
        return g.reshape(4, 2, g.shape[0] // 8, g.shape[1])

    in_flight = []

    def send_grads(tag, grads):
        lands = [lax.empty((8,) + g.shape[2:], BF16) for g in grads]
        ss, rs, g_thru, l_thru, token = _reduce_start(grads, lands, name=f"reduce_start_{tag}")
        in_flight.append((tag, g_thru, l_thru, ss, rs))
        return token[0:1, 0:1]

    def ffn_bwd(dz, xin, hf, fact, l):
        d_wdown = _matmul(fact, dz, name=f"ffn{l}_down_dw", ta=True, tm=1408, tn=1024, tk=512, out_dtype=BF16)
        dfa = _matmul(dz, w_down[l], name=f"ffn{l}_down_dx", tb=True, tm=1024, tn=1408, tk=1024)
        dx_parts, d_wup, dcw, dcb = _ffn_up_bwd(hf, dfa, xin, w_up[l], cw_f[l], cb_f[l], name=f"ffn{l}_up_bwd")
        d_wup = _to_bf16(d_wup, name=f"ffn{l}_up_dw_bf16")
        tok = send_grads(f"ffn{l}", [owner_view(d_wup), owner_view(d_wdown)])
        return [(dx_parts, 1.0), (dz, alpha)], dcw, dcb, tok

    dz, dg11, db11 = _add_ln_bwd([(dy, 1.0)], x3, ffn1, lng(1, 1), alpha, name="ln11_bwd")
    dx3, dcw1, dcb1, tok = ffn_bwd(dz, x3, hf1, f1, 1)
    dz, dg10, db10 = _add_ln_bwd(dx3, x2, att, lng(1, 0) + tok, alpha, name="ln10_bwd")
    d_wo = _matmul(ao, dz, name="att_out_dw", ta=True, tm=1024, tn=1024, tk=1024, out_dtype=BF16)
    dao = _matmul(dz, w_o, name="att_out_dx", tb=True, tm=1024, tn=1024, tk=1024)
    dq, dkc, dkp, d_sinks = _attn_bwd(qkv, dao, lse, sinks, name="att_core_bwd")
    dqkv, d_bqkv = _dqkv_assemble(dq, dkc, dkp, name="att_dqkv")
    d_wqkv = _matmul(x2, dqkv, name="att_qkv_dw", ta=True, tm=1024, tn=dqkv.shape[1], tk=1024, out_dtype=BF16)
    d_wqkv_st = jnp.moveaxis(d_wqkv.reshape(d_wqkv.shape[0], 4, -1), 1, 0)
    tok = send_grads("att", [owner_view(d_wqkv_st), owner_view(d_wo)])
    dx2 = _matmul(dqkv, w_qkv, name="att_qkv_dx", tb=True, tm=1024, tn=1024, tk=dqkv.shape[1], addend=(dz, alpha))
    dz, dg01, db01 = _add_ln_bwd([(dx2, 1.0)], x1, ffn0, lng(0, 1) + tok, alpha, name="ln01_bwd")
    dx1, dcw0, dcb0, tok = ffn_bwd(dz, x1, hf0, f0, 0)
    dz, dg00, db00 = _add_ln_bwd(dx1, xs, mix, lng(0, 0) + tok, alpha, name="ln00_bwd")
    d_wout = _matmul(ab, dz, name="mix_out_dw", ta=True, tm=1024, tn=1024, tk=1024, out_dtype=BF16)
    dab = _matmul(dz, w_out, name="mix_out_dx", tb=True, tm=1024, tn=1024, tk=1024)
    dh0, d_cwa, d_cba, d_ga, d_ba, d_gb, d_bb, d_ws, d_sb = _mixer_bwd(
        h0, dab, conv_w_a, *mix_vecs, ws, wst, sbb, tril, name="mix_mid_bwd")
    d_win = _matmul(xs, dh0, name="mix_in_dw", ta=True, tm=1024, tn=512, tk=1024, out_stack=4, out_dtype=BF16)
    tok = send_grads("mix", [owner_view(d_win), owner_view(d_wout)])
    grad_x = _matmul(dh0, w_in, name="mix_in_dx", tb=True, tm=1024, tn=1024, tk=512, addend=(dz, alpha),
                     bias=jnp.broadcast_to(tok, (1, d)))

    place = jnp.stack([q_idx, c_idx, 4 * lax.axis_index("x") + 2 * lax.axis_index("y") + c_idx]).astype(jnp.int32)
    where = {"mix": [(0, None), (1, None)], "att": [(2, None), (3, None)], "ffn0": [(4, 0), (5, 0)], "ffn1": [(4, 1), (5, 1)]}
    shard_bufs = [None] * 6
    layout = []
    for tag, g_thru, l_thru, ss, rs in in_flight:
        own, landed = _reduce_wait(g_thru, l_thru, ss, rs, grad_x, name=f"reduce_wait_{tag}")
        for k, (o, lead) in enumerate(where[tag]):
            piece = own[k].shape[2:]
            shape = (2,) + piece if lead is None else (2, 2) + piece
            shard_bufs[o] = _octo_sum(own[k], landed[k], place, shard_bufs[o], (lead, shape), name=f"reduce_sum_{tag}{k}")
            layout.append((o, lead))
    shared = _sibling_share(shard_bufs, layout, name="reduce_share")
    g_win = shared[0].reshape(ab_w_in.shape)
    g_wout = shared[1].reshape(ab_w_out.shape)
    g_wqkv = shared[2].reshape(c_w_qkv.shape)
    g_wo = shared[3].reshape(c_w_o.shape)
    g_wup = shared[4].reshape(ffn_w_up.shape)
    g_wdown = shared[5].reshape(ffn_w_down.shape)

    small_w = [a_conv_w, a_conv_b, a_norm_g, a_norm_b, b_norm_g, b_norm_b, b_spatial_w, b_spatial_b, c_b_qkv,
               c_sinks, ffn_conv_w, ffn_conv_b, ln_g, ln_b]
    small_m = [m_a_conv_w, m_a_conv_b, m_a_norm_g, m_a_norm_b, m_b_norm_g, m_b_norm_b, m_b_spatial_w, m_b_spatial_b,
               m_c_b_qkv, m_c_sinks, m_ffn_conv_w, m_ffn_conv_b, m_ln_g, m_ln_b]
    small_v = [v_a_conv_w, v_a_conv_b, v_a_norm_g, v_a_norm_b, v_b_norm_g, v_b_norm_b, v_b_spatial_w, v_b_spatial_b,
               v_c_b_qkv, v_c_sinks, v_ffn_conv_w, v_ffn_conv_b, v_ln_g, v_ln_b]
    local = [d_cwa, d_cba, d_ga, d_ba, d_gb, d_bb, d_ws, d_sb, d_bqkv, d_sinks, dcw0, dcb0, dcw1, dcb1,
             dg00, dg01, dg10, dg11, db00, db01, db10, db11, sq_err]
    small_out, loss = _small_tail(local, list(zip(small_w, small_m, small_v)), name="small_tail")
    loss = loss[0, 0]
    small_g = [o[0] for o in small_out]
    sm_delta = [o[1] for o in small_out]
    sm_m = [o[2] for o in small_out]
    sm_v = [o[3] for o in small_out]

    def adamw_big(w, g, m, v, name):
        two_d = lambda a: a.reshape(-1, a.shape[-1])
        outs = _adamw(two_d(w), two_d(g), two_d(m), two_d(v), name=name)
        return [o.reshape(w.shape) for o in outs]

    big_w = [ab_w_in, ab_w_out, c_w_qkv, c_w_o, ffn_w_up, ffn_w_down]
    big_g = [g_win, g_wout, g_wqkv, g_wo, g_wup, g_wdown]
    big_m = [m_ab_w_in, m_ab_w_out, m_c_w_qkv, m_c_w_o, m_ffn_w_up, m_ffn_w_down]
    big_v = [v_ab_w_in, v_ab_w_out, v_c_w_qkv, v_c_w_o, v_ffn_w_up, v_ffn_w_down]
    big_out = [adamw_big(w, g, m, v, f"adamw_big{t}") for t, (w, g, m, v) in enumerate(zip(big_w, big_g, big_m, big_v))]

    order_big = {0: 0, 9: 1, 10: 2, 13: 3, 14: 4, 17: 5}
    order_small = {1: 0, 2: 1, 3: 2, 4: 3, 5: 4, 6: 5, 7: 6, 8: 7, 11: 8, 12: 9, 15: 10, 16: 11, 18: 12, 19: 13}
    grads, deltas, new_m, new_v = [], [], [], []
    for pos_w in range(20):
        if pos_w in order_big:
            t = order_big[pos_w]
            grads.append(big_g[t])
            deltas.append(big_out[t][0])
            new_m.append(big_out[t][1])
            new_v.append(big_out[t][2])
        else:
            t = order_small[pos_w]
            grads.append(small_g[t])
            deltas.append(sm_delta[t])
            new_m.append(sm_m[t])
            new_v.append(sm_v[t])
    return (loss, grad_x[None], *grads, *deltas, *new_m, *new_v)
```

```python
import math

import jax
import jax.numpy as jnp
from jax import lax
from jax.experimental import pallas as pl
from jax.experimental.pallas import tpu as pltpu

F32 = jnp.float32
BF16 = jnp.bfloat16
MESH = pl.DeviceIdType.MESH

LN_EPS = 1e-5
HEAD_DIM = 64
ATT_BLOCK = 128
Q_PER_KV = 8
A_KERNEL = 31
CONV_HALO = 32
FFN_HALO = 8
B_CHUNK = 128
LANES = 128
MXU_WIDTH = 256
GELU_C = math.sqrt(2.0 / math.pi)
ADAM_LR = 0.001
ADAM_B1 = 0.9
ADAM_B2 = 0.999
ADAM_EPS = 1e-08
ADAM_WD = 0.01
ADAM_STEP = 10
VMEM_LIMIT = 56 * 1024 * 1024


def _cp(*dims):
    return pltpu.CompilerParams(dimension_semantics=dims, vmem_limit_bytes=VMEM_LIMIT)


def _pick(n, prefs):
    for p in prefs:
        if n % p == 0:
            return p
    return n


def _sig(x):
    return 1.0 / (1.0 + jnp.exp(-x))


def _gelu(x):
    t = jnp.tanh(GELU_C * (x + 0.044715 * (x * x * x)))
    return x * (0.5 * (1.0 + t)), t


def _gelu_grad(x, t):
    return 0.5 * (1.0 + t) + 0.5 * x * (1.0 - t * t) * (GELU_C * (1.0 + 3.0 * 0.044715 * x * x))


def _ln_stats(z):
    mu = jnp.mean(z, axis=-1, keepdims=True)
    zc = z - mu
    var = jnp.mean(zc * zc, axis=-1, keepdims=True)
    rstd = lax.rsqrt(var + LN_EPS)
    return zc * rstd, rstd


def _ln_bwd(dxh, xh, rstd):
    return rstd * (dxh - jnp.mean(dxh, axis=-1, keepdims=True) - xh * jnp.mean(dxh * xh, axis=-1, keepdims=True))


def _rowsum(a):
    return jnp.sum(a, axis=0, keepdims=True)


def _lshape(a):
    return (a.shape[0], a.shape[1]) if a.ndim == 2 else (a.shape[1], a.shape[0] * a.shape[2])


def _spec2(arr, blk_r, blk_c, ridx, cidx):
    if len(arr.shape) == 2:
        return pl.BlockSpec((blk_r, blk_c), lambda i, j, k: (ridx(i, j, k), cidx(i, j, k)))
    per = arr.shape[2] // blk_c
    assert arr.shape[2] % blk_c == 0
    return pl.BlockSpec((None, blk_r, blk_c), lambda i, j, k: (cidx(i, j, k) // per, ridx(i, j, k), cidx(i, j, k) % per))


def _matmul(a, b, *, name, ta=False, tb=False, tm, tn, tk, out_dtype=F32, out_stack=None, bias=None, addend=None):
    ar, ac = _lshape(a)
    br, bc = _lshape(b)
    m, kdim = (ac, ar) if ta else (ar, ac)
    n = br if tb else bc
    assert (bc if tb else br) == kdim
    tm, tn, tk = min(tm, m), min(tn, n), min(tk, kdim)
    assert m % tm == 0 and n % tn == 0 and kdim % tk == 0, (name, m, n, kdim, tm, tn, tk)
    nk = kdim // tk
    gi, gj, gk = (lambda i, j, k: i), (lambda i, j, k: j), (lambda i, j, k: k)
    a_spec = _spec2(a, tk, tm, gk, gi) if ta else _spec2(a, tm, tk, gi, gk)
    b_spec = _spec2(b, tn, tk, gj, gk) if tb else _spec2(b, tk, tn, gk, gj)
    if out_stack is None:
        out_sds = jax.ShapeDtypeStruct((m, n), out_dtype)
    else:
        out_sds = jax.ShapeDtypeStruct((out_stack, m, n // out_stack), out_dtype)
    o_spec = _spec2(out_sds, tm, tn, gi, gj)
    in_specs = [a_spec, b_spec]
    args = [a, b]
    if bias is not None:
        in_specs.append(pl.BlockSpec((1, tn), lambda i, j, k: (0, j)))
        args.append(bias)
    scale = None
    if addend is not None:
        add_arr, scale = addend
        in_specs.append(pl.BlockSpec((tm, tn), lambda i, j, k: (i, j)))
        args.append(add_arr)
    use_acc = nk > 1 and out_dtype != F32
    dn = (((0 if ta else 1,), (1 if tb else 0,)), ((), ()))

    def body(*refs):
        a_ref, b_ref = refs[0], refs[1]
        pos = 2
        bias_ref = add_ref = None
        if bias is not None:
            bias_ref = refs[pos]
            pos += 1
        if addend is not None:
            add_ref = refs[pos]
            pos += 1
        o_ref = refs[pos]
        acc_ref = refs[pos + 1] if use_acc else o_ref
        p = lax.dot_general(a_ref[...].astype(BF16), b_ref[...].astype(BF16), dn, preferred_element_type=F32)

        def finish(val):
            if bias_ref is not None:
                val = val + bias_ref[...]
            if add_ref is not None:
                val = val + scale * add_ref[...]
            return val.astype(out_dtype)

        if nk == 1:
            o_ref[...] = finish(p)
        else:
            k = pl.program_id(2)

            @pl.when(k == 0)
            def _():
                acc_ref[...] = p

            @pl.when(k > 0)
            def _():
                acc_ref[...] += p

            if use_acc or bias_ref is not None or add_ref is not None:
                @pl.when(k == nk - 1)
                def _():
                    o_ref[...] = finish(acc_ref[...])

    return pl.pallas_call(
        body, name=name, grid=(m // tm, n // tn, nk), in_specs=in_specs, out_specs=o_spec, out_shape=out_sds,
        scratch_shapes=[pltpu.VMEM((tm, tn), F32)] if use_acc else [],
        compiler_params=_cp("parallel", "parallel", "arbitrary"),
    )(*args)


def _add_ln_fwd(x, s, g, b, alpha, *, name):
    rows, d = x.shape
    t = _pick(rows, (512, 256))

    def body(x_ref, s_ref, g_ref, b_ref, y_ref):
        xh, _ = _ln_stats(alpha * x_ref[...] + s_ref[...])
        y_ref[...] = xh * g_ref[...] + b_ref[...]

    row = pl.BlockSpec((t, d), lambda i: (i, 0))
    vec = pl.BlockSpec((1, d), lambda i: (0, 0))
    return pl.pallas_call(body, name=name, grid=(rows // t,), in_specs=[row, row, vec, vec], out_specs=row,
                          out_shape=jax.ShapeDtypeStruct((rows, d), F32), compiler_params=_cp("parallel"))(x, s, g, b)


def _add_ln_bwd(dy_terms, x, s, g, alpha, *, name):
    rows, d = x.shape
    t = _pick(rows, (512, 256))
    nterm = len(dy_terms)
    scales = [sc for _, sc in dy_terms]
    ranks = [a.ndim for a, _ in dy_terms]

    def body(*refs):
        dy_refs = refs[:nterm]
        x_ref, s_ref, g_ref, dz_ref, dg_ref, db_ref = refs[nterm:]

        @pl.when(pl.program_id(0) == 0)
        def _():
            dg_ref[...] = jnp.zeros_like(dg_ref)
            db_ref[...] = jnp.zeros_like(db_ref)

        dyv = None
        for r, sc, rank in zip(dy_refs, scales, ranks):
            slabs = [r[...]] if rank == 2 else [r[p] for p in range(r.shape[0])]
            for v in slabs:
                v = v if sc == 1.0 else sc * v
                dyv = v if dyv is None else dyv + v
        xh, rstd = _ln_stats(alpha * x_ref[...] + s_ref[...])
        dz_ref[...] = _ln_bwd(dyv * g_ref[...], xh, rstd)
        dg_ref[...] += _rowsum(dyv * xh)
        db_ref[...] += _rowsum(dyv)

    row = pl.BlockSpec((t, d), lambda i: (i, 0))
    vec = pl.BlockSpec((1, d), lambda i: (0, 0))
    vsds = jax.ShapeDtypeStruct((1, d), F32)
    dy_specs = [row if a.ndim == 2 else pl.BlockSpec((a.shape[0], t, d), lambda i: (0, i, 0)) for a, _ in dy_terms]
    return pl.pallas_call(body, name=name, grid=(rows // t,), in_specs=dy_specs + [row, row, vec], out_specs=[row, vec, vec],
                          out_shape=[jax.ShapeDtypeStruct((rows, d), F32), vsds, vsds],
                          compiler_params=_cp("arbitrary"))(*[a for a, _ in dy_terms], x, s, g)


def _loss_and_grad(y, tgt, *, name):
    rows, d = y.shape
    t = _pick(rows, (512, 256))

    def body(y_ref, t_ref, l_ref, dy_ref):
        @pl.when(pl.program_id(0) == 0)
        def _():
            l_ref[...] = jnp.zeros_like(l_ref)

        e = y_ref[...] - t_ref[...]
        l_ref[...] += _rowsum(e * e)
        dy_ref[...] = e * (1.0 / d)

    row = pl.BlockSpec((t, d), lambda i: (i, 0))
    vec = pl.BlockSpec((1, d), lambda i: (0, 0))
    return pl.pallas_call(body, name=name, grid=(rows // t,), in_specs=[row, row], out_specs=[vec, row],
                          out_shape=[jax.ShapeDtypeStruct((1, d), F32), jax.ShapeDtypeStruct((rows, d), F32)],
                          compiler_params=_cp("arbitrary"))(y, tgt)


def _col_blocks(width):
    out, pos = [], 0
    while pos < width:
        w = MXU_WIDTH if width - pos >= MXU_WIDTH else width - pos
        out.append(slice(pos, pos + w))
        pos += w
    return out


def _conv3(e, w, b):
    r1 = pltpu.roll(e, 1, 0)
    r2 = pltpu.roll(e, 2, 0)
    return w[0:1, :] * r2 + w[1:2, :] * r1 + w[2:3, :] * e + b, (r2, r1, e)


def _ffn_up_fwd(x, w_up, cw, cb, *, name):
    rows, d = x.shape
    nq, _, tc = w_up.shape
    nj = nq // 2
    f = tc * nj
    tm = _pick(rows, (512, 256))
    blocks = _col_blocks(tc)
    once = pl.Buffered(1)

    def body(x_ref, wg_ref, wv_ref, cw_ref, cb_ref, hf_ref, f_ref, prev_ref):
        @pl.when(pl.program_id(1) == 0)
        def _():
            prev_ref[...] = jnp.zeros_like(prev_ref)

        xb = x_ref[...].astype(BF16)
        for cs in blocks:
            hc = []
            for s, w_ref in ((0, wg_ref), (1, wv_ref)):
                h = jnp.dot(xb, w_ref[:, cs], preferred_element_type=F32)
                hf_ref[s, :, cs] = h
                e = jnp.concatenate([prev_ref[s, :, cs], h], axis=0)
                prev_ref[s, :, cs] = h[tm - FFN_HALO:]
                y, _ = _conv3(e, cw_ref[s, :, cs], cb_ref[s, :, cs])
                hc.append(y[FFN_HALO:])
            gl, _ = _gelu(hc[0])
            f_ref[:, cs] = (gl * hc[1]).astype(BF16)

    in_specs = [
        pl.BlockSpec((tm, d), lambda j, i: (i, 0)),
        pl.BlockSpec((None, d, tc), lambda j, i: (j, 0, 0), pipeline_mode=once),
        pl.BlockSpec((None, d, tc), lambda j, i: (nj + j, 0, 0), pipeline_mode=once),
        pl.BlockSpec((2, 3, tc), lambda j, i: (0, 0, j)),
        pl.BlockSpec((2, 1, tc), lambda j, i: (0, 0, j)),
    ]
    out_specs = [pl.BlockSpec((2, tm, tc), lambda j, i: (0, i, j)), pl.BlockSpec((tm, tc), lambda j, i: (i, j))]
    out_shape = [jax.ShapeDtypeStruct((2, rows, f), F32), jax.ShapeDtypeStruct((rows, f), BF16)]
    return pl.pallas_call(body, name=name, grid=(nj, rows // tm), in_specs=in_specs, out_specs=out_specs, out_shape=out_shape,
                          scratch_shapes=[pltpu.VMEM((2, FFN_HALO, tc), F32)],
                          compiler_params=_cp("parallel", "arbitrary"))(x, w_up, w_up, cw, cb)


def _ffn_up_bwd(hf, df, x, w_up, cw, cb, *, name):
    _, rows, f = hf.shape
    d = x.shape[1]
    nq, _, tc = w_up.shape
    nj = nq // 2
    tm = _pick(rows, (512, 256))
    hb = tm // FFN_HALO
    once = pl.Buffered(1)
    ni = rows // tm
    last_blk = rows // FFN_HALO - 1
    ext = tm + 2 * FFN_HALO
    tile = slice(FFN_HALO, FFN_HALO + tm)
    blocks = _col_blocks(tc)

    def body(h_ref, hp_ref, hn_ref, d_ref, dn_ref, x_ref, wg_ref, wv_ref, cw_ref, cb_ref, dx_ref, dw_out_ref, dcw_ref, dcb_ref,
             dw_ref):
        i = pl.program_id(1)
        first = i == 0
        last = i == ni - 1

        @pl.when(first)
        def _():
            dw_ref[...] = jnp.zeros_like(dw_ref)
            dcw_ref[...] = jnp.zeros_like(dcw_ref)
            dcb_ref[...] = jnp.zeros_like(dcb_ref)

        xt = x_ref[...].astype(BF16).T
        dx = None
        for cs in blocks:
            wc = cs.stop - cs.start
            de = jnp.concatenate([jnp.zeros((FFN_HALO, wc), F32), d_ref[:, cs], jnp.where(last, 0.0, dn_ref[:, cs])], axis=0)
            taps, hc = [], []
            for s in range(2):
                e = jnp.concatenate([jnp.where(first, 0.0, hp_ref[s, :, cs]), h_ref[s, :, cs], hn_ref[s, :, cs]], axis=0)
                y, tp = _conv3(e, cw_ref[s, :, cs], cb_ref[s, :, cs])
                hc.append(y)
                taps.append(tp)
            gl, th = _gelu(hc[0])
            dhc = (de * hc[1] * _gelu_grad(hc[0], th), de * gl)
            for s, w_ref in ((0, wg_ref), (1, wv_ref)):
                w = cw_ref[s, :, cs]
                g = dhc[s]
                dh = (w[2:3, :] * g + w[1:2, :] * pltpu.roll(g, ext - 1, 0) + w[0:1, :] * pltpu.roll(g, ext - 2, 0))[tile]
                gt = g[tile]
                for k in range(3):
                    dcw_ref[s, k:k + 1, cs] += _rowsum(gt * taps[s][k][tile])
                dcb_ref[s, :, cs] += _rowsum(gt)
                dhb = dh.astype(BF16)
                part = lax.dot_general(dhb, w_ref[:, cs], (((1,), (1,)), ((), ())), preferred_element_type=F32)
                dx = part if dx is None else dx + part
                dw_ref[s, :, cs] += jnp.dot(xt, dhb, preferred_element_type=F32)
        dx_ref[...] = dx

        @pl.when(last)
        def _():
            dw_out_ref[...] = dw_ref[...].astype(BF16)

    in_specs = [
        pl.BlockSpec((2, tm, tc), lambda j, i: (0, i, j)),
        pl.BlockSpec((2, FFN_HALO, tc), lambda j, i: (0, jnp.maximum(i * hb - 1, 0), j)),
        pl.BlockSpec((2, FFN_HALO, tc), lambda j, i: (0, jnp.minimum((i + 1) * hb, last_blk), j)),
        pl.BlockSpec((tm, tc), lambda j, i: (i, j)),
        pl.BlockSpec((FFN_HALO, tc), lambda j, i: (jnp.minimum((i + 1) * hb, last_blk), j)),
        pl.BlockSpec((tm, d), lambda j, i: (i, 0)),
        pl.BlockSpec((None, d, tc), lambda j, i: (j, 0, 0), pipeline_mode=once),
        pl.BlockSpec((None, d, tc), lambda j, i: (nj + j, 0, 0), pipeline_mode=once),
        pl.BlockSpec((2, 3, tc), lambda j, i: (0, 0, j)),
        pl.BlockSpec((2, 1, tc), lambda j, i: (0, 0, j)),
    ]
    out_specs = [
        pl.BlockSpec((None, tm, d), lambda j, i: (j, i, 0)),
        pl.BlockSpec((2, None, d, tc), lambda j, i: (0, j, 0, 0), pipeline_mode=once),
        pl.BlockSpec((2, 3, tc), lambda j, i: (0, 0, j)),
        pl.BlockSpec((2, 1, tc), lambda j, i: (0, 0, j)),
    ]
    out_shape = [jax.ShapeDtypeStruct((nj, rows, d), F32), jax.ShapeDtypeStruct((2, nj, d, tc), BF16),
                 jax.ShapeDtypeStruct((2, 3, f), F32), jax.ShapeDtypeStruct((2, 1, f), F32)]
    dx, dw, dcw, dcb = pl.pallas_call(body, name=name, grid=(nj, ni), in_specs=in_specs, out_specs=out_specs,
                                      out_shape=out_shape, scratch_shapes=[pltpu.VMEM((2, d, tc), F32)],
                                      compiler_params=_cp("parallel", "arbitrary"))(
        hf, hf, hf, df, df, x, w_up, w_up, cw, cb)
    return dx, dw.reshape(nq, d, tc), dcw, dcb


def _mixer_fwd(h0, cw, cb, ga, ba, gb, bb, ws, sbb, *, name):
    _, rows, w = h0.shape
    t = _pick(rows, (256,))
    hb = t // CONV_HALO
    groups = w // B_CHUNK

    def body(h_ref, hp_ref, cw_ref, cb_ref, ga_ref, ba_ref, gb_ref, bb_ref, ws_ref, sb_ref, o_ref):
        first = pl.program_id(0) == 0
        a1 = h_ref[0] * _sig(h_ref[1])
        a1p = jnp.where(first, 0.0, hp_ref[0] * _sig(hp_ref[1]))
        e = jnp.concatenate([a1p, a1], axis=0)
        acc = cw_ref[A_KERNEL - 1:A_KERNEL, :] * e
        for k in range(A_KERNEL - 1):
            acc = acc + cw_ref[k:k + 1, :] * pltpu.roll(e, A_KERNEL - 1 - k, 0)
        xh, _ = _ln_stats(acc[CONV_HALO:] + cb_ref[...])
        a3 = xh * ga_ref[...] + ba_ref[...]
        o_ref[:, 0:w] = (a3 * _sig(a3)).astype(BF16)

        u, _ = _gelu(h_ref[2])
        v1, _ = _gelu(h_ref[3])
        xh2, _ = _ln_stats(v1)
        v2 = (xh2 * gb_ref[...] + bb_ref[...]).astype(BF16)
        for c in range(t // B_CHUNK):
            rs = slice(c * B_CHUNK, (c + 1) * B_CHUNK)
            for g in range(groups):
                cs = slice(g * B_CHUNK, (g + 1) * B_CHUNK)
                mixed = jnp.dot(ws_ref[g], v2[rs, cs], preferred_element_type=F32) + sb_ref[g]
                o_ref[rs, w + g * B_CHUNK:w + (g + 1) * B_CHUNK] = (u[rs, cs] * mixed).astype(BF16)

    vec = pl.BlockSpec((1, w), lambda i: (0, 0))
    grp = pl.BlockSpec((groups, B_CHUNK, B_CHUNK), lambda i: (0, 0, 0))
    in_specs = [
        pl.BlockSpec((4, t, w), lambda i: (0, i, 0)),
        pl.BlockSpec((2, CONV_HALO, w), lambda i: (0, jnp.maximum(i * hb - 1, 0), 0)),
        pl.BlockSpec((A_KERNEL, w), lambda i: (0, 0)),
        vec, vec, vec, vec, vec, grp, grp,
    ]
    return pl.pallas_call(body, name=name, grid=(rows // t,), in_specs=in_specs,
                          out_specs=pl.BlockSpec((t, 2 * w), lambda i: (i, 0)),
                          out_shape=jax.ShapeDtypeStruct((rows, 2 * w), BF16),
                          compiler_params=_cp("parallel"))(h0, h0, cw, cb, ga, ba, gb, bb, ws, sbb)


def _mixer_bwd(h0, dab, cw, cb, ga, ba, gb, bb, ws, wst, sbb, tril, *, name):
    _, rows, w = h0.shape
    t = _pick(rows, (256,))
    hb = t // CONV_HALO
    ni = rows // t
    last_blk = rows // CONV_HALO - 1
    ext = t + 2 * CONV_HALO
    tile = slice(CONV_HALO, CONV_HALO + t)
    groups = w // B_CHUNK
    taps = A_KERNEL - 1

    def body(h_ref, hp_ref, hn_ref, d_ref, dn_ref, cw_ref, cb_ref, ga_ref, ba_ref, gb_ref, bb_ref, ws_ref, wst_ref,
             sb_ref, tril_ref, dh_ref, dcw_ref, dcb_ref, dga_ref, dba_ref, dgb_ref, dbb_ref, dws_ref, dsb_ref):
        i = pl.program_id(0)
        first = i == 0
        last = i == ni - 1

        @pl.when(first)
        def _():
            for r in (dcw_ref, dcb_ref, dga_ref, dba_ref, dgb_ref, dbb_ref, dws_ref, dsb_ref):
                r[...] = jnp.zeros_like(r)

        av_e = jnp.concatenate([hp_ref[0], h_ref[0], hn_ref[0]], axis=0)
        sg_e = _sig(jnp.concatenate([hp_ref[1], h_ref[1], hn_ref[1]], axis=0))
        rows_e = lax.broadcasted_iota(jnp.int32, (ext, 1), 0)
        a1_e = jnp.where(first & (rows_e < CONV_HALO), 0.0, av_e * sg_e)
        acc = cw_ref[taps:taps + 1, :] * a1_e
        for k in range(taps):
            acc = acc + cw_ref[k:k + 1, :] * pltpu.roll(a1_e, taps - k, 0)
        xh, rstd = _ln_stats(acc + cb_ref[...])
        a3 = xh * ga_ref[...] + ba_ref[...]
        s3 = _sig(a3)
        da_e = jnp.concatenate([jnp.zeros((CONV_HALO, w), F32), d_ref[:, 0:w], jnp.where(last, 0.0, dn_ref[...])], axis=0)
        da3 = da_e * (s3 * (1.0 + a3 * (1.0 - s3)))
        da2 = _ln_bwd(da3 * ga_ref[...], xh, rstd)
        dga_ref[...] += _rowsum(da3[tile] * xh[tile])
        dba_ref[...] += _rowsum(da3[tile])
        da2t = da2[tile]
        dcb_ref[...] += _rowsum(da2t)
        dcw_ref[taps:taps + 1, :] += _rowsum(da2t * a1_e[tile])
        da1 = cw_ref[taps:taps + 1, :] * da2
        for k in range(taps):
            sh = taps - k
            dcw_ref[k:k + 1, :] += _rowsum(da2t * pltpu.roll(a1_e, sh, 0)[tile])
            da1 = da1 + cw_ref[k:k + 1, :] * pltpu.roll(da2, ext - sh, 0)
        da1t = da1[tile]
        sgt = sg_e[tile]
        dh_ref[0] = (da1t * sgt).astype(BF16)
        dh_ref[1] = (da1t * h_ref[0] * sgt * (1.0 - sgt)).astype(BF16)

        bu = h_ref[2]
        bv = h_ref[3]
        u, tu = _gelu(bu)
        v1, tv = _gelu(bv)
        xh2, rstd2 = _ln_stats(v1)
        v2 = (xh2 * gb_ref[...] + bb_ref[...]).astype(BF16)
        db = d_ref[:, w:2 * w]
        dmx_all = db * u
        du_parts, dv2_parts = [], []
        for c in range(t // B_CHUNK):
            rs = slice(c * B_CHUNK, (c + 1) * B_CHUNK)
            du_row, dv2_row = [], []
            for g in range(groups):
                cs = slice(g * B_CHUNK, (g + 1) * B_CHUNK)
                v2cg = v2[rs, cs]
                mixed = jnp.dot(ws_ref[g], v2cg, preferred_element_type=F32) + sb_ref[g]
                dmx = dmx_all[rs, cs]
                dmxb = dmx.astype(BF16)
                du_row.append(db[rs, cs] * mixed)
                dv2_row.append(jnp.dot(wst_ref[g], dmxb, preferred_element_type=F32))
                dws_ref[g] += tril_ref[...] * lax.dot_general(dmxb, v2cg, (((1,), (1,)), ((), ())),
                                                               preferred_element_type=F32)
                dsb_ref[g:g + 1, :] += _rowsum(dmx.T)
            du_parts.append(jnp.concatenate(du_row, axis=1))
            dv2_parts.append(jnp.concatenate(dv2_row, axis=1))
        du = jnp.concatenate(du_parts, axis=0)
        dv2 = jnp.concatenate(dv2_parts, axis=0)
        dgb_ref[...] += _rowsum(dv2 * xh2)
        dbb_ref[...] += _rowsum(dv2)
        dv1 = _ln_bwd(dv2 * gb_ref[...], xh2, rstd2)
        dh_ref[2] = (du * _gelu_grad(bu, tu)).astype(BF16)
        dh_ref[3] = (dv1 * _gelu_grad(bv, tv)).astype(BF16)

    vec = pl.BlockSpec((1, w), lambda i: (0, 0))
    grp = pl.BlockSpec((groups, B_CHUNK, B_CHUNK), lambda i: (0, 0, 0))
    in_specs = [
        pl.BlockSpec((4, t, w), lambda i: (0, i, 0)),
        pl.BlockSpec((2, CONV_HALO, w), lambda i: (0, jnp.maximum(i * hb - 1, 0), 0)),
        pl.BlockSpec((2, CONV_HALO, w), lambda i: (0, jnp.minimum((i + 1) * hb, last_blk), 0)),
        pl.BlockSpec((t, 2 * w), lambda i: (i, 0)),
        pl.BlockSpec((CONV_HALO, w), lambda i: (jnp.minimum((i + 1) * hb, last_blk), 0)),
        pl.BlockSpec((A_KERNEL, w), lambda i: (0, 0)),
        vec, vec, vec, vec, vec, grp, grp, grp,
        pl.BlockSpec((B_CHUNK, B_CHUNK), lambda i: (0, 0)),
    ]
    vsds = jax.ShapeDtypeStruct((1, w), F32)
    out_specs = [
        pl.BlockSpec((4, t, w), lambda i: (0, i, 0)),
        pl.BlockSpec((A_KERNEL, w), lambda i: (0, 0)),
        vec, vec, vec, vec, vec, grp,
        pl.BlockSpec((groups, B_CHUNK), lambda i: (0, 0)),
    ]
    out_shape = [jax.ShapeDtypeStruct((4, rows, w), BF16), jax.ShapeDtypeStruct((A_KERNEL, w), F32),
                 vsds, vsds, vsds, vsds, vsds, jax.ShapeDtypeStruct((groups, B_CHUNK, B_CHUNK), F32),
                 jax.ShapeDtypeStruct((groups, B_CHUNK), F32)]
    return pl.pallas_call(body, name=name, grid=(ni,), in_specs=in_specs, out_specs=out_specs, out_shape=out_shape,
                          compiler_params=_cp("arbitrary"))(h0, h0, h0, dab, dab, cw, cb, ga, ba, gb, bb, ws, wst, sbb, tril)


GROUP_ROWS = Q_PER_KV * ATT_BLOCK


def _attn_mask(n):
    qi = lax.broadcasted_iota(jnp.int32, (GROUP_ROWS, 2 * ATT_BLOCK), 0) & (ATT_BLOCK - 1)
    sj = lax.broadcasted_iota(jnp.int32, (GROUP_ROWS, 2 * ATT_BLOCK), 1)
    diff = qi + ATT_BLOCK - sj
    return (diff >= 0) & (diff < ATT_BLOCK) & ((n > 0) | (sj >= ATT_BLOCK))


def _stack_heads(ref, kvh, dtype):
    heads = [ref[:, (kvh * Q_PER_KV + g) * HEAD_DIM:(kvh * Q_PER_KV + g + 1) * HEAD_DIM] for g in range(Q_PER_KV)]
    return jnp.concatenate(heads, axis=0).astype(dtype)


def _per_row_sink(sink_ref, kvh):
    head = lax.broadcasted_iota(jnp.int32, (GROUP_ROWS, 1), 0) // ATT_BLOCK
    out = jnp.zeros((GROUP_ROWS, 1), F32)
    for g in range(Q_PER_KV):
        out = jnp.where(head == g, sink_ref[kvh * Q_PER_KV + g], out)
    return out


def _attn_specs(rows, n_q):
    dq = n_q * HEAD_DIM
    dkv = 2 * (n_q // Q_PER_KV) * HEAD_DIM
    kv_blk = dq // dkv
    assert dq % dkv == 0
    return dq, dkv, [
        pl.BlockSpec(memory_space=pltpu.SMEM),
        pl.BlockSpec((ATT_BLOCK, dq), lambda n: (n, 0)),
        pl.BlockSpec((ATT_BLOCK, dkv), lambda n: (n, kv_blk)),
        pl.BlockSpec((ATT_BLOCK, dkv), lambda n: (jnp.maximum(n - 1, 0), kv_blk)),
    ]


def _kv_pair(kvc_ref, kvp_ref, kvh, n_kv):
    ks = slice(kvh * HEAD_DIM, (kvh + 1) * HEAD_DIM)
    vs = slice((n_kv + kvh) * HEAD_DIM, (n_kv + kvh + 1) * HEAD_DIM)
    kk = jnp.concatenate([kvp_ref[:, ks], kvc_ref[:, ks]], axis=0).astype(BF16)
    vv = jnp.concatenate([kvp_ref[:, vs], kvc_ref[:, vs]], axis=0).astype(BF16)
    return kk, vv


def _attn_fwd(qkv, sinks, *, name):
    rows = qkv.shape[0]
    n_q = sinks.shape[0]
    n_kv = n_q // Q_PER_KV
    scale = 1.0 / math.sqrt(HEAD_DIM)
    dq, _, in_specs = _attn_specs(rows, n_q)

    def body(sink_ref, q_ref, kvc_ref, kvp_ref, o_ref, lse_ref):
        valid = _attn_mask(pl.program_id(0))
        for kvh in range(n_kv):
            kk, vv = _kv_pair(kvc_ref, kvp_ref, kvh, n_kv)
            qs = _stack_heads(q_ref, kvh, BF16)
            s = lax.dot_general(qs, kk, (((1,), (1,)), ((), ())), preferred_element_type=F32)
            s = jnp.where(valid, s * scale, -jnp.inf)
            sk = _per_row_sink(sink_ref, kvh)
            m = jnp.maximum(jnp.max(s, axis=1, keepdims=True), sk)
            p = jnp.exp(s - m)
            l = jnp.sum(p, axis=1, keepdims=True) + jnp.exp(sk - m)
            o = jnp.dot((p / l).astype(BF16), vv, preferred_element_type=F32)
            lse = m + jnp.log(l)
            for g in range(Q_PER_KV):
                h = kvh * Q_PER_KV + g
                rs = slice(g * ATT_BLOCK, (g + 1) * ATT_BLOCK)
                o_ref[:, h * HEAD_DIM:(h + 1) * HEAD_DIM] = o[rs]
                lse_ref[:, h:h + 1] = lse[rs]

    out_specs = [pl.BlockSpec((ATT_BLOCK, dq), lambda n: (n, 0)), pl.BlockSpec((ATT_BLOCK, n_q), lambda n: (n, 0))]
    out_shape = [jax.ShapeDtypeStruct((rows, dq), F32), jax.ShapeDtypeStruct((rows, n_q), F32)]
    return pl.pallas_call(body, name=name, grid=(rows // ATT_BLOCK,), in_specs=in_specs, out_specs=out_specs,
                          out_shape=out_shape, compiler_params=_cp("parallel"))(sinks, qkv, qkv, qkv)


def _attn_bwd(qkv, dout, lse, sinks, *, name):
    rows = qkv.shape[0]
    n_q = sinks.shape[0]
    n_kv = n_q // Q_PER_KV
    scale = 1.0 / math.sqrt(HEAD_DIM)
    dq_w, dkv_w, in_specs = _attn_specs(rows, n_q)
    blk_q = pl.BlockSpec((ATT_BLOCK, dq_w), lambda n: (n, 0))
    blk_kv = pl.BlockSpec((ATT_BLOCK, dkv_w), lambda n: (n, 0))
    in_specs = in_specs + [blk_q, pl.BlockSpec((ATT_BLOCK, n_q), lambda n: (n, 0))]

    def body(sink_ref, q_ref, kvc_ref, kvp_ref, do_ref, lse_ref, dq_ref, dkc_ref, dkp_ref, dsink_ref):
        n = pl.program_id(0)

        @pl.when(n == 0)
        def _():
            dsink_ref[...] = jnp.zeros_like(dsink_ref)

        valid = _attn_mask(n)
        head_ids = lax.broadcasted_iota(jnp.int32, (1, n_q), 1)
        dsink = jnp.zeros((1, n_q), F32)
        for kvh in range(n_kv):
            kk, vv = _kv_pair(kvc_ref, kvp_ref, kvh, n_kv)
            qs = _stack_heads(q_ref, kvh, BF16)
            dos = _stack_heads(do_ref, kvh, BF16)
            lse = jnp.concatenate([lse_ref[:, kvh * Q_PER_KV + g:kvh * Q_PER_KV + g + 1] for g in range(Q_PER_KV)], axis=0)
            s = lax.dot_general(qs, kk, (((1,), (1,)), ((), ())), preferred_element_type=F32)
            s = jnp.where(valid, s * scale, -jnp.inf)
            p = jnp.exp(s - lse)
            dp = lax.dot_general(dos, vv, (((1,), (1,)), ((), ())), preferred_element_type=F32)
            delta = jnp.sum(p * dp, axis=1, keepdims=True)
            ds = (p * (dp - delta) * scale).astype(BF16)
            sink_term = jnp.exp(_per_row_sink(sink_ref, kvh) - lse) * delta
            dqs = jnp.dot(ds, kk, preferred_element_type=F32)
            for g in range(Q_PER_KV):
                h = kvh * Q_PER_KV + g
                rs = slice(g * ATT_BLOCK, (g + 1) * ATT_BLOCK)
                dsink = dsink + jnp.where(head_ids == h, -jnp.sum(sink_term[rs]), 0.0)
                dq_ref[:, h * HEAD_DIM:(h + 1) * HEAD_DIM] = dqs[rs]
            dk = lax.dot_general(ds, qs, (((0,), (0,)), ((), ())), preferred_element_type=F32)
            dv = lax.dot_general(p.astype(BF16), dos, (((0,), (0,)), ((), ())), preferred_element_type=F32)
            ks = slice(kvh * HEAD_DIM, (kvh + 1) * HEAD_DIM)
            vs = slice((n_kv + kvh) * HEAD_DIM, (n_kv + kvh + 1) * HEAD_DIM)
            dkp_ref[:, ks] = dk[0:ATT_BLOCK]
            dkc_ref[:, ks] = dk[ATT_BLOCK:]
            dkp_ref[:, vs] = dv[0:ATT_BLOCK]
            dkc_ref[:, vs] = dv[ATT_BLOCK:]
        dsink_ref[...] += dsink

    out_specs = [blk_q, blk_kv, blk_kv, pl.BlockSpec((1, n_q), lambda n: (0, 0))]
    out_shape = [jax.ShapeDtypeStruct((rows, dq_w), F32), jax.ShapeDtypeStruct((rows, dkv_w), F32),
                 jax.ShapeDtypeStruct((rows, dkv_w), F32), jax.ShapeDtypeStruct((1, n_q), F32)]
    return pl.pallas_call(body, name=name, grid=(rows // ATT_BLOCK,), in_specs=in_specs, out_specs=out_specs,
                          out_shape=out_shape, compiler_params=_cp("arbitrary"))(sinks, qkv, qkv, qkv, dout, lse)


def _dqkv_assemble(dq, dkc, dkp, *, name):
    rows, dq_w = dq.shape
    dkv_w = dkc.shape[1]
    nb = rows // ATT_BLOCK

    def body(dq_ref, dkc_ref, dkp_ref, o_ref, db_ref):
        n = pl.program_id(0)

        @pl.when(n == 0)
        def _():
            db_ref[...] = jnp.zeros_like(db_ref)

        dqv = dq_ref[...]
        dkv = dkc_ref[...] + jnp.where(n == nb - 1, 0.0, dkp_ref[...])
        o_ref[:, 0:dq_w] = dqv.astype(BF16)
        o_ref[:, dq_w:dq_w + dkv_w] = dkv.astype(BF16)
        db_ref[:, 0:dq_w] += _rowsum(dqv)
        db_ref[:, dq_w:dq_w + dkv_w] += _rowsum(dkv)

    width = dq_w + dkv_w
    in_specs = [pl.BlockSpec((ATT_BLOCK, dq_w), lambda n: (n, 0)), pl.BlockSpec((ATT_BLOCK, dkv_w), lambda n: (n, 0)),
                pl.BlockSpec((ATT_BLOCK, dkv_w), lambda n: (jnp.minimum(n + 1, nb - 1), 0))]
    out_specs = [pl.BlockSpec((ATT_BLOCK, width), lambda n: (n, 0)), pl.BlockSpec((1, width), lambda n: (0, 0))]
    out_shape = [jax.ShapeDtypeStruct((rows, width), BF16), jax.ShapeDtypeStruct((1, width), F32)]
    return pl.pallas_call(body, name=name, grid=(nb,), in_specs=in_specs, out_specs=out_specs, out_shape=out_shape,
                          compiler_params=_cp("arbitrary"))(dq, dkc, dkp)


def _row_tile(r, c):
    budget = 2 * 1024 * 1024 // (4 * c)
    for cand in (1024, 512, 256, 128, 64, 32, 16):
        if cand <= budget and r % cand == 0:
            return cand
    return r


def _octo_sum(own, recv, place, dest, lead, *, name):
    _, _, r, c = own.shape
    t = _row_tile(r, c)
    lead_idx, buf_shape = lead

    def body(place_ref, own_ref, *rest):
        o_ref = rest[7] if dest is None else rest[8]
        acc = own_ref[...].astype(F32)
        for k in range(7):
            acc = acc + rest[k][...].astype(F32)
        o_ref[...] = acc

    def peer(mask):
        return pl.BlockSpec((None, t, c), lambda i, pr: (pr[2] ^ mask, i, 0))

    if lead_idx is None:
        o_spec = pl.BlockSpec((None, t, c), lambda i, pr: (pr[1], i, 0))
    else:
        o_spec = pl.BlockSpec((None, None, t, c), lambda i, pr: (lead_idx, pr[1], i, 0))
    in_specs = [pl.BlockSpec((None, None, t, c), lambda i, pr: (pr[0], pr[1], i, 0))] + [peer(m) for m in range(1, 8)]
    args = [place, own] + [recv] * 7
    aliases = {}
    if dest is not None:
        in_specs.append(HBM)
        args.append(dest)
        aliases = {9: 0}
    grid_spec = pltpu.PrefetchScalarGridSpec(num_scalar_prefetch=1, grid=(r // t,), in_specs=in_specs, out_specs=o_spec)
    return pl.pallas_call(body, name=name, grid_spec=grid_spec, out_shape=jax.ShapeDtypeStruct(buf_shape, F32),
                          input_output_aliases=aliases, compiler_params=_cp("parallel"))(*args)


def _adamw_math(w, g, m, v):
    nm = ADAM_B1 * m + (1.0 - ADAM_B1) * g
    nv = ADAM_B2 * v + (1.0 - ADAM_B2) * (g * g)
    m_hat = nm / (1.0 - ADAM_B1 ** ADAM_STEP)
    v_hat = nv / (1.0 - ADAM_B2 ** ADAM_STEP)
    return -ADAM_LR * (m_hat / (jnp.sqrt(v_hat) + ADAM_EPS) + ADAM_WD * w), nm, nv


def _adamw(w, g, m, v, *, name):
    r, c = w.shape
    t = _row_tile(r, c)

    def body(w_ref, g_ref, m_ref, v_ref, d_ref, nm_ref, nv_ref, go_ref):
        gv = g_ref[...]
        d_ref[...], nm_ref[...], nv_ref[...] = _adamw_math(w_ref[...], gv, m_ref[...], v_ref[...])
        go_ref[...] = gv

    blk = pl.BlockSpec((t, c), lambda i: (i, 0))
    sds = jax.ShapeDtypeStruct((r, c), F32)
    return pl.pallas_call(body, name=name, grid=(r // t,), in_specs=[blk] * 4, out_specs=[blk] * 4,
                          out_shape=[sds] * 4, compiler_params=_cp("parallel"))(w, g, m, v)


HBM = pl.BlockSpec(memory_space=pl.ANY)


def _place():
    x, y, c = lax.axis_index("x"), lax.axis_index("y"), lax.axis_index("c")
    chips = [(1 - x, y), (x, 1 - y), (1 - x, 1 - y)]
    return x, y, c, 2 * x + y, (x, y, 1 - c), chips


def _rcopy(src, dst, ssem, rsem, dev):
    return pltpu.make_async_remote_copy(src_ref=src, dst_ref=dst, send_sem=ssem, recv_sem=rsem, device_id=dev,
                                        device_id_type=MESH)


HBM_ONLY = pl.BlockSpec(memory_space=pltpu.HBM)
SEM = pl.BlockSpec(memory_space=pltpu.SEMAPHORE)


def _peers():
    x, y, c = lax.axis_index("x"), lax.axis_index("y"), lax.axis_index("c")
    out = []
    for mask in range(1, 8):
        px = 1 - x if mask & 4 else x
        py = 1 - y if mask & 2 else y
        pc = 1 - c if mask & 1 else c
        out.append(((px, py, pc), 2 * px + py, pc, 4 * px + 2 * py + pc))
    return 4 * x + 2 * y + c, out


def _reduce_start(grads, lands, *, name):
    nt = len(grads)

    def body(*refs):
        ssems, rsems = refs[2 * nt:3 * nt], refs[3 * nt:4 * nt]
        g_out, l_out, token = refs[4 * nt:5 * nt], refs[5 * nt:6 * nt], refs[6 * nt]
        me, peers = _peers()
        for t in range(nt):
            for k, (dev, chip, core, _) in enumerate(peers):
                _rcopy(g_out[t].at[chip, core], l_out[t].at[me], ssems[t].at[k], rsems[t].at[k], dev).start()
        token[...] = jnp.zeros_like(token)

    sems = [pltpu.SemaphoreType.DMA((7,))] * (2 * nt)
    out_shape = (sems + [pltpu.HBM(g.shape, g.dtype) for g in grads] + [pltpu.HBM(l.shape, l.dtype) for l in lands]
                 + [jax.ShapeDtypeStruct((8, LANES), F32)])
    res = pl.pallas_call(
        body, name=name, in_specs=[HBM_ONLY] * (2 * nt),
        out_specs=[SEM] * (2 * nt) + [HBM_ONLY] * (2 * nt) + [pl.BlockSpec(memory_space=pltpu.VMEM)], out_shape=out_shape,
        input_output_aliases={t: 2 * nt + t for t in range(2 * nt)},
        compiler_params=pltpu.CompilerParams(has_side_effects=DATAFLOW),
    )(*[pltpu.with_memory_space_constraint(a, pltpu.HBM) for a in list(grads) + list(lands)])
    return res[:nt], res[nt:2 * nt], res[2 * nt:3 * nt], res[3 * nt:4 * nt], res[4 * nt]


def _reduce_wait(grads, lands, ssems, rsems, after, *, name):
    nt = len(grads)

    def body(*refs):
        ssem_refs, rsem_refs = refs[2 * nt:3 * nt], refs[3 * nt:4 * nt]
        g_out, l_out = refs[4 * nt + 1:5 * nt + 1], refs[5 * nt + 1:6 * nt + 1]
        me, peers = _peers()
        for t in range(nt):
            for k, (dev, chip, core, _) in enumerate(peers):
                _rcopy(g_out[t].at[chip, core], l_out[t].at[me], ssem_refs[t].at[k], rsem_refs[t].at[k], dev).wait_send()
        for t in range(nt):
            for k, (dev, _, _, idx) in enumerate(peers):
                slot = l_out[t].at[idx]
                _rcopy(slot, slot, ssem_refs[t].at[k], rsem_refs[t].at[k], dev).wait_recv()

    res = pl.pallas_call(
        body, name=name, in_specs=[HBM_ONLY] * (2 * nt) + [SEM] * (2 * nt) + [HBM], out_specs=[HBM_ONLY] * (2 * nt),
        out_shape=[pltpu.HBM(a.shape, a.dtype) for a in list(grads) + list(lands)],
        input_output_aliases={t: t for t in range(2 * nt)},
        compiler_params=pltpu.CompilerParams(has_side_effects=DATAFLOW),
    )(*grads, *lands, *ssems, *rsems, after)
    return list(res[:nt]), list(res[nt:])
DATAFLOW = pltpu.SideEffectType.DATAFLOW_SIDE_EFFECTING


def _gather_start(bufs, *, name):
    nt = len(bufs)

    def body(*refs):
        ssems, rsems, outs = refs[nt:2 * nt], refs[2 * nt:3 * nt], refs[3 * nt:4 * nt]
        x, y, c, q, sib, chips = _place()
        for t in range(nt):
            for j, (px, py) in enumerate(chips):
                mine = outs[t].at[q]
                _rcopy(mine, mine, ssems[t].at[j], rsems[t].at[j], (px, py, c)).start()

    sems = [pltpu.SemaphoreType.DMA((3,))] * (2 * nt)
    out_shape = sems + [pltpu.HBM(b.shape, b.dtype) for b in bufs]
    res = pl.pallas_call(
        body, name=name, in_specs=[HBM_ONLY] * nt, out_specs=[SEM] * (2 * nt) + [HBM_ONLY] * nt, out_shape=out_shape,
        input_output_aliases={t: 2 * nt + t for t in range(nt)},
        compiler_params=pltpu.CompilerParams(has_side_effects=DATAFLOW),
    )(*[pltpu.with_memory_space_constraint(b, pltpu.HBM) for b in bufs])
    return res[:nt], res[nt:2 * nt], res[2 * nt:]


def _gather_wait(bufs, ssems, rsems, after, *, name):
    nt = len(bufs)

    def body(*refs):
        ssem_refs, rsem_refs = refs[nt:2 * nt], refs[2 * nt:3 * nt]
        outs = refs[3 * nt + 1:]
        x, y, c, q, sib, chips = _place()
        for t in range(nt):
            for j, (px, py) in enumerate(chips):
                mine = outs[t].at[q]
                _rcopy(mine, mine, ssem_refs[t].at[j], rsem_refs[t].at[j], (px, py, c)).wait_send()
        for t in range(nt):
            for j, (px, py) in enumerate(chips):
                theirs = outs[t].at[2 * px + py]
                _rcopy(theirs, theirs, ssem_refs[t].at[j], rsem_refs[t].at[j], (px, py, c)).wait_recv()

    res = pl.pallas_call(
        body, name=name, in_specs=[HBM_ONLY] * nt + [SEM] * (2 * nt) + [HBM], out_specs=[HBM_ONLY] * nt,
        out_shape=[pltpu.HBM(b.shape, b.dtype) for b in bufs], input_output_aliases={t: t for t in range(nt)},
        compiler_params=pltpu.CompilerParams(has_side_effects=DATAFLOW),
    )(*bufs, *ssems, *rsems, after)
    return list(res)


def _sibling_share(bufs, layout, *, name):
    no = len(bufs)
    nt = len(layout)

    def body(*refs):
        outs = refs[no:2 * no]
        ssem, rsem = refs[2 * no:]
        x, y, c, q, sib, chips = _place()

        def slot(t, half):
            o, lead = layout[t]
            return outs[o].at[half] if lead is None else outs[o].at[lead, half]

        sends = []
        for t in range(nt):
            cp = _rcopy(slot(t, c), slot(t, c), ssem.at[t], rsem.at[t], sib)
            cp.start()
            sends.append(cp)
        for t in range(nt):
            _rcopy(slot(t, 1 - c), slot(t, 1 - c), ssem.at[t], rsem.at[t], sib).wait_recv()
        for cp in sends:
            cp.wait_send()

    out_shape = [jax.ShapeDtypeStruct(b.shape, b.dtype) for b in bufs]
    return pl.pallas_call(
        body, name=name, in_specs=[HBM] * no, out_specs=[HBM] * no, out_shape=out_shape,
        input_output_aliases={o: o for o in range(no)},
        scratch_shapes=[pltpu.SemaphoreType.DMA((nt,)), pltpu.SemaphoreType.DMA((nt,))],
    )(*bufs)


def _small_tail(local, params, *, name):
    (cwa, cba, ga, ba, gb, bb, dws, dsb, dbq, dsk, cwf0, cbf0, cwf1, cbf1,
     g00, g01, g10, g11, b00, b01, b10, b11, err) = local
    n_local = len(local)
    kw, wa = cwa.shape
    ng = dws.shape[0]
    nqkv = dbq.shape[1]
    nsk = dsk.shape[1]
    f = cwf0.shape[2]
    dm = err.shape[1]
    row_vec = 8 * (-(-kw // 8))
    shapes = [(row_vec + 8, wa), (ng * B_CHUNK + 8, B_CHUNK), (8, nqkv), (2, 2, 8, f), (16, dm)]
    n_grp = len(shapes)
    flat_params = [a for triple in params for a in triple]
    n_par = len(params)

    def reduce_body(*refs):
        loc = refs[:n_local]
        tot = refs[n_local:n_local + n_grp]
        scr = refs[n_local + n_grp:]
        grp, from_sib, pair, gath = (scr[k * n_grp:(k + 1) * n_grp] for k in range(4))
        ssem1, rsem1, ssem2, rsem2 = scr[4 * n_grp:]
        x, y, core, q, sib, chips = _place()

        for gr in grp:
            gr[...] = jnp.zeros_like(gr)
        a, b, c, dd, e = grp
        a[0:kw, :] = loc[0][...]
        for k in range(5):
            a[row_vec + k:row_vec + k + 1, :] = loc[1 + k][...]
        for g in range(ng):
            b[g * B_CHUNK:(g + 1) * B_CHUNK, :] = loc[6][g]
        b[ng * B_CHUNK:ng * B_CHUNK + ng, :] = loc[7][...]
        c[0:1, :] = loc[8][...]
        c[1:2, 0:nsk] = loc[9][...]
        for l in range(2):
            for s in range(2):
                dd[l, s, 0:3, :] = loc[10 + 2 * l][s]
                dd[l, s, 3:4, :] = loc[11 + 2 * l][s]
        for k in range(9):
            e[k:k + 1, :] = loc[14 + k][...]

        sends = []
        for gi in range(n_grp):
            cp = _rcopy(grp[gi], from_sib[gi], ssem1.at[gi], rsem1.at[gi], sib)
            cp.start()
            sends.append(cp)
        for gi in range(n_grp):
            _rcopy(grp[gi], from_sib[gi], ssem1.at[gi], rsem1.at[gi], sib).wait_recv()
            both = grp[gi][...] + from_sib[gi][...]
            pair[gi][...] = both
            gath[gi][q] = both
            for j, (px, py) in enumerate(chips):
                cp = _rcopy(pair[gi], gath[gi].at[q], ssem2.at[gi, j], rsem2.at[gi, j], (px, py, core))
                cp.start()
                sends.append(cp)
        for gi in range(n_grp):
            for j, (px, py) in enumerate(chips):
                slot = gath[gi].at[2 * px + py]
                _rcopy(slot, slot, ssem2.at[gi, j], rsem2.at[gi, j], (px, py, core)).wait_recv()
            acc = gath[gi][0]
            for k in range(1, 4):
                acc = acc + gath[gi][k]
            tot[gi][...] = acc
        for cp in sends:
            cp.wait_send()

    vm = pl.BlockSpec(memory_space=pltpu.VMEM)
    scratch = ([pltpu.VMEM(s, F32) for s in shapes] * 3 + [pltpu.VMEM((4,) + s, F32) for s in shapes]
               + [pltpu.SemaphoreType.DMA((n_grp,)), pltpu.SemaphoreType.DMA((n_grp,)),
                  pltpu.SemaphoreType.DMA((n_grp, 3)), pltpu.SemaphoreType.DMA((n_grp, 3))])
    totals = pl.pallas_call(
        reduce_body, name=name + "_reduce", in_specs=[vm] * n_local, out_specs=[vm] * n_grp,
        out_shape=[jax.ShapeDtypeStruct(s, F32) for s in shapes], scratch_shapes=scratch,
        compiler_params=pltpu.CompilerParams(vmem_limit_bytes=VMEM_LIMIT),
    )(*local)

    def adamw_body(*refs):
        ta, tb, tc, td, te = refs[:n_grp]
        par = refs[n_grp:n_grp + 3 * n_par]
        outs = refs[n_grp + 3 * n_par:n_grp + 7 * n_par]
        loss_ref = refs[n_grp + 7 * n_par]
        q = 2 * lax.axis_index("x") + lax.axis_index("y")

        def mine(piece):
            out = piece(0)
            for k in range(1, 4):
                out = jnp.where(q == k, piece(k), out)
            return out

        def update(p, grad, index=None):
            at = (lambda r: r[...]) if index is None else (lambda r: r[index])
            w_ref, m_ref, v_ref = par[3 * p:3 * p + 3]
            g_ref, d_ref, nm_ref, nv_ref = outs[4 * p:4 * p + 4]
            delta, nm, nv = _adamw_math(at(w_ref), grad, at(m_ref), at(v_ref))
            for r, val in ((g_ref, grad), (d_ref, delta), (nm_ref, nm), (nv_ref, nv)):
                if index is None:
                    r[...] = val
                else:
                    r[index] = val

        wq = wa // 4
        update(0, mine(lambda k: ta[0:kw, k * wq:(k + 1) * wq]), (0,))
        for k in range(5):
            update(1 + k, ta[row_vec + k:row_vec + k + 1, :])
        for g in range(ng):
            update(6, tb[g * B_CHUNK:(g + 1) * B_CHUNK, :], (0, g))
        update(7, tb[ng * B_CHUNK:ng * B_CHUNK + ng, :], (0,))
        nq4 = nqkv // 4
        update(8, mine(lambda k: tc[0:1, k * nq4:(k + 1) * nq4]))
        update(9, tc[1:2, 0:nsk])
        fh = f // 2
        for l in range(2):
            update(10, mine(lambda k: td[l, k // 2, 0:3, (k % 2) * fh:(k % 2 + 1) * fh]), (l,))
            update(11, jnp.concatenate([td[l, 0, 3:4, :], td[l, 1, 3:4, :]], axis=1), (slice(l, l + 1),))
        dq4 = dm // 4
        for i in range(2):
            for j in range(2):
                for p, base in ((12, 0), (13, 4)):
                    row = base + 2 * i + j
                    update(p, mine(lambda k: te[row:row + 1, k * dq4:(k + 1) * dq4]), (i, slice(j, j + 1)))
        loss_ref[...] = (0.5 / dm) * jnp.sum(te[8:9, :], axis=1, keepdims=True)

    out_shape = []
    for w, _, _ in params:
        out_shape += [jax.ShapeDtypeStruct(w.shape, F32)] * 4
    out_shape.append(jax.ShapeDtypeStruct((1, 1), F32))
    res = pl.pallas_call(
        adamw_body, name=name + "_adamw", in_specs=[vm] * (n_grp + 3 * n_par), out_specs=[vm] * len(out_shape),
        out_shape=out_shape, compiler_params=pltpu.CompilerParams(vmem_limit_bytes=VMEM_LIMIT),
    )(*totals, *flat_params)
    return [res[4 * p:4 * p + 4] for p in range(n_par)], res[-1]


def _pack(arrays, rows_multiple):
    flat = jnp.concatenate([a.reshape(-1) for a in arrays])
    rows = -(-flat.shape[0] // LANES)
    rows = -(-rows // rows_multiple) * rows_multiple
    flat = jnp.pad(flat, (0, rows * LANES - flat.shape[0]))
    return flat.reshape(rows, LANES)


def _unshard_cols(stacked):
    moved = jnp.moveaxis(stacked, 0, -2)
    return moved.reshape(moved.shape[:-2] + (4 * stacked.shape[-1],))


def kernel(x, ab_w_in, a_conv_w, a_conv_b, a_norm_g, a_norm_b, b_norm_g, b_norm_b, b_spatial_w, b_spatial_b, ab_w_out, c_w_qkv, c_b_qkv, c_sinks, c_w_o, ffn_w_up, ffn_conv_w, ffn_conv_b, ffn_w_down, ln_g, ln_b, loss_target, m_ab_w_in, m_a_conv_w, m_a_conv_b, m_a_norm_g, m_a_norm_b, m_b_norm_g, m_b_norm_b, m_b_spatial_w, m_b_spatial_b, m_ab_w_out, m_c_w_qkv, m_c_b_qkv, m_c_sinks, m_c_w_o, m_ffn_w_up, m_ffn_conv_w, m_ffn_conv_b, m_ffn_w_down, m_ln_g, m_ln_b, v_ab_w_in, v_a_conv_w, v_a_conv_b, v_a_norm_g, v_a_norm_b, v_b_norm_g, v_b_norm_b, v_b_spatial_w, v_b_spatial_b, v_ab_w_out, v_c_w_qkv, v_c_b_qkv, v_c_sinks, v_c_w_o, v_ffn_w_up, v_ffn_conv_w, v_ffn_conv_b, v_ffn_w_down, v_ln_g, v_ln_b):
    rows, d = x.shape[1], x.shape[2]
    depth = ln_g.shape[0]
    assert depth == 2 and x.shape[0] == 1
    alpha = (2.0 * depth) ** 0.25
    f = ffn_w_down.shape[1] * 4
    n_q = c_sinks.shape[1]
    q_idx = 2 * lax.axis_index("x") + lax.axis_index("y")
    c_idx = lax.axis_index("c")
    xs, tgt = x[0], loss_target[0]

    def own_slot(part):
        buf = lax.empty((4,) + part.shape, part.dtype)
        return lax.dynamic_update_slice(buf, part[None], (q_idx, 0, 0, 0))

    def halves(wm):
        return own_slot(wm.astype(BF16).reshape((2, wm.shape[0] // 2) + wm.shape[1:]))

    small_sharded = [a_conv_w[0], c_b_qkv[0], ffn_conv_w, ln_g, ln_b]
    small_pack = _pack(small_sharded, 16)
    bufs = [halves(ab_w_in[0]), own_slot(small_pack.reshape(2, small_pack.shape[0] // 2, LANES)), halves(ab_w_out[0]),
            halves(ffn_w_up[0]), halves(ffn_w_down[0]), halves(c_w_qkv[0]), halves(c_w_o[0]),
            halves(ffn_w_up[1]), halves(ffn_w_down[1])]
    ssems, rsems, started = _gather_start(bufs, name="gather_start")

    def arrive(idx, after, tag):
        got = _gather_wait([started[i] for i in idx], [ssems[i] for i in idx], [rsems[i] for i in idx], after,
                           name=f"gather_wait_{tag}")
        return [g.reshape(4, 2 * g.shape[2], g.shape[3]) for g in got]

    w_in, small_all = arrive([0, 1], xs, "in")
    small_all = small_all.reshape(4, -1)
    sh_shapes = [s.shape for s in small_sharded]
    pieces, pos = [], 0
    for s in sh_shapes:
        n = math.prod(s)
        pieces.append(_unshard_cols(small_all[:, pos:pos + n].reshape((4,) + s)))
        pos += n
    conv_w_a, b_qkv, conv_w_f, ln_gf, ln_bf = pieces

    tril = jnp.tril(jnp.ones((B_CHUNK, B_CHUNK), F32))
    ws = (b_spatial_w[0] * tril).astype(BF16)
    wst = jnp.swapaxes(ws, 1, 2)
    sbb = jnp.broadcast_to(b_spatial_b[0][:, :, None], b_spatial_w[0].shape)
    mix_vecs = [a_conv_b, a_norm_g, a_norm_b, b_norm_g, b_norm_b]
    cw_f = [jnp.swapaxes(conv_w_f[l].reshape(3, 2, f), 0, 1) for l in range(depth)]
    cb_f = [ffn_conv_b[l].reshape(2, 1, f) for l in range(depth)]
    lng = lambda i, j: ln_gf[i, j].reshape(1, d)
    lnb = lambda i, j: ln_bf[i, j].reshape(1, d)
    sinks = c_sinks[0]

    w_up, w_down = [None, None], [None, None]

    def ffn_fwd(xin, l):
        w_up[l], = arrive([3 + 4 * l], xin, f"up{l}")
        hf, fact = _ffn_up_fwd(xin, w_up[l], cw_f[l], cb_f[l], name=f"ffn{l}_up")
        w_down[l] = arrive([4 + 4 * l], fact, f"down{l}")[0].reshape(-1, d)
        out = _matmul(fact, w_down[l], name=f"ffn{l}_down", tm=512, tn=1024, tk=2816)
        return hf, fact, out

    h0 = _matmul(xs, w_in, name="mix_in", tm=1024, tn=512, tk=1024, out_stack=4)
    ab = _mixer_fwd(h0, conv_w_a, *mix_vecs, ws, sbb, name="mix_mid")
    w_out = arrive([2], ab, "out")[0].reshape(-1, d)
    mix = _matmul(ab, w_out, name="mix_out", tm=1024, tn=1024, tk=1024)
    x1 = _add_ln_fwd(xs, mix, lng(0, 0), lnb(0, 0), alpha, name="ln00")
    hf0, f0, ffn0 = ffn_fwd(x1, 0)
    x2 = _add_ln_fwd(x1, ffn0, lng(0, 1), lnb(0, 1), alpha, name="ln01")
    w_qkv = _unshard_cols(arrive([5], x2, "qkv")[0])
    qkv = _matmul(x2, w_qkv, name="att_qkv", tm=1024, tn=w_qkv.shape[1], tk=1024, bias=b_qkv.reshape(1, -1))
    ao, lse = _attn_fwd(qkv, sinks, name="att_core")
    w_o = arrive([6], ao, "o")[0].reshape(-1, d)
    att = _matmul(ao, w_o, name="att_out", tm=1024, tn=1024, tk=1024)
    x3 = _add_ln_fwd(x2, att, lng(1, 0), lnb(1, 0), alpha, name="ln10")
    hf1, f1, ffn1 = ffn_fwd(x3, 1)
    x4 = _add_ln_fwd(x3, ffn1, lng(1, 1), lnb(1, 1), alpha, name="ln11")
    sq_err, dy = _loss_and_grad(x4, tgt, name="loss")

    def owner_view(g):
        if g.ndim == 3:
            return g.reshape(4, 2, g.shape[1] // 2, g.shape[2])
        return g.reshape(4, 2, g.shape[0] // 8, g.shape[1])

    in_flight = []

    def send_grads(tag, grads):
        lands = [lax.empty((8,) + g.shape[2:], BF16) for g in grads]
        ss, rs, g_thru, l_thru, token = _reduce_start(grads, lands, name=f"reduce_start_{tag}")
        in_flight.append((tag, g_thru, l_thru, ss, rs))
        return token[0:1, 0:1]

    def ffn_bwd(dz, xin, hf, fact, l):
        d_wdown = _matmul(fact, dz, name=f"ffn{l}_down_dw", ta=True, tm=1408, tn=1024, tk=2048, out_dtype=BF16)
        dfa = _matmul(dz, w_down[l], name=f"ffn{l}_down_dx", tb=True, tm=1024, tn=1408, tk=1024)
        dx_parts, d_wup, dcw, dcb = _ffn_up_bwd(hf, dfa, xin, w_up[l], cw_f[l], cb_f[l], name=f"ffn{l}_up_bwd")
        tok = send_grads(f"ffn{l}", [owner_view(d_wup), owner_view(d_wdown)])
        return [(dx_parts, 1.0), (dz, alpha)], dcw, dcb, tok

    dz, dg11, db11 = _add_ln_bwd([(dy, 1.0)], x3, ffn1, lng(1, 1), alpha, name="ln11_bwd")
    dx3, dcw1, dcb1, tok = ffn_bwd(dz, x3, hf1, f1, 1)
    dz, dg10, db10 = _add_ln_bwd(dx3, x2, att, lng(1, 0) + tok, alpha, name="ln10_bwd")
    d_wo = _matmul(ao, dz, name="att_out_dw", ta=True, tm=1024, tn=1024, tk=1024, out_dtype=BF16)
    dao = _matmul(dz, w_o, name="att_out_dx", tb=True, tm=1024, tn=1024, tk=1024)
    dq, dkc, dkp, d_sinks = _attn_bwd(qkv, dao, lse, sinks, name="att_core_bwd")
    dqkv, d_bqkv = _dqkv_assemble(dq, dkc, dkp, name="att_dqkv")
    d_wqkv = _matmul(x2, dqkv, name="att_qkv_dw", ta=True, tm=1024, tn=dqkv.shape[1], tk=1024, out_dtype=BF16)
    d_wqkv_st = jnp.moveaxis(d_wqkv.reshape(d_wqkv.shape[0], 4, -1), 1, 0)
    tok = send_grads("att", [owner_view(d_wqkv_st), owner_view(d_wo)])
    dx2 = _matmul(dqkv, w_qkv, name="att_qkv_dx", tb=True, tm=1024, tn=1024, tk=dqkv.shape[1], addend=(dz, alpha))
    dz, dg01, db01 = _add_ln_bwd([(dx2, 1.0)], x1, ffn0, lng(0, 1) + tok, alpha, name="ln01_bwd")
    dx1, dcw0, dcb0, tok = ffn_bwd(dz, x1, hf0, f0, 0)
    dz, dg00, db00 = _add_ln_bwd(dx1, xs, mix, lng(0, 0) + tok, alpha, name="ln00_bwd")
    d_wout = _matmul(ab, dz, name="mix_out_dw", ta=True, tm=1024, tn=1024, tk=1024, out_dtype=BF16)
    dab = _matmul(dz, w_out, name="mix_out_dx", tb=True, tm=1024, tn=1024, tk=1024)
    dh0, d_cwa, d_cba, d_ga, d_ba, d_gb, d_bb, d_ws, d_sb = _mixer_bwd(
        h0, dab, conv_w_a, *mix_vecs, ws, wst, sbb, tril, name="mix_mid_bwd")
    d_win = _matmul(xs, dh0, name="mix_in_dw", ta=True, tm=1024, tn=512, tk=1024, out_stack=4, out_dtype=BF16)
    tok = send_grads("mix", [owner_view(d_win), owner_view(d_wout)])
    grad_x = _matmul(dh0, w_in, name="mix_in_dx", tb=True, tm=1024, tn=1024, tk=512, addend=(dz, alpha),
                     bias=jnp.broadcast_to(tok, (1, d)))

    place = jnp.stack([q_idx, c_idx, 4 * lax.axis_index("x") + 2 * lax.axis_index("y") + c_idx]).astype(jnp.int32)
    where = {"mix": [(0, None), (1, None)], "att": [(2, None), (3, None)], "ffn0": [(4, 0), (5, 0)], "ffn1": [(4, 1), (5, 1)]}
    shard_bufs = [None] * 6
    layout = []
    for tag, g_thru, l_thru, ss, rs in in_flight:
        own, landed = _reduce_wait(g_thru, l_thru, ss, rs, grad_x, name=f"reduce_wait_{tag}")
        for k, (o, lead) in enumerate(where[tag]):
            piece = own[k].shape[2:]
            shape = (2,) + piece if lead is None else (2, 2) + piece
            shard_bufs[o] = _octo_sum(own[k], landed[k], place, shard_bufs[o], (lead, shape), name=f"reduce_sum_{tag}{k}")
            layout.append((o, lead))
    shared = _sibling_share(shard_bufs, layout, name="reduce_share")
    g_win = shared[0].reshape(ab_w_in.shape)
    g_wout = shared[1].reshape(ab_w_out.shape)
    g_wqkv = shared[2].reshape(c_w_qkv.shape)
    g_wo = shared[3].reshape(c_w_o.shape)
    g_wup = shared[4].reshape(ffn_w_up.shape)
    g_wdown = shared[5].reshape(ffn_w_down.shape)

    small_w = [a_conv_w, a_conv_b, a_norm_g, a_norm_b, b_norm_g, b_norm_b, b_spatial_w, b_spatial_b, c_b_qkv,
               c_sinks, ffn_conv_w, ffn_conv_b, ln_g, ln_b]
    small_m = [m_a_conv_w, m_a_conv_b, m_a_norm_g, m_a_norm_b, m_b_norm_g, m_b_norm_b, m_b_spatial_w, m_b_spatial_b,
               m_c_b_qkv, m_c_sinks, m_ffn_conv_w, m_ffn_conv_b, m_ln_g, m_ln_b]
    small_v = [v_a_conv_w, v_a_conv_b, v_a_norm_g, v_a_norm_b, v_b_norm_g, v_b_norm_b, v_b_spatial_w, v_b_spatial_b,
               v_c_b_qkv, v_c_sinks, v_ffn_conv_w, v_ffn_conv_b, v_ln_g, v_ln_b]
    local = [d_cwa, d_cba, d_ga, d_ba, d_gb, d_bb, d_ws, d_sb, d_bqkv, d_sinks, dcw0, dcb0, dcw1, dcb1,
             dg00, dg01, dg10, dg11, db00, db01, db10, db11, sq_err]
    small_out, loss = _small_tail(local, list(zip(small_w, small_m, small_v)), name="small_tail")
    loss = loss[0, 0]
    small_g = [o[0] for o in small_out]
    sm_delta = [o[1] for o in small_out]
    sm_m = [o[2] for o in small_out]
    sm_v = [o[3] for o in small_out]

    def adamw_big(w, g, m, v, name):
        two_d = lambda a: a.reshape(-1, a.shape[-1])
        outs = _adamw(two_d(w), two_d(g), two_d(m), two_d(v), name=name)
        return [o.reshape(w.shape) for o in outs]

    big_w = [ab_w_in, ab_w_out, c_w_qkv, c_w_o, ffn_w_up, ffn_w_down]
    big_g = [g_win, g_wout, g_wqkv, g_wo, g_wup, g_wdown]
    big_m = [m_ab_w_in, m_ab_w_out, m_c_w_qkv, m_c_w_o, m_ffn_w_up, m_ffn_w_down]
    big_v = [v_ab_w_in, v_ab_w_out, v_c_w_qkv, v_c_w_o, v_ffn_w_up, v_ffn_w_down]
    big_out = [adamw_big(w, g, m, v, f"adamw_big{t}") for t, (w, g, m, v) in enumerate(zip(big_w, big_g, big_m, big_v))]

    order_big = {0: 0, 9: 1, 10: 2, 13: 3, 14: 4, 17: 5}
    order_small = {1: 0, 2: 1, 3: 2, 4: 3, 5: 4, 6: 5, 7: 6, 8: 7, 11: 8, 12: 9, 15: 10, 16: 11, 18: 12, 19: 13}
    grads, deltas, new_m, new_v = [], [], [], []
    for pos_w in range(20):
        if pos_w in order_big:
            t = order_big[pos_w]
            grads.append(big_out[t][3])
            deltas.append(big_out[t][0])
            new_m.append(big_out[t][1])
            new_v.append(big_out[t][2])
        else:
            t = order_small[pos_w]
            grads.append(small_g[t])
            deltas.append(sm_delta[t])
            new_m.append(sm_m[t])
            new_v.append(sm_v[t])
    return (loss, grad_x[None], *grads, *deltas, *new_m, *new_v)
```

```python
import math

import jax
import jax.numpy as jnp
from jax import lax
from jax.experimental import pallas as pl
from jax.experimental.pallas import tpu as pltpu

F32 = jnp.float32
BF16 = jnp.bfloat16
MESH = pl.DeviceIdType.MESH

LN_EPS = 1e-5
HEAD_DIM = 64
ATT_BLOCK = 128
Q_PER_KV = 8
A_KERNEL = 31
CONV_HALO = 32
FFN_HALO = 8
B_CHUNK = 128
LANES = 128
MXU_WIDTH = 256
GELU_C = math.sqrt(2.0 / math.pi)
ADAM_LR = 0.001
ADAM_B1 = 0.9
ADAM_B2 = 0.999
ADAM_EPS = 1e-08
ADAM_WD = 0.01
ADAM_STEP = 10
VMEM_LIMIT = 56 * 1024 * 1024


def _cp(*dims):
    return pltpu.CompilerParams(dimension_semantics=dims, vmem_limit_bytes=VMEM_LIMIT)


def _pick(n, prefs):
    for p in prefs:
        if n % p == 0:
            return p
    return n


def _sig(x):
    return 1.0 / (1.0 + jnp.exp(-x))


def _gelu(x):
    t = jnp.tanh(GELU_C * (x + 0.044715 * (x * x * x)))
    return x * (0.5 * (1.0 + t)), t


def _gelu_grad(x, t):
    return 0.5 * (1.0 + t) + 0.5 * x * (1.0 - t * t) * (GELU_C * (1.0 + 3.0 * 0.044715 * x * x))


def _ln_stats(z):
    mu = jnp.mean(z, axis=-1, keepdims=True)
    zc = z - mu
    var = jnp.mean(zc * zc, axis=-1, keepdims=True)
    rstd = lax.rsqrt(var + LN_EPS)
    return zc * rstd, rstd


def _ln_bwd(dxh, xh, rstd):
    return rstd * (dxh - jnp.mean(dxh, axis=-1, keepdims=True) - xh * jnp.mean(dxh * xh, axis=-1, keepdims=True))


def _rowsum(a):
    return jnp.sum(a, axis=0, keepdims=True)


def _lshape(a):
    return (a.shape[0], a.shape[1]) if a.ndim == 2 else (a.shape[1], a.shape[0] * a.shape[2])


def _spec2(arr, blk_r, blk_c, ridx, cidx):
    if len(arr.shape) == 2:
        return pl.BlockSpec((blk_r, blk_c), lambda i, j, k: (ridx(i, j, k), cidx(i, j, k)))
    per = arr.shape[2] // blk_c
    assert arr.shape[2] % blk_c == 0
    return pl.BlockSpec((None, blk_r, blk_c), lambda i, j, k: (cidx(i, j, k) // per, ridx(i, j, k), cidx(i, j, k) % per))


def _matmul(a, b, *, name, ta=False, tb=False, tm, tn, tk, out_dtype=F32, out_stack=None, bias=None, addend=None):
    ar, ac = _lshape(a)
    br, bc = _lshape(b)
    m, kdim = (ac, ar) if ta else (ar, ac)
    n = br if tb else bc
    assert (bc if tb else br) == kdim
    tm, tn, tk = min(tm, m), min(tn, n), min(tk, kdim)
    assert m % tm == 0 and n % tn == 0 and kdim % tk == 0, (name, m, n, kdim, tm, tn, tk)
    nk = kdim // tk
    gi, gj, gk = (lambda i, j, k: i), (lambda i, j, k: j), (lambda i, j, k: k)
    a_spec = _spec2(a, tk, tm, gk, gi) if ta else _spec2(a, tm, tk, gi, gk)
    b_spec = _spec2(b, tn, tk, gj, gk) if tb else _spec2(b, tk, tn, gk, gj)
    if out_stack is None:
        out_sds = jax.ShapeDtypeStruct((m, n), out_dtype)
    else:
        out_sds = jax.ShapeDtypeStruct((out_stack, m, n // out_stack), out_dtype)
    o_spec = _spec2(out_sds, tm, tn, gi, gj)
    in_specs = [a_spec, b_spec]
    args = [a, b]
    if bias is not None:
        in_specs.append(pl.BlockSpec((1, tn), lambda i, j, k: (0, j)))
        args.append(bias)
    scale = None
    if addend is not None:
        add_arr, scale = addend
        in_specs.append(pl.BlockSpec((tm, tn), lambda i, j, k: (i, j)))
        args.append(add_arr)
    use_acc = nk > 1 and out_dtype != F32
    dn = (((0 if ta else 1,), (1 if tb else 0,)), ((), ()))

    def body(*refs):
        a_ref, b_ref = refs[0], refs[1]
        pos = 2
        bias_ref = add_ref = None
        if bias is not None:
            bias_ref = refs[pos]
            pos += 1
        if addend is not None:
            add_ref = refs[pos]
            pos += 1
        o_ref = refs[pos]
        acc_ref = refs[pos + 1] if use_acc else o_ref
        p = lax.dot_general(a_ref[...].astype(BF16), b_ref[...].astype(BF16), dn, preferred_element_type=F32)

        def finish(val):
            if bias_ref is not None:
                val = val + bias_ref[...]
            if add_ref is not None:
                val = val + scale * add_ref[...]
            return val.astype(out_dtype)

        if nk == 1:
            o_ref[...] = finish(p)
        else:
            k = pl.program_id(2)

            @pl.when(k == 0)
            def _():
                acc_ref[...] = p

            @pl.when(k > 0)
            def _():
                acc_ref[...] += p

            if use_acc or bias_ref is not None or add_ref is not None:
                @pl.when(k == nk - 1)
                def _():
                    o_ref[...] = finish(acc_ref[...])

    return pl.pallas_call(
        body, name=name, grid=(m // tm, n // tn, nk), in_specs=in_specs, out_specs=o_spec, out_shape=out_sds,
        scratch_shapes=[pltpu.VMEM((tm, tn), F32)] if use_acc else [],
        compiler_params=_cp("parallel", "parallel", "arbitrary"),
    )(*args)


def _add_ln_fwd(x, s, g, b, alpha, *, name):
    rows, d = x.shape
    t = _pick(rows, (512, 256))

    def body(x_ref, s_ref, g_ref, b_ref, y_ref):
        xh, _ = _ln_stats(alpha * x_ref[...] + s_ref[...])
        y_ref[...] = xh * g_ref[...] + b_ref[...]

    row = pl.BlockSpec((t, d), lambda i: (i, 0))
    vec = pl.BlockSpec((1, d), lambda i: (0, 0))
    return pl.pallas_call(body, name=name, grid=(rows // t,), in_specs=[row, row, vec, vec], out_specs=row,
                          out_shape=jax.ShapeDtypeStruct((rows, d), F32), compiler_params=_cp("parallel"))(x, s, g, b)


def _add_ln_bwd(dy_terms, x, s, g, alpha, *, name):
    rows, d = x.shape
    t = _pick(rows, (512, 256))
    nterm = len(dy_terms)
    scales = [sc for _, sc in dy_terms]
    ranks = [a.ndim for a, _ in dy_terms]

    def body(*refs):
        dy_refs = refs[:nterm]
        x_ref, s_ref, g_ref, dz_ref, dg_ref, db_ref = refs[nterm:]

        @pl.when(pl.program_id(0) == 0)
        def _():
            dg_ref[...] = jnp.zeros_like(dg_ref)
            db_ref[...] = jnp.zeros_like(db_ref)

        dyv = None
        for r, sc, rank in zip(dy_refs, scales, ranks):
            slabs = [r[...]] if rank == 2 else [r[p] for p in range(r.shape[0])]
            for v in slabs:
                v = v if sc == 1.0 else sc * v
                dyv = v if dyv is None else dyv + v
        xh, rstd = _ln_stats(alpha * x_ref[...] + s_ref[...])
        dz_ref[...] = _ln_bwd(dyv * g_ref[...], xh, rstd)
        dg_ref[...] += _rowsum(dyv * xh)
        db_ref[...] += _rowsum(dyv)

    row = pl.BlockSpec((t, d), lambda i: (i, 0))
    vec = pl.BlockSpec((1, d), lambda i: (0, 0))
    vsds = jax.ShapeDtypeStruct((1, d), F32)
    dy_specs = [row if a.ndim == 2 else pl.BlockSpec((a.shape[0], t, d), lambda i: (0, i, 0)) for a, _ in dy_terms]
    return pl.pallas_call(body, name=name, grid=(rows // t,), in_specs=dy_specs + [row, row, vec], out_specs=[row, vec, vec],
                          out_shape=[jax.ShapeDtypeStruct((rows, d), F32), vsds, vsds],
                          compiler_params=_cp("arbitrary"))(*[a for a, _ in dy_terms], x, s, g)


def _loss_and_grad(y, tgt, *, name):
    rows, d = y.shape
    t = _pick(rows, (512, 256))

    def body(y_ref, t_ref, l_ref, dy_ref):
        @pl.when(pl.program_id(0) == 0)
        def _():
            l_ref[...] = jnp.zeros_like(l_ref)

        e = y_ref[...] - t_ref[...]
        l_ref[...] += _rowsum(e * e)
        dy_ref[...] = e * (1.0 / d)

    row = pl.BlockSpec((t, d), lambda i: (i, 0))
    vec = pl.BlockSpec((1, d), lambda i: (0, 0))
    return pl.pallas_call(body, name=name, grid=(rows // t,), in_specs=[row, row], out_specs=[vec, row],
                          out_shape=[jax.ShapeDtypeStruct((1, d), F32), jax.ShapeDtypeStruct((rows, d), F32)],
                          compiler_params=_cp("arbitrary"))(y, tgt)


def _col_blocks(width):
    out, pos = [], 0
    while pos < width:
        w = MXU_WIDTH if width - pos >= MXU_WIDTH else width - pos
        out.append(slice(pos, pos + w))
        pos += w
    return out


def _conv3(e, w, b):
    r1 = pltpu.roll(e, 1, 0)
    r2 = pltpu.roll(e, 2, 0)
    return w[0:1, :] * r2 + w[1:2, :] * r1 + w[2:3, :] * e + b, (r2, r1, e)


def _ffn_up_fwd(x, w_up, cw, cb, *, name):
    rows, d = x.shape
    nq, _, tc = w_up.shape
    nj = nq // 2
    f = tc * nj
    tm = _pick(rows, (512, 256))
    blocks = _col_blocks(tc)

    def body(x_ref, wg_ref, wv_ref, cw_ref, cb_ref, hf_ref, f_ref, prev_ref):
        @pl.when(pl.program_id(1) == 0)
        def _():
            prev_ref[...] = jnp.zeros_like(prev_ref)

        xb = x_ref[...].astype(BF16)
        for cs in blocks:
            hc = []
            for s, w_ref in ((0, wg_ref), (1, wv_ref)):
                h = jnp.dot(xb, w_ref[:, cs], preferred_element_type=F32)
                hf_ref[s, :, cs] = h
                e = jnp.concatenate([prev_ref[s, :, cs], h], axis=0)
                prev_ref[s, :, cs] = h[tm - FFN_HALO:]
                y, _ = _conv3(e, cw_ref[s, :, cs], cb_ref[s, :, cs])
                hc.append(y[FFN_HALO:])
            gl, _ = _gelu(hc[0])
            f_ref[:, cs] = (gl * hc[1]).astype(BF16)

    in_specs = [
        pl.BlockSpec((tm, d), lambda j, i: (i, 0)),
        pl.BlockSpec((None, d, tc), lambda j, i: (j, 0, 0)),
        pl.BlockSpec((None, d, tc), lambda j, i: (nj + j, 0, 0)),
        pl.BlockSpec((2, 3, tc), lambda j, i: (0, 0, j)),
        pl.BlockSpec((2, 1, tc), lambda j, i: (0, 0, j)),
    ]
    out_specs = [pl.BlockSpec((2, tm, tc), lambda j, i: (0, i, j)), pl.BlockSpec((tm, tc), lambda j, i: (i, j))]
    out_shape = [jax.ShapeDtypeStruct((2, rows, f), F32), jax.ShapeDtypeStruct((rows, f), BF16)]
    return pl.pallas_call(body, name=name, grid=(nj, rows // tm), in_specs=in_specs, out_specs=out_specs, out_shape=out_shape,
                          scratch_shapes=[pltpu.VMEM((2, FFN_HALO, tc), F32)],
                          compiler_params=_cp("parallel", "arbitrary"))(x, w_up, w_up, cw, cb)


def _ffn_up_bwd(hf, df, x, w_up, cw, cb, *, name):
    _, rows, f = hf.shape
    d = x.shape[1]
    nq, _, tc = w_up.shape
    nj = nq // 2
    tm = _pick(rows, (512, 256))
    hb = tm // FFN_HALO
    once = pl.Buffered(1)
    ni = rows // tm
    last_blk = rows // FFN_HALO - 1
    ext = tm + 2 * FFN_HALO
    tile = slice(FFN_HALO, FFN_HALO + tm)
    blocks = _col_blocks(tc)

    def body(h_ref, hp_ref, hn_ref, d_ref, dn_ref, x_ref, wg_ref, wv_ref, cw_ref, cb_ref, dx_ref, dw_out_ref, dcw_ref, dcb_ref,
             dw_ref):
        i = pl.program_id(1)
        first = i == 0
        last = i == ni - 1

        @pl.when(first)
        def _():
            dw_ref[...] = jnp.zeros_like(dw_ref)
            dcw_ref[...] = jnp.zeros_like(dcw_ref)
            dcb_ref[...] = jnp.zeros_like(dcb_ref)

        xt = x_ref[...].astype(BF16).T
        dx = None
        for cs in blocks:
            wc = cs.stop - cs.start
            de = jnp.concatenate([jnp.zeros((FFN_HALO, wc), F32), d_ref[:, cs], jnp.where(last, 0.0, dn_ref[:, cs])], axis=0)
            taps, hc = [], []
            for s in range(2):
                e = jnp.concatenate([jnp.where(first, 0.0, hp_ref[s, :, cs]), h_ref[s, :, cs], hn_ref[s, :, cs]], axis=0)
                y, tp = _conv3(e, cw_ref[s, :, cs], cb_ref[s, :, cs])
                hc.append(y)
                taps.append(tp)
            gl, th = _gelu(hc[0])
            dhc = (de * hc[1] * _gelu_grad(hc[0], th), de * gl)
            for s, w_ref in ((0, wg_ref), (1, wv_ref)):
                w = cw_ref[s, :, cs]
                g = dhc[s]
                dh = (w[2:3, :] * g + w[1:2, :] * pltpu.roll(g, ext - 1, 0) + w[0:1, :] * pltpu.roll(g, ext - 2, 0))[tile]
                gt = g[tile]
                for k in range(3):
                    dcw_ref[s, k:k + 1, cs] += _rowsum(gt * taps[s][k][tile])
                dcb_ref[s, :, cs] += _rowsum(gt)
                dhb = dh.astype(BF16)
                part = lax.dot_general(dhb, w_ref[:, cs], (((1,), (1,)), ((), ())), preferred_element_type=F32)
                dx = part if dx is None else dx + part
                dw_ref[s, :, cs] += jnp.dot(xt, dhb, preferred_element_type=F32)
        dx_ref[...] = dx

        @pl.when(last)
        def _():
            dw_out_ref[...] = dw_ref[...].astype(BF16)

    in_specs = [
        pl.BlockSpec((2, tm, tc), lambda j, i: (0, i, j)),
        pl.BlockSpec((2, FFN_HALO, tc), lambda j, i: (0, jnp.maximum(i * hb - 1, 0), j)),
        pl.BlockSpec((2, FFN_HALO, tc), lambda j, i: (0, jnp.minimum((i + 1) * hb, last_blk), j)),
        pl.BlockSpec((tm, tc), lambda j, i: (i, j)),
        pl.BlockSpec((FFN_HALO, tc), lambda j, i: (jnp.minimum((i + 1) * hb, last_blk), j)),
        pl.BlockSpec((tm, d), lambda j, i: (i, 0)),
        pl.BlockSpec((None, d, tc), lambda j, i: (j, 0, 0), pipeline_mode=once),
        pl.BlockSpec((None, d, tc), lambda j, i: (nj + j, 0, 0), pipeline_mode=once),
        pl.BlockSpec((2, 3, tc), lambda j, i: (0, 0, j)),
        pl.BlockSpec((2, 1, tc), lambda j, i: (0, 0, j)),
    ]
    out_specs = [
        pl.BlockSpec((None, tm, d), lambda j, i: (j, i, 0)),
        pl.BlockSpec((2, None, d, tc), lambda j, i: (0, j, 0, 0), pipeline_mode=once),
        pl.BlockSpec((2, 3, tc), lambda j, i: (0, 0, j)),
        pl.BlockSpec((2, 1, tc), lambda j, i: (0, 0, j)),
    ]
    out_shape = [jax.ShapeDtypeStruct((nj, rows, d), F32), jax.ShapeDtypeStruct((2, nj, d, tc), BF16),
                 jax.ShapeDtypeStruct((2, 3, f), F32), jax.ShapeDtypeStruct((2, 1, f), F32)]
    dx, dw, dcw, dcb = pl.pallas_call(body, name=name, grid=(nj, ni), in_specs=in_specs, out_specs=out_specs,
                                      out_shape=out_shape, scratch_shapes=[pltpu.VMEM((2, d, tc), F32)],
                                      compiler_params=_cp("parallel", "arbitrary"))(
        hf, hf, hf, df, df, x, w_up, w_up, cw, cb)
    return dx, dw.reshape(nq, d, tc), dcw, dcb


def _mixer_fwd(h0, cw, cb, ga, ba, gb, bb, ws, sbb, *, name):
    _, rows, w = h0.shape
    t = _pick(rows, (256,))
    hb = t // CONV_HALO
    groups = w // B_CHUNK

    def body(h_ref, hp_ref, cw_ref, cb_ref, ga_ref, ba_ref, gb_ref, bb_ref, ws_ref, sb_ref, o_ref, a2_ref):
        first = pl.program_id(0) == 0
        a1 = h_ref[0] * _sig(h_ref[1])
        a1p = jnp.where(first, 0.0, hp_ref[0] * _sig(hp_ref[1]))
        e = jnp.concatenate([a1p, a1], axis=0)
        acc = cw_ref[A_KERNEL - 1:A_KERNEL, :] * e
        for k in range(A_KERNEL - 1):
            acc = acc + cw_ref[k:k + 1, :] * pltpu.roll(e, A_KERNEL - 1 - k, 0)
        a2 = acc[CONV_HALO:] + cb_ref[...]
        a2_ref[...] = a2
        xh, _ = _ln_stats(a2)
        a3 = xh * ga_ref[...] + ba_ref[...]
        o_ref[:, 0:w] = (a3 * _sig(a3)).astype(BF16)

        u, _ = _gelu(h_ref[2])
        v1, _ = _gelu(h_ref[3])
        xh2, _ = _ln_stats(v1)
        v2 = (xh2 * gb_ref[...] + bb_ref[...]).astype(BF16)
        for c in range(t // B_CHUNK):
            rs = slice(c * B_CHUNK, (c + 1) * B_CHUNK)
            for g in range(groups):
                cs = slice(g * B_CHUNK, (g + 1) * B_CHUNK)
                mixed = jnp.dot(ws_ref[g], v2[rs, cs], preferred_element_type=F32) + sb_ref[g]
                o_ref[rs, w + g * B_CHUNK:w + (g + 1) * B_CHUNK] = (u[rs, cs] * mixed).astype(BF16)

    vec = pl.BlockSpec((1, w), lambda i: (0, 0))
    grp = pl.BlockSpec((groups, B_CHUNK, B_CHUNK), lambda i: (0, 0, 0))
    in_specs = [
        pl.BlockSpec((4, t, w), lambda i: (0, i, 0)),
        pl.BlockSpec((2, CONV_HALO, w), lambda i: (0, jnp.maximum(i * hb - 1, 0), 0)),
        pl.BlockSpec((A_KERNEL, w), lambda i: (0, 0)),
        vec, vec, vec, vec, vec, grp, grp,
    ]
    out_specs = [pl.BlockSpec((t, 2 * w), lambda i: (i, 0)), pl.BlockSpec((t, w), lambda i: (i, 0))]
    out_shape = [jax.ShapeDtypeStruct((rows, 2 * w), BF16), jax.ShapeDtypeStruct((rows, w), F32)]
    return pl.pallas_call(body, name=name, grid=(rows // t,), in_specs=in_specs, out_specs=out_specs, out_shape=out_shape,
                          compiler_params=_cp("parallel"))(h0, h0, cw, cb, ga, ba, gb, bb, ws, sbb)


def _mixer_bwd(h0, a2, dab, x, w_in, res, res_scale, cw, ga, ba, gb, bb, ws, wst, sbb, tril, *, name):
    _, rows, w = h0.shape
    d = x.shape[1]
    once = pl.Buffered(1)
    t = _pick(rows, (256,))
    hb = t // CONV_HALO
    ni = rows // t
    last_blk = rows // CONV_HALO - 1
    ext = t + CONV_HALO
    tile = slice(0, t)
    groups = w // B_CHUNK
    taps = A_KERNEL - 1

    def body(h_ref, a2_ref, a2n_ref, d_ref, dn_ref, x_ref, win_ref, res_ref, cw_ref, ga_ref, ba_ref, gb_ref, bb_ref,
             ws_ref, wst_ref, sb_ref, tril_ref, dx_ref, dwin_ref, dcw_ref, dcb_ref, dga_ref, dba_ref, dgb_ref, dbb_ref,
             dws_ref, dsb_ref, dw_ref):
        i = pl.program_id(0)
        first = i == 0
        last = i == ni - 1

        @pl.when(first)
        def _():
            for r in (dw_ref, dcw_ref, dcb_ref, dga_ref, dba_ref, dgb_ref, dbb_ref, dws_ref, dsb_ref):
                r[...] = jnp.zeros_like(r)

        xt = x_ref[...].astype(BF16).T
        dx_terms = []

        def through_w_in(slot, dh):
            dhb = dh.astype(BF16)
            dx_terms.append(lax.dot_general(dhb, win_ref[slot], (((1,), (1,)), ((), ())), preferred_element_type=F32))
            dw_ref[slot] += jnp.dot(xt, dhb, preferred_element_type=F32)

        xh, rstd = _ln_stats(jnp.concatenate([a2_ref[...], a2n_ref[...]], axis=0))
        a3 = xh * ga_ref[...] + ba_ref[...]
        s3 = _sig(a3)
        da_e = jnp.concatenate([d_ref[:, 0:w], jnp.where(last, 0.0, dn_ref[...])], axis=0)
        da3 = da_e * (s3 * (1.0 + a3 * (1.0 - s3)))
        da2 = _ln_bwd(da3 * ga_ref[...], xh, rstd)
        dga_ref[...] += _rowsum(da3[tile] * xh[tile])
        dba_ref[...] += _rowsum(da3[tile])
        dcb_ref[...] += _rowsum(da2[tile])
        sgt = _sig(h_ref[1])
        a1t = h_ref[0] * sgt
        da1t = None
        for k in range(A_KERNEL):
            sh = taps - k
            fed = (da2 if sh == 0 else pltpu.roll(da2, ext - sh, 0))[tile]
            dcw_ref[k:k + 1, :] += _rowsum(a1t * fed)
            term = cw_ref[k:k + 1, :] * fed
            da1t = term if da1t is None else da1t + term
        through_w_in(0, da1t * sgt)
        through_w_in(1, da1t * h_ref[0] * sgt * (1.0 - sgt))

        bu = h_ref[2]
        bv = h_ref[3]
        u, tu = _gelu(bu)
        v1, tv = _gelu(bv)
        xh2, rstd2 = _ln_stats(v1)
        v2 = (xh2 * gb_ref[...] + bb_ref[...]).astype(BF16)
        db = d_ref[:, w:2 * w]
        dmx_all = db * u
        du_parts, dv2_parts = [], []
        for c in range(t // B_CHUNK):
            rs = slice(c * B_CHUNK, (c + 1) * B_CHUNK)
            du_row, dv2_row = [], []
            for g in range(groups):
                cs = slice(g * B_CHUNK, (g + 1) * B_CHUNK)
                v2cg = v2[rs, cs]
                mixed = jnp.dot(ws_ref[g], v2cg, preferred_element_type=F32) + sb_ref[g]
                dmx = dmx_all[rs, cs]
                dmxb = dmx.astype(BF16)
                du_row.append(db[rs, cs] * mixed)
                dv2_row.append(jnp.dot(wst_ref[g], dmxb, preferred_element_type=F32))
                dws_ref[g] += tril_ref[...] * lax.dot_general(dmxb, v2cg, (((1,), (1,)), ((), ())),
                                                               preferred_element_type=F32)
                dsb_ref[g:g + 1, :] += _rowsum(dmx.T)
            du_parts.append(jnp.concatenate(du_row, axis=1))
            dv2_parts.append(jnp.concatenate(dv2_row, axis=1))
        du = jnp.concatenate(du_parts, axis=0)
        dv2 = jnp.concatenate(dv2_parts, axis=0)
        dgb_ref[...] += _rowsum(dv2 * xh2)
        dbb_ref[...] += _rowsum(dv2)
        dv1 = _ln_bwd(dv2 * gb_ref[...], xh2, rstd2)
        through_w_in(2, du * _gelu_grad(bu, tu))
        through_w_in(3, dv1 * _gelu_grad(bv, tv))
        dx_ref[...] = res_scale * res_ref[...] + ((dx_terms[0] + dx_terms[1]) + (dx_terms[2] + dx_terms[3]))

        @pl.when(last)
        def _():
            dwin_ref[...] = dw_ref[...].astype(BF16)

    vec = pl.BlockSpec((1, w), lambda i: (0, 0))
    grp = pl.BlockSpec((groups, B_CHUNK, B_CHUNK), lambda i: (0, 0, 0))
    halo = pl.BlockSpec((CONV_HALO, w), lambda i: (jnp.minimum((i + 1) * hb, last_blk), 0))
    wide = pl.BlockSpec((t, d), lambda i: (i, 0))
    in_specs = [
        pl.BlockSpec((4, t, w), lambda i: (0, i, 0)),
        pl.BlockSpec((t, w), lambda i: (i, 0)),
        halo,
        pl.BlockSpec((t, 2 * w), lambda i: (i, 0)),
        halo,
        wide,
        pl.BlockSpec((4, d, w), lambda i: (0, 0, 0), pipeline_mode=once),
        wide,
        pl.BlockSpec((A_KERNEL, w), lambda i: (0, 0)),
        vec, vec, vec, vec, grp, grp, grp,
        pl.BlockSpec((B_CHUNK, B_CHUNK), lambda i: (0, 0)),
    ]
    vsds = jax.ShapeDtypeStruct((1, w), F32)
    out_specs = [
        wide,
        pl.BlockSpec((4, d, w), lambda i: (0, 0, 0), pipeline_mode=once),
        pl.BlockSpec((A_KERNEL, w), lambda i: (0, 0)),
        vec, vec, vec, vec, vec, grp,
        pl.BlockSpec((groups, B_CHUNK), lambda i: (0, 0)),
    ]
    out_shape = [jax.ShapeDtypeStruct((rows, d), F32), jax.ShapeDtypeStruct((4, d, w), BF16),
                 jax.ShapeDtypeStruct((A_KERNEL, w), F32),
                 vsds, vsds, vsds, vsds, vsds, jax.ShapeDtypeStruct((groups, B_CHUNK, B_CHUNK), F32),
                 jax.ShapeDtypeStruct((groups, B_CHUNK), F32)]
    return pl.pallas_call(body, name=name, grid=(ni,), in_specs=in_specs, out_specs=out_specs, out_shape=out_shape,
                          scratch_shapes=[pltpu.VMEM((4, d, w), F32)], compiler_params=_cp("arbitrary"))(
        h0, a2, a2, dab, dab, x, w_in, res, cw, ga, ba, gb, bb, ws, wst, sbb, tril)


GROUP_ROWS = Q_PER_KV * ATT_BLOCK


def _attn_mask(n):
    qi = lax.broadcasted_iota(jnp.int32, (GROUP_ROWS, 2 * ATT_BLOCK), 0) & (ATT_BLOCK - 1)
    sj = lax.broadcasted_iota(jnp.int32, (GROUP_ROWS, 2 * ATT_BLOCK), 1)
    diff = qi + ATT_BLOCK - sj
    return (diff >= 0) & (diff < ATT_BLOCK) & ((n > 0) | (sj >= ATT_BLOCK))


def _stack_heads(ref, kvh, dtype):
    heads = [ref[:, (kvh * Q_PER_KV + g) * HEAD_DIM:(kvh * Q_PER_KV + g + 1) * HEAD_DIM] for g in range(Q_PER_KV)]
    return jnp.concatenate(heads, axis=0).astype(dtype)


def _per_row_sink(sink_ref, kvh):
    head = lax.broadcasted_iota(jnp.int32, (GROUP_ROWS, 1), 0) // ATT_BLOCK
    out = jnp.zeros((GROUP_ROWS, 1), F32)
    for g in range(Q_PER_KV):
        out = jnp.where(head == g, sink_ref[kvh * Q_PER_KV + g], out)
    return out


def _attn_specs(rows, n_q):
    dq = n_q * HEAD_DIM
    dkv = 2 * (n_q // Q_PER_KV) * HEAD_DIM
    kv_blk = dq // dkv
    assert dq % dkv == 0
    return dq, dkv, [
        pl.BlockSpec(memory_space=pltpu.SMEM),
        pl.BlockSpec((ATT_BLOCK, dq), lambda n: (n, 0)),
        pl.BlockSpec((ATT_BLOCK, dkv), lambda n: (n, kv_blk)),
        pl.BlockSpec((ATT_BLOCK, dkv), lambda n: (jnp.maximum(n - 1, 0), kv_blk)),
    ]


def _kv_pair(kvc_ref, kvp_ref, kvh, n_kv):
    ks = slice(kvh * HEAD_DIM, (kvh + 1) * HEAD_DIM)
    vs = slice((n_kv + kvh) * HEAD_DIM, (n_kv + kvh + 1) * HEAD_DIM)
    kk = jnp.concatenate([kvp_ref[:, ks], kvc_ref[:, ks]], axis=0).astype(BF16)
    vv = jnp.concatenate([kvp_ref[:, vs], kvc_ref[:, vs]], axis=0).astype(BF16)
    return kk, vv


def _attn_fwd(qkv, sinks, *, name):
    rows = qkv.shape[0]
    n_q = sinks.shape[0]
    n_kv = n_q // Q_PER_KV
    scale = 1.0 / math.sqrt(HEAD_DIM)
    dq, _, in_specs = _attn_specs(rows, n_q)

    def body(sink_ref, q_ref, kvc_ref, kvp_ref, o_ref, lse_ref):
        valid = _attn_mask(pl.program_id(0))
        for kvh in range(n_kv):
            kk, vv = _kv_pair(kvc_ref, kvp_ref, kvh, n_kv)
            qs = _stack_heads(q_ref, kvh, BF16)
            s = lax.dot_general(qs, kk, (((1,), (1,)), ((), ())), preferred_element_type=F32)
            s = jnp.where(valid, s * scale, -jnp.inf)
            sk = _per_row_sink(sink_ref, kvh)
            m = jnp.maximum(jnp.max(s, axis=1, keepdims=True), sk)
            p = jnp.exp(s - m)
            l = jnp.sum(p, axis=1, keepdims=True) + jnp.exp(sk - m)
            o = jnp.dot((p / l).astype(BF16), vv, preferred_element_type=F32)
            lse = m + jnp.log(l)
            for g in range(Q_PER_KV):
                h = kvh * Q_PER_KV + g
                rs = slice(g * ATT_BLOCK, (g + 1) * ATT_BLOCK)
                o_ref[:, h * HEAD_DIM:(h + 1) * HEAD_DIM] = o[rs]
                lse_ref[:, h:h + 1] = lse[rs]

    out_specs = [pl.BlockSpec((ATT_BLOCK, dq), lambda n: (n, 0)), pl.BlockSpec((ATT_BLOCK, n_q), lambda n: (n, 0))]
    out_shape = [jax.ShapeDtypeStruct((rows, dq), F32), jax.ShapeDtypeStruct((rows, n_q), F32)]
    return pl.pallas_call(body, name=name, grid=(rows // ATT_BLOCK,), in_specs=in_specs, out_specs=out_specs,
                          out_shape=out_shape, compiler_params=_cp("parallel"))(sinks, qkv, qkv, qkv)


def _attn_bwd(qkv, dout, lse, sinks, *, name):
    rows = qkv.shape[0]
    n_q = sinks.shape[0]
    n_kv = n_q // Q_PER_KV
    scale = 1.0 / math.sqrt(HEAD_DIM)
    dq_w, dkv_w, in_specs = _attn_specs(rows, n_q)
    blk_q = pl.BlockSpec((ATT_BLOCK, dq_w), lambda n: (n, 0))
    blk_kv = pl.BlockSpec((ATT_BLOCK, dkv_w), lambda n: (n, 0))
    in_specs = in_specs + [blk_q, pl.BlockSpec((ATT_BLOCK, n_q), lambda n: (n, 0))]

    def body(sink_ref, q_ref, kvc_ref, kvp_ref, do_ref, lse_ref, dq_ref, dkc_ref, dkp_ref, dsink_ref):
        n = pl.program_id(0)

        @pl.when(n == 0)
        def _():
            dsink_ref[...] = jnp.zeros_like(dsink_ref)

        valid = _attn_mask(n)
        head_ids = lax.broadcasted_iota(jnp.int32, (1, n_q), 1)
        dsink = jnp.zeros((1, n_q), F32)
        for kvh in range(n_kv):
            kk, vv = _kv_pair(kvc_ref, kvp_ref, kvh, n_kv)
            qs = _stack_heads(q_ref, kvh, BF16)
            dos = _stack_heads(do_ref, kvh, BF16)
            lse = jnp.concatenate([lse_ref[:, kvh * Q_PER_KV + g:kvh * Q_PER_KV + g + 1] for g in range(Q_PER_KV)], axis=0)
            s = lax.dot_general(qs, kk, (((1,), (1,)), ((), ())), preferred_element_type=F32)
            s = jnp.where(valid, s * scale, -jnp.inf)
            p = jnp.exp(s - lse)
            dp = lax.dot_general(dos, vv, (((1,), (1,)), ((), ())), preferred_element_type=F32)
            delta = jnp.sum(p * dp, axis=1, keepdims=True)
            ds = (p * (dp - delta) * scale).astype(BF16)
            sink_term = jnp.exp(_per_row_sink(sink_ref, kvh) - lse) * delta
            dqs = jnp.dot(ds, kk, preferred_element_type=F32)
            for g in range(Q_PER_KV):
                h = kvh * Q_PER_KV + g
                rs = slice(g * ATT_BLOCK, (g + 1) * ATT_BLOCK)
                dsink = dsink + jnp.where(head_ids == h, -jnp.sum(sink_term[rs]), 0.0)
                dq_ref[:, h * HEAD_DIM:(h + 1) * HEAD_DIM] = dqs[rs]
            dk = lax.dot_general(ds, qs, (((0,), (0,)), ((), ())), preferred_element_type=F32)
            dv = lax.dot_general(p.astype(BF16), dos, (((0,), (0,)), ((), ())), preferred_element_type=F32)
            ks = slice(kvh * HEAD_DIM, (kvh + 1) * HEAD_DIM)
            vs = slice((n_kv + kvh) * HEAD_DIM, (n_kv + kvh + 1) * HEAD_DIM)
            dkp_ref[:, ks] = dk[0:ATT_BLOCK]
            dkc_ref[:, ks] = dk[ATT_BLOCK:]
            dkp_ref[:, vs] = dv[0:ATT_BLOCK]
            dkc_ref[:, vs] = dv[ATT_BLOCK:]
        dsink_ref[...] += dsink

    out_specs = [blk_q, blk_kv, blk_kv, pl.BlockSpec((1, n_q), lambda n: (0, 0))]
    out_shape = [jax.ShapeDtypeStruct((rows, dq_w), F32), jax.ShapeDtypeStruct((rows, dkv_w), F32),
                 jax.ShapeDtypeStruct((rows, dkv_w), F32), jax.ShapeDtypeStruct((1, n_q), F32)]
    return pl.pallas_call(body, name=name, grid=(rows // ATT_BLOCK,), in_specs=in_specs, out_specs=out_specs,
                          out_shape=out_shape, compiler_params=_cp("arbitrary"))(sinks, qkv, qkv, qkv, dout, lse)


def _dqkv_assemble(dq, dkc, dkp, *, name):
    rows, dq_w = dq.shape
    dkv_w = dkc.shape[1]
    nb = rows // ATT_BLOCK

    def body(dq_ref, dkc_ref, dkp_ref, o_ref, db_ref):
        n = pl.program_id(0)

        @pl.when(n == 0)
        def _():
            db_ref[...] = jnp.zeros_like(db_ref)

        dqv = dq_ref[...]
        dkv = dkc_ref[...] + jnp.where(n == nb - 1, 0.0, dkp_ref[...])
        o_ref[:, 0:dq_w] = dqv.astype(BF16)
        o_ref[:, dq_w:dq_w + dkv_w] = dkv.astype(BF16)
        db_ref[:, 0:dq_w] += _rowsum(dqv)
        db_ref[:, dq_w:dq_w + dkv_w] += _rowsum(dkv)

    width = dq_w + dkv_w
    in_specs = [pl.BlockSpec((ATT_BLOCK, dq_w), lambda n: (n, 0)), pl.BlockSpec((ATT_BLOCK, dkv_w), lambda n: (n, 0)),
                pl.BlockSpec((ATT_BLOCK, dkv_w), lambda n: (jnp.minimum(n + 1, nb - 1), 0))]
    out_specs = [pl.BlockSpec((ATT_BLOCK, width), lambda n: (n, 0)), pl.BlockSpec((1, width), lambda n: (0, 0))]
    out_shape = [jax.ShapeDtypeStruct((rows, width), BF16), jax.ShapeDtypeStruct((1, width), F32)]
    return pl.pallas_call(body, name=name, grid=(nb,), in_specs=in_specs, out_specs=out_specs, out_shape=out_shape,
                          compiler_params=_cp("arbitrary"))(dq, dkc, dkp)


def _row_tile(r, c):
    budget = 2 * 1024 * 1024 // (4 * c)
    for cand in (1024, 512, 256, 128, 64, 32, 16):
        if cand <= budget and r % cand == 0:
            return cand
    return r


def _octo_sum(own, recv, place, dest, lead, *, name):
    _, _, r, c = own.shape
    t = _row_tile(r, c)
    lead_idx, buf_shape = lead

    def body(place_ref, own_ref, *rest):
        o_ref = rest[7] if dest is None else rest[8]
        acc = own_ref[...].astype(F32)
        for k in range(7):
            acc = acc + rest[k][...].astype(F32)
        o_ref[...] = acc

    def peer(mask):
        return pl.BlockSpec((None, t, c), lambda i, pr: (pr[2] ^ mask, i, 0))

    if lead_idx is None:
        o_spec = pl.BlockSpec((None, t, c), lambda i, pr: (pr[1], i, 0))
    else:
        o_spec = pl.BlockSpec((None, None, t, c), lambda i, pr: (lead_idx, pr[1], i, 0))
    in_specs = [pl.BlockSpec((None, None, t, c), lambda i, pr: (pr[0], pr[1], i, 0))] + [peer(m) for m in range(1, 8)]
    args = [place, own] + [recv] * 7
    aliases = {}
    if dest is not None:
        in_specs.append(HBM)
        args.append(dest)
        aliases = {9: 0}
    grid_spec = pltpu.PrefetchScalarGridSpec(num_scalar_prefetch=1, grid=(r // t,), in_specs=in_specs, out_specs=o_spec)
    return pl.pallas_call(body, name=name, grid_spec=grid_spec, out_shape=jax.ShapeDtypeStruct(buf_shape, F32),
                          input_output_aliases=aliases, compiler_params=_cp("parallel"))(*args)


def _adamw_math(w, g, m, v):
    nm = ADAM_B1 * m + (1.0 - ADAM_B1) * g
    nv = ADAM_B2 * v + (1.0 - ADAM_B2) * (g * g)
    m_hat = nm / (1.0 - ADAM_B1 ** ADAM_STEP)
    v_hat = nv / (1.0 - ADAM_B2 ** ADAM_STEP)
    return -ADAM_LR * (m_hat / (jnp.sqrt(v_hat) + ADAM_EPS) + ADAM_WD * w), nm, nv


def _adamw(w, g, m, v, *, name):
    r, c = w.shape
    t = _row_tile(r, c)

    def body(w_ref, g_ref, m_ref, v_ref, d_ref, nm_ref, nv_ref, go_ref):
        gv = g_ref[...]
        d_ref[...], nm_ref[...], nv_ref[...] = _adamw_math(w_ref[...], gv, m_ref[...], v_ref[...])
        go_ref[...] = gv

    blk = pl.BlockSpec((t, c), lambda i: (i, 0))
    sds = jax.ShapeDtypeStruct((r, c), F32)
    return pl.pallas_call(body, name=name, grid=(r // t,), in_specs=[blk] * 4, out_specs=[blk] * 4,
                          out_shape=[sds] * 4, compiler_params=_cp("parallel"))(w, g, m, v)


HBM = pl.BlockSpec(memory_space=pl.ANY)


def _place():
    x, y, c = lax.axis_index("x"), lax.axis_index("y"), lax.axis_index("c")
    chips = [(1 - x, y), (x, 1 - y), (1 - x, 1 - y)]
    return x, y, c, 2 * x + y, (x, y, 1 - c), chips


def _rcopy(src, dst, ssem, rsem, dev):
    return pltpu.make_async_remote_copy(src_ref=src, dst_ref=dst, send_sem=ssem, recv_sem=rsem, device_id=dev,
                                        device_id_type=MESH)


HBM_ONLY = pl.BlockSpec(memory_space=pltpu.HBM)
SEM = pl.BlockSpec(memory_space=pltpu.SEMAPHORE)


def _peers():
    x, y, c = lax.axis_index("x"), lax.axis_index("y"), lax.axis_index("c")
    out = []
    for mask in range(1, 8):
        px = 1 - x if mask & 4 else x
        py = 1 - y if mask & 2 else y
        pc = 1 - c if mask & 1 else c
        out.append(((px, py, pc), 2 * px + py, pc, 4 * px + 2 * py + pc))
    return 4 * x + 2 * y + c, out


def _reduce_start(grads, lands, *, name):
    nt = len(grads)

    def body(*refs):
        ssems, rsems = refs[2 * nt:3 * nt], refs[3 * nt:4 * nt]
        g_out, l_out, token = refs[4 * nt:5 * nt], refs[5 * nt:6 * nt], refs[6 * nt]
        me, peers = _peers()
        for t in range(nt):
            for k, (dev, chip, core, _) in enumerate(peers):
                _rcopy(g_out[t].at[chip, core], l_out[t].at[me], ssems[t].at[k], rsems[t].at[k], dev).start()
        token[...] = jnp.zeros_like(token)

    sems = [pltpu.SemaphoreType.DMA((7,))] * (2 * nt)
    out_shape = (sems + [pltpu.HBM(g.shape, g.dtype) for g in grads] + [pltpu.HBM(l.shape, l.dtype) for l in lands]
                 + [jax.ShapeDtypeStruct((8, LANES), F32)])
    res = pl.pallas_call(
        body, name=name, in_specs=[HBM_ONLY] * (2 * nt),
        out_specs=[SEM] * (2 * nt) + [HBM_ONLY] * (2 * nt) + [pl.BlockSpec(memory_space=pltpu.VMEM)], out_shape=out_shape,
        input_output_aliases={t: 2 * nt + t for t in range(2 * nt)},
        compiler_params=pltpu.CompilerParams(has_side_effects=DATAFLOW),
    )(*[pltpu.with_memory_space_constraint(a, pltpu.HBM) for a in list(grads) + list(lands)])
    return res[:nt], res[nt:2 * nt], res[2 * nt:3 * nt], res[3 * nt:4 * nt], res[4 * nt]


def _reduce_wait(grads, lands, ssems, rsems, after, *, name):
    nt = len(grads)

    def body(*refs):
        ssem_refs, rsem_refs = refs[2 * nt:3 * nt], refs[3 * nt:4 * nt]
        g_out, l_out = refs[4 * nt + 1:5 * nt + 1], refs[5 * nt + 1:6 * nt + 1]
        me, peers = _peers()
        for t in range(nt):
            for k, (dev, chip, core, _) in enumerate(peers):
                _rcopy(g_out[t].at[chip, core], l_out[t].at[me], ssem_refs[t].at[k], rsem_refs[t].at[k], dev).wait_send()
        for t in range(nt):
            for k, (dev, _, _, idx) in enumerate(peers):
                slot = l_out[t].at[idx]
                _rcopy(slot, slot, ssem_refs[t].at[k], rsem_refs[t].at[k], dev).wait_recv()

    res = pl.pallas_call(
        body, name=name, in_specs=[HBM_ONLY] * (2 * nt) + [SEM] * (2 * nt) + [HBM], out_specs=[HBM_ONLY] * (2 * nt),
        out_shape=[pltpu.HBM(a.shape, a.dtype) for a in list(grads) + list(lands)],
        input_output_aliases={t: t for t in range(2 * nt)},
        compiler_params=pltpu.CompilerParams(has_side_effects=DATAFLOW),
    )(*grads, *lands, *ssems, *rsems, after)
    return list(res[:nt]), list(res[nt:])
DATAFLOW = pltpu.SideEffectType.DATAFLOW_SIDE_EFFECTING


def _gather_start(bufs, *, name):
    nt = len(bufs)

    def body(*refs):
        ssems, rsems, outs = refs[nt:2 * nt], refs[2 * nt:3 * nt], refs[3 * nt:4 * nt]
        x, y, c, q, sib, chips = _place()
        for t in range(nt):
            for j, (px, py) in enumerate(chips):
                mine = outs[t].at[q]
                _rcopy(mine, mine, ssems[t].at[j], rsems[t].at[j], (px, py, c)).start()

    sems = [pltpu.SemaphoreType.DMA((3,))] * (2 * nt)
    out_shape = sems + [pltpu.HBM(b.shape, b.dtype) for b in bufs]
    res = pl.pallas_call(
        body, name=name, in_specs=[HBM_ONLY] * nt, out_specs=[SEM] * (2 * nt) + [HBM_ONLY] * nt, out_shape=out_shape,
        input_output_aliases={t: 2 * nt + t for t in range(nt)},
        compiler_params=pltpu.CompilerParams(has_side_effects=DATAFLOW),
    )(*[pltpu.with_memory_space_constraint(b, pltpu.HBM) for b in bufs])
    return res[:nt], res[nt:2 * nt], res[2 * nt:]


def _gather_wait(bufs, ssems, rsems, after, *, name):
    nt = len(bufs)

    def body(*refs):
        ssem_refs, rsem_refs = refs[nt:2 * nt], refs[2 * nt:3 * nt]
        outs = refs[3 * nt + 1:]
        x, y, c, q, sib, chips = _place()
        for t in range(nt):
            for j, (px, py) in enumerate(chips):
                mine = outs[t].at[q]
                _rcopy(mine, mine, ssem_refs[t].at[j], rsem_refs[t].at[j], (px, py, c)).wait_send()
        for t in range(nt):
            for j, (px, py) in enumerate(chips):
                theirs = outs[t].at[2 * px + py]
                _rcopy(theirs, theirs, ssem_refs[t].at[j], rsem_refs[t].at[j], (px, py, c)).wait_recv()

    res = pl.pallas_call(
        body, name=name, in_specs=[HBM_ONLY] * nt + [SEM] * (2 * nt) + [HBM], out_specs=[HBM_ONLY] * nt,
        out_shape=[pltpu.HBM(b.shape, b.dtype) for b in bufs], input_output_aliases={t: t for t in range(nt)},
        compiler_params=pltpu.CompilerParams(has_side_effects=DATAFLOW),
    )(*bufs, *ssems, *rsems, after)
    return list(res)


def _sibling_share(bufs, layout, *, name):
    no = len(bufs)
    nt = len(layout)

    def body(*refs):
        outs = refs[no:2 * no]
        ssem, rsem = refs[2 * no:]
        x, y, c, q, sib, chips = _place()

        def slot(t, half):
            o, lead = layout[t]
            return outs[o].at[half] if lead is None else outs[o].at[lead, half]

        sends = []
        for t in range(nt):
            cp = _rcopy(slot(t, c), slot(t, c), ssem.at[t], rsem.at[t], sib)
            cp.start()
            sends.append(cp)
        for t in range(nt):
            _rcopy(slot(t, 1 - c), slot(t, 1 - c), ssem.at[t], rsem.at[t], sib).wait_recv()
        for cp in sends:
            cp.wait_send()

    out_shape = [jax.ShapeDtypeStruct(b.shape, b.dtype) for b in bufs]
    return pl.pallas_call(
        body, name=name, in_specs=[HBM] * no, out_specs=[HBM] * no, out_shape=out_shape,
        input_output_aliases={o: o for o in range(no)},
        scratch_shapes=[pltpu.SemaphoreType.DMA((nt,)), pltpu.SemaphoreType.DMA((nt,))],
    )(*bufs)


def _small_tail(local, params, *, name):
    (cwa, cba, ga, ba, gb, bb, dws, dsb, dbq, dsk, cwf0, cbf0, cwf1, cbf1,
     g00, g01, g10, g11, b00, b01, b10, b11, err) = local
    n_local = len(local)
    kw, wa = cwa.shape
    ng = dws.shape[0]
    nqkv = dbq.shape[1]
    nsk = dsk.shape[1]
    f = cwf0.shape[2]
    dm = err.shape[1]
    row_vec = 8 * (-(-kw // 8))
    shapes = [(row_vec + 8, wa), (ng * B_CHUNK + 8, B_CHUNK), (8, nqkv), (2, 2, 8, f), (16, dm)]
    n_grp = len(shapes)
    flat_params = [a for triple in params for a in triple]
    n_par = len(params)

    def reduce_body(*refs):
        loc = refs[:n_local]
        tot = refs[n_local:n_local + n_grp]
        scr = refs[n_local + n_grp:]
        grp, from_sib, pair, gath = (scr[k * n_grp:(k + 1) * n_grp] for k in range(4))
        ssem1, rsem1, ssem2, rsem2 = scr[4 * n_grp:]
        x, y, core, q, sib, chips = _place()

        for gr in grp:
            gr[...] = jnp.zeros_like(gr)
        a, b, c, dd, e = grp
        a[0:kw, :] = loc[0][...]
        for k in range(5):
            a[row_vec + k:row_vec + k + 1, :] = loc[1 + k][...]
        for g in range(ng):
            b[g * B_CHUNK:(g + 1) * B_CHUNK, :] = loc[6][g]
        b[ng * B_CHUNK:ng * B_CHUNK + ng, :] = loc[7][...]
        c[0:1, :] = loc[8][...]
        c[1:2, 0:nsk] = loc[9][...]
        for l in range(2):
            for s in range(2):
                dd[l, s, 0:3, :] = loc[10 + 2 * l][s]
                dd[l, s, 3:4, :] = loc[11 + 2 * l][s]
        for k in range(9):
            e[k:k + 1, :] = loc[14 + k][...]

        sends = []
        for gi in range(n_grp):
            cp = _rcopy(grp[gi], from_sib[gi], ssem1.at[gi], rsem1.at[gi], sib)
            cp.start()
            sends.append(cp)
        for gi in range(n_grp):
            _rcopy(grp[gi], from_sib[gi], ssem1.at[gi], rsem1.at[gi], sib).wait_recv()
            both = grp[gi][...] + from_sib[gi][...]
            pair[gi][...] = both
            gath[gi][q] = both
            for j, (px, py) in enumerate(chips):
                cp = _rcopy(pair[gi], gath[gi].at[q], ssem2.at[gi, j], rsem2.at[gi, j], (px, py, core))
                cp.start()
                sends.append(cp)
        for gi in range(n_grp):
            for j, (px, py) in enumerate(chips):
                slot = gath[gi].at[2 * px + py]
                _rcopy(slot, slot, ssem2.at[gi, j], rsem2.at[gi, j], (px, py, core)).wait_recv()
            acc = gath[gi][0]
            for k in range(1, 4):
                acc = acc + gath[gi][k]
            tot[gi][...] = acc
        for cp in sends:
            cp.wait_send()

    vm = pl.BlockSpec(memory_space=pltpu.VMEM)
    scratch = ([pltpu.VMEM(s, F32) for s in shapes] * 3 + [pltpu.VMEM((4,) + s, F32) for s in shapes]
               + [pltpu.SemaphoreType.DMA((n_grp,)), pltpu.SemaphoreType.DMA((n_grp,)),
                  pltpu.SemaphoreType.DMA((n_grp, 3)), pltpu.SemaphoreType.DMA((n_grp, 3))])
    totals = pl.pallas_call(
        reduce_body, name=name + "_reduce", in_specs=[vm] * n_local, out_specs=[vm] * n_grp,
        out_shape=[jax.ShapeDtypeStruct(s, F32) for s in shapes], scratch_shapes=scratch,
        compiler_params=pltpu.CompilerParams(vmem_limit_bytes=VMEM_LIMIT),
    )(*local)

    def adamw_body(*refs):
        ta, tb, tc, td, te = refs[:n_grp]
        par = refs[n_grp:n_grp + 3 * n_par]
        outs = refs[n_grp + 3 * n_par:n_grp + 7 * n_par]
        loss_ref = refs[n_grp + 7 * n_par]
        q = 2 * lax.axis_index("x") + lax.axis_index("y")

        def mine(piece):
            out = piece(0)
            for k in range(1, 4):
                out = jnp.where(q == k, piece(k), out)
            return out

        def update(p, grad, index=None):
            at = (lambda r: r[...]) if index is None else (lambda r: r[index])
            w_ref, m_ref, v_ref = par[3 * p:3 * p + 3]
            g_ref, d_ref, nm_ref, nv_ref = outs[4 * p:4 * p + 4]
            delta, nm, nv = _adamw_math(at(w_ref), grad, at(m_ref), at(v_ref))
            for r, val in ((g_ref, grad), (d_ref, delta), (nm_ref, nm), (nv_ref, nv)):
                if index is None:
                    r[...] = val
                else:
                    r[index] = val

        wq = wa // 4
        update(0, mine(lambda k: ta[0:kw, k * wq:(k + 1) * wq]), (0,))
        for k in range(5):
            update(1 + k, ta[row_vec + k:row_vec + k + 1, :])
        for g in range(ng):
            update(6, tb[g * B_CHUNK:(g + 1) * B_CHUNK, :], (0, g))
        update(7, tb[ng * B_CHUNK:ng * B_CHUNK + ng, :], (0,))
        nq4 = nqkv // 4
        update(8, mine(lambda k: tc[0:1, k * nq4:(k + 1) * nq4]))
        update(9, tc[1:2, 0:nsk])
        fh = f // 2
        for l in range(2):
            update(10, mine(lambda k: td[l, k // 2, 0:3, (k % 2) * fh:(k % 2 + 1) * fh]), (l,))
            update(11, jnp.concatenate([td[l, 0, 3:4, :], td[l, 1, 3:4, :]], axis=1), (slice(l, l + 1),))
        dq4 = dm // 4
        for i in range(2):
            for j in range(2):
                for p, base in ((12, 0), (13, 4)):
                    row = base + 2 * i + j
                    update(p, mine(lambda k: te[row:row + 1, k * dq4:(k + 1) * dq4]), (i, slice(j, j + 1)))
        loss_ref[...] = (0.5 / dm) * jnp.sum(te[8:9, :], axis=1, keepdims=True)

    out_shape = []
    for w, _, _ in params:
        out_shape += [jax.ShapeDtypeStruct(w.shape, F32)] * 4
    out_shape.append(jax.ShapeDtypeStruct((1, 1), F32))
    res = pl.pallas_call(
        adamw_body, name=name + "_adamw", in_specs=[vm] * (n_grp + 3 * n_par), out_specs=[vm] * len(out_shape),
        out_shape=out_shape, compiler_params=pltpu.CompilerParams(vmem_limit_bytes=VMEM_LIMIT),
    )(*totals, *flat_params)
    return [res[4 * p:4 * p + 4] for p in range(n_par)], res[-1]


def _pack(arrays, rows_multiple):
    flat = jnp.concatenate([a.reshape(-1) for a in arrays])
    rows = -(-flat.shape[0] // LANES)
    rows = -(-rows // rows_multiple) * rows_multiple
    flat = jnp.pad(flat, (0, rows * LANES - flat.shape[0]))
    return flat.reshape(rows, LANES)


def _unshard_cols(stacked):
    moved = jnp.moveaxis(stacked, 0, -2)
    return moved.reshape(moved.shape[:-2] + (4 * stacked.shape[-1],))


def kernel(x, ab_w_in, a_conv_w, a_conv_b, a_norm_g, a_norm_b, b_norm_g, b_norm_b, b_spatial_w, b_spatial_b, ab_w_out, c_w_qkv, c_b_qkv, c_sinks, c_w_o, ffn_w_up, ffn_conv_w, ffn_conv_b, ffn_w_down, ln_g, ln_b, loss_target, m_ab_w_in, m_a_conv_w, m_a_conv_b, m_a_norm_g, m_a_norm_b, m_b_norm_g, m_b_norm_b, m_b_spatial_w, m_b_spatial_b, m_ab_w_out, m_c_w_qkv, m_c_b_qkv, m_c_sinks, m_c_w_o, m_ffn_w_up, m_ffn_conv_w, m_ffn_conv_b, m_ffn_w_down, m_ln_g, m_ln_b, v_ab_w_in, v_a_conv_w, v_a_conv_b, v_a_norm_g, v_a_norm_b, v_b_norm_g, v_b_norm_b, v_b_spatial_w, v_b_spatial_b, v_ab_w_out, v_c_w_qkv, v_c_b_qkv, v_c_sinks, v_c_w_o, v_ffn_w_up, v_ffn_conv_w, v_ffn_conv_b, v_ffn_w_down, v_ln_g, v_ln_b):
    rows, d = x.shape[1], x.shape[2]
    depth = ln_g.shape[0]
    assert depth == 2 and x.shape[0] == 1
    alpha = (2.0 * depth) ** 0.25
    f = ffn_w_down.shape[1] * 4
    n_q = c_sinks.shape[1]
    q_idx = 2 * lax.axis_index("x") + lax.axis_index("y")
    c_idx = lax.axis_index("c")
    xs, tgt = x[0], loss_target[0]

    def own_slot(part):
        buf = lax.empty((4,) + part.shape, part.dtype)
        return lax.dynamic_update_slice(buf, part[None], (q_idx, 0, 0, 0))

    def halves(wm):
        return own_slot(wm.astype(BF16).reshape((2, wm.shape[0] // 2) + wm.shape[1:]))

    small_sharded = [a_conv_w[0], c_b_qkv[0], ffn_conv_w, ln_g, ln_b]
    small_pack = _pack(small_sharded, 16)
    bufs = [halves(ab_w_in[0]), own_slot(small_pack.reshape(2, small_pack.shape[0] // 2, LANES)), halves(ab_w_out[0]),
            halves(ffn_w_up[0]), halves(ffn_w_down[0]), halves(c_w_qkv[0]), halves(c_w_o[0]),
            halves(ffn_w_up[1]), halves(ffn_w_down[1])]
    ssems, rsems, started = _gather_start(bufs, name="gather_start")

    def arrive(idx, after, tag):
        got = _gather_wait([started[i] for i in idx], [ssems[i] for i in idx], [rsems[i] for i in idx], after,
                           name=f"gather_wait_{tag}")
        return [g.reshape(4, 2 * g.shape[2], g.shape[3]) for g in got]

    w_in, small_all = arrive([0, 1], xs, "in")
    small_all = small_all.reshape(4, -1)
    sh_shapes = [s.shape for s in small_sharded]
    pieces, pos = [], 0
    for s in sh_shapes:
        n = math.prod(s)
        pieces.append(_unshard_cols(small_all[:, pos:pos + n].reshape((4,) + s)))
        pos += n
    conv_w_a, b_qkv, conv_w_f, ln_gf, ln_bf = pieces

    tril = jnp.tril(jnp.ones((B_CHUNK, B_CHUNK), F32))
    ws = (b_spatial_w[0] * tril).astype(BF16)
    wst = jnp.swapaxes(ws, 1, 2)
    sbb = jnp.broadcast_to(b_spatial_b[0][:, :, None], b_spatial_w[0].shape)
    mix_vecs = [a_conv_b, a_norm_g, a_norm_b, b_norm_g, b_norm_b]
    cw_f = [jnp.swapaxes(conv_w_f[l].reshape(3, 2, f), 0, 1) for l in range(depth)]
    cb_f = [ffn_conv_b[l].reshape(2, 1, f) for l in range(depth)]
    lng = lambda i, j: ln_gf[i, j].reshape(1, d)
    lnb = lambda i, j: ln_bf[i, j].reshape(1, d)
    sinks = c_sinks[0]

    w_up, w_down = [None, None], [None, None]

    def ffn_fwd(xin, l):
        w_up[l], = arrive([3 + 4 * l], xin, f"up{l}")
        hf, fact = _ffn_up_fwd(xin, w_up[l], cw_f[l], cb_f[l], name=f"ffn{l}_up")
        w_down[l] = arrive([4 + 4 * l], fact, f"down{l}")[0].reshape(-1, d)
        out = _matmul(fact, w_down[l], name=f"ffn{l}_down", tm=512, tn=1024, tk=2816)
        return hf, fact, out

    h0 = _matmul(xs, w_in, name="mix_in", tm=1024, tn=512, tk=1024, out_stack=4)
    ab, a2 = _mixer_fwd(h0, conv_w_a, *mix_vecs, ws, sbb, name="mix_mid")
    w_out = arrive([2], ab, "out")[0].reshape(-1, d)
    mix = _matmul(ab, w_out, name="mix_out", tm=1024, tn=1024, tk=1024)
    x1 = _add_ln_fwd(xs, mix, lng(0, 0), lnb(0, 0), alpha, name="ln00")
    hf0, f0, ffn0 = ffn_fwd(x1, 0)
    x2 = _add_ln_fwd(x1, ffn0, lng(0, 1), lnb(0, 1), alpha, name="ln01")
    w_qkv = _unshard_cols(arrive([5], x2, "qkv")[0])
    qkv = _matmul(x2, w_qkv, name="att_qkv", tm=1024, tn=w_qkv.shape[1], tk=1024, bias=b_qkv.reshape(1, -1))
    ao, lse = _attn_fwd(qkv, sinks, name="att_core")
    w_o = arrive([6], ao, "o")[0].reshape(-1, d)
    att = _matmul(ao, w_o, name="att_out", tm=1024, tn=1024, tk=1024)
    x3 = _add_ln_fwd(x2, att, lng(1, 0), lnb(1, 0), alpha, name="ln10")
    hf1, f1, ffn1 = ffn_fwd(x3, 1)
    x4 = _add_ln_fwd(x3, ffn1, lng(1, 1), lnb(1, 1), alpha, name="ln11")
    sq_err, dy = _loss_and_grad(x4, tgt, name="loss")

    def owner_view(g):
        if g.ndim == 3:
            return g.reshape(4, 2, g.shape[1] // 2, g.shape[2])
        return g.reshape(4, 2, g.shape[0] // 8, g.shape[1])

    in_flight = []

    def send_grads(tag, grads):
        lands = [lax.empty((8,) + g.shape[2:], BF16) for g in grads]
        ss, rs, g_thru, l_thru, token = _reduce_start(grads, lands, name=f"reduce_start_{tag}")
        in_flight.append((tag, g_thru, l_thru, ss, rs))
        return token[0:1, 0:1]

    def ffn_bwd(dz, xin, hf, fact, l):
        d_wdown = _matmul(fact, dz, name=f"ffn{l}_down_dw", ta=True, tm=1408, tn=1024, tk=2048, out_dtype=BF16)
        dfa = _matmul(dz, w_down[l], name=f"ffn{l}_down_dx", tb=True, tm=1024, tn=1408, tk=1024)
        dx_parts, d_wup, dcw, dcb = _ffn_up_bwd(hf, dfa, xin, w_up[l], cw_f[l], cb_f[l], name=f"ffn{l}_up_bwd")
        tok = send_grads(f"ffn{l}", [owner_view(d_wup), owner_view(d_wdown)])
        return [(dx_parts, 1.0), (dz, alpha)], dcw, dcb, tok

    dz, dg11, db11 = _add_ln_bwd([(dy, 1.0)], x3, ffn1, lng(1, 1), alpha, name="ln11_bwd")
    dx3, dcw1, dcb1, tok = ffn_bwd(dz, x3, hf1, f1, 1)
    dz, dg10, db10 = _add_ln_bwd(dx3, x2, att, lng(1, 0) + tok, alpha, name="ln10_bwd")
    d_wo = _matmul(ao, dz, name="att_out_dw", ta=True, tm=1024, tn=1024, tk=1024, out_dtype=BF16)
    dao = _matmul(dz, w_o, name="att_out_dx", tb=True, tm=1024, tn=1024, tk=1024)
    dq, dkc, dkp, d_sinks = _attn_bwd(qkv, dao, lse, sinks, name="att_core_bwd")
    dqkv, d_bqkv = _dqkv_assemble(dq, dkc, dkp, name="att_dqkv")
    d_wqkv = _matmul(x2, dqkv, name="att_qkv_dw", ta=True, tm=1024, tn=dqkv.shape[1], tk=1024, out_dtype=BF16)
    d_wqkv_st = jnp.moveaxis(d_wqkv.reshape(d_wqkv.shape[0], 4, -1), 1, 0)
    tok = send_grads("att", [owner_view(d_wqkv_st), owner_view(d_wo)])
    dx2 = _matmul(dqkv, w_qkv, name="att_qkv_dx", tb=True, tm=1024, tn=1024, tk=dqkv.shape[1], addend=(dz, alpha))
    dz, dg01, db01 = _add_ln_bwd([(dx2, 1.0)], x1, ffn0, lng(0, 1) + tok, alpha, name="ln01_bwd")
    dx1, dcw0, dcb0, tok = ffn_bwd(dz, x1, hf0, f0, 0)
    dz, dg00, db00 = _add_ln_bwd(dx1, xs, mix, lng(0, 0) + tok, alpha, name="ln00_bwd")
    d_wout = _matmul(ab, dz, name="mix_out_dw", ta=True, tm=1024, tn=1024, tk=1024, out_dtype=BF16)
    dab = _matmul(dz, w_out, name="mix_out_dx", tb=True, tm=1024, tn=1024, tk=1024)
    grad_x, d_win, d_cwa, d_cba, d_ga, d_ba, d_gb, d_bb, d_ws, d_sb = _mixer_bwd(
        h0, a2, dab, xs, w_in, dz, alpha, conv_w_a, *mix_vecs[1:], ws, wst, sbb, tril, name="mix_bwd")
    send_grads("mix", [owner_view(d_win), owner_view(d_wout)])

    place = jnp.stack([q_idx, c_idx, 4 * lax.axis_index("x") + 2 * lax.axis_index("y") + c_idx]).astype(jnp.int32)
    where = {"mix": [(0, None), (1, None)], "att": [(2, None), (3, None)], "ffn0": [(4, 0), (5, 0)], "ffn1": [(4, 1), (5, 1)]}
    shard_bufs = [None] * 6
    layout = []
    for tag, g_thru, l_thru, ss, rs in in_flight:
        own, landed = _reduce_wait(g_thru, l_thru, ss, rs, grad_x, name=f"reduce_wait_{tag}")
        for k, (o, lead) in enumerate(where[tag]):
            piece = own[k].shape[2:]
            shape = (2,) + piece if lead is None else (2, 2) + piece
            shard_bufs[o] = _octo_sum(own[k], landed[k], place, shard_bufs[o], (lead, shape), name=f"reduce_sum_{tag}{k}")
            layout.append((o, lead))
    shared = _sibling_share(shard_bufs, layout, name="reduce_share")
    g_win = shared[0].reshape(ab_w_in.shape)
    g_wout = shared[1].reshape(ab_w_out.shape)
    g_wqkv = shared[2].reshape(c_w_qkv.shape)
    g_wo = shared[3].reshape(c_w_o.shape)
    g_wup = shared[4].reshape(ffn_w_up.shape)
    g_wdown = shared[5].reshape(ffn_w_down.shape)

    small_w = [a_conv_w, a_conv_b, a_norm_g, a_norm_b, b_norm_g, b_norm_b, b_spatial_w, b_spatial_b, c_b_qkv,
               c_sinks, ffn_conv_w, ffn_conv_b, ln_g, ln_b]
    small_m = [m_a_conv_w, m_a_conv_b, m_a_norm_g, m_a_norm_b, m_b_norm_g, m_b_norm_b, m_b_spatial_w, m_b_spatial_b,
               m_c_b_qkv, m_c_sinks, m_ffn_conv_w, m_ffn_conv_b, m_ln_g, m_ln_b]
    small_v = [v_a_conv_w, v_a_conv_b, v_a_norm_g, v_a_norm_b, v_b_norm_g, v_b_norm_b, v_b_spatial_w, v_b_spatial_b,
               v_c_b_qkv, v_c_sinks, v_ffn_conv_w, v_ffn_conv_b, v_ln_g, v_ln_b]
    local = [d_cwa, d_cba, d_ga, d_ba, d_gb, d_bb, d_ws, d_sb, d_bqkv, d_sinks, dcw0, dcb0, dcw1, dcb1,
             dg00, dg01, dg10, dg11, db00, db01, db10, db11, sq_err]
    small_out, loss = _small_tail(local, list(zip(small_w, small_m, small_v)), name="small_tail")
    loss = loss[0, 0]
    small_g = [o[0] for o in small_out]
    sm_delta = [o[1] for o in small_out]
    sm_m = [o[2] for o in small_out]
    sm_v = [o[3] for o in small_out]

    def adamw_big(w, g, m, v, name):
        two_d = lambda a: a.reshape(-1, a.shape[-1])
        outs = _adamw(two_d(w), two_d(g), two_d(m), two_d(v), name=name)
        return [o.reshape(w.shape) for o in outs]

    big_w = [ab_w_in, ab_w_out, c_w_qkv, c_w_o, ffn_w_up, ffn_w_down]
    big_g = [g_win, g_wout, g_wqkv, g_wo, g_wup, g_wdown]
    big_m = [m_ab_w_in, m_ab_w_out, m_c_w_qkv, m_c_w_o, m_ffn_w_up, m_ffn_w_down]
    big_v = [v_ab_w_in, v_ab_w_out, v_c_w_qkv, v_c_w_o, v_ffn_w_up, v_ffn_w_down]
    big_out = [adamw_big(w, g, m, v, f"adamw_big{t}") for t, (w, g, m, v) in enumerate(zip(big_w, big_g, big_m, big_v))]

    order_big = {0: 0, 9: 1, 10: 2, 13: 3, 14: 4, 17: 5}
    order_small = {1: 0, 2: 1, 3: 2, 4: 3, 5: 4, 6: 5, 7: 6, 8: 7, 11: 8, 12: 9, 15: 10, 16: 11, 18: 12, 19: 13}
    grads, deltas, new_m, new_v = [], [], [], []
    for pos_w in range(20):
        if pos_w in order_big:
            t = order_big[pos_w]
            grads.append(big_out[t][3])
            deltas.append(big_out[t][0])
            new_m.append(big_out[t][1])
            new_v.append(big_out[t][2])
        else:
            t = order_small[pos_w]
            grads.append(small_g[t])
            deltas.append(sm_delta[t])
            new_m.append(sm_m[t])
            new_v.append(sm_v[t])
    return (loss, grad_x[None], *grads, *deltas, *new_m, *new_v)
```

```python
import math

import jax
import jax.numpy as jnp
from jax import lax
from jax.experimental import pallas as pl
from jax.experimental.pallas import tpu as pltpu

F32 = jnp.float32
BF16 = jnp.bfloat16
MESH = pl.DeviceIdType.MESH

LN_EPS = 1e-5
HEAD_DIM = 64
ATT_BLOCK = 128
Q_PER_KV = 8
A_KERNEL = 31
CONV_HALO = 32
FFN_HALO = 8
B_CHUNK = 128
LANES = 128
MXU_WIDTH = 256
GELU_C = math.sqrt(2.0 / math.pi)
ADAM_LR = 0.001
ADAM_B1 = 0.9
ADAM_B2 = 0.999
ADAM_EPS = 1e-08
ADAM_WD = 0.01
ADAM_STEP = 10
VMEM_LIMIT = 56 * 1024 * 1024


def _cp(*dims):
    return pltpu.CompilerParams(dimension_semantics=dims, vmem_limit_bytes=VMEM_LIMIT)


def _pick(n, prefs):
    for p in prefs:
        if n % p == 0:
            return p
    return n


def _sig(x):
    return 1.0 / (1.0 + jnp.exp(-x))


def _gelu(x):
    t = jnp.tanh(GELU_C * (x + 0.044715 * (x * x * x)))
    return x * (0.5 * (1.0 + t)), t


def _gelu_grad(x, t):
    return 0.5 * (1.0 + t) + 0.5 * x * (1.0 - t * t) * (GELU_C * (1.0 + 3.0 * 0.044715 * x * x))


def _ln_stats(z):
    mu = jnp.mean(z, axis=-1, keepdims=True)
    zc = z - mu
    var = jnp.mean(zc * zc, axis=-1, keepdims=True)
    rstd = lax.rsqrt(var + LN_EPS)
    return zc * rstd, rstd


def _ln_bwd(dxh, xh, rstd):
    return rstd * (dxh - jnp.mean(dxh, axis=-1, keepdims=True) - xh * jnp.mean(dxh * xh, axis=-1, keepdims=True))


def _rowsum(a):
    return jnp.sum(a, axis=0, keepdims=True)


def _lshape(a):
    return (a.shape[0], a.shape[1]) if a.ndim == 2 else (a.shape[1], a.shape[0] * a.shape[2])


def _spec2(arr, blk_r, blk_c, ridx, cidx):
    if len(arr.shape) == 2:
        return pl.BlockSpec((blk_r, blk_c), lambda i, j, k: (ridx(i, j, k), cidx(i, j, k)))
    per = arr.shape[2] // blk_c
    assert arr.shape[2] % blk_c == 0
    return pl.BlockSpec((None, blk_r, blk_c), lambda i, j, k: (cidx(i, j, k) // per, ridx(i, j, k), cidx(i, j, k) % per))


def _matmul(a, b, *, name, ta=False, tb=False, tm, tn, tk, out_dtype=F32, out_stack=None, bias=None, addend=None):
    ar, ac = _lshape(a)
    br, bc = _lshape(b)
    m, kdim = (ac, ar) if ta else (ar, ac)
    n = br if tb else bc
    assert (bc if tb else br) == kdim
    tm, tn, tk = min(tm, m), min(tn, n), min(tk, kdim)
    assert m % tm == 0 and n % tn == 0 and kdim % tk == 0, (name, m, n, kdim, tm, tn, tk)
    nk = kdim // tk
    gi, gj, gk = (lambda i, j, k: i), (lambda i, j, k: j), (lambda i, j, k: k)
    a_spec = _spec2(a, tk, tm, gk, gi) if ta else _spec2(a, tm, tk, gi, gk)
    b_spec = _spec2(b, tn, tk, gj, gk) if tb else _spec2(b, tk, tn, gk, gj)
    if out_stack is None:
        out_sds = jax.ShapeDtypeStruct((m, n), out_dtype)
    else:
        out_sds = jax.ShapeDtypeStruct((out_stack, m, n // out_stack), out_dtype)
    o_spec = _spec2(out_sds, tm, tn, gi, gj)
    in_specs = [a_spec, b_spec]
    args = [a, b]
    if bias is not None:
        in_specs.append(pl.BlockSpec((1, tn), lambda i, j, k: (0, j)))
        args.append(bias)
    scale = None
    if addend is not None:
        add_arr, scale = addend
        in_specs.append(pl.BlockSpec((tm, tn), lambda i, j, k: (i, j)))
        args.append(add_arr)
    use_acc = nk > 1 and out_dtype != F32
    dn = (((0 if ta else 1,), (1 if tb else 0,)), ((), ()))

    def body(*refs):
        a_ref, b_ref = refs[0], refs[1]
        pos = 2
        bias_ref = add_ref = None
        if bias is not None:
            bias_ref = refs[pos]
            pos += 1
        if addend is not None:
            add_ref = refs[pos]
            pos += 1
        o_ref = refs[pos]
        acc_ref = refs[pos + 1] if use_acc else o_ref
        p = lax.dot_general(a_ref[...].astype(BF16), b_ref[...].astype(BF16), dn, preferred_element_type=F32)

        def finish(val):
            if bias_ref is not None:
                val = val + bias_ref[...]
            if add_ref is not None:
                val = val + scale * add_ref[...]
            return val.astype(out_dtype)

        if nk == 1:
            o_ref[...] = finish(p)
        else:
            k = pl.program_id(2)

            @pl.when(k == 0)
            def _():
                acc_ref[...] = p

            @pl.when(k > 0)
            def _():
                acc_ref[...] += p

            if use_acc or bias_ref is not None or add_ref is not None:
                @pl.when(k == nk - 1)
                def _():
                    o_ref[...] = finish(acc_ref[...])

    return pl.pallas_call(
        body, name=name, grid=(m // tm, n // tn, nk), in_specs=in_specs, out_specs=o_spec, out_shape=out_sds,
        scratch_shapes=[pltpu.VMEM((tm, tn), F32)] if use_acc else [],
        compiler_params=_cp("parallel", "parallel", "arbitrary"),
    )(*args)


def _add_ln_fwd(x, s, g, b, alpha, *, name):
    rows, d = x.shape
    t = _pick(rows, (512, 256))

    def body(x_ref, s_ref, g_ref, b_ref, y_ref):
        xh, _ = _ln_stats(alpha * x_ref[...] + s_ref[...])
        y_ref[...] = xh * g_ref[...] + b_ref[...]

    row = pl.BlockSpec((t, d), lambda i: (i, 0))
    vec = pl.BlockSpec((1, d), lambda i: (0, 0))
    return pl.pallas_call(body, name=name, grid=(rows // t,), in_specs=[row, row, vec, vec], out_specs=row,
                          out_shape=jax.ShapeDtypeStruct((rows, d), F32), compiler_params=_cp("parallel"))(x, s, g, b)


def _add_ln_bwd(dy_terms, x, s, g, alpha, *, name):
    rows, d = x.shape
    t = _pick(rows, (512, 256))
    nterm = len(dy_terms)
    scales = [sc for _, sc in dy_terms]
    ranks = [a.ndim for a, _ in dy_terms]

    def body(*refs):
        dy_refs = refs[:nterm]
        x_ref, s_ref, g_ref, dz_ref, dg_ref, db_ref = refs[nterm:]

        @pl.when(pl.program_id(0) == 0)
        def _():
            dg_ref[...] = jnp.zeros_like(dg_ref)
            db_ref[...] = jnp.zeros_like(db_ref)

        dyv = None
        for r, sc, rank in zip(dy_refs, scales, ranks):
            slabs = [r[...]] if rank == 2 else [r[p] for p in range(r.shape[0])]
            for v in slabs:
                v = v if sc == 1.0 else sc * v
                dyv = v if dyv is None else dyv + v
        xh, rstd = _ln_stats(alpha * x_ref[...] + s_ref[...])
        dz_ref[...] = _ln_bwd(dyv * g_ref[...], xh, rstd)
        dg_ref[...] += _rowsum(dyv * xh)
        db_ref[...] += _rowsum(dyv)

    row = pl.BlockSpec((t, d), lambda i: (i, 0))
    vec = pl.BlockSpec((1, d), lambda i: (0, 0))
    vsds = jax.ShapeDtypeStruct((1, d), F32)
    dy_specs = [row if a.ndim == 2 else pl.BlockSpec((a.shape[0], t, d), lambda i: (0, i, 0)) for a, _ in dy_terms]
    return pl.pallas_call(body, name=name, grid=(rows // t,), in_specs=dy_specs + [row, row, vec], out_specs=[row, vec, vec],
                          out_shape=[jax.ShapeDtypeStruct((rows, d), F32), vsds, vsds],
                          compiler_params=_cp("arbitrary"))(*[a for a, _ in dy_terms], x, s, g)


def _loss_and_grad(y, tgt, *, name):
    rows, d = y.shape
    t = _pick(rows, (512, 256))

    def body(y_ref, t_ref, l_ref, dy_ref):
        @pl.when(pl.program_id(0) == 0)
        def _():
            l_ref[...] = jnp.zeros_like(l_ref)

        e = y_ref[...] - t_ref[...]
        l_ref[...] += _rowsum(e * e)
        dy_ref[...] = e * (1.0 / d)

    row = pl.BlockSpec((t, d), lambda i: (i, 0))
    vec = pl.BlockSpec((1, d), lambda i: (0, 0))
    return pl.pallas_call(body, name=name, grid=(rows // t,), in_specs=[row, row], out_specs=[vec, row],
                          out_shape=[jax.ShapeDtypeStruct((1, d), F32), jax.ShapeDtypeStruct((rows, d), F32)],
                          compiler_params=_cp("arbitrary"))(y, tgt)


def _col_blocks(width):
    out, pos = [], 0
    while pos < width:
        w = MXU_WIDTH if width - pos >= MXU_WIDTH else width - pos
        out.append(slice(pos, pos + w))
        pos += w
    return out


def _conv3(e, w, b):
    r1 = pltpu.roll(e, 1, 0)
    r2 = pltpu.roll(e, 2, 0)
    return w[0:1, :] * r2 + w[1:2, :] * r1 + w[2:3, :] * e + b, (r2, r1, e)


def _ffn_up_fwd(x, w_up, cw, cb, *, name):
    rows, d = x.shape
    nq, _, tc = w_up.shape
    nj = nq // 2
    f = tc * nj
    tm = _pick(rows, (512, 256))
    blocks = _col_blocks(tc)

    def body(x_ref, wg_ref, wv_ref, cw_ref, cb_ref, hf_ref, f_ref, prev_ref):
        @pl.when(pl.program_id(1) == 0)
        def _():
            prev_ref[...] = jnp.zeros_like(prev_ref)

        xb = x_ref[...].astype(BF16)
        for cs in blocks:
            hc = []
            for s, w_ref in ((0, wg_ref), (1, wv_ref)):
                h = jnp.dot(xb, w_ref[:, cs], preferred_element_type=F32)
                hf_ref[s, :, cs] = h
                e = jnp.concatenate([prev_ref[s, :, cs], h], axis=0)
                prev_ref[s, :, cs] = h[tm - FFN_HALO:]
                y, _ = _conv3(e, cw_ref[s, :, cs], cb_ref[s, :, cs])
                hc.append(y[FFN_HALO:])
            gl, _ = _gelu(hc[0])
            f_ref[:, cs] = (gl * hc[1]).astype(BF16)

    in_specs = [
        pl.BlockSpec((tm, d), lambda j, i: (i, 0)),
        pl.BlockSpec((None, d, tc), lambda j, i: (j, 0, 0)),
        pl.BlockSpec((None, d, tc), lambda j, i: (nj + j, 0, 0)),
        pl.BlockSpec((2, 3, tc), lambda j, i: (0, 0, j)),
        pl.BlockSpec((2, 1, tc), lambda j, i: (0, 0, j)),
    ]
    out_specs = [pl.BlockSpec((2, tm, tc), lambda j, i: (0, i, j)), pl.BlockSpec((tm, tc), lambda j, i: (i, j))]
    out_shape = [jax.ShapeDtypeStruct((2, rows, f), F32), jax.ShapeDtypeStruct((rows, f), BF16)]
    return pl.pallas_call(body, name=name, grid=(nj, rows // tm), in_specs=in_specs, out_specs=out_specs, out_shape=out_shape,
                          scratch_shapes=[pltpu.VMEM((2, FFN_HALO, tc), F32)],
                          compiler_params=_cp("parallel", "arbitrary"))(x, w_up, w_up, cw, cb)


def _ffn_up_bwd(hf, df, x, w_up, cw, cb, *, name):
    _, rows, f = hf.shape
    d = x.shape[1]
    nq, _, tc = w_up.shape
    nj = nq // 2
    tm = _pick(rows, (512, 256))
    hb = tm // FFN_HALO
    once = pl.Buffered(1)
    ni = rows // tm
    last_blk = rows // FFN_HALO - 1
    ext = tm + 2 * FFN_HALO
    tile = slice(FFN_HALO, FFN_HALO + tm)
    blocks = _col_blocks(tc)

    def body(h_ref, hp_ref, hn_ref, d_ref, dn_ref, x_ref, wg_ref, wv_ref, cw_ref, cb_ref, dx_ref, dw_out_ref, dcw_ref, dcb_ref,
             dw_ref):
        i = pl.program_id(1)
        first = i == 0
        last = i == ni - 1

        @pl.when(first)
        def _():
            dw_ref[...] = jnp.zeros_like(dw_ref)
            dcw_ref[...] = jnp.zeros_like(dcw_ref)
            dcb_ref[...] = jnp.zeros_like(dcb_ref)

        xt = x_ref[...].astype(BF16).T
        dx = None
        for cs in blocks:
            wc = cs.stop - cs.start
            de = jnp.concatenate([jnp.zeros((FFN_HALO, wc), F32), d_ref[:, cs], jnp.where(last, 0.0, dn_ref[:, cs])], axis=0)
            taps, hc = [], []
            for s in range(2):
                e = jnp.concatenate([jnp.where(first, 0.0, hp_ref[s, :, cs]), h_ref[s, :, cs], hn_ref[s, :, cs]], axis=0)
                y, tp = _conv3(e, cw_ref[s, :, cs], cb_ref[s, :, cs])
                hc.append(y)
                taps.append(tp)
            gl, th = _gelu(hc[0])
            dhc = (de * hc[1] * _gelu_grad(hc[0], th), de * gl)
            for s, w_ref in ((0, wg_ref), (1, wv_ref)):
                w = cw_ref[s, :, cs]
                g = dhc[s]
                dh = (w[2:3, :] * g + w[1:2, :] * pltpu.roll(g, ext - 1, 0) + w[0:1, :] * pltpu.roll(g, ext - 2, 0))[tile]
                gt = g[tile]
                for k in range(3):
                    dcw_ref[s, k:k + 1, cs] += _rowsum(gt * taps[s][k][tile])
                dcb_ref[s, :, cs] += _rowsum(gt)
                dhb = dh.astype(BF16)
                part = lax.dot_general(dhb, w_ref[:, cs], (((1,), (1,)), ((), ())), preferred_element_type=F32)
                dx = part if dx is None else dx + part
                dw_ref[s, :, cs] += jnp.dot(xt, dhb, preferred_element_type=F32)
        dx_ref[...] = dx

        @pl.when(last)
        def _():
            dw_out_ref[...] = dw_ref[...].astype(BF16)

    in_specs = [
        pl.BlockSpec((2, tm, tc), lambda j, i: (0, i, j)),
        pl.BlockSpec((2, FFN_HALO, tc), lambda j, i: (0, jnp.maximum(i * hb - 1, 0), j)),
        pl.BlockSpec((2, FFN_HALO, tc), lambda j, i: (0, jnp.minimum((i + 1) * hb, last_blk), j)),
        pl.BlockSpec((tm, tc), lambda j, i: (i, j)),
        pl.BlockSpec((FFN_HALO, tc), lambda j, i: (jnp.minimum((i + 1) * hb, last_blk), j)),
        pl.BlockSpec((tm, d), lambda j, i: (i, 0)),
        pl.BlockSpec((None, d, tc), lambda j, i: (j, 0, 0), pipeline_mode=once),
        pl.BlockSpec((None, d, tc), lambda j, i: (nj + j, 0, 0), pipeline_mode=once),
        pl.BlockSpec((2, 3, tc), lambda j, i: (0, 0, j)),
        pl.BlockSpec((2, 1, tc), lambda j, i: (0, 0, j)),
    ]
    out_specs = [
        pl.BlockSpec((None, tm, d), lambda j, i: (j, i, 0)),
        pl.BlockSpec((2, None, d, tc), lambda j, i: (0, j, 0, 0), pipeline_mode=once),
        pl.BlockSpec((2, 3, tc), lambda j, i: (0, 0, j)),
        pl.BlockSpec((2, 1, tc), lambda j, i: (0, 0, j)),
    ]
    out_shape = [jax.ShapeDtypeStruct((nj, rows, d), F32), jax.ShapeDtypeStruct((2, nj, d, tc), BF16),
                 jax.ShapeDtypeStruct((2, 3, f), F32), jax.ShapeDtypeStruct((2, 1, f), F32)]
    dx, dw, dcw, dcb = pl.pallas_call(body, name=name, grid=(nj, ni), in_specs=in_specs, out_specs=out_specs,
                                      out_shape=out_shape, scratch_shapes=[pltpu.VMEM((2, d, tc), F32)],
                                      compiler_params=_cp("parallel", "arbitrary"))(
        hf, hf, hf, df, df, x, w_up, w_up, cw, cb)
    return dx, dw.reshape(nq, d, tc), dcw, dcb


def _mixer_fwd(h0, cw, cb, ga, ba, gb, bb, ws, sbb, *, name):
    _, rows, w = h0.shape
    t = _pick(rows, (256,))
    hb = t // CONV_HALO
    groups = w // B_CHUNK

    def body(h_ref, hp_ref, cw_ref, cb_ref, ga_ref, ba_ref, gb_ref, bb_ref, ws_ref, sb_ref, o_ref, a2_ref):
        first = pl.program_id(0) == 0
        a1 = h_ref[0] * _sig(h_ref[1])
        a1p = jnp.where(first, 0.0, hp_ref[0] * _sig(hp_ref[1]))
        e = jnp.concatenate([a1p, a1], axis=0)
        acc = cw_ref[A_KERNEL - 1:A_KERNEL, :] * e
        for k in range(A_KERNEL - 1):
            acc = acc + cw_ref[k:k + 1, :] * pltpu.roll(e, A_KERNEL - 1 - k, 0)
        a2 = acc[CONV_HALO:] + cb_ref[...]
        a2_ref[...] = a2
        xh, _ = _ln_stats(a2)
        a3 = xh * ga_ref[...] + ba_ref[...]
        o_ref[:, 0:w] = (a3 * _sig(a3)).astype(BF16)

        u, _ = _gelu(h_ref[2])
        v1, _ = _gelu(h_ref[3])
        xh2, _ = _ln_stats(v1)
        v2 = (xh2 * gb_ref[...] + bb_ref[...]).astype(BF16)
        for c in range(t // B_CHUNK):
            rs = slice(c * B_CHUNK, (c + 1) * B_CHUNK)
            for g in range(groups):
                cs = slice(g * B_CHUNK, (g + 1) * B_CHUNK)
                mixed = jnp.dot(ws_ref[g], v2[rs, cs], preferred_element_type=F32) + sb_ref[g]
                o_ref[rs, w + g * B_CHUNK:w + (g + 1) * B_CHUNK] = (u[rs, cs] * mixed).astype(BF16)

    vec = pl.BlockSpec((1, w), lambda i: (0, 0))
    grp = pl.BlockSpec((groups, B_CHUNK, B_CHUNK), lambda i: (0, 0, 0))
    in_specs = [
        pl.BlockSpec((4, t, w), lambda i: (0, i, 0)),
        pl.BlockSpec((2, CONV_HALO, w), lambda i: (0, jnp.maximum(i * hb - 1, 0), 0)),
        pl.BlockSpec((A_KERNEL, w), lambda i: (0, 0)),
        vec, vec, vec, vec, vec, grp, grp,
    ]
    out_specs = [pl.BlockSpec((t, 2 * w), lambda i: (i, 0)), pl.BlockSpec((t, w), lambda i: (i, 0))]
    out_shape = [jax.ShapeDtypeStruct((rows, 2 * w), BF16), jax.ShapeDtypeStruct((rows, w), F32)]
    return pl.pallas_call(body, name=name, grid=(rows // t,), in_specs=in_specs, out_specs=out_specs, out_shape=out_shape,
                          compiler_params=_cp("parallel"))(h0, h0, cw, cb, ga, ba, gb, bb, ws, sbb)


def _mixer_bwd(h0, a2, dab, x, w_in, res, res_scale, cw, ga, ba, gb, bb, ws, wst, sbb, tril, *, name):
    _, rows, w = h0.shape
    d = x.shape[1]
    once = pl.Buffered(1)
    t = _pick(rows, (256,))
    hb = t // CONV_HALO
    ni = rows // t
    last_blk = rows // CONV_HALO - 1
    ext = t + CONV_HALO
    tile = slice(0, t)
    groups = w // B_CHUNK
    taps = A_KERNEL - 1

    def body(h_ref, a2_ref, a2n_ref, d_ref, dn_ref, x_ref, win_ref, res_ref, cw_ref, ga_ref, ba_ref, gb_ref, bb_ref,
             ws_ref, wst_ref, sb_ref, tril_ref, dx_ref, dwin_ref, dcw_ref, dcb_ref, dga_ref, dba_ref, dgb_ref, dbb_ref,
             dws_ref, dsb_ref, dw_ref):
        i = pl.program_id(0)
        first = i == 0
        last = i == ni - 1

        @pl.when(first)
        def _():
            for r in (dw_ref, dcw_ref, dcb_ref, dga_ref, dba_ref, dgb_ref, dbb_ref, dws_ref, dsb_ref):
                r[...] = jnp.zeros_like(r)

        xt = x_ref[...].astype(BF16).T
        dx_terms = []

        def through_w_in(slot, dh):
            dhb = dh.astype(BF16)
            dx_terms.append(lax.dot_general(dhb, win_ref[slot], (((1,), (1,)), ((), ())), preferred_element_type=F32))
            dw_ref[slot] += jnp.dot(xt, dhb, preferred_element_type=F32)

        xh, rstd = _ln_stats(jnp.concatenate([a2_ref[...], a2n_ref[...]], axis=0))
        a3 = xh * ga_ref[...] + ba_ref[...]
        s3 = _sig(a3)
        da_e = jnp.concatenate([d_ref[:, 0:w], jnp.where(last, 0.0, dn_ref[...])], axis=0)
        da3 = da_e * (s3 * (1.0 + a3 * (1.0 - s3)))
        da2 = _ln_bwd(da3 * ga_ref[...], xh, rstd)
        dga_ref[...] += _rowsum(da3[tile] * xh[tile])
        dba_ref[...] += _rowsum(da3[tile])
        dcb_ref[...] += _rowsum(da2[tile])
        sgt = _sig(h_ref[1])
        a1t = h_ref[0] * sgt
        da1t = None
        for k in range(A_KERNEL):
            sh = taps - k
            fed = (da2 if sh == 0 else pltpu.roll(da2, ext - sh, 0))[tile]
            dcw_ref[k:k + 1, :] += _rowsum(a1t * fed)
            term = cw_ref[k:k + 1, :] * fed
            da1t = term if da1t is None else da1t + term
        through_w_in(0, da1t * sgt)
        through_w_in(1, da1t * h_ref[0] * sgt * (1.0 - sgt))

        bu = h_ref[2]
        bv = h_ref[3]
        u, tu = _gelu(bu)
        v1, tv = _gelu(bv)
        xh2, rstd2 = _ln_stats(v1)
        v2 = (xh2 * gb_ref[...] + bb_ref[...]).astype(BF16)
        db = d_ref[:, w:2 * w]
        dmx_all = db * u
        du_parts, dv2_parts = [], []
        for c in range(t // B_CHUNK):
            rs = slice(c * B_CHUNK, (c + 1) * B_CHUNK)
            du_row, dv2_row = [], []
            for g in range(groups):
                cs = slice(g * B_CHUNK, (g + 1) * B_CHUNK)
                v2cg = v2[rs, cs]
                mixed = jnp.dot(ws_ref[g], v2cg, preferred_element_type=F32) + sb_ref[g]
                dmx = dmx_all[rs, cs]
                dmxb = dmx.astype(BF16)
                du_row.append(db[rs, cs] * mixed)
                dv2_row.append(jnp.dot(wst_ref[g], dmxb, preferred_element_type=F32))
                dws_ref[g] += tril_ref[...] * lax.dot_general(dmxb, v2cg, (((1,), (1,)), ((), ())),
                                                               preferred_element_type=F32)
                dsb_ref[g:g + 1, :] += _rowsum(dmx.T)
            du_parts.append(jnp.concatenate(du_row, axis=1))
            dv2_parts.append(jnp.concatenate(dv2_row, axis=1))
        du = jnp.concatenate(du_parts, axis=0)
        dv2 = jnp.concatenate(dv2_parts, axis=0)
        dgb_ref[...] += _rowsum(dv2 * xh2)
        dbb_ref[...] += _rowsum(dv2)
        dv1 = _ln_bwd(dv2 * gb_ref[...], xh2, rstd2)
        through_w_in(2, du * _gelu_grad(bu, tu))
        through_w_in(3, dv1 * _gelu_grad(bv, tv))
        dx_ref[...] = res_scale * res_ref[...] + ((dx_terms[0] + dx_terms[1]) + (dx_terms[2] + dx_terms[3]))

        @pl.when(last)
        def _():
            dwin_ref[...] = dw_ref[...].astype(BF16)

    vec = pl.BlockSpec((1, w), lambda i: (0, 0))
    grp = pl.BlockSpec((groups, B_CHUNK, B_CHUNK), lambda i: (0, 0, 0))
    halo = pl.BlockSpec((CONV_HALO, w), lambda i: (jnp.minimum((i + 1) * hb, last_blk), 0))
    wide = pl.BlockSpec((t, d), lambda i: (i, 0))
    in_specs = [
        pl.BlockSpec((4, t, w), lambda i: (0, i, 0)),
        pl.BlockSpec((t, w), lambda i: (i, 0)),
        halo,
        pl.BlockSpec((t, 2 * w), lambda i: (i, 0)),
        halo,
        wide,
        pl.BlockSpec((4, d, w), lambda i: (0, 0, 0), pipeline_mode=once),
        wide,
        pl.BlockSpec((A_KERNEL, w), lambda i: (0, 0)),
        vec, vec, vec, vec, grp, grp, grp,
        pl.BlockSpec((B_CHUNK, B_CHUNK), lambda i: (0, 0)),
    ]
    vsds = jax.ShapeDtypeStruct((1, w), F32)
    out_specs = [
        wide,
        pl.BlockSpec((4, d, w), lambda i: (0, 0, 0), pipeline_mode=once),
        pl.BlockSpec((A_KERNEL, w), lambda i: (0, 0)),
        vec, vec, vec, vec, vec, grp,
        pl.BlockSpec((groups, B_CHUNK), lambda i: (0, 0)),
    ]
    out_shape = [jax.ShapeDtypeStruct((rows, d), F32), jax.ShapeDtypeStruct((4, d, w), BF16),
                 jax.ShapeDtypeStruct((A_KERNEL, w), F32),
                 vsds, vsds, vsds, vsds, vsds, jax.ShapeDtypeStruct((groups, B_CHUNK, B_CHUNK), F32),
                 jax.ShapeDtypeStruct((groups, B_CHUNK), F32)]
    return pl.pallas_call(body, name=name, grid=(ni,), in_specs=in_specs, out_specs=out_specs, out_shape=out_shape,
                          scratch_shapes=[pltpu.VMEM((4, d, w), F32)], compiler_params=_cp("arbitrary"))(
        h0, a2, a2, dab, dab, x, w_in, res, cw, ga, ba, gb, bb, ws, wst, sbb, tril)


GROUP_ROWS = Q_PER_KV * ATT_BLOCK


def _attn_mask(n):
    qi = lax.broadcasted_iota(jnp.int32, (GROUP_ROWS, 2 * ATT_BLOCK), 0) & (ATT_BLOCK - 1)
    sj = lax.broadcasted_iota(jnp.int32, (GROUP_ROWS, 2 * ATT_BLOCK), 1)
    diff = qi + ATT_BLOCK - sj
    return (diff >= 0) & (diff < ATT_BLOCK) & ((n > 0) | (sj >= ATT_BLOCK))


def _stack_heads(ref, kvh, dtype):
    heads = [ref[:, (kvh * Q_PER_KV + g) * HEAD_DIM:(kvh * Q_PER_KV + g + 1) * HEAD_DIM] for g in range(Q_PER_KV)]
    return jnp.concatenate(heads, axis=0).astype(dtype)


def _per_row_sink(sink_ref, kvh):
    head = lax.broadcasted_iota(jnp.int32, (GROUP_ROWS, 1), 0) // ATT_BLOCK
    out = jnp.zeros((GROUP_ROWS, 1), F32)
    for g in range(Q_PER_KV):
        out = jnp.where(head == g, sink_ref[kvh * Q_PER_KV + g], out)
    return out


def _attn_specs(rows, n_q):
    dq = n_q * HEAD_DIM
    dkv = 2 * (n_q // Q_PER_KV) * HEAD_DIM
    kv_blk = dq // dkv
    assert dq % dkv == 0
    return dq, dkv, [
        pl.BlockSpec(memory_space=pltpu.SMEM),
        pl.BlockSpec((ATT_BLOCK, dq), lambda n: (n, 0)),
        pl.BlockSpec((ATT_BLOCK, dkv), lambda n: (n, kv_blk)),
        pl.BlockSpec((ATT_BLOCK, dkv), lambda n: (jnp.maximum(n - 1, 0), kv_blk)),
    ]


def _kv_pair(kvc_ref, kvp_ref, kvh, n_kv):
    ks = slice(kvh * HEAD_DIM, (kvh + 1) * HEAD_DIM)
    vs = slice((n_kv + kvh) * HEAD_DIM, (n_kv + kvh + 1) * HEAD_DIM)
    kk = jnp.concatenate([kvp_ref[:, ks], kvc_ref[:, ks]], axis=0).astype(BF16)
    vv = jnp.concatenate([kvp_ref[:, vs], kvc_ref[:, vs]], axis=0).astype(BF16)
    return kk, vv


def _attn_fwd(qkv, sinks, *, name):
    rows = qkv.shape[0]
    n_q = sinks.shape[0]
    n_kv = n_q // Q_PER_KV
    scale = 1.0 / math.sqrt(HEAD_DIM)
    dq, _, in_specs = _attn_specs(rows, n_q)

    def body(sink_ref, q_ref, kvc_ref, kvp_ref, o_ref, lse_ref):
        valid = _attn_mask(pl.program_id(0))
        for kvh in range(n_kv):
            kk, vv = _kv_pair(kvc_ref, kvp_ref, kvh, n_kv)
            qs = _stack_heads(q_ref, kvh, BF16)
            s = lax.dot_general(qs, kk, (((1,), (1,)), ((), ())), preferred_element_type=F32)
            s = jnp.where(valid, s * scale, -jnp.inf)
            sk = _per_row_sink(sink_ref, kvh)
            m = jnp.maximum(jnp.max(s, axis=1, keepdims=True), sk)
            p = jnp.exp(s - m)
            l = jnp.sum(p, axis=1, keepdims=True) + jnp.exp(sk - m)
            o = jnp.dot((p / l).astype(BF16), vv, preferred_element_type=F32)
            lse = m + jnp.log(l)
            for g in range(Q_PER_KV):
                h = kvh * Q_PER_KV + g
                rs = slice(g * ATT_BLOCK, (g + 1) * ATT_BLOCK)
                o_ref[:, h * HEAD_DIM:(h + 1) * HEAD_DIM] = o[rs]
                lse_ref[:, h:h + 1] = lse[rs]

    out_specs = [pl.BlockSpec((ATT_BLOCK, dq), lambda n: (n, 0)), pl.BlockSpec((ATT_BLOCK, n_q), lambda n: (n, 0))]
    out_shape = [jax.ShapeDtypeStruct((rows, dq), F32), jax.ShapeDtypeStruct((rows, n_q), F32)]
    return pl.pallas_call(body, name=name, grid=(rows // ATT_BLOCK,), in_specs=in_specs, out_specs=out_specs,
                          out_shape=out_shape, compiler_params=_cp("parallel"))(sinks, qkv, qkv, qkv)


def _attn_bwd(qkv, dout, lse, sinks, *, name):
    rows = qkv.shape[0]
    n_q = sinks.shape[0]
    n_kv = n_q // Q_PER_KV
    scale = 1.0 / math.sqrt(HEAD_DIM)
    dq_w, dkv_w, in_specs = _attn_specs(rows, n_q)
    blk_q = pl.BlockSpec((ATT_BLOCK, dq_w), lambda n: (n, 0))
    blk_kv = pl.BlockSpec((ATT_BLOCK, dkv_w), lambda n: (n, 0))
    in_specs = in_specs + [blk_q, pl.BlockSpec((ATT_BLOCK, n_q), lambda n: (n, 0))]

    def body(sink_ref, q_ref, kvc_ref, kvp_ref, do_ref, lse_ref, dq_ref, dkc_ref, dkp_ref, dsink_ref):
        n = pl.program_id(0)

        @pl.when(n == 0)
        def _():
            dsink_ref[...] = jnp.zeros_like(dsink_ref)

        valid = _attn_mask(n)
        head_ids = lax.broadcasted_iota(jnp.int32, (1, n_q), 1)
        dsink = jnp.zeros((1, n_q), F32)
        for kvh in range(n_kv):
            kk, vv = _kv_pair(kvc_ref, kvp_ref, kvh, n_kv)
            qs = _stack_heads(q_ref, kvh, BF16)
            dos = _stack_heads(do_ref, kvh, BF16)
            lse = jnp.concatenate([lse_ref[:, kvh * Q_PER_KV + g:kvh * Q_PER_KV + g + 1] for g in range(Q_PER_KV)], axis=0)
            s = lax.dot_general(qs, kk, (((1,), (1,)), ((), ())), preferred_element_type=F32)
            s = jnp.where(valid, s * scale, -jnp.inf)
            p = jnp.exp(s - lse)
            dp = lax.dot_general(dos, vv, (((1,), (1,)), ((), ())), preferred_element_type=F32)
            delta = jnp.sum(p * dp, axis=1, keepdims=True)
            ds = (p * (dp - delta) * scale).astype(BF16)
            sink_term = jnp.exp(_per_row_sink(sink_ref, kvh) - lse) * delta
            dqs = jnp.dot(ds, kk, preferred_element_type=F32)
            for g in range(Q_PER_KV):
                h = kvh * Q_PER_KV + g
                rs = slice(g * ATT_BLOCK, (g + 1) * ATT_BLOCK)
                dsink = dsink + jnp.where(head_ids == h, -jnp.sum(sink_term[rs]), 0.0)
                dq_ref[:, h * HEAD_DIM:(h + 1) * HEAD_DIM] = dqs[rs]
            dk = lax.dot_general(ds, qs, (((0,), (0,)), ((), ())), preferred_element_type=F32)
            dv = lax.dot_general(p.astype(BF16), dos, (((0,), (0,)), ((), ())), preferred_element_type=F32)
            ks = slice(kvh * HEAD_DIM, (kvh + 1) * HEAD_DIM)
            vs = slice((n_kv + kvh) * HEAD_DIM, (n_kv + kvh + 1) * HEAD_DIM)
            dkp_ref[:, ks] = dk[0:ATT_BLOCK]
            dkc_ref[:, ks] = dk[ATT_BLOCK:]
            dkp_ref[:, vs] = dv[0:ATT_BLOCK]
            dkc_ref[:, vs] = dv[ATT_BLOCK:]
        dsink_ref[...] += dsink

    out_specs = [blk_q, blk_kv, blk_kv, pl.BlockSpec((1, n_q), lambda n: (0, 0))]
    out_shape = [jax.ShapeDtypeStruct((rows, dq_w), F32), jax.ShapeDtypeStruct((rows, dkv_w), F32),
                 jax.ShapeDtypeStruct((rows, dkv_w), F32), jax.ShapeDtypeStruct((1, n_q), F32)]
    return pl.pallas_call(body, name=name, grid=(rows // ATT_BLOCK,), in_specs=in_specs, out_specs=out_specs,
                          out_shape=out_shape, compiler_params=_cp("arbitrary"))(sinks, qkv, qkv, qkv, dout, lse)


def _dqkv_assemble(dq, dkc, dkp, *, name):
    rows, dq_w = dq.shape
    dkv_w = dkc.shape[1]
    nb = rows // ATT_BLOCK

    def body(dq_ref, dkc_ref, dkp_ref, o_ref, db_ref):
        n = pl.program_id(0)

        @pl.when(n == 0)
        def _():
            db_ref[...] = jnp.zeros_like(db_ref)

        dqv = dq_ref[...]
        dkv = dkc_ref[...] + jnp.where(n == nb - 1, 0.0, dkp_ref[...])
        o_ref[:, 0:dq_w] = dqv.astype(BF16)
        o_ref[:, dq_w:dq_w + dkv_w] = dkv.astype(BF16)
        db_ref[:, 0:dq_w] += _rowsum(dqv)
        db_ref[:, dq_w:dq_w + dkv_w] += _rowsum(dkv)

    width = dq_w + dkv_w
    in_specs = [pl.BlockSpec((ATT_BLOCK, dq_w), lambda n: (n, 0)), pl.BlockSpec((ATT_BLOCK, dkv_w), lambda n: (n, 0)),
                pl.BlockSpec((ATT_BLOCK, dkv_w), lambda n: (jnp.minimum(n + 1, nb - 1), 0))]
    out_specs = [pl.BlockSpec((ATT_BLOCK, width), lambda n: (n, 0)), pl.BlockSpec((1, width), lambda n: (0, 0))]
    out_shape = [jax.ShapeDtypeStruct((rows, width), BF16), jax.ShapeDtypeStruct((1, width), F32)]
    return pl.pallas_call(body, name=name, grid=(nb,), in_specs=in_specs, out_specs=out_specs, out_shape=out_shape,
                          compiler_params=_cp("arbitrary"))(dq, dkc, dkp)


def _row_tile(r, c):
    budget = 2 * 1024 * 1024 // (4 * c)
    for cand in (1024, 512, 256, 128, 64, 32, 16):
        if cand <= budget and r % cand == 0:
            return cand
    return r


def _octo_sum(own, recv, place, dest, lead, *, name):
    _, _, r, c = own.shape
    t = _row_tile(r, c)
    lead_idx, buf_shape = lead

    def body(place_ref, own_ref, *rest):
        o_ref = rest[7] if dest is None else rest[8]
        acc = own_ref[...].astype(F32)
        for k in range(7):
            acc = acc + rest[k][...].astype(F32)
        o_ref[...] = acc

    def peer(mask):
        return pl.BlockSpec((None, t, c), lambda i, pr: (pr[2] ^ mask, i, 0))

    if lead_idx is None:
        o_spec = pl.BlockSpec((None, t, c), lambda i, pr: (pr[1], i, 0))
    else:
        o_spec = pl.BlockSpec((None, None, t, c), lambda i, pr: (lead_idx, pr[1], i, 0))
    in_specs = [pl.BlockSpec((None, None, t, c), lambda i, pr: (pr[0], pr[1], i, 0))] + [peer(m) for m in range(1, 8)]
    args = [place, own] + [recv] * 7
    aliases = {}
    if dest is not None:
        in_specs.append(HBM)
        args.append(dest)
        aliases = {9: 0}
    grid_spec = pltpu.PrefetchScalarGridSpec(num_scalar_prefetch=1, grid=(r // t,), in_specs=in_specs, out_specs=o_spec)
    return pl.pallas_call(body, name=name, grid_spec=grid_spec, out_shape=jax.ShapeDtypeStruct(buf_shape, F32),
                          input_output_aliases=aliases, compiler_params=_cp("parallel"))(*args)


def _adamw_math(w, g, m, v):
    nm = ADAM_B1 * m + (1.0 - ADAM_B1) * g
    nv = ADAM_B2 * v + (1.0 - ADAM_B2) * (g * g)
    m_hat = nm / (1.0 - ADAM_B1 ** ADAM_STEP)
    v_hat = nv / (1.0 - ADAM_B2 ** ADAM_STEP)
    return -ADAM_LR * (m_hat / (jnp.sqrt(v_hat) + ADAM_EPS) + ADAM_WD * w), nm, nv


def _adamw(w, g, m, v, *, name):
    r, c = w.shape
    t = _row_tile(r, c)

    def body(w_ref, g_ref, m_ref, v_ref, d_ref, nm_ref, nv_ref, go_ref):
        gv = g_ref[...]
        d_ref[...], nm_ref[...], nv_ref[...] = _adamw_math(w_ref[...], gv, m_ref[...], v_ref[...])
        go_ref[...] = gv

    blk = pl.BlockSpec((t, c), lambda i: (i, 0))
    sds = jax.ShapeDtypeStruct((r, c), F32)
    return pl.pallas_call(body, name=name, grid=(r // t,), in_specs=[blk] * 4, out_specs=[blk] * 4,
                          out_shape=[sds] * 4, compiler_params=_cp("parallel"))(w, g, m, v)


HBM = pl.BlockSpec(memory_space=pl.ANY)


def _place():
    x, y, c = lax.axis_index("x"), lax.axis_index("y"), lax.axis_index("c")
    chips = [(1 - x, y), (x, 1 - y), (1 - x, 1 - y)]
    return x, y, c, 2 * x + y, (x, y, 1 - c), chips


def _rcopy(src, dst, ssem, rsem, dev):
    return pltpu.make_async_remote_copy(src_ref=src, dst_ref=dst, send_sem=ssem, recv_sem=rsem, device_id=dev,
                                        device_id_type=MESH)


HBM_ONLY = pl.BlockSpec(memory_space=pltpu.HBM)
SEM = pl.BlockSpec(memory_space=pltpu.SEMAPHORE)


def _peers():
    x, y, c = lax.axis_index("x"), lax.axis_index("y"), lax.axis_index("c")
    out = []
    for mask in range(1, 8):
        px = 1 - x if mask & 4 else x
        py = 1 - y if mask & 2 else y
        pc = 1 - c if mask & 1 else c
        out.append(((px, py, pc), 2 * px + py, pc, 4 * px + 2 * py + pc))
    return 4 * x + 2 * y + c, out


def _reduce_start(grads, lands, *, name):
    nt = len(grads)

    def body(*refs):
        ssems, rsems = refs[2 * nt:3 * nt], refs[3 * nt:4 * nt]
        g_out, l_out, token = refs[4 * nt:5 * nt], refs[5 * nt:6 * nt], refs[6 * nt]
        me, peers = _peers()
        for t in range(nt):
            for k, (dev, chip, core, _) in enumerate(peers):
                _rcopy(g_out[t].at[chip, core], l_out[t].at[me], ssems[t].at[k], rsems[t].at[k], dev).start()
        token[...] = jnp.zeros_like(token)

    sems = [pltpu.SemaphoreType.DMA((7,))] * (2 * nt)
    out_shape = (sems + [pltpu.HBM(g.shape, g.dtype) for g in grads] + [pltpu.HBM(l.shape, l.dtype) for l in lands]
                 + [jax.ShapeDtypeStruct((8, LANES), F32)])
    res = pl.pallas_call(
        body, name=name, in_specs=[HBM_ONLY] * (2 * nt),
        out_specs=[SEM] * (2 * nt) + [HBM_ONLY] * (2 * nt) + [pl.BlockSpec(memory_space=pltpu.VMEM)], out_shape=out_shape,
        input_output_aliases={t: 2 * nt + t for t in range(2 * nt)},
        compiler_params=pltpu.CompilerParams(has_side_effects=DATAFLOW),
    )(*[pltpu.with_memory_space_constraint(a, pltpu.HBM) for a in list(grads) + list(lands)])
    return res[:nt], res[nt:2 * nt], res[2 * nt:3 * nt], res[3 * nt:4 * nt], res[4 * nt]


def _reduce_wait(grads, lands, ssems, rsems, after, *, name):
    nt = len(grads)

    def body(*refs):
        ssem_refs, rsem_refs = refs[2 * nt:3 * nt], refs[3 * nt:4 * nt]
        g_out, l_out = refs[4 * nt + 1:5 * nt + 1], refs[5 * nt + 1:6 * nt + 1]
        me, peers = _peers()
        for t in range(nt):
            for k, (dev, chip, core, _) in enumerate(peers):
                _rcopy(g_out[t].at[chip, core], l_out[t].at[me], ssem_refs[t].at[k], rsem_refs[t].at[k], dev).wait_send()
        for t in range(nt):
            for k, (dev, _, _, idx) in enumerate(peers):
                slot = l_out[t].at[idx]
                _rcopy(slot, slot, ssem_refs[t].at[k], rsem_refs[t].at[k], dev).wait_recv()

    res = pl.pallas_call(
        body, name=name, in_specs=[HBM_ONLY] * (2 * nt) + [SEM] * (2 * nt) + [HBM], out_specs=[HBM_ONLY] * (2 * nt),
        out_shape=[pltpu.HBM(a.shape, a.dtype) for a in list(grads) + list(lands)],
        input_output_aliases={t: t for t in range(2 * nt)},
        compiler_params=pltpu.CompilerParams(has_side_effects=DATAFLOW),
    )(*grads, *lands, *ssems, *rsems, after)
    return list(res[:nt]), list(res[nt:])
DATAFLOW = pltpu.SideEffectType.DATAFLOW_SIDE_EFFECTING


def _gather_start(bufs, *, name):
    nt = len(bufs)

    def body(*refs):
        ssems, rsems, outs = refs[nt:2 * nt], refs[2 * nt:3 * nt], refs[3 * nt:4 * nt]
        x, y, c, q, sib, chips = _place()
        for t in range(nt):
            for j, (px, py) in enumerate(chips):
                mine = outs[t].at[q]
                _rcopy(mine, mine, ssems[t].at[j], rsems[t].at[j], (px, py, c)).start()

    sems = [pltpu.SemaphoreType.DMA((3,))] * (2 * nt)
    out_shape = sems + [pltpu.HBM(b.shape, b.dtype) for b in bufs]
    res = pl.pallas_call(
        body, name=name, in_specs=[HBM_ONLY] * nt, out_specs=[SEM] * (2 * nt) + [HBM_ONLY] * nt, out_shape=out_shape,
        input_output_aliases={t: 2 * nt + t for t in range(nt)},
        compiler_params=pltpu.CompilerParams(has_side_effects=DATAFLOW),
    )(*[pltpu.with_memory_space_constraint(b, pltpu.HBM) for b in bufs])
    return res[:nt], res[nt:2 * nt], res[2 * nt:]


def _gather_wait(bufs, ssems, rsems, after, *, name):
    nt = len(bufs)

    def body(*refs):
        ssem_refs, rsem_refs = refs[nt:2 * nt], refs[2 * nt:3 * nt]
        outs = refs[3 * nt + 1:]
        x, y, c, q, sib, chips = _place()
        for t in range(nt):
            for j, (px, py) in enumerate(chips):
                mine = outs[t].at[q]
                _rcopy(mine, mine, ssem_refs[t].at[j], rsem_refs[t].at[j], (px, py, c)).wait_send()
        for t in range(nt):
            for j, (px, py) in enumerate(chips):
                theirs = outs[t].at[2 * px + py]
                _rcopy(theirs, theirs, ssem_refs[t].at[j], rsem_refs[t].at[j], (px, py, c)).wait_recv()

    res = pl.pallas_call(
        body, name=name, in_specs=[HBM_ONLY] * nt + [SEM] * (2 * nt) + [HBM], out_specs=[HBM_ONLY] * nt,
        out_shape=[pltpu.HBM(b.shape, b.dtype) for b in bufs], input_output_aliases={t: t for t in range(nt)},
        compiler_params=pltpu.CompilerParams(has_side_effects=DATAFLOW),
    )(*bufs, *ssems, *rsems, after)
    return list(res)


def _sibling_share(bufs, layout, *, name):
    no = len(bufs)
    nt = len(layout)

    def body(*refs):
        outs = refs[no:2 * no]
        ssem, rsem = refs[2 * no:]
        x, y, c, q, sib, chips = _place()

        def slot(t, half):
            o, lead = layout[t]
            return outs[o].at[half] if lead is None else outs[o].at[lead, half]

        sends = []
        for t in range(nt):
            cp = _rcopy(slot(t, c), slot(t, c), ssem.at[t], rsem.at[t], sib)
            cp.start()
            sends.append(cp)
        for t in range(nt):
            _rcopy(slot(t, 1 - c), slot(t, 1 - c), ssem.at[t], rsem.at[t], sib).wait_recv()
        for cp in sends:
            cp.wait_send()

    out_shape = [jax.ShapeDtypeStruct(b.shape, b.dtype) for b in bufs]
    return pl.pallas_call(
        body, name=name, in_specs=[HBM] * no, out_specs=[HBM] * no, out_shape=out_shape,
        input_output_aliases={o: o for o in range(no)},
        scratch_shapes=[pltpu.SemaphoreType.DMA((nt,)), pltpu.SemaphoreType.DMA((nt,))],
    )(*bufs)


def _small_tail(local, params, *, name):
    (cwa, cba, ga, ba, gb, bb, dws, dsb, dbq, dsk, cwf0, cbf0, cwf1, cbf1,
     g00, g01, g10, g11, b00, b01, b10, b11, err) = local
    n_local = len(local)
    kw, wa = cwa.shape
    ng = dws.shape[0]
    nqkv = dbq.shape[1]
    nsk = dsk.shape[1]
    f = cwf0.shape[2]
    dm = err.shape[1]
    row_vec = 8 * (-(-kw // 8))
    shapes = [(row_vec + 8, wa), (ng * B_CHUNK + 8, B_CHUNK), (8, nqkv), (2, 2, 8, f), (16, dm)]
    n_grp = len(shapes)
    flat_params = [a for triple in params for a in triple]
    n_par = len(params)

    def reduce_body(*refs):
        loc = refs[:n_local]
        tot = refs[n_local:n_local + n_grp]
        scr = refs[n_local + n_grp:]
        grp, from_sib, pair, gath = (scr[k * n_grp:(k + 1) * n_grp] for k in range(4))
        ssem1, rsem1, ssem2, rsem2 = scr[4 * n_grp:]
        x, y, core, q, sib, chips = _place()

        for gr in grp:
            gr[...] = jnp.zeros_like(gr)
        a, b, c, dd, e = grp
        a[0:kw, :] = loc[0][...]
        for k in range(5):
            a[row_vec + k:row_vec + k + 1, :] = loc[1 + k][...]
        for g in range(ng):
            b[g * B_CHUNK:(g + 1) * B_CHUNK, :] = loc[6][g]
        b[ng * B_CHUNK:ng * B_CHUNK + ng, :] = loc[7][...]
        c[0:1, :] = loc[8][...]
        c[1:2, 0:nsk] = loc[9][...]
        for l in range(2):
            for s in range(2):
                dd[l, s, 0:3, :] = loc[10 + 2 * l][s]
                dd[l, s, 3:4, :] = loc[11 + 2 * l][s]
        for k in range(9):
            e[k:k + 1, :] = loc[14 + k][...]

        sends = []
        for gi in range(n_grp):
            cp = _rcopy(grp[gi], from_sib[gi], ssem1.at[gi], rsem1.at[gi], sib)
            cp.start()
            sends.append(cp)
        for gi in range(n_grp):
            _rcopy(grp[gi], from_sib[gi], ssem1.at[gi], rsem1.at[gi], sib).wait_recv()
            both = grp[gi][...] + from_sib[gi][...]
            pair[gi][...] = both
            gath[gi][q] = both
            for j, (px, py) in enumerate(chips):
                cp = _rcopy(pair[gi], gath[gi].at[q], ssem2.at[gi, j], rsem2.at[gi, j], (px, py, core))
                cp.start()
                sends.append(cp)
        for gi in range(n_grp):
            for j, (px, py) in enumerate(chips):
                slot = gath[gi].at[2 * px + py]
                _rcopy(slot, slot, ssem2.at[gi, j], rsem2.at[gi, j], (px, py, core)).wait_recv()
            acc = gath[gi][0]
            for k in range(1, 4):
                acc = acc + gath[gi][k]
            tot[gi][...] = acc
        for cp in sends:
            cp.wait_send()

    vm = pl.BlockSpec(memory_space=pltpu.VMEM)
    scratch = ([pltpu.VMEM(s, F32) for s in shapes] * 3 + [pltpu.VMEM((4,) + s, F32) for s in shapes]
               + [pltpu.SemaphoreType.DMA((n_grp,)), pltpu.SemaphoreType.DMA((n_grp,)),
                  pltpu.SemaphoreType.DMA((n_grp, 3)), pltpu.SemaphoreType.DMA((n_grp, 3))])
    totals = pl.pallas_call(
        reduce_body, name=name + "_reduce", in_specs=[vm] * n_local, out_specs=[vm] * n_grp,
        out_shape=[jax.ShapeDtypeStruct(s, F32) for s in shapes], scratch_shapes=scratch,
        compiler_params=pltpu.CompilerParams(vmem_limit_bytes=VMEM_LIMIT),
    )(*local)

    def adamw_body(*refs):
        ta, tb, tc, td, te = refs[:n_grp]
        par = refs[n_grp:n_grp + 3 * n_par]
        outs = refs[n_grp + 3 * n_par:n_grp + 7 * n_par]
        loss_ref = refs[n_grp + 7 * n_par]
        q = 2 * lax.axis_index("x") + lax.axis_index("y")

        def mine(piece):
            out = piece(0)
            for k in range(1, 4):
                out = jnp.where(q == k, piece(k), out)
            return out

        def update(p, grad, index=None):
            at = (lambda r: r[...]) if index is None else (lambda r: r[index])
            w_ref, m_ref, v_ref = par[3 * p:3 * p + 3]
            g_ref, d_ref, nm_ref, nv_ref = outs[4 * p:4 * p + 4]
            delta, nm, nv = _adamw_math(at(w_ref), grad, at(m_ref), at(v_ref))
            for r, val in ((g_ref, grad), (d_ref, delta), (nm_ref, nm), (nv_ref, nv)):
                if index is None:
                    r[...] = val
                else:
                    r[index] = val

        wq = wa // 4
        update(0, mine(lambda k: ta[0:kw, k * wq:(k + 1) * wq]), (0,))
        for k in range(5):
            update(1 + k, ta[row_vec + k:row_vec + k + 1, :])
        for g in range(ng):
            update(6, tb[g * B_CHUNK:(g + 1) * B_CHUNK, :], (0, g))
        update(7, tb[ng * B_CHUNK:ng * B_CHUNK + ng, :], (0,))
        nq4 = nqkv // 4
        update(8, mine(lambda k: tc[0:1, k * nq4:(k + 1) * nq4]))
        update(9, tc[1:2, 0:nsk])
        fh = f // 2
        for l in range(2):
            update(10, mine(lambda k: td[l, k // 2, 0:3, (k % 2) * fh:(k % 2 + 1) * fh]), (l,))
            update(11, jnp.concatenate([td[l, 0, 3:4, :], td[l, 1, 3:4, :]], axis=1), (slice(l, l + 1),))
        dq4 = dm // 4
        for i in range(2):
            for j in range(2):
                for p, base in ((12, 0), (13, 4)):
                    row = base + 2 * i + j
                    update(p, mine(lambda k: te[row:row + 1, k * dq4:(k + 1) * dq4]), (i, slice(j, j + 1)))
        loss_ref[...] = (0.5 / dm) * jnp.sum(te[8:9, :], axis=1, keepdims=True)

    out_shape = []
    for w, _, _ in params:
        out_shape += [jax.ShapeDtypeStruct(w.shape, F32)] * 4
    out_shape.append(jax.ShapeDtypeStruct((1, 1), F32))
    res = pl.pallas_call(
        adamw_body, name=name + "_adamw", in_specs=[vm] * (n_grp + 3 * n_par), out_specs=[vm] * len(out_shape),
        out_shape=out_shape, compiler_params=pltpu.CompilerParams(vmem_limit_bytes=VMEM_LIMIT),
    )(*totals, *flat_params)
    return [res[4 * p:4 * p + 4] for p in range(n_par)], res[-1]


def _pack(arrays, rows_multiple):
    flat = jnp.concatenate([a.reshape(-1) for a in arrays])
    rows = -(-flat.shape[0] // LANES)
    rows = -(-rows // rows_multiple) * rows_multiple
    flat = jnp.pad(flat, (0, rows * LANES - flat.shape[0]))
    return flat.reshape(rows, LANES)


def _unshard_cols(stacked):
    moved = jnp.moveaxis(stacked, 0, -2)
    return moved.reshape(moved.shape[:-2] + (4 * stacked.shape[-1],))


def kernel(x, ab_w_in, a_conv_w, a_conv_b, a_norm_g, a_norm_b, b_norm_g, b_norm_b, b_spatial_w, b_spatial_b, ab_w_out, c_w_qkv, c_b_qkv, c_sinks, c_w_o, ffn_w_up, ffn_conv_w, ffn_conv_b, ffn_w_down, ln_g, ln_b, loss_target, m_ab_w_in, m_a_conv_w, m_a_conv_b, m_a_norm_g, m_a_norm_b, m_b_norm_g, m_b_norm_b, m_b_spatial_w, m_b_spatial_b, m_ab_w_out, m_c_w_qkv, m_c_b_qkv, m_c_sinks, m_c_w_o, m_ffn_w_up, m_ffn_conv_w, m_ffn_conv_b, m_ffn_w_down, m_ln_g, m_ln_b, v_ab_w_in, v_a_conv_w, v_a_conv_b, v_a_norm_g, v_a_norm_b, v_b_norm_g, v_b_norm_b, v_b_spatial_w, v_b_spatial_b, v_ab_w_out, v_c_w_qkv, v_c_b_qkv, v_c_sinks, v_c_w_o, v_ffn_w_up, v_ffn_conv_w, v_ffn_conv_b, v_ffn_w_down, v_ln_g, v_ln_b):
    rows, d = x.shape[1], x.shape[2]
    depth = ln_g.shape[0]
    assert depth == 2 and x.shape[0] == 1
    alpha = (2.0 * depth) ** 0.25
    f = ffn_w_down.shape[1] * 4
    n_q = c_sinks.shape[1]
    q_idx = 2 * lax.axis_index("x") + lax.axis_index("y")
    c_idx = lax.axis_index("c")
    xs, tgt = x[0], loss_target[0]

    def own_slot(part):
        buf = lax.empty((4,) + part.shape, part.dtype)
        return lax.dynamic_update_slice(buf, part[None], (q_idx, 0, 0, 0))

    def halves(wm):
        return own_slot(wm.astype(BF16).reshape((2, wm.shape[0] // 2) + wm.shape[1:]))

    small_sharded = [a_conv_w[0], c_b_qkv[0], ffn_conv_w, ln_g, ln_b]
    small_pack = _pack(small_sharded, 16)
    bufs = [halves(ab_w_in[0]), own_slot(small_pack.reshape(2, small_pack.shape[0] // 2, LANES)), halves(ab_w_out[0]),
            halves(ffn_w_up[0]), halves(ffn_w_down[0]), halves(c_w_qkv[0]), halves(c_w_o[0]),
            halves(ffn_w_up[1]), halves(ffn_w_down[1])]
    ssems, rsems, started = _gather_start(bufs, name="gather_start")

    def arrive(idx, after, tag):
        got = _gather_wait([started[i] for i in idx], [ssems[i] for i in idx], [rsems[i] for i in idx], after,
                           name=f"gather_wait_{tag}")
        return [g.reshape(4, 2 * g.shape[2], g.shape[3]) for g in got]

    w_in, small_all = arrive([0, 1], xs, "in")
    small_all = small_all.reshape(4, -1)
    sh_shapes = [s.shape for s in small_sharded]
    pieces, pos = [], 0
    for s in sh_shapes:
        n = math.prod(s)
        pieces.append(_unshard_cols(small_all[:, pos:pos + n].reshape((4,) + s)))
        pos += n
    conv_w_a, b_qkv, conv_w_f, ln_gf, ln_bf = pieces

    tril = jnp.tril(jnp.ones((B_CHUNK, B_CHUNK), F32))
    ws = (b_spatial_w[0] * tril).astype(BF16)
    wst = jnp.swapaxes(ws, 1, 2)
    sbb = jnp.broadcast_to(b_spatial_b[0][:, :, None], b_spatial_w[0].shape)
    mix_vecs = [a_conv_b, a_norm_g, a_norm_b, b_norm_g, b_norm_b]
    cw_f = [jnp.swapaxes(conv_w_f[l].reshape(3, 2, f), 0, 1) for l in range(depth)]
    cb_f = [ffn_conv_b[l].reshape(2, 1, f) for l in range(depth)]
    lng = lambda i, j: ln_gf[i, j].reshape(1, d)
    lnb = lambda i, j: ln_bf[i, j].reshape(1, d)
    sinks = c_sinks[0]

    w_up, w_down = [None, None], [None, None]

    def ffn_fwd(xin, l):
        w_up[l], = arrive([3 + 4 * l], xin, f"up{l}")
        hf, fact = _ffn_up_fwd(xin, w_up[l], cw_f[l], cb_f[l], name=f"ffn{l}_up")
        w_down[l] = arrive([4 + 4 * l], fact, f"down{l}")[0].reshape(-1, d)
        out = _matmul(fact, w_down[l], name=f"ffn{l}_down", tm=512, tn=1024, tk=2816)
        return hf, fact, out

    h0 = _matmul(xs, w_in, name="mix_in", tm=1024, tn=512, tk=1024, out_stack=4)
    ab, a2 = _mixer_fwd(h0, conv_w_a, *mix_vecs, ws, sbb, name="mix_mid")
    w_out = arrive([2], ab, "out")[0].reshape(-1, d)
    mix = _matmul(ab, w_out, name="mix_out", tm=1024, tn=1024, tk=1024)
    x1 = _add_ln_fwd(xs, mix, lng(0, 0), lnb(0, 0), alpha, name="ln00")
    hf0, f0, ffn0 = ffn_fwd(x1, 0)
    x2 = _add_ln_fwd(x1, ffn0, lng(0, 1), lnb(0, 1), alpha, name="ln01")
    w_qkv = _unshard_cols(arrive([5], x2, "qkv")[0])
    qkv = _matmul(x2, w_qkv, name="att_qkv", tm=1024, tn=w_qkv.shape[1], tk=1024, bias=b_qkv.reshape(1, -1))
    ao, lse = _attn_fwd(qkv, sinks, name="att_core")
    w_o = arrive([6], ao, "o")[0].reshape(-1, d)
    att = _matmul(ao, w_o, name="att_out", tm=1024, tn=1024, tk=1024)
    x3 = _add_ln_fwd(x2, att, lng(1, 0), lnb(1, 0), alpha, name="ln10")
    hf1, f1, ffn1 = ffn_fwd(x3, 1)
    x4 = _add_ln_fwd(x3, ffn1, lng(1, 1), lnb(1, 1), alpha, name="ln11")
    sq_err, dy = _loss_and_grad(x4, tgt, name="loss")

    def owner_view(g):
        if g.ndim == 3:
            return g.reshape(4, 2, g.shape[1] // 2, g.shape[2])
        return g.reshape(4, 2, g.shape[0] // 8, g.shape[1])

    in_flight = []

    def send_grads(tag, grads):
        lands = [lax.empty((8,) + g.shape[2:], BF16) for g in grads]
        ss, rs, g_thru, l_thru, token = _reduce_start(grads, lands, name=f"reduce_start_{tag}")
        in_flight.append((tag, g_thru, l_thru, ss, rs))
        return token[0:1, 0:1]

    def ffn_bwd(dz, xin, hf, fact, l):
        d_wdown = _matmul(fact, dz, name=f"ffn{l}_down_dw", ta=True, tm=1408, tn=1024, tk=2048, out_dtype=BF16)
        dfa = _matmul(dz, w_down[l], name=f"ffn{l}_down_dx", tb=True, tm=1024, tn=1408, tk=1024)
        dx_parts, d_wup, dcw, dcb = _ffn_up_bwd(hf, dfa, xin, w_up[l], cw_f[l], cb_f[l], name=f"ffn{l}_up_bwd")
        tok = send_grads(f"ffn{l}", [owner_view(d_wup), owner_view(d_wdown)])
        return [(dx_parts, 1.0), (dz, alpha)], dcw, dcb, tok

    dz, dg11, db11 = _add_ln_bwd([(dy, 1.0)], x3, ffn1, lng(1, 1), alpha, name="ln11_bwd")
    dx3, dcw1, dcb1, tok = ffn_bwd(dz, x3, hf1, f1, 1)
    dz, dg10, db10 = _add_ln_bwd(dx3, x2, att, lng(1, 0) + tok, alpha, name="ln10_bwd")
    d_wo = _matmul(ao, dz, name="att_out_dw", ta=True, tm=1024, tn=1024, tk=1024, out_dtype=BF16)
    dao = _matmul(dz, w_o, name="att_out_dx", tb=True, tm=1024, tn=1024, tk=1024)
    dq, dkc, dkp, d_sinks = _attn_bwd(qkv, dao, lse, sinks, name="att_core_bwd")
    dqkv, d_bqkv = _dqkv_assemble(dq, dkc, dkp, name="att_dqkv")
    d_wqkv = _matmul(x2, dqkv, name="att_qkv_dw", ta=True, tm=1024, tn=dqkv.shape[1], tk=1024, out_dtype=BF16)
    d_wqkv_st = jnp.moveaxis(d_wqkv.reshape(d_wqkv.shape[0], 4, -1), 1, 0)
    tok = send_grads("att", [owner_view(d_wqkv_st), owner_view(d_wo)])
    dx2 = _matmul(dqkv, w_qkv, name="att_qkv_dx", tb=True, tm=1024, tn=1024, tk=dqkv.shape[1], addend=(dz, alpha))
    dz, dg01, db01 = _add_ln_bwd([(dx2, 1.0)], x1, ffn0, lng(0, 1) + tok, alpha, name="ln01_bwd")
    dx1, dcw0, dcb0, tok = ffn_bwd(dz, x1, hf0, f0, 0)
    dz, dg00, db00 = _add_ln_bwd(dx1, xs, mix, lng(0, 0) + tok, alpha, name="ln00_bwd")
    d_wout = _matmul(ab, dz, name="mix_out_dw", ta=True, tm=1024, tn=1024, tk=1024, out_dtype=BF16)
    dab = _matmul(dz, w_out, name="mix_out_dx", tb=True, tm=1024, tn=1024, tk=1024)
    grad_x, d_win, d_cwa, d_cba, d_ga, d_ba, d_gb, d_bb, d_ws, d_sb = _mixer_bwd(
        h0, a2, dab, xs, w_in, dz, alpha, conv_w_a, *mix_vecs[1:], ws, wst, sbb, tril, name="mix_bwd")
    send_grads("mix", [owner_view(d_win), owner_view(d_wout)])

    small_w = [a_conv_w, a_conv_b, a_norm_g, a_norm_b, b_norm_g, b_norm_b, b_spatial_w, b_spatial_b, c_b_qkv,
               c_sinks, ffn_conv_w, ffn_conv_b, ln_g, ln_b]
    small_m = [m_a_conv_w, m_a_conv_b, m_a_norm_g, m_a_norm_b, m_b_norm_g, m_b_norm_b, m_b_spatial_w, m_b_spatial_b,
               m_c_b_qkv, m_c_sinks, m_ffn_conv_w, m_ffn_conv_b, m_ln_g, m_ln_b]
    small_v = [v_a_conv_w, v_a_conv_b, v_a_norm_g, v_a_norm_b, v_b_norm_g, v_b_norm_b, v_b_spatial_w, v_b_spatial_b,
               v_c_b_qkv, v_c_sinks, v_ffn_conv_w, v_ffn_conv_b, v_ln_g, v_ln_b]
    local = [d_cwa, d_cba, d_ga, d_ba, d_gb, d_bb, d_ws, d_sb, d_bqkv, d_sinks, dcw0, dcb0, dcw1, dcb1,
             dg00, dg01, dg10, dg11, db00, db01, db10, db11, sq_err]
    small_out, loss = _small_tail(local, list(zip(small_w, small_m, small_v)), name="small_tail")
    loss = loss[0, 0]
    small_g = [o[0] for o in small_out]
    sm_delta = [o[1] for o in small_out]
    sm_m = [o[2] for o in small_out]
    sm_v = [o[3] for o in small_out]

    place = jnp.stack([q_idx, c_idx, 4 * lax.axis_index("x") + 2 * lax.axis_index("y") + c_idx]).astype(jnp.int32)
    where = {"mix": [(0, None), (1, None)], "att": [(2, None), (3, None)], "ffn0": [(4, 0), (5, 0)], "ffn1": [(4, 1), (5, 1)]}
    big_w = [ab_w_in, ab_w_out, c_w_qkv, c_w_o, ffn_w_up, ffn_w_down]
    big_m = [m_ab_w_in, m_ab_w_out, m_c_w_qkv, m_c_w_o, m_ffn_w_up, m_ffn_w_down]
    big_v = [v_ab_w_in, v_ab_w_out, v_c_w_qkv, v_c_w_o, v_ffn_w_up, v_ffn_w_down]
    big_out = [None] * 6

    def finish(tags, after, label):
        bufs, layout = {}, []
        for tag, g_thru, l_thru, ss, rs in in_flight:
            if tag not in tags:
                continue
            own, landed = _reduce_wait(g_thru, l_thru, ss, rs, after, name=f"reduce_wait_{tag}")
            for k, (o, lead) in enumerate(where[tag]):
                piece = own[k].shape[2:]
                shape = (2,) + piece if lead is None else (2, 2) + piece
                bufs[o] = _octo_sum(own[k], landed[k], place, bufs.get(o), (lead, shape), name=f"reduce_sum_{tag}{k}")
                layout.append((o, lead))
        order = sorted(bufs)
        shared = _sibling_share([bufs[o] for o in order], [(order.index(o), lead) for o, lead in layout],
                                name=f"reduce_share_{label}")
        for o, g in zip(order, shared):
            w = big_w[o]
            two_d = lambda a: a.reshape(-1, a.shape[-1])
            outs = _adamw(two_d(w), two_d(g), two_d(big_m[o]), two_d(big_v[o]), name=f"adamw_big{o}")
            big_out[o] = [r.reshape(w.shape) for r in outs]
        return big_out[order[-1]][0]

    done = finish(("ffn1", "att", "ffn0"), sm_delta[0], "early")
    finish(("mix",), done, "mix")

    order_big = {0: 0, 9: 1, 10: 2, 13: 3, 14: 4, 17: 5}
    order_small = {1: 0, 2: 1, 3: 2, 4: 3, 5: 4, 6: 5, 7: 6, 8: 7, 11: 8, 12: 9, 15: 10, 16: 11, 18: 12, 19: 13}
    grads, deltas, new_m, new_v = [], [], [], []
    for pos_w in range(20):
        if pos_w in order_big:
            t = order_big[pos_w]
            grads.append(big_out[t][3])
            deltas.append(big_out[t][0])
            new_m.append(big_out[t][1])
            new_v.append(big_out[t][2])
        else:
            t = order_small[pos_w]
            grads.append(small_g[t])
            deltas.append(sm_delta[t])
            new_m.append(sm_m[t])
            new_v.append(sm_v[t])
    return (loss, grad_x[None], *grads, *deltas, *new_m, *new_v)
```

```python
import math

import jax
import jax.numpy as jnp
from jax import lax
from jax.experimental import pallas as pl
from jax.experimental.pallas import tpu as pltpu

F32 = jnp.float32
BF16 = jnp.bfloat16
MESH = pl.DeviceIdType.MESH

LN_EPS = 1e-5
HEAD_DIM = 64
ATT_BLOCK = 128
Q_PER_KV = 8
A_KERNEL = 31
CONV_HALO = 32
FFN_HALO = 8
B_CHUNK = 128
LANES = 128
MXU_WIDTH = 256
GELU_C = math.sqrt(2.0 / math.pi)
ADAM_LR = 0.001
ADAM_B1 = 0.9
ADAM_B2 = 0.999
ADAM_EPS = 1e-08
ADAM_WD = 0.01
ADAM_STEP = 10
VMEM_LIMIT = 56 * 1024 * 1024


def _cp(*dims):
    return pltpu.CompilerParams(dimension_semantics=dims, vmem_limit_bytes=VMEM_LIMIT)


def _pick(n, prefs):
    for p in prefs:
        if n % p == 0:
            return p
    return n


def _sig(x):
    return 1.0 / (1.0 + jnp.exp(-x))


def _gelu(x):
    t = jnp.tanh(GELU_C * (x + 0.044715 * (x * x * x)))
    return x * (0.5 * (1.0 + t)), t


def _gelu_grad(x, t):
    return 0.5 * (1.0 + t) + 0.5 * x * (1.0 - t * t) * (GELU_C * (1.0 + 3.0 * 0.044715 * x * x))


def _ln_stats(z):
    mu = jnp.mean(z, axis=-1, keepdims=True)
    zc = z - mu
    var = jnp.mean(zc * zc, axis=-1, keepdims=True)
    rstd = lax.rsqrt(var + LN_EPS)
    return zc * rstd, rstd


def _ln_bwd(dxh, xh, rstd):
    return rstd * (dxh - jnp.mean(dxh, axis=-1, keepdims=True) - xh * jnp.mean(dxh * xh, axis=-1, keepdims=True))


def _rowsum(a):
    return jnp.sum(a, axis=0, keepdims=True)


def _lshape(a):
    return (a.shape[0], a.shape[1]) if a.ndim == 2 else (a.shape[1], a.shape[0] * a.shape[2])


def _spec2(arr, blk_r, blk_c, ridx, cidx):
    if len(arr.shape) == 2:
        return pl.BlockSpec((blk_r, blk_c), lambda i, j, k: (ridx(i, j, k), cidx(i, j, k)))
    per = arr.shape[2] // blk_c
    assert arr.shape[2] % blk_c == 0
    return pl.BlockSpec((None, blk_r, blk_c), lambda i, j, k: (cidx(i, j, k) // per, ridx(i, j, k), cidx(i, j, k) % per))


def _matmul(a, b, *, name, ta=False, tb=False, tm, tn, tk, out_dtype=F32, out_stack=None, bias=None, addend=None):
    ar, ac = _lshape(a)
    br, bc = _lshape(b)
    m, kdim = (ac, ar) if ta else (ar, ac)
    n = br if tb else bc
    assert (bc if tb else br) == kdim
    tm, tn, tk = min(tm, m), min(tn, n), min(tk, kdim)
    assert m % tm == 0 and n % tn == 0 and kdim % tk == 0, (name, m, n, kdim, tm, tn, tk)
    nk = kdim // tk
    gi, gj, gk = (lambda i, j, k: i), (lambda i, j, k: j), (lambda i, j, k: k)
    a_spec = _spec2(a, tk, tm, gk, gi) if ta else _spec2(a, tm, tk, gi, gk)
    b_spec = _spec2(b, tn, tk, gj, gk) if tb else _spec2(b, tk, tn, gk, gj)
    if out_stack is None:
        out_sds = jax.ShapeDtypeStruct((m, n), out_dtype)
    else:
        out_sds = jax.ShapeDtypeStruct((out_stack, m, n // out_stack), out_dtype)
    o_spec = _spec2(out_sds, tm, tn, gi, gj)
    in_specs = [a_spec, b_spec]
    args = [a, b]
    if bias is not None:
        in_specs.append(pl.BlockSpec((1, tn), lambda i, j, k: (0, j)))
        args.append(bias)
    scale = None
    if addend is not None:
        add_arr, scale = addend
        in_specs.append(pl.BlockSpec((tm, tn), lambda i, j, k: (i, j)))
        args.append(add_arr)
    use_acc = nk > 1 and out_dtype != F32
    dn = (((0 if ta else 1,), (1 if tb else 0,)), ((), ()))

    def body(*refs):
        a_ref, b_ref = refs[0], refs[1]
        pos = 2
        bias_ref = add_ref = None
        if bias is not None:
            bias_ref = refs[pos]
            pos += 1
        if addend is not None:
            add_ref = refs[pos]
            pos += 1
        o_ref = refs[pos]
        acc_ref = refs[pos + 1] if use_acc else o_ref
        p = lax.dot_general(a_ref[...].astype(BF16), b_ref[...].astype(BF16), dn, preferred_element_type=F32)

        def finish(val):
            if bias_ref is not None:
                val = val + bias_ref[...]
            if add_ref is not None:
                val = val + scale * add_ref[...]
            return val.astype(out_dtype)

        if nk == 1:
            o_ref[...] = finish(p)
        else:
            k = pl.program_id(2)

            @pl.when(k == 0)
            def _():
                acc_ref[...] = p

            @pl.when(k > 0)
            def _():
                acc_ref[...] += p

            if use_acc or bias_ref is not None or add_ref is not None:
                @pl.when(k == nk - 1)
                def _():
                    o_ref[...] = finish(acc_ref[...])

    return pl.pallas_call(
        body, name=name, grid=(m // tm, n // tn, nk), in_specs=in_specs, out_specs=o_spec, out_shape=out_sds,
        scratch_shapes=[pltpu.VMEM((tm, tn), F32)] if use_acc else [],
        compiler_params=_cp("parallel", "parallel", "arbitrary"),
    )(*args)


def _add_ln_fwd(x, s, g, b, alpha, *, name):
    rows, d = x.shape
    t = _pick(rows, (512, 256))

    def body(x_ref, s_ref, g_ref, b_ref, y_ref):
        xh, _ = _ln_stats(alpha * x_ref[...] + s_ref[...])
        y_ref[...] = xh * g_ref[...] + b_ref[...]

    row = pl.BlockSpec((t, d), lambda i: (i, 0))
    vec = pl.BlockSpec((1, d), lambda i: (0, 0))
    return pl.pallas_call(body, name=name, grid=(rows // t,), in_specs=[row, row, vec, vec], out_specs=row,
                          out_shape=jax.ShapeDtypeStruct((rows, d), F32), compiler_params=_cp("parallel"))(x, s, g, b)


def _add_ln_bwd(dy_terms, x, s, g, alpha, *, name):
    rows, d = x.shape
    t = _pick(rows, (512, 256))
    nterm = len(dy_terms)
    scales = [sc for _, sc in dy_terms]
    ranks = [a.ndim for a, _ in dy_terms]

    def body(*refs):
        dy_refs = refs[:nterm]
        x_ref, s_ref, g_ref, dz_ref, dg_ref, db_ref = refs[nterm:]

        @pl.when(pl.program_id(0) == 0)
        def _():
            dg_ref[...] = jnp.zeros_like(dg_ref)
            db_ref[...] = jnp.zeros_like(db_ref)

        dyv = None
        for r, sc, rank in zip(dy_refs, scales, ranks):
            slabs = [r[...]] if rank == 2 else [r[p] for p in range(r.shape[0])]
            for v in slabs:
                v = v if sc == 1.0 else sc * v
                dyv = v if dyv is None else dyv + v
        xh, rstd = _ln_stats(alpha * x_ref[...] + s_ref[...])
        dz_ref[...] = _ln_bwd(dyv * g_ref[...], xh, rstd)
        dg_ref[...] += _rowsum(dyv * xh)
        db_ref[...] += _rowsum(dyv)

    row = pl.BlockSpec((t, d), lambda i: (i, 0))
    vec = pl.BlockSpec((1, d), lambda i: (0, 0))
    vsds = jax.ShapeDtypeStruct((1, d), F32)
    dy_specs = [row if a.ndim == 2 else pl.BlockSpec((a.shape[0], t, d), lambda i: (0, i, 0)) for a, _ in dy_terms]
    return pl.pallas_call(body, name=name, grid=(rows // t,), in_specs=dy_specs + [row, row, vec], out_specs=[row, vec, vec],
                          out_shape=[jax.ShapeDtypeStruct((rows, d), F32), vsds, vsds],
                          compiler_params=_cp("arbitrary"))(*[a for a, _ in dy_terms], x, s, g)


def _loss_and_grad(y, tgt, *, name):
    rows, d = y.shape
    t = _pick(rows, (512, 256))

    def body(y_ref, t_ref, l_ref, dy_ref):
        @pl.when(pl.program_id(0) == 0)
        def _():
            l_ref[...] = jnp.zeros_like(l_ref)

        e = y_ref[...] - t_ref[...]
        l_ref[...] += _rowsum(e * e)
        dy_ref[...] = e * (1.0 / d)

    row = pl.BlockSpec((t, d), lambda i: (i, 0))
    vec = pl.BlockSpec((1, d), lambda i: (0, 0))
    return pl.pallas_call(body, name=name, grid=(rows // t,), in_specs=[row, row], out_specs=[vec, row],
                          out_shape=[jax.ShapeDtypeStruct((1, d), F32), jax.ShapeDtypeStruct((rows, d), F32)],
                          compiler_params=_cp("arbitrary"))(y, tgt)


def _col_blocks(width):
    out, pos = [], 0
    while pos < width:
        w = MXU_WIDTH if width - pos >= MXU_WIDTH else width - pos
        out.append(slice(pos, pos + w))
        pos += w
    return out


def _conv3(e, w, b):
    r1 = pltpu.roll(e, 1, 0)
    r2 = pltpu.roll(e, 2, 0)
    return w[0:1, :] * r2 + w[1:2, :] * r1 + w[2:3, :] * e + b, (r2, r1, e)


def _ffn_up_fwd(x, w_up, cw, cb, *, name):
    rows, d = x.shape
    nq, _, tc = w_up.shape
    nj = nq // 2
    f = tc * nj
    tm = _pick(rows, (512, 256))
    blocks = _col_blocks(tc)

    def body(x_ref, wg_ref, wv_ref, cw_ref, cb_ref, hf_ref, f_ref, prev_ref):
        @pl.when(pl.program_id(1) == 0)
        def _():
            prev_ref[...] = jnp.zeros_like(prev_ref)

        xb = x_ref[...].astype(BF16)
        for cs in blocks:
            hc = []
            for s, w_ref in ((0, wg_ref), (1, wv_ref)):
                h = jnp.dot(xb, w_ref[:, cs], preferred_element_type=F32)
                hf_ref[s, :, cs] = h
                e = jnp.concatenate([prev_ref[s, :, cs], h], axis=0)
                prev_ref[s, :, cs] = h[tm - FFN_HALO:]
                y, _ = _conv3(e, cw_ref[s, :, cs], cb_ref[s, :, cs])
                hc.append(y[FFN_HALO:])
            gl, _ = _gelu(hc[0])
            f_ref[:, cs] = (gl * hc[1]).astype(BF16)

    in_specs = [
        pl.BlockSpec((tm, d), lambda j, i: (i, 0)),
        pl.BlockSpec((None, d, tc), lambda j, i: (j, 0, 0)),
        pl.BlockSpec((None, d, tc), lambda j, i: (nj + j, 0, 0)),
        pl.BlockSpec((2, 3, tc), lambda j, i: (0, 0, j)),
        pl.BlockSpec((2, 1, tc), lambda j, i: (0, 0, j)),
    ]
    out_specs = [pl.BlockSpec((2, tm, tc), lambda j, i: (0, i, j)), pl.BlockSpec((tm, tc), lambda j, i: (i, j))]
    out_shape = [jax.ShapeDtypeStruct((2, rows, f), F32), jax.ShapeDtypeStruct((rows, f), BF16)]
    return pl.pallas_call(body, name=name, grid=(nj, rows // tm), in_specs=in_specs, out_specs=out_specs, out_shape=out_shape,
                          scratch_shapes=[pltpu.VMEM((2, FFN_HALO, tc), F32)],
                          compiler_params=_cp("parallel", "arbitrary"))(x, w_up, w_up, cw, cb)


def _ffn_up_bwd(hf, df, x, w_up, cw, cb, *, name):
    _, rows, f = hf.shape
    d = x.shape[1]
    nq, _, tc = w_up.shape
    nj = nq // 2
    tm = _pick(rows, (512, 256))
    hb = tm // FFN_HALO
    once = pl.Buffered(1)
    ni = rows // tm
    last_blk = rows // FFN_HALO - 1
    ext = tm + 2 * FFN_HALO
    tile = slice(FFN_HALO, FFN_HALO + tm)
    blocks = _col_blocks(tc)

    def body(h_ref, hp_ref, hn_ref, d_ref, dn_ref, x_ref, wg_ref, wv_ref, cw_ref, cb_ref, dx_ref, dw_out_ref, dcw_ref, dcb_ref,
             dw_ref):
        i = pl.program_id(1)
        first = i == 0
        last = i == ni - 1

        @pl.when(first)
        def _():
            dw_ref[...] = jnp.zeros_like(dw_ref)
            dcw_ref[...] = jnp.zeros_like(dcw_ref)
            dcb_ref[...] = jnp.zeros_like(dcb_ref)

        xt = x_ref[...].astype(BF16).T
        dx = None
        for cs in blocks:
            wc = cs.stop - cs.start
            de = jnp.concatenate([jnp.zeros((FFN_HALO, wc), F32), d_ref[:, cs], jnp.where(last, 0.0, dn_ref[:, cs])], axis=0)
            taps, hc = [], []
            for s in range(2):
                e = jnp.concatenate([jnp.where(first, 0.0, hp_ref[s, :, cs]), h_ref[s, :, cs], hn_ref[s, :, cs]], axis=0)
                y, tp = _conv3(e, cw_ref[s, :, cs], cb_ref[s, :, cs])
                hc.append(y)
                taps.append(tp)
            gl, th = _gelu(hc[0])
            dhc = (de * hc[1] * _gelu_grad(hc[0], th), de * gl)
            for s, w_ref in ((0, wg_ref), (1, wv_ref)):
                w = cw_ref[s, :, cs]
                g = dhc[s]
                dh = (w[2:3, :] * g + w[1:2, :] * pltpu.roll(g, ext - 1, 0) + w[0:1, :] * pltpu.roll(g, ext - 2, 0))[tile]
                gt = g[tile]
                for k in range(3):
                    dcw_ref[s, k:k + 1, cs] += _rowsum(gt * taps[s][k][tile])
                dcb_ref[s, :, cs] += _rowsum(gt)
                dhb = dh.astype(BF16)
                part = lax.dot_general(dhb, w_ref[:, cs], (((1,), (1,)), ((), ())), preferred_element_type=F32)
                dx = part if dx is None else dx + part
                dw_ref[s, :, cs] += jnp.dot(xt, dhb, preferred_element_type=F32)
        dx_ref[...] = dx

        @pl.when(last)
        def _():
            dw_out_ref[...] = dw_ref[...].astype(BF16)

    in_specs = [
        pl.BlockSpec((2, tm, tc), lambda j, i: (0, i, j)),
        pl.BlockSpec((2, FFN_HALO, tc), lambda j, i: (0, jnp.maximum(i * hb - 1, 0), j)),
        pl.BlockSpec((2, FFN_HALO, tc), lambda j, i: (0, jnp.minimum((i + 1) * hb, last_blk), j)),
        pl.BlockSpec((tm, tc), lambda j, i: (i, j)),
        pl.BlockSpec((FFN_HALO, tc), lambda j, i: (jnp.minimum((i + 1) * hb, last_blk), j)),
        pl.BlockSpec((tm, d), lambda j, i: (i, 0)),
        pl.BlockSpec((None, d, tc), lambda j, i: (j, 0, 0), pipeline_mode=once),
        pl.BlockSpec((None, d, tc), lambda j, i: (nj + j, 0, 0), pipeline_mode=once),
        pl.BlockSpec((2, 3, tc), lambda j, i: (0, 0, j)),
        pl.BlockSpec((2, 1, tc), lambda j, i: (0, 0, j)),
    ]
    out_specs = [
        pl.BlockSpec((None, tm, d), lambda j, i: (j, i, 0)),
        pl.BlockSpec((2, None, d, tc), lambda j, i: (0, j, 0, 0), pipeline_mode=once),
        pl.BlockSpec((2, 3, tc), lambda j, i: (0, 0, j)),
        pl.BlockSpec((2, 1, tc), lambda j, i: (0, 0, j)),
    ]
    out_shape = [jax.ShapeDtypeStruct((nj, rows, d), F32), jax.ShapeDtypeStruct((2, nj, d, tc), BF16),
                 jax.ShapeDtypeStruct((2, 3, f), F32), jax.ShapeDtypeStruct((2, 1, f), F32)]
    dx, dw, dcw, dcb = pl.pallas_call(body, name=name, grid=(nj, ni), in_specs=in_specs, out_specs=out_specs,
                                      out_shape=out_shape, scratch_shapes=[pltpu.VMEM((2, d, tc), F32)],
                                      compiler_params=_cp("parallel", "arbitrary"))(
        hf, hf, hf, df, df, x, w_up, w_up, cw, cb)
    return dx, dw.reshape(nq, d, tc), dcw, dcb


def _mixer_fwd(h0, cw, cb, ga, ba, gb, bb, ws, sbb, *, name):
    _, rows, w = h0.shape
    t = _pick(rows, (256,))
    hb = t // CONV_HALO
    groups = w // B_CHUNK

    def body(h_ref, hp_ref, cw_ref, cb_ref, ga_ref, ba_ref, gb_ref, bb_ref, ws_ref, sb_ref, o_ref, a2_ref):
        first = pl.program_id(0) == 0
        a1 = h_ref[0] * _sig(h_ref[1])
        a1p = jnp.where(first, 0.0, hp_ref[0] * _sig(hp_ref[1]))
        e = jnp.concatenate([a1p, a1], axis=0)
        acc = cw_ref[A_KERNEL - 1:A_KERNEL, :] * e
        for k in range(A_KERNEL - 1):
            acc = acc + cw_ref[k:k + 1, :] * pltpu.roll(e, A_KERNEL - 1 - k, 0)
        a2 = acc[CONV_HALO:] + cb_ref[...]
        a2_ref[...] = a2
        xh, _ = _ln_stats(a2)
        a3 = xh * ga_ref[...] + ba_ref[...]
        o_ref[:, 0:w] = (a3 * _sig(a3)).astype(BF16)

        u, _ = _gelu(h_ref[2])
        v1, _ = _gelu(h_ref[3])
        xh2, _ = _ln_stats(v1)
        v2 = (xh2 * gb_ref[...] + bb_ref[...]).astype(BF16)
        for c in range(t // B_CHUNK):
            rs = slice(c * B_CHUNK, (c + 1) * B_CHUNK)
            for g in range(groups):
                cs = slice(g * B_CHUNK, (g + 1) * B_CHUNK)
                mixed = jnp.dot(ws_ref[g], v2[rs, cs], preferred_element_type=F32) + sb_ref[g]
                o_ref[rs, w + g * B_CHUNK:w + (g + 1) * B_CHUNK] = (u[rs, cs] * mixed).astype(BF16)

    vec = pl.BlockSpec((1, w), lambda i: (0, 0))
    grp = pl.BlockSpec((groups, B_CHUNK, B_CHUNK), lambda i: (0, 0, 0))
    in_specs = [
        pl.BlockSpec((4, t, w), lambda i: (0, i, 0)),
        pl.BlockSpec((2, CONV_HALO, w), lambda i: (0, jnp.maximum(i * hb - 1, 0), 0)),
        pl.BlockSpec((A_KERNEL, w), lambda i: (0, 0)),
        vec, vec, vec, vec, vec, grp, grp,
    ]
    out_specs = [pl.BlockSpec((t, 2 * w), lambda i: (i, 0)), pl.BlockSpec((t, w), lambda i: (i, 0))]
    out_shape = [jax.ShapeDtypeStruct((rows, 2 * w), BF16), jax.ShapeDtypeStruct((rows, w), F32)]
    return pl.pallas_call(body, name=name, grid=(rows // t,), in_specs=in_specs, out_specs=out_specs, out_shape=out_shape,
                          compiler_params=_cp("parallel"))(h0, h0, cw, cb, ga, ba, gb, bb, ws, sbb)


def _mixer_bwd(h0, a2, dab, x, w_in, res, res_scale, cw, ga, ba, gb, bb, ws, wst, sbb, tril, *, name):
    _, rows, w = h0.shape
    d = x.shape[1]
    once = pl.Buffered(1)
    t = _pick(rows, (256,))
    hb = t // CONV_HALO
    ni = rows // t
    last_blk = rows // CONV_HALO - 1
    ext = t + CONV_HALO
    tile = slice(0, t)
    groups = w // B_CHUNK
    taps = A_KERNEL - 1

    def body(h_ref, a2_ref, a2n_ref, d_ref, dn_ref, x_ref, win_ref, res_ref, cw_ref, ga_ref, ba_ref, gb_ref, bb_ref,
             ws_ref, wst_ref, sb_ref, tril_ref, dx_ref, dwin_ref, dcw_ref, dcb_ref, dga_ref, dba_ref, dgb_ref, dbb_ref,
             dws_ref, dsb_ref, dw_ref):
        i = pl.program_id(0)
        first = i == 0
        last = i == ni - 1

        @pl.when(first)
        def _():
            for r in (dw_ref, dcw_ref, dcb_ref, dga_ref, dba_ref, dgb_ref, dbb_ref, dws_ref, dsb_ref):
                r[...] = jnp.zeros_like(r)

        xt = x_ref[...].astype(BF16).T
        dx_terms = []

        def through_w_in(slot, dh):
            dhb = dh.astype(BF16)
            dx_terms.append(lax.dot_general(dhb, win_ref[slot], (((1,), (1,)), ((), ())), preferred_element_type=F32))
            dw_ref[slot] += jnp.dot(xt, dhb, preferred_element_type=F32)

        xh, rstd = _ln_stats(jnp.concatenate([a2_ref[...], a2n_ref[...]], axis=0))
        a3 = xh * ga_ref[...] + ba_ref[...]
        s3 = _sig(a3)
        da_e = jnp.concatenate([d_ref[:, 0:w], jnp.where(last, 0.0, dn_ref[...])], axis=0)
        da3 = da_e * (s3 * (1.0 + a3 * (1.0 - s3)))
        da2 = _ln_bwd(da3 * ga_ref[...], xh, rstd)
        dga_ref[...] += _rowsum(da3[tile] * xh[tile])
        dba_ref[...] += _rowsum(da3[tile])
        dcb_ref[...] += _rowsum(da2[tile])
        sgt = _sig(h_ref[1])
        a1t = h_ref[0] * sgt
        da1t = None
        for k in range(A_KERNEL):
            sh = taps - k
            fed = (da2 if sh == 0 else pltpu.roll(da2, ext - sh, 0))[tile]
            dcw_ref[k:k + 1, :] += _rowsum(a1t * fed)
            term = cw_ref[k:k + 1, :] * fed
            da1t = term if da1t is None else da1t + term
        through_w_in(0, da1t * sgt)
        through_w_in(1, da1t * h_ref[0] * sgt * (1.0 - sgt))

        bu = h_ref[2]
        bv = h_ref[3]
        u, tu = _gelu(bu)
        v1, tv = _gelu(bv)
        xh2, rstd2 = _ln_stats(v1)
        v2 = (xh2 * gb_ref[...] + bb_ref[...]).astype(BF16)
        db = d_ref[:, w:2 * w]
        dmx_all = db * u
        du_parts, dv2_parts = [], []
        for c in range(t // B_CHUNK):
            rs = slice(c * B_CHUNK, (c + 1) * B_CHUNK)
            du_row, dv2_row = [], []
            for g in range(groups):
                cs = slice(g * B_CHUNK, (g + 1) * B_CHUNK)
                v2cg = v2[rs, cs]
                mixed = jnp.dot(ws_ref[g], v2cg, preferred_element_type=F32) + sb_ref[g]
                dmx = dmx_all[rs, cs]
                dmxb = dmx.astype(BF16)
                du_row.append(db[rs, cs] * mixed)
                dv2_row.append(jnp.dot(wst_ref[g], dmxb, preferred_element_type=F32))
                dws_ref[g] += tril_ref[...] * lax.dot_general(dmxb, v2cg, (((1,), (1,)), ((), ())),
                                                               preferred_element_type=F32)
                dsb_ref[g:g + 1, :] += _rowsum(dmx.T)
            du_parts.append(jnp.concatenate(du_row, axis=1))
            dv2_parts.append(jnp.concatenate(dv2_row, axis=1))
        du = jnp.concatenate(du_parts, axis=0)
        dv2 = jnp.concatenate(dv2_parts, axis=0)
        dgb_ref[...] += _rowsum(dv2 * xh2)
        dbb_ref[...] += _rowsum(dv2)
        dv1 = _ln_bwd(dv2 * gb_ref[...], xh2, rstd2)
        through_w_in(2, du * _gelu_grad(bu, tu))
        through_w_in(3, dv1 * _gelu_grad(bv, tv))
        dx_ref[...] = res_scale * res_ref[...] + ((dx_terms[0] + dx_terms[1]) + (dx_terms[2] + dx_terms[3]))

        @pl.when(last)
        def _():
            dwin_ref[...] = dw_ref[...].astype(BF16)

    vec = pl.BlockSpec((1, w), lambda i: (0, 0))
    grp = pl.BlockSpec((groups, B_CHUNK, B_CHUNK), lambda i: (0, 0, 0))
    halo = pl.BlockSpec((CONV_HALO, w), lambda i: (jnp.minimum((i + 1) * hb, last_blk), 0))
    wide = pl.BlockSpec((t, d), lambda i: (i, 0))
    in_specs = [
        pl.BlockSpec((4, t, w), lambda i: (0, i, 0)),
        pl.BlockSpec((t, w), lambda i: (i, 0)),
        halo,
        pl.BlockSpec((t, 2 * w), lambda i: (i, 0)),
        halo,
        wide,
        pl.BlockSpec((4, d, w), lambda i: (0, 0, 0), pipeline_mode=once),
        wide,
        pl.BlockSpec((A_KERNEL, w), lambda i: (0, 0)),
        vec, vec, vec, vec, grp, grp, grp,
        pl.BlockSpec((B_CHUNK, B_CHUNK), lambda i: (0, 0)),
    ]
    vsds = jax.ShapeDtypeStruct((1, w), F32)
    out_specs = [
        wide,
        pl.BlockSpec((4, d, w), lambda i: (0, 0, 0), pipeline_mode=once),
        pl.BlockSpec((A_KERNEL, w), lambda i: (0, 0)),
        vec, vec, vec, vec, vec, grp,
        pl.BlockSpec((groups, B_CHUNK), lambda i: (0, 0)),
    ]
    out_shape = [jax.ShapeDtypeStruct((rows, d), F32), jax.ShapeDtypeStruct((4, d, w), BF16),
                 jax.ShapeDtypeStruct((A_KERNEL, w), F32),
                 vsds, vsds, vsds, vsds, vsds, jax.ShapeDtypeStruct((groups, B_CHUNK, B_CHUNK), F32),
                 jax.ShapeDtypeStruct((groups, B_CHUNK), F32)]
    return pl.pallas_call(body, name=name, grid=(ni,), in_specs=in_specs, out_specs=out_specs, out_shape=out_shape,
                          scratch_shapes=[pltpu.VMEM((4, d, w), F32)], compiler_params=_cp("arbitrary"))(
        h0, a2, a2, dab, dab, x, w_in, res, cw, ga, ba, gb, bb, ws, wst, sbb, tril)


GROUP_ROWS = Q_PER_KV * ATT_BLOCK


def _attn_mask(n):
    qi = lax.broadcasted_iota(jnp.int32, (GROUP_ROWS, 2 * ATT_BLOCK), 0) & (ATT_BLOCK - 1)
    sj = lax.broadcasted_iota(jnp.int32, (GROUP_ROWS, 2 * ATT_BLOCK), 1)
    diff = qi + ATT_BLOCK - sj
    return (diff >= 0) & (diff < ATT_BLOCK) & ((n > 0) | (sj >= ATT_BLOCK))


def _stack_heads(ref, kvh, dtype):
    heads = [ref[:, (kvh * Q_PER_KV + g) * HEAD_DIM:(kvh * Q_PER_KV + g + 1) * HEAD_DIM] for g in range(Q_PER_KV)]
    return jnp.concatenate(heads, axis=0).astype(dtype)


def _per_row_sink(sink_ref, kvh):
    head = lax.broadcasted_iota(jnp.int32, (GROUP_ROWS, 1), 0) // ATT_BLOCK
    out = jnp.zeros((GROUP_ROWS, 1), F32)
    for g in range(Q_PER_KV):
        out = jnp.where(head == g, sink_ref[kvh * Q_PER_KV + g], out)
    return out


def _attn_specs(rows, n_q):
    dq = n_q * HEAD_DIM
    dkv = 2 * (n_q // Q_PER_KV) * HEAD_DIM
    kv_blk = dq // dkv
    assert dq % dkv == 0
    return dq, dkv, [
        pl.BlockSpec(memory_space=pltpu.SMEM),
        pl.BlockSpec((ATT_BLOCK, dq), lambda n: (n, 0)),
        pl.BlockSpec((ATT_BLOCK, dkv), lambda n: (n, kv_blk)),
        pl.BlockSpec((ATT_BLOCK, dkv), lambda n: (jnp.maximum(n - 1, 0), kv_blk)),
    ]


def _kv_pair(kvc_ref, kvp_ref, kvh, n_kv):
    ks = slice(kvh * HEAD_DIM, (kvh + 1) * HEAD_DIM)
    vs = slice((n_kv + kvh) * HEAD_DIM, (n_kv + kvh + 1) * HEAD_DIM)
    kk = jnp.concatenate([kvp_ref[:, ks], kvc_ref[:, ks]], axis=0).astype(BF16)
    vv = jnp.concatenate([kvp_ref[:, vs], kvc_ref[:, vs]], axis=0).astype(BF16)
    return kk, vv


def _attn_fwd(qkv, sinks, *, name):
    rows = qkv.shape[0]
    n_q = sinks.shape[0]
    n_kv = n_q // Q_PER_KV
    scale = 1.0 / math.sqrt(HEAD_DIM)
    dq, _, in_specs = _attn_specs(rows, n_q)

    def body(sink_ref, q_ref, kvc_ref, kvp_ref, o_ref, lse_ref):
        valid = _attn_mask(pl.program_id(0))
        for kvh in range(n_kv):
            kk, vv = _kv_pair(kvc_ref, kvp_ref, kvh, n_kv)
            qs = _stack_heads(q_ref, kvh, BF16)
            s = lax.dot_general(qs, kk, (((1,), (1,)), ((), ())), preferred_element_type=F32)
            s = jnp.where(valid, s * scale, -jnp.inf)
            sk = _per_row_sink(sink_ref, kvh)
            m = jnp.maximum(jnp.max(s, axis=1, keepdims=True), sk)
            p = jnp.exp(s - m)
            l = jnp.sum(p, axis=1, keepdims=True) + jnp.exp(sk - m)
            o = jnp.dot((p / l).astype(BF16), vv, preferred_element_type=F32)
            lse = m + jnp.log(l)
            for g in range(Q_PER_KV):
                h = kvh * Q_PER_KV + g
                rs = slice(g * ATT_BLOCK, (g + 1) * ATT_BLOCK)
                o_ref[:, h * HEAD_DIM:(h + 1) * HEAD_DIM] = o[rs]
                lse_ref[:, h:h + 1] = lse[rs]

    out_specs = [pl.BlockSpec((ATT_BLOCK, dq), lambda n: (n, 0)), pl.BlockSpec((ATT_BLOCK, n_q), lambda n: (n, 0))]
    out_shape = [jax.ShapeDtypeStruct((rows, dq), F32), jax.ShapeDtypeStruct((rows, n_q), F32)]
    return pl.pallas_call(body, name=name, grid=(rows // ATT_BLOCK,), in_specs=in_specs, out_specs=out_specs,
                          out_shape=out_shape, compiler_params=_cp("parallel"))(sinks, qkv, qkv, qkv)


def _attn_bwd(qkv, dout, lse, sinks, *, name):
    rows = qkv.shape[0]
    n_q = sinks.shape[0]
    n_kv = n_q // Q_PER_KV
    scale = 1.0 / math.sqrt(HEAD_DIM)
    dq_w, dkv_w, in_specs = _attn_specs(rows, n_q)
    blk_q = pl.BlockSpec((ATT_BLOCK, dq_w), lambda n: (n, 0))
    blk_kv = pl.BlockSpec((ATT_BLOCK, dkv_w), lambda n: (n, 0))
    in_specs = in_specs + [blk_q, pl.BlockSpec((ATT_BLOCK, n_q), lambda n: (n, 0))]

    def body(sink_ref, q_ref, kvc_ref, kvp_ref, do_ref, lse_ref, dq_ref, dkc_ref, dkp_ref, dsink_ref):
        n = pl.program_id(0)

        @pl.when(n == 0)
        def _():
            dsink_ref[...] = jnp.zeros_like(dsink_ref)

        valid = _attn_mask(n)
        head_ids = lax.broadcasted_iota(jnp.int32, (1, n_q), 1)
        dsink = jnp.zeros((1, n_q), F32)
        for kvh in range(n_kv):
            kk, vv = _kv_pair(kvc_ref, kvp_ref, kvh, n_kv)
            qs = _stack_heads(q_ref, kvh, BF16)
            dos = _stack_heads(do_ref, kvh, BF16)
            lse = jnp.concatenate([lse_ref[:, kvh * Q_PER_KV + g:kvh * Q_PER_KV + g + 1] for g in range(Q_PER_KV)], axis=0)
            s = lax.dot_general(qs, kk, (((1,), (1,)), ((), ())), preferred_element_type=F32)
            s = jnp.where(valid, s * scale, -jnp.inf)
            p = jnp.exp(s - lse)
            dp = lax.dot_general(dos, vv, (((1,), (1,)), ((), ())), preferred_element_type=F32)
            delta = jnp.sum(p * dp, axis=1, keepdims=True)
            ds = (p * (dp - delta) * scale).astype(BF16)
            sink_term = jnp.exp(_per_row_sink(sink_ref, kvh) - lse) * delta
            dqs = jnp.dot(ds, kk, preferred_element_type=F32)
            for g in range(Q_PER_KV):
                h = kvh * Q_PER_KV + g
                rs = slice(g * ATT_BLOCK, (g + 1) * ATT_BLOCK)
                dsink = dsink + jnp.where(head_ids == h, -jnp.sum(sink_term[rs]), 0.0)
                dq_ref[:, h * HEAD_DIM:(h + 1) * HEAD_DIM] = dqs[rs]
            dk = lax.dot_general(ds, qs, (((0,), (0,)), ((), ())), preferred_element_type=F32)
            dv = lax.dot_general(p.astype(BF16), dos, (((0,), (0,)), ((), ())), preferred_element_type=F32)
            ks = slice(kvh * HEAD_DIM, (kvh + 1) * HEAD_DIM)
            vs = slice((n_kv + kvh) * HEAD_DIM, (n_kv + kvh + 1) * HEAD_DIM)
            dkp_ref[:, ks] = dk[0:ATT_BLOCK]
            dkc_ref[:, ks] = dk[ATT_BLOCK:]
            dkp_ref[:, vs] = dv[0:ATT_BLOCK]
            dkc_ref[:, vs] = dv[ATT_BLOCK:]
        dsink_ref[...] += dsink

    out_specs = [blk_q, blk_kv, blk_kv, pl.BlockSpec((1, n_q), lambda n: (0, 0))]
    out_shape = [jax.ShapeDtypeStruct((rows, dq_w), F32), jax.ShapeDtypeStruct((rows, dkv_w), F32),
                 jax.ShapeDtypeStruct((rows, dkv_w), F32), jax.ShapeDtypeStruct((1, n_q), F32)]
    return pl.pallas_call(body, name=name, grid=(rows // ATT_BLOCK,), in_specs=in_specs, out_specs=out_specs,
                          out_shape=out_shape, compiler_params=_cp("arbitrary"))(sinks, qkv, qkv, qkv, dout, lse)


def _dqkv_assemble(dq, dkc, dkp, *, name):
    rows, dq_w = dq.shape
    dkv_w = dkc.shape[1]
    nb = rows // ATT_BLOCK

    def body(dq_ref, dkc_ref, dkp_ref, o_ref, db_ref):
        n = pl.program_id(0)

        @pl.when(n == 0)
        def _():
            db_ref[...] = jnp.zeros_like(db_ref)

        dqv = dq_ref[...]
        dkv = dkc_ref[...] + jnp.where(n == nb - 1, 0.0, dkp_ref[...])
        o_ref[:, 0:dq_w] = dqv.astype(BF16)
        o_ref[:, dq_w:dq_w + dkv_w] = dkv.astype(BF16)
        db_ref[:, 0:dq_w] += _rowsum(dqv)
        db_ref[:, dq_w:dq_w + dkv_w] += _rowsum(dkv)

    width = dq_w + dkv_w
    in_specs = [pl.BlockSpec((ATT_BLOCK, dq_w), lambda n: (n, 0)), pl.BlockSpec((ATT_BLOCK, dkv_w), lambda n: (n, 0)),
                pl.BlockSpec((ATT_BLOCK, dkv_w), lambda n: (jnp.minimum(n + 1, nb - 1), 0))]
    out_specs = [pl.BlockSpec((ATT_BLOCK, width), lambda n: (n, 0)), pl.BlockSpec((1, width), lambda n: (0, 0))]
    out_shape = [jax.ShapeDtypeStruct((rows, width), BF16), jax.ShapeDtypeStruct((1, width), F32)]
    return pl.pallas_call(body, name=name, grid=(nb,), in_specs=in_specs, out_specs=out_specs, out_shape=out_shape,
                          compiler_params=_cp("arbitrary"))(dq, dkc, dkp)


def _row_tile(r, c):
    budget = 2 * 1024 * 1024 // (4 * c)
    for cand in (1024, 512, 256, 128, 64, 32, 16):
        if cand <= budget and r % cand == 0:
            return cand
    return r


def _octo_sum(own, recv, place, dest, lead, *, name):
    _, _, r, c = own.shape
    t = _row_tile(r, c)
    lead_idx, buf_shape = lead

    def body(place_ref, own_ref, *rest):
        o_ref = rest[7] if dest is None else rest[8]
        acc = own_ref[...].astype(F32)
        for k in range(7):
            acc = acc + rest[k][...].astype(F32)
        o_ref[...] = acc

    def peer(mask):
        return pl.BlockSpec((None, t, c), lambda i, pr: (pr[2] ^ mask, i, 0))

    if lead_idx is None:
        o_spec = pl.BlockSpec((None, t, c), lambda i, pr: (pr[1], i, 0))
    else:
        o_spec = pl.BlockSpec((None, None, t, c), lambda i, pr: (lead_idx, pr[1], i, 0))
    in_specs = [pl.BlockSpec((None, None, t, c), lambda i, pr: (pr[0], pr[1], i, 0))] + [peer(m) for m in range(1, 8)]
    args = [place, own] + [recv] * 7
    aliases = {}
    if dest is not None:
        in_specs.append(HBM)
        args.append(dest)
        aliases = {9: 0}
    grid_spec = pltpu.PrefetchScalarGridSpec(num_scalar_prefetch=1, grid=(r // t,), in_specs=in_specs, out_specs=o_spec)
    return pl.pallas_call(body, name=name, grid_spec=grid_spec, out_shape=jax.ShapeDtypeStruct(buf_shape, F32),
                          input_output_aliases=aliases, compiler_params=_cp("parallel"))(*args)


def _adamw_math(w, g, m, v):
    nm = ADAM_B1 * m + (1.0 - ADAM_B1) * g
    nv = ADAM_B2 * v + (1.0 - ADAM_B2) * (g * g)
    m_hat = nm / (1.0 - ADAM_B1 ** ADAM_STEP)
    v_hat = nv / (1.0 - ADAM_B2 ** ADAM_STEP)
    return -ADAM_LR * (m_hat / (jnp.sqrt(v_hat) + ADAM_EPS) + ADAM_WD * w), nm, nv


def _adamw(w, g, m, v, *, name):
    r, c = w.shape
    t = _row_tile(r, c)

    def body(w_ref, g_ref, m_ref, v_ref, d_ref, nm_ref, nv_ref, go_ref):
        gv = g_ref[...]
        d_ref[...], nm_ref[...], nv_ref[...] = _adamw_math(w_ref[...], gv, m_ref[...], v_ref[...])
        go_ref[...] = gv

    blk = pl.BlockSpec((t, c), lambda i: (i, 0))
    sds = jax.ShapeDtypeStruct((r, c), F32)
    return pl.pallas_call(body, name=name, grid=(r // t,), in_specs=[blk] * 4, out_specs=[blk] * 4,
                          out_shape=[sds] * 4, compiler_params=_cp("parallel"))(w, g, m, v)


HBM = pl.BlockSpec(memory_space=pl.ANY)


def _place():
    x, y, c = lax.axis_index("x"), lax.axis_index("y"), lax.axis_index("c")
    chips = [(1 - x, y), (x, 1 - y), (1 - x, 1 - y)]
    return x, y, c, 2 * x + y, (x, y, 1 - c), chips


def _rcopy(src, dst, ssem, rsem, dev):
    return pltpu.make_async_remote_copy(src_ref=src, dst_ref=dst, send_sem=ssem, recv_sem=rsem, device_id=dev,
                                        device_id_type=MESH)


HBM_ONLY = pl.BlockSpec(memory_space=pltpu.HBM)
SEM = pl.BlockSpec(memory_space=pltpu.SEMAPHORE)


def _peers():
    x, y, c = lax.axis_index("x"), lax.axis_index("y"), lax.axis_index("c")
    out = []
    for mask in range(1, 8):
        px = 1 - x if mask & 4 else x
        py = 1 - y if mask & 2 else y
        pc = 1 - c if mask & 1 else c
        out.append(((px, py, pc), 2 * px + py, pc, 4 * px + 2 * py + pc))
    return 4 * x + 2 * y + c, out


def _reduce_start(grads, lands, *, name):
    nt = len(grads)

    def body(*refs):
        ssems, rsems = refs[2 * nt:3 * nt], refs[3 * nt:4 * nt]
        g_out, l_out, token = refs[4 * nt:5 * nt], refs[5 * nt:6 * nt], refs[6 * nt]
        me, peers = _peers()
        for t in range(nt):
            for k, (dev, chip, core, _) in enumerate(peers):
                _rcopy(g_out[t].at[chip, core], l_out[t].at[me], ssems[t].at[k], rsems[t].at[k], dev).start()
        token[...] = jnp.zeros_like(token)

    sems = [pltpu.SemaphoreType.DMA((7,))] * (2 * nt)
    out_shape = (sems + [pltpu.HBM(g.shape, g.dtype) for g in grads] + [pltpu.HBM(l.shape, l.dtype) for l in lands]
                 + [jax.ShapeDtypeStruct((8, LANES), F32)])
    res = pl.pallas_call(
        body, name=name, in_specs=[HBM_ONLY] * (2 * nt),
        out_specs=[SEM] * (2 * nt) + [HBM_ONLY] * (2 * nt) + [pl.BlockSpec(memory_space=pltpu.VMEM)], out_shape=out_shape,
        input_output_aliases={t: 2 * nt + t for t in range(2 * nt)},
        compiler_params=pltpu.CompilerParams(has_side_effects=DATAFLOW),
    )(*[pltpu.with_memory_space_constraint(a, pltpu.HBM) for a in list(grads) + list(lands)])
    return res[:nt], res[nt:2 * nt], res[2 * nt:3 * nt], res[3 * nt:4 * nt], res[4 * nt]


def _reduce_wait(grads, lands, ssems, rsems, after, *, name):
    nt = len(grads)

    def body(*refs):
        ssem_refs, rsem_refs = refs[2 * nt:3 * nt], refs[3 * nt:4 * nt]
        g_out, l_out = refs[4 * nt + 1:5 * nt + 1], refs[5 * nt + 1:6 * nt + 1]
        me, peers = _peers()
        for t in range(nt):
            for k, (dev, chip, core, _) in enumerate(peers):
                _rcopy(g_out[t].at[chip, core], l_out[t].at[me], ssem_refs[t].at[k], rsem_refs[t].at[k], dev).wait_send()
        for t in range(nt):
            for k, (dev, _, _, idx) in enumerate(peers):
                slot = l_out[t].at[idx]
                _rcopy(slot, slot, ssem_refs[t].at[k], rsem_refs[t].at[k], dev).wait_recv()

    res = pl.pallas_call(
        body, name=name, in_specs=[HBM_ONLY] * (2 * nt) + [SEM] * (2 * nt) + [HBM], out_specs=[HBM_ONLY] * (2 * nt),
        out_shape=[pltpu.HBM(a.shape, a.dtype) for a in list(grads) + list(lands)],
        input_output_aliases={t: t for t in range(2 * nt)},
        compiler_params=pltpu.CompilerParams(has_side_effects=DATAFLOW),
    )(*grads, *lands, *ssems, *rsems, after)
    return list(res[:nt]), list(res[nt:])
DATAFLOW = pltpu.SideEffectType.DATAFLOW_SIDE_EFFECTING


def _gather_start(bufs, *, name):
    nt = len(bufs)

    def body(*refs):
        ssems, rsems, outs = refs[nt:2 * nt], refs[2 * nt:3 * nt], refs[3 * nt:4 * nt]
        x, y, c, q, sib, chips = _place()
        for t in range(nt):
            for j, (px, py) in enumerate(chips):
                mine = outs[t].at[q]
                _rcopy(mine, mine, ssems[t].at[j], rsems[t].at[j], (px, py, c)).start()

    sems = [pltpu.SemaphoreType.DMA((3,))] * (2 * nt)
    out_shape = sems + [pltpu.HBM(b.shape, b.dtype) for b in bufs]
    res = pl.pallas_call(
        body, name=name, in_specs=[HBM_ONLY] * nt, out_specs=[SEM] * (2 * nt) + [HBM_ONLY] * nt, out_shape=out_shape,
        input_output_aliases={t: 2 * nt + t for t in range(nt)},
        compiler_params=pltpu.CompilerParams(has_side_effects=DATAFLOW),
    )(*[pltpu.with_memory_space_constraint(b, pltpu.HBM) for b in bufs])
    return res[:nt], res[nt:2 * nt], res[2 * nt:]


def _gather_wait(bufs, ssems, rsems, after, *, name):
    nt = len(bufs)

    def body(*refs):
        ssem_refs, rsem_refs = refs[nt:2 * nt], refs[2 * nt:3 * nt]
        outs = refs[3 * nt + 1:]
        x, y, c, q, sib, chips = _place()
        for t in range(nt):
            for j, (px, py) in enumerate(chips):
                mine = outs[t].at[q]
                _rcopy(mine, mine, ssem_refs[t].at[j], rsem_refs[t].at[j], (px, py, c)).wait_send()
        for t in range(nt):
            for j, (px, py) in enumerate(chips):
                theirs = outs[t].at[2 * px + py]
                _rcopy(theirs, theirs, ssem_refs[t].at[j], rsem_refs[t].at[j], (px, py, c)).wait_recv()

    res = pl.pallas_call(
        body, name=name, in_specs=[HBM_ONLY] * nt + [SEM] * (2 * nt) + [HBM], out_specs=[HBM_ONLY] * nt,
        out_shape=[pltpu.HBM(b.shape, b.dtype) for b in bufs], input_output_aliases={t: t for t in range(nt)},
        compiler_params=pltpu.CompilerParams(has_side_effects=DATAFLOW),
    )(*bufs, *ssems, *rsems, after)
    return list(res)


def _sibling_share(bufs, layout, *, name):
    no = len(bufs)
    nt = len(layout)

    def body(*refs):
        outs = refs[no:2 * no]
        ssem, rsem = refs[2 * no:]
        x, y, c, q, sib, chips = _place()

        def slot(t, half):
            o, lead = layout[t]
            return outs[o].at[half] if lead is None else outs[o].at[lead, half]

        sends = []
        for t in range(nt):
            cp = _rcopy(slot(t, c), slot(t, c), ssem.at[t], rsem.at[t], sib)
            cp.start()
            sends.append(cp)
        for t in range(nt):
            _rcopy(slot(t, 1 - c), slot(t, 1 - c), ssem.at[t], rsem.at[t], sib).wait_recv()
        for cp in sends:
            cp.wait_send()

    out_shape = [jax.ShapeDtypeStruct(b.shape, b.dtype) for b in bufs]
    return pl.pallas_call(
        body, name=name, in_specs=[HBM] * no, out_specs=[HBM] * no, out_shape=out_shape,
        input_output_aliases={o: o for o in range(no)},
        scratch_shapes=[pltpu.SemaphoreType.DMA((nt,)), pltpu.SemaphoreType.DMA((nt,))],
    )(*bufs)


def _small_tail(local, params, *, name):
    (cwa, cba, ga, ba, gb, bb, dws, dsb, dbq, dsk, cwf0, cbf0, cwf1, cbf1,
     g00, g01, g10, g11, b00, b01, b10, b11, err) = local
    n_local = len(local)
    kw, wa = cwa.shape
    ng = dws.shape[0]
    nqkv = dbq.shape[1]
    nsk = dsk.shape[1]
    f = cwf0.shape[2]
    dm = err.shape[1]
    row_vec = 8 * (-(-kw // 8))
    shapes = [(row_vec + 8, wa), (ng * B_CHUNK + 8, B_CHUNK), (8, nqkv), (2, 2, 8, f), (16, dm)]
    n_grp = len(shapes)
    flat_params = [a for triple in params for a in triple]
    n_par = len(params)

    def reduce_body(*refs):
        loc = refs[:n_local]
        tot = refs[n_local:n_local + n_grp]
        scr = refs[n_local + n_grp:]
        grp, from_sib, pair, gath = (scr[k * n_grp:(k + 1) * n_grp] for k in range(4))
        ssem1, rsem1, ssem2, rsem2 = scr[4 * n_grp:]
        x, y, core, q, sib, chips = _place()

        for gr in grp:
            gr[...] = jnp.zeros_like(gr)
        a, b, c, dd, e = grp
        a[0:kw, :] = loc[0][...]
        for k in range(5):
            a[row_vec + k:row_vec + k + 1, :] = loc[1 + k][...]
        for g in range(ng):
            b[g * B_CHUNK:(g + 1) * B_CHUNK, :] = loc[6][g]
        b[ng * B_CHUNK:ng * B_CHUNK + ng, :] = loc[7][...]
        c[0:1, :] = loc[8][...]
        c[1:2, 0:nsk] = loc[9][...]
        for l in range(2):
            for s in range(2):
                dd[l, s, 0:3, :] = loc[10 + 2 * l][s]
                dd[l, s, 3:4, :] = loc[11 + 2 * l][s]
        for k in range(9):
            e[k:k + 1, :] = loc[14 + k][...]

        sends = []
        for gi in range(n_grp):
            cp = _rcopy(grp[gi], from_sib[gi], ssem1.at[gi], rsem1.at[gi], sib)
            cp.start()
            sends.append(cp)
        for gi in range(n_grp):
            _rcopy(grp[gi], from_sib[gi], ssem1.at[gi], rsem1.at[gi], sib).wait_recv()
            both = grp[gi][...] + from_sib[gi][...]
            pair[gi][...] = both
            gath[gi][q] = both
            for j, (px, py) in enumerate(chips):
                cp = _rcopy(pair[gi], gath[gi].at[q], ssem2.at[gi, j], rsem2.at[gi, j], (px, py, core))
                cp.start()
                sends.append(cp)
        for gi in range(n_grp):
            for j, (px, py) in enumerate(chips):
                slot = gath[gi].at[2 * px + py]
                _rcopy(slot, slot, ssem2.at[gi, j], rsem2.at[gi, j], (px, py, core)).wait_recv()
            acc = gath[gi][0]
            for k in range(1, 4):
                acc = acc + gath[gi][k]
            tot[gi][...] = acc
        for cp in sends:
            cp.wait_send()

    vm = pl.BlockSpec(memory_space=pltpu.VMEM)
    scratch = ([pltpu.VMEM(s, F32) for s in shapes] * 3 + [pltpu.VMEM((4,) + s, F32) for s in shapes]
               + [pltpu.SemaphoreType.DMA((n_grp,)), pltpu.SemaphoreType.DMA((n_grp,)),
                  pltpu.SemaphoreType.DMA((n_grp, 3)), pltpu.SemaphoreType.DMA((n_grp, 3))])
    totals = pl.pallas_call(
        reduce_body, name=name + "_reduce", in_specs=[vm] * n_local, out_specs=[vm] * n_grp,
        out_shape=[jax.ShapeDtypeStruct(s, F32) for s in shapes], scratch_shapes=scratch,
        compiler_params=pltpu.CompilerParams(vmem_limit_bytes=VMEM_LIMIT),
    )(*local)

    def adamw_body(*refs):
        ta, tb, tc, td, te = refs[:n_grp]
        par = refs[n_grp:n_grp + 3 * n_par]
        outs = refs[n_grp + 3 * n_par:n_grp + 7 * n_par]
        loss_ref = refs[n_grp + 7 * n_par]
        q = 2 * lax.axis_index("x") + lax.axis_index("y")

        def mine(piece):
            out = piece(0)
            for k in range(1, 4):
                out = jnp.where(q == k, piece(k), out)
            return out

        def update(p, grad, index=None):
            at = (lambda r: r[...]) if index is None else (lambda r: r[index])
            w_ref, m_ref, v_ref = par[3 * p:3 * p + 3]
            g_ref, d_ref, nm_ref, nv_ref = outs[4 * p:4 * p + 4]
            delta, nm, nv = _adamw_math(at(w_ref), grad, at(m_ref), at(v_ref))
            for r, val in ((g_ref, grad), (d_ref, delta), (nm_ref, nm), (nv_ref, nv)):
                if index is None:
                    r[...] = val
                else:
                    r[index] = val

        wq = wa // 4
        update(0, mine(lambda k: ta[0:kw, k * wq:(k + 1) * wq]), (0,))
        for k in range(5):
            update(1 + k, ta[row_vec + k:row_vec + k + 1, :])
        for g in range(ng):
            update(6, tb[g * B_CHUNK:(g + 1) * B_CHUNK, :], (0, g))
        update(7, tb[ng * B_CHUNK:ng * B_CHUNK + ng, :], (0,))
        nq4 = nqkv // 4
        update(8, mine(lambda k: tc[0:1, k * nq4:(k + 1) * nq4]))
        update(9, tc[1:2, 0:nsk])
        fh = f // 2
        for l in range(2):
            update(10, mine(lambda k: td[l, k // 2, 0:3, (k % 2) * fh:(k % 2 + 1) * fh]), (l,))
            update(11, jnp.concatenate([td[l, 0, 3:4, :], td[l, 1, 3:4, :]], axis=1), (slice(l, l + 1),))
        dq4 = dm // 4
        for i in range(2):
            for j in range(2):
                for p, base in ((12, 0), (13, 4)):
                    row = base + 2 * i + j
                    update(p, mine(lambda k: te[row:row + 1, k * dq4:(k + 1) * dq4]), (i, slice(j, j + 1)))
        loss_ref[...] = (0.5 / dm) * jnp.sum(te[8:9, :], axis=1, keepdims=True)

    out_shape = []
    for w, _, _ in params:
        out_shape += [jax.ShapeDtypeStruct(w.shape, F32)] * 4
    out_shape.append(jax.ShapeDtypeStruct((1, 1), F32))
    res = pl.pallas_call(
        adamw_body, name=name + "_adamw", in_specs=[vm] * (n_grp + 3 * n_par), out_specs=[vm] * len(out_shape),
        out_shape=out_shape, compiler_params=pltpu.CompilerParams(vmem_limit_bytes=VMEM_LIMIT),
    )(*totals, *flat_params)
    return [res[4 * p:4 * p + 4] for p in range(n_par)], res[-1]


def _pack(arrays, rows_multiple):
    flat = jnp.concatenate([a.reshape(-1) for a in arrays])
    rows = -(-flat.shape[0] // LANES)
    rows = -(-rows // rows_multiple) * rows_multiple
    flat = jnp.pad(flat, (0, rows * LANES - flat.shape[0]))
    return flat.reshape(rows, LANES)


def _unshard_cols(stacked):
    moved = jnp.moveaxis(stacked, 0, -2)
    return moved.reshape(moved.shape[:-2] + (4 * stacked.shape[-1],))


def kernel(x, ab_w_in, a_conv_w, a_conv_b, a_norm_g, a_norm_b, b_norm_g, b_norm_b, b_spatial_w, b_spatial_b, ab_w_out, c_w_qkv, c_b_qkv, c_sinks, c_w_o, ffn_w_up, ffn_conv_w, ffn_conv_b, ffn_w_down, ln_g, ln_b, loss_target, m_ab_w_in, m_a_conv_w, m_a_conv_b, m_a_norm_g, m_a_norm_b, m_b_norm_g, m_b_norm_b, m_b_spatial_w, m_b_spatial_b, m_ab_w_out, m_c_w_qkv, m_c_b_qkv, m_c_sinks, m_c_w_o, m_ffn_w_up, m_ffn_conv_w, m_ffn_conv_b, m_ffn_w_down, m_ln_g, m_ln_b, v_ab_w_in, v_a_conv_w, v_a_conv_b, v_a_norm_g, v_a_norm_b, v_b_norm_g, v_b_norm_b, v_b_spatial_w, v_b_spatial_b, v_ab_w_out, v_c_w_qkv, v_c_b_qkv, v_c_sinks, v_c_w_o, v_ffn_w_up, v_ffn_conv_w, v_ffn_conv_b, v_ffn_w_down, v_ln_g, v_ln_b):
    rows, d = x.shape[1], x.shape[2]
    depth = ln_g.shape[0]
    assert depth == 2 and x.shape[0] == 1
    alpha = (2.0 * depth) ** 0.25
    f = ffn_w_down.shape[1] * 4
    n_q = c_sinks.shape[1]
    q_idx = 2 * lax.axis_index("x") + lax.axis_index("y")
    c_idx = lax.axis_index("c")
    xs, tgt = x[0], loss_target[0]

    def own_slot(part):
        buf = lax.empty((4,) + part.shape, part.dtype)
        return lax.dynamic_update_slice(buf, part[None], (q_idx, 0, 0, 0))

    def halves(wm):
        return own_slot(wm.astype(BF16).reshape((2, wm.shape[0] // 2) + wm.shape[1:]))

    small_sharded = [a_conv_w[0], c_b_qkv[0], ffn_conv_w, ln_g, ln_b]
    small_pack = _pack(small_sharded, 16)
    bufs = [halves(ab_w_in[0]), own_slot(small_pack.reshape(2, small_pack.shape[0] // 2, LANES)), halves(ab_w_out[0]),
            halves(ffn_w_up[0]), halves(ffn_w_down[0]), halves(c_w_qkv[0]), halves(c_w_o[0]),
            halves(ffn_w_up[1]), halves(ffn_w_down[1])]
    ssems, rsems, started = _gather_start(bufs, name="gather_start")

    def arrive(idx, after, tag):
        got = _gather_wait([started[i] for i in idx], [ssems[i] for i in idx], [rsems[i] for i in idx], after,
                           name=f"gather_wait_{tag}")
        return [g.reshape(4, 2 * g.shape[2], g.shape[3]) for g in got]

    w_in, small_all = arrive([0, 1], xs, "in")
    small_all = small_all.reshape(4, -1)
    sh_shapes = [s.shape for s in small_sharded]
    pieces, pos = [], 0
    for s in sh_shapes:
        n = math.prod(s)
        pieces.append(_unshard_cols(small_all[:, pos:pos + n].reshape((4,) + s)))
        pos += n
    conv_w_a, b_qkv, conv_w_f, ln_gf, ln_bf = pieces

    tril = jnp.tril(jnp.ones((B_CHUNK, B_CHUNK), F32))
    ws = (b_spatial_w[0] * tril).astype(BF16)
    wst = jnp.swapaxes(ws, 1, 2)
    sbb = jnp.broadcast_to(b_spatial_b[0][:, :, None], b_spatial_w[0].shape)
    mix_vecs = [a_conv_b, a_norm_g, a_norm_b, b_norm_g, b_norm_b]
    cw_f = [jnp.swapaxes(conv_w_f[l].reshape(3, 2, f), 0, 1) for l in range(depth)]
    cb_f = [ffn_conv_b[l].reshape(2, 1, f) for l in range(depth)]
    lng = lambda i, j: ln_gf[i, j].reshape(1, d)
    lnb = lambda i, j: ln_bf[i, j].reshape(1, d)
    sinks = c_sinks[0]

    w_up, w_down = [None, None], [None, None]

    def ffn_fwd(xin, l):
        w_up[l], = arrive([3 + 4 * l], xin, f"up{l}")
        hf, fact = _ffn_up_fwd(xin, w_up[l], cw_f[l], cb_f[l], name=f"ffn{l}_up")
        w_down[l] = arrive([4 + 4 * l], fact, f"down{l}")[0].reshape(-1, d)
        out = _matmul(fact, w_down[l], name=f"ffn{l}_down", tm=512, tn=1024, tk=2816)
        return hf, fact, out

    h0 = _matmul(xs, w_in, name="mix_in", tm=1024, tn=512, tk=1024, out_stack=4)
    ab, a2 = _mixer_fwd(h0, conv_w_a, *mix_vecs, ws, sbb, name="mix_mid")
    w_out = arrive([2], ab, "out")[0].reshape(-1, d)
    mix = _matmul(ab, w_out, name="mix_out", tm=1024, tn=1024, tk=1024)
    x1 = _add_ln_fwd(xs, mix, lng(0, 0), lnb(0, 0), alpha, name="ln00")
    hf0, f0, ffn0 = ffn_fwd(x1, 0)
    x2 = _add_ln_fwd(x1, ffn0, lng(0, 1), lnb(0, 1), alpha, name="ln01")
    w_qkv = _unshard_cols(arrive([5], x2, "qkv")[0])
    qkv = _matmul(x2, w_qkv, name="att_qkv", tm=1024, tn=w_qkv.shape[1], tk=1024, bias=b_qkv.reshape(1, -1))
    ao, lse = _attn_fwd(qkv, sinks, name="att_core")
    w_o = arrive([6], ao, "o")[0].reshape(-1, d)
    att = _matmul(ao, w_o, name="att_out", tm=1024, tn=1024, tk=1024)
    x3 = _add_ln_fwd(x2, att, lng(1, 0), lnb(1, 0), alpha, name="ln10")
    hf1, f1, ffn1 = ffn_fwd(x3, 1)
    x4 = _add_ln_fwd(x3, ffn1, lng(1, 1), lnb(1, 1), alpha, name="ln11")
    sq_err, dy = _loss_and_grad(x4, tgt, name="loss")

    def owner_view(g):
        if g.ndim == 3:
            return g.reshape(4, 2, g.shape[1] // 2, g.shape[2])
        return g.reshape(4, 2, g.shape[0] // 8, g.shape[1])

    in_flight = []

    def send_grads(tag, grads):
        lands = [lax.empty((8,) + g.shape[2:], BF16) for g in grads]
        ss, rs, g_thru, l_thru, token = _reduce_start(grads, lands, name=f"reduce_start_{tag}")
        in_flight.append((tag, g_thru, l_thru, ss, rs))
        return token[0:1, 0:1]

    def ffn_bwd(dz, xin, hf, fact, l):
        d_wdown = _matmul(fact, dz, name=f"ffn{l}_down_dw", ta=True, tm=1408, tn=1024, tk=2048, out_dtype=BF16)
        dfa = _matmul(dz, w_down[l], name=f"ffn{l}_down_dx", tb=True, tm=1024, tn=1408, tk=1024)
        dx_parts, d_wup, dcw, dcb = _ffn_up_bwd(hf, dfa, xin, w_up[l], cw_f[l], cb_f[l], name=f"ffn{l}_up_bwd")
        tok = send_grads(f"ffn{l}", [owner_view(d_wup), owner_view(d_wdown)])
        return [(dx_parts, 1.0), (dz, alpha)], dcw, dcb, tok

    dz, dg11, db11 = _add_ln_bwd([(dy, 1.0)], x3, ffn1, lng(1, 1), alpha, name="ln11_bwd")
    dx3, dcw1, dcb1, tok = ffn_bwd(dz, x3, hf1, f1, 1)
    dz, dg10, db10 = _add_ln_bwd(dx3, x2, att, lng(1, 0) + tok, alpha, name="ln10_bwd")
    d_wo = _matmul(ao, dz, name="att_out_dw", ta=True, tm=1024, tn=1024, tk=1024, out_dtype=BF16)
    dao = _matmul(dz, w_o, name="att_out_dx", tb=True, tm=1024, tn=1024, tk=1024)
    dq, dkc, dkp, d_sinks = _attn_bwd(qkv, dao, lse, sinks, name="att_core_bwd")
    dqkv, d_bqkv = _dqkv_assemble(dq, dkc, dkp, name="att_dqkv")
    d_wqkv = _matmul(x2, dqkv, name="att_qkv_dw", ta=True, tm=1024, tn=dqkv.shape[1], tk=1024, out_dtype=BF16)
    d_wqkv_st = jnp.moveaxis(d_wqkv.reshape(d_wqkv.shape[0], 4, -1), 1, 0)
    tok = send_grads("att", [owner_view(d_wqkv_st), owner_view(d_wo)])
    dx2 = _matmul(dqkv, w_qkv, name="att_qkv_dx", tb=True, tm=1024, tn=1024, tk=dqkv.shape[1], addend=(dz, alpha))
    dz, dg01, db01 = _add_ln_bwd([(dx2, 1.0)], x1, ffn0, lng(0, 1) + tok, alpha, name="ln01_bwd")
    dx1, dcw0, dcb0, tok = ffn_bwd(dz, x1, hf0, f0, 0)
    dz, dg00, db00 = _add_ln_bwd(dx1, xs, mix, lng(0, 0) + tok, alpha, name="ln00_bwd")
    d_wout = _matmul(ab, dz, name="mix_out_dw", ta=True, tm=1024, tn=1024, tk=1024, out_dtype=BF16)
    dab = _matmul(dz, w_out, name="mix_out_dx", tb=True, tm=1024, tn=1024, tk=1024)
    grad_x, d_win, d_cwa, d_cba, d_ga, d_ba, d_gb, d_bb, d_ws, d_sb = _mixer_bwd(
        h0, a2, dab, xs, w_in, dz, alpha, conv_w_a, *mix_vecs[1:], ws, wst, sbb, tril, name="mix_bwd")
    tok = send_grads("mix", [owner_view(d_win), owner_view(d_wout)])
    sq_err = sq_err + tok

    small_w = [a_conv_w, a_conv_b, a_norm_g, a_norm_b, b_norm_g, b_norm_b, b_spatial_w, b_spatial_b, c_b_qkv,
               c_sinks, ffn_conv_w, ffn_conv_b, ln_g, ln_b]
    small_m = [m_a_conv_w, m_a_conv_b, m_a_norm_g, m_a_norm_b, m_b_norm_g, m_b_norm_b, m_b_spatial_w, m_b_spatial_b,
               m_c_b_qkv, m_c_sinks, m_ffn_conv_w, m_ffn_conv_b, m_ln_g, m_ln_b]
    small_v = [v_a_conv_w, v_a_conv_b, v_a_norm_g, v_a_norm_b, v_b_norm_g, v_b_norm_b, v_b_spatial_w, v_b_spatial_b,
               v_c_b_qkv, v_c_sinks, v_ffn_conv_w, v_ffn_conv_b, v_ln_g, v_ln_b]
    local = [d_cwa, d_cba, d_ga, d_ba, d_gb, d_bb, d_ws, d_sb, d_bqkv, d_sinks, dcw0, dcb0, dcw1, dcb1,
             dg00, dg01, dg10, dg11, db00, db01, db10, db11, sq_err]
    small_out, loss = _small_tail(local, list(zip(small_w, small_m, small_v)), name="small_tail")
    loss = loss[0, 0]
    small_g = [o[0] for o in small_out]
    sm_delta = [o[1] for o in small_out]
    sm_m = [o[2] for o in small_out]
    sm_v = [o[3] for o in small_out]

    place = jnp.stack([q_idx, c_idx, 4 * lax.axis_index("x") + 2 * lax.axis_index("y") + c_idx]).astype(jnp.int32)
    where = {"mix": [(0, None), (1, None)], "att": [(2, None), (3, None)], "ffn0": [(4, 0), (5, 0)], "ffn1": [(4, 1), (5, 1)]}
    big_w = [ab_w_in, ab_w_out, c_w_qkv, c_w_o, ffn_w_up, ffn_w_down]
    big_m = [m_ab_w_in, m_ab_w_out, m_c_w_qkv, m_c_w_o, m_ffn_w_up, m_ffn_w_down]
    big_v = [v_ab_w_in, v_ab_w_out, v_c_w_qkv, v_c_w_o, v_ffn_w_up, v_ffn_w_down]
    big_out = [None] * 6

    def finish(tags, after, label):
        bufs, layout = {}, []
        for tag, g_thru, l_thru, ss, rs in in_flight:
            if tag not in tags:
                continue
            own, landed = _reduce_wait(g_thru, l_thru, ss, rs, after, name=f"reduce_wait_{tag}")
            for k, (o, lead) in enumerate(where[tag]):
                piece = own[k].shape[2:]
                shape = (2,) + piece if lead is None else (2, 2) + piece
                bufs[o] = _octo_sum(own[k], landed[k], place, bufs.get(o), (lead, shape), name=f"reduce_sum_{tag}{k}")
                layout.append((o, lead))
        order = sorted(bufs)
        shared = _sibling_share([bufs[o] for o in order], [(order.index(o), lead) for o, lead in layout],
                                name=f"reduce_share_{label}")
        for o, g in zip(order, shared):
            w = big_w[o]
            two_d = lambda a: a.reshape(-1, a.shape[-1])
            outs = _adamw(two_d(w), two_d(g), two_d(big_m[o]), two_d(big_v[o]), name=f"adamw_big{o}")
            big_out[o] = [r.reshape(w.shape) for r in outs]
        return big_out[order[-1]][0]

    done = finish(("ffn1", "att", "ffn0"), sm_delta[0], "early")
    finish(("mix",), done, "mix")

    order_big = {0: 0, 9: 1, 10: 2, 13: 3, 14: 4, 17: 5}
    order_small = {1: 0, 2: 1, 3: 2, 4: 3, 5: 4, 6: 5, 7: 6, 8: 7, 11: 8, 12: 9, 15: 10, 16: 11, 18: 12, 19: 13}
    grads, deltas, new_m, new_v = [], [], [], []
    for pos_w in range(20):
        if pos_w in order_big:
            t = order_big[pos_w]
            grads.append(big_out[t][3])
            deltas.append(big_out[t][0])
            new_m.append(big_out[t][1])
            new_v.append(big_out[t][2])
        else:
            t = order_small[pos_w]
            grads.append(small_g[t])
            deltas.append(sm_delta[t])
            new_m.append(sm_m[t])
            new_v.append(sm_v[t])
    return (loss, grad_x[None], *grads, *deltas, *new_m, *new_v)
```

```python
import math

import jax
import jax.numpy as jnp
from jax import lax
from jax.experimental import pallas as pl
from jax.experimental.pallas import tpu as pltpu

F32 = jnp.float32
BF16 = jnp.bfloat16
MESH = pl.DeviceIdType.MESH

LN_EPS = 1e-5
HEAD_DIM = 64
ATT_BLOCK = 128
Q_PER_KV = 8
A_KERNEL = 31
CONV_HALO = 32
FFN_HALO = 8
B_CHUNK = 128
LANES = 128
MXU_WIDTH = 256
GELU_C = math.sqrt(2.0 / math.pi)
ADAM_LR = 0.001
ADAM_B1 = 0.9
ADAM_B2 = 0.999
ADAM_EPS = 1e-08
ADAM_WD = 0.01
ADAM_STEP = 10
VMEM_LIMIT = 56 * 1024 * 1024


def _cp(*dims):
    return pltpu.CompilerParams(dimension_semantics=dims, vmem_limit_bytes=VMEM_LIMIT)


def _pick(n, prefs):
    for p in prefs:
        if n % p == 0:
            return p
    return n


def _sig(x):
    return 1.0 / (1.0 + jnp.exp(-x))


def _gelu(x):
    t = jnp.tanh(GELU_C * (x + 0.044715 * (x * x * x)))
    return x * (0.5 * (1.0 + t)), t


def _gelu_grad(x, t):
    return 0.5 * (1.0 + t) + 0.5 * x * (1.0 - t * t) * (GELU_C * (1.0 + 3.0 * 0.044715 * x * x))


def _ln_stats(z):
    mu = jnp.mean(z, axis=-1, keepdims=True)
    zc = z - mu
    var = jnp.mean(zc * zc, axis=-1, keepdims=True)
    rstd = lax.rsqrt(var + LN_EPS)
    return zc * rstd, rstd


def _ln_bwd(dxh, xh, rstd):
    return rstd * (dxh - jnp.mean(dxh, axis=-1, keepdims=True) - xh * jnp.mean(dxh * xh, axis=-1, keepdims=True))


def _rowsum(a):
    return jnp.sum(a, axis=0, keepdims=True)


def _lshape(a):
    return (a.shape[0], a.shape[1]) if a.ndim == 2 else (a.shape[1], a.shape[0] * a.shape[2])


def _spec2(arr, blk_r, blk_c, ridx, cidx):
    if len(arr.shape) == 2:
        return pl.BlockSpec((blk_r, blk_c), lambda i, j, k: (ridx(i, j, k), cidx(i, j, k)))
    per = arr.shape[2] // blk_c
    assert arr.shape[2] % blk_c == 0
    return pl.BlockSpec((None, blk_r, blk_c), lambda i, j, k: (cidx(i, j, k) // per, ridx(i, j, k), cidx(i, j, k) % per))


def _matmul(a, b, *, name, ta=False, tb=False, tm, tn, tk, out_dtype=F32, out_stack=None, bias=None, addend=None):
    ar, ac = _lshape(a)
    br, bc = _lshape(b)
    m, kdim = (ac, ar) if ta else (ar, ac)
    n = br if tb else bc
    assert (bc if tb else br) == kdim
    tm, tn, tk = min(tm, m), min(tn, n), min(tk, kdim)
    assert m % tm == 0 and n % tn == 0 and kdim % tk == 0, (name, m, n, kdim, tm, tn, tk)
    nk = kdim // tk
    gi, gj, gk = (lambda i, j, k: i), (lambda i, j, k: j), (lambda i, j, k: k)
    a_spec = _spec2(a, tk, tm, gk, gi) if ta else _spec2(a, tm, tk, gi, gk)
    b_spec = _spec2(b, tn, tk, gj, gk) if tb else _spec2(b, tk, tn, gk, gj)
    if out_stack is None:
        out_sds = jax.ShapeDtypeStruct((m, n), out_dtype)
    else:
        out_sds = jax.ShapeDtypeStruct((out_stack, m, n // out_stack), out_dtype)
    o_spec = _spec2(out_sds, tm, tn, gi, gj)
    in_specs = [a_spec, b_spec]
    args = [a, b]
    if bias is not None:
        in_specs.append(pl.BlockSpec((1, tn), lambda i, j, k: (0, j)))
        args.append(bias)
    scale = None
    if addend is not None:
        add_arr, scale = addend
        in_specs.append(pl.BlockSpec((tm, tn), lambda i, j, k: (i, j)))
        args.append(add_arr)
    use_acc = nk > 1 and out_dtype != F32
    dn = (((0 if ta else 1,), (1 if tb else 0,)), ((), ()))

    def body(*refs):
        a_ref, b_ref = refs[0], refs[1]
        pos = 2
        bias_ref = add_ref = None
        if bias is not None:
            bias_ref = refs[pos]
            pos += 1
        if addend is not None:
            add_ref = refs[pos]
            pos += 1
        o_ref = refs[pos]
        acc_ref = refs[pos + 1] if use_acc else o_ref
        p = lax.dot_general(a_ref[...].astype(BF16), b_ref[...].astype(BF16), dn, preferred_element_type=F32)

        def finish(val):
            if bias_ref is not None:
                val = val + bias_ref[...]
            if add_ref is not None:
                val = val + scale * add_ref[...]
            return val.astype(out_dtype)

        if nk == 1:
            o_ref[...] = finish(p)
        else:
            k = pl.program_id(2)

            @pl.when(k == 0)
            def _():
                acc_ref[...] = p

            @pl.when(k > 0)
            def _():
                acc_ref[...] += p

            if use_acc or bias_ref is not None or add_ref is not None:
                @pl.when(k == nk - 1)
                def _():
                    o_ref[...] = finish(acc_ref[...])

    return pl.pallas_call(
        body, name=name, grid=(m // tm, n // tn, nk), in_specs=in_specs, out_specs=o_spec, out_shape=out_sds,
        scratch_shapes=[pltpu.VMEM((tm, tn), F32)] if use_acc else [],
        compiler_params=_cp("parallel", "parallel", "arbitrary"),
    )(*args)


def _add_ln_fwd(x, s, g, b, alpha, *, name):
    rows, d = x.shape
    t = _pick(rows, (512, 256))

    def body(x_ref, s_ref, g_ref, b_ref, y_ref):
        xh, _ = _ln_stats(alpha * x_ref[...] + s_ref[...])
        y_ref[...] = xh * g_ref[...] + b_ref[...]

    row = pl.BlockSpec((t, d), lambda i: (i, 0))
    vec = pl.BlockSpec((1, d), lambda i: (0, 0))
    return pl.pallas_call(body, name=name, grid=(rows // t,), in_specs=[row, row, vec, vec], out_specs=row,
                          out_shape=jax.ShapeDtypeStruct((rows, d), F32), compiler_params=_cp("parallel"))(x, s, g, b)


def _add_ln_bwd(dy_terms, x, s, g, alpha, *, name):
    rows, d = x.shape
    t = _pick(rows, (512, 256))
    nterm = len(dy_terms)
    scales = [sc for _, sc in dy_terms]
    ranks = [a.ndim for a, _ in dy_terms]

    def body(*refs):
        dy_refs = refs[:nterm]
        x_ref, s_ref, g_ref, dz_ref, dg_ref, db_ref = refs[nterm:]

        @pl.when(pl.program_id(0) == 0)
        def _():
            dg_ref[...] = jnp.zeros_like(dg_ref)
            db_ref[...] = jnp.zeros_like(db_ref)

        dyv = None
        for r, sc, rank in zip(dy_refs, scales, ranks):
            slabs = [r[...]] if rank == 2 else [r[p] for p in range(r.shape[0])]
            for v in slabs:
                v = v if sc == 1.0 else sc * v
                dyv = v if dyv is None else dyv + v
        xh, rstd = _ln_stats(alpha * x_ref[...] + s_ref[...])
        dz_ref[...] = _ln_bwd(dyv * g_ref[...], xh, rstd)
        dg_ref[...] += _rowsum(dyv * xh)
        db_ref[...] += _rowsum(dyv)

    row = pl.BlockSpec((t, d), lambda i: (i, 0))
    vec = pl.BlockSpec((1, d), lambda i: (0, 0))
    vsds = jax.ShapeDtypeStruct((1, d), F32)
    dy_specs = [row if a.ndim == 2 else pl.BlockSpec((a.shape[0], t, d), lambda i: (0, i, 0)) for a, _ in dy_terms]
    return pl.pallas_call(body, name=name, grid=(rows // t,), in_specs=dy_specs + [row, row, vec], out_specs=[row, vec, vec],
                          out_shape=[jax.ShapeDtypeStruct((rows, d), F32), vsds, vsds],
                          compiler_params=_cp("arbitrary"))(*[a for a, _ in dy_terms], x, s, g)


def _loss_and_grad(y, tgt, *, name):
    rows, d = y.shape
    t = _pick(rows, (512, 256))

    def body(y_ref, t_ref, l_ref, dy_ref):
        @pl.when(pl.program_id(0) == 0)
        def _():
            l_ref[...] = jnp.zeros_like(l_ref)

        e = y_ref[...] - t_ref[...]
        l_ref[...] += _rowsum(e * e)
        dy_ref[...] = e * (1.0 / d)

    row = pl.BlockSpec((t, d), lambda i: (i, 0))
    vec = pl.BlockSpec((1, d), lambda i: (0, 0))
    return pl.pallas_call(body, name=name, grid=(rows // t,), in_specs=[row, row], out_specs=[vec, row],
                          out_shape=[jax.ShapeDtypeStruct((1, d), F32), jax.ShapeDtypeStruct((rows, d), F32)],
                          compiler_params=_cp("arbitrary"))(y, tgt)


def _col_blocks(width):
    out, pos = [], 0
    while pos < width:
        w = MXU_WIDTH if width - pos >= MXU_WIDTH else width - pos
        out.append(slice(pos, pos + w))
        pos += w
    return out


def _conv3(e, w, b):
    r1 = pltpu.roll(e, 1, 0)
    r2 = pltpu.roll(e, 2, 0)
    return w[0:1, :] * r2 + w[1:2, :] * r1 + w[2:3, :] * e + b, (r2, r1, e)


def _ffn_up_fwd(x, w_up, cw, cb, *, name):
    rows, d = x.shape
    nq, _, tc = w_up.shape
    nj = nq // 2
    f = tc * nj
    tm = _pick(rows, (512, 256))
    blocks = _col_blocks(tc)

    def body(x_ref, wg_ref, wv_ref, cw_ref, cb_ref, hf_ref, f_ref, prev_ref):
        @pl.when(pl.program_id(1) == 0)
        def _():
            prev_ref[...] = jnp.zeros_like(prev_ref)

        xb = x_ref[...].astype(BF16)
        for cs in blocks:
            hc = []
            for s, w_ref in ((0, wg_ref), (1, wv_ref)):
                h = jnp.dot(xb, w_ref[:, cs], preferred_element_type=F32)
                hf_ref[s, :, cs] = h
                e = jnp.concatenate([prev_ref[s, :, cs], h], axis=0)
                prev_ref[s, :, cs] = h[tm - FFN_HALO:]
                y, _ = _conv3(e, cw_ref[s, :, cs], cb_ref[s, :, cs])
                hc.append(y[FFN_HALO:])
            gl, _ = _gelu(hc[0])
            f_ref[:, cs] = (gl * hc[1]).astype(BF16)

    in_specs = [
        pl.BlockSpec((tm, d), lambda j, i: (i, 0)),
        pl.BlockSpec((None, d, tc), lambda j, i: (j, 0, 0)),
        pl.BlockSpec((None, d, tc), lambda j, i: (nj + j, 0, 0)),
        pl.BlockSpec((2, 3, tc), lambda j, i: (0, 0, j)),
        pl.BlockSpec((2, 1, tc), lambda j, i: (0, 0, j)),
    ]
    out_specs = [pl.BlockSpec((2, tm, tc), lambda j, i: (0, i, j)), pl.BlockSpec((tm, tc), lambda j, i: (i, j))]
    out_shape = [jax.ShapeDtypeStruct((2, rows, f), F32), jax.ShapeDtypeStruct((rows, f), BF16)]
    return pl.pallas_call(body, name=name, grid=(nj, rows // tm), in_specs=in_specs, out_specs=out_specs, out_shape=out_shape,
                          scratch_shapes=[pltpu.VMEM((2, FFN_HALO, tc), F32)],
                          compiler_params=_cp("parallel", "arbitrary"))(x, w_up, w_up, cw, cb)


def _ffn_up_bwd(hf, df, x, w_up, cw, cb, *, name):
    _, rows, f = hf.shape
    d = x.shape[1]
    nq, _, tc = w_up.shape
    nj = nq // 2
    tm = _pick(rows, (512, 256))
    hb = tm // FFN_HALO
    once = pl.Buffered(1)
    ni = rows // tm
    last_blk = rows // FFN_HALO - 1
    ext = tm + 2 * FFN_HALO
    tile = slice(FFN_HALO, FFN_HALO + tm)
    blocks = _col_blocks(tc)

    def body(h_ref, hp_ref, hn_ref, d_ref, dn_ref, x_ref, wg_ref, wv_ref, cw_ref, cb_ref, dx_ref, dw_out_ref, dcw_ref, dcb_ref,
             dw_ref):
        i = pl.program_id(1)
        first = i == 0
        last = i == ni - 1

        @pl.when(first)
        def _():
            dw_ref[...] = jnp.zeros_like(dw_ref)
            dcw_ref[...] = jnp.zeros_like(dcw_ref)
            dcb_ref[...] = jnp.zeros_like(dcb_ref)

        xt = x_ref[...].astype(BF16).T
        dx = None
        for cs in blocks:
            wc = cs.stop - cs.start
            de = jnp.concatenate([jnp.zeros((FFN_HALO, wc), F32), d_ref[:, cs], jnp.where(last, 0.0, dn_ref[:, cs])], axis=0)
            taps, hc = [], []
            for s in range(2):
                e = jnp.concatenate([jnp.where(first, 0.0, hp_ref[s, :, cs]), h_ref[s, :, cs], hn_ref[s, :, cs]], axis=0)
                y, tp = _conv3(e, cw_ref[s, :, cs], cb_ref[s, :, cs])
                hc.append(y)
                taps.append(tp)
            gl, th = _gelu(hc[0])
            dhc = (de * hc[1] * _gelu_grad(hc[0], th), de * gl)
            for s, w_ref in ((0, wg_ref), (1, wv_ref)):
                w = cw_ref[s, :, cs]
                g = dhc[s]
                dh = (w[2:3, :] * g + w[1:2, :] * pltpu.roll(g, ext - 1, 0) + w[0:1, :] * pltpu.roll(g, ext - 2, 0))[tile]
                gt = g[tile]
                for k in range(3):
                    dcw_ref[s, k:k + 1, cs] += _rowsum(gt * taps[s][k][tile])
                dcb_ref[s, :, cs] += _rowsum(gt)
                dhb = dh.astype(BF16)
                part = lax.dot_general(dhb, w_ref[:, cs], (((1,), (1,)), ((), ())), preferred_element_type=F32)
                dx = part if dx is None else dx + part
                dw_ref[s, :, cs] += jnp.dot(xt, dhb, preferred_element_type=F32)
        dx_ref[...] = dx

        @pl.when(last)
        def _():
            dw_out_ref[...] = dw_ref[...].astype(BF16)

    in_specs = [
        pl.BlockSpec((2, tm, tc), lambda j, i: (0, i, j)),
        pl.BlockSpec((2, FFN_HALO, tc), lambda j, i: (0, jnp.maximum(i * hb - 1, 0), j)),
        pl.BlockSpec((2, FFN_HALO, tc), lambda j, i: (0, jnp.minimum((i + 1) * hb, last_blk), j)),
        pl.BlockSpec((tm, tc), lambda j, i: (i, j)),
        pl.BlockSpec((FFN_HALO, tc), lambda j, i: (jnp.minimum((i + 1) * hb, last_blk), j)),
        pl.BlockSpec((tm, d), lambda j, i: (i, 0)),
        pl.BlockSpec((None, d, tc), lambda j, i: (j, 0, 0), pipeline_mode=once),
        pl.BlockSpec((None, d, tc), lambda j, i: (nj + j, 0, 0), pipeline_mode=once),
        pl.BlockSpec((2, 3, tc), lambda j, i: (0, 0, j)),
        pl.BlockSpec((2, 1, tc), lambda j, i: (0, 0, j)),
    ]
    out_specs = [
        pl.BlockSpec((None, tm, d), lambda j, i: (j, i, 0)),
        pl.BlockSpec((2, None, d, tc), lambda j, i: (0, j, 0, 0), pipeline_mode=once),
        pl.BlockSpec((2, 3, tc), lambda j, i: (0, 0, j)),
        pl.BlockSpec((2, 1, tc), lambda j, i: (0, 0, j)),
    ]
    out_shape = [jax.ShapeDtypeStruct((nj, rows, d), F32), jax.ShapeDtypeStruct((2, nj, d, tc), BF16),
                 jax.ShapeDtypeStruct((2, 3, f), F32), jax.ShapeDtypeStruct((2, 1, f), F32)]
    dx, dw, dcw, dcb = pl.pallas_call(body, name=name, grid=(nj, ni), in_specs=in_specs, out_specs=out_specs,
                                      out_shape=out_shape, scratch_shapes=[pltpu.VMEM((2, d, tc), F32)],
                                      compiler_params=_cp("parallel", "arbitrary"))(
        hf, hf, hf, df, df, x, w_up, w_up, cw, cb)
    return dx, dw.reshape(nq, d, tc), dcw, dcb


def _mixer_fwd(h0, cw, cb, ga, ba, gb, bb, ws, sbb, *, name):
    _, rows, w = h0.shape
    t = _pick(rows, (256,))
    hb = t // CONV_HALO
    groups = w // B_CHUNK

    def body(h_ref, hp_ref, cw_ref, cb_ref, ga_ref, ba_ref, gb_ref, bb_ref, ws_ref, sb_ref, o_ref, a2_ref):
        first = pl.program_id(0) == 0
        a1 = h_ref[0] * _sig(h_ref[1])
        a1p = jnp.where(first, 0.0, hp_ref[0] * _sig(hp_ref[1]))
        e = jnp.concatenate([a1p, a1], axis=0)
        acc = cw_ref[A_KERNEL - 1:A_KERNEL, :] * e
        for k in range(A_KERNEL - 1):
            acc = acc + cw_ref[k:k + 1, :] * pltpu.roll(e, A_KERNEL - 1 - k, 0)
        a2 = acc[CONV_HALO:] + cb_ref[...]
        a2_ref[...] = a2
        xh, _ = _ln_stats(a2)
        a3 = xh * ga_ref[...] + ba_ref[...]
        o_ref[:, 0:w] = (a3 * _sig(a3)).astype(BF16)

        u, _ = _gelu(h_ref[2])
        v1, _ = _gelu(h_ref[3])
        xh2, _ = _ln_stats(v1)
        v2 = (xh2 * gb_ref[...] + bb_ref[...]).astype(BF16)
        for c in range(t // B_CHUNK):
            rs = slice(c * B_CHUNK, (c + 1) * B_CHUNK)
            for g in range(groups):
                cs = slice(g * B_CHUNK, (g + 1) * B_CHUNK)
                mixed = jnp.dot(ws_ref[g], v2[rs, cs], preferred_element_type=F32) + sb_ref[g]
                o_ref[rs, w + g * B_CHUNK:w + (g + 1) * B_CHUNK] = (u[rs, cs] * mixed).astype(BF16)

    vec = pl.BlockSpec((1, w), lambda i: (0, 0))
    grp = pl.BlockSpec((groups, B_CHUNK, B_CHUNK), lambda i: (0, 0, 0))
    in_specs = [
        pl.BlockSpec((4, t, w), lambda i: (0, i, 0)),
        pl.BlockSpec((2, CONV_HALO, w), lambda i: (0, jnp.maximum(i * hb - 1, 0), 0)),
        pl.BlockSpec((A_KERNEL, w), lambda i: (0, 0)),
        vec, vec, vec, vec, vec, grp, grp,
    ]
    out_specs = [pl.BlockSpec((t, 2 * w), lambda i: (i, 0)), pl.BlockSpec((t, w), lambda i: (i, 0))]
    out_shape = [jax.ShapeDtypeStruct((rows, 2 * w), BF16), jax.ShapeDtypeStruct((rows, w), F32)]
    return pl.pallas_call(body, name=name, grid=(rows // t,), in_specs=in_specs, out_specs=out_specs, out_shape=out_shape,
                          compiler_params=_cp("parallel"))(h0, h0, cw, cb, ga, ba, gb, bb, ws, sbb)


def _mixer_bwd(h0, a2, dab, x, w_in, res, res_scale, cw, ga, ba, gb, bb, ws, wst, sbb, tril, *, name):
    _, rows, w = h0.shape
    d = x.shape[1]
    once = pl.Buffered(1)
    t = _pick(rows, (256,))
    hb = t // CONV_HALO
    ni = rows // t
    last_blk = rows // CONV_HALO - 1
    ext = t + CONV_HALO
    tile = slice(0, t)
    groups = w // B_CHUNK
    taps = A_KERNEL - 1

    def body(h_ref, a2_ref, a2n_ref, d_ref, dn_ref, x_ref, win_ref, res_ref, cw_ref, ga_ref, ba_ref, gb_ref, bb_ref,
             ws_ref, wst_ref, sb_ref, tril_ref, dx_ref, dwin_ref, dcw_ref, dcb_ref, dga_ref, dba_ref, dgb_ref, dbb_ref,
             dws_ref, dsb_ref, dw_ref):
        i = pl.program_id(0)
        first = i == 0
        last = i == ni - 1

        @pl.when(first)
        def _():
            for r in (dw_ref, dcw_ref, dcb_ref, dga_ref, dba_ref, dgb_ref, dbb_ref, dws_ref, dsb_ref):
                r[...] = jnp.zeros_like(r)

        xt = x_ref[...].astype(BF16).T
        dx_terms = []

        def through_w_in(slot, dh):
            dhb = dh.astype(BF16)
            dx_terms.append(lax.dot_general(dhb, win_ref[slot], (((1,), (1,)), ((), ())), preferred_element_type=F32))
            dw_ref[slot] += jnp.dot(xt, dhb, preferred_element_type=F32)

        xh, rstd = _ln_stats(jnp.concatenate([a2_ref[...], a2n_ref[...]], axis=0))
        a3 = xh * ga_ref[...] + ba_ref[...]
        s3 = _sig(a3)
        da_e = jnp.concatenate([d_ref[:, 0:w], jnp.where(last, 0.0, dn_ref[...])], axis=0)
        da3 = da_e * (s3 * (1.0 + a3 * (1.0 - s3)))
        da2 = _ln_bwd(da3 * ga_ref[...], xh, rstd)
        dga_ref[...] += _rowsum(da3[tile] * xh[tile])
        dba_ref[...] += _rowsum(da3[tile])
        dcb_ref[...] += _rowsum(da2[tile])
        sgt = _sig(h_ref[1])
        a1t = h_ref[0] * sgt
        da1t = None
        for k in range(A_KERNEL):
            sh = taps - k
            fed = (da2 if sh == 0 else pltpu.roll(da2, ext - sh, 0))[tile]
            dcw_ref[k:k + 1, :] += _rowsum(a1t * fed)
            term = cw_ref[k:k + 1, :] * fed
            da1t = term if da1t is None else da1t + term
        through_w_in(0, da1t * sgt)
        through_w_in(1, da1t * h_ref[0] * sgt * (1.0 - sgt))

        bu = h_ref[2]
        bv = h_ref[3]
        u, tu = _gelu(bu)
        v1, tv = _gelu(bv)
        xh2, rstd2 = _ln_stats(v1)
        v2 = (xh2 * gb_ref[...] + bb_ref[...]).astype(BF16)
        db = d_ref[:, w:2 * w]
        dmx_all = db * u
        du_parts, dv2_parts = [], []
        for c in range(t // B_CHUNK):
            rs = slice(c * B_CHUNK, (c + 1) * B_CHUNK)
            du_row, dv2_row = [], []
            for g in range(groups):
                cs = slice(g * B_CHUNK, (g + 1) * B_CHUNK)
                v2cg = v2[rs, cs]
                mixed = jnp.dot(ws_ref[g], v2cg, preferred_element_type=F32) + sb_ref[g]
                dmx = dmx_all[rs, cs]
                dmxb = dmx.astype(BF16)
                du_row.append(db[rs, cs] * mixed)
                dv2_row.append(jnp.dot(wst_ref[g], dmxb, preferred_element_type=F32))
                dws_ref[g] += tril_ref[...] * lax.dot_general(dmxb, v2cg, (((1,), (1,)), ((), ())),
                                                               preferred_element_type=F32)
                dsb_ref[g:g + 1, :] += _rowsum(dmx.T)
            du_parts.append(jnp.concatenate(du_row, axis=1))
            dv2_parts.append(jnp.concatenate(dv2_row, axis=1))
        du = jnp.concatenate(du_parts, axis=0)
        dv2 = jnp.concatenate(dv2_parts, axis=0)
        dgb_ref[...] += _rowsum(dv2 * xh2)
        dbb_ref[...] += _rowsum(dv2)
        dv1 = _ln_bwd(dv2 * gb_ref[...], xh2, rstd2)
        through_w_in(2, du * _gelu_grad(bu, tu))
        through_w_in(3, dv1 * _gelu_grad(bv, tv))
        dx_ref[...] = res_scale * res_ref[...] + ((dx_terms[0] + dx_terms[1]) + (dx_terms[2] + dx_terms[3]))

        @pl.when(last)
        def _():
            dwin_ref[...] = dw_ref[...].astype(BF16)

    vec = pl.BlockSpec((1, w), lambda i: (0, 0))
    grp = pl.BlockSpec((groups, B_CHUNK, B_CHUNK), lambda i: (0, 0, 0))
    halo = pl.BlockSpec((CONV_HALO, w), lambda i: (jnp.minimum((i + 1) * hb, last_blk), 0))
    wide = pl.BlockSpec((t, d), lambda i: (i, 0))
    in_specs = [
        pl.BlockSpec((4, t, w), lambda i: (0, i, 0)),
        pl.BlockSpec((t, w), lambda i: (i, 0)),
        halo,
        pl.BlockSpec((t, 2 * w), lambda i: (i, 0)),
        halo,
        wide,
        pl.BlockSpec((4, d, w), lambda i: (0, 0, 0), pipeline_mode=once),
        wide,
        pl.BlockSpec((A_KERNEL, w), lambda i: (0, 0)),
        vec, vec, vec, vec, grp, grp, grp,
        pl.BlockSpec((B_CHUNK, B_CHUNK), lambda i: (0, 0)),
    ]
    vsds = jax.ShapeDtypeStruct((1, w), F32)
    out_specs = [
        wide,
        pl.BlockSpec((4, d, w), lambda i: (0, 0, 0), pipeline_mode=once),
        pl.BlockSpec((A_KERNEL, w), lambda i: (0, 0)),
        vec, vec, vec, vec, vec, grp,
        pl.BlockSpec((groups, B_CHUNK), lambda i: (0, 0)),
    ]
    out_shape = [jax.ShapeDtypeStruct((rows, d), F32), jax.ShapeDtypeStruct((4, d, w), BF16),
                 jax.ShapeDtypeStruct((A_KERNEL, w), F32),
                 vsds, vsds, vsds, vsds, vsds, jax.ShapeDtypeStruct((groups, B_CHUNK, B_CHUNK), F32),
                 jax.ShapeDtypeStruct((groups, B_CHUNK), F32)]
    return pl.pallas_call(body, name=name, grid=(ni,), in_specs=in_specs, out_specs=out_specs, out_shape=out_shape,
                          scratch_shapes=[pltpu.VMEM((4, d, w), F32)], compiler_params=_cp("arbitrary"))(
        h0, a2, a2, dab, dab, x, w_in, res, cw, ga, ba, gb, bb, ws, wst, sbb, tril)


GROUP_ROWS = Q_PER_KV * ATT_BLOCK


def _attn_mask(n):
    qi = lax.broadcasted_iota(jnp.int32, (GROUP_ROWS, 2 * ATT_BLOCK), 0) & (ATT_BLOCK - 1)
    sj = lax.broadcasted_iota(jnp.int32, (GROUP_ROWS, 2 * ATT_BLOCK), 1)
    diff = qi + ATT_BLOCK - sj
    return (diff >= 0) & (diff < ATT_BLOCK) & ((n > 0) | (sj >= ATT_BLOCK))


def _stack_heads(ref, kvh, dtype):
    heads = [ref[:, (kvh * Q_PER_KV + g) * HEAD_DIM:(kvh * Q_PER_KV + g + 1) * HEAD_DIM] for g in range(Q_PER_KV)]
    return jnp.concatenate(heads, axis=0).astype(dtype)


def _per_row_sink(sink_ref, kvh):
    head = lax.broadcasted_iota(jnp.int32, (GROUP_ROWS, 1), 0) // ATT_BLOCK
    out = jnp.zeros((GROUP_ROWS, 1), F32)
    for g in range(Q_PER_KV):
        out = jnp.where(head == g, sink_ref[kvh * Q_PER_KV + g], out)
    return out


def _attn_specs(rows, n_q):
    dq = n_q * HEAD_DIM
    dkv = 2 * (n_q // Q_PER_KV) * HEAD_DIM
    kv_blk = dq // dkv
    assert dq % dkv == 0
    return dq, dkv, [
        pl.BlockSpec(memory_space=pltpu.SMEM),
        pl.BlockSpec((ATT_BLOCK, dq), lambda n: (n, 0)),
        pl.BlockSpec((ATT_BLOCK, dkv), lambda n: (n, kv_blk)),
        pl.BlockSpec((ATT_BLOCK, dkv), lambda n: (jnp.maximum(n - 1, 0), kv_blk)),
    ]


def _kv_pair(kvc_ref, kvp_ref, kvh, n_kv):
    ks = slice(kvh * HEAD_DIM, (kvh + 1) * HEAD_DIM)
    vs = slice((n_kv + kvh) * HEAD_DIM, (n_kv + kvh + 1) * HEAD_DIM)
    kk = jnp.concatenate([kvp_ref[:, ks], kvc_ref[:, ks]], axis=0).astype(BF16)
    vv = jnp.concatenate([kvp_ref[:, vs], kvc_ref[:, vs]], axis=0).astype(BF16)
    return kk, vv


def _attn_fwd(qkv, sinks, *, name):
    rows = qkv.shape[0]
    n_q = sinks.shape[0]
    n_kv = n_q // Q_PER_KV
    scale = 1.0 / math.sqrt(HEAD_DIM)
    dq, _, in_specs = _attn_specs(rows, n_q)

    def body(sink_ref, q_ref, kvc_ref, kvp_ref, o_ref, lse_ref):
        valid = _attn_mask(pl.program_id(0))
        for kvh in range(n_kv):
            kk, vv = _kv_pair(kvc_ref, kvp_ref, kvh, n_kv)
            qs = _stack_heads(q_ref, kvh, BF16)
            s = lax.dot_general(qs, kk, (((1,), (1,)), ((), ())), preferred_element_type=F32)
            s = jnp.where(valid, s * scale, -jnp.inf)
            sk = _per_row_sink(sink_ref, kvh)
            m = jnp.maximum(jnp.max(s, axis=1, keepdims=True), sk)
            p = jnp.exp(s - m)
            l = jnp.sum(p, axis=1, keepdims=True) + jnp.exp(sk - m)
            o = jnp.dot((p / l).astype(BF16), vv, preferred_element_type=F32)
            lse = m + jnp.log(l)
            for g in range(Q_PER_KV):
                h = kvh * Q_PER_KV + g
                rs = slice(g * ATT_BLOCK, (g + 1) * ATT_BLOCK)
                o_ref[:, h * HEAD_DIM:(h + 1) * HEAD_DIM] = o[rs]
                lse_ref[:, h:h + 1] = lse[rs]

    out_specs = [pl.BlockSpec((ATT_BLOCK, dq), lambda n: (n, 0)), pl.BlockSpec((ATT_BLOCK, n_q), lambda n: (n, 0))]
    out_shape = [jax.ShapeDtypeStruct((rows, dq), F32), jax.ShapeDtypeStruct((rows, n_q), F32)]
    return pl.pallas_call(body, name=name, grid=(rows // ATT_BLOCK,), in_specs=in_specs, out_specs=out_specs,
                          out_shape=out_shape, compiler_params=_cp("parallel"))(sinks, qkv, qkv, qkv)


def _attn_bwd(qkv, dout, lse, sinks, *, name):
    rows = qkv.shape[0]
    n_q = sinks.shape[0]
    n_kv = n_q // Q_PER_KV
    scale = 1.0 / math.sqrt(HEAD_DIM)
    dq_w, dkv_w, in_specs = _attn_specs(rows, n_q)
    blk_q = pl.BlockSpec((ATT_BLOCK, dq_w), lambda n: (n, 0))
    blk_kv = pl.BlockSpec((ATT_BLOCK, dkv_w), lambda n: (n, 0))
    in_specs = in_specs + [blk_q, pl.BlockSpec((ATT_BLOCK, n_q), lambda n: (n, 0))]

    def body(sink_ref, q_ref, kvc_ref, kvp_ref, do_ref, lse_ref, dq_ref, dkc_ref, dkp_ref, dsink_ref):
        n = pl.program_id(0)

        @pl.when(n == 0)
        def _():
            dsink_ref[...] = jnp.zeros_like(dsink_ref)

        valid = _attn_mask(n)
        head_ids = lax.broadcasted_iota(jnp.int32, (1, n_q), 1)
        dsink = jnp.zeros((1, n_q), F32)
        for kvh in range(n_kv):
            kk, vv = _kv_pair(kvc_ref, kvp_ref, kvh, n_kv)
            qs = _stack_heads(q_ref, kvh, BF16)
            dos = _stack_heads(do_ref, kvh, BF16)
            lse = jnp.concatenate([lse_ref[:, kvh * Q_PER_KV + g:kvh * Q_PER_KV + g + 1] for g in range(Q_PER_KV)], axis=0)
            s = lax.dot_general(qs, kk, (((1,), (1,)), ((), ())), preferred_element_type=F32)
            s = jnp.where(valid, s * scale, -jnp.inf)
            p = jnp.exp(s - lse)
            dp = lax.dot_general(dos, vv, (((1,), (1,)), ((), ())), preferred_element_type=F32)
            delta = jnp.sum(p * dp, axis=1, keepdims=True)
            ds = (p * (dp - delta) * scale).astype(BF16)
            sink_term = jnp.exp(_per_row_sink(sink_ref, kvh) - lse) * delta
            dqs = jnp.dot(ds, kk, preferred_element_type=F32)
            for g in range(Q_PER_KV):
                h = kvh * Q_PER_KV + g
                rs = slice(g * ATT_BLOCK, (g + 1) * ATT_BLOCK)
                dsink = dsink + jnp.where(head_ids == h, -jnp.sum(sink_term[rs]), 0.0)
                dq_ref[:, h * HEAD_DIM:(h + 1) * HEAD_DIM] = dqs[rs]
            dk = lax.dot_general(ds, qs, (((0,), (0,)), ((), ())), preferred_element_type=F32)
            dv = lax.dot_general(p.astype(BF16), dos, (((0,), (0,)), ((), ())), preferred_element_type=F32)
            ks = slice(kvh * HEAD_DIM, (kvh + 1) * HEAD_DIM)
            vs = slice((n_kv + kvh) * HEAD_DIM, (n_kv + kvh + 1) * HEAD_DIM)
            dkp_ref[:, ks] = dk[0:ATT_BLOCK]
            dkc_ref[:, ks] = dk[ATT_BLOCK:]
            dkp_ref[:, vs] = dv[0:ATT_BLOCK]
            dkc_ref[:, vs] = dv[ATT_BLOCK:]
        dsink_ref[...] += dsink

    out_specs = [blk_q, blk_kv, blk_kv, pl.BlockSpec((1, n_q), lambda n: (0, 0))]
    out_shape = [jax.ShapeDtypeStruct((rows, dq_w), F32), jax.ShapeDtypeStruct((rows, dkv_w), F32),
                 jax.ShapeDtypeStruct((rows, dkv_w), F32), jax.ShapeDtypeStruct((1, n_q), F32)]
    return pl.pallas_call(body, name=name, grid=(rows // ATT_BLOCK,), in_specs=in_specs, out_specs=out_specs,
                          out_shape=out_shape, compiler_params=_cp("arbitrary"))(sinks, qkv, qkv, qkv, dout, lse)


def _dqkv_assemble(dq, dkc, dkp, *, name):
    rows, dq_w = dq.shape
    dkv_w = dkc.shape[1]
    nb = rows // ATT_BLOCK

    def body(dq_ref, dkc_ref, dkp_ref, o_ref, db_ref):
        n = pl.program_id(0)

        @pl.when(n == 0)
        def _():
            db_ref[...] = jnp.zeros_like(db_ref)

        dqv = dq_ref[...]
        dkv = dkc_ref[...] + jnp.where(n == nb - 1, 0.0, dkp_ref[...])
        o_ref[:, 0:dq_w] = dqv.astype(BF16)
        o_ref[:, dq_w:dq_w + dkv_w] = dkv.astype(BF16)
        db_ref[:, 0:dq_w] += _rowsum(dqv)
        db_ref[:, dq_w:dq_w + dkv_w] += _rowsum(dkv)

    width = dq_w + dkv_w
    in_specs = [pl.BlockSpec((ATT_BLOCK, dq_w), lambda n: (n, 0)), pl.BlockSpec((ATT_BLOCK, dkv_w), lambda n: (n, 0)),
                pl.BlockSpec((ATT_BLOCK, dkv_w), lambda n: (jnp.minimum(n + 1, nb - 1), 0))]
    out_specs = [pl.BlockSpec((ATT_BLOCK, width), lambda n: (n, 0)), pl.BlockSpec((1, width), lambda n: (0, 0))]
    out_shape = [jax.ShapeDtypeStruct((rows, width), BF16), jax.ShapeDtypeStruct((1, width), F32)]
    return pl.pallas_call(body, name=name, grid=(nb,), in_specs=in_specs, out_specs=out_specs, out_shape=out_shape,
                          compiler_params=_cp("arbitrary"))(dq, dkc, dkp)


def _row_tile(r, c):
    budget = 2 * 1024 * 1024 // (4 * c)
    for cand in (1024, 512, 256, 128, 64, 32, 16):
        if cand <= budget and r % cand == 0:
            return cand
    return r


def _octo_sum(own, recv, place, dest, lead, *, name):
    _, _, r, c = own.shape
    t = _row_tile(r, c)
    lead_idx, buf_shape = lead

    def body(place_ref, own_ref, *rest):
        o_ref = rest[7] if dest is None else rest[8]
        acc = own_ref[...].astype(F32)
        for k in range(7):
            acc = acc + rest[k][...].astype(F32)
        o_ref[...] = acc

    def peer(mask):
        return pl.BlockSpec((None, t, c), lambda i, pr: (pr[2] ^ mask, i, 0))

    if lead_idx is None:
        o_spec = pl.BlockSpec((None, t, c), lambda i, pr: (pr[1], i, 0))
    else:
        o_spec = pl.BlockSpec((None, None, t, c), lambda i, pr: (lead_idx, pr[1], i, 0))
    in_specs = [pl.BlockSpec((None, None, t, c), lambda i, pr: (pr[0], pr[1], i, 0))] + [peer(m) for m in range(1, 8)]
    args = [place, own] + [recv] * 7
    aliases = {}
    if dest is not None:
        in_specs.append(HBM)
        args.append(dest)
        aliases = {9: 0}
    grid_spec = pltpu.PrefetchScalarGridSpec(num_scalar_prefetch=1, grid=(r // t,), in_specs=in_specs, out_specs=o_spec)
    return pl.pallas_call(body, name=name, grid_spec=grid_spec, out_shape=jax.ShapeDtypeStruct(buf_shape, F32),
                          input_output_aliases=aliases, compiler_params=_cp("parallel"))(*args)


def _adamw_math(w, g, m, v):
    nm = ADAM_B1 * m + (1.0 - ADAM_B1) * g
    nv = ADAM_B2 * v + (1.0 - ADAM_B2) * (g * g)
    m_hat = nm / (1.0 - ADAM_B1 ** ADAM_STEP)
    v_hat = nv / (1.0 - ADAM_B2 ** ADAM_STEP)
    return -ADAM_LR * (m_hat / (jnp.sqrt(v_hat) + ADAM_EPS) + ADAM_WD * w), nm, nv


def _adamw(w, g, m, v, *, name):
    r, c = w.shape
    t = _row_tile(r, c)

    def body(w_ref, g_ref, m_ref, v_ref, d_ref, nm_ref, nv_ref, go_ref):
        gv = g_ref[...]
        d_ref[...], nm_ref[...], nv_ref[...] = _adamw_math(w_ref[...], gv, m_ref[...], v_ref[...])
        go_ref[...] = gv

    blk = pl.BlockSpec((t, c), lambda i: (i, 0))
    sds = jax.ShapeDtypeStruct((r, c), F32)
    return pl.pallas_call(body, name=name, grid=(r // t,), in_specs=[blk] * 4, out_specs=[blk] * 4,
                          out_shape=[sds] * 4, compiler_params=_cp("parallel"))(w, g, m, v)


HBM = pl.BlockSpec(memory_space=pl.ANY)


def _place():
    x, y, c = lax.axis_index("x"), lax.axis_index("y"), lax.axis_index("c")
    chips = [(1 - x, y), (x, 1 - y), (1 - x, 1 - y)]
    return x, y, c, 2 * x + y, (x, y, 1 - c), chips


def _rcopy(src, dst, ssem, rsem, dev):
    return pltpu.make_async_remote_copy(src_ref=src, dst_ref=dst, send_sem=ssem, recv_sem=rsem, device_id=dev,
                                        device_id_type=MESH)


HBM_ONLY = pl.BlockSpec(memory_space=pltpu.HBM)
SEM = pl.BlockSpec(memory_space=pltpu.SEMAPHORE)


def _peers():
    x, y, c = lax.axis_index("x"), lax.axis_index("y"), lax.axis_index("c")
    out = []
    for mask in range(1, 8):
        px = 1 - x if mask & 4 else x
        py = 1 - y if mask & 2 else y
        pc = 1 - c if mask & 1 else c
        out.append(((px, py, pc), 2 * px + py, pc, 4 * px + 2 * py + pc))
    return 4 * x + 2 * y + c, out


def _reduce_start(grads, lands, after, *, name):
    nt = len(grads)

    def body(*refs):
        ssems, rsems = refs[2 * nt + 1:3 * nt + 1], refs[3 * nt + 1:4 * nt + 1]
        g_out, l_out, token = refs[4 * nt + 1:5 * nt + 1], refs[5 * nt + 1:6 * nt + 1], refs[6 * nt + 1]
        me, peers = _peers()
        for t in range(nt):
            for k, (dev, chip, core, _) in enumerate(peers):
                _rcopy(g_out[t].at[chip, core], l_out[t].at[me], ssems[t].at[k], rsems[t].at[k], dev).start()
        token[...] = jnp.zeros_like(token)

    sems = [pltpu.SemaphoreType.DMA((7,))] * (2 * nt)
    out_shape = (sems + [pltpu.HBM(g.shape, g.dtype) for g in grads] + [pltpu.HBM(l.shape, l.dtype) for l in lands]
                 + [jax.ShapeDtypeStruct((8, LANES), F32)])
    res = pl.pallas_call(
        body, name=name, in_specs=[HBM_ONLY] * (2 * nt + 1),
        out_specs=[SEM] * (2 * nt) + [HBM_ONLY] * (2 * nt) + [pl.BlockSpec(memory_space=pltpu.VMEM)], out_shape=out_shape,
        input_output_aliases={t: 2 * nt + t for t in range(2 * nt)},
        compiler_params=pltpu.CompilerParams(has_side_effects=DATAFLOW),
    )(*[pltpu.with_memory_space_constraint(a, pltpu.HBM) for a in list(grads) + list(lands) + [after]])
    return res[:nt], res[nt:2 * nt], res[2 * nt:3 * nt], res[3 * nt:4 * nt], res[4 * nt]


def _reduce_wait(grads, lands, ssems, rsems, after, *, name):
    nt = len(grads)

    def body(*refs):
        ssem_refs, rsem_refs = refs[2 * nt:3 * nt], refs[3 * nt:4 * nt]
        g_out, l_out = refs[4 * nt + 1:5 * nt + 1], refs[5 * nt + 1:6 * nt + 1]
        me, peers = _peers()
        for t in range(nt):
            for k, (dev, chip, core, _) in enumerate(peers):
                _rcopy(g_out[t].at[chip, core], l_out[t].at[me], ssem_refs[t].at[k], rsem_refs[t].at[k], dev).wait_send()
        for t in range(nt):
            for k, (dev, _, _, idx) in enumerate(peers):
                slot = l_out[t].at[idx]
                _rcopy(slot, slot, ssem_refs[t].at[k], rsem_refs[t].at[k], dev).wait_recv()

    res = pl.pallas_call(
        body, name=name, in_specs=[HBM_ONLY] * (2 * nt) + [SEM] * (2 * nt) + [HBM_ONLY], out_specs=[HBM_ONLY] * (2 * nt),
        out_shape=[pltpu.HBM(a.shape, a.dtype) for a in list(grads) + list(lands)],
        input_output_aliases={t: t for t in range(2 * nt)},
        compiler_params=pltpu.CompilerParams(has_side_effects=DATAFLOW),
    )(*grads, *lands, *ssems, *rsems, pltpu.with_memory_space_constraint(after, pltpu.HBM))
    return list(res[:nt]), list(res[nt:])
DATAFLOW = pltpu.SideEffectType.DATAFLOW_SIDE_EFFECTING


def _gather_start(bufs, *, name):
    nt = len(bufs)

    def body(*refs):
        ssems, rsems, outs = refs[nt:2 * nt], refs[2 * nt:3 * nt], refs[3 * nt:4 * nt]
        x, y, c, q, sib, chips = _place()
        for t in range(nt):
            for j, (px, py) in enumerate(chips):
                mine = outs[t].at[q]
                _rcopy(mine, mine, ssems[t].at[j], rsems[t].at[j], (px, py, c)).start()

    sems = [pltpu.SemaphoreType.DMA((3,))] * (2 * nt)
    out_shape = sems + [pltpu.HBM(b.shape, b.dtype) for b in bufs]
    res = pl.pallas_call(
        body, name=name, in_specs=[HBM_ONLY] * nt, out_specs=[SEM] * (2 * nt) + [HBM_ONLY] * nt, out_shape=out_shape,
        input_output_aliases={t: 2 * nt + t for t in range(nt)},
        compiler_params=pltpu.CompilerParams(has_side_effects=DATAFLOW),
    )(*[pltpu.with_memory_space_constraint(b, pltpu.HBM) for b in bufs])
    return res[:nt], res[nt:2 * nt], res[2 * nt:]


def _gather_wait(bufs, ssems, rsems, after, *, name):
    nt = len(bufs)

    def body(*refs):
        ssem_refs, rsem_refs = refs[nt:2 * nt], refs[2 * nt:3 * nt]
        outs = refs[3 * nt + 1:]
        x, y, c, q, sib, chips = _place()
        for t in range(nt):
            for j, (px, py) in enumerate(chips):
                mine = outs[t].at[q]
                _rcopy(mine, mine, ssem_refs[t].at[j], rsem_refs[t].at[j], (px, py, c)).wait_send()
        for t in range(nt):
            for j, (px, py) in enumerate(chips):
                theirs = outs[t].at[2 * px + py]
                _rcopy(theirs, theirs, ssem_refs[t].at[j], rsem_refs[t].at[j], (px, py, c)).wait_recv()

    res = pl.pallas_call(
        body, name=name, in_specs=[HBM_ONLY] * nt + [SEM] * (2 * nt) + [HBM], out_specs=[HBM_ONLY] * nt,
        out_shape=[pltpu.HBM(b.shape, b.dtype) for b in bufs], input_output_aliases={t: t for t in range(nt)},
        compiler_params=pltpu.CompilerParams(has_side_effects=DATAFLOW),
    )(*bufs, *ssems, *rsems, after)
    return list(res)


def _sibling_share(bufs, layout, *, name):
    no = len(bufs)
    nt = len(layout)

    def body(*refs):
        outs = refs[no:2 * no]
        ssem, rsem = refs[2 * no:]
        x, y, c, q, sib, chips = _place()

        def slot(t, half):
            o, lead = layout[t]
            return outs[o].at[half] if lead is None else outs[o].at[lead, half]

        sends = []
        for t in range(nt):
            cp = _rcopy(slot(t, c), slot(t, c), ssem.at[t], rsem.at[t], sib)
            cp.start()
            sends.append(cp)
        for t in range(nt):
            _rcopy(slot(t, 1 - c), slot(t, 1 - c), ssem.at[t], rsem.at[t], sib).wait_recv()
        for cp in sends:
            cp.wait_send()

    out_shape = [jax.ShapeDtypeStruct(b.shape, b.dtype) for b in bufs]
    return pl.pallas_call(
        body, name=name, in_specs=[HBM] * no, out_specs=[HBM] * no, out_shape=out_shape,
        input_output_aliases={o: o for o in range(no)},
        scratch_shapes=[pltpu.SemaphoreType.DMA((nt,)), pltpu.SemaphoreType.DMA((nt,))],
    )(*bufs)


def _small_tail(local, params, *, name):
    (cwa, cba, ga, ba, gb, bb, dws, dsb, dbq, dsk, cwf0, cbf0, cwf1, cbf1,
     g00, g01, g10, g11, b00, b01, b10, b11, err) = local
    n_local = len(local)
    kw, wa = cwa.shape
    ng = dws.shape[0]
    nqkv = dbq.shape[1]
    nsk = dsk.shape[1]
    f = cwf0.shape[2]
    dm = err.shape[1]
    row_vec = 8 * (-(-kw // 8))
    shapes = [(row_vec + 8, wa), (ng * B_CHUNK + 8, B_CHUNK), (8, nqkv), (2, 2, 8, f), (16, dm)]
    n_grp = len(shapes)
    flat_params = [a for triple in params for a in triple]
    n_par = len(params)

    def reduce_body(*refs):
        loc = refs[:n_local]
        tot = refs[n_local:n_local + n_grp]
        scr = refs[n_local + n_grp:]
        grp, from_sib, pair, gath = (scr[k * n_grp:(k + 1) * n_grp] for k in range(4))
        ssem1, rsem1, ssem2, rsem2 = scr[4 * n_grp:]
        x, y, core, q, sib, chips = _place()

        for gr in grp:
            gr[...] = jnp.zeros_like(gr)
        a, b, c, dd, e = grp
        a[0:kw, :] = loc[0][...]
        for k in range(5):
            a[row_vec + k:row_vec + k + 1, :] = loc[1 + k][...]
        for g in range(ng):
            b[g * B_CHUNK:(g + 1) * B_CHUNK, :] = loc[6][g]
        b[ng * B_CHUNK:ng * B_CHUNK + ng, :] = loc[7][...]
        c[0:1, :] = loc[8][...]
        c[1:2, 0:nsk] = loc[9][...]
        for l in range(2):
            for s in range(2):
                dd[l, s, 0:3, :] = loc[10 + 2 * l][s]
                dd[l, s, 3:4, :] = loc[11 + 2 * l][s]
        for k in range(9):
            e[k:k + 1, :] = loc[14 + k][...]

        sends = []
        for gi in range(n_grp):
            cp = _rcopy(grp[gi], from_sib[gi], ssem1.at[gi], rsem1.at[gi], sib)
            cp.start()
            sends.append(cp)
        for gi in range(n_grp):
            _rcopy(grp[gi], from_sib[gi], ssem1.at[gi], rsem1.at[gi], sib).wait_recv()
            both = grp[gi][...] + from_sib[gi][...]
            pair[gi][...] = both
            gath[gi][q] = both
            for j, (px, py) in enumerate(chips):
                cp = _rcopy(pair[gi], gath[gi].at[q], ssem2.at[gi, j], rsem2.at[gi, j], (px, py, core))
                cp.start()
                sends.append(cp)
        for gi in range(n_grp):
            for j, (px, py) in enumerate(chips):
                slot = gath[gi].at[2 * px + py]
                _rcopy(slot, slot, ssem2.at[gi, j], rsem2.at[gi, j], (px, py, core)).wait_recv()
            acc = gath[gi][0]
            for k in range(1, 4):
                acc = acc + gath[gi][k]
            tot[gi][...] = acc
        for cp in sends:
            cp.wait_send()

    vm = pl.BlockSpec(memory_space=pltpu.VMEM)
    scratch = ([pltpu.VMEM(s, F32) for s in shapes] * 3 + [pltpu.VMEM((4,) + s, F32) for s in shapes]
               + [pltpu.SemaphoreType.DMA((n_grp,)), pltpu.SemaphoreType.DMA((n_grp,)),
                  pltpu.SemaphoreType.DMA((n_grp, 3)), pltpu.SemaphoreType.DMA((n_grp, 3))])
    totals = pl.pallas_call(
        reduce_body, name=name + "_reduce", in_specs=[vm] * n_local, out_specs=[vm] * n_grp,
        out_shape=[jax.ShapeDtypeStruct(s, F32) for s in shapes], scratch_shapes=scratch,
        compiler_params=pltpu.CompilerParams(vmem_limit_bytes=VMEM_LIMIT),
    )(*local)

    def adamw_body(*refs):
        ta, tb, tc, td, te = refs[:n_grp]
        par = refs[n_grp:n_grp + 3 * n_par]
        outs = refs[n_grp + 3 * n_par:n_grp + 7 * n_par]
        loss_ref = refs[n_grp + 7 * n_par]
        q = 2 * lax.axis_index("x") + lax.axis_index("y")

        def mine(piece):
            out = piece(0)
            for k in range(1, 4):
                out = jnp.where(q == k, piece(k), out)
            return out

        def update(p, grad, index=None):
            at = (lambda r: r[...]) if index is None else (lambda r: r[index])
            w_ref, m_ref, v_ref = par[3 * p:3 * p + 3]
            g_ref, d_ref, nm_ref, nv_ref = outs[4 * p:4 * p + 4]
            delta, nm, nv = _adamw_math(at(w_ref), grad, at(m_ref), at(v_ref))
            for r, val in ((g_ref, grad), (d_ref, delta), (nm_ref, nm), (nv_ref, nv)):
                if index is None:
                    r[...] = val
                else:
                    r[index] = val

        wq = wa // 4
        update(0, mine(lambda k: ta[0:kw, k * wq:(k + 1) * wq]), (0,))
        for k in range(5):
            update(1 + k, ta[row_vec + k:row_vec + k + 1, :])
        for g in range(ng):
            update(6, tb[g * B_CHUNK:(g + 1) * B_CHUNK, :], (0, g))
        update(7, tb[ng * B_CHUNK:ng * B_CHUNK + ng, :], (0,))
        nq4 = nqkv // 4
        update(8, mine(lambda k: tc[0:1, k * nq4:(k + 1) * nq4]))
        update(9, tc[1:2, 0:nsk])
        fh = f // 2
        for l in range(2):
            update(10, mine(lambda k: td[l, k // 2, 0:3, (k % 2) * fh:(k % 2 + 1) * fh]), (l,))
            update(11, jnp.concatenate([td[l, 0, 3:4, :], td[l, 1, 3:4, :]], axis=1), (slice(l, l + 1),))
        dq4 = dm // 4
        for i in range(2):
            for j in range(2):
                for p, base in ((12, 0), (13, 4)):
                    row = base + 2 * i + j
                    update(p, mine(lambda k: te[row:row + 1, k * dq4:(k + 1) * dq4]), (i, slice(j, j + 1)))
        loss_ref[...] = (0.5 / dm) * jnp.sum(te[8:9, :], axis=1, keepdims=True)

    out_shape = []
    for w, _, _ in params:
        out_shape += [jax.ShapeDtypeStruct(w.shape, F32)] * 4
    out_shape.append(jax.ShapeDtypeStruct((1, 1), F32))
    res = pl.pallas_call(
        adamw_body, name=name + "_adamw", in_specs=[vm] * (n_grp + 3 * n_par), out_specs=[vm] * len(out_shape),
        out_shape=out_shape, compiler_params=pltpu.CompilerParams(vmem_limit_bytes=VMEM_LIMIT),
    )(*totals, *flat_params)
    return [res[4 * p:4 * p + 4] for p in range(n_par)], res[-1]


def _pack(arrays, rows_multiple):
    flat = jnp.concatenate([a.reshape(-1) for a in arrays])
    rows = -(-flat.shape[0] // LANES)
    rows = -(-rows // rows_multiple) * rows_multiple
    flat = jnp.pad(flat, (0, rows * LANES - flat.shape[0]))
    return flat.reshape(rows, LANES)


def _unshard_cols(stacked):
    moved = jnp.moveaxis(stacked, 0, -2)
    return moved.reshape(moved.shape[:-2] + (4 * stacked.shape[-1],))


def kernel(x, ab_w_in, a_conv_w, a_conv_b, a_norm_g, a_norm_b, b_norm_g, b_norm_b, b_spatial_w, b_spatial_b, ab_w_out, c_w_qkv, c_b_qkv, c_sinks, c_w_o, ffn_w_up, ffn_conv_w, ffn_conv_b, ffn_w_down, ln_g, ln_b, loss_target, m_ab_w_in, m_a_conv_w, m_a_conv_b, m_a_norm_g, m_a_norm_b, m_b_norm_g, m_b_norm_b, m_b_spatial_w, m_b_spatial_b, m_ab_w_out, m_c_w_qkv, m_c_b_qkv, m_c_sinks, m_c_w_o, m_ffn_w_up, m_ffn_conv_w, m_ffn_conv_b, m_ffn_w_down, m_ln_g, m_ln_b, v_ab_w_in, v_a_conv_w, v_a_conv_b, v_a_norm_g, v_a_norm_b, v_b_norm_g, v_b_norm_b, v_b_spatial_w, v_b_spatial_b, v_ab_w_out, v_c_w_qkv, v_c_b_qkv, v_c_sinks, v_c_w_o, v_ffn_w_up, v_ffn_conv_w, v_ffn_conv_b, v_ffn_w_down, v_ln_g, v_ln_b):
    rows, d = x.shape[1], x.shape[2]
    depth = ln_g.shape[0]
    assert depth == 2 and x.shape[0] == 1
    alpha = (2.0 * depth) ** 0.25
    f = ffn_w_down.shape[1] * 4
    n_q = c_sinks.shape[1]
    q_idx = 2 * lax.axis_index("x") + lax.axis_index("y")
    c_idx = lax.axis_index("c")
    xs, tgt = x[0], loss_target[0]

    def own_slot(part):
        buf = lax.empty((4,) + part.shape, part.dtype)
        return lax.dynamic_update_slice(buf, part[None], (q_idx, 0, 0, 0))

    def halves(wm):
        return own_slot(wm.astype(BF16).reshape((2, wm.shape[0] // 2) + wm.shape[1:]))

    small_sharded = [a_conv_w[0], c_b_qkv[0], ffn_conv_w, ln_g, ln_b]
    small_pack = _pack(small_sharded, 16)
    bufs = [halves(ab_w_in[0]), own_slot(small_pack.reshape(2, small_pack.shape[0] // 2, LANES)), halves(ab_w_out[0]),
            halves(ffn_w_up[0]), halves(ffn_w_down[0]), halves(c_w_qkv[0]), halves(c_w_o[0]),
            halves(ffn_w_up[1]), halves(ffn_w_down[1])]
    ssems, rsems, started = _gather_start(bufs, name="gather_start")

    def arrive(idx, after, tag):
        got = _gather_wait([started[i] for i in idx], [ssems[i] for i in idx], [rsems[i] for i in idx], after,
                           name=f"gather_wait_{tag}")
        return [g.reshape(4, 2 * g.shape[2], g.shape[3]) for g in got]

    w_in, small_all = arrive([0, 1], xs, "in")
    small_all = small_all.reshape(4, -1)
    sh_shapes = [s.shape for s in small_sharded]
    pieces, pos = [], 0
    for s in sh_shapes:
        n = math.prod(s)
        pieces.append(_unshard_cols(small_all[:, pos:pos + n].reshape((4,) + s)))
        pos += n
    conv_w_a, b_qkv, conv_w_f, ln_gf, ln_bf = pieces

    tril = jnp.tril(jnp.ones((B_CHUNK, B_CHUNK), F32))
    ws = (b_spatial_w[0] * tril).astype(BF16)
    wst = jnp.swapaxes(ws, 1, 2)
    sbb = jnp.broadcast_to(b_spatial_b[0][:, :, None], b_spatial_w[0].shape)
    mix_vecs = [a_conv_b, a_norm_g, a_norm_b, b_norm_g, b_norm_b]
    cw_f = [jnp.swapaxes(conv_w_f[l].reshape(3, 2, f), 0, 1) for l in range(depth)]
    cb_f = [ffn_conv_b[l].reshape(2, 1, f) for l in range(depth)]
    lng = lambda i, j: ln_gf[i, j].reshape(1, d)
    lnb = lambda i, j: ln_bf[i, j].reshape(1, d)
    sinks = c_sinks[0]

    w_up, w_down = [None, None], [None, None]

    def ffn_fwd(xin, l):
        w_up[l], = arrive([3 + 4 * l], xin, f"up{l}")
        hf, fact = _ffn_up_fwd(xin, w_up[l], cw_f[l], cb_f[l], name=f"ffn{l}_up")
        w_down[l] = arrive([4 + 4 * l], fact, f"down{l}")[0].reshape(-1, d)
        out = _matmul(fact, w_down[l], name=f"ffn{l}_down", tm=512, tn=1024, tk=2816)
        return hf, fact, out

    h0 = _matmul(xs, w_in, name="mix_in", tm=1024, tn=512, tk=1024, out_stack=4)
    ab, a2 = _mixer_fwd(h0, conv_w_a, *mix_vecs, ws, sbb, name="mix_mid")
    w_out = arrive([2], ab, "out")[0].reshape(-1, d)
    mix = _matmul(ab, w_out, name="mix_out", tm=1024, tn=1024, tk=1024)
    x1 = _add_ln_fwd(xs, mix, lng(0, 0), lnb(0, 0), alpha, name="ln00")
    hf0, f0, ffn0 = ffn_fwd(x1, 0)
    x2 = _add_ln_fwd(x1, ffn0, lng(0, 1), lnb(0, 1), alpha, name="ln01")
    w_qkv = _unshard_cols(arrive([5], x2, "qkv")[0])
    qkv = _matmul(x2, w_qkv, name="att_qkv", tm=1024, tn=w_qkv.shape[1], tk=1024, bias=b_qkv.reshape(1, -1))
    ao, lse = _attn_fwd(qkv, sinks, name="att_core")
    w_o = arrive([6], ao, "o")[0].reshape(-1, d)
    att = _matmul(ao, w_o, name="att_out", tm=1024, tn=1024, tk=1024)
    x3 = _add_ln_fwd(x2, att, lng(1, 0), lnb(1, 0), alpha, name="ln10")
    hf1, f1, ffn1 = ffn_fwd(x3, 1)
    x4 = _add_ln_fwd(x3, ffn1, lng(1, 1), lnb(1, 1), alpha, name="ln11")
    sq_err, dy = _loss_and_grad(x4, tgt, name="loss")

    def owner_view(g):
        if g.ndim == 3:
            return g.reshape(4, 2, g.shape[1] // 2, g.shape[2])
        return g.reshape(4, 2, g.shape[0] // 8, g.shape[1])

    in_flight = []

    def send_grads(tag, grads, after):
        lands = [lax.empty((8,) + g.shape[2:], BF16) for g in grads]
        ss, rs, g_thru, l_thru, token = _reduce_start(grads, lands, after, name=f"reduce_start_{tag}")
        in_flight.append((tag, g_thru, l_thru, ss, rs))
        return token[0:1, 0:1]

    def ffn_bwd(dz, xin, hf, fact, l):
        d_wdown = _matmul(fact, dz, name=f"ffn{l}_down_dw", ta=True, tm=1408, tn=1024, tk=2048, out_dtype=BF16)
        dfa = _matmul(dz, w_down[l], name=f"ffn{l}_down_dx", tb=True, tm=1024, tn=1408, tk=1024)
        dx_parts, d_wup, dcw, dcb = _ffn_up_bwd(hf, dfa, xin, w_up[l], cw_f[l], cb_f[l], name=f"ffn{l}_up_bwd")
        tok = send_grads(f"ffn{l}", [owner_view(d_wup), owner_view(d_wdown)], dcb)
        return [(dx_parts, 1.0), (dz, alpha)], dcw, dcb, tok

    dz, dg11, db11 = _add_ln_bwd([(dy, 1.0)], x3, ffn1, lng(1, 1), alpha, name="ln11_bwd")
    dx3, dcw1, dcb1, tok = ffn_bwd(dz, x3, hf1, f1, 1)
    dz, dg10, db10 = _add_ln_bwd(dx3, x2, att, lng(1, 0) + tok, alpha, name="ln10_bwd")
    d_wo = _matmul(ao, dz, name="att_out_dw", ta=True, tm=1024, tn=1024, tk=1024, out_dtype=BF16)
    dao = _matmul(dz, w_o, name="att_out_dx", tb=True, tm=1024, tn=1024, tk=1024)
    dq, dkc, dkp, d_sinks = _attn_bwd(qkv, dao, lse, sinks, name="att_core_bwd")
    dqkv, d_bqkv = _dqkv_assemble(dq, dkc, dkp, name="att_dqkv")
    d_wqkv = _matmul(x2, dqkv, name="att_qkv_dw", ta=True, tm=1024, tn=dqkv.shape[1], tk=1024, out_dtype=BF16)
    d_wqkv_st = jnp.moveaxis(d_wqkv.reshape(d_wqkv.shape[0], 4, -1), 1, 0)
    tok = send_grads("att", [owner_view(d_wqkv_st), owner_view(d_wo)], d_bqkv)
    dx2 = _matmul(dqkv, w_qkv, name="att_qkv_dx", tb=True, tm=1024, tn=1024, tk=dqkv.shape[1], addend=(dz, alpha))
    dz, dg01, db01 = _add_ln_bwd([(dx2, 1.0)], x1, ffn0, lng(0, 1) + tok, alpha, name="ln01_bwd")
    dx1, dcw0, dcb0, tok = ffn_bwd(dz, x1, hf0, f0, 0)
    dz, dg00, db00 = _add_ln_bwd(dx1, xs, mix, lng(0, 0) + tok, alpha, name="ln00_bwd")
    d_wout = _matmul(ab, dz, name="mix_out_dw", ta=True, tm=1024, tn=1024, tk=1024, out_dtype=BF16)
    dab = _matmul(dz, w_out, name="mix_out_dx", tb=True, tm=1024, tn=1024, tk=1024)
    grad_x, d_win, d_cwa, d_cba, d_ga, d_ba, d_gb, d_bb, d_ws, d_sb = _mixer_bwd(
        h0, a2, dab, xs, w_in, dz, alpha, conv_w_a, *mix_vecs[1:], ws, wst, sbb, tril, name="mix_bwd")

    small_w = [a_conv_w, a_conv_b, a_norm_g, a_norm_b, b_norm_g, b_norm_b, b_spatial_w, b_spatial_b, c_b_qkv,
               c_sinks, ffn_conv_w, ffn_conv_b, ln_g, ln_b]
    small_m = [m_a_conv_w, m_a_conv_b, m_a_norm_g, m_a_norm_b, m_b_norm_g, m_b_norm_b, m_b_spatial_w, m_b_spatial_b,
               m_c_b_qkv, m_c_sinks, m_ffn_conv_w, m_ffn_conv_b, m_ln_g, m_ln_b]
    small_v = [v_a_conv_w, v_a_conv_b, v_a_norm_g, v_a_norm_b, v_b_norm_g, v_b_norm_b, v_b_spatial_w, v_b_spatial_b,
               v_c_b_qkv, v_c_sinks, v_ffn_conv_w, v_ffn_conv_b, v_ln_g, v_ln_b]
    local = [d_cwa, d_cba, d_ga, d_ba, d_gb, d_bb, d_ws, d_sb, d_bqkv, d_sinks, dcw0, dcb0, dcw1, dcb1,
             dg00, dg01, dg10, dg11, db00, db01, db10, db11, sq_err]
    small_out, loss = _small_tail(local, list(zip(small_w, small_m, small_v)), name="small_tail")
    loss = loss[0, 0]
    small_g = [o[0] for o in small_out]
    sm_delta = [o[1] for o in small_out]
    sm_m = [o[2] for o in small_out]
    sm_v = [o[3] for o in small_out]

    place = jnp.stack([q_idx, c_idx, 4 * lax.axis_index("x") + 2 * lax.axis_index("y") + c_idx]).astype(jnp.int32)
    where = {"mix": [(0, None), (1, None)], "att": [(2, None), (3, None)], "ffn0": [(4, 0), (5, 0)], "ffn1": [(4, 1), (5, 1)]}
    big_w = [ab_w_in, ab_w_out, c_w_qkv, c_w_o, ffn_w_up, ffn_w_down]
    big_m = [m_ab_w_in, m_ab_w_out, m_c_w_qkv, m_c_w_o, m_ffn_w_up, m_ffn_w_down]
    big_v = [v_ab_w_in, v_ab_w_out, v_c_w_qkv, v_c_w_o, v_ffn_w_up, v_ffn_w_down]
    big_out = [None] * 6

    def finish(tags, after, label):
        bufs, layout = {}, []
        for tag, g_thru, l_thru, ss, rs in in_flight:
            if tag not in tags:
                continue
            own, landed = _reduce_wait(g_thru, l_thru, ss, rs, after, name=f"reduce_wait_{tag}")
            for k, (o, lead) in enumerate(where[tag]):
                piece = own[k].shape[2:]
                shape = (2,) + piece if lead is None else (2, 2) + piece
                bufs[o] = _octo_sum(own[k], landed[k], place, bufs.get(o), (lead, shape), name=f"reduce_sum_{tag}{k}")
                layout.append((o, lead))
        order = sorted(bufs)
        shared = _sibling_share([bufs[o] for o in order], [(order.index(o), lead) for o, lead in layout],
                                name=f"reduce_share_{label}")
        for o, g in zip(order, shared):
            w = big_w[o]
            two_d = lambda a: a.reshape(-1, a.shape[-1])
            outs = _adamw(two_d(w), two_d(g), two_d(big_m[o]), two_d(big_v[o]), name=f"adamw_big{o}")
            big_out[o] = [r.reshape(w.shape) for r in outs]
        return big_out[order[-1]][0]

    tok = send_grads("mix", [owner_view(d_win), owner_view(d_wout)], after=sm_delta[0])
    done = finish(("ffn1", "att", "ffn0"), sm_delta[6] + tok, "early")
    finish(("mix",), done, "mix")

    order_big = {0: 0, 9: 1, 10: 2, 13: 3, 14: 4, 17: 5}
    order_small = {1: 0, 2: 1, 3: 2, 4: 3, 5: 4, 6: 5, 7: 6, 8: 7, 11: 8, 12: 9, 15: 10, 16: 11, 18: 12, 19: 13}
    grads, deltas, new_m, new_v = [], [], [], []
    for pos_w in range(20):
        if pos_w in order_big:
            t = order_big[pos_w]
            grads.append(big_out[t][3])
            deltas.append(big_out[t][0])
            new_m.append(big_out[t][1])
            new_v.append(big_out[t][2])
        else:
            t = order_small[pos_w]
            grads.append(small_g[t])
            deltas.append(sm_delta[t])
            new_m.append(sm_m[t])
            new_v.append(sm_v[t])
    return (loss, grad_x[None], *grads, *deltas, *new_m, *new_v)
```

```python
import math

import jax
import jax.numpy as jnp
from jax import lax
from jax.experimental import pallas as pl
from jax.experimental.pallas import tpu as pltpu

F32 = jnp.float32
BF16 = jnp.bfloat16
MESH = pl.DeviceIdType.MESH

LN_EPS = 1e-5
HEAD_DIM = 64
ATT_BLOCK = 128
Q_PER_KV = 8
A_KERNEL = 31
CONV_HALO = 32
FFN_HALO = 8
BF16_ROWS = 16
B_CHUNK = 128
LANES = 128
MXU_WIDTH = 256
GELU_C = math.sqrt(2.0 / math.pi)
ADAM_LR = 0.001
ADAM_B1 = 0.9
ADAM_B2 = 0.999
ADAM_EPS = 1e-08
ADAM_WD = 0.01
ADAM_STEP = 10
VMEM_LIMIT = 56 * 1024 * 1024


def _cp(*dims):
    return pltpu.CompilerParams(dimension_semantics=dims, vmem_limit_bytes=VMEM_LIMIT)


def _pick(n, prefs):
    for p in prefs:
        if n % p == 0:
            return p
    return n


def _sig(x):
    return 1.0 / (1.0 + jnp.exp(-x))


def _gelu(x):
    t = jnp.tanh(GELU_C * (x + 0.044715 * (x * x * x)))
    return x * (0.5 * (1.0 + t)), t


def _gelu_grad(x, t):
    return 0.5 * (1.0 + t) + 0.5 * x * (1.0 - t * t) * (GELU_C * (1.0 + 3.0 * 0.044715 * x * x))


def _ln_stats(z):
    mu = jnp.mean(z, axis=-1, keepdims=True)
    zc = z - mu
    var = jnp.mean(zc * zc, axis=-1, keepdims=True)
    rstd = lax.rsqrt(var + LN_EPS)
    return zc * rstd, rstd


def _ln_bwd(dxh, xh, rstd):
    return rstd * (dxh - jnp.mean(dxh, axis=-1, keepdims=True) - xh * jnp.mean(dxh * xh, axis=-1, keepdims=True))


def _rowsum(a):
    return jnp.sum(a, axis=0, keepdims=True)


def _lshape(a):
    return (a.shape[0], a.shape[1]) if a.ndim == 2 else (a.shape[1], a.shape[0] * a.shape[2])


def _spec2(arr, blk_r, blk_c, ridx, cidx):
    if len(arr.shape) == 2:
        return pl.BlockSpec((blk_r, blk_c), lambda i, j, k: (ridx(i, j, k), cidx(i, j, k)))
    per = arr.shape[2] // blk_c
    assert arr.shape[2] % blk_c == 0
    return pl.BlockSpec((None, blk_r, blk_c), lambda i, j, k: (cidx(i, j, k) // per, ridx(i, j, k), cidx(i, j, k) % per))


def _matmul(a, b, *, name, ta=False, tb=False, tm, tn, tk, out_dtype=F32, out_stack=None, bias=None, addend=None):
    ar, ac = _lshape(a)
    br, bc = _lshape(b)
    m, kdim = (ac, ar) if ta else (ar, ac)
    n = br if tb else bc
    assert (bc if tb else br) == kdim
    tm, tn, tk = min(tm, m), min(tn, n), min(tk, kdim)
    assert m % tm == 0 and n % tn == 0 and kdim % tk == 0, (name, m, n, kdim, tm, tn, tk)
    nk = kdim // tk
    gi, gj, gk = (lambda i, j, k: i), (lambda i, j, k: j), (lambda i, j, k: k)
    a_spec = _spec2(a, tk, tm, gk, gi) if ta else _spec2(a, tm, tk, gi, gk)
    b_spec = _spec2(b, tn, tk, gj, gk) if tb else _spec2(b, tk, tn, gk, gj)
    if out_stack is None:
        out_sds = jax.ShapeDtypeStruct((m, n), out_dtype)
    else:
        out_sds = jax.ShapeDtypeStruct((out_stack, m, n // out_stack), out_dtype)
    o_spec = _spec2(out_sds, tm, tn, gi, gj)
    in_specs = [a_spec, b_spec]
    args = [a, b]
    if bias is not None:
        in_specs.append(pl.BlockSpec((1, tn), lambda i, j, k: (0, j)))
        args.append(bias)
    scale = None
    if addend is not None:
        add_arr, scale = addend
        in_specs.append(pl.BlockSpec((tm, tn), lambda i, j, k: (i, j)))
        args.append(add_arr)
    use_acc = nk > 1 and out_dtype != F32
    dn = (((0 if ta else 1,), (1 if tb else 0,)), ((), ()))

    def body(*refs):
        a_ref, b_ref = refs[0], refs[1]
        pos = 2
        bias_ref = add_ref = None
        if bias is not None:
            bias_ref = refs[pos]
            pos += 1
        if addend is not None:
            add_ref = refs[pos]
            pos += 1
        o_ref = refs[pos]
        acc_ref = refs[pos + 1] if use_acc else o_ref
        p = lax.dot_general(a_ref[...].astype(BF16), b_ref[...].astype(BF16), dn, preferred_element_type=F32)

        def finish(val):
            if bias_ref is not None:
                val = val + bias_ref[...]
            if add_ref is not None:
                val = val + scale * add_ref[...]
            return val.astype(out_dtype)

        if nk == 1:
            o_ref[...] = finish(p)
        else:
            k = pl.program_id(2)

            @pl.when(k == 0)
            def _():
                acc_ref[...] = p

            @pl.when(k > 0)
            def _():
                acc_ref[...] += p

            if use_acc or bias_ref is not None or add_ref is not None:
                @pl.when(k == nk - 1)
                def _():
                    o_ref[...] = finish(acc_ref[...])

    return pl.pallas_call(
        body, name=name, grid=(m // tm, n // tn, nk), in_specs=in_specs, out_specs=o_spec, out_shape=out_sds,
        scratch_shapes=[pltpu.VMEM((tm, tn), F32)] if use_acc else [],
        compiler_params=_cp("parallel", "parallel", "arbitrary"),
    )(*args)


def _add_ln_fwd(x, s, g, b, alpha, *, name):
    rows, d = x.shape
    t = _pick(rows, (512, 256))

    def body(x_ref, s_ref, g_ref, b_ref, y_ref):
        xh, _ = _ln_stats(alpha * x_ref[...] + s_ref[...])
        y_ref[...] = xh * g_ref[...] + b_ref[...]

    row = pl.BlockSpec((t, d), lambda i: (i, 0))
    vec = pl.BlockSpec((1, d), lambda i: (0, 0))
    return pl.pallas_call(body, name=name, grid=(rows // t,), in_specs=[row, row, vec, vec], out_specs=row,
                          out_shape=jax.ShapeDtypeStruct((rows, d), F32), compiler_params=_cp("parallel"))(x, s, g, b)


def _add_ln_bwd(dy_terms, x, s, g, alpha, *, name):
    rows, d = x.shape
    t = _pick(rows, (512, 256))
    nterm = len(dy_terms)
    scales = [sc for _, sc in dy_terms]
    ranks = [a.ndim for a, _ in dy_terms]

    def body(*refs):
        dy_refs = refs[:nterm]
        x_ref, s_ref, g_ref, dz_ref, dg_ref, db_ref = refs[nterm:]

        @pl.when(pl.program_id(0) == 0)
        def _():
            dg_ref[...] = jnp.zeros_like(dg_ref)
            db_ref[...] = jnp.zeros_like(db_ref)

        dyv = None
        for r, sc, rank in zip(dy_refs, scales, ranks):
            slabs = [r[...]] if rank == 2 else [r[p] for p in range(r.shape[0])]
            for v in slabs:
                v = v if sc == 1.0 else sc * v
                dyv = v if dyv is None else dyv + v
        xh, rstd = _ln_stats(alpha * x_ref[...] + s_ref[...])
        dz_ref[...] = _ln_bwd(dyv * g_ref[...], xh, rstd)
        dg_ref[...] += _rowsum(dyv * xh)
        db_ref[...] += _rowsum(dyv)

    row = pl.BlockSpec((t, d), lambda i: (i, 0))
    vec = pl.BlockSpec((1, d), lambda i: (0, 0))
    vsds = jax.ShapeDtypeStruct((1, d), F32)
    dy_specs = [row if a.ndim == 2 else pl.BlockSpec((a.shape[0], t, d), lambda i: (0, i, 0)) for a, _ in dy_terms]
    return pl.pallas_call(body, name=name, grid=(rows // t,), in_specs=dy_specs + [row, row, vec], out_specs=[row, vec, vec],
                          out_shape=[jax.ShapeDtypeStruct((rows, d), F32), vsds, vsds],
                          compiler_params=_cp("arbitrary"))(*[a for a, _ in dy_terms], x, s, g)


def _add_ln_loss(x, s, g, b, tgt, alpha, *, name):
    rows, d = x.shape
    t = _pick(rows, (512, 256))

    def body(x_ref, s_ref, g_ref, b_ref, t_ref, l_ref, dy_ref):
        @pl.when(pl.program_id(0) == 0)
        def _():
            l_ref[...] = jnp.zeros_like(l_ref)

        xh, _ = _ln_stats(alpha * x_ref[...] + s_ref[...])
        e = (xh * g_ref[...] + b_ref[...]) - t_ref[...]
        l_ref[...] += _rowsum(e * e)
        dy_ref[...] = e * (1.0 / d)

    row = pl.BlockSpec((t, d), lambda i: (i, 0))
    vec = pl.BlockSpec((1, d), lambda i: (0, 0))
    return pl.pallas_call(body, name=name, grid=(rows // t,), in_specs=[row, row, vec, vec, row], out_specs=[vec, row],
                          out_shape=[jax.ShapeDtypeStruct((1, d), F32), jax.ShapeDtypeStruct((rows, d), F32)],
                          compiler_params=_cp("arbitrary"))(x, s, g, b, tgt)


def _col_blocks(width, step):
    return [slice(pos, min(pos + step, width)) for pos in range(0, width, step)]


def _conv3(e, w, b):
    r1 = pltpu.roll(e, 1, 0)
    r2 = pltpu.roll(e, 2, 0)
    return w[0:1, :] * r2 + w[1:2, :] * r1 + w[2:3, :] * e + b, (r2, r1, e)


def _ffn_up_fwd(x, w_up, cw, cb, *, name):
    rows, d = x.shape
    nq, _, tc = w_up.shape
    nj = nq // 2
    f = tc * nj
    tm = _pick(rows, (512, 256))
    blocks = _col_blocks(tc, tc)

    def body(x_ref, wg_ref, wv_ref, cw_ref, cb_ref, hf_ref, f_ref, prev_ref):
        @pl.when(pl.program_id(1) == 0)
        def _():
            prev_ref[...] = jnp.zeros_like(prev_ref)

        xb = x_ref[...].astype(BF16)
        for cs in blocks:
            hc = []
            for s, w_ref in ((0, wg_ref), (1, wv_ref)):
                h = jnp.dot(xb, w_ref[:, cs], preferred_element_type=F32)
                hf_ref[s, :, cs] = h
                e = jnp.concatenate([prev_ref[s, :, cs], h], axis=0)
                prev_ref[s, :, cs] = h[tm - FFN_HALO:]
                y, _ = _conv3(e, cw_ref[s, :, cs], cb_ref[s, :, cs])
                hc.append(y[FFN_HALO:])
            gl, _ = _gelu(hc[0])
            f_ref[:, cs] = (gl * hc[1]).astype(BF16)

    in_specs = [
        pl.BlockSpec((tm, d), lambda j, i: (i, 0)),
        pl.BlockSpec((None, d, tc), lambda j, i: (j, 0, 0)),
        pl.BlockSpec((None, d, tc), lambda j, i: (nj + j, 0, 0)),
        pl.BlockSpec((2, 3, tc), lambda j, i: (0, 0, j)),
        pl.BlockSpec((2, 1, tc), lambda j, i: (0, 0, j)),
    ]
    out_specs = [pl.BlockSpec((2, tm, tc), lambda j, i: (0, i, j)), pl.BlockSpec((tm, tc), lambda j, i: (i, j))]
    out_shape = [jax.ShapeDtypeStruct((2, rows, f), F32), jax.ShapeDtypeStruct((rows, f), BF16)]
    return pl.pallas_call(body, name=name, grid=(nj, rows // tm), in_specs=in_specs, out_specs=out_specs, out_shape=out_shape,
                          scratch_shapes=[pltpu.VMEM((2, FFN_HALO, tc), F32)],
                          compiler_params=_cp("parallel", "arbitrary"))(x, w_up, w_up, cw, cb)


def _ffn_up_bwd(hf, df, x, w_up, cw, cb, *, name):
    _, rows, f = hf.shape
    d = x.shape[1]
    nq, _, tc = w_up.shape
    nj = nq // 2
    tm = _pick(rows, (512, 256))
    hb = tm // FFN_HALO
    once = pl.Buffered(1)
    ni = rows // tm
    last_blk = rows // FFN_HALO - 1
    ext = tm + 2 * FFN_HALO
    tile = slice(FFN_HALO, FFN_HALO + tm)
    blocks = _col_blocks(tc, MXU_WIDTH)

    def body(h_ref, hp_ref, hn_ref, d_ref, dn_ref, x_ref, wg_ref, wv_ref, cw_ref, cb_ref, dx_ref, dw_out_ref, dcw_ref, dcb_ref,
             dw_ref):
        i = pl.program_id(1)
        first = i == 0
        last = i == ni - 1

        @pl.when(first)
        def _():
            dw_ref[...] = jnp.zeros_like(dw_ref)
            dcw_ref[...] = jnp.zeros_like(dcw_ref)
            dcb_ref[...] = jnp.zeros_like(dcb_ref)

        xt = x_ref[...].astype(BF16).T
        dx = None
        for cs in blocks:
            wc = cs.stop - cs.start
            d_next = dn_ref[:, cs].astype(F32)[0:FFN_HALO]
            de = jnp.concatenate([jnp.zeros((FFN_HALO, wc), F32), d_ref[:, cs].astype(F32), jnp.where(last, 0.0, d_next)], axis=0)
            taps, hc = [], []
            for s in range(2):
                e = jnp.concatenate([jnp.where(first, 0.0, hp_ref[s, :, cs]), h_ref[s, :, cs], hn_ref[s, :, cs]], axis=0)
                y, tp = _conv3(e, cw_ref[s, :, cs], cb_ref[s, :, cs])
                hc.append(y)
                taps.append(tp)
            gl, th = _gelu(hc[0])
            dhc = (de * hc[1] * _gelu_grad(hc[0], th), de * gl)
            for s, w_ref in ((0, wg_ref), (1, wv_ref)):
                w = cw_ref[s, :, cs]
                g = dhc[s]
                dh = (w[2:3, :] * g + w[1:2, :] * pltpu.roll(g, ext - 1, 0) + w[0:1, :] * pltpu.roll(g, ext - 2, 0))[tile]
                gt = g[tile]
                for k in range(3):
                    dcw_ref[s, k:k + 1, cs] += _rowsum(gt * taps[s][k][tile])
                dcb_ref[s, :, cs] += _rowsum(gt)
                dhb = dh.astype(BF16)
                part = lax.dot_general(dhb, w_ref[:, cs], (((1,), (1,)), ((), ())), preferred_element_type=F32)
                dx = part if dx is None else dx + part
                dw_ref[s, :, cs] += jnp.dot(xt, dhb, preferred_element_type=F32)
        dx_ref[...] = dx

        @pl.when(last)
        def _():
            dw_out_ref[...] = dw_ref[...].astype(BF16)

    in_specs = [
        pl.BlockSpec((2, tm, tc), lambda j, i: (0, i, j)),
        pl.BlockSpec((2, FFN_HALO, tc), lambda j, i: (0, jnp.maximum(i * hb - 1, 0), j)),
        pl.BlockSpec((2, FFN_HALO, tc), lambda j, i: (0, jnp.minimum((i + 1) * hb, last_blk), j)),
        pl.BlockSpec((tm, tc), lambda j, i: (i, j)),
        pl.BlockSpec((BF16_ROWS, tc), lambda j, i: (jnp.minimum((i + 1) * (tm // BF16_ROWS), rows // BF16_ROWS - 1), j)),
        pl.BlockSpec((tm, d), lambda j, i: (i, 0)),
        pl.BlockSpec((None, d, tc), lambda j, i: (j, 0, 0), pipeline_mode=once),
        pl.BlockSpec((None, d, tc), lambda j, i: (nj + j, 0, 0), pipeline_mode=once),
        pl.BlockSpec((2, 3, tc), lambda j, i: (0, 0, j)),
        pl.BlockSpec((2, 1, tc), lambda j, i: (0, 0, j)),
    ]
    out_specs = [
        pl.BlockSpec((None, tm, d), lambda j, i: (j, i, 0)),
        pl.BlockSpec((2, None, d, tc), lambda j, i: (0, j, 0, 0), pipeline_mode=once),
        pl.BlockSpec((2, 3, tc), lambda j, i: (0, 0, j)),
        pl.BlockSpec((2, 1, tc), lambda j, i: (0, 0, j)),
    ]
    out_shape = [jax.ShapeDtypeStruct((nj, rows, d), F32), jax.ShapeDtypeStruct((2, nj, d, tc), BF16),
                 jax.ShapeDtypeStruct((2, 3, f), F32), jax.ShapeDtypeStruct((2, 1, f), F32)]
    dx, dw, dcw, dcb = pl.pallas_call(body, name=name, grid=(nj, ni), in_specs=in_specs, out_specs=out_specs,
                                      out_shape=out_shape, scratch_shapes=[pltpu.VMEM((2, d, tc), F32)],
                                      compiler_params=_cp("parallel", "arbitrary"))(
        hf, hf, hf, df, df, x, w_up, w_up, cw, cb)
    return dx, dw.reshape(nq, d, tc), dcw, dcb


def _mixer_fwd(x, w_in, cw, cb, ga, ba, gb, bb, ws, sbb, *, name):
    rows, d = x.shape
    _, _, w = w_in.shape
    t = _pick(rows, (256,))
    groups = w // B_CHUNK

    def body(x_ref, win_ref, cw_ref, cb_ref, ga_ref, ba_ref, gb_ref, bb_ref, ws_ref, sb_ref, h_ref, o_ref, a2_ref, prev_ref):
        @pl.when(pl.program_id(0) == 0)
        def _():
            prev_ref[...] = jnp.zeros_like(prev_ref)

        xb = x_ref[...].astype(BF16)
        for s in range(4):
            h_ref[s] = jnp.dot(xb, win_ref[s], preferred_element_type=F32)
        a1 = h_ref[0] * _sig(h_ref[1])
        e = jnp.concatenate([prev_ref[...], a1], axis=0)
        prev_ref[...] = a1[t - CONV_HALO:]
        acc = cw_ref[A_KERNEL - 1:A_KERNEL, :] * e
        for k in range(A_KERNEL - 1):
            acc = acc + cw_ref[k:k + 1, :] * pltpu.roll(e, A_KERNEL - 1 - k, 0)
        a2 = acc[CONV_HALO:] + cb_ref[...]
        a2_ref[...] = a2
        xh, _ = _ln_stats(a2)
        a3 = xh * ga_ref[...] + ba_ref[...]
        o_ref[:, 0:w] = (a3 * _sig(a3)).astype(BF16)

        u, _ = _gelu(h_ref[2])
        v1, _ = _gelu(h_ref[3])
        xh2, _ = _ln_stats(v1)
        v2 = (xh2 * gb_ref[...] + bb_ref[...]).astype(BF16)
        for c in range(t // B_CHUNK):
            rs = slice(c * B_CHUNK, (c + 1) * B_CHUNK)
            for g in range(groups):
                cs = slice(g * B_CHUNK, (g + 1) * B_CHUNK)
                mixed = jnp.dot(ws_ref[g], v2[rs, cs], preferred_element_type=F32) + sb_ref[g]
                o_ref[rs, w + g * B_CHUNK:w + (g + 1) * B_CHUNK] = (u[rs, cs] * mixed).astype(BF16)

    vec = pl.BlockSpec((1, w), lambda i: (0, 0))
    grp = pl.BlockSpec((groups, B_CHUNK, B_CHUNK), lambda i: (0, 0, 0))
    in_specs = [
        pl.BlockSpec((t, d), lambda i: (i, 0)),
        pl.BlockSpec((4, d, w), lambda i: (0, 0, 0)),
        pl.BlockSpec((A_KERNEL, w), lambda i: (0, 0)),
        vec, vec, vec, vec, vec, grp, grp,
    ]
    out_specs = [pl.BlockSpec((4, t, w), lambda i: (0, i, 0)), pl.BlockSpec((t, 2 * w), lambda i: (i, 0)),
                 pl.BlockSpec((t, w), lambda i: (i, 0))]
    out_shape = [jax.ShapeDtypeStruct((4, rows, w), F32), jax.ShapeDtypeStruct((rows, 2 * w), BF16),
                 jax.ShapeDtypeStruct((rows, w), F32)]
    return pl.pallas_call(body, name=name, grid=(rows // t,), in_specs=in_specs, out_specs=out_specs, out_shape=out_shape,
                          scratch_shapes=[pltpu.VMEM((CONV_HALO, w), F32)],
                          compiler_params=_cp("arbitrary"))(x, w_in, cw, cb, ga, ba, gb, bb, ws, sbb)


def _mixer_bwd(h0, a2, dab, x, w_in, res, res_scale, cw, ga, ba, gb, bb, ws, wst, sbb, tril, *, name):
    _, rows, w = h0.shape
    d = x.shape[1]
    once = pl.Buffered(1)
    t = _pick(rows, (256,))
    hb = t // CONV_HALO
    ni = rows // t
    last_blk = rows // CONV_HALO - 1
    ext = t + CONV_HALO
    tile = slice(0, t)
    groups = w // B_CHUNK
    taps = A_KERNEL - 1

    def body(h_ref, a2_ref, a2n_ref, d_ref, dn_ref, x_ref, win_ref, res_ref, cw_ref, ga_ref, ba_ref, gb_ref, bb_ref,
             ws_ref, wst_ref, sb_ref, tril_ref, dx_ref, dwin_ref, dcw_ref, dcb_ref, dga_ref, dba_ref, dgb_ref, dbb_ref,
             dws_ref, dsb_ref, dw_ref):
        i = pl.program_id(0)
        first = i == 0
        last = i == ni - 1

        @pl.when(first)
        def _():
            for r in (dw_ref, dcw_ref, dcb_ref, dga_ref, dba_ref, dgb_ref, dbb_ref, dws_ref, dsb_ref):
                r[...] = jnp.zeros_like(r)

        xt = x_ref[...].astype(BF16).T
        dx_terms = []

        def through_w_in(slot, dh):
            dhb = dh.astype(BF16)
            dx_terms.append(lax.dot_general(dhb, win_ref[slot], (((1,), (1,)), ((), ())), preferred_element_type=F32))
            dw_ref[slot] += jnp.dot(xt, dhb, preferred_element_type=F32)

        xh, rstd = _ln_stats(jnp.concatenate([a2_ref[...], a2n_ref[...]], axis=0))
        a3 = xh * ga_ref[...] + ba_ref[...]
        s3 = _sig(a3)
        da_e = jnp.concatenate([d_ref[:, 0:w], jnp.where(last, 0.0, dn_ref[...])], axis=0)
        da3 = da_e * (s3 * (1.0 + a3 * (1.0 - s3)))
        da2 = _ln_bwd(da3 * ga_ref[...], xh, rstd)
        dga_ref[...] += _rowsum(da3[tile] * xh[tile])
        dba_ref[...] += _rowsum(da3[tile])
        dcb_ref[...] += _rowsum(da2[tile])
        sgt = _sig(h_ref[1])
        a1t = h_ref[0] * sgt
        da1t = None
        for k in range(A_KERNEL):
            sh = taps - k
            fed = (da2 if sh == 0 else pltpu.roll(da2, ext - sh, 0))[tile]
            dcw_ref[k:k + 1, :] += _rowsum(a1t * fed)
            term = cw_ref[k:k + 1, :] * fed
            da1t = term if da1t is None else da1t + term
        through_w_in(0, da1t * sgt)
        through_w_in(1, da1t * h_ref[0] * sgt * (1.0 - sgt))

        bu = h_ref[2]
        bv = h_ref[3]
        u, tu = _gelu(bu)
        v1, tv = _gelu(bv)
        xh2, rstd2 = _ln_stats(v1)
        v2 = (xh2 * gb_ref[...] + bb_ref[...]).astype(BF16)
        db = d_ref[:, w:2 * w]
        dmx_all = db * u
        du_parts, dv2_parts = [], []
        for c in range(t // B_CHUNK):
            rs = slice(c * B_CHUNK, (c + 1) * B_CHUNK)
            du_row, dv2_row = [], []
            for g in range(groups):
                cs = slice(g * B_CHUNK, (g + 1) * B_CHUNK)
                v2cg = v2[rs, cs]
                mixed = jnp.dot(ws_ref[g], v2cg, preferred_element_type=F32) + sb_ref[g]
                dmx = dmx_all[rs, cs]
                dmxb = dmx.astype(BF16)
                du_row.append(db[rs, cs] * mixed)
                dv2_row.append(jnp.dot(wst_ref[g], dmxb, preferred_element_type=F32))
                dws_ref[g] += tril_ref[...] * lax.dot_general(dmxb, v2cg, (((1,), (1,)), ((), ())),
                                                               preferred_element_type=F32)
                dsb_ref[g:g + 1, :] += _rowsum(dmx.T)
            du_parts.append(jnp.concatenate(du_row, axis=1))
            dv2_parts.append(jnp.concatenate(dv2_row, axis=1))
        du = jnp.concatenate(du_parts, axis=0)
        dv2 = jnp.concatenate(dv2_parts, axis=0)
        dgb_ref[...] += _rowsum(dv2 * xh2)
        dbb_ref[...] += _rowsum(dv2)
        dv1 = _ln_bwd(dv2 * gb_ref[...], xh2, rstd2)
        through_w_in(2, du * _gelu_grad(bu, tu))
        through_w_in(3, dv1 * _gelu_grad(bv, tv))
        dx_ref[...] = res_scale * res_ref[...] + ((dx_terms[0] + dx_terms[1]) + (dx_terms[2] + dx_terms[3]))

        @pl.when(last)
        def _():
            dwin_ref[...] = dw_ref[...].astype(BF16)

    vec = pl.BlockSpec((1, w), lambda i: (0, 0))
    grp = pl.BlockSpec((groups, B_CHUNK, B_CHUNK), lambda i: (0, 0, 0))
    halo = pl.BlockSpec((CONV_HALO, w), lambda i: (jnp.minimum((i + 1) * hb, last_blk), 0))
    wide = pl.BlockSpec((t, d), lambda i: (i, 0))
    in_specs = [
        pl.BlockSpec((4, t, w), lambda i: (0, i, 0)),
        pl.BlockSpec((t, w), lambda i: (i, 0)),
        halo,
        pl.BlockSpec((t, 2 * w), lambda i: (i, 0)),
        halo,
        wide,
        pl.BlockSpec((4, d, w), lambda i: (0, 0, 0), pipeline_mode=once),
        wide,
        pl.BlockSpec((A_KERNEL, w), lambda i: (0, 0)),
        vec, vec, vec, vec, grp, grp, grp,
        pl.BlockSpec((B_CHUNK, B_CHUNK), lambda i: (0, 0)),
    ]
    vsds = jax.ShapeDtypeStruct((1, w), F32)
    out_specs = [
        wide,
        pl.BlockSpec((4, d, w), lambda i: (0, 0, 0), pipeline_mode=once),
        pl.BlockSpec((A_KERNEL, w), lambda i: (0, 0)),
        vec, vec, vec, vec, vec, grp,
        pl.BlockSpec((groups, B_CHUNK), lambda i: (0, 0)),
    ]
    out_shape = [jax.ShapeDtypeStruct((rows, d), F32), jax.ShapeDtypeStruct((4, d, w), BF16),
                 jax.ShapeDtypeStruct((A_KERNEL, w), F32),
                 vsds, vsds, vsds, vsds, vsds, jax.ShapeDtypeStruct((groups, B_CHUNK, B_CHUNK), F32),
                 jax.ShapeDtypeStruct((groups, B_CHUNK), F32)]
    return pl.pallas_call(body, name=name, grid=(ni,), in_specs=in_specs, out_specs=out_specs, out_shape=out_shape,
                          scratch_shapes=[pltpu.VMEM((4, d, w), F32)], compiler_params=_cp("arbitrary"))(
        h0, a2, a2, dab, dab, x, w_in, res, cw, ga, ba, gb, bb, ws, wst, sbb, tril)


GROUP_ROWS = Q_PER_KV * ATT_BLOCK


def _attn_mask(n):
    qi = lax.broadcasted_iota(jnp.int32, (GROUP_ROWS, 2 * ATT_BLOCK), 0) & (ATT_BLOCK - 1)
    sj = lax.broadcasted_iota(jnp.int32, (GROUP_ROWS, 2 * ATT_BLOCK), 1)
    diff = qi + ATT_BLOCK - sj
    return (diff >= 0) & (diff < ATT_BLOCK) & ((n > 0) | (sj >= ATT_BLOCK))


def _stack_heads(ref, kvh, dtype):
    heads = [ref[:, (kvh * Q_PER_KV + g) * HEAD_DIM:(kvh * Q_PER_KV + g + 1) * HEAD_DIM] for g in range(Q_PER_KV)]
    return jnp.concatenate(heads, axis=0).astype(dtype)


def _per_row_sink(sink_ref, kvh):
    head = lax.broadcasted_iota(jnp.int32, (GROUP_ROWS, 1), 0) // ATT_BLOCK
    out = jnp.zeros((GROUP_ROWS, 1), F32)
    for g in range(Q_PER_KV):
        out = jnp.where(head == g, sink_ref[kvh * Q_PER_KV + g], out)
    return out


def _attn_specs(rows, n_q):
    dq = n_q * HEAD_DIM
    dkv = 2 * (n_q // Q_PER_KV) * HEAD_DIM
    kv_blk = dq // dkv
    assert dq % dkv == 0
    return dq, dkv, [
        pl.BlockSpec(memory_space=pltpu.SMEM),
        pl.BlockSpec((ATT_BLOCK, dq), lambda n: (n, 0)),
        pl.BlockSpec((ATT_BLOCK, dkv), lambda n: (n, kv_blk)),
        pl.BlockSpec((ATT_BLOCK, dkv), lambda n: (jnp.maximum(n - 1, 0), kv_blk)),
    ]


def _kv_pair(kvc_ref, kvp_ref, kvh, n_kv):
    ks = slice(kvh * HEAD_DIM, (kvh + 1) * HEAD_DIM)
    vs = slice((n_kv + kvh) * HEAD_DIM, (n_kv + kvh + 1) * HEAD_DIM)
    kk = jnp.concatenate([kvp_ref[:, ks], kvc_ref[:, ks]], axis=0).astype(BF16)
    vv = jnp.concatenate([kvp_ref[:, vs], kvc_ref[:, vs]], axis=0).astype(BF16)
    return kk, vv


def _attn_fwd(qkv, sinks, *, name):
    rows = qkv.shape[0]
    n_q = sinks.shape[0]
    n_kv = n_q // Q_PER_KV
    scale = 1.0 / math.sqrt(HEAD_DIM)
    dq, _, in_specs = _attn_specs(rows, n_q)

    def body(sink_ref, q_ref, kvc_ref, kvp_ref, o_ref, lse_ref):
        valid = _attn_mask(pl.program_id(0))
        for kvh in range(n_kv):
            kk, vv = _kv_pair(kvc_ref, kvp_ref, kvh, n_kv)
            qs = _stack_heads(q_ref, kvh, BF16)
            s = lax.dot_general(qs, kk, (((1,), (1,)), ((), ())), preferred_element_type=F32)
            s = jnp.where(valid, s * scale, -jnp.inf)
            sk = _per_row_sink(sink_ref, kvh)
            m = jnp.maximum(jnp.max(s, axis=1, keepdims=True), sk)
            p = jnp.exp(s - m)
            l = jnp.sum(p, axis=1, keepdims=True) + jnp.exp(sk - m)
            o = jnp.dot((p / l).astype(BF16), vv, preferred_element_type=F32)
            lse = m + jnp.log(l)
            for g in range(Q_PER_KV):
                h = kvh * Q_PER_KV + g
                rs = slice(g * ATT_BLOCK, (g + 1) * ATT_BLOCK)
                o_ref[:, h * HEAD_DIM:(h + 1) * HEAD_DIM] = o[rs]
                lse_ref[:, h:h + 1] = lse[rs]

    out_specs = [pl.BlockSpec((ATT_BLOCK, dq), lambda n: (n, 0)), pl.BlockSpec((ATT_BLOCK, n_q), lambda n: (n, 0))]
    out_shape = [jax.ShapeDtypeStruct((rows, dq), F32), jax.ShapeDtypeStruct((rows, n_q), F32)]
    return pl.pallas_call(body, name=name, grid=(rows // ATT_BLOCK,), in_specs=in_specs, out_specs=out_specs,
                          out_shape=out_shape, compiler_params=_cp("parallel"))(sinks, qkv, qkv, qkv)


def _attn_bwd(qkv, dout, lse, sinks, *, name):
    rows = qkv.shape[0]
    n_q = sinks.shape[0]
    n_kv = n_q // Q_PER_KV
    scale = 1.0 / math.sqrt(HEAD_DIM)
    dq_w, dkv_w, in_specs = _attn_specs(rows, n_q)
    blk_q = pl.BlockSpec((ATT_BLOCK, dq_w), lambda n: (n, 0))
    blk_kv = pl.BlockSpec((ATT_BLOCK, dkv_w), lambda n: (n, 0))
    in_specs = in_specs + [blk_q, pl.BlockSpec((ATT_BLOCK, n_q), lambda n: (n, 0))]

    def body(sink_ref, q_ref, kvc_ref, kvp_ref, do_ref, lse_ref, dq_ref, dkc_ref, dkp_ref, dsink_ref):
        n = pl.program_id(0)

        @pl.when(n == 0)
        def _():
            dsink_ref[...] = jnp.zeros_like(dsink_ref)

        valid = _attn_mask(n)
        head_ids = lax.broadcasted_iota(jnp.int32, (1, n_q), 1)
        dsink = jnp.zeros((1, n_q), F32)
        for kvh in range(n_kv):
            kk, vv = _kv_pair(kvc_ref, kvp_ref, kvh, n_kv)
            qs = _stack_heads(q_ref, kvh, BF16)
            dos = _stack_heads(do_ref, kvh, BF16)
            lse = jnp.concatenate([lse_ref[:, kvh * Q_PER_KV + g:kvh * Q_PER_KV + g + 1] for g in range(Q_PER_KV)], axis=0)
            s = lax.dot_general(qs, kk, (((1,), (1,)), ((), ())), preferred_element_type=F32)
            s = jnp.where(valid, s * scale, -jnp.inf)
            p = jnp.exp(s - lse)
            dp = lax.dot_general(dos, vv, (((1,), (1,)), ((), ())), preferred_element_type=F32)
            delta = jnp.sum(p * dp, axis=1, keepdims=True)
            ds = (p * (dp - delta) * scale).astype(BF16)
            sink_term = jnp.exp(_per_row_sink(sink_ref, kvh) - lse) * delta
            dqs = jnp.dot(ds, kk, preferred_element_type=F32)
            for g in range(Q_PER_KV):
                h = kvh * Q_PER_KV + g
                rs = slice(g * ATT_BLOCK, (g + 1) * ATT_BLOCK)
                dsink = dsink + jnp.where(head_ids == h, -jnp.sum(sink_term[rs]), 0.0)
                dq_ref[:, h * HEAD_DIM:(h + 1) * HEAD_DIM] = dqs[rs]
            dk = lax.dot_general(ds, qs, (((0,), (0,)), ((), ())), preferred_element_type=F32)
            dv = lax.dot_general(p.astype(BF16), dos, (((0,), (0,)), ((), ())), preferred_element_type=F32)
            ks = slice(kvh * HEAD_DIM, (kvh + 1) * HEAD_DIM)
            vs = slice((n_kv + kvh) * HEAD_DIM, (n_kv + kvh + 1) * HEAD_DIM)
            dkp_ref[:, ks] = dk[0:ATT_BLOCK]
            dkc_ref[:, ks] = dk[ATT_BLOCK:]
            dkp_ref[:, vs] = dv[0:ATT_BLOCK]
            dkc_ref[:, vs] = dv[ATT_BLOCK:]
        dsink_ref[...] += dsink

    out_specs = [blk_q, blk_kv, blk_kv, pl.BlockSpec((1, n_q), lambda n: (0, 0))]
    out_shape = [jax.ShapeDtypeStruct((rows, dq_w), F32), jax.ShapeDtypeStruct((rows, dkv_w), F32),
                 jax.ShapeDtypeStruct((rows, dkv_w), F32), jax.ShapeDtypeStruct((1, n_q), F32)]
    return pl.pallas_call(body, name=name, grid=(rows // ATT_BLOCK,), in_specs=in_specs, out_specs=out_specs,
                          out_shape=out_shape, compiler_params=_cp("arbitrary"))(sinks, qkv, qkv, qkv, dout, lse)


def _dqkv_assemble(dq, dkc, dkp, *, name):
    rows, dq_w = dq.shape
    dkv_w = dkc.shape[1]
    nb = rows // ATT_BLOCK

    def body(dq_ref, dkc_ref, dkp_ref, o_ref, db_ref):
        n = pl.program_id(0)

        @pl.when(n == 0)
        def _():
            db_ref[...] = jnp.zeros_like(db_ref)

        dqv = dq_ref[...]
        dkv = dkc_ref[...] + jnp.where(n == nb - 1, 0.0, dkp_ref[...])
        o_ref[:, 0:dq_w] = dqv.astype(BF16)
        o_ref[:, dq_w:dq_w + dkv_w] = dkv.astype(BF16)
        db_ref[:, 0:dq_w] += _rowsum(dqv)
        db_ref[:, dq_w:dq_w + dkv_w] += _rowsum(dkv)

    width = dq_w + dkv_w
    in_specs = [pl.BlockSpec((ATT_BLOCK, dq_w), lambda n: (n, 0)), pl.BlockSpec((ATT_BLOCK, dkv_w), lambda n: (n, 0)),
                pl.BlockSpec((ATT_BLOCK, dkv_w), lambda n: (jnp.minimum(n + 1, nb - 1), 0))]
    out_specs = [pl.BlockSpec((ATT_BLOCK, width), lambda n: (n, 0)), pl.BlockSpec((1, width), lambda n: (0, 0))]
    out_shape = [jax.ShapeDtypeStruct((rows, width), BF16), jax.ShapeDtypeStruct((1, width), F32)]
    return pl.pallas_call(body, name=name, grid=(nb,), in_specs=in_specs, out_specs=out_specs, out_shape=out_shape,
                          compiler_params=_cp("arbitrary"))(dq, dkc, dkp)


def _row_tile(r, c):
    budget = 2 * 1024 * 1024 // (4 * c)
    for cand in (1024, 512, 256, 128, 64, 32, 16):
        if cand <= budget and r % cand == 0:
            return cand
    return r


def _octo_sum(own, recv, place, dest, lead, *, name):
    _, _, r, c = own.shape
    t = _row_tile(r, c)
    lead_idx, buf_shape = lead

    def body(place_ref, own_ref, *rest):
        o_ref = rest[7] if dest is None else rest[8]
        acc = own_ref[...].astype(F32)
        for k in range(7):
            acc = acc + rest[k][...].astype(F32)
        o_ref[...] = acc

    def peer(mask):
        return pl.BlockSpec((None, t, c), lambda i, pr: (pr[2] ^ mask, i, 0))

    if lead_idx is None:
        o_spec = pl.BlockSpec((None, t, c), lambda i, pr: (pr[1], i, 0))
    else:
        o_spec = pl.BlockSpec((None, None, t, c), lambda i, pr: (lead_idx, pr[1], i, 0))
    in_specs = [pl.BlockSpec((None, None, t, c), lambda i, pr: (pr[0], pr[1], i, 0))] + [peer(m) for m in range(1, 8)]
    args = [place, own] + [recv] * 7
    aliases = {}
    if dest is not None:
        in_specs.append(HBM)
        args.append(dest)
        aliases = {9: 0}
    grid_spec = pltpu.PrefetchScalarGridSpec(num_scalar_prefetch=1, grid=(r // t,), in_specs=in_specs, out_specs=o_spec)
    return pl.pallas_call(body, name=name, grid_spec=grid_spec, out_shape=jax.ShapeDtypeStruct(buf_shape, F32),
                          input_output_aliases=aliases, compiler_params=_cp("parallel"))(*args)


def _adamw_math(w, g, m, v):
    nm = ADAM_B1 * m + (1.0 - ADAM_B1) * g
    nv = ADAM_B2 * v + (1.0 - ADAM_B2) * (g * g)
    m_hat = nm / (1.0 - ADAM_B1 ** ADAM_STEP)
    v_hat = nv / (1.0 - ADAM_B2 ** ADAM_STEP)
    return -ADAM_LR * (m_hat / (jnp.sqrt(v_hat) + ADAM_EPS) + ADAM_WD * w), nm, nv


def _adamw(w, g, m, v, *, name):
    r, c = w.shape
    t = _row_tile(r, c)

    def body(w_ref, g_ref, m_ref, v_ref, d_ref, nm_ref, nv_ref, go_ref):
        gv = g_ref[...]
        d_ref[...], nm_ref[...], nv_ref[...] = _adamw_math(w_ref[...], gv, m_ref[...], v_ref[...])
        go_ref[...] = gv

    blk = pl.BlockSpec((t, c), lambda i: (i, 0))
    sds = jax.ShapeDtypeStruct((r, c), F32)
    return pl.pallas_call(body, name=name, grid=(r // t,), in_specs=[blk] * 4, out_specs=[blk] * 4,
                          out_shape=[sds] * 4, compiler_params=_cp("parallel"))(w, g, m, v)


HBM = pl.BlockSpec(memory_space=pl.ANY)


def _place():
    x, y, c = lax.axis_index("x"), lax.axis_index("y"), lax.axis_index("c")
    chips = [(1 - x, y), (x, 1 - y), (1 - x, 1 - y)]
    return x, y, c, 2 * x + y, (x, y, 1 - c), chips


def _rcopy(src, dst, ssem, rsem, dev):
    return pltpu.make_async_remote_copy(src_ref=src, dst_ref=dst, send_sem=ssem, recv_sem=rsem, device_id=dev,
                                        device_id_type=MESH)


HBM_ONLY = pl.BlockSpec(memory_space=pltpu.HBM)
SEM = pl.BlockSpec(memory_space=pltpu.SEMAPHORE)


def _peers():
    x, y, c = lax.axis_index("x"), lax.axis_index("y"), lax.axis_index("c")
    out = []
    for mask in range(1, 8):
        px = 1 - x if mask & 4 else x
        py = 1 - y if mask & 2 else y
        pc = 1 - c if mask & 1 else c
        out.append(((px, py, pc), 2 * px + py, pc, 4 * px + 2 * py + pc))
    return 4 * x + 2 * y + c, out


def _reduce_start(grads, lands, after, *, name):
    nt = len(grads)

    def body(*refs):
        ssems, rsems = refs[2 * nt + 1:3 * nt + 1], refs[3 * nt + 1:4 * nt + 1]
        g_out, l_out, token = refs[4 * nt + 1:5 * nt + 1], refs[5 * nt + 1:6 * nt + 1], refs[6 * nt + 1]
        me, peers = _peers()
        for t in range(nt):
            for k, (dev, chip, core, _) in enumerate(peers):
                _rcopy(g_out[t].at[chip, core], l_out[t].at[me], ssems[t].at[k], rsems[t].at[k], dev).start()
        token[...] = jnp.zeros_like(token)

    sems = [pltpu.SemaphoreType.DMA((7,))] * (2 * nt)
    out_shape = (sems + [pltpu.HBM(g.shape, g.dtype) for g in grads] + [pltpu.HBM(l.shape, l.dtype) for l in lands]
                 + [jax.ShapeDtypeStruct((8, LANES), F32)])
    res = pl.pallas_call(
        body, name=name, in_specs=[HBM_ONLY] * (2 * nt + 1),
        out_specs=[SEM] * (2 * nt) + [HBM_ONLY] * (2 * nt) + [pl.BlockSpec(memory_space=pltpu.VMEM)], out_shape=out_shape,
        input_output_aliases={t: 2 * nt + t for t in range(2 * nt)},
        compiler_params=pltpu.CompilerParams(has_side_effects=DATAFLOW),
    )(*[pltpu.with_memory_space_constraint(a, pltpu.HBM) for a in list(grads) + list(lands) + [after]])
    return res[:nt], res[nt:2 * nt], res[2 * nt:3 * nt], res[3 * nt:4 * nt], res[4 * nt]


def _reduce_wait(grads, lands, ssems, rsems, after, *, name):
    nt = len(grads)

    def body(*refs):
        ssem_refs, rsem_refs = refs[2 * nt:3 * nt], refs[3 * nt:4 * nt]
        g_out, l_out = refs[4 * nt + 1:5 * nt + 1], refs[5 * nt + 1:6 * nt + 1]
        me, peers = _peers()
        for t in range(nt):
            for k, (dev, chip, core, _) in enumerate(peers):
                _rcopy(g_out[t].at[chip, core], l_out[t].at[me], ssem_refs[t].at[k], rsem_refs[t].at[k], dev).wait_send()
        for t in range(nt):
            for k, (dev, _, _, idx) in enumerate(peers):
                slot = l_out[t].at[idx]
                _rcopy(slot, slot, ssem_refs[t].at[k], rsem_refs[t].at[k], dev).wait_recv()

    res = pl.pallas_call(
        body, name=name, in_specs=[HBM_ONLY] * (2 * nt) + [SEM] * (2 * nt) + [HBM_ONLY], out_specs=[HBM_ONLY] * (2 * nt),
        out_shape=[pltpu.HBM(a.shape, a.dtype) for a in list(grads) + list(lands)],
        input_output_aliases={t: t for t in range(2 * nt)},
        compiler_params=pltpu.CompilerParams(has_side_effects=DATAFLOW),
    )(*grads, *lands, *ssems, *rsems, pltpu.with_memory_space_constraint(after, pltpu.HBM))
    return list(res[:nt]), list(res[nt:])
DATAFLOW = pltpu.SideEffectType.DATAFLOW_SIDE_EFFECTING


def _gather_start(bufs, *, name):
    nt = len(bufs)

    def body(*refs):
        ssems, rsems, outs = refs[nt:2 * nt], refs[2 * nt:3 * nt], refs[3 * nt:4 * nt]
        x, y, c, q, sib, chips = _place()
        for t in range(nt):
            for j, (px, py) in enumerate(chips):
                mine = outs[t].at[q]
                _rcopy(mine, mine, ssems[t].at[j], rsems[t].at[j], (px, py, c)).start()

    sems = [pltpu.SemaphoreType.DMA((3,))] * (2 * nt)
    out_shape = sems + [pltpu.HBM(b.shape, b.dtype) for b in bufs]
    res = pl.pallas_call(
        body, name=name, in_specs=[HBM_ONLY] * nt, out_specs=[SEM] * (2 * nt) + [HBM_ONLY] * nt, out_shape=out_shape,
        input_output_aliases={t: 2 * nt + t for t in range(nt)},
        compiler_params=pltpu.CompilerParams(has_side_effects=DATAFLOW),
    )(*[pltpu.with_memory_space_constraint(b, pltpu.HBM) for b in bufs])
    return res[:nt], res[nt:2 * nt], res[2 * nt:]


def _gather_wait(bufs, ssems, rsems, after, *, name):
    nt = len(bufs)

    def body(*refs):
        ssem_refs, rsem_refs = refs[nt:2 * nt], refs[2 * nt:3 * nt]
        outs = refs[3 * nt + 1:]
        x, y, c, q, sib, chips = _place()
        for t in range(nt):
            for j, (px, py) in enumerate(chips):
                mine = outs[t].at[q]
                _rcopy(mine, mine, ssem_refs[t].at[j], rsem_refs[t].at[j], (px, py, c)).wait_send()
        for t in range(nt):
            for j, (px, py) in enumerate(chips):
                theirs = outs[t].at[2 * px + py]
                _rcopy(theirs, theirs, ssem_refs[t].at[j], rsem_refs[t].at[j], (px, py, c)).wait_recv()

    res = pl.pallas_call(
        body, name=name, in_specs=[HBM_ONLY] * nt + [SEM] * (2 * nt) + [HBM], out_specs=[HBM_ONLY] * nt,
        out_shape=[pltpu.HBM(b.shape, b.dtype) for b in bufs], input_output_aliases={t: t for t in range(nt)},
        compiler_params=pltpu.CompilerParams(has_side_effects=DATAFLOW),
    )(*bufs, *ssems, *rsems, after)
    return list(res)


def _sibling_share(bufs, layout, *, name):
    no = len(bufs)
    nt = len(layout)

    def body(*refs):
        outs = refs[no:2 * no]
        ssem, rsem = refs[2 * no:]
        x, y, c, q, sib, chips = _place()

        def slot(t, half):
            o, lead = layout[t]
            return outs[o].at[half] if lead is None else outs[o].at[lead, half]

        sends = []
        for t in range(nt):
            cp = _rcopy(slot(t, c), slot(t, c), ssem.at[t], rsem.at[t], sib)
            cp.start()
            sends.append(cp)
        for t in range(nt):
            _rcopy(slot(t, 1 - c), slot(t, 1 - c), ssem.at[t], rsem.at[t], sib).wait_recv()
        for cp in sends:
            cp.wait_send()

    out_shape = [jax.ShapeDtypeStruct(b.shape, b.dtype) for b in bufs]
    return pl.pallas_call(
        body, name=name, in_specs=[HBM] * no, out_specs=[HBM] * no, out_shape=out_shape,
        input_output_aliases={o: o for o in range(no)},
        scratch_shapes=[pltpu.SemaphoreType.DMA((nt,)), pltpu.SemaphoreType.DMA((nt,))],
    )(*bufs)


def _small_tail(local, params, *, name):
    (cwa, cba, ga, ba, gb, bb, dws, dsb, dbq, dsk, cwf0, cbf0, cwf1, cbf1,
     g00, g01, g10, g11, b00, b01, b10, b11, err) = local
    n_local = len(local)
    kw, wa = cwa.shape
    ng = dws.shape[0]
    nqkv = dbq.shape[1]
    nsk = dsk.shape[1]
    f = cwf0.shape[2]
    dm = err.shape[1]
    row_vec = 8 * (-(-kw // 8))
    shapes = [(row_vec + 8, wa), (ng * B_CHUNK + 8, B_CHUNK), (8, nqkv), (2, 2, 8, f), (16, dm)]
    n_grp = len(shapes)
    flat_params = [a for triple in params for a in triple]
    n_par = len(params)

    def reduce_body(*refs):
        loc = refs[:n_local]
        tot = refs[n_local:n_local + n_grp]
        scr = refs[n_local + n_grp:]
        grp, from_sib, pair, gath = (scr[k * n_grp:(k + 1) * n_grp] for k in range(4))
        ssem1, rsem1, ssem2, rsem2 = scr[4 * n_grp:]
        x, y, core, q, sib, chips = _place()

        for gr in grp:
            gr[...] = jnp.zeros_like(gr)
        a, b, c, dd, e = grp
        a[0:kw, :] = loc[0][...]
        for k in range(5):
            a[row_vec + k:row_vec + k + 1, :] = loc[1 + k][...]
        for g in range(ng):
            b[g * B_CHUNK:(g + 1) * B_CHUNK, :] = loc[6][g]
        b[ng * B_CHUNK:ng * B_CHUNK + ng, :] = loc[7][...]
        c[0:1, :] = loc[8][...]
        c[1:2, 0:nsk] = loc[9][...]
        for l in range(2):
            for s in range(2):
                dd[l, s, 0:3, :] = loc[10 + 2 * l][s]
                dd[l, s, 3:4, :] = loc[11 + 2 * l][s]
        for k in range(9):
            e[k:k + 1, :] = loc[14 + k][...]

        sends = []
        for gi in range(n_grp):
            cp = _rcopy(grp[gi], from_sib[gi], ssem1.at[gi], rsem1.at[gi], sib)
            cp.start()
            sends.append(cp)
        for gi in range(n_grp):
            _rcopy(grp[gi], from_sib[gi], ssem1.at[gi], rsem1.at[gi], sib).wait_recv()
            both = grp[gi][...] + from_sib[gi][...]
            pair[gi][...] = both
            gath[gi][q] = both
            for j, (px, py) in enumerate(chips):
                cp = _rcopy(pair[gi], gath[gi].at[q], ssem2.at[gi, j], rsem2.at[gi, j], (px, py, core))
                cp.start()
                sends.append(cp)
        for gi in range(n_grp):
            for j, (px, py) in enumerate(chips):
                slot = gath[gi].at[2 * px + py]
                _rcopy(slot, slot, ssem2.at[gi, j], rsem2.at[gi, j], (px, py, core)).wait_recv()
            acc = gath[gi][0]
            for k in range(1, 4):
                acc = acc + gath[gi][k]
            tot[gi][...] = acc
        for cp in sends:
            cp.wait_send()

    vm = pl.BlockSpec(memory_space=pltpu.VMEM)
    scratch = ([pltpu.VMEM(s, F32) for s in shapes] * 3 + [pltpu.VMEM((4,) + s, F32) for s in shapes]
               + [pltpu.SemaphoreType.DMA((n_grp,)), pltpu.SemaphoreType.DMA((n_grp,)),
                  pltpu.SemaphoreType.DMA((n_grp, 3)), pltpu.SemaphoreType.DMA((n_grp, 3))])
    totals = pl.pallas_call(
        reduce_body, name=name + "_reduce", in_specs=[vm] * n_local, out_specs=[vm] * n_grp,
        out_shape=[jax.ShapeDtypeStruct(s, F32) for s in shapes], scratch_shapes=scratch,
        compiler_params=pltpu.CompilerParams(vmem_limit_bytes=VMEM_LIMIT),
    )(*local)

    def adamw_body(*refs):
        ta, tb, tc, td, te = refs[:n_grp]
        par = refs[n_grp:n_grp + 3 * n_par]
        outs = refs[n_grp + 3 * n_par:n_grp + 7 * n_par]
        loss_ref = refs[n_grp + 7 * n_par]
        q = 2 * lax.axis_index("x") + lax.axis_index("y")

        def mine(piece):
            out = piece(0)
            for k in range(1, 4):
                out = jnp.where(q == k, piece(k), out)
            return out

        def update(p, grad, index=None):
            at = (lambda r: r[...]) if index is None else (lambda r: r[index])
            w_ref, m_ref, v_ref = par[3 * p:3 * p + 3]
            g_ref, d_ref, nm_ref, nv_ref = outs[4 * p:4 * p + 4]
            delta, nm, nv = _adamw_math(at(w_ref), grad, at(m_ref), at(v_ref))
            for r, val in ((g_ref, grad), (d_ref, delta), (nm_ref, nm), (nv_ref, nv)):
                if index is None:
                    r[...] = val
                else:
                    r[index] = val

        wq = wa // 4
        update(0, mine(lambda k: ta[0:kw, k * wq:(k + 1) * wq]), (0,))
        for k in range(5):
            update(1 + k, ta[row_vec + k:row_vec + k + 1, :])
        for g in range(ng):
            update(6, tb[g * B_CHUNK:(g + 1) * B_CHUNK, :], (0, g))
        update(7, tb[ng * B_CHUNK:ng * B_CHUNK + ng, :], (0,))
        nq4 = nqkv // 4
        update(8, mine(lambda k: tc[0:1, k * nq4:(k + 1) * nq4]))
        update(9, tc[1:2, 0:nsk])
        fh = f // 2
        for l in range(2):
            update(10, mine(lambda k: td[l, k // 2, 0:3, (k % 2) * fh:(k % 2 + 1) * fh]), (l,))
            update(11, jnp.concatenate([td[l, 0, 3:4, :], td[l, 1, 3:4, :]], axis=1), (slice(l, l + 1),))
        dq4 = dm // 4
        for i in range(2):
            for j in range(2):
                for p, base in ((12, 0), (13, 4)):
                    row = base + 2 * i + j
                    update(p, mine(lambda k: te[row:row + 1, k * dq4:(k + 1) * dq4]), (i, slice(j, j + 1)))
        loss_ref[...] = (0.5 / dm) * jnp.sum(te[8:9, :], axis=1, keepdims=True)

    out_shape = []
    for w, _, _ in params:
        out_shape += [jax.ShapeDtypeStruct(w.shape, F32)] * 4
    out_shape.append(jax.ShapeDtypeStruct((1, 1), F32))
    res = pl.pallas_call(
        adamw_body, name=name + "_adamw", in_specs=[vm] * (n_grp + 3 * n_par), out_specs=[vm] * len(out_shape),
        out_shape=out_shape, compiler_params=pltpu.CompilerParams(vmem_limit_bytes=VMEM_LIMIT),
    )(*totals, *flat_params)
    return [res[4 * p:4 * p + 4] for p in range(n_par)], res[-1]


def _pack(arrays, rows_multiple):
    flat = jnp.concatenate([a.reshape(-1) for a in arrays])
    rows = -(-flat.shape[0] // LANES)
    rows = -(-rows // rows_multiple) * rows_multiple
    flat = jnp.pad(flat, (0, rows * LANES - flat.shape[0]))
    return flat.reshape(rows, LANES)


def _unshard_cols(stacked):
    moved = jnp.moveaxis(stacked, 0, -2)
    return moved.reshape(moved.shape[:-2] + (4 * stacked.shape[-1],))


def kernel(x, ab_w_in, a_conv_w, a_conv_b, a_norm_g, a_norm_b, b_norm_g, b_norm_b, b_spatial_w, b_spatial_b, ab_w_out, c_w_qkv, c_b_qkv, c_sinks, c_w_o, ffn_w_up, ffn_conv_w, ffn_conv_b, ffn_w_down, ln_g, ln_b, loss_target, m_ab_w_in, m_a_conv_w, m_a_conv_b, m_a_norm_g, m_a_norm_b, m_b_norm_g, m_b_norm_b, m_b_spatial_w, m_b_spatial_b, m_ab_w_out, m_c_w_qkv, m_c_b_qkv, m_c_sinks, m_c_w_o, m_ffn_w_up, m_ffn_conv_w, m_ffn_conv_b, m_ffn_w_down, m_ln_g, m_ln_b, v_ab_w_in, v_a_conv_w, v_a_conv_b, v_a_norm_g, v_a_norm_b, v_b_norm_g, v_b_norm_b, v_b_spatial_w, v_b_spatial_b, v_ab_w_out, v_c_w_qkv, v_c_b_qkv, v_c_sinks, v_c_w_o, v_ffn_w_up, v_ffn_conv_w, v_ffn_conv_b, v_ffn_w_down, v_ln_g, v_ln_b):
    rows, d = x.shape[1], x.shape[2]
    depth = ln_g.shape[0]
    assert depth == 2 and x.shape[0] == 1
    alpha = (2.0 * depth) ** 0.25
    f = ffn_w_down.shape[1] * 4
    n_q = c_sinks.shape[1]
    q_idx = 2 * lax.axis_index("x") + lax.axis_index("y")
    c_idx = lax.axis_index("c")
    xs, tgt = x[0], loss_target[0]

    def own_slot(part):
        buf = lax.empty((4,) + part.shape, part.dtype)
        return lax.dynamic_update_slice(buf, part[None], (q_idx, 0, 0, 0))

    def halves(wm):
        return own_slot(wm.astype(BF16).reshape((2, wm.shape[0] // 2) + wm.shape[1:]))

    small_sharded = [a_conv_w[0], c_b_qkv[0], ffn_conv_w, ln_g, ln_b]
    small_pack = _pack(small_sharded, 16)
    bufs = [halves(ab_w_in[0]), own_slot(small_pack.reshape(2, small_pack.shape[0] // 2, LANES)), halves(ab_w_out[0]),
            halves(ffn_w_up[0]), halves(ffn_w_down[0]), halves(c_w_qkv[0]), halves(c_w_o[0]),
            halves(ffn_w_up[1]), halves(ffn_w_down[1])]
    ssems, rsems, started = _gather_start(bufs, name="gather_start")

    def arrive(idx, after, tag):
        got = _gather_wait([started[i] for i in idx], [ssems[i] for i in idx], [rsems[i] for i in idx], after,
                           name=f"gather_wait_{tag}")
        return [g.reshape(4, 2 * g.shape[2], g.shape[3]) for g in got]

    w_in, small_all = arrive([0, 1], xs, "in")
    small_all = small_all.reshape(4, -1)
    sh_shapes = [s.shape for s in small_sharded]
    pieces, pos = [], 0
    for s in sh_shapes:
        n = math.prod(s)
        pieces.append(_unshard_cols(small_all[:, pos:pos + n].reshape((4,) + s)))
        pos += n
    conv_w_a, b_qkv, conv_w_f, ln_gf, ln_bf = pieces

    tril = jnp.tril(jnp.ones((B_CHUNK, B_CHUNK), F32))
    ws = (b_spatial_w[0] * tril).astype(BF16)
    wst = jnp.swapaxes(ws, 1, 2)
    sbb = jnp.broadcast_to(b_spatial_b[0][:, :, None], b_spatial_w[0].shape)
    mix_vecs = [a_conv_b, a_norm_g, a_norm_b, b_norm_g, b_norm_b]
    cw_f = [jnp.swapaxes(conv_w_f[l].reshape(3, 2, f), 0, 1) for l in range(depth)]
    cb_f = [ffn_conv_b[l].reshape(2, 1, f) for l in range(depth)]
    lng = lambda i, j: ln_gf[i, j].reshape(1, d)
    lnb = lambda i, j: ln_bf[i, j].reshape(1, d)
    sinks = c_sinks[0]

    w_up, w_down = [None, None], [None, None]

    def ffn_fwd(xin, l):
        w_up[l], = arrive([3 + 4 * l], xin, f"up{l}")
        hf, fact = _ffn_up_fwd(xin, w_up[l], cw_f[l], cb_f[l], name=f"ffn{l}_up")
        w_down[l] = arrive([4 + 4 * l], fact, f"down{l}")[0].reshape(-1, d)
        out = _matmul(fact, w_down[l], name=f"ffn{l}_down", tm=512, tn=1024, tk=2816)
        return hf, fact, out

    h0, ab, a2 = _mixer_fwd(xs, w_in, conv_w_a, *mix_vecs, ws, sbb, name="mix_fwd")
    w_out = arrive([2], ab, "out")[0].reshape(-1, d)
    mix = _matmul(ab, w_out, name="mix_out", tm=1024, tn=1024, tk=1024)
    x1 = _add_ln_fwd(xs, mix, lng(0, 0), lnb(0, 0), alpha, name="ln00")
    hf0, f0, ffn0 = ffn_fwd(x1, 0)
    x2 = _add_ln_fwd(x1, ffn0, lng(0, 1), lnb(0, 1), alpha, name="ln01")
    w_qkv = _unshard_cols(arrive([5], x2, "qkv")[0])
    qkv = _matmul(x2, w_qkv, name="att_qkv", tm=1024, tn=w_qkv.shape[1], tk=1024, bias=b_qkv.reshape(1, -1))
    ao, lse = _attn_fwd(qkv, sinks, name="att_core")
    w_o = arrive([6], ao, "o")[0].reshape(-1, d)
    att = _matmul(ao, w_o, name="att_out", tm=1024, tn=1024, tk=1024)
    x3 = _add_ln_fwd(x2, att, lng(1, 0), lnb(1, 0), alpha, name="ln10")
    hf1, f1, ffn1 = ffn_fwd(x3, 1)
    sq_err, dy = _add_ln_loss(x3, ffn1, lng(1, 1), lnb(1, 1), tgt, alpha, name="ln11_loss")

    def owner_view(g):
        if g.ndim == 3:
            return g.reshape(4, 2, g.shape[1] // 2, g.shape[2])
        return g.reshape(4, 2, g.shape[0] // 8, g.shape[1])

    in_flight = []

    def send_grads(tag, grads, after):
        lands = [lax.empty((8,) + g.shape[2:], BF16) for g in grads]
        ss, rs, g_thru, l_thru, token = _reduce_start(grads, lands, after, name=f"reduce_start_{tag}")
        in_flight.append((tag, g_thru, l_thru, ss, rs))
        return token[0:1, 0:1]

    def ffn_bwd(dz, xin, hf, fact, l):
        d_wdown = _matmul(fact, dz, name=f"ffn{l}_down_dw", ta=True, tm=1408, tn=1024, tk=2048, out_dtype=BF16)
        dfa = _matmul(dz, w_down[l], name=f"ffn{l}_down_dx", tb=True, tm=1024, tn=1408, tk=1024, out_dtype=BF16)
        dx_parts, d_wup, dcw, dcb = _ffn_up_bwd(hf, dfa, xin, w_up[l], cw_f[l], cb_f[l], name=f"ffn{l}_up_bwd")
        tok = send_grads(f"ffn{l}", [owner_view(d_wup), owner_view(d_wdown)], dcb)
        return [(dx_parts, 1.0), (dz, alpha)], dcw, dcb, tok

    dz, dg11, db11 = _add_ln_bwd([(dy, 1.0)], x3, ffn1, lng(1, 1), alpha, name="ln11_bwd")
    dx3, dcw1, dcb1, tok = ffn_bwd(dz, x3, hf1, f1, 1)
    dz, dg10, db10 = _add_ln_bwd(dx3, x2, att, lng(1, 0) + tok, alpha, name="ln10_bwd")
    d_wo = _matmul(ao, dz, name="att_out_dw", ta=True, tm=1024, tn=1024, tk=1024, out_dtype=BF16)
    dao = _matmul(dz, w_o, name="att_out_dx", tb=True, tm=1024, tn=1024, tk=1024)
    dq, dkc, dkp, d_sinks = _attn_bwd(qkv, dao, lse, sinks, name="att_core_bwd")
    dqkv, d_bqkv = _dqkv_assemble(dq, dkc, dkp, name="att_dqkv")
    d_wqkv = _matmul(x2, dqkv, name="att_qkv_dw", ta=True, tm=1024, tn=dqkv.shape[1], tk=1024, out_dtype=BF16)
    d_wqkv_st = jnp.moveaxis(d_wqkv.reshape(d_wqkv.shape[0], 4, -1), 1, 0)
    tok = send_grads("att", [owner_view(d_wqkv_st), owner_view(d_wo)], d_bqkv)
    dx2 = _matmul(dqkv, w_qkv, name="att_qkv_dx", tb=True, tm=1024, tn=1024, tk=dqkv.shape[1], addend=(dz, alpha))
    dz, dg01, db01 = _add_ln_bwd([(dx2, 1.0)], x1, ffn0, lng(0, 1) + tok, alpha, name="ln01_bwd")
    dx1, dcw0, dcb0, tok = ffn_bwd(dz, x1, hf0, f0, 0)
    dz, dg00, db00 = _add_ln_bwd(dx1, xs, mix, lng(0, 0) + tok, alpha, name="ln00_bwd")
    d_wout = _matmul(ab, dz, name="mix_out_dw", ta=True, tm=1024, tn=1024, tk=1024, out_dtype=BF16)
    dab = _matmul(dz, w_out, name="mix_out_dx", tb=True, tm=1024, tn=1024, tk=1024)
    grad_x, d_win, d_cwa, d_cba, d_ga, d_ba, d_gb, d_bb, d_ws, d_sb = _mixer_bwd(
        h0, a2, dab, xs, w_in, dz, alpha, conv_w_a, *mix_vecs[1:], ws, wst, sbb, tril, name="mix_bwd")

    small_w = [a_conv_w, a_conv_b, a_norm_g, a_norm_b, b_norm_g, b_norm_b, b_spatial_w, b_spatial_b, c_b_qkv,
               c_sinks, ffn_conv_w, ffn_conv_b, ln_g, ln_b]
    small_m = [m_a_conv_w, m_a_conv_b, m_a_norm_g, m_a_norm_b, m_b_norm_g, m_b_norm_b, m_b_spatial_w, m_b_spatial_b,
               m_c_b_qkv, m_c_sinks, m_ffn_conv_w, m_ffn_conv_b, m_ln_g, m_ln_b]
    small_v = [v_a_conv_w, v_a_conv_b, v_a_norm_g, v_a_norm_b, v_b_norm_g, v_b_norm_b, v_b_spatial_w, v_b_spatial_b,
               v_c_b_qkv, v_c_sinks, v_ffn_conv_w, v_ffn_conv_b, v_ln_g, v_ln_b]
    local = [d_cwa, d_cba, d_ga, d_ba, d_gb, d_bb, d_ws, d_sb, d_bqkv, d_sinks, dcw0, dcb0, dcw1, dcb1,
             dg00, dg01, dg10, dg11, db00, db01, db10, db11, sq_err]
    small_out, loss = _small_tail(local, list(zip(small_w, small_m, small_v)), name="small_tail")
    loss = loss[0, 0]
    small_g = [o[0] for o in small_out]
    sm_delta = [o[1] for o in small_out]
    sm_m = [o[2] for o in small_out]
    sm_v = [o[3] for o in small_out]

    place = jnp.stack([q_idx, c_idx, 4 * lax.axis_index("x") + 2 * lax.axis_index("y") + c_idx]).astype(jnp.int32)
    where = {"mix": [(0, None), (1, None)], "att": [(2, None), (3, None)], "ffn0": [(4, 0), (5, 0)], "ffn1": [(4, 1), (5, 1)]}
    big_w = [ab_w_in, ab_w_out, c_w_qkv, c_w_o, ffn_w_up, ffn_w_down]
    big_m = [m_ab_w_in, m_ab_w_out, m_c_w_qkv, m_c_w_o, m_ffn_w_up, m_ffn_w_down]
    big_v = [v_ab_w_in, v_ab_w_out, v_c_w_qkv, v_c_w_o, v_ffn_w_up, v_ffn_w_down]
    big_out = [None] * 6

    def finish(tags, after, label):
        bufs, layout = {}, []
        for tag, g_thru, l_thru, ss, rs in in_flight:
            if tag not in tags:
                continue
            own, landed = _reduce_wait(g_thru, l_thru, ss, rs, after, name=f"reduce_wait_{tag}")
            for k, (o, lead) in enumerate(where[tag]):
                piece = own[k].shape[2:]
                shape = (2,) + piece if lead is None else (2, 2) + piece
                bufs[o] = _octo_sum(own[k], landed[k], place, bufs.get(o), (lead, shape), name=f"reduce_sum_{tag}{k}")
                layout.append((o, lead))
        order = sorted(bufs)
        shared = _sibling_share([bufs[o] for o in order], [(order.index(o), lead) for o, lead in layout],
                                name=f"reduce_share_{label}")
        for o, g in zip(order, shared):
            w = big_w[o]
            two_d = lambda a: a.reshape(-1, a.shape[-1])
            outs = _adamw(two_d(w), two_d(g), two_d(big_m[o]), two_d(big_v[o]), name=f"adamw_big{o}")
            big_out[o] = [r.reshape(w.shape) for r in outs]
        return big_out[order[-1]][0]

    tok = send_grads("mix", [owner_view(d_win), owner_view(d_wout)], after=sm_delta[0])
    done = finish(("ffn1", "att", "ffn0"), sm_delta[6] + tok, "early")
    finish(("mix",), done, "mix")

    order_big = {0: 0, 9: 1, 10: 2, 13: 3, 14: 4, 17: 5}
    order_small = {1: 0, 2: 1, 3: 2, 4: 3, 5: 4, 6: 5, 7: 6, 8: 7, 11: 8, 12: 9, 15: 10, 16: 11, 18: 12, 19: 13}
    grads, deltas, new_m, new_v = [], [], [], []
    for pos_w in range(20):
        if pos_w in order_big:
            t = order_big[pos_w]
            grads.append(big_out[t][3])
            deltas.append(big_out[t][0])
            new_m.append(big_out[t][1])
            new_v.append(big_out[t][2])
        else:
            t = order_small[pos_w]
            grads.append(small_g[t])
            deltas.append(sm_delta[t])
            new_m.append(sm_m[t])
            new_v.append(sm_v[t])
    return (loss, grad_x[None], *grads, *deltas, *new_m, *new_v)
```

```python
import math

import jax
import jax.numpy as jnp
from jax import lax
from jax.experimental import pallas as pl
from jax.experimental.pallas import tpu as pltpu

F32 = jnp.float32
BF16 = jnp.bfloat16
MESH = pl.DeviceIdType.MESH

LN_EPS = 1e-5
HEAD_DIM = 64
ATT_BLOCK = 128
Q_PER_KV = 8
A_KERNEL = 31
CONV_HALO = 32
FFN_HALO = 8
BF16_ROWS = 16
B_CHUNK = 128
LANES = 128
MXU_WIDTH = 256
GELU_C = math.sqrt(2.0 / math.pi)
ADAM_LR = 0.001
ADAM_B1 = 0.9
ADAM_B2 = 0.999
ADAM_EPS = 1e-08
ADAM_WD = 0.01
ADAM_STEP = 10
VMEM_LIMIT = 56 * 1024 * 1024


def _cp(*dims):
    return pltpu.CompilerParams(dimension_semantics=dims, vmem_limit_bytes=VMEM_LIMIT)


def _pick(n, prefs):
    for p in prefs:
        if n % p == 0:
            return p
    return n


def _sig(x):
    return 1.0 / (1.0 + jnp.exp(-x))


def _gelu(x):
    t = jnp.tanh(GELU_C * (x + 0.044715 * (x * x * x)))
    return x * (0.5 * (1.0 + t)), t


def _gelu_grad(x, t):
    return 0.5 * (1.0 + t) + 0.5 * x * (1.0 - t * t) * (GELU_C * (1.0 + 3.0 * 0.044715 * x * x))


def _ln_stats(z):
    mu = jnp.mean(z, axis=-1, keepdims=True)
    zc = z - mu
    var = jnp.mean(zc * zc, axis=-1, keepdims=True)
    rstd = lax.rsqrt(var + LN_EPS)
    return zc * rstd, rstd


def _ln_bwd(dxh, xh, rstd):
    return rstd * (dxh - jnp.mean(dxh, axis=-1, keepdims=True) - xh * jnp.mean(dxh * xh, axis=-1, keepdims=True))


def _rowsum(a):
    return jnp.sum(a, axis=0, keepdims=True)


def _lshape(a):
    return (a.shape[0], a.shape[1]) if a.ndim == 2 else (a.shape[1], a.shape[0] * a.shape[2])


def _spec2(arr, blk_r, blk_c, ridx, cidx):
    if len(arr.shape) == 2:
        return pl.BlockSpec((blk_r, blk_c), lambda i, j, k: (ridx(i, j, k), cidx(i, j, k)))
    per = arr.shape[2] // blk_c
    assert arr.shape[2] % blk_c == 0
    return pl.BlockSpec((None, blk_r, blk_c), lambda i, j, k: (cidx(i, j, k) // per, ridx(i, j, k), cidx(i, j, k) % per))


def _matmul(a, b, *, name, ta=False, tb=False, tm, tn, tk, out_dtype=F32, out_stack=None, bias=None, addend=None):
    ar, ac = _lshape(a)
    br, bc = _lshape(b)
    m, kdim = (ac, ar) if ta else (ar, ac)
    n = br if tb else bc
    assert (bc if tb else br) == kdim
    tm, tn, tk = min(tm, m), min(tn, n), min(tk, kdim)
    assert m % tm == 0 and n % tn == 0 and kdim % tk == 0, (name, m, n, kdim, tm, tn, tk)
    nk = kdim // tk
    gi, gj, gk = (lambda i, j, k: i), (lambda i, j, k: j), (lambda i, j, k: k)
    a_spec = _spec2(a, tk, tm, gk, gi) if ta else _spec2(a, tm, tk, gi, gk)
    b_spec = _spec2(b, tn, tk, gj, gk) if tb else _spec2(b, tk, tn, gk, gj)
    if out_stack is None:
        out_sds = jax.ShapeDtypeStruct((m, n), out_dtype)
    else:
        out_sds = jax.ShapeDtypeStruct((out_stack, m, n // out_stack), out_dtype)
    o_spec = _spec2(out_sds, tm, tn, gi, gj)
    in_specs = [a_spec, b_spec]
    args = [a, b]
    if bias is not None:
        in_specs.append(pl.BlockSpec((1, tn), lambda i, j, k: (0, j)))
        args.append(bias)
    scale = None
    if addend is not None:
        add_arr, scale = addend
        in_specs.append(pl.BlockSpec((tm, tn), lambda i, j, k: (i, j)))
        args.append(add_arr)
    use_acc = nk > 1 and out_dtype != F32
    dn = (((0 if ta else 1,), (1 if tb else 0,)), ((), ()))

    def body(*refs):
        a_ref, b_ref = refs[0], refs[1]
        pos = 2
        bias_ref = add_ref = None
        if bias is not None:
            bias_ref = refs[pos]
            pos += 1
        if addend is not None:
            add_ref = refs[pos]
            pos += 1
        o_ref = refs[pos]
        acc_ref = refs[pos + 1] if use_acc else o_ref
        p = lax.dot_general(a_ref[...].astype(BF16), b_ref[...].astype(BF16), dn, preferred_element_type=F32)

        def finish(val):
            if bias_ref is not None:
                val = val + bias_ref[...]
            if add_ref is not None:
                val = val + scale * add_ref[...]
            return val.astype(out_dtype)

        if nk == 1:
            o_ref[...] = finish(p)
        else:
            k = pl.program_id(2)

            @pl.when(k == 0)
            def _():
                acc_ref[...] = p

            @pl.when(k > 0)
            def _():
                acc_ref[...] += p

            if use_acc or bias_ref is not None or add_ref is not None:
                @pl.when(k == nk - 1)
                def _():
                    o_ref[...] = finish(acc_ref[...])

    return pl.pallas_call(
        body, name=name, grid=(m // tm, n // tn, nk), in_specs=in_specs, out_specs=o_spec, out_shape=out_sds,
        scratch_shapes=[pltpu.VMEM((tm, tn), F32)] if use_acc else [],
        compiler_params=_cp("parallel", "parallel", "arbitrary"),
    )(*args)


def _add_ln_fwd(x, s, g, b, alpha, *, name):
    rows, d = x.shape
    t = _pick(rows, (512, 256))

    def body(x_ref, s_ref, g_ref, b_ref, y_ref):
        xh, _ = _ln_stats(alpha * x_ref[...] + s_ref[...])
        y_ref[...] = xh * g_ref[...] + b_ref[...]

    row = pl.BlockSpec((t, d), lambda i: (i, 0))
    vec = pl.BlockSpec((1, d), lambda i: (0, 0))
    return pl.pallas_call(body, name=name, grid=(rows // t,), in_specs=[row, row, vec, vec], out_specs=row,
                          out_shape=jax.ShapeDtypeStruct((rows, d), F32), compiler_params=_cp("parallel"))(x, s, g, b)


def _add_ln_bwd(dy_terms, x, s, g, alpha, *, name):
    rows, d = x.shape
    t = _pick(rows, (512, 256))
    nterm = len(dy_terms)
    scales = [sc for _, sc in dy_terms]
    ranks = [a.ndim for a, _ in dy_terms]

    def body(*refs):
        dy_refs = refs[:nterm]
        x_ref, s_ref, g_ref, dz_ref, dg_ref, db_ref = refs[nterm:]

        @pl.when(pl.program_id(0) == 0)
        def _():
            dg_ref[...] = jnp.zeros_like(dg_ref)
            db_ref[...] = jnp.zeros_like(db_ref)

        dyv = None
        for r, sc, rank in zip(dy_refs, scales, ranks):
            slabs = [r[...]] if rank == 2 else [r[p] for p in range(r.shape[0])]
            for v in slabs:
                v = v if sc == 1.0 else sc * v
                dyv = v if dyv is None else dyv + v
        xh, rstd = _ln_stats(alpha * x_ref[...] + s_ref[...])
        dz_ref[...] = _ln_bwd(dyv * g_ref[...], xh, rstd)
        dg_ref[...] += _rowsum(dyv * xh)
        db_ref[...] += _rowsum(dyv)

    row = pl.BlockSpec((t, d), lambda i: (i, 0))
    vec = pl.BlockSpec((1, d), lambda i: (0, 0))
    vsds = jax.ShapeDtypeStruct((1, d), F32)
    dy_specs = [row if a.ndim == 2 else pl.BlockSpec((a.shape[0], t, d), lambda i: (0, i, 0)) for a, _ in dy_terms]
    return pl.pallas_call(body, name=name, grid=(rows // t,), in_specs=dy_specs + [row, row, vec], out_specs=[row, vec, vec],
                          out_shape=[jax.ShapeDtypeStruct((rows, d), F32), vsds, vsds],
                          compiler_params=_cp("arbitrary"))(*[a for a, _ in dy_terms], x, s, g)


def _add_ln_loss(x, s, g, b, tgt, alpha, *, name):
    rows, d = x.shape
    t = _pick(rows, (512, 256))

    def body(x_ref, s_ref, g_ref, b_ref, t_ref, l_ref, dy_ref):
        @pl.when(pl.program_id(0) == 0)
        def _():
            l_ref[...] = jnp.zeros_like(l_ref)

        xh, _ = _ln_stats(alpha * x_ref[...] + s_ref[...])
        e = (xh * g_ref[...] + b_ref[...]) - t_ref[...]
        l_ref[...] += _rowsum(e * e)
        dy_ref[...] = e * (1.0 / d)

    row = pl.BlockSpec((t, d), lambda i: (i, 0))
    vec = pl.BlockSpec((1, d), lambda i: (0, 0))
    return pl.pallas_call(body, name=name, grid=(rows // t,), in_specs=[row, row, vec, vec, row], out_specs=[vec, row],
                          out_shape=[jax.ShapeDtypeStruct((1, d), F32), jax.ShapeDtypeStruct((rows, d), F32)],
                          compiler_params=_cp("arbitrary"))(x, s, g, b, tgt)


def _col_blocks(width, step):
    return [slice(pos, min(pos + step, width)) for pos in range(0, width, step)]


def _conv3(e, w, b):
    r1 = pltpu.roll(e, 1, 0)
    r2 = pltpu.roll(e, 2, 0)
    return w[0:1, :] * r2 + w[1:2, :] * r1 + w[2:3, :] * e + b, (r2, r1, e)


def _ffn_up_fwd(x, w_up, cw, cb, *, name):
    rows, d = x.shape
    nq, _, tc = w_up.shape
    nj = nq // 2
    f = tc * nj
    tm = _pick(rows, (512, 256))
    blocks = _col_blocks(tc, tc)

    def body(x_ref, wg_ref, wv_ref, cw_ref, cb_ref, hf_ref, f_ref, prev_ref):
        @pl.when(pl.program_id(1) == 0)
        def _():
            prev_ref[...] = jnp.zeros_like(prev_ref)

        xb = x_ref[...].astype(BF16)
        for cs in blocks:
            hc = []
            for s, w_ref in ((0, wg_ref), (1, wv_ref)):
                h = jnp.dot(xb, w_ref[:, cs], preferred_element_type=F32)
                hf_ref[s, :, cs] = h
                e = jnp.concatenate([prev_ref[s, :, cs], h], axis=0)
                prev_ref[s, :, cs] = h[tm - FFN_HALO:]
                y, _ = _conv3(e, cw_ref[s, :, cs], cb_ref[s, :, cs])
                hc.append(y[FFN_HALO:])
            gl, _ = _gelu(hc[0])
            f_ref[:, cs] = (gl * hc[1]).astype(BF16)

    in_specs = [
        pl.BlockSpec((tm, d), lambda j, i: (i, 0)),
        pl.BlockSpec((None, d, tc), lambda j, i: (j, 0, 0)),
        pl.BlockSpec((None, d, tc), lambda j, i: (nj + j, 0, 0)),
        pl.BlockSpec((2, 3, tc), lambda j, i: (0, 0, j)),
        pl.BlockSpec((2, 1, tc), lambda j, i: (0, 0, j)),
    ]
    out_specs = [pl.BlockSpec((2, tm, tc), lambda j, i: (0, i, j)), pl.BlockSpec((tm, tc), lambda j, i: (i, j))]
    out_shape = [jax.ShapeDtypeStruct((2, rows, f), F32), jax.ShapeDtypeStruct((rows, f), BF16)]
    return pl.pallas_call(body, name=name, grid=(nj, rows // tm), in_specs=in_specs, out_specs=out_specs, out_shape=out_shape,
                          scratch_shapes=[pltpu.VMEM((2, FFN_HALO, tc), F32)],
                          compiler_params=_cp("parallel", "arbitrary"))(x, w_up, w_up, cw, cb)


def _ffn_up_bwd(hf, df, x, w_up, cw, cb, *, name):
    _, rows, f = hf.shape
    d = x.shape[1]
    nq, _, tc = w_up.shape
    nj = nq // 2
    tm = _pick(rows, (512, 256))
    hb = tm // FFN_HALO
    once = pl.Buffered(1)
    ni = rows // tm
    last_blk = rows // FFN_HALO - 1
    ext = tm + 2 * FFN_HALO
    tile = slice(FFN_HALO, FFN_HALO + tm)
    blocks = _col_blocks(tc, MXU_WIDTH)

    def body(h_ref, hp_ref, hn_ref, d_ref, dn_ref, x_ref, wg_ref, wv_ref, cw_ref, cb_ref, dx_ref, dw_out_ref, dcw_ref, dcb_ref,
             dw_ref):
        i = pl.program_id(1)
        first = i == 0
        last = i == ni - 1

        @pl.when(first)
        def _():
            dw_ref[...] = jnp.zeros_like(dw_ref)
            dcw_ref[...] = jnp.zeros_like(dcw_ref)
            dcb_ref[...] = jnp.zeros_like(dcb_ref)

        xt = x_ref[...].astype(BF16).T
        dx = None
        for cs in blocks:
            wc = cs.stop - cs.start
            d_next = dn_ref[:, cs].astype(F32)[0:FFN_HALO]
            de = jnp.concatenate([jnp.zeros((FFN_HALO, wc), F32), d_ref[:, cs].astype(F32), jnp.where(last, 0.0, d_next)], axis=0)
            taps, hc = [], []
            for s in range(2):
                e = jnp.concatenate([jnp.where(first, 0.0, hp_ref[s, :, cs]), h_ref[s, :, cs], hn_ref[s, :, cs]], axis=0)
                y, tp = _conv3(e, cw_ref[s, :, cs], cb_ref[s, :, cs])
                hc.append(y)
                taps.append(tp)
            gl, th = _gelu(hc[0])
            dhc = (de * hc[1] * _gelu_grad(hc[0], th), de * gl)
            for s, w_ref in ((0, wg_ref), (1, wv_ref)):
                w = cw_ref[s, :, cs]
                g = dhc[s]
                dh = (w[2:3, :] * g + w[1:2, :] * pltpu.roll(g, ext - 1, 0) + w[0:1, :] * pltpu.roll(g, ext - 2, 0))[tile]
                gt = g[tile]
                for k in range(3):
                    dcw_ref[s, k:k + 1, cs] += _rowsum(gt * taps[s][k][tile])
                dcb_ref[s, :, cs] += _rowsum(gt)
                dhb = dh.astype(BF16)
                part = lax.dot_general(dhb, w_ref[:, cs], (((1,), (1,)), ((), ())), preferred_element_type=F32)
                dx = part if dx is None else dx + part
                dw_ref[s, :, cs] += jnp.dot(xt, dhb, preferred_element_type=F32)
        dx_ref[...] = dx

        @pl.when(last)
        def _():
            dw_out_ref[...] = dw_ref[...].astype(BF16)

    in_specs = [
        pl.BlockSpec((2, tm, tc), lambda j, i: (0, i, j)),
        pl.BlockSpec((2, FFN_HALO, tc), lambda j, i: (0, jnp.maximum(i * hb - 1, 0), j)),
        pl.BlockSpec((2, FFN_HALO, tc), lambda j, i: (0, jnp.minimum((i + 1) * hb, last_blk), j)),
        pl.BlockSpec((tm, tc), lambda j, i: (i, j)),
        pl.BlockSpec((BF16_ROWS, tc), lambda j, i: (jnp.minimum((i + 1) * (tm // BF16_ROWS), rows // BF16_ROWS - 1), j)),
        pl.BlockSpec((tm, d), lambda j, i: (i, 0)),
        pl.BlockSpec((None, d, tc), lambda j, i: (j, 0, 0), pipeline_mode=once),
        pl.BlockSpec((None, d, tc), lambda j, i: (nj + j, 0, 0), pipeline_mode=once),
        pl.BlockSpec((2, 3, tc), lambda j, i: (0, 0, j)),
        pl.BlockSpec((2, 1, tc), lambda j, i: (0, 0, j)),
    ]
    out_specs = [
        pl.BlockSpec((None, tm, d), lambda j, i: (j, i, 0)),
        pl.BlockSpec((2, None, d, tc), lambda j, i: (0, j, 0, 0), pipeline_mode=once),
        pl.BlockSpec((2, 3, tc), lambda j, i: (0, 0, j)),
        pl.BlockSpec((2, 1, tc), lambda j, i: (0, 0, j)),
    ]
    out_shape = [jax.ShapeDtypeStruct((nj, rows, d), F32), jax.ShapeDtypeStruct((2, nj, d, tc), BF16),
                 jax.ShapeDtypeStruct((2, 3, f), F32), jax.ShapeDtypeStruct((2, 1, f), F32)]
    dx, dw, dcw, dcb = pl.pallas_call(body, name=name, grid=(nj, ni), in_specs=in_specs, out_specs=out_specs,
                                      out_shape=out_shape, scratch_shapes=[pltpu.VMEM((2, d, tc), F32)],
                                      compiler_params=_cp("parallel", "arbitrary"))(
        hf, hf, hf, df, df, x, w_up, w_up, cw, cb)
    return dx, dw.reshape(nq, d, tc), dcw, dcb


def _mixer_fwd(x, w_in, cw, cb, ga, ba, gb, bb, ws, sbb, *, name):
    rows, d = x.shape
    _, _, w = w_in.shape
    t = _pick(rows, (256,))
    groups = w // B_CHUNK

    def body(x_ref, win_ref, cw_ref, cb_ref, ga_ref, ba_ref, gb_ref, bb_ref, ws_ref, sb_ref, h_ref, o_ref, a2_ref, prev_ref):
        @pl.when(pl.program_id(0) == 0)
        def _():
            prev_ref[...] = jnp.zeros_like(prev_ref)

        xb = x_ref[...].astype(BF16)
        for s in range(4):
            h_ref[s] = jnp.dot(xb, win_ref[s], preferred_element_type=F32)
        a1 = h_ref[0] * _sig(h_ref[1])
        e = jnp.concatenate([prev_ref[...], a1], axis=0)
        prev_ref[...] = a1[t - CONV_HALO:]
        acc = cw_ref[A_KERNEL - 1:A_KERNEL, :] * e
        for k in range(A_KERNEL - 1):
            acc = acc + cw_ref[k:k + 1, :] * pltpu.roll(e, A_KERNEL - 1 - k, 0)
        a2 = acc[CONV_HALO:] + cb_ref[...]
        a2_ref[...] = a2
        xh, _ = _ln_stats(a2)
        a3 = xh * ga_ref[...] + ba_ref[...]
        o_ref[:, 0:w] = (a3 * _sig(a3)).astype(BF16)

        u, _ = _gelu(h_ref[2])
        v1, _ = _gelu(h_ref[3])
        xh2, _ = _ln_stats(v1)
        v2 = (xh2 * gb_ref[...] + bb_ref[...]).astype(BF16)
        for c in range(t // B_CHUNK):
            rs = slice(c * B_CHUNK, (c + 1) * B_CHUNK)
            for g in range(groups):
                cs = slice(g * B_CHUNK, (g + 1) * B_CHUNK)
                mixed = jnp.dot(ws_ref[g], v2[rs, cs], preferred_element_type=F32) + sb_ref[g]
                o_ref[rs, w + g * B_CHUNK:w + (g + 1) * B_CHUNK] = (u[rs, cs] * mixed).astype(BF16)

    vec = pl.BlockSpec((1, w), lambda i: (0, 0))
    grp = pl.BlockSpec((groups, B_CHUNK, B_CHUNK), lambda i: (0, 0, 0))
    in_specs = [
        pl.BlockSpec((t, d), lambda i: (i, 0)),
        pl.BlockSpec((4, d, w), lambda i: (0, 0, 0)),
        pl.BlockSpec((A_KERNEL, w), lambda i: (0, 0)),
        vec, vec, vec, vec, vec, grp, grp,
    ]
    out_specs = [pl.BlockSpec((4, t, w), lambda i: (0, i, 0)), pl.BlockSpec((t, 2 * w), lambda i: (i, 0)),
                 pl.BlockSpec((t, w), lambda i: (i, 0))]
    out_shape = [jax.ShapeDtypeStruct((4, rows, w), F32), jax.ShapeDtypeStruct((rows, 2 * w), BF16),
                 jax.ShapeDtypeStruct((rows, w), F32)]
    return pl.pallas_call(body, name=name, grid=(rows // t,), in_specs=in_specs, out_specs=out_specs, out_shape=out_shape,
                          scratch_shapes=[pltpu.VMEM((CONV_HALO, w), F32)],
                          compiler_params=_cp("arbitrary"))(x, w_in, cw, cb, ga, ba, gb, bb, ws, sbb)


def _mixer_bwd(h0, a2, dab, x, w_in, res, res_scale, cw, ga, ba, gb, bb, ws, wst, sbb, tril, *, name):
    _, rows, w = h0.shape
    d = x.shape[1]
    once = pl.Buffered(1)
    t = _pick(rows, (256,))
    hb = t // CONV_HALO
    ni = rows // t
    last_blk = rows // CONV_HALO - 1
    ext = t + CONV_HALO
    tile = slice(0, t)
    groups = w // B_CHUNK
    taps = A_KERNEL - 1

    def body(h_ref, a2_ref, a2n_ref, d_ref, dn_ref, x_ref, win_ref, res_ref, cw_ref, ga_ref, ba_ref, gb_ref, bb_ref,
             ws_ref, wst_ref, sb_ref, tril_ref, dx_ref, dwin_ref, dcw_ref, dcb_ref, dga_ref, dba_ref, dgb_ref, dbb_ref,
             dws_ref, dsb_ref, dw_ref):
        i = pl.program_id(0)
        first = i == 0
        last = i == ni - 1

        @pl.when(first)
        def _():
            for r in (dw_ref, dcw_ref, dcb_ref, dga_ref, dba_ref, dgb_ref, dbb_ref, dws_ref, dsb_ref):
                r[...] = jnp.zeros_like(r)

        xt = x_ref[...].astype(BF16).T
        dx_terms = []

        def through_w_in(slot, dh):
            dhb = dh.astype(BF16)
            dx_terms.append(lax.dot_general(dhb, win_ref[slot], (((1,), (1,)), ((), ())), preferred_element_type=F32))
            dw_ref[slot] += jnp.dot(xt, dhb, preferred_element_type=F32)

        xh, rstd = _ln_stats(jnp.concatenate([a2_ref[...], a2n_ref[...]], axis=0))
        a3 = xh * ga_ref[...] + ba_ref[...]
        s3 = _sig(a3)
        da_e = jnp.concatenate([d_ref[:, 0:w], jnp.where(last, 0.0, dn_ref[...])], axis=0)
        da3 = da_e * (s3 * (1.0 + a3 * (1.0 - s3)))
        da2 = _ln_bwd(da3 * ga_ref[...], xh, rstd)
        dga_ref[...] += _rowsum(da3[tile] * xh[tile])
        dba_ref[...] += _rowsum(da3[tile])
        dcb_ref[...] += _rowsum(da2[tile])
        sgt = _sig(h_ref[1])
        a1t = h_ref[0] * sgt
        da1t = None
        for k in range(A_KERNEL):
            sh = taps - k
            fed = (da2 if sh == 0 else pltpu.roll(da2, ext - sh, 0))[tile]
            dcw_ref[k:k + 1, :] += _rowsum(a1t * fed)
            term = cw_ref[k:k + 1, :] * fed
            da1t = term if da1t is None else da1t + term
        through_w_in(0, da1t * sgt)
        through_w_in(1, da1t * h_ref[0] * sgt * (1.0 - sgt))

        bu = h_ref[2]
        bv = h_ref[3]
        u, tu = _gelu(bu)
        v1, tv = _gelu(bv)
        xh2, rstd2 = _ln_stats(v1)
        v2 = (xh2 * gb_ref[...] + bb_ref[...]).astype(BF16)
        db = d_ref[:, w:2 * w]
        dmx_all = db * u
        du_parts, dv2_parts = [], []
        for c in range(t // B_CHUNK):
            rs = slice(c * B_CHUNK, (c + 1) * B_CHUNK)
            du_row, dv2_row = [], []
            for g in range(groups):
                cs = slice(g * B_CHUNK, (g + 1) * B_CHUNK)
                v2cg = v2[rs, cs]
                mixed = jnp.dot(ws_ref[g], v2cg, preferred_element_type=F32) + sb_ref[g]
                dmx = dmx_all[rs, cs]
                dmxb = dmx.astype(BF16)
                du_row.append(db[rs, cs] * mixed)
                dv2_row.append(jnp.dot(wst_ref[g], dmxb, preferred_element_type=F32))
                dws_ref[g] += tril_ref[...] * lax.dot_general(dmxb, v2cg, (((1,), (1,)), ((), ())),
                                                               preferred_element_type=F32)
                dsb_ref[g:g + 1, :] += _rowsum(dmx.T)
            du_parts.append(jnp.concatenate(du_row, axis=1))
            dv2_parts.append(jnp.concatenate(dv2_row, axis=1))
        du = jnp.concatenate(du_parts, axis=0)
        dv2 = jnp.concatenate(dv2_parts, axis=0)
        dgb_ref[...] += _rowsum(dv2 * xh2)
        dbb_ref[...] += _rowsum(dv2)
        dv1 = _ln_bwd(dv2 * gb_ref[...], xh2, rstd2)
        through_w_in(2, du * _gelu_grad(bu, tu))
        through_w_in(3, dv1 * _gelu_grad(bv, tv))
        dx_ref[...] = res_scale * res_ref[...] + ((dx_terms[0] + dx_terms[1]) + (dx_terms[2] + dx_terms[3]))

        @pl.when(last)
        def _():
            dwin_ref[...] = dw_ref[...].astype(BF16)

    vec = pl.BlockSpec((1, w), lambda i: (0, 0))
    grp = pl.BlockSpec((groups, B_CHUNK, B_CHUNK), lambda i: (0, 0, 0))
    halo = pl.BlockSpec((CONV_HALO, w), lambda i: (jnp.minimum((i + 1) * hb, last_blk), 0))
    wide = pl.BlockSpec((t, d), lambda i: (i, 0))
    in_specs = [
        pl.BlockSpec((4, t, w), lambda i: (0, i, 0)),
        pl.BlockSpec((t, w), lambda i: (i, 0)),
        halo,
        pl.BlockSpec((t, 2 * w), lambda i: (i, 0)),
        halo,
        wide,
        pl.BlockSpec((4, d, w), lambda i: (0, 0, 0), pipeline_mode=once),
        wide,
        pl.BlockSpec((A_KERNEL, w), lambda i: (0, 0)),
        vec, vec, vec, vec, grp, grp, grp,
        pl.BlockSpec((B_CHUNK, B_CHUNK), lambda i: (0, 0)),
    ]
    vsds = jax.ShapeDtypeStruct((1, w), F32)
    out_specs = [
        wide,
        pl.BlockSpec((4, d, w), lambda i: (0, 0, 0), pipeline_mode=once),
        pl.BlockSpec((A_KERNEL, w), lambda i: (0, 0)),
        vec, vec, vec, vec, vec, grp,
        pl.BlockSpec((groups, B_CHUNK), lambda i: (0, 0)),
    ]
    out_shape = [jax.ShapeDtypeStruct((rows, d), F32), jax.ShapeDtypeStruct((4, d, w), BF16),
                 jax.ShapeDtypeStruct((A_KERNEL, w), F32),
                 vsds, vsds, vsds, vsds, vsds, jax.ShapeDtypeStruct((groups, B_CHUNK, B_CHUNK), F32),
                 jax.ShapeDtypeStruct((groups, B_CHUNK), F32)]
    return pl.pallas_call(body, name=name, grid=(ni,), in_specs=in_specs, out_specs=out_specs, out_shape=out_shape,
                          scratch_shapes=[pltpu.VMEM((4, d, w), F32)], compiler_params=_cp("arbitrary"))(
        h0, a2, a2, dab, dab, x, w_in, res, cw, ga, ba, gb, bb, ws, wst, sbb, tril)


GROUP_ROWS = Q_PER_KV * ATT_BLOCK


def _attn_mask(n):
    qi = lax.broadcasted_iota(jnp.int32, (GROUP_ROWS, 2 * ATT_BLOCK), 0) & (ATT_BLOCK - 1)
    sj = lax.broadcasted_iota(jnp.int32, (GROUP_ROWS, 2 * ATT_BLOCK), 1)
    diff = qi + ATT_BLOCK - sj
    return (diff >= 0) & (diff < ATT_BLOCK) & ((n > 0) | (sj >= ATT_BLOCK))


def _stack_heads(ref, kvh, dtype):
    heads = [ref[:, (kvh * Q_PER_KV + g) * HEAD_DIM:(kvh * Q_PER_KV + g + 1) * HEAD_DIM] for g in range(Q_PER_KV)]
    return jnp.concatenate(heads, axis=0).astype(dtype)


def _per_row_sink(sink_ref, kvh):
    head = lax.broadcasted_iota(jnp.int32, (GROUP_ROWS, 1), 0) // ATT_BLOCK
    out = jnp.zeros((GROUP_ROWS, 1), F32)
    for g in range(Q_PER_KV):
        out = jnp.where(head == g, sink_ref[kvh * Q_PER_KV + g], out)
    return out


def _attn_specs(rows, n_q):
    dq = n_q * HEAD_DIM
    dkv = 2 * (n_q // Q_PER_KV) * HEAD_DIM
    kv_blk = dq // dkv
    assert dq % dkv == 0
    return dq, dkv, [
        pl.BlockSpec(memory_space=pltpu.SMEM),
        pl.BlockSpec((ATT_BLOCK, dq), lambda n: (n, 0)),
        pl.BlockSpec((ATT_BLOCK, dkv), lambda n: (n, kv_blk)),
        pl.BlockSpec((ATT_BLOCK, dkv), lambda n: (jnp.maximum(n - 1, 0), kv_blk)),
    ]


def _kv_pair(kvc_ref, kvp_ref, kvh, n_kv):
    ks = slice(kvh * HEAD_DIM, (kvh + 1) * HEAD_DIM)
    vs = slice((n_kv + kvh) * HEAD_DIM, (n_kv + kvh + 1) * HEAD_DIM)
    kk = jnp.concatenate([kvp_ref[:, ks], kvc_ref[:, ks]], axis=0).astype(BF16)
    vv = jnp.concatenate([kvp_ref[:, vs], kvc_ref[:, vs]], axis=0).astype(BF16)
    return kk, vv


def _attn_fwd(qkv, sinks, *, name):
    rows = qkv.shape[0]
    n_q = sinks.shape[0]
    n_kv = n_q // Q_PER_KV
    scale = 1.0 / math.sqrt(HEAD_DIM)
    dq, _, in_specs = _attn_specs(rows, n_q)

    def body(sink_ref, q_ref, kvc_ref, kvp_ref, o_ref, lse_ref):
        valid = _attn_mask(pl.program_id(0))
        for kvh in range(n_kv):
            kk, vv = _kv_pair(kvc_ref, kvp_ref, kvh, n_kv)
            qs = _stack_heads(q_ref, kvh, BF16)
            s = lax.dot_general(qs, kk, (((1,), (1,)), ((), ())), preferred_element_type=F32)
            s = jnp.where(valid, s * scale, -jnp.inf)
            sk = _per_row_sink(sink_ref, kvh)
            m = jnp.maximum(jnp.max(s, axis=1, keepdims=True), sk)
            p = jnp.exp(s - m)
            l = jnp.sum(p, axis=1, keepdims=True) + jnp.exp(sk - m)
            o = jnp.dot((p / l).astype(BF16), vv, preferred_element_type=F32)
            lse = m + jnp.log(l)
            for g in range(Q_PER_KV):
                h = kvh * Q_PER_KV + g
                rs = slice(g * ATT_BLOCK, (g + 1) * ATT_BLOCK)
                o_ref[:, h * HEAD_DIM:(h + 1) * HEAD_DIM] = o[rs]
                lse_ref[:, h:h + 1] = lse[rs]

    out_specs = [pl.BlockSpec((ATT_BLOCK, dq), lambda n: (n, 0)), pl.BlockSpec((ATT_BLOCK, n_q), lambda n: (n, 0))]
    out_shape = [jax.ShapeDtypeStruct((rows, dq), F32), jax.ShapeDtypeStruct((rows, n_q), F32)]
    return pl.pallas_call(body, name=name, grid=(rows // ATT_BLOCK,), in_specs=in_specs, out_specs=out_specs,
                          out_shape=out_shape, compiler_params=_cp("parallel"))(sinks, qkv, qkv, qkv)


def _attn_bwd(qkv, dout, lse, sinks, *, name):
    rows = qkv.shape[0]
    n_q = sinks.shape[0]
    n_kv = n_q // Q_PER_KV
    scale = 1.0 / math.sqrt(HEAD_DIM)
    dq_w, dkv_w, in_specs = _attn_specs(rows, n_q)
    blk_q = pl.BlockSpec((ATT_BLOCK, dq_w), lambda n: (n, 0))
    blk_kv = pl.BlockSpec((ATT_BLOCK, dkv_w), lambda n: (n, 0))
    in_specs = in_specs + [blk_q, pl.BlockSpec((ATT_BLOCK, n_q), lambda n: (n, 0))]

    def body(sink_ref, q_ref, kvc_ref, kvp_ref, do_ref, lse_ref, dq_ref, dkc_ref, dkp_ref, dsink_ref):
        n = pl.program_id(0)

        @pl.when(n == 0)
        def _():
            dsink_ref[...] = jnp.zeros_like(dsink_ref)

        valid = _attn_mask(n)
        head_ids = lax.broadcasted_iota(jnp.int32, (1, n_q), 1)
        dsink = jnp.zeros((1, n_q), F32)
        for kvh in range(n_kv):
            kk, vv = _kv_pair(kvc_ref, kvp_ref, kvh, n_kv)
            qs = _stack_heads(q_ref, kvh, BF16)
            dos = _stack_heads(do_ref, kvh, BF16)
            lse = jnp.concatenate([lse_ref[:, kvh * Q_PER_KV + g:kvh * Q_PER_KV + g + 1] for g in range(Q_PER_KV)], axis=0)
            s = lax.dot_general(qs, kk, (((1,), (1,)), ((), ())), preferred_element_type=F32)
            s = jnp.where(valid, s * scale, -jnp.inf)
            p = jnp.exp(s - lse)
            dp = lax.dot_general(dos, vv, (((1,), (1,)), ((), ())), preferred_element_type=F32)
            delta = jnp.sum(p * dp, axis=1, keepdims=True)
            ds = (p * (dp - delta) * scale).astype(BF16)
            sink_term = jnp.exp(_per_row_sink(sink_ref, kvh) - lse) * delta
            dqs = jnp.dot(ds, kk, preferred_element_type=F32)
            for g in range(Q_PER_KV):
                h = kvh * Q_PER_KV + g
                rs = slice(g * ATT_BLOCK, (g + 1) * ATT_BLOCK)
                dsink = dsink + jnp.where(head_ids == h, -jnp.sum(sink_term[rs]), 0.0)
                dq_ref[:, h * HEAD_DIM:(h + 1) * HEAD_DIM] = dqs[rs]
            dk = lax.dot_general(ds, qs, (((0,), (0,)), ((), ())), preferred_element_type=F32)
            dv = lax.dot_general(p.astype(BF16), dos, (((0,), (0,)), ((), ())), preferred_element_type=F32)
            ks = slice(kvh * HEAD_DIM, (kvh + 1) * HEAD_DIM)
            vs = slice((n_kv + kvh) * HEAD_DIM, (n_kv + kvh + 1) * HEAD_DIM)
            dkp_ref[:, ks] = dk[0:ATT_BLOCK]
            dkc_ref[:, ks] = dk[ATT_BLOCK:]
            dkp_ref[:, vs] = dv[0:ATT_BLOCK]
            dkc_ref[:, vs] = dv[ATT_BLOCK:]
        dsink_ref[...] += dsink

    out_specs = [blk_q, blk_kv, blk_kv, pl.BlockSpec((1, n_q), lambda n: (0, 0))]
    out_shape = [jax.ShapeDtypeStruct((rows, dq_w), F32), jax.ShapeDtypeStruct((rows, dkv_w), F32),
                 jax.ShapeDtypeStruct((rows, dkv_w), F32), jax.ShapeDtypeStruct((1, n_q), F32)]
    return pl.pallas_call(body, name=name, grid=(rows // ATT_BLOCK,), in_specs=in_specs, out_specs=out_specs,
                          out_shape=out_shape, compiler_params=_cp("arbitrary"))(sinks, qkv, qkv, qkv, dout, lse)


def _dqkv_assemble(dq, dkc, dkp, *, name):
    rows, dq_w = dq.shape
    dkv_w = dkc.shape[1]
    nb = rows // ATT_BLOCK

    def body(dq_ref, dkc_ref, dkp_ref, o_ref, db_ref):
        n = pl.program_id(0)

        @pl.when(n == 0)
        def _():
            db_ref[...] = jnp.zeros_like(db_ref)

        dqv = dq_ref[...]
        dkv = dkc_ref[...] + jnp.where(n == nb - 1, 0.0, dkp_ref[...])
        o_ref[:, 0:dq_w] = dqv.astype(BF16)
        o_ref[:, dq_w:dq_w + dkv_w] = dkv.astype(BF16)
        db_ref[:, 0:dq_w] += _rowsum(dqv)
        db_ref[:, dq_w:dq_w + dkv_w] += _rowsum(dkv)

    width = dq_w + dkv_w
    in_specs = [pl.BlockSpec((ATT_BLOCK, dq_w), lambda n: (n, 0)), pl.BlockSpec((ATT_BLOCK, dkv_w), lambda n: (n, 0)),
                pl.BlockSpec((ATT_BLOCK, dkv_w), lambda n: (jnp.minimum(n + 1, nb - 1), 0))]
    out_specs = [pl.BlockSpec((ATT_BLOCK, width), lambda n: (n, 0)), pl.BlockSpec((1, width), lambda n: (0, 0))]
    out_shape = [jax.ShapeDtypeStruct((rows, width), BF16), jax.ShapeDtypeStruct((1, width), F32)]
    return pl.pallas_call(body, name=name, grid=(nb,), in_specs=in_specs, out_specs=out_specs, out_shape=out_shape,
                          compiler_params=_cp("arbitrary"))(dq, dkc, dkp)


def _row_tile(r, c):
    budget = 2 * 1024 * 1024 // (4 * c)
    for cand in (1024, 512, 256, 128, 64, 32, 16):
        if cand <= budget and r % cand == 0:
            return cand
    return r


def _octo_sum(own, recv, place, dest, lead, *, name):
    _, _, r, c = own.shape
    t = _row_tile(r, c)
    lead_idx, buf_shape = lead

    def body(place_ref, own_ref, *rest):
        o_ref = rest[7] if dest is None else rest[8]
        acc = own_ref[...].astype(F32)
        for k in range(7):
            acc = acc + rest[k][...].astype(F32)
        o_ref[...] = acc

    def peer(mask):
        return pl.BlockSpec((None, t, c), lambda i, pr: (pr[2] ^ mask, i, 0))

    if lead_idx is None:
        o_spec = pl.BlockSpec((None, t, c), lambda i, pr: (pr[1], i, 0))
    else:
        o_spec = pl.BlockSpec((None, None, t, c), lambda i, pr: (lead_idx, pr[1], i, 0))
    in_specs = [pl.BlockSpec((None, None, t, c), lambda i, pr: (pr[0], pr[1], i, 0))] + [peer(m) for m in range(1, 8)]
    args = [place, own] + [recv] * 7
    aliases = {}
    if dest is not None:
        in_specs.append(HBM)
        args.append(dest)
        aliases = {9: 0}
    grid_spec = pltpu.PrefetchScalarGridSpec(num_scalar_prefetch=1, grid=(r // t,), in_specs=in_specs, out_specs=o_spec)
    return pl.pallas_call(body, name=name, grid_spec=grid_spec, out_shape=jax.ShapeDtypeStruct(buf_shape, F32),
                          input_output_aliases=aliases, compiler_params=_cp("parallel"))(*args)


def _adamw_math(w, g, m, v):
    nm = ADAM_B1 * m + (1.0 - ADAM_B1) * g
    nv = ADAM_B2 * v + (1.0 - ADAM_B2) * (g * g)
    m_hat = nm / (1.0 - ADAM_B1 ** ADAM_STEP)
    v_hat = nv / (1.0 - ADAM_B2 ** ADAM_STEP)
    return -ADAM_LR * (m_hat / (jnp.sqrt(v_hat) + ADAM_EPS) + ADAM_WD * w), nm, nv


def _adamw(w, g, m, v, *, name):
    r, c = w.shape
    t = _row_tile(r, c)

    def body(w_ref, g_ref, m_ref, v_ref, d_ref, nm_ref, nv_ref, go_ref):
        gv = g_ref[...]
        d_ref[...], nm_ref[...], nv_ref[...] = _adamw_math(w_ref[...], gv, m_ref[...], v_ref[...])
        go_ref[...] = gv

    blk = pl.BlockSpec((t, c), lambda i: (i, 0))
    sds = jax.ShapeDtypeStruct((r, c), F32)
    return pl.pallas_call(body, name=name, grid=(r // t,), in_specs=[blk] * 4, out_specs=[blk] * 4,
                          out_shape=[sds] * 4, compiler_params=_cp("parallel"))(w, g, m, v)


HBM = pl.BlockSpec(memory_space=pl.ANY)


def _place():
    x, y, c = lax.axis_index("x"), lax.axis_index("y"), lax.axis_index("c")
    chips = [(1 - x, y), (x, 1 - y), (1 - x, 1 - y)]
    return x, y, c, 2 * x + y, (x, y, 1 - c), chips


def _rcopy(src, dst, ssem, rsem, dev):
    return pltpu.make_async_remote_copy(src_ref=src, dst_ref=dst, send_sem=ssem, recv_sem=rsem, device_id=dev,
                                        device_id_type=MESH)


HBM_ONLY = pl.BlockSpec(memory_space=pltpu.HBM)
SEM = pl.BlockSpec(memory_space=pltpu.SEMAPHORE)


def _peers():
    x, y, c = lax.axis_index("x"), lax.axis_index("y"), lax.axis_index("c")
    out = []
    for mask in range(1, 8):
        px = 1 - x if mask & 4 else x
        py = 1 - y if mask & 2 else y
        pc = 1 - c if mask & 1 else c
        out.append(((px, py, pc), 2 * px + py, pc, 4 * px + 2 * py + pc))
    return 4 * x + 2 * y + c, out


def _reduce_start(grads, lands, after, *, name):
    nt = len(grads)

    def body(*refs):
        ssems, rsems = refs[2 * nt + 1:3 * nt + 1], refs[3 * nt + 1:4 * nt + 1]
        g_out, l_out, token = refs[4 * nt + 1:5 * nt + 1], refs[5 * nt + 1:6 * nt + 1], refs[6 * nt + 1]
        me, peers = _peers()
        for t in range(nt):
            for k, (dev, chip, core, _) in enumerate(peers):
                _rcopy(g_out[t].at[chip, core], l_out[t].at[me], ssems[t].at[k], rsems[t].at[k], dev).start()
        token[...] = jnp.zeros_like(token)

    sems = [pltpu.SemaphoreType.DMA((7,))] * (2 * nt)
    out_shape = (sems + [pltpu.HBM(g.shape, g.dtype) for g in grads] + [pltpu.HBM(l.shape, l.dtype) for l in lands]
                 + [jax.ShapeDtypeStruct((8, LANES), F32)])
    res = pl.pallas_call(
        body, name=name, in_specs=[HBM_ONLY] * (2 * nt + 1),
        out_specs=[SEM] * (2 * nt) + [HBM_ONLY] * (2 * nt) + [pl.BlockSpec(memory_space=pltpu.VMEM)], out_shape=out_shape,
        input_output_aliases={t: 2 * nt + t for t in range(2 * nt)},
        compiler_params=pltpu.CompilerParams(has_side_effects=DATAFLOW),
    )(*[pltpu.with_memory_space_constraint(a, pltpu.HBM) for a in list(grads) + list(lands) + [after]])
    return res[:nt], res[nt:2 * nt], res[2 * nt:3 * nt], res[3 * nt:4 * nt], res[4 * nt]


def _reduce_wait(grads, lands, ssems, rsems, after, *, name):
    nt = len(grads)

    def body(*refs):
        ssem_refs, rsem_refs = refs[2 * nt:3 * nt], refs[3 * nt:4 * nt]
        g_out, l_out = refs[4 * nt + 1:5 * nt + 1], refs[5 * nt + 1:6 * nt + 1]
        me, peers = _peers()
        for t in range(nt):
            for k, (dev, chip, core, _) in enumerate(peers):
                _rcopy(g_out[t].at[chip, core], l_out[t].at[me], ssem_refs[t].at[k], rsem_refs[t].at[k], dev).wait_send()
        for t in range(nt):
            for k, (dev, _, _, idx) in enumerate(peers):
                slot = l_out[t].at[idx]
                _rcopy(slot, slot, ssem_refs[t].at[k], rsem_refs[t].at[k], dev).wait_recv()

    res = pl.pallas_call(
        body, name=name, in_specs=[HBM_ONLY] * (2 * nt) + [SEM] * (2 * nt) + [HBM_ONLY], out_specs=[HBM_ONLY] * (2 * nt),
        out_shape=[pltpu.HBM(a.shape, a.dtype) for a in list(grads) + list(lands)],
        input_output_aliases={t: t for t in range(2 * nt)},
        compiler_params=pltpu.CompilerParams(has_side_effects=DATAFLOW),
    )(*grads, *lands, *ssems, *rsems, pltpu.with_memory_space_constraint(after, pltpu.HBM))
    return list(res[:nt]), list(res[nt:])
DATAFLOW = pltpu.SideEffectType.DATAFLOW_SIDE_EFFECTING


def _gather_now(bufs, *, name):
    nt = len(bufs)

    def body(*refs):
        outs = refs[nt:2 * nt]
        ssem, rsem = refs[2 * nt:]
        x, y, c, q, sib, chips = _place()
        sends = []
        for t in range(nt):
            for j, (px, py) in enumerate(chips):
                mine = outs[t].at[q, c]
                cp = _rcopy(mine, mine, ssem.at[t, j], rsem.at[t, j], (px, py, c))
                cp.start()
                sends.append(cp)
        for t in range(nt):
            for j, (px, py) in enumerate(chips):
                landed = outs[t].at[2 * px + py, c]
                _rcopy(landed, landed, ssem.at[t, j], rsem.at[t, j], (px, py, c)).wait_recv()
                cp = _rcopy(landed, landed, ssem.at[t, 3 + j], rsem.at[t, 3 + j], sib)
                cp.start()
                sends.append(cp)
        for t in range(nt):
            for j, (px, py) in enumerate(chips):
                passed = outs[t].at[2 * px + py, 1 - c]
                _rcopy(passed, passed, ssem.at[t, 3 + j], rsem.at[t, 3 + j], sib).wait_recv()
        for cp in sends:
            cp.wait_send()

    out_shape = [jax.ShapeDtypeStruct(b.shape, b.dtype) for b in bufs]
    return pl.pallas_call(
        body, name=name, in_specs=[HBM] * nt, out_specs=[HBM] * nt, out_shape=out_shape,
        input_output_aliases={t: t for t in range(nt)},
        scratch_shapes=[pltpu.SemaphoreType.DMA((nt, 6)), pltpu.SemaphoreType.DMA((nt, 6))],
    )(*bufs)


def _gather_start(bufs, *, name):
    nt = len(bufs)

    def body(*refs):
        ssems, rsems, outs = refs[nt:2 * nt], refs[2 * nt:3 * nt], refs[3 * nt:4 * nt]
        x, y, c, q, sib, chips = _place()
        for t in range(nt):
            for j, (px, py) in enumerate(chips):
                mine = outs[t].at[q]
                _rcopy(mine, mine, ssems[t].at[j], rsems[t].at[j], (px, py, c)).start()

    sems = [pltpu.SemaphoreType.DMA((3,))] * (2 * nt)
    out_shape = sems + [pltpu.HBM(b.shape, b.dtype) for b in bufs]
    res = pl.pallas_call(
        body, name=name, in_specs=[HBM_ONLY] * nt, out_specs=[SEM] * (2 * nt) + [HBM_ONLY] * nt, out_shape=out_shape,
        input_output_aliases={t: 2 * nt + t for t in range(nt)},
        compiler_params=pltpu.CompilerParams(has_side_effects=DATAFLOW),
    )(*[pltpu.with_memory_space_constraint(b, pltpu.HBM) for b in bufs])
    return res[:nt], res[nt:2 * nt], res[2 * nt:]


def _gather_wait(bufs, ssems, rsems, after, *, name):
    nt = len(bufs)

    def body(*refs):
        ssem_refs, rsem_refs = refs[nt:2 * nt], refs[2 * nt:3 * nt]
        outs = refs[3 * nt + 1:]
        x, y, c, q, sib, chips = _place()
        for t in range(nt):
            for j, (px, py) in enumerate(chips):
                mine = outs[t].at[q]
                _rcopy(mine, mine, ssem_refs[t].at[j], rsem_refs[t].at[j], (px, py, c)).wait_send()
        for t in range(nt):
            for j, (px, py) in enumerate(chips):
                theirs = outs[t].at[2 * px + py]
                _rcopy(theirs, theirs, ssem_refs[t].at[j], rsem_refs[t].at[j], (px, py, c)).wait_recv()

    res = pl.pallas_call(
        body, name=name, in_specs=[HBM_ONLY] * nt + [SEM] * (2 * nt) + [HBM], out_specs=[HBM_ONLY] * nt,
        out_shape=[pltpu.HBM(b.shape, b.dtype) for b in bufs], input_output_aliases={t: t for t in range(nt)},
        compiler_params=pltpu.CompilerParams(has_side_effects=DATAFLOW),
    )(*bufs, *ssems, *rsems, after)
    return list(res)


def _sibling_share(bufs, layout, *, name):
    no = len(bufs)
    nt = len(layout)

    def body(*refs):
        outs = refs[no:2 * no]
        ssem, rsem = refs[2 * no:]
        x, y, c, q, sib, chips = _place()

        def slot(t, half):
            o, lead = layout[t]
            return outs[o].at[half] if lead is None else outs[o].at[lead, half]

        sends = []
        for t in range(nt):
            cp = _rcopy(slot(t, c), slot(t, c), ssem.at[t], rsem.at[t], sib)
            cp.start()
            sends.append(cp)
        for t in range(nt):
            _rcopy(slot(t, 1 - c), slot(t, 1 - c), ssem.at[t], rsem.at[t], sib).wait_recv()
        for cp in sends:
            cp.wait_send()

    out_shape = [jax.ShapeDtypeStruct(b.shape, b.dtype) for b in bufs]
    return pl.pallas_call(
        body, name=name, in_specs=[HBM] * no, out_specs=[HBM] * no, out_shape=out_shape,
        input_output_aliases={o: o for o in range(no)},
        scratch_shapes=[pltpu.SemaphoreType.DMA((nt,)), pltpu.SemaphoreType.DMA((nt,))],
    )(*bufs)


def _small_tail(local, params, *, name):
    (cwa, cba, ga, ba, gb, bb, dws, dsb, dbq, dsk, cwf0, cbf0, cwf1, cbf1,
     g00, g01, g10, g11, b00, b01, b10, b11, err) = local
    n_local = len(local)
    kw, wa = cwa.shape
    ng = dws.shape[0]
    nqkv = dbq.shape[1]
    nsk = dsk.shape[1]
    f = cwf0.shape[2]
    dm = err.shape[1]
    row_vec = 8 * (-(-kw // 8))
    shapes = [(row_vec + 8, wa), (ng * B_CHUNK + 8, B_CHUNK), (8, nqkv), (2, 2, 8, f), (16, dm)]
    n_grp = len(shapes)
    flat_params = [a for triple in params for a in triple]
    n_par = len(params)

    def reduce_body(*refs):
        loc = refs[:n_local]
        tot = refs[n_local:n_local + n_grp]
        scr = refs[n_local + n_grp:]
        grp, from_sib, pair, gath = (scr[k * n_grp:(k + 1) * n_grp] for k in range(4))
        ssem1, rsem1, ssem2, rsem2 = scr[4 * n_grp:]
        x, y, core, q, sib, chips = _place()

        for gr in grp:
            gr[...] = jnp.zeros_like(gr)
        a, b, c, dd, e = grp
        a[0:kw, :] = loc[0][...]
        for k in range(5):
            a[row_vec + k:row_vec + k + 1, :] = loc[1 + k][...]
        for g in range(ng):
            b[g * B_CHUNK:(g + 1) * B_CHUNK, :] = loc[6][g]
        b[ng * B_CHUNK:ng * B_CHUNK + ng, :] = loc[7][...]
        c[0:1, :] = loc[8][...]
        c[1:2, 0:nsk] = loc[9][...]
        for l in range(2):
            for s in range(2):
                dd[l, s, 0:3, :] = loc[10 + 2 * l][s]
                dd[l, s, 3:4, :] = loc[11 + 2 * l][s]
        for k in range(9):
            e[k:k + 1, :] = loc[14 + k][...]

        sends = []
        for gi in range(n_grp):
            cp = _rcopy(grp[gi], from_sib[gi], ssem1.at[gi], rsem1.at[gi], sib)
            cp.start()
            sends.append(cp)
        for gi in range(n_grp):
            _rcopy(grp[gi], from_sib[gi], ssem1.at[gi], rsem1.at[gi], sib).wait_recv()
            both = grp[gi][...] + from_sib[gi][...]
            pair[gi][...] = both
            gath[gi][q] = both
            for j, (px, py) in enumerate(chips):
                cp = _rcopy(pair[gi], gath[gi].at[q], ssem2.at[gi, j], rsem2.at[gi, j], (px, py, core))
                cp.start()
                sends.append(cp)
        for gi in range(n_grp):
            for j, (px, py) in enumerate(chips):
                slot = gath[gi].at[2 * px + py]
                _rcopy(slot, slot, ssem2.at[gi, j], rsem2.at[gi, j], (px, py, core)).wait_recv()
            acc = gath[gi][0]
            for k in range(1, 4):
                acc = acc + gath[gi][k]
            tot[gi][...] = acc
        for cp in sends:
            cp.wait_send()

    vm = pl.BlockSpec(memory_space=pltpu.VMEM)
    scratch = ([pltpu.VMEM(s, F32) for s in shapes] * 3 + [pltpu.VMEM((4,) + s, F32) for s in shapes]
               + [pltpu.SemaphoreType.DMA((n_grp,)), pltpu.SemaphoreType.DMA((n_grp,)),
                  pltpu.SemaphoreType.DMA((n_grp, 3)), pltpu.SemaphoreType.DMA((n_grp, 3))])
    totals = pl.pallas_call(
        reduce_body, name=name + "_reduce", in_specs=[vm] * n_local, out_specs=[vm] * n_grp,
        out_shape=[jax.ShapeDtypeStruct(s, F32) for s in shapes], scratch_shapes=scratch,
        compiler_params=pltpu.CompilerParams(vmem_limit_bytes=VMEM_LIMIT),
    )(*local)

    def adamw_body(*refs):
        ta, tb, tc, td, te = refs[:n_grp]
        par = refs[n_grp:n_grp + 3 * n_par]
        outs = refs[n_grp + 3 * n_par:n_grp + 7 * n_par]
        loss_ref = refs[n_grp + 7 * n_par]
        q = 2 * lax.axis_index("x") + lax.axis_index("y")

        def mine(piece):
            out = piece(0)
            for k in range(1, 4):
                out = jnp.where(q == k, piece(k), out)
            return out

        def update(p, grad, index=None):
            at = (lambda r: r[...]) if index is None else (lambda r: r[index])
            w_ref, m_ref, v_ref = par[3 * p:3 * p + 3]
            g_ref, d_ref, nm_ref, nv_ref = outs[4 * p:4 * p + 4]
            delta, nm, nv = _adamw_math(at(w_ref), grad, at(m_ref), at(v_ref))
            for r, val in ((g_ref, grad), (d_ref, delta), (nm_ref, nm), (nv_ref, nv)):
                if index is None:
                    r[...] = val
                else:
                    r[index] = val

        wq = wa // 4
        update(0, mine(lambda k: ta[0:kw, k * wq:(k + 1) * wq]), (0,))
        for k in range(5):
            update(1 + k, ta[row_vec + k:row_vec + k + 1, :])
        for g in range(ng):
            update(6, tb[g * B_CHUNK:(g + 1) * B_CHUNK, :], (0, g))
        update(7, tb[ng * B_CHUNK:ng * B_CHUNK + ng, :], (0,))
        nq4 = nqkv // 4
        update(8, mine(lambda k: tc[0:1, k * nq4:(k + 1) * nq4]))
        update(9, tc[1:2, 0:nsk])
        fh = f // 2
        for l in range(2):
            update(10, mine(lambda k: td[l, k // 2, 0:3, (k % 2) * fh:(k % 2 + 1) * fh]), (l,))
            update(11, jnp.concatenate([td[l, 0, 3:4, :], td[l, 1, 3:4, :]], axis=1), (slice(l, l + 1),))
        dq4 = dm // 4
        for i in range(2):
            for j in range(2):
                for p, base in ((12, 0), (13, 4)):
                    row = base + 2 * i + j
                    update(p, mine(lambda k: te[row:row + 1, k * dq4:(k + 1) * dq4]), (i, slice(j, j + 1)))
        loss_ref[...] = (0.5 / dm) * jnp.sum(te[8:9, :], axis=1, keepdims=True)

    out_shape = []
    for w, _, _ in params:
        out_shape += [jax.ShapeDtypeStruct(w.shape, F32)] * 4
    out_shape.append(jax.ShapeDtypeStruct((1, 1), F32))
    res = pl.pallas_call(
        adamw_body, name=name + "_adamw", in_specs=[vm] * (n_grp + 3 * n_par), out_specs=[vm] * len(out_shape),
        out_shape=out_shape, compiler_params=pltpu.CompilerParams(vmem_limit_bytes=VMEM_LIMIT),
    )(*totals, *flat_params)
    return [res[4 * p:4 * p + 4] for p in range(n_par)], res[-1]


def _pack(arrays, rows_multiple):
    flat = jnp.concatenate([a.reshape(-1) for a in arrays])
    rows = -(-flat.shape[0] // LANES)
    rows = -(-rows // rows_multiple) * rows_multiple
    flat = jnp.pad(flat, (0, rows * LANES - flat.shape[0]))
    return flat.reshape(rows, LANES)


def _unshard_cols(stacked):
    moved = jnp.moveaxis(stacked, 0, -2)
    return moved.reshape(moved.shape[:-2] + (4 * stacked.shape[-1],))


def kernel(x, ab_w_in, a_conv_w, a_conv_b, a_norm_g, a_norm_b, b_norm_g, b_norm_b, b_spatial_w, b_spatial_b, ab_w_out, c_w_qkv, c_b_qkv, c_sinks, c_w_o, ffn_w_up, ffn_conv_w, ffn_conv_b, ffn_w_down, ln_g, ln_b, loss_target, m_ab_w_in, m_a_conv_w, m_a_conv_b, m_a_norm_g, m_a_norm_b, m_b_norm_g, m_b_norm_b, m_b_spatial_w, m_b_spatial_b, m_ab_w_out, m_c_w_qkv, m_c_b_qkv, m_c_sinks, m_c_w_o, m_ffn_w_up, m_ffn_conv_w, m_ffn_conv_b, m_ffn_w_down, m_ln_g, m_ln_b, v_ab_w_in, v_a_conv_w, v_a_conv_b, v_a_norm_g, v_a_norm_b, v_b_norm_g, v_b_norm_b, v_b_spatial_w, v_b_spatial_b, v_ab_w_out, v_c_w_qkv, v_c_b_qkv, v_c_sinks, v_c_w_o, v_ffn_w_up, v_ffn_conv_w, v_ffn_conv_b, v_ffn_w_down, v_ln_g, v_ln_b):
    rows, d = x.shape[1], x.shape[2]
    depth = ln_g.shape[0]
    assert depth == 2 and x.shape[0] == 1
    alpha = (2.0 * depth) ** 0.25
    f = ffn_w_down.shape[1] * 4
    n_q = c_sinks.shape[1]
    q_idx = 2 * lax.axis_index("x") + lax.axis_index("y")
    c_idx = lax.axis_index("c")
    xs, tgt = x[0], loss_target[0]

    def own_slot(part):
        buf = lax.empty((4,) + part.shape, part.dtype)
        return lax.dynamic_update_slice(buf, part[None], (q_idx, 0, 0, 0))

    def halves(wm):
        return own_slot(wm.astype(BF16).reshape((2, wm.shape[0] // 2) + wm.shape[1:]))

    small_sharded = [a_conv_w[0], c_b_qkv[0], ffn_conv_w, ln_g, ln_b]
    small_pack = _pack(small_sharded, 16)
    bufs = [halves(ab_w_in[0]), own_slot(small_pack.reshape(2, small_pack.shape[0] // 2, LANES)), halves(ab_w_out[0]),
            halves(ffn_w_up[0]), halves(ffn_w_down[0]), halves(c_w_qkv[0]), halves(c_w_o[0]),
            halves(ffn_w_up[1]), halves(ffn_w_down[1])]
    whole = lambda g: g.reshape(4, 2 * g.shape[2], g.shape[3])
    n_now = 3
    w_in, small_all, w_out = [whole(g) for g in _gather_now(bufs[:n_now], name="gather_now")]
    w_out = w_out.reshape(-1, d)
    ssems, rsems, started = _gather_start(bufs[n_now:], name="gather_start")

    def arrive(idx, after, tag):
        idx = [i - n_now for i in idx]
        got = _gather_wait([started[i] for i in idx], [ssems[i] for i in idx], [rsems[i] for i in idx], after,
                           name=f"gather_wait_{tag}")
        return [whole(g) for g in got]

    small_all = small_all.reshape(4, -1)
    sh_shapes = [s.shape for s in small_sharded]
    pieces, pos = [], 0
    for s in sh_shapes:
        n = math.prod(s)
        pieces.append(_unshard_cols(small_all[:, pos:pos + n].reshape((4,) + s)))
        pos += n
    conv_w_a, b_qkv, conv_w_f, ln_gf, ln_bf = pieces

    tril = jnp.tril(jnp.ones((B_CHUNK, B_CHUNK), F32))
    ws = (b_spatial_w[0] * tril).astype(BF16)
    wst = jnp.swapaxes(ws, 1, 2)
    sbb = jnp.broadcast_to(b_spatial_b[0][:, :, None], b_spatial_w[0].shape)
    mix_vecs = [a_conv_b, a_norm_g, a_norm_b, b_norm_g, b_norm_b]
    cw_f = [jnp.swapaxes(conv_w_f[l].reshape(3, 2, f), 0, 1) for l in range(depth)]
    cb_f = [ffn_conv_b[l].reshape(2, 1, f) for l in range(depth)]
    lng = lambda i, j: ln_gf[i, j].reshape(1, d)
    lnb = lambda i, j: ln_bf[i, j].reshape(1, d)
    sinks = c_sinks[0]

    w_up, w_down = [None, None], [None, None]

    def ffn_fwd(xin, l):
        w_up[l], = arrive([3 + 4 * l], xin, f"up{l}")
        hf, fact = _ffn_up_fwd(xin, w_up[l], cw_f[l], cb_f[l], name=f"ffn{l}_up")
        w_down[l] = arrive([4 + 4 * l], fact, f"down{l}")[0].reshape(-1, d)
        out = _matmul(fact, w_down[l], name=f"ffn{l}_down", tm=512, tn=1024, tk=2816)
        return hf, fact, out

    h0, ab, a2 = _mixer_fwd(xs, w_in, conv_w_a, *mix_vecs, ws, sbb, name="mix_fwd")
    mix = _matmul(ab, w_out, name="mix_out", tm=1024, tn=1024, tk=1024)
    x1 = _add_ln_fwd(xs, mix, lng(0, 0), lnb(0, 0), alpha, name="ln00")
    hf0, f0, ffn0 = ffn_fwd(x1, 0)
    x2 = _add_ln_fwd(x1, ffn0, lng(0, 1), lnb(0, 1), alpha, name="ln01")
    w_qkv = _unshard_cols(arrive([5], x2, "qkv")[0])
    qkv = _matmul(x2, w_qkv, name="att_qkv", tm=1024, tn=w_qkv.shape[1], tk=1024, bias=b_qkv.reshape(1, -1))
    ao, lse = _attn_fwd(qkv, sinks, name="att_core")
    w_o = arrive([6], ao, "o")[0].reshape(-1, d)
    att = _matmul(ao, w_o, name="att_out", tm=1024, tn=1024, tk=1024)
    x3 = _add_ln_fwd(x2, att, lng(1, 0), lnb(1, 0), alpha, name="ln10")
    hf1, f1, ffn1 = ffn_fwd(x3, 1)
    sq_err, dy = _add_ln_loss(x3, ffn1, lng(1, 1), lnb(1, 1), tgt, alpha, name="ln11_loss")

    def owner_view(g):
        if g.ndim == 3:
            return g.reshape(4, 2, g.shape[1] // 2, g.shape[2])
        return g.reshape(4, 2, g.shape[0] // 8, g.shape[1])

    in_flight = []

    def send_grads(tag, grads, after):
        lands = [lax.empty((8,) + g.shape[2:], BF16) for g in grads]
        ss, rs, g_thru, l_thru, token = _reduce_start(grads, lands, after, name=f"reduce_start_{tag}")
        in_flight.append((tag, g_thru, l_thru, ss, rs))
        return token[0:1, 0:1]

    def ffn_bwd(dz, xin, hf, fact, l):
        d_wdown = _matmul(fact, dz, name=f"ffn{l}_down_dw", ta=True, tm=1408, tn=1024, tk=2048, out_dtype=BF16)
        dfa = _matmul(dz, w_down[l], name=f"ffn{l}_down_dx", tb=True, tm=1024, tn=1408, tk=1024, out_dtype=BF16)
        dx_parts, d_wup, dcw, dcb = _ffn_up_bwd(hf, dfa, xin, w_up[l], cw_f[l], cb_f[l], name=f"ffn{l}_up_bwd")
        tok = send_grads(f"ffn{l}", [owner_view(d_wup), owner_view(d_wdown)], dcb)
        return [(dx_parts, 1.0), (dz, alpha)], dcw, dcb, tok

    dz, dg11, db11 = _add_ln_bwd([(dy, 1.0)], x3, ffn1, lng(1, 1), alpha, name="ln11_bwd")
    dx3, dcw1, dcb1, tok = ffn_bwd(dz, x3, hf1, f1, 1)
    dz, dg10, db10 = _add_ln_bwd(dx3, x2, att, lng(1, 0) + tok, alpha, name="ln10_bwd")
    d_wo = _matmul(ao, dz, name="att_out_dw", ta=True, tm=1024, tn=1024, tk=1024, out_dtype=BF16)
    dao = _matmul(dz, w_o, name="att_out_dx", tb=True, tm=1024, tn=1024, tk=1024)
    dq, dkc, dkp, d_sinks = _attn_bwd(qkv, dao, lse, sinks, name="att_core_bwd")
    dqkv, d_bqkv = _dqkv_assemble(dq, dkc, dkp, name="att_dqkv")
    d_wqkv = _matmul(x2, dqkv, name="att_qkv_dw", ta=True, tm=1024, tn=dqkv.shape[1], tk=1024, out_dtype=BF16)
    d_wqkv_st = jnp.moveaxis(d_wqkv.reshape(d_wqkv.shape[0], 4, -1), 1, 0)
    tok = send_grads("att", [owner_view(d_wqkv_st), owner_view(d_wo)], d_bqkv)
    dx2 = _matmul(dqkv, w_qkv, name="att_qkv_dx", tb=True, tm=1024, tn=1024, tk=dqkv.shape[1], addend=(dz, alpha))
    dz, dg01, db01 = _add_ln_bwd([(dx2, 1.0)], x1, ffn0, lng(0, 1) + tok, alpha, name="ln01_bwd")
    dx1, dcw0, dcb0, tok = ffn_bwd(dz, x1, hf0, f0, 0)
    dz, dg00, db00 = _add_ln_bwd(dx1, xs, mix, lng(0, 0) + tok, alpha, name="ln00_bwd")
    d_wout = _matmul(ab, dz, name="mix_out_dw", ta=True, tm=1024, tn=1024, tk=1024, out_dtype=BF16)
    dab = _matmul(dz, w_out, name="mix_out_dx", tb=True, tm=1024, tn=1024, tk=1024)
    grad_x, d_win, d_cwa, d_cba, d_ga, d_ba, d_gb, d_bb, d_ws, d_sb = _mixer_bwd(
        h0, a2, dab, xs, w_in, dz, alpha, conv_w_a, *mix_vecs[1:], ws, wst, sbb, tril, name="mix_bwd")

    small_w = [a_conv_w, a_conv_b, a_norm_g, a_norm_b, b_norm_g, b_norm_b, b_spatial_w, b_spatial_b, c_b_qkv,
               c_sinks, ffn_conv_w, ffn_conv_b, ln_g, ln_b]
    small_m = [m_a_conv_w, m_a_conv_b, m_a_norm_g, m_a_norm_b, m_b_norm_g, m_b_norm_b, m_b_spatial_w, m_b_spatial_b,
               m_c_b_qkv, m_c_sinks, m_ffn_conv_w, m_ffn_conv_b, m_ln_g, m_ln_b]
    small_v = [v_a_conv_w, v_a_conv_b, v_a_norm_g, v_a_norm_b, v_b_norm_g, v_b_norm_b, v_b_spatial_w, v_b_spatial_b,
               v_c_b_qkv, v_c_sinks, v_ffn_conv_w, v_ffn_conv_b, v_ln_g, v_ln_b]
    local = [d_cwa, d_cba, d_ga, d_ba, d_gb, d_bb, d_ws, d_sb, d_bqkv, d_sinks, dcw0, dcb0, dcw1, dcb1,
             dg00, dg01, dg10, dg11, db00, db01, db10, db11, sq_err]
    small_out, loss = _small_tail(local, list(zip(small_w, small_m, small_v)), name="small_tail")
    loss = loss[0, 0]
    small_g = [o[0] for o in small_out]
    sm_delta = [o[1] for o in small_out]
    sm_m = [o[2] for o in small_out]
    sm_v = [o[3] for o in small_out]

    place = jnp.stack([q_idx, c_idx, 4 * lax.axis_index("x") + 2 * lax.axis_index("y") + c_idx]).astype(jnp.int32)
    where = {"mix": [(0, None), (1, None)], "att": [(2, None), (3, None)], "ffn0": [(4, 0), (5, 0)], "ffn1": [(4, 1), (5, 1)]}
    big_w = [ab_w_in, ab_w_out, c_w_qkv, c_w_o, ffn_w_up, ffn_w_down]
    big_m = [m_ab_w_in, m_ab_w_out, m_c_w_qkv, m_c_w_o, m_ffn_w_up, m_ffn_w_down]
    big_v = [v_ab_w_in, v_ab_w_out, v_c_w_qkv, v_c_w_o, v_ffn_w_up, v_ffn_w_down]
    big_out = [None] * 6

    def finish(tags, after, label):
        bufs, layout = {}, []
        for tag, g_thru, l_thru, ss, rs in in_flight:
            if tag not in tags:
                continue
            own, landed = _reduce_wait(g_thru, l_thru, ss, rs, after, name=f"reduce_wait_{tag}")
            for k, (o, lead) in enumerate(where[tag]):
                piece = own[k].shape[2:]
                shape = (2,) + piece if lead is None else (2, 2) + piece
                bufs[o] = _octo_sum(own[k], landed[k], place, bufs.get(o), (lead, shape), name=f"reduce_sum_{tag}{k}")
                layout.append((o, lead))
        order = sorted(bufs)
        shared = _sibling_share([bufs[o] for o in order], [(order.index(o), lead) for o, lead in layout],
                                name=f"reduce_share_{label}")
        for o, g in zip(order, shared):
            w = big_w[o]
            two_d = lambda a: a.reshape(-1, a.shape[-1])
            outs = _adamw(two_d(w), two_d(g), two_d(big_m[o]), two_d(big_v[o]), name=f"adamw_big{o}")
            big_out[o] = [r.reshape(w.shape) for r in outs]
        return big_out[order[-1]][0]

    tok = send_grads("mix", [owner_view(d_win), owner_view(d_wout)], after=sm_delta[0])
    done = finish(("ffn1", "att", "ffn0"), sm_delta[6] + tok, "early")
    finish(("mix",), done, "mix")

    order_big = {0: 0, 9: 1, 10: 2, 13: 3, 14: 4, 17: 5}
    order_small = {1: 0, 2: 1, 3: 2, 4: 3, 5: 4, 6: 5, 7: 6, 8: 7, 11: 8, 12: 9, 15: 10, 16: 11, 18: 12, 19: 13}
    grads, deltas, new_m, new_v = [], [], [], []
    for pos_w in range(20):
        if pos_w in order_big:
            t = order_big[pos_w]
            grads.append(big_out[t][3])
            deltas.append(big_out[t][0])
            new_m.append(big_out[t][1])
            new_v.append(big_out[t][2])
        else:
            t = order_small[pos_w]
            grads.append(small_g[t])
            deltas.append(sm_delta[t])
            new_m.append(sm_m[t])
            new_v.append(sm_v[t])
    return (loss, grad_x[None], *grads, *deltas, *new_m, *new_v)
```

```python
import math

import jax
import jax.numpy as jnp
from jax import lax
from jax.experimental import pallas as pl
from jax.experimental.pallas import tpu as pltpu

F32 = jnp.float32
BF16 = jnp.bfloat16
MESH = pl.DeviceIdType.MESH

LN_EPS = 1e-5
HEAD_DIM = 64
ATT_BLOCK = 128
Q_PER_KV = 8
A_KERNEL = 31
CONV_HALO = 32
FFN_HALO = 8
BF16_ROWS = 16
B_CHUNK = 128
LANES = 128
MXU_WIDTH = 256
GELU_C = math.sqrt(2.0 / math.pi)
ADAM_LR = 0.001
ADAM_B1 = 0.9
ADAM_B2 = 0.999
ADAM_EPS = 1e-08
ADAM_WD = 0.01
ADAM_STEP = 10
VMEM_LIMIT = 56 * 1024 * 1024


def _cp(*dims):
    return pltpu.CompilerParams(dimension_semantics=dims, vmem_limit_bytes=VMEM_LIMIT)


def _pick(n, prefs):
    for p in prefs:
        if n % p == 0:
            return p
    return n


def _sig(x):
    return 1.0 / (1.0 + jnp.exp(-x))


def _gelu(x):
    t = jnp.tanh(GELU_C * (x + 0.044715 * (x * x * x)))
    return x * (0.5 * (1.0 + t)), t


def _gelu_grad(x, t):
    return 0.5 * (1.0 + t) + 0.5 * x * (1.0 - t * t) * (GELU_C * (1.0 + 3.0 * 0.044715 * x * x))


def _ln_stats(z):
    mu = jnp.mean(z, axis=-1, keepdims=True)
    zc = z - mu
    var = jnp.mean(zc * zc, axis=-1, keepdims=True)
    rstd = lax.rsqrt(var + LN_EPS)
    return zc * rstd, rstd


def _ln_bwd(dxh, xh, rstd):
    return rstd * (dxh - jnp.mean(dxh, axis=-1, keepdims=True) - xh * jnp.mean(dxh * xh, axis=-1, keepdims=True))


def _rowsum(a):
    return jnp.sum(a, axis=0, keepdims=True)


def _lshape(a):
    return (a.shape[0], a.shape[1]) if a.ndim == 2 else (a.shape[1], a.shape[0] * a.shape[2])


def _spec2(arr, blk_r, blk_c, ridx, cidx):
    if len(arr.shape) == 2:
        return pl.BlockSpec((blk_r, blk_c), lambda i, j, k: (ridx(i, j, k), cidx(i, j, k)))
    per = arr.shape[2] // blk_c
    assert arr.shape[2] % blk_c == 0
    return pl.BlockSpec((None, blk_r, blk_c), lambda i, j, k: (cidx(i, j, k) // per, ridx(i, j, k), cidx(i, j, k) % per))


def _matmul(a, b, *, name, ta=False, tb=False, tm, tn, tk, out_dtype=F32, out_stack=None, bias=None, addend=None):
    ar, ac = _lshape(a)
    br, bc = _lshape(b)
    m, kdim = (ac, ar) if ta else (ar, ac)
    n = br if tb else bc
    assert (bc if tb else br) == kdim
    tm, tn, tk = min(tm, m), min(tn, n), min(tk, kdim)
    assert m % tm == 0 and n % tn == 0 and kdim % tk == 0, (name, m, n, kdim, tm, tn, tk)
    nk = kdim // tk
    gi, gj, gk = (lambda i, j, k: i), (lambda i, j, k: j), (lambda i, j, k: k)
    a_spec = _spec2(a, tk, tm, gk, gi) if ta else _spec2(a, tm, tk, gi, gk)
    b_spec = _spec2(b, tn, tk, gj, gk) if tb else _spec2(b, tk, tn, gk, gj)
    if out_stack is None:
        out_sds = jax.ShapeDtypeStruct((m, n), out_dtype)
    else:
        out_sds = jax.ShapeDtypeStruct((out_stack, m, n // out_stack), out_dtype)
    o_spec = _spec2(out_sds, tm, tn, gi, gj)
    in_specs = [a_spec, b_spec]
    args = [a, b]
    if bias is not None:
        in_specs.append(pl.BlockSpec((1, tn), lambda i, j, k: (0, j)))
        args.append(bias)
    scale = None
    if addend is not None:
        add_arr, scale = addend
        in_specs.append(pl.BlockSpec((tm, tn), lambda i, j, k: (i, j)))
        args.append(add_arr)
    use_acc = nk > 1 and out_dtype != F32
    dn = (((0 if ta else 1,), (1 if tb else 0,)), ((), ()))

    def body(*refs):
        a_ref, b_ref = refs[0], refs[1]
        pos = 2
        bias_ref = add_ref = None
        if bias is not None:
            bias_ref = refs[pos]
            pos += 1
        if addend is not None:
            add_ref = refs[pos]
            pos += 1
        o_ref = refs[pos]
        acc_ref = refs[pos + 1] if use_acc else o_ref
        p = lax.dot_general(a_ref[...].astype(BF16), b_ref[...].astype(BF16), dn, preferred_element_type=F32)

        def finish(val):
            if bias_ref is not None:
                val = val + bias_ref[...]
            if add_ref is not None:
                val = val + scale * add_ref[...]
            return val.astype(out_dtype)

        if nk == 1:
            o_ref[...] = finish(p)
        else:
            k = pl.program_id(2)

            @pl.when(k == 0)
            def _():
                acc_ref[...] = p

            @pl.when(k > 0)
            def _():
                acc_ref[...] += p

            if use_acc or bias_ref is not None or add_ref is not None:
                @pl.when(k == nk - 1)
                def _():
                    o_ref[...] = finish(acc_ref[...])

    return pl.pallas_call(
        body, name=name, grid=(m // tm, n // tn, nk), in_specs=in_specs, out_specs=o_spec, out_shape=out_sds,
        scratch_shapes=[pltpu.VMEM((tm, tn), F32)] if use_acc else [],
        compiler_params=_cp("parallel", "parallel", "arbitrary"),
    )(*args)


def _add_ln_fwd(x, s, g, b, alpha, *, name):
    rows, d = x.shape
    t = _pick(rows, (512, 256))

    def body(x_ref, s_ref, g_ref, b_ref, y_ref):
        xh, _ = _ln_stats(alpha * x_ref[...] + s_ref[...])
        y_ref[...] = xh * g_ref[...] + b_ref[...]

    row = pl.BlockSpec((t, d), lambda i: (i, 0))
    vec = pl.BlockSpec((1, d), lambda i: (0, 0))
    return pl.pallas_call(body, name=name, grid=(rows // t,), in_specs=[row, row, vec, vec], out_specs=row,
                          out_shape=jax.ShapeDtypeStruct((rows, d), F32), compiler_params=_cp("parallel"))(x, s, g, b)


def _add_ln_bwd(dy_terms, x, s, g, alpha, *, name):
    rows, d = x.shape
    t = _pick(rows, (512, 256))
    nterm = len(dy_terms)
    scales = [sc for _, sc in dy_terms]
    ranks = [a.ndim for a, _ in dy_terms]

    def body(*refs):
        dy_refs = refs[:nterm]
        x_ref, s_ref, g_ref, dz_ref, dg_ref, db_ref = refs[nterm:]

        @pl.when(pl.program_id(0) == 0)
        def _():
            dg_ref[...] = jnp.zeros_like(dg_ref)
            db_ref[...] = jnp.zeros_like(db_ref)

        dyv = None
        for r, sc, rank in zip(dy_refs, scales, ranks):
            slabs = [r[...]] if rank == 2 else [r[p] for p in range(r.shape[0])]
            for v in slabs:
                v = v if sc == 1.0 else sc * v
                dyv = v if dyv is None else dyv + v
        xh, rstd = _ln_stats(alpha * x_ref[...] + s_ref[...])
        dz_ref[...] = _ln_bwd(dyv * g_ref[...], xh, rstd)
        dg_ref[...] += _rowsum(dyv * xh)
        db_ref[...] += _rowsum(dyv)

    row = pl.BlockSpec((t, d), lambda i: (i, 0))
    vec = pl.BlockSpec((1, d), lambda i: (0, 0))
    vsds = jax.ShapeDtypeStruct((1, d), F32)
    dy_specs = [row if a.ndim == 2 else pl.BlockSpec((a.shape[0], t, d), lambda i: (0, i, 0)) for a, _ in dy_terms]
    return pl.pallas_call(body, name=name, grid=(rows // t,), in_specs=dy_specs + [row, row, vec], out_specs=[row, vec, vec],
                          out_shape=[jax.ShapeDtypeStruct((rows, d), F32), vsds, vsds],
                          compiler_params=_cp("arbitrary"))(*[a for a, _ in dy_terms], x, s, g)


def _add_ln_loss(x, s, g, b, tgt, alpha, *, name):
    rows, d = x.shape
    t = _pick(rows, (512, 256))

    def body(x_ref, s_ref, g_ref, b_ref, t_ref, l_ref, dy_ref):
        @pl.when(pl.program_id(0) == 0)
        def _():
            l_ref[...] = jnp.zeros_like(l_ref)

        xh, _ = _ln_stats(alpha * x_ref[...] + s_ref[...])
        e = (xh * g_ref[...] + b_ref[...]) - t_ref[...]
        l_ref[...] += _rowsum(e * e)
        dy_ref[...] = e * (1.0 / d)

    row = pl.BlockSpec((t, d), lambda i: (i, 0))
    vec = pl.BlockSpec((1, d), lambda i: (0, 0))
    return pl.pallas_call(body, name=name, grid=(rows // t,), in_specs=[row, row, vec, vec, row], out_specs=[vec, row],
                          out_shape=[jax.ShapeDtypeStruct((1, d), F32), jax.ShapeDtypeStruct((rows, d), F32)],
                          compiler_params=_cp("arbitrary"))(x, s, g, b, tgt)


def _col_blocks(width, step):
    return [slice(pos, min(pos + step, width)) for pos in range(0, width, step)]


def _conv3(e, w, b):
    r1 = pltpu.roll(e, 1, 0)
    r2 = pltpu.roll(e, 2, 0)
    return w[0:1, :] * r2 + w[1:2, :] * r1 + w[2:3, :] * e + b, (r2, r1, e)


def _ffn_up_fwd(x, w_up, cw, cb, *, name):
    rows, d = x.shape
    nq, _, tc = w_up.shape
    nj = nq // 2
    f = tc * nj
    tm = _pick(rows, (512, 256))
    blocks = _col_blocks(tc, tc)

    def body(x_ref, wg_ref, wv_ref, cw_ref, cb_ref, hf_ref, f_ref, prev_ref):
        @pl.when(pl.program_id(1) == 0)
        def _():
            prev_ref[...] = jnp.zeros_like(prev_ref)

        xb = x_ref[...].astype(BF16)
        for cs in blocks:
            hc = []
            for s, w_ref in ((0, wg_ref), (1, wv_ref)):
                h = jnp.dot(xb, w_ref[:, cs], preferred_element_type=F32)
                hf_ref[s, :, cs] = h
                e = jnp.concatenate([prev_ref[s, :, cs], h], axis=0)
                prev_ref[s, :, cs] = h[tm - FFN_HALO:]
                y, _ = _conv3(e, cw_ref[s, :, cs], cb_ref[s, :, cs])
                hc.append(y[FFN_HALO:])
            gl, _ = _gelu(hc[0])
            f_ref[:, cs] = (gl * hc[1]).astype(BF16)

    in_specs = [
        pl.BlockSpec((tm, d), lambda j, i: (i, 0)),
        pl.BlockSpec((None, d, tc), lambda j, i: (j, 0, 0)),
        pl.BlockSpec((None, d, tc), lambda j, i: (nj + j, 0, 0)),
        pl.BlockSpec((2, 3, tc), lambda j, i: (0, 0, j)),
        pl.BlockSpec((2, 1, tc), lambda j, i: (0, 0, j)),
    ]
    out_specs = [pl.BlockSpec((2, tm, tc), lambda j, i: (0, i, j)), pl.BlockSpec((tm, tc), lambda j, i: (i, j))]
    out_shape = [jax.ShapeDtypeStruct((2, rows, f), F32), jax.ShapeDtypeStruct((rows, f), BF16)]
    return pl.pallas_call(body, name=name, grid=(nj, rows // tm), in_specs=in_specs, out_specs=out_specs, out_shape=out_shape,
                          scratch_shapes=[pltpu.VMEM((2, FFN_HALO, tc), F32)],
                          compiler_params=_cp("parallel", "arbitrary"))(x, w_up, w_up, cw, cb)


def _ffn_up_bwd(hf, df, x, w_up, cw, cb, *, name):
    _, rows, f = hf.shape
    d = x.shape[1]
    nq, _, tc = w_up.shape
    nj = nq // 2
    tm = _pick(rows, (512, 256))
    hb = tm // FFN_HALO
    once = pl.Buffered(1)
    ni = rows // tm
    last_blk = rows // FFN_HALO - 1
    ext = tm + 2 * FFN_HALO
    tile = slice(FFN_HALO, FFN_HALO + tm)
    blocks = _col_blocks(tc, MXU_WIDTH)

    def body(h_ref, hp_ref, hn_ref, d_ref, dn_ref, x_ref, wg_ref, wv_ref, cw_ref, cb_ref, dx_ref, dw_out_ref, dcw_ref, dcb_ref,
             dw_ref):
        i = pl.program_id(1)
        first = i == 0
        last = i == ni - 1

        @pl.when(first)
        def _():
            dw_ref[...] = jnp.zeros_like(dw_ref)
            dcw_ref[...] = jnp.zeros_like(dcw_ref)
            dcb_ref[...] = jnp.zeros_like(dcb_ref)

        xt = x_ref[...].astype(BF16).T
        dx = None
        for cs in blocks:
            wc = cs.stop - cs.start
            d_next = dn_ref[:, cs].astype(F32)[0:FFN_HALO]
            de = jnp.concatenate([jnp.zeros((FFN_HALO, wc), F32), d_ref[:, cs].astype(F32), jnp.where(last, 0.0, d_next)], axis=0)
            taps, hc = [], []
            for s in range(2):
                e = jnp.concatenate([jnp.where(first, 0.0, hp_ref[s, :, cs]), h_ref[s, :, cs], hn_ref[s, :, cs]], axis=0)
                y, tp = _conv3(e, cw_ref[s, :, cs], cb_ref[s, :, cs])
                hc.append(y)
                taps.append(tp)
            gl, th = _gelu(hc[0])
            dhc = (de * hc[1] * _gelu_grad(hc[0], th), de * gl)
            for s, w_ref in ((0, wg_ref), (1, wv_ref)):
                w = cw_ref[s, :, cs]
                g = dhc[s]
                dh = (w[2:3, :] * g + w[1:2, :] * pltpu.roll(g, ext - 1, 0) + w[0:1, :] * pltpu.roll(g, ext - 2, 0))[tile]
                gt = g[tile]
                for k in range(3):
                    dcw_ref[s, k:k + 1, cs] += _rowsum(gt * taps[s][k][tile])
                dcb_ref[s, :, cs] += _rowsum(gt)
                dhb = dh.astype(BF16)
                part = lax.dot_general(dhb, w_ref[:, cs], (((1,), (1,)), ((), ())), preferred_element_type=F32)
                dx = part if dx is None else dx + part
                dw_ref[s, :, cs] += jnp.dot(xt, dhb, preferred_element_type=F32)
        dx_ref[...] = dx

        @pl.when(last)
        def _():
            dw_out_ref[...] = dw_ref[...].astype(BF16)

    in_specs = [
        pl.BlockSpec((2, tm, tc), lambda j, i: (0, i, j)),
        pl.BlockSpec((2, FFN_HALO, tc), lambda j, i: (0, jnp.maximum(i * hb - 1, 0), j)),
        pl.BlockSpec((2, FFN_HALO, tc), lambda j, i: (0, jnp.minimum((i + 1) * hb, last_blk), j)),
        pl.BlockSpec((tm, tc), lambda j, i: (i, j)),
        pl.BlockSpec((BF16_ROWS, tc), lambda j, i: (jnp.minimum((i + 1) * (tm // BF16_ROWS), rows // BF16_ROWS - 1), j)),
        pl.BlockSpec((tm, d), lambda j, i: (i, 0)),
        pl.BlockSpec((None, d, tc), lambda j, i: (j, 0, 0), pipeline_mode=once),
        pl.BlockSpec((None, d, tc), lambda j, i: (nj + j, 0, 0), pipeline_mode=once),
        pl.BlockSpec((2, 3, tc), lambda j, i: (0, 0, j)),
        pl.BlockSpec((2, 1, tc), lambda j, i: (0, 0, j)),
    ]
    out_specs = [
        pl.BlockSpec((None, tm, d), lambda j, i: (j, i, 0)),
        pl.BlockSpec((2, None, d, tc), lambda j, i: (0, j, 0, 0), pipeline_mode=once),
        pl.BlockSpec((2, 3, tc), lambda j, i: (0, 0, j)),
        pl.BlockSpec((2, 1, tc), lambda j, i: (0, 0, j)),
    ]
    out_shape = [jax.ShapeDtypeStruct((nj, rows, d), F32), jax.ShapeDtypeStruct((2, nj, d, tc), BF16),
                 jax.ShapeDtypeStruct((2, 3, f), F32), jax.ShapeDtypeStruct((2, 1, f), F32)]
    dx, dw, dcw, dcb = pl.pallas_call(body, name=name, grid=(nj, ni), in_specs=in_specs, out_specs=out_specs,
                                      out_shape=out_shape, scratch_shapes=[pltpu.VMEM((2, d, tc), F32)],
                                      compiler_params=_cp("parallel", "arbitrary"))(
        hf, hf, hf, df, df, x, w_up, w_up, cw, cb)
    return dx, dw.reshape(nq, d, tc), dcw, dcb


def _mixer_fwd(x, w_in, cw, cb, ga, ba, gb, bb, ws, sbb, *, name):
    rows, d = x.shape
    _, _, w = w_in.shape
    t = _pick(rows, (256,))
    groups = w // B_CHUNK

    def body(x_ref, win_ref, cw_ref, cb_ref, ga_ref, ba_ref, gb_ref, bb_ref, ws_ref, sb_ref, h_ref, o_ref, a2_ref, prev_ref):
        @pl.when(pl.program_id(0) == 0)
        def _():
            prev_ref[...] = jnp.zeros_like(prev_ref)

        xb = x_ref[...].astype(BF16)
        for s in range(4):
            h_ref[s] = jnp.dot(xb, win_ref[s], preferred_element_type=F32)
        a1 = h_ref[0] * _sig(h_ref[1])
        e = jnp.concatenate([prev_ref[...], a1], axis=0)
        prev_ref[...] = a1[t - CONV_HALO:]
        acc = cw_ref[A_KERNEL - 1:A_KERNEL, :] * e
        for k in range(A_KERNEL - 1):
            acc = acc + cw_ref[k:k + 1, :] * pltpu.roll(e, A_KERNEL - 1 - k, 0)
        a2 = acc[CONV_HALO:] + cb_ref[...]
        a2_ref[...] = a2
        xh, _ = _ln_stats(a2)
        a3 = xh * ga_ref[...] + ba_ref[...]
        o_ref[:, 0:w] = (a3 * _sig(a3)).astype(BF16)

        u, _ = _gelu(h_ref[2])
        v1, _ = _gelu(h_ref[3])
        xh2, _ = _ln_stats(v1)
        v2 = (xh2 * gb_ref[...] + bb_ref[...]).astype(BF16)
        for c in range(t // B_CHUNK):
            rs = slice(c * B_CHUNK, (c + 1) * B_CHUNK)
            for g in range(groups):
                cs = slice(g * B_CHUNK, (g + 1) * B_CHUNK)
                mixed = jnp.dot(ws_ref[g], v2[rs, cs], preferred_element_type=F32) + sb_ref[g]
                o_ref[rs, w + g * B_CHUNK:w + (g + 1) * B_CHUNK] = (u[rs, cs] * mixed).astype(BF16)

    vec = pl.BlockSpec((1, w), lambda i: (0, 0))
    grp = pl.BlockSpec((groups, B_CHUNK, B_CHUNK), lambda i: (0, 0, 0))
    in_specs = [
        pl.BlockSpec((t, d), lambda i: (i, 0)),
        pl.BlockSpec((4, d, w), lambda i: (0, 0, 0)),
        pl.BlockSpec((A_KERNEL, w), lambda i: (0, 0)),
        vec, vec, vec, vec, vec, grp, grp,
    ]
    out_specs = [pl.BlockSpec((4, t, w), lambda i: (0, i, 0)), pl.BlockSpec((t, 2 * w), lambda i: (i, 0)),
                 pl.BlockSpec((t, w), lambda i: (i, 0))]
    out_shape = [jax.ShapeDtypeStruct((4, rows, w), F32), jax.ShapeDtypeStruct((rows, 2 * w), BF16),
                 jax.ShapeDtypeStruct((rows, w), F32)]
    return pl.pallas_call(body, name=name, grid=(rows // t,), in_specs=in_specs, out_specs=out_specs, out_shape=out_shape,
                          scratch_shapes=[pltpu.VMEM((CONV_HALO, w), F32)],
                          compiler_params=_cp("arbitrary"))(x, w_in, cw, cb, ga, ba, gb, bb, ws, sbb)


def _mixer_bwd(h0, a2, dab, x, w_in, res, res_scale, cw, ga, ba, gb, bb, ws, wst, sbb, tril, *, name):
    _, rows, w = h0.shape
    d = x.shape[1]
    once = pl.Buffered(1)
    t = _pick(rows, (256,))
    hb = t // CONV_HALO
    ni = rows // t
    last_blk = rows // CONV_HALO - 1
    ext = t + CONV_HALO
    tile = slice(0, t)
    groups = w // B_CHUNK
    taps = A_KERNEL - 1

    def body(h_ref, a2_ref, a2n_ref, d_ref, dn_ref, x_ref, win_ref, res_ref, cw_ref, ga_ref, ba_ref, gb_ref, bb_ref,
             ws_ref, wst_ref, sb_ref, tril_ref, dx_ref, dwin_ref, dcw_ref, dcb_ref, dga_ref, dba_ref, dgb_ref, dbb_ref,
             dws_ref, dsb_ref, dw_ref):
        i = pl.program_id(0)
        first = i == 0
        last = i == ni - 1

        @pl.when(first)
        def _():
            for r in (dw_ref, dcw_ref, dcb_ref, dga_ref, dba_ref, dgb_ref, dbb_ref, dws_ref, dsb_ref):
                r[...] = jnp.zeros_like(r)

        xt = x_ref[...].astype(BF16).T
        dx_terms = []

        def through_w_in(slot, dh):
            dhb = dh.astype(BF16)
            dx_terms.append(lax.dot_general(dhb, win_ref[slot], (((1,), (1,)), ((), ())), preferred_element_type=F32))
            dw_ref[slot] += jnp.dot(xt, dhb, preferred_element_type=F32)

        xh, rstd = _ln_stats(jnp.concatenate([a2_ref[...], a2n_ref[...]], axis=0))
        a3 = xh * ga_ref[...] + ba_ref[...]
        s3 = _sig(a3)
        da_e = jnp.concatenate([d_ref[:, 0:w], jnp.where(last, 0.0, dn_ref[...])], axis=0)
        da3 = da_e * (s3 * (1.0 + a3 * (1.0 - s3)))
        da2 = _ln_bwd(da3 * ga_ref[...], xh, rstd)
        dga_ref[...] += _rowsum(da3[tile] * xh[tile])
        dba_ref[...] += _rowsum(da3[tile])
        dcb_ref[...] += _rowsum(da2[tile])
        sgt = _sig(h_ref[1])
        a1t = h_ref[0] * sgt
        da1t = None
        for k in range(A_KERNEL):
            sh = taps - k
            fed = (da2 if sh == 0 else pltpu.roll(da2, ext - sh, 0))[tile]
            dcw_ref[k:k + 1, :] += _rowsum(a1t * fed)
            term = cw_ref[k:k + 1, :] * fed
            da1t = term if da1t is None else da1t + term
        through_w_in(0, da1t * sgt)
        through_w_in(1, da1t * h_ref[0] * sgt * (1.0 - sgt))

        bu = h_ref[2]
        bv = h_ref[3]
        u, tu = _gelu(bu)
        v1, tv = _gelu(bv)
        xh2, rstd2 = _ln_stats(v1)
        v2 = (xh2 * gb_ref[...] + bb_ref[...]).astype(BF16)
        db = d_ref[:, w:2 * w]
        dmx_all = db * u
        du_parts, dv2_parts = [], []
        for c in range(t // B_CHUNK):
            rs = slice(c * B_CHUNK, (c + 1) * B_CHUNK)
            du_row, dv2_row = [], []
            for g in range(groups):
                cs = slice(g * B_CHUNK, (g + 1) * B_CHUNK)
                v2cg = v2[rs, cs]
                mixed = jnp.dot(ws_ref[g], v2cg, preferred_element_type=F32) + sb_ref[g]
                dmx = dmx_all[rs, cs]
                dmxb = dmx.astype(BF16)
                du_row.append(db[rs, cs] * mixed)
                dv2_row.append(jnp.dot(wst_ref[g], dmxb, preferred_element_type=F32))
                dws_ref[g] += tril_ref[...] * lax.dot_general(dmxb, v2cg, (((1,), (1,)), ((), ())),
                                                               preferred_element_type=F32)
                dsb_ref[g:g + 1, :] += _rowsum(dmx.T)
            du_parts.append(jnp.concatenate(du_row, axis=1))
            dv2_parts.append(jnp.concatenate(dv2_row, axis=1))
        du = jnp.concatenate(du_parts, axis=0)
        dv2 = jnp.concatenate(dv2_parts, axis=0)
        dgb_ref[...] += _rowsum(dv2 * xh2)
        dbb_ref[...] += _rowsum(dv2)
        dv1 = _ln_bwd(dv2 * gb_ref[...], xh2, rstd2)
        through_w_in(2, du * _gelu_grad(bu, tu))
        through_w_in(3, dv1 * _gelu_grad(bv, tv))
        dx_ref[...] = res_scale * res_ref[...] + ((dx_terms[0] + dx_terms[1]) + (dx_terms[2] + dx_terms[3]))

        @pl.when(last)
        def _():
            dwin_ref[...] = dw_ref[...].astype(BF16)

    vec = pl.BlockSpec((1, w), lambda i: (0, 0))
    grp = pl.BlockSpec((groups, B_CHUNK, B_CHUNK), lambda i: (0, 0, 0))
    halo = pl.BlockSpec((CONV_HALO, w), lambda i: (jnp.minimum((i + 1) * hb, last_blk), 0))
    wide = pl.BlockSpec((t, d), lambda i: (i, 0))
    in_specs = [
        pl.BlockSpec((4, t, w), lambda i: (0, i, 0)),
        pl.BlockSpec((t, w), lambda i: (i, 0)),
        halo,
        pl.BlockSpec((t, 2 * w), lambda i: (i, 0)),
        halo,
        wide,
        pl.BlockSpec((4, d, w), lambda i: (0, 0, 0), pipeline_mode=once),
        wide,
        pl.BlockSpec((A_KERNEL, w), lambda i: (0, 0)),
        vec, vec, vec, vec, grp, grp, grp,
        pl.BlockSpec((B_CHUNK, B_CHUNK), lambda i: (0, 0)),
    ]
    vsds = jax.ShapeDtypeStruct((1, w), F32)
    out_specs = [
        wide,
        pl.BlockSpec((4, d, w), lambda i: (0, 0, 0), pipeline_mode=once),
        pl.BlockSpec((A_KERNEL, w), lambda i: (0, 0)),
        vec, vec, vec, vec, vec, grp,
        pl.BlockSpec((groups, B_CHUNK), lambda i: (0, 0)),
    ]
    out_shape = [jax.ShapeDtypeStruct((rows, d), F32), jax.ShapeDtypeStruct((4, d, w), BF16),
                 jax.ShapeDtypeStruct((A_KERNEL, w), F32),
                 vsds, vsds, vsds, vsds, vsds, jax.ShapeDtypeStruct((groups, B_CHUNK, B_CHUNK), F32),
                 jax.ShapeDtypeStruct((groups, B_CHUNK), F32)]
    return pl.pallas_call(body, name=name, grid=(ni,), in_specs=in_specs, out_specs=out_specs, out_shape=out_shape,
                          scratch_shapes=[pltpu.VMEM((4, d, w), F32)], compiler_params=_cp("arbitrary"))(
        h0, a2, a2, dab, dab, x, w_in, res, cw, ga, ba, gb, bb, ws, wst, sbb, tril)


GROUP_ROWS = Q_PER_KV * ATT_BLOCK


def _attn_mask(n):
    qi = lax.broadcasted_iota(jnp.int32, (GROUP_ROWS, 2 * ATT_BLOCK), 0) & (ATT_BLOCK - 1)
    sj = lax.broadcasted_iota(jnp.int32, (GROUP_ROWS, 2 * ATT_BLOCK), 1)
    diff = qi + ATT_BLOCK - sj
    return (diff >= 0) & (diff < ATT_BLOCK) & ((n > 0) | (sj >= ATT_BLOCK))


def _stack_heads(ref, kvh, dtype):
    heads = [ref[:, (kvh * Q_PER_KV + g) * HEAD_DIM:(kvh * Q_PER_KV + g + 1) * HEAD_DIM] for g in range(Q_PER_KV)]
    return jnp.concatenate(heads, axis=0).astype(dtype)


def _per_row_sink(sink_ref, kvh):
    head = lax.broadcasted_iota(jnp.int32, (GROUP_ROWS, 1), 0) // ATT_BLOCK
    out = jnp.zeros((GROUP_ROWS, 1), F32)
    for g in range(Q_PER_KV):
        out = jnp.where(head == g, sink_ref[kvh * Q_PER_KV + g], out)
    return out


def _attn_specs(rows, n_q):
    dq = n_q * HEAD_DIM
    dkv = 2 * (n_q // Q_PER_KV) * HEAD_DIM
    kv_blk = dq // dkv
    assert dq % dkv == 0
    return dq, dkv, [
        pl.BlockSpec(memory_space=pltpu.SMEM),
        pl.BlockSpec((ATT_BLOCK, dq), lambda n: (n, 0)),
        pl.BlockSpec((ATT_BLOCK, dkv), lambda n: (n, kv_blk)),
        pl.BlockSpec((ATT_BLOCK, dkv), lambda n: (jnp.maximum(n - 1, 0), kv_blk)),
    ]


def _kv_pair(kvc_ref, kvp_ref, kvh, n_kv):
    ks = slice(kvh * HEAD_DIM, (kvh + 1) * HEAD_DIM)
    vs = slice((n_kv + kvh) * HEAD_DIM, (n_kv + kvh + 1) * HEAD_DIM)
    kk = jnp.concatenate([kvp_ref[:, ks], kvc_ref[:, ks]], axis=0).astype(BF16)
    vv = jnp.concatenate([kvp_ref[:, vs], kvc_ref[:, vs]], axis=0).astype(BF16)
    return kk, vv


def _attn_fwd(qkv, sinks, *, name):
    rows = qkv.shape[0]
    n_q = sinks.shape[0]
    n_kv = n_q // Q_PER_KV
    scale = 1.0 / math.sqrt(HEAD_DIM)
    dq, _, in_specs = _attn_specs(rows, n_q)

    def body(sink_ref, q_ref, kvc_ref, kvp_ref, o_ref, lse_ref):
        valid = _attn_mask(pl.program_id(0))
        for kvh in range(n_kv):
            kk, vv = _kv_pair(kvc_ref, kvp_ref, kvh, n_kv)
            qs = _stack_heads(q_ref, kvh, BF16)
            s = lax.dot_general(qs, kk, (((1,), (1,)), ((), ())), preferred_element_type=F32)
            s = jnp.where(valid, s * scale, -jnp.inf)
            sk = _per_row_sink(sink_ref, kvh)
            m = jnp.maximum(jnp.max(s, axis=1, keepdims=True), sk)
            p = jnp.exp(s - m)
            l = jnp.sum(p, axis=1, keepdims=True) + jnp.exp(sk - m)
            o = jnp.dot((p / l).astype(BF16), vv, preferred_element_type=F32)
            lse = m + jnp.log(l)
            for g in range(Q_PER_KV):
                h = kvh * Q_PER_KV + g
                rs = slice(g * ATT_BLOCK, (g + 1) * ATT_BLOCK)
                o_ref[:, h * HEAD_DIM:(h + 1) * HEAD_DIM] = o[rs]
                lse_ref[:, h:h + 1] = lse[rs]

    out_specs = [pl.BlockSpec((ATT_BLOCK, dq), lambda n: (n, 0)), pl.BlockSpec((ATT_BLOCK, n_q), lambda n: (n, 0))]
    out_shape = [jax.ShapeDtypeStruct((rows, dq), F32), jax.ShapeDtypeStruct((rows, n_q), F32)]
    return pl.pallas_call(body, name=name, grid=(rows // ATT_BLOCK,), in_specs=in_specs, out_specs=out_specs,
                          out_shape=out_shape, compiler_params=_cp("parallel"))(sinks, qkv, qkv, qkv)


def _attn_bwd(qkv, dout, lse, sinks, *, name):
    rows = qkv.shape[0]
    n_q = sinks.shape[0]
    n_kv = n_q // Q_PER_KV
    scale = 1.0 / math.sqrt(HEAD_DIM)
    dq_w, dkv_w, in_specs = _attn_specs(rows, n_q)
    blk_q = pl.BlockSpec((ATT_BLOCK, dq_w), lambda n: (n, 0))
    blk_kv = pl.BlockSpec((ATT_BLOCK, dkv_w), lambda n: (n, 0))
    in_specs = in_specs + [blk_q, pl.BlockSpec((ATT_BLOCK, n_q), lambda n: (n, 0))]

    def body(sink_ref, q_ref, kvc_ref, kvp_ref, do_ref, lse_ref, dq_ref, dkc_ref, dkp_ref, dsink_ref):
        n = pl.program_id(0)

        @pl.when(n == 0)
        def _():
            dsink_ref[...] = jnp.zeros_like(dsink_ref)

        valid = _attn_mask(n)
        head_ids = lax.broadcasted_iota(jnp.int32, (1, n_q), 1)
        dsink = jnp.zeros((1, n_q), F32)
        for kvh in range(n_kv):
            kk, vv = _kv_pair(kvc_ref, kvp_ref, kvh, n_kv)
            qs = _stack_heads(q_ref, kvh, BF16)
            dos = _stack_heads(do_ref, kvh, BF16)
            lse = jnp.concatenate([lse_ref[:, kvh * Q_PER_KV + g:kvh * Q_PER_KV + g + 1] for g in range(Q_PER_KV)], axis=0)
            s = lax.dot_general(qs, kk, (((1,), (1,)), ((), ())), preferred_element_type=F32)
            s = jnp.where(valid, s * scale, -jnp.inf)
            p = jnp.exp(s - lse)
            dp = lax.dot_general(dos, vv, (((1,), (1,)), ((), ())), preferred_element_type=F32)
            delta = jnp.sum(p * dp, axis=1, keepdims=True)
            ds = (p * (dp - delta) * scale).astype(BF16)
            sink_term = jnp.exp(_per_row_sink(sink_ref, kvh) - lse) * delta
            dqs = jnp.dot(ds, kk, preferred_element_type=F32)
            for g in range(Q_PER_KV):
                h = kvh * Q_PER_KV + g
                rs = slice(g * ATT_BLOCK, (g + 1) * ATT_BLOCK)
                dsink = dsink + jnp.where(head_ids == h, -jnp.sum(sink_term[rs]), 0.0)
                dq_ref[:, h * HEAD_DIM:(h + 1) * HEAD_DIM] = dqs[rs]
            dk = lax.dot_general(ds, qs, (((0,), (0,)), ((), ())), preferred_element_type=F32)
            dv = lax.dot_general(p.astype(BF16), dos, (((0,), (0,)), ((), ())), preferred_element_type=F32)
            ks = slice(kvh * HEAD_DIM, (kvh + 1) * HEAD_DIM)
            vs = slice((n_kv + kvh) * HEAD_DIM, (n_kv + kvh + 1) * HEAD_DIM)
            dkp_ref[:, ks] = dk[0:ATT_BLOCK]
            dkc_ref[:, ks] = dk[ATT_BLOCK:]
            dkp_ref[:, vs] = dv[0:ATT_BLOCK]
            dkc_ref[:, vs] = dv[ATT_BLOCK:]
        dsink_ref[...] += dsink

    out_specs = [blk_q, blk_kv, blk_kv, pl.BlockSpec((1, n_q), lambda n: (0, 0))]
    out_shape = [jax.ShapeDtypeStruct((rows, dq_w), F32), jax.ShapeDtypeStruct((rows, dkv_w), F32),
                 jax.ShapeDtypeStruct((rows, dkv_w), F32), jax.ShapeDtypeStruct((1, n_q), F32)]
    return pl.pallas_call(body, name=name, grid=(rows // ATT_BLOCK,), in_specs=in_specs, out_specs=out_specs,
                          out_shape=out_shape, compiler_params=_cp("arbitrary"))(sinks, qkv, qkv, qkv, dout, lse)


def _dqkv_assemble(dq, dkc, dkp, *, name):
    rows, dq_w = dq.shape
    dkv_w = dkc.shape[1]
    nb = rows // ATT_BLOCK

    def body(dq_ref, dkc_ref, dkp_ref, o_ref, db_ref):
        n = pl.program_id(0)

        @pl.when(n == 0)
        def _():
            db_ref[...] = jnp.zeros_like(db_ref)

        dqv = dq_ref[...]
        dkv = dkc_ref[...] + jnp.where(n == nb - 1, 0.0, dkp_ref[...])
        o_ref[:, 0:dq_w] = dqv.astype(BF16)
        o_ref[:, dq_w:dq_w + dkv_w] = dkv.astype(BF16)
        db_ref[:, 0:dq_w] += _rowsum(dqv)
        db_ref[:, dq_w:dq_w + dkv_w] += _rowsum(dkv)

    width = dq_w + dkv_w
    in_specs = [pl.BlockSpec((ATT_BLOCK, dq_w), lambda n: (n, 0)), pl.BlockSpec((ATT_BLOCK, dkv_w), lambda n: (n, 0)),
                pl.BlockSpec((ATT_BLOCK, dkv_w), lambda n: (jnp.minimum(n + 1, nb - 1), 0))]
    out_specs = [pl.BlockSpec((ATT_BLOCK, width), lambda n: (n, 0)), pl.BlockSpec((1, width), lambda n: (0, 0))]
    out_shape = [jax.ShapeDtypeStruct((rows, width), BF16), jax.ShapeDtypeStruct((1, width), F32)]
    return pl.pallas_call(body, name=name, grid=(nb,), in_specs=in_specs, out_specs=out_specs, out_shape=out_shape,
                          compiler_params=_cp("arbitrary"))(dq, dkc, dkp)


def _row_tile(r, c):
    budget = 2 * 1024 * 1024 // (4 * c)
    for cand in (1024, 512, 256, 128, 64, 32, 16):
        if cand <= budget and r % cand == 0:
            return cand
    return r


def _octo_sum(own, recv, place, dest, lead, *, name):
    _, _, r, c = own.shape
    t = _row_tile(r, c)
    lead_idx, buf_shape = lead

    def body(place_ref, own_ref, *rest):
        o_ref = rest[7] if dest is None else rest[8]
        acc = own_ref[...].astype(F32)
        for k in range(7):
            acc = acc + rest[k][...].astype(F32)
        o_ref[...] = acc

    def peer(mask):
        return pl.BlockSpec((None, t, c), lambda i, pr: (pr[2] ^ mask, i, 0))

    if lead_idx is None:
        o_spec = pl.BlockSpec((None, t, c), lambda i, pr: (pr[1], i, 0))
    else:
        o_spec = pl.BlockSpec((None, None, t, c), lambda i, pr: (lead_idx, pr[1], i, 0))
    in_specs = [pl.BlockSpec((None, None, t, c), lambda i, pr: (pr[0], pr[1], i, 0))] + [peer(m) for m in range(1, 8)]
    args = [place, own] + [recv] * 7
    aliases = {}
    if dest is not None:
        in_specs.append(HBM)
        args.append(dest)
        aliases = {9: 0}
    grid_spec = pltpu.PrefetchScalarGridSpec(num_scalar_prefetch=1, grid=(r // t,), in_specs=in_specs, out_specs=o_spec)
    return pl.pallas_call(body, name=name, grid_spec=grid_spec, out_shape=jax.ShapeDtypeStruct(buf_shape, F32),
                          input_output_aliases=aliases, compiler_params=_cp("parallel"))(*args)


def _adamw_math(w, g, m, v):
    nm = ADAM_B1 * m + (1.0 - ADAM_B1) * g
    nv = ADAM_B2 * v + (1.0 - ADAM_B2) * (g * g)
    m_hat = nm / (1.0 - ADAM_B1 ** ADAM_STEP)
    v_hat = nv / (1.0 - ADAM_B2 ** ADAM_STEP)
    return -ADAM_LR * (m_hat / (jnp.sqrt(v_hat) + ADAM_EPS) + ADAM_WD * w), nm, nv


def _adamw(w, g, m, v, *, name):
    r, c = w.shape
    t = _row_tile(r, c)

    def body(w_ref, g_ref, m_ref, v_ref, d_ref, nm_ref, nv_ref, go_ref):
        gv = g_ref[...]
        d_ref[...], nm_ref[...], nv_ref[...] = _adamw_math(w_ref[...], gv, m_ref[...], v_ref[...])
        go_ref[...] = gv

    blk = pl.BlockSpec((t, c), lambda i: (i, 0))
    sds = jax.ShapeDtypeStruct((r, c), F32)
    return pl.pallas_call(body, name=name, grid=(r // t,), in_specs=[blk] * 4, out_specs=[blk] * 4,
                          out_shape=[sds] * 4, compiler_params=_cp("parallel"))(w, g, m, v)


HBM = pl.BlockSpec(memory_space=pl.ANY)


def _place():
    x, y, c = lax.axis_index("x"), lax.axis_index("y"), lax.axis_index("c")
    chips = [(1 - x, y), (x, 1 - y), (1 - x, 1 - y)]
    return x, y, c, 2 * x + y, (x, y, 1 - c), chips


def _rcopy(src, dst, ssem, rsem, dev):
    return pltpu.make_async_remote_copy(src_ref=src, dst_ref=dst, send_sem=ssem, recv_sem=rsem, device_id=dev,
                                        device_id_type=MESH)


HBM_ONLY = pl.BlockSpec(memory_space=pltpu.HBM)
SEM = pl.BlockSpec(memory_space=pltpu.SEMAPHORE)


def _peers():
    x, y, c = lax.axis_index("x"), lax.axis_index("y"), lax.axis_index("c")
    out = []
    for mask in range(1, 8):
        px = 1 - x if mask & 4 else x
        py = 1 - y if mask & 2 else y
        pc = 1 - c if mask & 1 else c
        out.append(((px, py, pc), 2 * px + py, pc, 4 * px + 2 * py + pc))
    return 4 * x + 2 * y + c, out


def _reduce_start(grads, lands, after, *, name):
    nt = len(grads)

    def body(*refs):
        ssems, rsems = refs[2 * nt + 1:3 * nt + 1], refs[3 * nt + 1:4 * nt + 1]
        g_out, l_out, token = refs[4 * nt + 1:5 * nt + 1], refs[5 * nt + 1:6 * nt + 1], refs[6 * nt + 1]
        me, peers = _peers()
        for t in range(nt):
            for k, (dev, chip, core, _) in enumerate(peers):
                _rcopy(g_out[t].at[chip, core], l_out[t].at[me], ssems[t].at[k], rsems[t].at[k], dev).start()
        token[...] = jnp.zeros_like(token)

    sems = [pltpu.SemaphoreType.DMA((7,))] * (2 * nt)
    out_shape = (sems + [pltpu.HBM(g.shape, g.dtype) for g in grads] + [pltpu.HBM(l.shape, l.dtype) for l in lands]
                 + [jax.ShapeDtypeStruct((8, LANES), F32)])
    res = pl.pallas_call(
        body, name=name, in_specs=[HBM_ONLY] * (2 * nt + 1),
        out_specs=[SEM] * (2 * nt) + [HBM_ONLY] * (2 * nt) + [pl.BlockSpec(memory_space=pltpu.VMEM)], out_shape=out_shape,
        input_output_aliases={t: 2 * nt + t for t in range(2 * nt)},
        compiler_params=pltpu.CompilerParams(has_side_effects=DATAFLOW),
    )(*[pltpu.with_memory_space_constraint(a, pltpu.HBM) for a in list(grads) + list(lands) + [after]])
    return res[:nt], res[nt:2 * nt], res[2 * nt:3 * nt], res[3 * nt:4 * nt], res[4 * nt]


def _reduce_wait(grads, lands, ssems, rsems, after, *, name):
    nt = len(grads)

    def body(*refs):
        ssem_refs, rsem_refs = refs[2 * nt:3 * nt], refs[3 * nt:4 * nt]
        g_out, l_out = refs[4 * nt + 1:5 * nt + 1], refs[5 * nt + 1:6 * nt + 1]
        me, peers = _peers()
        for t in range(nt):
            for k, (dev, chip, core, _) in enumerate(peers):
                _rcopy(g_out[t].at[chip, core], l_out[t].at[me], ssem_refs[t].at[k], rsem_refs[t].at[k], dev).wait_send()
        for t in range(nt):
            for k, (dev, _, _, idx) in enumerate(peers):
                slot = l_out[t].at[idx]
                _rcopy(slot, slot, ssem_refs[t].at[k], rsem_refs[t].at[k], dev).wait_recv()

    res = pl.pallas_call(
        body, name=name, in_specs=[HBM_ONLY] * (2 * nt) + [SEM] * (2 * nt) + [HBM_ONLY], out_specs=[HBM_ONLY] * (2 * nt),
        out_shape=[pltpu.HBM(a.shape, a.dtype) for a in list(grads) + list(lands)],
        input_output_aliases={t: t for t in range(2 * nt)},
        compiler_params=pltpu.CompilerParams(has_side_effects=DATAFLOW),
    )(*grads, *lands, *ssems, *rsems, pltpu.with_memory_space_constraint(after, pltpu.HBM))
    return list(res[:nt]), list(res[nt:])
DATAFLOW = pltpu.SideEffectType.DATAFLOW_SIDE_EFFECTING


def _gather_now(bufs, *, name):
    nt = len(bufs)

    def body(*refs):
        outs = refs[nt:2 * nt]
        ssem, rsem = refs[2 * nt:]
        x, y, c, q, sib, chips = _place()
        sends = []
        for t in range(nt):
            for j, (px, py) in enumerate(chips):
                mine = outs[t].at[q, c]
                cp = _rcopy(mine, mine, ssem.at[t, j], rsem.at[t, j], (px, py, c))
                cp.start()
                sends.append(cp)
        for t in range(nt):
            for j, (px, py) in enumerate(chips):
                landed = outs[t].at[2 * px + py, c]
                _rcopy(landed, landed, ssem.at[t, j], rsem.at[t, j], (px, py, c)).wait_recv()
                cp = _rcopy(landed, landed, ssem.at[t, 3 + j], rsem.at[t, 3 + j], sib)
                cp.start()
                sends.append(cp)
        for t in range(nt):
            for j, (px, py) in enumerate(chips):
                passed = outs[t].at[2 * px + py, 1 - c]
                _rcopy(passed, passed, ssem.at[t, 3 + j], rsem.at[t, 3 + j], sib).wait_recv()
        for cp in sends:
            cp.wait_send()

    out_shape = [jax.ShapeDtypeStruct(b.shape, b.dtype) for b in bufs]
    return pl.pallas_call(
        body, name=name, in_specs=[HBM] * nt, out_specs=[HBM] * nt, out_shape=out_shape,
        input_output_aliases={t: t for t in range(nt)},
        scratch_shapes=[pltpu.SemaphoreType.DMA((nt, 6)), pltpu.SemaphoreType.DMA((nt, 6))],
    )(*bufs)


def _gather_start(bufs, half, *, name):
    nt = len(bufs)

    def body(*refs):
        ssems, rsems, outs = refs[nt:2 * nt], refs[2 * nt:3 * nt], refs[3 * nt:4 * nt]
        x, y, c, q, sib, chips = _place()
        for t in range(nt):
            for j, (px, py) in enumerate(chips):
                mine = outs[t].at[q, c] if half[t] else outs[t].at[q]
                _rcopy(mine, mine, ssems[t].at[j], rsems[t].at[j], (px, py, c)).start()

    sems = [pltpu.SemaphoreType.DMA((3,))] * (2 * nt)
    out_shape = sems + [pltpu.HBM(b.shape, b.dtype) for b in bufs]
    res = pl.pallas_call(
        body, name=name, in_specs=[HBM_ONLY] * nt, out_specs=[SEM] * (2 * nt) + [HBM_ONLY] * nt, out_shape=out_shape,
        input_output_aliases={t: 2 * nt + t for t in range(nt)},
        compiler_params=pltpu.CompilerParams(has_side_effects=DATAFLOW),
    )(*[pltpu.with_memory_space_constraint(b, pltpu.HBM) for b in bufs])
    return res[:nt], res[nt:2 * nt], res[2 * nt:]


def _gather_wait(bufs, half, ssems, rsems, after, *, name):
    nt = len(bufs)

    def body(*refs):
        ssem_refs, rsem_refs = refs[nt:2 * nt], refs[2 * nt:3 * nt]
        outs = refs[3 * nt + 1:]
        x, y, c, q, sib, chips = _place()
        for t in range(nt):
            for j, (px, py) in enumerate(chips):
                mine = outs[t].at[q, c] if half[t] else outs[t].at[q]
                _rcopy(mine, mine, ssem_refs[t].at[j], rsem_refs[t].at[j], (px, py, c)).wait_send()
        for t in range(nt):
            for j, (px, py) in enumerate(chips):
                theirs = outs[t].at[2 * px + py, c] if half[t] else outs[t].at[2 * px + py]
                _rcopy(theirs, theirs, ssem_refs[t].at[j], rsem_refs[t].at[j], (px, py, c)).wait_recv()

    res = pl.pallas_call(
        body, name=name, in_specs=[HBM_ONLY] * nt + [SEM] * (2 * nt) + [HBM], out_specs=[HBM_ONLY] * nt,
        out_shape=[pltpu.HBM(b.shape, b.dtype) for b in bufs], input_output_aliases={t: t for t in range(nt)},
        compiler_params=pltpu.CompilerParams(has_side_effects=DATAFLOW),
    )(*bufs, *ssems, *rsems, after)
    return list(res)


def _sibling_swap(bufs, *, name):
    nt = len(bufs)

    def body(*refs):
        outs = refs[nt:2 * nt]
        ssem, rsem = refs[2 * nt:]
        x, y, c, q, sib, chips = _place()
        sends = []
        for t in range(nt):
            for j, (px, py) in enumerate(chips):
                held = outs[t].at[2 * px + py, c]
                cp = _rcopy(held, held, ssem.at[t, j], rsem.at[t, j], sib)
                cp.start()
                sends.append(cp)
        for t in range(nt):
            for j, (px, py) in enumerate(chips):
                other = outs[t].at[2 * px + py, 1 - c]
                _rcopy(other, other, ssem.at[t, j], rsem.at[t, j], sib).wait_recv()
        for cp in sends:
            cp.wait_send()

    return pl.pallas_call(
        body, name=name, in_specs=[HBM] * nt, out_specs=[HBM] * nt,
        out_shape=[jax.ShapeDtypeStruct(b.shape, b.dtype) for b in bufs], input_output_aliases={t: t for t in range(nt)},
        scratch_shapes=[pltpu.SemaphoreType.DMA((nt, 3)), pltpu.SemaphoreType.DMA((nt, 3))],
    )(*bufs)


def _sibling_share(bufs, layout, *, name):
    no = len(bufs)
    nt = len(layout)

    def body(*refs):
        outs = refs[no:2 * no]
        ssem, rsem = refs[2 * no:]
        x, y, c, q, sib, chips = _place()

        def slot(t, half):
            o, lead = layout[t]
            return outs[o].at[half] if lead is None else outs[o].at[lead, half]

        sends = []
        for t in range(nt):
            cp = _rcopy(slot(t, c), slot(t, c), ssem.at[t], rsem.at[t], sib)
            cp.start()
            sends.append(cp)
        for t in range(nt):
            _rcopy(slot(t, 1 - c), slot(t, 1 - c), ssem.at[t], rsem.at[t], sib).wait_recv()
        for cp in sends:
            cp.wait_send()

    out_shape = [jax.ShapeDtypeStruct(b.shape, b.dtype) for b in bufs]
    return pl.pallas_call(
        body, name=name, in_specs=[HBM] * no, out_specs=[HBM] * no, out_shape=out_shape,
        input_output_aliases={o: o for o in range(no)},
        scratch_shapes=[pltpu.SemaphoreType.DMA((nt,)), pltpu.SemaphoreType.DMA((nt,))],
    )(*bufs)


def _small_tail(local, params, *, name):
    (cwa, cba, ga, ba, gb, bb, dws, dsb, dbq, dsk, cwf0, cbf0, cwf1, cbf1,
     g00, g01, g10, g11, b00, b01, b10, b11, err) = local
    n_local = len(local)
    kw, wa = cwa.shape
    ng = dws.shape[0]
    nqkv = dbq.shape[1]
    nsk = dsk.shape[1]
    f = cwf0.shape[2]
    dm = err.shape[1]
    row_vec = 8 * (-(-kw // 8))
    shapes = [(row_vec + 8, wa), (ng * B_CHUNK + 8, B_CHUNK), (8, nqkv), (2, 2, 8, f), (16, dm)]
    n_grp = len(shapes)
    flat_params = [a for triple in params for a in triple]
    n_par = len(params)

    def reduce_body(*refs):
        loc = refs[:n_local]
        tot = refs[n_local:n_local + n_grp]
        scr = refs[n_local + n_grp:]
        grp, from_sib, pair, gath = (scr[k * n_grp:(k + 1) * n_grp] for k in range(4))
        ssem1, rsem1, ssem2, rsem2 = scr[4 * n_grp:]
        x, y, core, q, sib, chips = _place()

        for gr in grp:
            gr[...] = jnp.zeros_like(gr)
        a, b, c, dd, e = grp
        a[0:kw, :] = loc[0][...]
        for k in range(5):
            a[row_vec + k:row_vec + k + 1, :] = loc[1 + k][...]
        for g in range(ng):
            b[g * B_CHUNK:(g + 1) * B_CHUNK, :] = loc[6][g]
        b[ng * B_CHUNK:ng * B_CHUNK + ng, :] = loc[7][...]
        c[0:1, :] = loc[8][...]
        c[1:2, 0:nsk] = loc[9][...]
        for l in range(2):
            for s in range(2):
                dd[l, s, 0:3, :] = loc[10 + 2 * l][s]
                dd[l, s, 3:4, :] = loc[11 + 2 * l][s]
        for k in range(9):
            e[k:k + 1, :] = loc[14 + k][...]

        sends = []
        for gi in range(n_grp):
            cp = _rcopy(grp[gi], from_sib[gi], ssem1.at[gi], rsem1.at[gi], sib)
            cp.start()
            sends.append(cp)
        for gi in range(n_grp):
            _rcopy(grp[gi], from_sib[gi], ssem1.at[gi], rsem1.at[gi], sib).wait_recv()
            both = grp[gi][...] + from_sib[gi][...]
            pair[gi][...] = both
            gath[gi][q] = both
            for j, (px, py) in enumerate(chips):
                cp = _rcopy(pair[gi], gath[gi].at[q], ssem2.at[gi, j], rsem2.at[gi, j], (px, py, core))
                cp.start()
                sends.append(cp)
        for gi in range(n_grp):
            for j, (px, py) in enumerate(chips):
                slot = gath[gi].at[2 * px + py]
                _rcopy(slot, slot, ssem2.at[gi, j], rsem2.at[gi, j], (px, py, core)).wait_recv()
            acc = gath[gi][0]
            for k in range(1, 4):
                acc = acc + gath[gi][k]
            tot[gi][...] = acc
        for cp in sends:
            cp.wait_send()

    vm = pl.BlockSpec(memory_space=pltpu.VMEM)
    scratch = ([pltpu.VMEM(s, F32) for s in shapes] * 3 + [pltpu.VMEM((4,) + s, F32) for s in shapes]
               + [pltpu.SemaphoreType.DMA((n_grp,)), pltpu.SemaphoreType.DMA((n_grp,)),
                  pltpu.SemaphoreType.DMA((n_grp, 3)), pltpu.SemaphoreType.DMA((n_grp, 3))])
    totals = pl.pallas_call(
        reduce_body, name=name + "_reduce", in_specs=[vm] * n_local, out_specs=[vm] * n_grp,
        out_shape=[jax.ShapeDtypeStruct(s, F32) for s in shapes], scratch_shapes=scratch,
        compiler_params=pltpu.CompilerParams(vmem_limit_bytes=VMEM_LIMIT),
    )(*local)

    def adamw_body(*refs):
        ta, tb, tc, td, te = refs[:n_grp]
        par = refs[n_grp:n_grp + 3 * n_par]
        outs = refs[n_grp + 3 * n_par:n_grp + 7 * n_par]
        loss_ref = refs[n_grp + 7 * n_par]
        q = 2 * lax.axis_index("x") + lax.axis_index("y")

        def mine(piece):
            out = piece(0)
            for k in range(1, 4):
                out = jnp.where(q == k, piece(k), out)
            return out

        def update(p, grad, index=None):
            at = (lambda r: r[...]) if index is None else (lambda r: r[index])
            w_ref, m_ref, v_ref = par[3 * p:3 * p + 3]
            g_ref, d_ref, nm_ref, nv_ref = outs[4 * p:4 * p + 4]
            delta, nm, nv = _adamw_math(at(w_ref), grad, at(m_ref), at(v_ref))
            for r, val in ((g_ref, grad), (d_ref, delta), (nm_ref, nm), (nv_ref, nv)):
                if index is None:
                    r[...] = val
                else:
                    r[index] = val

        wq = wa // 4
        update(0, mine(lambda k: ta[0:kw, k * wq:(k + 1) * wq]), (0,))
        for k in range(5):
            update(1 + k, ta[row_vec + k:row_vec + k + 1, :])
        for g in range(ng):
            update(6, tb[g * B_CHUNK:(g + 1) * B_CHUNK, :], (0, g))
        update(7, tb[ng * B_CHUNK:ng * B_CHUNK + ng, :], (0,))
        nq4 = nqkv // 4
        update(8, mine(lambda k: tc[0:1, k * nq4:(k + 1) * nq4]))
        update(9, tc[1:2, 0:nsk])
        fh = f // 2
        for l in range(2):
            update(10, mine(lambda k: td[l, k // 2, 0:3, (k % 2) * fh:(k % 2 + 1) * fh]), (l,))
            update(11, jnp.concatenate([td[l, 0, 3:4, :], td[l, 1, 3:4, :]], axis=1), (slice(l, l + 1),))
        dq4 = dm // 4
        for i in range(2):
            for j in range(2):
                for p, base in ((12, 0), (13, 4)):
                    row = base + 2 * i + j
                    update(p, mine(lambda k: te[row:row + 1, k * dq4:(k + 1) * dq4]), (i, slice(j, j + 1)))
        loss_ref[...] = (0.5 / dm) * jnp.sum(te[8:9, :], axis=1, keepdims=True)

    out_shape = []
    for w, _, _ in params:
        out_shape += [jax.ShapeDtypeStruct(w.shape, F32)] * 4
    out_shape.append(jax.ShapeDtypeStruct((1, 1), F32))
    res = pl.pallas_call(
        adamw_body, name=name + "_adamw", in_specs=[vm] * (n_grp + 3 * n_par), out_specs=[vm] * len(out_shape),
        out_shape=out_shape, compiler_params=pltpu.CompilerParams(vmem_limit_bytes=VMEM_LIMIT),
    )(*totals, *flat_params)
    return [res[4 * p:4 * p + 4] for p in range(n_par)], res[-1]


def _pack(arrays, rows_multiple):
    flat = jnp.concatenate([a.reshape(-1) for a in arrays])
    rows = -(-flat.shape[0] // LANES)
    rows = -(-rows // rows_multiple) * rows_multiple
    flat = jnp.pad(flat, (0, rows * LANES - flat.shape[0]))
    return flat.reshape(rows, LANES)


def _unshard_cols(stacked):
    moved = jnp.moveaxis(stacked, 0, -2)
    return moved.reshape(moved.shape[:-2] + (4 * stacked.shape[-1],))


def kernel(x, ab_w_in, a_conv_w, a_conv_b, a_norm_g, a_norm_b, b_norm_g, b_norm_b, b_spatial_w, b_spatial_b, ab_w_out, c_w_qkv, c_b_qkv, c_sinks, c_w_o, ffn_w_up, ffn_conv_w, ffn_conv_b, ffn_w_down, ln_g, ln_b, loss_target, m_ab_w_in, m_a_conv_w, m_a_conv_b, m_a_norm_g, m_a_norm_b, m_b_norm_g, m_b_norm_b, m_b_spatial_w, m_b_spatial_b, m_ab_w_out, m_c_w_qkv, m_c_b_qkv, m_c_sinks, m_c_w_o, m_ffn_w_up, m_ffn_conv_w, m_ffn_conv_b, m_ffn_w_down, m_ln_g, m_ln_b, v_ab_w_in, v_a_conv_w, v_a_conv_b, v_a_norm_g, v_a_norm_b, v_b_norm_g, v_b_norm_b, v_b_spatial_w, v_b_spatial_b, v_ab_w_out, v_c_w_qkv, v_c_b_qkv, v_c_sinks, v_c_w_o, v_ffn_w_up, v_ffn_conv_w, v_ffn_conv_b, v_ffn_w_down, v_ln_g, v_ln_b):
    rows, d = x.shape[1], x.shape[2]
    depth = ln_g.shape[0]
    assert depth == 2 and x.shape[0] == 1
    alpha = (2.0 * depth) ** 0.25
    f = ffn_w_down.shape[1] * 4
    n_q = c_sinks.shape[1]
    q_idx = 2 * lax.axis_index("x") + lax.axis_index("y")
    c_idx = lax.axis_index("c")
    xs, tgt = x[0], loss_target[0]

    def own_slot(part):
        buf = lax.empty((4,) + part.shape, part.dtype)
        return lax.dynamic_update_slice(buf, part[None], (q_idx, 0, 0, 0))

    def halves(wm):
        return own_slot(wm.astype(BF16).reshape((2, wm.shape[0] // 2) + wm.shape[1:]))

    small_sharded = [a_conv_w[0], c_b_qkv[0], ffn_conv_w, ln_g, ln_b]
    small_pack = _pack(small_sharded, 16)
    bufs = [halves(ab_w_in[0]), own_slot(small_pack.reshape(2, small_pack.shape[0] // 2, LANES)), halves(ab_w_out[0]),
            halves(ffn_w_up[0]), halves(ffn_w_down[0]), halves(c_w_qkv[0]), halves(c_w_o[0]),
            halves(ffn_w_up[1]), halves(ffn_w_down[1])]
    whole = lambda g: g.reshape(4, 2 * g.shape[2], g.shape[3])
    n_now = 2
    w_in, small_all = [whole(g) for g in _gather_now(bufs[:n_now], name="gather_now")]
    later = bufs[n_now:]
    half = [True, True] + [False] * (len(later) - 2)
    ssems, rsems, started = _gather_start(later, half, name="gather_start")

    def arrive(idx, after, tag):
        idx = [i - n_now for i in idx]
        halved = [half[i] for i in idx]
        got = _gather_wait([started[i] for i in idx], halved, [ssems[i] for i in idx], [rsems[i] for i in idx], after,
                           name=f"gather_wait_{tag}")
        if all(halved):
            got = _sibling_swap(got, name=f"gather_swap_{tag}")
        return [whole(g) for g in got]

    small_all = small_all.reshape(4, -1)
    sh_shapes = [s.shape for s in small_sharded]
    pieces, pos = [], 0
    for s in sh_shapes:
        n = math.prod(s)
        pieces.append(_unshard_cols(small_all[:, pos:pos + n].reshape((4,) + s)))
        pos += n
    conv_w_a, b_qkv, conv_w_f, ln_gf, ln_bf = pieces

    tril = jnp.tril(jnp.ones((B_CHUNK, B_CHUNK), F32))
    ws = (b_spatial_w[0] * tril).astype(BF16)
    wst = jnp.swapaxes(ws, 1, 2)
    sbb = jnp.broadcast_to(b_spatial_b[0][:, :, None], b_spatial_w[0].shape)
    mix_vecs = [a_conv_b, a_norm_g, a_norm_b, b_norm_g, b_norm_b]
    cw_f = [jnp.swapaxes(conv_w_f[l].reshape(3, 2, f), 0, 1) for l in range(depth)]
    cb_f = [ffn_conv_b[l].reshape(2, 1, f) for l in range(depth)]
    lng = lambda i, j: ln_gf[i, j].reshape(1, d)
    lnb = lambda i, j: ln_bf[i, j].reshape(1, d)
    sinks = c_sinks[0]

    w_up, w_down = [None, None], [None, None]

    def ffn_fwd(xin, l):
        w_up[l], = arrive([3 + 4 * l], xin, f"up{l}")
        hf, fact = _ffn_up_fwd(xin, w_up[l], cw_f[l], cb_f[l], name=f"ffn{l}_up")
        w_down[l] = arrive([4 + 4 * l], fact, f"down{l}")[0].reshape(-1, d)
        out = _matmul(fact, w_down[l], name=f"ffn{l}_down", tm=512, tn=1024, tk=2816)
        return hf, fact, out

    h0, ab, a2 = _mixer_fwd(xs, w_in, conv_w_a, *mix_vecs, ws, sbb, name="mix_fwd")
    w_out = arrive([2], ab, "out")[0].reshape(-1, d)
    mix = _matmul(ab, w_out, name="mix_out", tm=1024, tn=1024, tk=1024)
    x1 = _add_ln_fwd(xs, mix, lng(0, 0), lnb(0, 0), alpha, name="ln00")
    hf0, f0, ffn0 = ffn_fwd(x1, 0)
    x2 = _add_ln_fwd(x1, ffn0, lng(0, 1), lnb(0, 1), alpha, name="ln01")
    w_qkv = _unshard_cols(arrive([5], x2, "qkv")[0])
    qkv = _matmul(x2, w_qkv, name="att_qkv", tm=1024, tn=w_qkv.shape[1], tk=1024, bias=b_qkv.reshape(1, -1))
    ao, lse = _attn_fwd(qkv, sinks, name="att_core")
    w_o = arrive([6], ao, "o")[0].reshape(-1, d)
    att = _matmul(ao, w_o, name="att_out", tm=1024, tn=1024, tk=1024)
    x3 = _add_ln_fwd(x2, att, lng(1, 0), lnb(1, 0), alpha, name="ln10")
    hf1, f1, ffn1 = ffn_fwd(x3, 1)
    sq_err, dy = _add_ln_loss(x3, ffn1, lng(1, 1), lnb(1, 1), tgt, alpha, name="ln11_loss")

    def owner_view(g):
        if g.ndim == 3:
            return g.reshape(4, 2, g.shape[1] // 2, g.shape[2])
        return g.reshape(4, 2, g.shape[0] // 8, g.shape[1])

    in_flight = []

    def send_grads(tag, grads, after):
        lands = [lax.empty((8,) + g.shape[2:], BF16) for g in grads]
        ss, rs, g_thru, l_thru, token = _reduce_start(grads, lands, after, name=f"reduce_start_{tag}")
        in_flight.append((tag, g_thru, l_thru, ss, rs))
        return token[0:1, 0:1]

    def ffn_bwd(dz, xin, hf, fact, l):
        d_wdown = _matmul(fact, dz, name=f"ffn{l}_down_dw", ta=True, tm=1408, tn=1024, tk=2048, out_dtype=BF16)
        dfa = _matmul(dz, w_down[l], name=f"ffn{l}_down_dx", tb=True, tm=1024, tn=1408, tk=1024, out_dtype=BF16)
        dx_parts, d_wup, dcw, dcb = _ffn_up_bwd(hf, dfa, xin, w_up[l], cw_f[l], cb_f[l], name=f"ffn{l}_up_bwd")
        tok = send_grads(f"ffn{l}", [owner_view(d_wup), owner_view(d_wdown)], dcb)
        return [(dx_parts, 1.0), (dz, alpha)], dcw, dcb, tok

    dz, dg11, db11 = _add_ln_bwd([(dy, 1.0)], x3, ffn1, lng(1, 1), alpha, name="ln11_bwd")
    dx3, dcw1, dcb1, tok = ffn_bwd(dz, x3, hf1, f1, 1)
    dz, dg10, db10 = _add_ln_bwd(dx3, x2, att, lng(1, 0) + tok, alpha, name="ln10_bwd")
    d_wo = _matmul(ao, dz, name="att_out_dw", ta=True, tm=1024, tn=1024, tk=1024, out_dtype=BF16)
    dao = _matmul(dz, w_o, name="att_out_dx", tb=True, tm=1024, tn=1024, tk=1024)
    dq, dkc, dkp, d_sinks = _attn_bwd(qkv, dao, lse, sinks, name="att_core_bwd")
    dqkv, d_bqkv = _dqkv_assemble(dq, dkc, dkp, name="att_dqkv")
    d_wqkv = _matmul(x2, dqkv, name="att_qkv_dw", ta=True, tm=1024, tn=dqkv.shape[1], tk=1024, out_dtype=BF16)
    d_wqkv_st = jnp.moveaxis(d_wqkv.reshape(d_wqkv.shape[0], 4, -1), 1, 0)
    tok = send_grads("att", [owner_view(d_wqkv_st), owner_view(d_wo)], d_bqkv)
    dx2 = _matmul(dqkv, w_qkv, name="att_qkv_dx", tb=True, tm=1024, tn=1024, tk=dqkv.shape[1], addend=(dz, alpha))
    dz, dg01, db01 = _add_ln_bwd([(dx2, 1.0)], x1, ffn0, lng(0, 1) + tok, alpha, name="ln01_bwd")
    dx1, dcw0, dcb0, tok = ffn_bwd(dz, x1, hf0, f0, 0)
    dz, dg00, db00 = _add_ln_bwd(dx1, xs, mix, lng(0, 0) + tok, alpha, name="ln00_bwd")
    d_wout = _matmul(ab, dz, name="mix_out_dw", ta=True, tm=1024, tn=1024, tk=1024, out_dtype=BF16)
    dab = _matmul(dz, w_out, name="mix_out_dx", tb=True, tm=1024, tn=1024, tk=1024)
    grad_x, d_win, d_cwa, d_cba, d_ga, d_ba, d_gb, d_bb, d_ws, d_sb = _mixer_bwd(
        h0, a2, dab, xs, w_in, dz, alpha, conv_w_a, *mix_vecs[1:], ws, wst, sbb, tril, name="mix_bwd")

    small_w = [a_conv_w, a_conv_b, a_norm_g, a_norm_b, b_norm_g, b_norm_b, b_spatial_w, b_spatial_b, c_b_qkv,
               c_sinks, ffn_conv_w, ffn_conv_b, ln_g, ln_b]
    small_m = [m_a_conv_w, m_a_conv_b, m_a_norm_g, m_a_norm_b, m_b_norm_g, m_b_norm_b, m_b_spatial_w, m_b_spatial_b,
               m_c_b_qkv, m_c_sinks, m_ffn_conv_w, m_ffn_conv_b, m_ln_g, m_ln_b]
    small_v = [v_a_conv_w, v_a_conv_b, v_a_norm_g, v_a_norm_b, v_b_norm_g, v_b_norm_b, v_b_spatial_w, v_b_spatial_b,
               v_c_b_qkv, v_c_sinks, v_ffn_conv_w, v_ffn_conv_b, v_ln_g, v_ln_b]
    local = [d_cwa, d_cba, d_ga, d_ba, d_gb, d_bb, d_ws, d_sb, d_bqkv, d_sinks, dcw0, dcb0, dcw1, dcb1,
             dg00, dg01, dg10, dg11, db00, db01, db10, db11, sq_err]
    small_out, loss = _small_tail(local, list(zip(small_w, small_m, small_v)), name="small_tail")
    loss = loss[0, 0]
    small_g = [o[0] for o in small_out]
    sm_delta = [o[1] for o in small_out]
    sm_m = [o[2] for o in small_out]
    sm_v = [o[3] for o in small_out]

    place = jnp.stack([q_idx, c_idx, 4 * lax.axis_index("x") + 2 * lax.axis_index("y") + c_idx]).astype(jnp.int32)
    where = {"mix": [(0, None), (1, None)], "att": [(2, None), (3, None)], "ffn0": [(4, 0), (5, 0)], "ffn1": [(4, 1), (5, 1)]}
    big_w = [ab_w_in, ab_w_out, c_w_qkv, c_w_o, ffn_w_up, ffn_w_down]
    big_m = [m_ab_w_in, m_ab_w_out, m_c_w_qkv, m_c_w_o, m_ffn_w_up, m_ffn_w_down]
    big_v = [v_ab_w_in, v_ab_w_out, v_c_w_qkv, v_c_w_o, v_ffn_w_up, v_ffn_w_down]
    big_out = [None] * 6

    def finish(tags, after, label):
        bufs, layout = {}, []
        for tag, g_thru, l_thru, ss, rs in in_flight:
            if tag not in tags:
                continue
            own, landed = _reduce_wait(g_thru, l_thru, ss, rs, after, name=f"reduce_wait_{tag}")
            for k, (o, lead) in enumerate(where[tag]):
                piece = own[k].shape[2:]
                shape = (2,) + piece if lead is None else (2, 2) + piece
                bufs[o] = _octo_sum(own[k], landed[k], place, bufs.get(o), (lead, shape), name=f"reduce_sum_{tag}{k}")
                layout.append((o, lead))
        order = sorted(bufs)
        shared = _sibling_share([bufs[o] for o in order], [(order.index(o), lead) for o, lead in layout],
                                name=f"reduce_share_{label}")
        for o, g in zip(order, shared):
            w = big_w[o]
            two_d = lambda a: a.reshape(-1, a.shape[-1])
            outs = _adamw(two_d(w), two_d(g), two_d(big_m[o]), two_d(big_v[o]), name=f"adamw_big{o}")
            big_out[o] = [r.reshape(w.shape) for r in outs]
        return big_out[order[-1]][0]

    tok = send_grads("mix", [owner_view(d_win), owner_view(d_wout)], after=sm_delta[0])
    done = finish(("ffn1", "att", "ffn0"), sm_delta[6] + tok, "early")
    finish(("mix",), done, "mix")

    order_big = {0: 0, 9: 1, 10: 2, 13: 3, 14: 4, 17: 5}
    order_small = {1: 0, 2: 1, 3: 2, 4: 3, 5: 4, 6: 5, 7: 6, 8: 7, 11: 8, 12: 9, 15: 10, 16: 11, 18: 12, 19: 13}
    grads, deltas, new_m, new_v = [], [], [], []
    for pos_w in range(20):
        if pos_w in order_big:
            t = order_big[pos_w]
            grads.append(big_out[t][3])
            deltas.append(big_out[t][0])
            new_m.append(big_out[t][1])
            new_v.append(big_out[t][2])
        else:
            t = order_small[pos_w]
            grads.append(small_g[t])
            deltas.append(sm_delta[t])
            new_m.append(sm_m[t])
            new_v.append(sm_v[t])
    return (loss, grad_x[None], *grads, *deltas, *new_m, *new_v)
```

```python
import math

import jax
import jax.numpy as jnp
from jax import lax
from jax.experimental import pallas as pl
from jax.experimental.pallas import tpu as pltpu

F32 = jnp.float32
BF16 = jnp.bfloat16
MESH = pl.DeviceIdType.MESH

LN_EPS = 1e-5
HEAD_DIM = 64
ATT_BLOCK = 128
Q_PER_KV = 8
A_KERNEL = 31
CONV_HALO = 32
FFN_HALO = 8
BF16_ROWS = 16
B_CHUNK = 128
LANES = 128
MXU_WIDTH = 256
GELU_C = math.sqrt(2.0 / math.pi)
ADAM_LR = 0.001
ADAM_B1 = 0.9
ADAM_B2 = 0.999
ADAM_EPS = 1e-08
ADAM_WD = 0.01
ADAM_STEP = 10
VMEM_LIMIT = 56 * 1024 * 1024


def _cp(*dims):
    return pltpu.CompilerParams(dimension_semantics=dims, vmem_limit_bytes=VMEM_LIMIT)


def _pick(n, prefs):
    for p in prefs:
        if n % p == 0:
            return p
    return n


def _sig(x):
    return 1.0 / (1.0 + jnp.exp(-x))


def _gelu(x):
    t = jnp.tanh(GELU_C * (x + 0.044715 * (x * x * x)))
    return x * (0.5 * (1.0 + t)), t


def _gelu_grad(x, t):
    return 0.5 * (1.0 + t) + 0.5 * x * (1.0 - t * t) * (GELU_C * (1.0 + 3.0 * 0.044715 * x * x))


def _ln_stats(z):
    mu = jnp.mean(z, axis=-1, keepdims=True)
    zc = z - mu
    var = jnp.mean(zc * zc, axis=-1, keepdims=True)
    rstd = lax.rsqrt(var + LN_EPS)
    return zc * rstd, rstd


def _ln_bwd(dxh, xh, rstd):
    return rstd * (dxh - jnp.mean(dxh, axis=-1, keepdims=True) - xh * jnp.mean(dxh * xh, axis=-1, keepdims=True))


def _rowsum(a):
    return jnp.sum(a, axis=0, keepdims=True)


def _lshape(a):
    return (a.shape[0], a.shape[1]) if a.ndim == 2 else (a.shape[1], a.shape[0] * a.shape[2])


def _spec2(arr, blk_r, blk_c, ridx, cidx):
    if len(arr.shape) == 2:
        return pl.BlockSpec((blk_r, blk_c), lambda i, j, k: (ridx(i, j, k), cidx(i, j, k)))
    per = arr.shape[2] // blk_c
    assert arr.shape[2] % blk_c == 0
    return pl.BlockSpec((None, blk_r, blk_c), lambda i, j, k: (cidx(i, j, k) // per, ridx(i, j, k), cidx(i, j, k) % per))


def _matmul(a, b, *, name, ta=False, tb=False, tm, tn, tk, out_dtype=F32, out_stack=None, bias=None, addend=None):
    ar, ac = _lshape(a)
    br, bc = _lshape(b)
    m, kdim = (ac, ar) if ta else (ar, ac)
    n = br if tb else bc
    assert (bc if tb else br) == kdim
    tm, tn, tk = min(tm, m), min(tn, n), min(tk, kdim)
    assert m % tm == 0 and n % tn == 0 and kdim % tk == 0, (name, m, n, kdim, tm, tn, tk)
    nk = kdim // tk
    gi, gj, gk = (lambda i, j, k: i), (lambda i, j, k: j), (lambda i, j, k: k)
    a_spec = _spec2(a, tk, tm, gk, gi) if ta else _spec2(a, tm, tk, gi, gk)
    b_spec = _spec2(b, tn, tk, gj, gk) if tb else _spec2(b, tk, tn, gk, gj)
    if out_stack is None:
        out_sds = jax.ShapeDtypeStruct((m, n), out_dtype)
    else:
        out_sds = jax.ShapeDtypeStruct((out_stack, m, n // out_stack), out_dtype)
    o_spec = _spec2(out_sds, tm, tn, gi, gj)
    in_specs = [a_spec, b_spec]
    args = [a, b]
    if bias is not None:
        in_specs.append(pl.BlockSpec((1, tn), lambda i, j, k: (0, j)))
        args.append(bias)
    scale = None
    if addend is not None:
        add_arr, scale = addend
        in_specs.append(pl.BlockSpec((tm, tn), lambda i, j, k: (i, j)))
        args.append(add_arr)
    use_acc = nk > 1 and out_dtype != F32
    dn = (((0 if ta else 1,), (1 if tb else 0,)), ((), ()))

    def body(*refs):
        a_ref, b_ref = refs[0], refs[1]
        pos = 2
        bias_ref = add_ref = None
        if bias is not None:
            bias_ref = refs[pos]
            pos += 1
        if addend is not None:
            add_ref = refs[pos]
            pos += 1
        o_ref = refs[pos]
        acc_ref = refs[pos + 1] if use_acc else o_ref
        p = lax.dot_general(a_ref[...].astype(BF16), b_ref[...].astype(BF16), dn, preferred_element_type=F32)

        def finish(val):
            if bias_ref is not None:
                val = val + bias_ref[...]
            if add_ref is not None:
                val = val + scale * add_ref[...]
            return val.astype(out_dtype)

        if nk == 1:
            o_ref[...] = finish(p)
        else:
            k = pl.program_id(2)

            @pl.when(k == 0)
            def _():
                acc_ref[...] = p

            @pl.when(k > 0)
            def _():
                acc_ref[...] += p

            if use_acc or bias_ref is not None or add_ref is not None:
                @pl.when(k == nk - 1)
                def _():
                    o_ref[...] = finish(acc_ref[...])

    return pl.pallas_call(
        body, name=name, grid=(m // tm, n // tn, nk), in_specs=in_specs, out_specs=o_spec, out_shape=out_sds,
        scratch_shapes=[pltpu.VMEM((tm, tn), F32)] if use_acc else [],
        compiler_params=_cp("parallel", "parallel", "arbitrary"),
    )(*args)


def _add_ln_fwd(x, s, g, b, alpha, *, name):
    rows, d = x.shape
    t = _pick(rows, (512, 256))

    def body(x_ref, s_ref, g_ref, b_ref, y_ref):
        xh, _ = _ln_stats(alpha * x_ref[...] + s_ref[...])
        y_ref[...] = xh * g_ref[...] + b_ref[...]

    row = pl.BlockSpec((t, d), lambda i: (i, 0))
    vec = pl.BlockSpec((1, d), lambda i: (0, 0))
    return pl.pallas_call(body, name=name, grid=(rows // t,), in_specs=[row, row, vec, vec], out_specs=row,
                          out_shape=jax.ShapeDtypeStruct((rows, d), F32), compiler_params=_cp("parallel"))(x, s, g, b)


def _add_ln_bwd(dy_terms, x, s, g, alpha, *, name):
    rows, d = x.shape
    t = _pick(rows, (512, 256))
    nterm = len(dy_terms)
    scales = [sc for _, sc in dy_terms]
    ranks = [a.ndim for a, _ in dy_terms]

    def body(*refs):
        dy_refs = refs[:nterm]
        x_ref, s_ref, g_ref, dz_ref, dg_ref, db_ref = refs[nterm:]

        @pl.when(pl.program_id(0) == 0)
        def _():
            dg_ref[...] = jnp.zeros_like(dg_ref)
            db_ref[...] = jnp.zeros_like(db_ref)

        dyv = None
        for r, sc, rank in zip(dy_refs, scales, ranks):
            slabs = [r[...]] if rank == 2 else [r[p] for p in range(r.shape[0])]
            for v in slabs:
                v = v if sc == 1.0 else sc * v
                dyv = v if dyv is None else dyv + v
        xh, rstd = _ln_stats(alpha * x_ref[...] + s_ref[...])
        dz_ref[...] = _ln_bwd(dyv * g_ref[...], xh, rstd)
        dg_ref[...] += _rowsum(dyv * xh)
        db_ref[...] += _rowsum(dyv)

    row = pl.BlockSpec((t, d), lambda i: (i, 0))
    vec = pl.BlockSpec((1, d), lambda i: (0, 0))
    vsds = jax.ShapeDtypeStruct((1, d), F32)
    dy_specs = [row if a.ndim == 2 else pl.BlockSpec((a.shape[0], t, d), lambda i: (0, i, 0)) for a, _ in dy_terms]
    return pl.pallas_call(body, name=name, grid=(rows // t,), in_specs=dy_specs + [row, row, vec], out_specs=[row, vec, vec],
                          out_shape=[jax.ShapeDtypeStruct((rows, d), F32), vsds, vsds],
                          compiler_params=_cp("arbitrary"))(*[a for a, _ in dy_terms], x, s, g)


def _add_ln_loss(x, s, g, b, tgt, alpha, *, name):
    rows, d = x.shape
    t = _pick(rows, (512, 256))

    def body(x_ref, s_ref, g_ref, b_ref, t_ref, l_ref, dy_ref):
        @pl.when(pl.program_id(0) == 0)
        def _():
            l_ref[...] = jnp.zeros_like(l_ref)

        xh, _ = _ln_stats(alpha * x_ref[...] + s_ref[...])
        e = (xh * g_ref[...] + b_ref[...]) - t_ref[...]
        l_ref[...] += _rowsum(e * e)
        dy_ref[...] = e * (1.0 / d)

    row = pl.BlockSpec((t, d), lambda i: (i, 0))
    vec = pl.BlockSpec((1, d), lambda i: (0, 0))
    return pl.pallas_call(body, name=name, grid=(rows // t,), in_specs=[row, row, vec, vec, row], out_specs=[vec, row],
                          out_shape=[jax.ShapeDtypeStruct((1, d), F32), jax.ShapeDtypeStruct((rows, d), F32)],
                          compiler_params=_cp("arbitrary"))(x, s, g, b, tgt)


def _col_blocks(width, step):
    return [slice(pos, min(pos + step, width)) for pos in range(0, width, step)]


def _conv3(e, w, b):
    r1 = pltpu.roll(e, 1, 0)
    r2 = pltpu.roll(e, 2, 0)
    return w[0:1, :] * r2 + w[1:2, :] * r1 + w[2:3, :] * e + b, (r2, r1, e)


def _ffn_up_fwd(x, w_up, cw, cb, *, name):
    rows, d = x.shape
    nq, _, tc = w_up.shape
    nj = nq // 2
    f = tc * nj
    tm = _pick(rows, (512, 256))
    blocks = _col_blocks(tc, tc)

    def body(x_ref, wg_ref, wv_ref, cw_ref, cb_ref, hf_ref, f_ref, prev_ref):
        @pl.when(pl.program_id(1) == 0)
        def _():
            prev_ref[...] = jnp.zeros_like(prev_ref)

        xb = x_ref[...].astype(BF16)
        for cs in blocks:
            hc = []
            for s, w_ref in ((0, wg_ref), (1, wv_ref)):
                h = jnp.dot(xb, w_ref[:, cs], preferred_element_type=F32)
                hf_ref[s, :, cs] = h
                e = jnp.concatenate([prev_ref[s, :, cs], h], axis=0)
                prev_ref[s, :, cs] = h[tm - FFN_HALO:]
                y, _ = _conv3(e, cw_ref[s, :, cs], cb_ref[s, :, cs])
                hc.append(y[FFN_HALO:])
            gl, _ = _gelu(hc[0])
            f_ref[:, cs] = (gl * hc[1]).astype(BF16)

    in_specs = [
        pl.BlockSpec((tm, d), lambda j, i: (i, 0)),
        pl.BlockSpec((None, d, tc), lambda j, i: (j, 0, 0)),
        pl.BlockSpec((None, d, tc), lambda j, i: (nj + j, 0, 0)),
        pl.BlockSpec((2, 3, tc), lambda j, i: (0, 0, j)),
        pl.BlockSpec((2, 1, tc), lambda j, i: (0, 0, j)),
    ]
    out_specs = [pl.BlockSpec((2, tm, tc), lambda j, i: (0, i, j)), pl.BlockSpec((tm, tc), lambda j, i: (i, j))]
    out_shape = [jax.ShapeDtypeStruct((2, rows, f), F32), jax.ShapeDtypeStruct((rows, f), BF16)]
    return pl.pallas_call(body, name=name, grid=(nj, rows // tm), in_specs=in_specs, out_specs=out_specs, out_shape=out_shape,
                          scratch_shapes=[pltpu.VMEM((2, FFN_HALO, tc), F32)],
                          compiler_params=_cp("parallel", "arbitrary"))(x, w_up, w_up, cw, cb)


def _ffn_up_bwd(hf, df, x, w_up, cw, cb, *, name):
    _, rows, f = hf.shape
    d = x.shape[1]
    nq, _, tc = w_up.shape
    nj = nq // 2
    tm = _pick(rows, (512, 256))
    hb = tm // FFN_HALO
    once = pl.Buffered(1)
    ni = rows // tm
    last_blk = rows // FFN_HALO - 1
    ext = tm + 2 * FFN_HALO
    tile = slice(FFN_HALO, FFN_HALO + tm)
    blocks = _col_blocks(tc, MXU_WIDTH)

    def body(h_ref, hp_ref, hn_ref, d_ref, dn_ref, x_ref, wg_ref, wv_ref, cw_ref, cb_ref, dx_ref, dw_out_ref, dcw_ref, dcb_ref,
             dw_ref):
        i = pl.program_id(1)
        first = i == 0
        last = i == ni - 1

        @pl.when(first)
        def _():
            dw_ref[...] = jnp.zeros_like(dw_ref)
            dcw_ref[...] = jnp.zeros_like(dcw_ref)
            dcb_ref[...] = jnp.zeros_like(dcb_ref)

        xt = x_ref[...].astype(BF16).T
        dx = None
        for cs in blocks:
            wc = cs.stop - cs.start
            d_next = dn_ref[:, cs].astype(F32)[0:FFN_HALO]
            de = jnp.concatenate([jnp.zeros((FFN_HALO, wc), F32), d_ref[:, cs].astype(F32), jnp.where(last, 0.0, d_next)], axis=0)
            taps, hc = [], []
            for s in range(2):
                e = jnp.concatenate([jnp.where(first, 0.0, hp_ref[s, :, cs]), h_ref[s, :, cs], hn_ref[s, :, cs]], axis=0)
                y, tp = _conv3(e, cw_ref[s, :, cs], cb_ref[s, :, cs])
                hc.append(y)
                taps.append(tp)
            gl, th = _gelu(hc[0])
            dhc = (de * hc[1] * _gelu_grad(hc[0], th), de * gl)
            for s, w_ref in ((0, wg_ref), (1, wv_ref)):
                w = cw_ref[s, :, cs]
                g = dhc[s]
                dh = (w[2:3, :] * g + w[1:2, :] * pltpu.roll(g, ext - 1, 0) + w[0:1, :] * pltpu.roll(g, ext - 2, 0))[tile]
                gt = g[tile]
                for k in range(3):
                    dcw_ref[s, k:k + 1, cs] += _rowsum(gt * taps[s][k][tile])
                dcb_ref[s, :, cs] += _rowsum(gt)
                dhb = dh.astype(BF16)
                part = lax.dot_general(dhb, w_ref[:, cs], (((1,), (1,)), ((), ())), preferred_element_type=F32)
                dx = part if dx is None else dx + part
                dw_ref[s, :, cs] += jnp.dot(xt, dhb, preferred_element_type=F32)
        dx_ref[...] = dx

        @pl.when(last)
        def _():
            dw_out_ref[...] = dw_ref[...].astype(BF16)

    in_specs = [
        pl.BlockSpec((2, tm, tc), lambda j, i: (0, i, j)),
        pl.BlockSpec((2, FFN_HALO, tc), lambda j, i: (0, jnp.maximum(i * hb - 1, 0), j)),
        pl.BlockSpec((2, FFN_HALO, tc), lambda j, i: (0, jnp.minimum((i + 1) * hb, last_blk), j)),
        pl.BlockSpec((tm, tc), lambda j, i: (i, j)),
        pl.BlockSpec((BF16_ROWS, tc), lambda j, i: (jnp.minimum((i + 1) * (tm // BF16_ROWS), rows // BF16_ROWS - 1), j)),
        pl.BlockSpec((tm, d), lambda j, i: (i, 0)),
        pl.BlockSpec((None, d, tc), lambda j, i: (j, 0, 0), pipeline_mode=once),
        pl.BlockSpec((None, d, tc), lambda j, i: (nj + j, 0, 0), pipeline_mode=once),
        pl.BlockSpec((2, 3, tc), lambda j, i: (0, 0, j)),
        pl.BlockSpec((2, 1, tc), lambda j, i: (0, 0, j)),
    ]
    out_specs = [
        pl.BlockSpec((None, tm, d), lambda j, i: (j, i, 0)),
        pl.BlockSpec((2, None, d, tc), lambda j, i: (0, j, 0, 0), pipeline_mode=once),
        pl.BlockSpec((2, 3, tc), lambda j, i: (0, 0, j)),
        pl.BlockSpec((2, 1, tc), lambda j, i: (0, 0, j)),
    ]
    out_shape = [jax.ShapeDtypeStruct((nj, rows, d), F32), jax.ShapeDtypeStruct((2, nj, d, tc), BF16),
                 jax.ShapeDtypeStruct((2, 3, f), F32), jax.ShapeDtypeStruct((2, 1, f), F32)]
    dx, dw, dcw, dcb = pl.pallas_call(body, name=name, grid=(nj, ni), in_specs=in_specs, out_specs=out_specs,
                                      out_shape=out_shape, scratch_shapes=[pltpu.VMEM((2, d, tc), F32)],
                                      compiler_params=_cp("parallel", "arbitrary"))(
        hf, hf, hf, df, df, x, w_up, w_up, cw, cb)
    return dx, dw.reshape(nq, d, tc), dcw, dcb


def _mixer_fwd(x, w_in, cw, cb, ga, ba, gb, bb, ws, sbb, *, name):
    rows, d = x.shape
    _, _, w = w_in.shape
    t = _pick(rows, (256,))
    groups = w // B_CHUNK

    def body(x_ref, win_ref, cw_ref, cb_ref, ga_ref, ba_ref, gb_ref, bb_ref, ws_ref, sb_ref, h_ref, o_ref, a2_ref, prev_ref):
        @pl.when(pl.program_id(0) == 0)
        def _():
            prev_ref[...] = jnp.zeros_like(prev_ref)

        xb = x_ref[...].astype(BF16)
        for s in range(4):
            h_ref[s] = jnp.dot(xb, win_ref[s], preferred_element_type=F32)
        a1 = h_ref[0] * _sig(h_ref[1])
        e = jnp.concatenate([prev_ref[...], a1], axis=0)
        prev_ref[...] = a1[t - CONV_HALO:]
        acc = cw_ref[A_KERNEL - 1:A_KERNEL, :] * e
        for k in range(A_KERNEL - 1):
            acc = acc + cw_ref[k:k + 1, :] * pltpu.roll(e, A_KERNEL - 1 - k, 0)
        a2 = acc[CONV_HALO:] + cb_ref[...]
        a2_ref[...] = a2
        xh, _ = _ln_stats(a2)
        a3 = xh * ga_ref[...] + ba_ref[...]
        o_ref[:, 0:w] = (a3 * _sig(a3)).astype(BF16)

        u, _ = _gelu(h_ref[2])
        v1, _ = _gelu(h_ref[3])
        xh2, _ = _ln_stats(v1)
        v2 = (xh2 * gb_ref[...] + bb_ref[...]).astype(BF16)
        for c in range(t // B_CHUNK):
            rs = slice(c * B_CHUNK, (c + 1) * B_CHUNK)
            for g in range(groups):
                cs = slice(g * B_CHUNK, (g + 1) * B_CHUNK)
                mixed = jnp.dot(ws_ref[g], v2[rs, cs], preferred_element_type=F32) + sb_ref[g]
                o_ref[rs, w + g * B_CHUNK:w + (g + 1) * B_CHUNK] = (u[rs, cs] * mixed).astype(BF16)

    vec = pl.BlockSpec((1, w), lambda i: (0, 0))
    grp = pl.BlockSpec((groups, B_CHUNK, B_CHUNK), lambda i: (0, 0, 0))
    in_specs = [
        pl.BlockSpec((t, d), lambda i: (i, 0)),
        pl.BlockSpec((4, d, w), lambda i: (0, 0, 0)),
        pl.BlockSpec((A_KERNEL, w), lambda i: (0, 0)),
        vec, vec, vec, vec, vec, grp, grp,
    ]
    out_specs = [pl.BlockSpec((4, t, w), lambda i: (0, i, 0)), pl.BlockSpec((t, 2 * w), lambda i: (i, 0)),
                 pl.BlockSpec((t, w), lambda i: (i, 0))]
    out_shape = [jax.ShapeDtypeStruct((4, rows, w), F32), jax.ShapeDtypeStruct((rows, 2 * w), BF16),
                 jax.ShapeDtypeStruct((rows, w), F32)]
    return pl.pallas_call(body, name=name, grid=(rows // t,), in_specs=in_specs, out_specs=out_specs, out_shape=out_shape,
                          scratch_shapes=[pltpu.VMEM((CONV_HALO, w), F32)],
                          compiler_params=_cp("arbitrary"))(x, w_in, cw, cb, ga, ba, gb, bb, ws, sbb)


def _mixer_bwd(h0, a2, dab, x, w_in, res, res_scale, cw, ga, ba, gb, bb, ws, wst, sbb, tril, *, name):
    _, rows, w = h0.shape
    d = x.shape[1]
    once = pl.Buffered(1)
    t = _pick(rows, (256,))
    hb = t // CONV_HALO
    ni = rows // t
    last_blk = rows // CONV_HALO - 1
    ext = t + CONV_HALO
    tile = slice(0, t)
    groups = w // B_CHUNK
    taps = A_KERNEL - 1

    def body(h_ref, a2_ref, a2n_ref, d_ref, dn_ref, x_ref, win_ref, res_ref, cw_ref, ga_ref, ba_ref, gb_ref, bb_ref,
             ws_ref, wst_ref, sb_ref, tril_ref, dx_ref, dwin_ref, dcw_ref, dcb_ref, dga_ref, dba_ref, dgb_ref, dbb_ref,
             dws_ref, dsb_ref, dw_ref):
        i = pl.program_id(0)
        first = i == 0
        last = i == ni - 1

        @pl.when(first)
        def _():
            for r in (dw_ref, dcw_ref, dcb_ref, dga_ref, dba_ref, dgb_ref, dbb_ref, dws_ref, dsb_ref):
                r[...] = jnp.zeros_like(r)

        xt = x_ref[...].astype(BF16).T
        dx_terms = []

        def through_w_in(slot, dh):
            dhb = dh.astype(BF16)
            dx_terms.append(lax.dot_general(dhb, win_ref[slot], (((1,), (1,)), ((), ())), preferred_element_type=F32))
            dw_ref[slot] += jnp.dot(xt, dhb, preferred_element_type=F32)

        xh, rstd = _ln_stats(jnp.concatenate([a2_ref[...], a2n_ref[...]], axis=0))
        a3 = xh * ga_ref[...] + ba_ref[...]
        s3 = _sig(a3)
        da_e = jnp.concatenate([d_ref[:, 0:w], jnp.where(last, 0.0, dn_ref[...])], axis=0)
        da3 = da_e * (s3 * (1.0 + a3 * (1.0 - s3)))
        da2 = _ln_bwd(da3 * ga_ref[...], xh, rstd)
        dga_ref[...] += _rowsum(da3[tile] * xh[tile])
        dba_ref[...] += _rowsum(da3[tile])
        dcb_ref[...] += _rowsum(da2[tile])
        sgt = _sig(h_ref[1])
        a1t = h_ref[0] * sgt
        da1t = None
        for k in range(A_KERNEL):
            sh = taps - k
            fed = (da2 if sh == 0 else pltpu.roll(da2, ext - sh, 0))[tile]
            dcw_ref[k:k + 1, :] += _rowsum(a1t * fed)
            term = cw_ref[k:k + 1, :] * fed
            da1t = term if da1t is None else da1t + term
        through_w_in(0, da1t * sgt)
        through_w_in(1, da1t * h_ref[0] * sgt * (1.0 - sgt))

        bu = h_ref[2]
        bv = h_ref[3]
        u, tu = _gelu(bu)
        v1, tv = _gelu(bv)
        xh2, rstd2 = _ln_stats(v1)
        v2 = (xh2 * gb_ref[...] + bb_ref[...]).astype(BF16)
        db = d_ref[:, w:2 * w]
        dmx_all = db * u
        du_parts, dv2_parts = [], []
        for c in range(t // B_CHUNK):
            rs = slice(c * B_CHUNK, (c + 1) * B_CHUNK)
            du_row, dv2_row = [], []
            for g in range(groups):
                cs = slice(g * B_CHUNK, (g + 1) * B_CHUNK)
                v2cg = v2[rs, cs]
                mixed = jnp.dot(ws_ref[g], v2cg, preferred_element_type=F32) + sb_ref[g]
                dmx = dmx_all[rs, cs]
                dmxb = dmx.astype(BF16)
                du_row.append(db[rs, cs] * mixed)
                dv2_row.append(jnp.dot(wst_ref[g], dmxb, preferred_element_type=F32))
                dws_ref[g] += tril_ref[...] * lax.dot_general(dmxb, v2cg, (((1,), (1,)), ((), ())),
                                                               preferred_element_type=F32)
                dsb_ref[g:g + 1, :] += _rowsum(dmx.T)
            du_parts.append(jnp.concatenate(du_row, axis=1))
            dv2_parts.append(jnp.concatenate(dv2_row, axis=1))
        du = jnp.concatenate(du_parts, axis=0)
        dv2 = jnp.concatenate(dv2_parts, axis=0)
        dgb_ref[...] += _rowsum(dv2 * xh2)
        dbb_ref[...] += _rowsum(dv2)
        dv1 = _ln_bwd(dv2 * gb_ref[...], xh2, rstd2)
        through_w_in(2, du * _gelu_grad(bu, tu))
        through_w_in(3, dv1 * _gelu_grad(bv, tv))
        dx_ref[...] = res_scale * res_ref[...] + ((dx_terms[0] + dx_terms[1]) + (dx_terms[2] + dx_terms[3]))

        @pl.when(last)
        def _():
            dwin_ref[...] = dw_ref[...].astype(BF16)

    vec = pl.BlockSpec((1, w), lambda i: (0, 0))
    grp = pl.BlockSpec((groups, B_CHUNK, B_CHUNK), lambda i: (0, 0, 0))
    halo = pl.BlockSpec((CONV_HALO, w), lambda i: (jnp.minimum((i + 1) * hb, last_blk), 0))
    wide = pl.BlockSpec((t, d), lambda i: (i, 0))
    in_specs = [
        pl.BlockSpec((4, t, w), lambda i: (0, i, 0)),
        pl.BlockSpec((t, w), lambda i: (i, 0)),
        halo,
        pl.BlockSpec((t, 2 * w), lambda i: (i, 0)),
        halo,
        wide,
        pl.BlockSpec((4, d, w), lambda i: (0, 0, 0), pipeline_mode=once),
        wide,
        pl.BlockSpec((A_KERNEL, w), lambda i: (0, 0)),
        vec, vec, vec, vec, grp, grp, grp,
        pl.BlockSpec((B_CHUNK, B_CHUNK), lambda i: (0, 0)),
    ]
    vsds = jax.ShapeDtypeStruct((1, w), F32)
    out_specs = [
        wide,
        pl.BlockSpec((4, d, w), lambda i: (0, 0, 0), pipeline_mode=once),
        pl.BlockSpec((A_KERNEL, w), lambda i: (0, 0)),
        vec, vec, vec, vec, vec, grp,
        pl.BlockSpec((groups, B_CHUNK), lambda i: (0, 0)),
    ]
    out_shape = [jax.ShapeDtypeStruct((rows, d), F32), jax.ShapeDtypeStruct((4, d, w), BF16),
                 jax.ShapeDtypeStruct((A_KERNEL, w), F32),
                 vsds, vsds, vsds, vsds, vsds, jax.ShapeDtypeStruct((groups, B_CHUNK, B_CHUNK), F32),
                 jax.ShapeDtypeStruct((groups, B_CHUNK), F32)]
    return pl.pallas_call(body, name=name, grid=(ni,), in_specs=in_specs, out_specs=out_specs, out_shape=out_shape,
                          scratch_shapes=[pltpu.VMEM((4, d, w), F32)], compiler_params=_cp("arbitrary"))(
        h0, a2, a2, dab, dab, x, w_in, res, cw, ga, ba, gb, bb, ws, wst, sbb, tril)


GROUP_ROWS = Q_PER_KV * ATT_BLOCK


def _attn_mask(n):
    qi = lax.broadcasted_iota(jnp.int32, (GROUP_ROWS, 2 * ATT_BLOCK), 0) & (ATT_BLOCK - 1)
    sj = lax.broadcasted_iota(jnp.int32, (GROUP_ROWS, 2 * ATT_BLOCK), 1)
    diff = qi + ATT_BLOCK - sj
    return (diff >= 0) & (diff < ATT_BLOCK) & ((n > 0) | (sj >= ATT_BLOCK))


def _stack_heads(ref, kvh, dtype):
    heads = [ref[:, (kvh * Q_PER_KV + g) * HEAD_DIM:(kvh * Q_PER_KV + g + 1) * HEAD_DIM] for g in range(Q_PER_KV)]
    return jnp.concatenate(heads, axis=0).astype(dtype)


def _per_row_sink(sink_ref, kvh):
    head = lax.broadcasted_iota(jnp.int32, (GROUP_ROWS, 1), 0) // ATT_BLOCK
    out = jnp.zeros((GROUP_ROWS, 1), F32)
    for g in range(Q_PER_KV):
        out = jnp.where(head == g, sink_ref[kvh * Q_PER_KV + g], out)
    return out


def _attn_specs(rows, n_q):
    dq = n_q * HEAD_DIM
    dkv = 2 * (n_q // Q_PER_KV) * HEAD_DIM
    kv_blk = dq // dkv
    assert dq % dkv == 0
    return dq, dkv, [
        pl.BlockSpec(memory_space=pltpu.SMEM),
        pl.BlockSpec((ATT_BLOCK, dq), lambda n: (n, 0)),
        pl.BlockSpec((ATT_BLOCK, dkv), lambda n: (n, kv_blk)),
        pl.BlockSpec((ATT_BLOCK, dkv), lambda n: (jnp.maximum(n - 1, 0), kv_blk)),
    ]


def _kv_pair(kvc_ref, kvp_ref, kvh, n_kv):
    ks = slice(kvh * HEAD_DIM, (kvh + 1) * HEAD_DIM)
    vs = slice((n_kv + kvh) * HEAD_DIM, (n_kv + kvh + 1) * HEAD_DIM)
    kk = jnp.concatenate([kvp_ref[:, ks], kvc_ref[:, ks]], axis=0).astype(BF16)
    vv = jnp.concatenate([kvp_ref[:, vs], kvc_ref[:, vs]], axis=0).astype(BF16)
    return kk, vv


def _attn_fwd(qkv, sinks, *, name):
    rows = qkv.shape[0]
    n_q = sinks.shape[0]
    n_kv = n_q // Q_PER_KV
    scale = 1.0 / math.sqrt(HEAD_DIM)
    dq, _, in_specs = _attn_specs(rows, n_q)

    def body(sink_ref, q_ref, kvc_ref, kvp_ref, o_ref, lse_ref):
        valid = _attn_mask(pl.program_id(0))
        for kvh in range(n_kv):
            kk, vv = _kv_pair(kvc_ref, kvp_ref, kvh, n_kv)
            qs = _stack_heads(q_ref, kvh, BF16)
            s = lax.dot_general(qs, kk, (((1,), (1,)), ((), ())), preferred_element_type=F32)
            s = jnp.where(valid, s * scale, -jnp.inf)
            sk = _per_row_sink(sink_ref, kvh)
            m = jnp.maximum(jnp.max(s, axis=1, keepdims=True), sk)
            p = jnp.exp(s - m)
            l = jnp.sum(p, axis=1, keepdims=True) + jnp.exp(sk - m)
            o = jnp.dot((p / l).astype(BF16), vv, preferred_element_type=F32)
            lse = m + jnp.log(l)
            for g in range(Q_PER_KV):
                h = kvh * Q_PER_KV + g
                rs = slice(g * ATT_BLOCK, (g + 1) * ATT_BLOCK)
                o_ref[:, h * HEAD_DIM:(h + 1) * HEAD_DIM] = o[rs]
                lse_ref[:, h:h + 1] = lse[rs]

    out_specs = [pl.BlockSpec((ATT_BLOCK, dq), lambda n: (n, 0)), pl.BlockSpec((ATT_BLOCK, n_q), lambda n: (n, 0))]
    out_shape = [jax.ShapeDtypeStruct((rows, dq), F32), jax.ShapeDtypeStruct((rows, n_q), F32)]
    return pl.pallas_call(body, name=name, grid=(rows // ATT_BLOCK,), in_specs=in_specs, out_specs=out_specs,
                          out_shape=out_shape, compiler_params=_cp("parallel"))(sinks, qkv, qkv, qkv)


def _attn_bwd(qkv, dout, lse, sinks, *, name):
    rows = qkv.shape[0]
    n_q = sinks.shape[0]
    n_kv = n_q // Q_PER_KV
    scale = 1.0 / math.sqrt(HEAD_DIM)
    dq_w, dkv_w, in_specs = _attn_specs(rows, n_q)
    blk_q = pl.BlockSpec((ATT_BLOCK, dq_w), lambda n: (n, 0))
    blk_kv = pl.BlockSpec((ATT_BLOCK, dkv_w), lambda n: (n, 0))
    in_specs = in_specs + [blk_q, pl.BlockSpec((ATT_BLOCK, n_q), lambda n: (n, 0))]

    def body(sink_ref, q_ref, kvc_ref, kvp_ref, do_ref, lse_ref, dq_ref, dkc_ref, dkp_ref, dsink_ref):
        n = pl.program_id(0)

        @pl.when(n == 0)
        def _():
            dsink_ref[...] = jnp.zeros_like(dsink_ref)

        valid = _attn_mask(n)
        head_ids = lax.broadcasted_iota(jnp.int32, (1, n_q), 1)
        dsink = jnp.zeros((1, n_q), F32)
        for kvh in range(n_kv):
            kk, vv = _kv_pair(kvc_ref, kvp_ref, kvh, n_kv)
            qs = _stack_heads(q_ref, kvh, BF16)
            dos = _stack_heads(do_ref, kvh, BF16)
            lse = jnp.concatenate([lse_ref[:, kvh * Q_PER_KV + g:kvh * Q_PER_KV + g + 1] for g in range(Q_PER_KV)], axis=0)
            s = lax.dot_general(qs, kk, (((1,), (1,)), ((), ())), preferred_element_type=F32)
            s = jnp.where(valid, s * scale, -jnp.inf)
            p = jnp.exp(s - lse)
            dp = lax.dot_general(dos, vv, (((1,), (1,)), ((), ())), preferred_element_type=F32)
            delta = jnp.sum(p * dp, axis=1, keepdims=True)
            ds = (p * (dp - delta) * scale).astype(BF16)
            sink_term = jnp.exp(_per_row_sink(sink_ref, kvh) - lse) * delta
            dqs = jnp.dot(ds, kk, preferred_element_type=F32)
            for g in range(Q_PER_KV):
                h = kvh * Q_PER_KV + g
                rs = slice(g * ATT_BLOCK, (g + 1) * ATT_BLOCK)
                dsink = dsink + jnp.where(head_ids == h, -jnp.sum(sink_term[rs]), 0.0)
                dq_ref[:, h * HEAD_DIM:(h + 1) * HEAD_DIM] = dqs[rs]
            dk = lax.dot_general(ds, qs, (((0,), (0,)), ((), ())), preferred_element_type=F32)
            dv = lax.dot_general(p.astype(BF16), dos, (((0,), (0,)), ((), ())), preferred_element_type=F32)
            ks = slice(kvh * HEAD_DIM, (kvh + 1) * HEAD_DIM)
            vs = slice((n_kv + kvh) * HEAD_DIM, (n_kv + kvh + 1) * HEAD_DIM)
            dkp_ref[:, ks] = dk[0:ATT_BLOCK]
            dkc_ref[:, ks] = dk[ATT_BLOCK:]
            dkp_ref[:, vs] = dv[0:ATT_BLOCK]
            dkc_ref[:, vs] = dv[ATT_BLOCK:]
        dsink_ref[...] += dsink

    out_specs = [blk_q, blk_kv, blk_kv, pl.BlockSpec((1, n_q), lambda n: (0, 0))]
    out_shape = [jax.ShapeDtypeStruct((rows, dq_w), F32), jax.ShapeDtypeStruct((rows, dkv_w), F32),
                 jax.ShapeDtypeStruct((rows, dkv_w), F32), jax.ShapeDtypeStruct((1, n_q), F32)]
    return pl.pallas_call(body, name=name, grid=(rows // ATT_BLOCK,), in_specs=in_specs, out_specs=out_specs,
                          out_shape=out_shape, compiler_params=_cp("arbitrary"))(sinks, qkv, qkv, qkv, dout, lse)


def _dqkv_assemble(dq, dkc, dkp, *, name):
    rows, dq_w = dq.shape
    dkv_w = dkc.shape[1]
    nb = rows // ATT_BLOCK

    def body(dq_ref, dkc_ref, dkp_ref, o_ref, db_ref):
        n = pl.program_id(0)

        @pl.when(n == 0)
        def _():
            db_ref[...] = jnp.zeros_like(db_ref)

        dqv = dq_ref[...]
        dkv = dkc_ref[...] + jnp.where(n == nb - 1, 0.0, dkp_ref[...])
        o_ref[:, 0:dq_w] = dqv.astype(BF16)
        o_ref[:, dq_w:dq_w + dkv_w] = dkv.astype(BF16)
        db_ref[:, 0:dq_w] += _rowsum(dqv)
        db_ref[:, dq_w:dq_w + dkv_w] += _rowsum(dkv)

    width = dq_w + dkv_w
    in_specs = [pl.BlockSpec((ATT_BLOCK, dq_w), lambda n: (n, 0)), pl.BlockSpec((ATT_BLOCK, dkv_w), lambda n: (n, 0)),
                pl.BlockSpec((ATT_BLOCK, dkv_w), lambda n: (jnp.minimum(n + 1, nb - 1), 0))]
    out_specs = [pl.BlockSpec((ATT_BLOCK, width), lambda n: (n, 0)), pl.BlockSpec((1, width), lambda n: (0, 0))]
    out_shape = [jax.ShapeDtypeStruct((rows, width), BF16), jax.ShapeDtypeStruct((1, width), F32)]
    return pl.pallas_call(body, name=name, grid=(nb,), in_specs=in_specs, out_specs=out_specs, out_shape=out_shape,
                          compiler_params=_cp("arbitrary"))(dq, dkc, dkp)


def _row_tile(r, c):
    budget = 2 * 1024 * 1024 // (4 * c)
    for cand in (1024, 512, 256, 128, 64, 32, 16):
        if cand <= budget and r % cand == 0:
            return cand
    return r


def _octo_sum(own, recv, place, dest, lead, *, name):
    _, _, r, c = own.shape
    t = _row_tile(r, c)
    lead_idx, buf_shape = lead

    def body(place_ref, own_ref, *rest):
        o_ref = rest[7] if dest is None else rest[8]
        acc = own_ref[...].astype(F32)
        for k in range(7):
            acc = acc + rest[k][...].astype(F32)
        o_ref[...] = acc

    def peer(mask):
        return pl.BlockSpec((None, t, c), lambda i, pr: (pr[2] ^ mask, i, 0))

    if lead_idx is None:
        o_spec = pl.BlockSpec((None, t, c), lambda i, pr: (pr[1], i, 0))
    else:
        o_spec = pl.BlockSpec((None, None, t, c), lambda i, pr: (lead_idx, pr[1], i, 0))
    in_specs = [pl.BlockSpec((None, None, t, c), lambda i, pr: (pr[0], pr[1], i, 0))] + [peer(m) for m in range(1, 8)]
    args = [place, own] + [recv] * 7
    aliases = {}
    if dest is not None:
        in_specs.append(HBM)
        args.append(dest)
        aliases = {9: 0}
    grid_spec = pltpu.PrefetchScalarGridSpec(num_scalar_prefetch=1, grid=(r // t,), in_specs=in_specs, out_specs=o_spec)
    return pl.pallas_call(body, name=name, grid_spec=grid_spec, out_shape=jax.ShapeDtypeStruct(buf_shape, F32),
                          input_output_aliases=aliases, compiler_params=_cp("parallel"))(*args)


def _adamw_math(w, g, m, v):
    nm = ADAM_B1 * m + (1.0 - ADAM_B1) * g
    nv = ADAM_B2 * v + (1.0 - ADAM_B2) * (g * g)
    m_hat = nm / (1.0 - ADAM_B1 ** ADAM_STEP)
    v_hat = nv / (1.0 - ADAM_B2 ** ADAM_STEP)
    return -ADAM_LR * (m_hat / (jnp.sqrt(v_hat) + ADAM_EPS) + ADAM_WD * w), nm, nv


def _adamw(w, g, m, v, *, name):
    r, c = w.shape
    t = _row_tile(r, c)

    def body(w_ref, g_ref, m_ref, v_ref, d_ref, nm_ref, nv_ref, go_ref):
        gv = g_ref[...]
        d_ref[...], nm_ref[...], nv_ref[...] = _adamw_math(w_ref[...], gv, m_ref[...], v_ref[...])
        go_ref[...] = gv

    blk = pl.BlockSpec((t, c), lambda i: (i, 0))
    sds = jax.ShapeDtypeStruct((r, c), F32)
    return pl.pallas_call(body, name=name, grid=(r // t,), in_specs=[blk] * 4, out_specs=[blk] * 4,
                          out_shape=[sds] * 4, compiler_params=_cp("parallel"))(w, g, m, v)


HBM = pl.BlockSpec(memory_space=pl.ANY)


def _place():
    x, y, c = lax.axis_index("x"), lax.axis_index("y"), lax.axis_index("c")
    chips = [(1 - x, y), (x, 1 - y), (1 - x, 1 - y)]
    return x, y, c, 2 * x + y, (x, y, 1 - c), chips


def _rcopy(src, dst, ssem, rsem, dev):
    return pltpu.make_async_remote_copy(src_ref=src, dst_ref=dst, send_sem=ssem, recv_sem=rsem, device_id=dev,
                                        device_id_type=MESH)


HBM_ONLY = pl.BlockSpec(memory_space=pltpu.HBM)
SEM = pl.BlockSpec(memory_space=pltpu.SEMAPHORE)


def _peers():
    x, y, c = lax.axis_index("x"), lax.axis_index("y"), lax.axis_index("c")
    out = []
    for mask in range(1, 8):
        px = 1 - x if mask & 4 else x
        py = 1 - y if mask & 2 else y
        pc = 1 - c if mask & 1 else c
        out.append(((px, py, pc), 2 * px + py, pc, 4 * px + 2 * py + pc))
    return 4 * x + 2 * y + c, out


def _reduce_start(grads, lands, after, *, name):
    nt = len(grads)

    def body(*refs):
        ssems, rsems = refs[2 * nt + 1:3 * nt + 1], refs[3 * nt + 1:4 * nt + 1]
        g_out, l_out, token = refs[4 * nt + 1:5 * nt + 1], refs[5 * nt + 1:6 * nt + 1], refs[6 * nt + 1]
        me, peers = _peers()
        for t in range(nt):
            for k, (dev, chip, core, _) in enumerate(peers):
                _rcopy(g_out[t].at[chip, core], l_out[t].at[me], ssems[t].at[k], rsems[t].at[k], dev).start()
        token[...] = jnp.zeros_like(token)

    sems = [pltpu.SemaphoreType.DMA((7,))] * (2 * nt)
    out_shape = (sems + [pltpu.HBM(g.shape, g.dtype) for g in grads] + [pltpu.HBM(l.shape, l.dtype) for l in lands]
                 + [jax.ShapeDtypeStruct((8, LANES), F32)])
    res = pl.pallas_call(
        body, name=name, in_specs=[HBM_ONLY] * (2 * nt + 1),
        out_specs=[SEM] * (2 * nt) + [HBM_ONLY] * (2 * nt) + [pl.BlockSpec(memory_space=pltpu.VMEM)], out_shape=out_shape,
        input_output_aliases={t: 2 * nt + t for t in range(2 * nt)},
        compiler_params=pltpu.CompilerParams(has_side_effects=DATAFLOW),
    )(*[pltpu.with_memory_space_constraint(a, pltpu.HBM) for a in list(grads) + list(lands) + [after]])
    return res[:nt], res[nt:2 * nt], res[2 * nt:3 * nt], res[3 * nt:4 * nt], res[4 * nt]


def _reduce_wait(grads, lands, ssems, rsems, after, *, name):
    nt = len(grads)

    def body(*refs):
        ssem_refs, rsem_refs = refs[2 * nt:3 * nt], refs[3 * nt:4 * nt]
        g_out, l_out = refs[4 * nt + 1:5 * nt + 1], refs[5 * nt + 1:6 * nt + 1]
        me, peers = _peers()
        for t in range(nt):
            for k, (dev, chip, core, _) in enumerate(peers):
                _rcopy(g_out[t].at[chip, core], l_out[t].at[me], ssem_refs[t].at[k], rsem_refs[t].at[k], dev).wait_send()
        for t in range(nt):
            for k, (dev, _, _, idx) in enumerate(peers):
                slot = l_out[t].at[idx]
                _rcopy(slot, slot, ssem_refs[t].at[k], rsem_refs[t].at[k], dev).wait_recv()

    res = pl.pallas_call(
        body, name=name, in_specs=[HBM_ONLY] * (2 * nt) + [SEM] * (2 * nt) + [HBM_ONLY], out_specs=[HBM_ONLY] * (2 * nt),
        out_shape=[pltpu.HBM(a.shape, a.dtype) for a in list(grads) + list(lands)],
        input_output_aliases={t: t for t in range(2 * nt)},
        compiler_params=pltpu.CompilerParams(has_side_effects=DATAFLOW),
    )(*grads, *lands, *ssems, *rsems, pltpu.with_memory_space_constraint(after, pltpu.HBM))
    return list(res[:nt]), list(res[nt:])
DATAFLOW = pltpu.SideEffectType.DATAFLOW_SIDE_EFFECTING


def _gather_now(bufs, *, name):
    nt = len(bufs)

    def body(*refs):
        outs = refs[nt:2 * nt]
        ssem, rsem = refs[2 * nt:]
        x, y, c, q, sib, chips = _place()
        sends = []
        for t in range(nt):
            for j, (px, py) in enumerate(chips):
                mine = outs[t].at[q, c]
                cp = _rcopy(mine, mine, ssem.at[t, j], rsem.at[t, j], (px, py, c))
                cp.start()
                sends.append(cp)
        for t in range(nt):
            for j, (px, py) in enumerate(chips):
                landed = outs[t].at[2 * px + py, c]
                _rcopy(landed, landed, ssem.at[t, j], rsem.at[t, j], (px, py, c)).wait_recv()
                cp = _rcopy(landed, landed, ssem.at[t, 3 + j], rsem.at[t, 3 + j], sib)
                cp.start()
                sends.append(cp)
        for t in range(nt):
            for j, (px, py) in enumerate(chips):
                passed = outs[t].at[2 * px + py, 1 - c]
                _rcopy(passed, passed, ssem.at[t, 3 + j], rsem.at[t, 3 + j], sib).wait_recv()
        for cp in sends:
            cp.wait_send()

    out_shape = [jax.ShapeDtypeStruct(b.shape, b.dtype) for b in bufs]
    return pl.pallas_call(
        body, name=name, in_specs=[HBM] * nt, out_specs=[HBM] * nt, out_shape=out_shape,
        input_output_aliases={t: t for t in range(nt)},
        scratch_shapes=[pltpu.SemaphoreType.DMA((nt, 6)), pltpu.SemaphoreType.DMA((nt, 6))],
    )(*bufs)


def _gather_start(bufs, half, after, *, name):
    nt = len(bufs)

    def body(*refs):
        ssems, rsems, outs = refs[nt + 1:2 * nt + 1], refs[2 * nt + 1:3 * nt + 1], refs[3 * nt + 1:4 * nt + 1]
        x, y, c, q, sib, chips = _place()
        for t in range(nt):
            for j, (px, py) in enumerate(chips):
                mine = outs[t].at[q, c] if half[t] else outs[t].at[q]
                _rcopy(mine, mine, ssems[t].at[j], rsems[t].at[j], (px, py, c)).start()

    sems = [pltpu.SemaphoreType.DMA((3,))] * (2 * nt)
    out_shape = sems + [pltpu.HBM(b.shape, b.dtype) for b in bufs]
    res = pl.pallas_call(
        body, name=name, in_specs=[HBM_ONLY] * (nt + 1), out_specs=[SEM] * (2 * nt) + [HBM_ONLY] * nt, out_shape=out_shape,
        input_output_aliases={t: 2 * nt + t for t in range(nt)},
        compiler_params=pltpu.CompilerParams(has_side_effects=DATAFLOW),
    )(*[pltpu.with_memory_space_constraint(b, pltpu.HBM) for b in list(bufs) + [after]])
    return res[:nt], res[nt:2 * nt], res[2 * nt:]


def _gather_wait(bufs, half, ssems, rsems, after, *, name):
    nt = len(bufs)

    def body(*refs):
        ssem_refs, rsem_refs = refs[nt:2 * nt], refs[2 * nt:3 * nt]
        outs = refs[3 * nt + 1:]
        x, y, c, q, sib, chips = _place()
        for t in range(nt):
            for j, (px, py) in enumerate(chips):
                mine = outs[t].at[q, c] if half[t] else outs[t].at[q]
                _rcopy(mine, mine, ssem_refs[t].at[j], rsem_refs[t].at[j], (px, py, c)).wait_send()
        for t in range(nt):
            for j, (px, py) in enumerate(chips):
                theirs = outs[t].at[2 * px + py, c] if half[t] else outs[t].at[2 * px + py]
                _rcopy(theirs, theirs, ssem_refs[t].at[j], rsem_refs[t].at[j], (px, py, c)).wait_recv()

    res = pl.pallas_call(
        body, name=name, in_specs=[HBM_ONLY] * nt + [SEM] * (2 * nt) + [HBM], out_specs=[HBM_ONLY] * nt,
        out_shape=[pltpu.HBM(b.shape, b.dtype) for b in bufs], input_output_aliases={t: t for t in range(nt)},
        compiler_params=pltpu.CompilerParams(has_side_effects=DATAFLOW),
    )(*bufs, *ssems, *rsems, after)
    return list(res)


def _sibling_swap(bufs, *, name):
    nt = len(bufs)

    def body(*refs):
        outs = refs[nt:2 * nt]
        ssem, rsem = refs[2 * nt:]
        x, y, c, q, sib, chips = _place()
        sends = []
        for t in range(nt):
            for j, (px, py) in enumerate(chips):
                held = outs[t].at[2 * px + py, c]
                cp = _rcopy(held, held, ssem.at[t, j], rsem.at[t, j], sib)
                cp.start()
                sends.append(cp)
        for t in range(nt):
            for j, (px, py) in enumerate(chips):
                other = outs[t].at[2 * px + py, 1 - c]
                _rcopy(other, other, ssem.at[t, j], rsem.at[t, j], sib).wait_recv()
        for cp in sends:
            cp.wait_send()

    return pl.pallas_call(
        body, name=name, in_specs=[HBM] * nt, out_specs=[HBM] * nt,
        out_shape=[jax.ShapeDtypeStruct(b.shape, b.dtype) for b in bufs], input_output_aliases={t: t for t in range(nt)},
        scratch_shapes=[pltpu.SemaphoreType.DMA((nt, 3)), pltpu.SemaphoreType.DMA((nt, 3))],
    )(*bufs)


def _sibling_share(bufs, layout, *, name):
    no = len(bufs)
    nt = len(layout)

    def body(*refs):
        outs = refs[no:2 * no]
        ssem, rsem = refs[2 * no:]
        x, y, c, q, sib, chips = _place()

        def slot(t, half):
            o, lead = layout[t]
            return outs[o].at[half] if lead is None else outs[o].at[lead, half]

        sends = []
        for t in range(nt):
            cp = _rcopy(slot(t, c), slot(t, c), ssem.at[t], rsem.at[t], sib)
            cp.start()
            sends.append(cp)
        for t in range(nt):
            _rcopy(slot(t, 1 - c), slot(t, 1 - c), ssem.at[t], rsem.at[t], sib).wait_recv()
        for cp in sends:
            cp.wait_send()

    out_shape = [jax.ShapeDtypeStruct(b.shape, b.dtype) for b in bufs]
    return pl.pallas_call(
        body, name=name, in_specs=[HBM] * no, out_specs=[HBM] * no, out_shape=out_shape,
        input_output_aliases={o: o for o in range(no)},
        scratch_shapes=[pltpu.SemaphoreType.DMA((nt,)), pltpu.SemaphoreType.DMA((nt,))],
    )(*bufs)


def _small_tail(local, params, *, name):
    (cwa, cba, ga, ba, gb, bb, dws, dsb, dbq, dsk, cwf0, cbf0, cwf1, cbf1,
     g00, g01, g10, g11, b00, b01, b10, b11, err) = local
    n_local = len(local)
    kw, wa = cwa.shape
    ng = dws.shape[0]
    nqkv = dbq.shape[1]
    nsk = dsk.shape[1]
    f = cwf0.shape[2]
    dm = err.shape[1]
    row_vec = 8 * (-(-kw // 8))
    shapes = [(row_vec + 8, wa), (ng * B_CHUNK + 8, B_CHUNK), (8, nqkv), (2, 2, 8, f), (16, dm)]
    n_grp = len(shapes)
    flat_params = [a for triple in params for a in triple]
    n_par = len(params)

    def reduce_body(*refs):
        loc = refs[:n_local]
        tot = refs[n_local:n_local + n_grp]
        scr = refs[n_local + n_grp:]
        grp, from_sib, pair, gath = (scr[k * n_grp:(k + 1) * n_grp] for k in range(4))
        ssem1, rsem1, ssem2, rsem2 = scr[4 * n_grp:]
        x, y, core, q, sib, chips = _place()

        for gr in grp:
            gr[...] = jnp.zeros_like(gr)
        a, b, c, dd, e = grp
        a[0:kw, :] = loc[0][...]
        for k in range(5):
            a[row_vec + k:row_vec + k + 1, :] = loc[1 + k][...]
        for g in range(ng):
            b[g * B_CHUNK:(g + 1) * B_CHUNK, :] = loc[6][g]
        b[ng * B_CHUNK:ng * B_CHUNK + ng, :] = loc[7][...]
        c[0:1, :] = loc[8][...]
        c[1:2, 0:nsk] = loc[9][...]
        for l in range(2):
            for s in range(2):
                dd[l, s, 0:3, :] = loc[10 + 2 * l][s]
                dd[l, s, 3:4, :] = loc[11 + 2 * l][s]
        for k in range(9):
            e[k:k + 1, :] = loc[14 + k][...]

        sends = []
        for gi in range(n_grp):
            cp = _rcopy(grp[gi], from_sib[gi], ssem1.at[gi], rsem1.at[gi], sib)
            cp.start()
            sends.append(cp)
        for gi in range(n_grp):
            _rcopy(grp[gi], from_sib[gi], ssem1.at[gi], rsem1.at[gi], sib).wait_recv()
            both = grp[gi][...] + from_sib[gi][...]
            pair[gi][...] = both
            gath[gi][q] = both
            for j, (px, py) in enumerate(chips):
                cp = _rcopy(pair[gi], gath[gi].at[q], ssem2.at[gi, j], rsem2.at[gi, j], (px, py, core))
                cp.start()
                sends.append(cp)
        for gi in range(n_grp):
            for j, (px, py) in enumerate(chips):
                slot = gath[gi].at[2 * px + py]
                _rcopy(slot, slot, ssem2.at[gi, j], rsem2.at[gi, j], (px, py, core)).wait_recv()
            acc = gath[gi][0]
            for k in range(1, 4):
                acc = acc + gath[gi][k]
            tot[gi][...] = acc
        for cp in sends:
            cp.wait_send()

    vm = pl.BlockSpec(memory_space=pltpu.VMEM)
    scratch = ([pltpu.VMEM(s, F32) for s in shapes] * 3 + [pltpu.VMEM((4,) + s, F32) for s in shapes]
               + [pltpu.SemaphoreType.DMA((n_grp,)), pltpu.SemaphoreType.DMA((n_grp,)),
                  pltpu.SemaphoreType.DMA((n_grp, 3)), pltpu.SemaphoreType.DMA((n_grp, 3))])
    totals = pl.pallas_call(
        reduce_body, name=name + "_reduce", in_specs=[vm] * n_local, out_specs=[vm] * n_grp,
        out_shape=[jax.ShapeDtypeStruct(s, F32) for s in shapes], scratch_shapes=scratch,
        compiler_params=pltpu.CompilerParams(vmem_limit_bytes=VMEM_LIMIT),
    )(*local)

    def adamw_body(*refs):
        ta, tb, tc, td, te = refs[:n_grp]
        par = refs[n_grp:n_grp + 3 * n_par]
        outs = refs[n_grp + 3 * n_par:n_grp + 7 * n_par]
        loss_ref = refs[n_grp + 7 * n_par]
        q = 2 * lax.axis_index("x") + lax.axis_index("y")

        def mine(piece):
            out = piece(0)
            for k in range(1, 4):
                out = jnp.where(q == k, piece(k), out)
            return out

        def update(p, grad, index=None):
            at = (lambda r: r[...]) if index is None else (lambda r: r[index])
            w_ref, m_ref, v_ref = par[3 * p:3 * p + 3]
            g_ref, d_ref, nm_ref, nv_ref = outs[4 * p:4 * p + 4]
            delta, nm, nv = _adamw_math(at(w_ref), grad, at(m_ref), at(v_ref))
            for r, val in ((g_ref, grad), (d_ref, delta), (nm_ref, nm), (nv_ref, nv)):
                if index is None:
                    r[...] = val
                else:
                    r[index] = val

        wq = wa // 4
        update(0, mine(lambda k: ta[0:kw, k * wq:(k + 1) * wq]), (0,))
        for k in range(5):
            update(1 + k, ta[row_vec + k:row_vec + k + 1, :])
        for g in range(ng):
            update(6, tb[g * B_CHUNK:(g + 1) * B_CHUNK, :], (0, g))
        update(7, tb[ng * B_CHUNK:ng * B_CHUNK + ng, :], (0,))
        nq4 = nqkv // 4
        update(8, mine(lambda k: tc[0:1, k * nq4:(k + 1) * nq4]))
        update(9, tc[1:2, 0:nsk])
        fh = f // 2
        for l in range(2):
            update(10, mine(lambda k: td[l, k // 2, 0:3, (k % 2) * fh:(k % 2 + 1) * fh]), (l,))
            update(11, jnp.concatenate([td[l, 0, 3:4, :], td[l, 1, 3:4, :]], axis=1), (slice(l, l + 1),))
        dq4 = dm // 4
        for i in range(2):
            for j in range(2):
                for p, base in ((12, 0), (13, 4)):
                    row = base + 2 * i + j
                    update(p, mine(lambda k: te[row:row + 1, k * dq4:(k + 1) * dq4]), (i, slice(j, j + 1)))
        loss_ref[...] = (0.5 / dm) * jnp.sum(te[8:9, :], axis=1, keepdims=True)

    out_shape = []
    for w, _, _ in params:
        out_shape += [jax.ShapeDtypeStruct(w.shape, F32)] * 4
    out_shape.append(jax.ShapeDtypeStruct((1, 1), F32))
    res = pl.pallas_call(
        adamw_body, name=name + "_adamw", in_specs=[vm] * (n_grp + 3 * n_par), out_specs=[vm] * len(out_shape),
        out_shape=out_shape, compiler_params=pltpu.CompilerParams(vmem_limit_bytes=VMEM_LIMIT),
    )(*totals, *flat_params)
    return [res[4 * p:4 * p + 4] for p in range(n_par)], res[-1]


def _pack(arrays, rows_multiple):
    flat = jnp.concatenate([a.reshape(-1) for a in arrays])
    rows = -(-flat.shape[0] // LANES)
    rows = -(-rows // rows_multiple) * rows_multiple
    flat = jnp.pad(flat, (0, rows * LANES - flat.shape[0]))
    return flat.reshape(rows, LANES)


def _unshard_cols(stacked):
    moved = jnp.moveaxis(stacked, 0, -2)
    return moved.reshape(moved.shape[:-2] + (4 * stacked.shape[-1],))


def kernel(x, ab_w_in, a_conv_w, a_conv_b, a_norm_g, a_norm_b, b_norm_g, b_norm_b, b_spatial_w, b_spatial_b, ab_w_out, c_w_qkv, c_b_qkv, c_sinks, c_w_o, ffn_w_up, ffn_conv_w, ffn_conv_b, ffn_w_down, ln_g, ln_b, loss_target, m_ab_w_in, m_a_conv_w, m_a_conv_b, m_a_norm_g, m_a_norm_b, m_b_norm_g, m_b_norm_b, m_b_spatial_w, m_b_spatial_b, m_ab_w_out, m_c_w_qkv, m_c_b_qkv, m_c_sinks, m_c_w_o, m_ffn_w_up, m_ffn_conv_w, m_ffn_conv_b, m_ffn_w_down, m_ln_g, m_ln_b, v_ab_w_in, v_a_conv_w, v_a_conv_b, v_a_norm_g, v_a_norm_b, v_b_norm_g, v_b_norm_b, v_b_spatial_w, v_b_spatial_b, v_ab_w_out, v_c_w_qkv, v_c_b_qkv, v_c_sinks, v_c_w_o, v_ffn_w_up, v_ffn_conv_w, v_ffn_conv_b, v_ffn_w_down, v_ln_g, v_ln_b):
    rows, d = x.shape[1], x.shape[2]
    depth = ln_g.shape[0]
    assert depth == 2 and x.shape[0] == 1
    alpha = (2.0 * depth) ** 0.25
    f = ffn_w_down.shape[1] * 4
    n_q = c_sinks.shape[1]
    q_idx = 2 * lax.axis_index("x") + lax.axis_index("y")
    c_idx = lax.axis_index("c")
    xs, tgt = x[0], loss_target[0]

    def own_slot(part):
        buf = lax.empty((4,) + part.shape, part.dtype)
        return lax.dynamic_update_slice(buf, part[None], (q_idx, 0, 0, 0))

    def halves(wm):
        return own_slot(wm.astype(BF16).reshape((2, wm.shape[0] // 2) + wm.shape[1:]))

    small_sharded = [a_conv_w[0], c_b_qkv[0], ffn_conv_w, ln_g, ln_b]
    small_pack = _pack(small_sharded, 16)
    bufs = [halves(ab_w_in[0]), own_slot(small_pack.reshape(2, small_pack.shape[0] // 2, LANES)), halves(ab_w_out[0]),
            halves(ffn_w_up[0]), halves(ffn_w_down[0]), halves(c_w_qkv[0]), halves(c_w_o[0]),
            halves(ffn_w_up[1]), halves(ffn_w_down[1])]
    whole = lambda g: g.reshape(4, 2 * g.shape[2], g.shape[3])
    n_now = 2
    first_two = _gather_now(bufs[:n_now], name="gather_now")
    w_in, small_all = [whole(g) for g in first_two]
    later = bufs[n_now:]
    half = [True, True] + [False] * (len(later) - 2)
    ssems, rsems, started = _gather_start(later, half, first_two[1], name="gather_start")

    def arrive(idx, after, tag):
        idx = [i - n_now for i in idx]
        halved = [half[i] for i in idx]
        got = _gather_wait([started[i] for i in idx], halved, [ssems[i] for i in idx], [rsems[i] for i in idx], after,
                           name=f"gather_wait_{tag}")
        if all(halved):
            got = _sibling_swap(got, name=f"gather_swap_{tag}")
        return [whole(g) for g in got]

    small_all = small_all.reshape(4, -1)
    sh_shapes = [s.shape for s in small_sharded]
    pieces, pos = [], 0
    for s in sh_shapes:
        n = math.prod(s)
        pieces.append(_unshard_cols(small_all[:, pos:pos + n].reshape((4,) + s)))
        pos += n
    conv_w_a, b_qkv, conv_w_f, ln_gf, ln_bf = pieces

    tril = jnp.tril(jnp.ones((B_CHUNK, B_CHUNK), F32))
    ws = (b_spatial_w[0] * tril).astype(BF16)
    wst = jnp.swapaxes(ws, 1, 2)
    sbb = jnp.broadcast_to(b_spatial_b[0][:, :, None], b_spatial_w[0].shape)
    mix_vecs = [a_conv_b, a_norm_g, a_norm_b, b_norm_g, b_norm_b]
    cw_f = [jnp.swapaxes(conv_w_f[l].reshape(3, 2, f), 0, 1) for l in range(depth)]
    cb_f = [ffn_conv_b[l].reshape(2, 1, f) for l in range(depth)]
    lng = lambda i, j: ln_gf[i, j].reshape(1, d)
    lnb = lambda i, j: ln_bf[i, j].reshape(1, d)
    sinks = c_sinks[0]

    w_up, w_down = [None, None], [None, None]

    def ffn_fwd(xin, l):
        w_up[l], = arrive([3 + 4 * l], xin, f"up{l}")
        hf, fact = _ffn_up_fwd(xin, w_up[l], cw_f[l], cb_f[l], name=f"ffn{l}_up")
        w_down[l] = arrive([4 + 4 * l], fact, f"down{l}")[0].reshape(-1, d)
        out = _matmul(fact, w_down[l], name=f"ffn{l}_down", tm=512, tn=1024, tk=2816)
        return hf, fact, out

    h0, ab, a2 = _mixer_fwd(xs, w_in, conv_w_a, *mix_vecs, ws, sbb, name="mix_fwd")
    w_out = arrive([2], ab, "out")[0].reshape(-1, d)
    mix = _matmul(ab, w_out, name="mix_out", tm=1024, tn=1024, tk=1024)
    x1 = _add_ln_fwd(xs, mix, lng(0, 0), lnb(0, 0), alpha, name="ln00")
    hf0, f0, ffn0 = ffn_fwd(x1, 0)
    x2 = _add_ln_fwd(x1, ffn0, lng(0, 1), lnb(0, 1), alpha, name="ln01")
    w_qkv = _unshard_cols(arrive([5], x2, "qkv")[0])
    qkv = _matmul(x2, w_qkv, name="att_qkv", tm=1024, tn=w_qkv.shape[1], tk=1024, bias=b_qkv.reshape(1, -1))
    ao, lse = _attn_fwd(qkv, sinks, name="att_core")
    w_o = arrive([6], ao, "o")[0].reshape(-1, d)
    att = _matmul(ao, w_o, name="att_out", tm=1024, tn=1024, tk=1024)
    x3 = _add_ln_fwd(x2, att, lng(1, 0), lnb(1, 0), alpha, name="ln10")
    hf1, f1, ffn1 = ffn_fwd(x3, 1)
    sq_err, dy = _add_ln_loss(x3, ffn1, lng(1, 1), lnb(1, 1), tgt, alpha, name="ln11_loss")

    def owner_view(g):
        if g.ndim == 3:
            return g.reshape(4, 2, g.shape[1] // 2, g.shape[2])
        return g.reshape(4, 2, g.shape[0] // 8, g.shape[1])

    in_flight = []

    def send_grads(tag, grads, after):
        lands = [lax.empty((8,) + g.shape[2:], BF16) for g in grads]
        ss, rs, g_thru, l_thru, token = _reduce_start(grads, lands, after, name=f"reduce_start_{tag}")
        in_flight.append((tag, g_thru, l_thru, ss, rs))
        return token[0:1, 0:1]

    def ffn_bwd(dz, xin, hf, fact, l):
        d_wdown = _matmul(fact, dz, name=f"ffn{l}_down_dw", ta=True, tm=1408, tn=1024, tk=2048, out_dtype=BF16)
        dfa = _matmul(dz, w_down[l], name=f"ffn{l}_down_dx", tb=True, tm=1024, tn=1408, tk=1024, out_dtype=BF16)
        dx_parts, d_wup, dcw, dcb = _ffn_up_bwd(hf, dfa, xin, w_up[l], cw_f[l], cb_f[l], name=f"ffn{l}_up_bwd")
        tok = send_grads(f"ffn{l}", [owner_view(d_wup), owner_view(d_wdown)], dcb)
        return [(dx_parts, 1.0), (dz, alpha)], dcw, dcb, tok

    dz, dg11, db11 = _add_ln_bwd([(dy, 1.0)], x3, ffn1, lng(1, 1), alpha, name="ln11_bwd")
    dx3, dcw1, dcb1, tok = ffn_bwd(dz, x3, hf1, f1, 1)
    dz, dg10, db10 = _add_ln_bwd(dx3, x2, att, lng(1, 0) + tok, alpha, name="ln10_bwd")
    d_wo = _matmul(ao, dz, name="att_out_dw", ta=True, tm=1024, tn=1024, tk=1024, out_dtype=BF16)
    dao = _matmul(dz, w_o, name="att_out_dx", tb=True, tm=1024, tn=1024, tk=1024)
    dq, dkc, dkp, d_sinks = _attn_bwd(qkv, dao, lse, sinks, name="att_core_bwd")
    dqkv, d_bqkv = _dqkv_assemble(dq, dkc, dkp, name="att_dqkv")
    d_wqkv = _matmul(x2, dqkv, name="att_qkv_dw", ta=True, tm=1024, tn=dqkv.shape[1], tk=1024, out_dtype=BF16)
    d_wqkv_st = jnp.moveaxis(d_wqkv.reshape(d_wqkv.shape[0], 4, -1), 1, 0)
    tok = send_grads("att", [owner_view(d_wqkv_st), owner_view(d_wo)], d_bqkv)
    dx2 = _matmul(dqkv, w_qkv, name="att_qkv_dx", tb=True, tm=1024, tn=1024, tk=dqkv.shape[1], addend=(dz, alpha))
    dz, dg01, db01 = _add_ln_bwd([(dx2, 1.0)], x1, ffn0, lng(0, 1) + tok, alpha, name="ln01_bwd")
    dx1, dcw0, dcb0, tok = ffn_bwd(dz, x1, hf0, f0, 0)
    dz, dg00, db00 = _add_ln_bwd(dx1, xs, mix, lng(0, 0) + tok, alpha, name="ln00_bwd")
    d_wout = _matmul(ab, dz, name="mix_out_dw", ta=True, tm=1024, tn=1024, tk=1024, out_dtype=BF16)
    dab = _matmul(dz, w_out, name="mix_out_dx", tb=True, tm=1024, tn=1024, tk=1024)
    grad_x, d_win, d_cwa, d_cba, d_ga, d_ba, d_gb, d_bb, d_ws, d_sb = _mixer_bwd(
        h0, a2, dab, xs, w_in, dz, alpha, conv_w_a, *mix_vecs[1:], ws, wst, sbb, tril, name="mix_bwd")

    small_w = [a_conv_w, a_conv_b, a_norm_g, a_norm_b, b_norm_g, b_norm_b, b_spatial_w, b_spatial_b, c_b_qkv,
               c_sinks, ffn_conv_w, ffn_conv_b, ln_g, ln_b]
    small_m = [m_a_conv_w, m_a_conv_b, m_a_norm_g, m_a_norm_b, m_b_norm_g, m_b_norm_b, m_b_spatial_w, m_b_spatial_b,
               m_c_b_qkv, m_c_sinks, m_ffn_conv_w, m_ffn_conv_b, m_ln_g, m_ln_b]
    small_v = [v_a_conv_w, v_a_conv_b, v_a_norm_g, v_a_norm_b, v_b_norm_g, v_b_norm_b, v_b_spatial_w, v_b_spatial_b,
               v_c_b_qkv, v_c_sinks, v_ffn_conv_w, v_ffn_conv_b, v_ln_g, v_ln_b]
    local = [d_cwa, d_cba, d_ga, d_ba, d_gb, d_bb, d_ws, d_sb, d_bqkv, d_sinks, dcw0, dcb0, dcw1, dcb1,
             dg00, dg01, dg10, dg11, db00, db01, db10, db11, sq_err]
    small_out, loss = _small_tail(local, list(zip(small_w, small_m, small_v)), name="small_tail")
    loss = loss[0, 0]
    small_g = [o[0] for o in small_out]
    sm_delta = [o[1] for o in small_out]
    sm_m = [o[2] for o in small_out]
    sm_v = [o[3] for o in small_out]

    place = jnp.stack([q_idx, c_idx, 4 * lax.axis_index("x") + 2 * lax.axis_index("y") + c_idx]).astype(jnp.int32)
    where = {"mix": [(0, None), (1, None)], "att": [(2, None), (3, None)], "ffn0": [(4, 0), (5, 0)], "ffn1": [(4, 1), (5, 1)]}
    big_w = [ab_w_in, ab_w_out, c_w_qkv, c_w_o, ffn_w_up, ffn_w_down]
    big_m = [m_ab_w_in, m_ab_w_out, m_c_w_qkv, m_c_w_o, m_ffn_w_up, m_ffn_w_down]
    big_v = [v_ab_w_in, v_ab_w_out, v_c_w_qkv, v_c_w_o, v_ffn_w_up, v_ffn_w_down]
    big_out = [None] * 6

    def finish(tags, after, label):
        bufs, layout = {}, []
        for tag, g_thru, l_thru, ss, rs in in_flight:
            if tag not in tags:
                continue
            own, landed = _reduce_wait(g_thru, l_thru, ss, rs, after, name=f"reduce_wait_{tag}")
            for k, (o, lead) in enumerate(where[tag]):
                piece = own[k].shape[2:]
                shape = (2,) + piece if lead is None else (2, 2) + piece
                bufs[o] = _octo_sum(own[k], landed[k], place, bufs.get(o), (lead, shape), name=f"reduce_sum_{tag}{k}")
                layout.append((o, lead))
        order = sorted(bufs)
        shared = _sibling_share([bufs[o] for o in order], [(order.index(o), lead) for o, lead in layout],
                                name=f"reduce_share_{label}")
        for o, g in zip(order, shared):
            w = big_w[o]
            two_d = lambda a: a.reshape(-1, a.shape[-1])
            outs = _adamw(two_d(w), two_d(g), two_d(big_m[o]), two_d(big_v[o]), name=f"adamw_big{o}")
            big_out[o] = [r.reshape(w.shape) for r in outs]
        return big_out[order[-1]][0]

    tok = send_grads("mix", [owner_view(d_win), owner_view(d_wout)], after=sm_delta[0])
    done = finish(("ffn1", "att", "ffn0"), sm_delta[6] + tok, "early")
    finish(("mix",), done, "mix")

    order_big = {0: 0, 9: 1, 10: 2, 13: 3, 14: 4, 17: 5}
    order_small = {1: 0, 2: 1, 3: 2, 4: 3, 5: 4, 6: 5, 7: 6, 8: 7, 11: 8, 12: 9, 15: 10, 16: 11, 18: 12, 19: 13}
    grads, deltas, new_m, new_v = [], [], [], []
    for pos_w in range(20):
        if pos_w in order_big:
            t = order_big[pos_w]
            grads.append(big_out[t][3])
            deltas.append(big_out[t][0])
            new_m.append(big_out[t][1])
            new_v.append(big_out[t][2])
        else:
            t = order_small[pos_w]
            grads.append(small_g[t])
            deltas.append(sm_delta[t])
            new_m.append(sm_m[t])
            new_v.append(sm_v[t])
    return (loss, grad_x[None], *grads, *deltas, *new_m, *new_v)
```

```python
import math

import jax
import jax.numpy as jnp
from jax import lax
from jax.experimental import pallas as pl
from jax.experimental.pallas import tpu as pltpu

F32 = jnp.float32
BF16 = jnp.bfloat16
MESH = pl.DeviceIdType.MESH

LN_EPS = 1e-5
HEAD_DIM = 64
ATT_BLOCK = 128
Q_PER_KV = 8
A_KERNEL = 31
CONV_HALO = 32
FFN_HALO = 8
BF16_ROWS = 16
B_CHUNK = 128
LANES = 128
MXU_WIDTH = 256
GELU_C = math.sqrt(2.0 / math.pi)
ADAM_LR = 0.001
ADAM_B1 = 0.9
ADAM_B2 = 0.999
ADAM_EPS = 1e-08
ADAM_WD = 0.01
ADAM_STEP = 10
VMEM_LIMIT = 56 * 1024 * 1024


def _cp(*dims):
    return pltpu.CompilerParams(dimension_semantics=dims, vmem_limit_bytes=VMEM_LIMIT)


def _pick(n, prefs):
    for p in prefs:
        if n % p == 0:
            return p
    return n


def _sig(x):
    return 1.0 / (1.0 + jnp.exp(-x))


def _gelu(x):
    t = jnp.tanh(GELU_C * (x + 0.044715 * (x * x * x)))
    return x * (0.5 * (1.0 + t)), t


def _gelu_grad(x, t):
    return 0.5 * (1.0 + t) + 0.5 * x * (1.0 - t * t) * (GELU_C * (1.0 + 3.0 * 0.044715 * x * x))


def _ln_stats(z):
    mu = jnp.mean(z, axis=-1, keepdims=True)
    zc = z - mu
    var = jnp.mean(zc * zc, axis=-1, keepdims=True)
    rstd = lax.rsqrt(var + LN_EPS)
    return zc * rstd, rstd


def _ln_bwd(dxh, xh, rstd):
    return rstd * (dxh - jnp.mean(dxh, axis=-1, keepdims=True) - xh * jnp.mean(dxh * xh, axis=-1, keepdims=True))


def _rowsum(a):
    return jnp.sum(a, axis=0, keepdims=True)


def _lshape(a):
    return (a.shape[0], a.shape[1]) if a.ndim == 2 else (a.shape[1], a.shape[0] * a.shape[2])


def _spec2(arr, blk_r, blk_c, ridx, cidx):
    if len(arr.shape) == 2:
        return pl.BlockSpec((blk_r, blk_c), lambda i, j, k: (ridx(i, j, k), cidx(i, j, k)))
    per = arr.shape[2] // blk_c
    assert arr.shape[2] % blk_c == 0
    return pl.BlockSpec((None, blk_r, blk_c), lambda i, j, k: (cidx(i, j, k) // per, ridx(i, j, k), cidx(i, j, k) % per))


def _matmul(a, b, *, name, ta=False, tb=False, tm, tn, tk, out_dtype=F32, out_stack=None, bias=None, addend=None):
    ar, ac = _lshape(a)
    br, bc = _lshape(b)
    m, kdim = (ac, ar) if ta else (ar, ac)
    n = br if tb else bc
    assert (bc if tb else br) == kdim
    tm, tn, tk = min(tm, m), min(tn, n), min(tk, kdim)
    assert m % tm == 0 and n % tn == 0 and kdim % tk == 0, (name, m, n, kdim, tm, tn, tk)
    nk = kdim // tk
    gi, gj, gk = (lambda i, j, k: i), (lambda i, j, k: j), (lambda i, j, k: k)
    a_spec = _spec2(a, tk, tm, gk, gi) if ta else _spec2(a, tm, tk, gi, gk)
    b_spec = _spec2(b, tn, tk, gj, gk) if tb else _spec2(b, tk, tn, gk, gj)
    if out_stack is None:
        out_sds = jax.ShapeDtypeStruct((m, n), out_dtype)
    else:
        out_sds = jax.ShapeDtypeStruct((out_stack, m, n // out_stack), out_dtype)
    o_spec = _spec2(out_sds, tm, tn, gi, gj)
    in_specs = [a_spec, b_spec]
    args = [a, b]
    if bias is not None:
        in_specs.append(pl.BlockSpec((1, tn), lambda i, j, k: (0, j)))
        args.append(bias)
    scale = None
    if addend is not None:
        add_arr, scale = addend
        in_specs.append(pl.BlockSpec((tm, tn), lambda i, j, k: (i, j)))
        args.append(add_arr)
    use_acc = nk > 1 and out_dtype != F32
    dn = (((0 if ta else 1,), (1 if tb else 0,)), ((), ()))

    def body(*refs):
        a_ref, b_ref = refs[0], refs[1]
        pos = 2
        bias_ref = add_ref = None
        if bias is not None:
            bias_ref = refs[pos]
            pos += 1
        if addend is not None:
            add_ref = refs[pos]
            pos += 1
        o_ref = refs[pos]
        acc_ref = refs[pos + 1] if use_acc else o_ref
        p = lax.dot_general(a_ref[...].astype(BF16), b_ref[...].astype(BF16), dn, preferred_element_type=F32)

        def finish(val):
            if bias_ref is not None:
                val = val + bias_ref[...]
            if add_ref is not None:
                val = val + scale * add_ref[...]
            return val.astype(out_dtype)

        if nk == 1:
            o_ref[...] = finish(p)
        else:
            k = pl.program_id(2)

            @pl.when(k == 0)
            def _():
                acc_ref[...] = p

            @pl.when(k > 0)
            def _():
                acc_ref[...] += p

            if use_acc or bias_ref is not None or add_ref is not None:
                @pl.when(k == nk - 1)
                def _():
                    o_ref[...] = finish(acc_ref[...])

    return pl.pallas_call(
        body, name=name, grid=(m // tm, n // tn, nk), in_specs=in_specs, out_specs=o_spec, out_shape=out_sds,
        scratch_shapes=[pltpu.VMEM((tm, tn), F32)] if use_acc else [],
        compiler_params=_cp("parallel", "parallel", "arbitrary"),
    )(*args)


def _add_ln_fwd(x, s, g, b, alpha, *, name):
    rows, d = x.shape
    t = _pick(rows, (512, 256))

    def body(x_ref, s_ref, g_ref, b_ref, y_ref):
        xh, _ = _ln_stats(alpha * x_ref[...] + s_ref[...])
        y_ref[...] = xh * g_ref[...] + b_ref[...]

    row = pl.BlockSpec((t, d), lambda i: (i, 0))
    vec = pl.BlockSpec((1, d), lambda i: (0, 0))
    return pl.pallas_call(body, name=name, grid=(rows // t,), in_specs=[row, row, vec, vec], out_specs=row,
                          out_shape=jax.ShapeDtypeStruct((rows, d), F32), compiler_params=_cp("parallel"))(x, s, g, b)


def _add_ln_bwd(dy_terms, x, s, g, alpha, *, name):
    rows, d = x.shape
    t = _pick(rows, (512, 256))
    nterm = len(dy_terms)
    scales = [sc for _, sc in dy_terms]
    ranks = [a.ndim for a, _ in dy_terms]

    def body(*refs):
        dy_refs = refs[:nterm]
        x_ref, s_ref, g_ref, dz_ref, dg_ref, db_ref = refs[nterm:]

        @pl.when(pl.program_id(0) == 0)
        def _():
            dg_ref[...] = jnp.zeros_like(dg_ref)
            db_ref[...] = jnp.zeros_like(db_ref)

        dyv = None
        for r, sc, rank in zip(dy_refs, scales, ranks):
            slabs = [r[...]] if rank == 2 else [r[p] for p in range(r.shape[0])]
            for v in slabs:
                v = v if sc == 1.0 else sc * v
                dyv = v if dyv is None else dyv + v
        xh, rstd = _ln_stats(alpha * x_ref[...] + s_ref[...])
        dz_ref[...] = _ln_bwd(dyv * g_ref[...], xh, rstd)
        dg_ref[...] += _rowsum(dyv * xh)
        db_ref[...] += _rowsum(dyv)

    row = pl.BlockSpec((t, d), lambda i: (i, 0))
    vec = pl.BlockSpec((1, d), lambda i: (0, 0))
    vsds = jax.ShapeDtypeStruct((1, d), F32)
    dy_specs = [row if a.ndim == 2 else pl.BlockSpec((a.shape[0], t, d), lambda i: (0, i, 0)) for a, _ in dy_terms]
    return pl.pallas_call(body, name=name, grid=(rows // t,), in_specs=dy_specs + [row, row, vec], out_specs=[row, vec, vec],
                          out_shape=[jax.ShapeDtypeStruct((rows, d), F32), vsds, vsds],
                          compiler_params=_cp("arbitrary"))(*[a for a, _ in dy_terms], x, s, g)


def _add_ln_loss(x, s, g, b, tgt, alpha, *, name):
    rows, d = x.shape
    t = _pick(rows, (512, 256))

    def body(x_ref, s_ref, g_ref, b_ref, t_ref, l_ref, dy_ref):
        @pl.when(pl.program_id(0) == 0)
        def _():
            l_ref[...] = jnp.zeros_like(l_ref)

        xh, _ = _ln_stats(alpha * x_ref[...] + s_ref[...])
        e = (xh * g_ref[...] + b_ref[...]) - t_ref[...]
        l_ref[...] += _rowsum(e * e)
        dy_ref[...] = e * (1.0 / d)

    row = pl.BlockSpec((t, d), lambda i: (i, 0))
    vec = pl.BlockSpec((1, d), lambda i: (0, 0))
    return pl.pallas_call(body, name=name, grid=(rows // t,), in_specs=[row, row, vec, vec, row], out_specs=[vec, row],
                          out_shape=[jax.ShapeDtypeStruct((1, d), F32), jax.ShapeDtypeStruct((rows, d), F32)],
                          compiler_params=_cp("arbitrary"))(x, s, g, b, tgt)


def _col_blocks(width, step):
    return [slice(pos, min(pos + step, width)) for pos in range(0, width, step)]


def _conv3(e, w, b):
    r1 = pltpu.roll(e, 1, 0)
    r2 = pltpu.roll(e, 2, 0)
    return w[0:1, :] * r2 + w[1:2, :] * r1 + w[2:3, :] * e + b, (r2, r1, e)


def _ffn_up_fwd(x, w_up, cw, cb, *, name):
    rows, d = x.shape
    nq, _, tc = w_up.shape
    nj = nq // 2
    f = tc * nj
    tm = _pick(rows, (512, 256))
    blocks = _col_blocks(tc, tc)

    def body(x_ref, wg_ref, wv_ref, cw_ref, cb_ref, hf_ref, f_ref, prev_ref):
        @pl.when(pl.program_id(1) == 0)
        def _():
            prev_ref[...] = jnp.zeros_like(prev_ref)

        xb = x_ref[...].astype(BF16)
        for cs in blocks:
            hc = []
            for s, w_ref in ((0, wg_ref), (1, wv_ref)):
                h = jnp.dot(xb, w_ref[:, cs], preferred_element_type=F32)
                hf_ref[s, :, cs] = h
                e = jnp.concatenate([prev_ref[s, :, cs], h], axis=0)
                prev_ref[s, :, cs] = h[tm - FFN_HALO:]
                y, _ = _conv3(e, cw_ref[s, :, cs], cb_ref[s, :, cs])
                hc.append(y[FFN_HALO:])
            gl, _ = _gelu(hc[0])
            f_ref[:, cs] = (gl * hc[1]).astype(BF16)

    in_specs = [
        pl.BlockSpec((tm, d), lambda j, i: (i, 0)),
        pl.BlockSpec((None, d, tc), lambda j, i: (j, 0, 0)),
        pl.BlockSpec((None, d, tc), lambda j, i: (nj + j, 0, 0)),
        pl.BlockSpec((2, 3, tc), lambda j, i: (0, 0, j)),
        pl.BlockSpec((2, 1, tc), lambda j, i: (0, 0, j)),
    ]
    out_specs = [pl.BlockSpec((2, tm, tc), lambda j, i: (0, i, j)), pl.BlockSpec((tm, tc), lambda j, i: (i, j))]
    out_shape = [jax.ShapeDtypeStruct((2, rows, f), F32), jax.ShapeDtypeStruct((rows, f), BF16)]
    return pl.pallas_call(body, name=name, grid=(nj, rows // tm), in_specs=in_specs, out_specs=out_specs, out_shape=out_shape,
                          scratch_shapes=[pltpu.VMEM((2, FFN_HALO, tc), F32)],
                          compiler_params=_cp("parallel", "arbitrary"))(x, w_up, w_up, cw, cb)


def _ffn_up_bwd(hf, df, x, w_up, cw, cb, *, name):
    _, rows, f = hf.shape
    d = x.shape[1]
    nq, _, tc = w_up.shape
    nj = nq // 2
    tm = _pick(rows, (512, 256))
    hb = tm // FFN_HALO
    once = pl.Buffered(1)
    ni = rows // tm
    last_blk = rows // FFN_HALO - 1
    ext = tm + 2 * FFN_HALO
    tile = slice(FFN_HALO, FFN_HALO + tm)
    blocks = _col_blocks(tc, MXU_WIDTH)

    def body(h_ref, hp_ref, hn_ref, d_ref, dn_ref, x_ref, wg_ref, wv_ref, cw_ref, cb_ref, dx_ref, dw_out_ref, dcw_ref, dcb_ref,
             dw_ref):
        i = pl.program_id(1)
        first = i == 0
        last = i == ni - 1

        @pl.when(first)
        def _():
            dw_ref[...] = jnp.zeros_like(dw_ref)
            dcw_ref[...] = jnp.zeros_like(dcw_ref)
            dcb_ref[...] = jnp.zeros_like(dcb_ref)

        xt = x_ref[...].astype(BF16).T
        dx = None
        for cs in blocks:
            wc = cs.stop - cs.start
            d_next = dn_ref[:, cs].astype(F32)[0:FFN_HALO]
            de = jnp.concatenate([jnp.zeros((FFN_HALO, wc), F32), d_ref[:, cs].astype(F32), jnp.where(last, 0.0, d_next)], axis=0)
            taps, hc = [], []
            for s in range(2):
                e = jnp.concatenate([jnp.where(first, 0.0, hp_ref[s, :, cs]), h_ref[s, :, cs], hn_ref[s, :, cs]], axis=0)
                y, tp = _conv3(e, cw_ref[s, :, cs], cb_ref[s, :, cs])
                hc.append(y)
                taps.append(tp)
            gl, th = _gelu(hc[0])
            dhc = (de * hc[1] * _gelu_grad(hc[0], th), de * gl)
            for s, w_ref in ((0, wg_ref), (1, wv_ref)):
                w = cw_ref[s, :, cs]
                g = dhc[s]
                dh = (w[2:3, :] * g + w[1:2, :] * pltpu.roll(g, ext - 1, 0) + w[0:1, :] * pltpu.roll(g, ext - 2, 0))[tile]
                gt = g[tile]
                for k in range(3):
                    dcw_ref[s, k:k + 1, cs] += _rowsum(gt * taps[s][k][tile])
                dcb_ref[s, :, cs] += _rowsum(gt)
                dhb = dh.astype(BF16)
                part = lax.dot_general(dhb, w_ref[:, cs], (((1,), (1,)), ((), ())), preferred_element_type=F32)
                dx = part if dx is None else dx + part
                dw_ref[s, :, cs] += jnp.dot(xt, dhb, preferred_element_type=F32)
        dx_ref[...] = dx

        @pl.when(last)
        def _():
            dw_out_ref[...] = dw_ref[...].astype(BF16)

    in_specs = [
        pl.BlockSpec((2, tm, tc), lambda j, i: (0, i, j)),
        pl.BlockSpec((2, FFN_HALO, tc), lambda j, i: (0, jnp.maximum(i * hb - 1, 0), j)),
        pl.BlockSpec((2, FFN_HALO, tc), lambda j, i: (0, jnp.minimum((i + 1) * hb, last_blk), j)),
        pl.BlockSpec((tm, tc), lambda j, i: (i, j)),
        pl.BlockSpec((BF16_ROWS, tc), lambda j, i: (jnp.minimum((i + 1) * (tm // BF16_ROWS), rows // BF16_ROWS - 1), j)),
        pl.BlockSpec((tm, d), lambda j, i: (i, 0)),
        pl.BlockSpec((None, d, tc), lambda j, i: (j, 0, 0), pipeline_mode=once),
        pl.BlockSpec((None, d, tc), lambda j, i: (nj + j, 0, 0), pipeline_mode=once),
        pl.BlockSpec((2, 3, tc), lambda j, i: (0, 0, j)),
        pl.BlockSpec((2, 1, tc), lambda j, i: (0, 0, j)),
    ]
    out_specs = [
        pl.BlockSpec((None, tm, d), lambda j, i: (j, i, 0)),
        pl.BlockSpec((2, None, d, tc), lambda j, i: (0, j, 0, 0), pipeline_mode=once),
        pl.BlockSpec((2, 3, tc), lambda j, i: (0, 0, j)),
        pl.BlockSpec((2, 1, tc), lambda j, i: (0, 0, j)),
    ]
    out_shape = [jax.ShapeDtypeStruct((nj, rows, d), F32), jax.ShapeDtypeStruct((2, nj, d, tc), BF16),
                 jax.ShapeDtypeStruct((2, 3, f), F32), jax.ShapeDtypeStruct((2, 1, f), F32)]
    dx, dw, dcw, dcb = pl.pallas_call(body, name=name, grid=(nj, ni), in_specs=in_specs, out_specs=out_specs,
                                      out_shape=out_shape, scratch_shapes=[pltpu.VMEM((2, d, tc), F32)],
                                      compiler_params=_cp("parallel", "arbitrary"))(
        hf, hf, hf, df, df, x, w_up, w_up, cw, cb)
    return dx, dw.reshape(nq, d, tc), dcw, dcb


def _mixer_fwd(x, w_in, cw, cb, ga, ba, gb, bb, ws, sbb, *, name):
    rows, d = x.shape
    _, _, w = w_in.shape
    t = _pick(rows, (256,))
    groups = w // B_CHUNK

    def body(x_ref, win_ref, cw_ref, cb_ref, ga_ref, ba_ref, gb_ref, bb_ref, ws_ref, sb_ref, h_ref, o_ref, a2_ref, prev_ref):
        @pl.when(pl.program_id(0) == 0)
        def _():
            prev_ref[...] = jnp.zeros_like(prev_ref)

        xb = x_ref[...].astype(BF16)
        for s in range(4):
            h_ref[s] = jnp.dot(xb, win_ref[s], preferred_element_type=F32)
        a1 = h_ref[0] * _sig(h_ref[1])
        e = jnp.concatenate([prev_ref[...], a1], axis=0)
        prev_ref[...] = a1[t - CONV_HALO:]
        acc = cw_ref[A_KERNEL - 1:A_KERNEL, :] * e
        for k in range(A_KERNEL - 1):
            acc = acc + cw_ref[k:k + 1, :] * pltpu.roll(e, A_KERNEL - 1 - k, 0)
        a2 = acc[CONV_HALO:] + cb_ref[...]
        a2_ref[...] = a2
        xh, _ = _ln_stats(a2)
        a3 = xh * ga_ref[...] + ba_ref[...]
        o_ref[:, 0:w] = (a3 * _sig(a3)).astype(BF16)

        u, _ = _gelu(h_ref[2])
        v1, _ = _gelu(h_ref[3])
        xh2, _ = _ln_stats(v1)
        v2 = (xh2 * gb_ref[...] + bb_ref[...]).astype(BF16)
        for c in range(t // B_CHUNK):
            rs = slice(c * B_CHUNK, (c + 1) * B_CHUNK)
            for g in range(groups):
                cs = slice(g * B_CHUNK, (g + 1) * B_CHUNK)
                mixed = jnp.dot(ws_ref[g], v2[rs, cs], preferred_element_type=F32) + sb_ref[g]
                o_ref[rs, w + g * B_CHUNK:w + (g + 1) * B_CHUNK] = (u[rs, cs] * mixed).astype(BF16)

    vec = pl.BlockSpec((1, w), lambda i: (0, 0))
    grp = pl.BlockSpec((groups, B_CHUNK, B_CHUNK), lambda i: (0, 0, 0))
    in_specs = [
        pl.BlockSpec((t, d), lambda i: (i, 0)),
        pl.BlockSpec((4, d, w), lambda i: (0, 0, 0)),
        pl.BlockSpec((A_KERNEL, w), lambda i: (0, 0)),
        vec, vec, vec, vec, vec, grp, grp,
    ]
    out_specs = [pl.BlockSpec((4, t, w), lambda i: (0, i, 0)), pl.BlockSpec((t, 2 * w), lambda i: (i, 0)),
                 pl.BlockSpec((t, w), lambda i: (i, 0))]
    out_shape = [jax.ShapeDtypeStruct((4, rows, w), F32), jax.ShapeDtypeStruct((rows, 2 * w), BF16),
                 jax.ShapeDtypeStruct((rows, w), F32)]
    return pl.pallas_call(body, name=name, grid=(rows // t,), in_specs=in_specs, out_specs=out_specs, out_shape=out_shape,
                          scratch_shapes=[pltpu.VMEM((CONV_HALO, w), F32)],
                          compiler_params=_cp("arbitrary"))(x, w_in, cw, cb, ga, ba, gb, bb, ws, sbb)


def _mixer_bwd(h0, a2, dab, x, w_in, res, res_scale, cw, ga, ba, gb, bb, ws, wst, sbb, tril, *, name):
    _, rows, w = h0.shape
    d = x.shape[1]
    once = pl.Buffered(1)
    t = _pick(rows, (256,))
    hb = t // CONV_HALO
    ni = rows // t
    last_blk = rows // CONV_HALO - 1
    ext = t + CONV_HALO
    tile = slice(0, t)
    groups = w // B_CHUNK
    taps = A_KERNEL - 1

    def body(h_ref, a2_ref, a2n_ref, d_ref, dn_ref, x_ref, win_ref, res_ref, cw_ref, ga_ref, ba_ref, gb_ref, bb_ref,
             ws_ref, wst_ref, sb_ref, tril_ref, dx_ref, dwin_ref, dcw_ref, dcb_ref, dga_ref, dba_ref, dgb_ref, dbb_ref,
             dws_ref, dsb_ref, dw_ref):
        i = pl.program_id(0)
        first = i == 0
        last = i == ni - 1

        @pl.when(first)
        def _():
            for r in (dw_ref, dcw_ref, dcb_ref, dga_ref, dba_ref, dgb_ref, dbb_ref, dws_ref, dsb_ref):
                r[...] = jnp.zeros_like(r)

        xt = x_ref[...].astype(BF16).T
        dx_terms = []

        def through_w_in(slot, dh):
            dhb = dh.astype(BF16)
            dx_terms.append(lax.dot_general(dhb, win_ref[slot], (((1,), (1,)), ((), ())), preferred_element_type=F32))
            dw_ref[slot] += jnp.dot(xt, dhb, preferred_element_type=F32)

        xh, rstd = _ln_stats(jnp.concatenate([a2_ref[...], a2n_ref[...]], axis=0))
        a3 = xh * ga_ref[...] + ba_ref[...]
        s3 = _sig(a3)
        da_e = jnp.concatenate([d_ref[:, 0:w], jnp.where(last, 0.0, dn_ref[...])], axis=0)
        da3 = da_e * (s3 * (1.0 + a3 * (1.0 - s3)))
        da2 = _ln_bwd(da3 * ga_ref[...], xh, rstd)
        dga_ref[...] += _rowsum(da3[tile] * xh[tile])
        dba_ref[...] += _rowsum(da3[tile])
        dcb_ref[...] += _rowsum(da2[tile])
        sgt = _sig(h_ref[1])
        a1t = h_ref[0] * sgt
        da1t = None
        for k in range(A_KERNEL):
            sh = taps - k
            fed = (da2 if sh == 0 else pltpu.roll(da2, ext - sh, 0))[tile]
            dcw_ref[k:k + 1, :] += _rowsum(a1t * fed)
            term = cw_ref[k:k + 1, :] * fed
            da1t = term if da1t is None else da1t + term
        through_w_in(0, da1t * sgt)
        through_w_in(1, da1t * h_ref[0] * sgt * (1.0 - sgt))

        bu = h_ref[2]
        bv = h_ref[3]
        u, tu = _gelu(bu)
        v1, tv = _gelu(bv)
        xh2, rstd2 = _ln_stats(v1)
        v2 = (xh2 * gb_ref[...] + bb_ref[...]).astype(BF16)
        db = d_ref[:, w:2 * w]
        dmx_all = db * u
        du_parts, dv2_parts = [], []
        for c in range(t // B_CHUNK):
            rs = slice(c * B_CHUNK, (c + 1) * B_CHUNK)
            du_row, dv2_row = [], []
            for g in range(groups):
                cs = slice(g * B_CHUNK, (g + 1) * B_CHUNK)
                v2cg = v2[rs, cs]
                mixed = jnp.dot(ws_ref[g], v2cg, preferred_element_type=F32) + sb_ref[g]
                dmx = dmx_all[rs, cs]
                dmxb = dmx.astype(BF16)
                du_row.append(db[rs, cs] * mixed)
                dv2_row.append(jnp.dot(wst_ref[g], dmxb, preferred_element_type=F32))
                dws_ref[g] += tril_ref[...] * lax.dot_general(dmxb, v2cg, (((1,), (1,)), ((), ())),
                                                               preferred_element_type=F32)
                dsb_ref[g:g + 1, :] += _rowsum(dmx.T)
            du_parts.append(jnp.concatenate(du_row, axis=1))
            dv2_parts.append(jnp.concatenate(dv2_row, axis=1))
        du = jnp.concatenate(du_parts, axis=0)
        dv2 = jnp.concatenate(dv2_parts, axis=0)
        dgb_ref[...] += _rowsum(dv2 * xh2)
        dbb_ref[...] += _rowsum(dv2)
        dv1 = _ln_bwd(dv2 * gb_ref[...], xh2, rstd2)
        through_w_in(2, du * _gelu_grad(bu, tu))
        through_w_in(3, dv1 * _gelu_grad(bv, tv))
        dx_ref[...] = res_scale * res_ref[...] + ((dx_terms[0] + dx_terms[1]) + (dx_terms[2] + dx_terms[3]))

        @pl.when(last)
        def _():
            dwin_ref[...] = dw_ref[...].astype(BF16)

    vec = pl.BlockSpec((1, w), lambda i: (0, 0))
    grp = pl.BlockSpec((groups, B_CHUNK, B_CHUNK), lambda i: (0, 0, 0))
    halo = pl.BlockSpec((CONV_HALO, w), lambda i: (jnp.minimum((i + 1) * hb, last_blk), 0))
    wide = pl.BlockSpec((t, d), lambda i: (i, 0))
    in_specs = [
        pl.BlockSpec((4, t, w), lambda i: (0, i, 0)),
        pl.BlockSpec((t, w), lambda i: (i, 0)),
        halo,
        pl.BlockSpec((t, 2 * w), lambda i: (i, 0)),
        halo,
        wide,
        pl.BlockSpec((4, d, w), lambda i: (0, 0, 0), pipeline_mode=once),
        wide,
        pl.BlockSpec((A_KERNEL, w), lambda i: (0, 0)),
        vec, vec, vec, vec, grp, grp, grp,
        pl.BlockSpec((B_CHUNK, B_CHUNK), lambda i: (0, 0)),
    ]
    vsds = jax.ShapeDtypeStruct((1, w), F32)
    out_specs = [
        wide,
        pl.BlockSpec((4, d, w), lambda i: (0, 0, 0), pipeline_mode=once),
        pl.BlockSpec((A_KERNEL, w), lambda i: (0, 0)),
        vec, vec, vec, vec, vec, grp,
        pl.BlockSpec((groups, B_CHUNK), lambda i: (0, 0)),
    ]
    out_shape = [jax.ShapeDtypeStruct((rows, d), F32), jax.ShapeDtypeStruct((4, d, w), BF16),
                 jax.ShapeDtypeStruct((A_KERNEL, w), F32),
                 vsds, vsds, vsds, vsds, vsds, jax.ShapeDtypeStruct((groups, B_CHUNK, B_CHUNK), F32),
                 jax.ShapeDtypeStruct((groups, B_CHUNK), F32)]
    return pl.pallas_call(body, name=name, grid=(ni,), in_specs=in_specs, out_specs=out_specs, out_shape=out_shape,
                          scratch_shapes=[pltpu.VMEM((4, d, w), F32)], compiler_params=_cp("arbitrary"))(
        h0, a2, a2, dab, dab, x, w_in, res, cw, ga, ba, gb, bb, ws, wst, sbb, tril)


GROUP_ROWS = Q_PER_KV * ATT_BLOCK


def _attn_mask(n):
    qi = lax.broadcasted_iota(jnp.int32, (GROUP_ROWS, 2 * ATT_BLOCK), 0) & (ATT_BLOCK - 1)
    sj = lax.broadcasted_iota(jnp.int32, (GROUP_ROWS, 2 * ATT_BLOCK), 1)
    diff = qi + ATT_BLOCK - sj
    return (diff >= 0) & (diff < ATT_BLOCK) & ((n > 0) | (sj >= ATT_BLOCK))


def _stack_heads(ref, kvh, dtype):
    heads = [ref[:, (kvh * Q_PER_KV + g) * HEAD_DIM:(kvh * Q_PER_KV + g + 1) * HEAD_DIM] for g in range(Q_PER_KV)]
    return jnp.concatenate(heads, axis=0).astype(dtype)


def _per_row_sink(sink_ref, kvh):
    head = lax.broadcasted_iota(jnp.int32, (GROUP_ROWS, 1), 0) // ATT_BLOCK
    out = jnp.zeros((GROUP_ROWS, 1), F32)
    for g in range(Q_PER_KV):
        out = jnp.where(head == g, sink_ref[kvh * Q_PER_KV + g], out)
    return out


def _attn_specs(rows, n_q):
    dq = n_q * HEAD_DIM
    dkv = 2 * (n_q // Q_PER_KV) * HEAD_DIM
    kv_blk = dq // dkv
    assert dq % dkv == 0
    return dq, dkv, [
        pl.BlockSpec(memory_space=pltpu.SMEM),
        pl.BlockSpec((ATT_BLOCK, dq), lambda n: (n, 0)),
        pl.BlockSpec((ATT_BLOCK, dkv), lambda n: (n, kv_blk)),
        pl.BlockSpec((ATT_BLOCK, dkv), lambda n: (jnp.maximum(n - 1, 0), kv_blk)),
    ]


def _kv_pair(kvc_ref, kvp_ref, kvh, n_kv):
    ks = slice(kvh * HEAD_DIM, (kvh + 1) * HEAD_DIM)
    vs = slice((n_kv + kvh) * HEAD_DIM, (n_kv + kvh + 1) * HEAD_DIM)
    kk = jnp.concatenate([kvp_ref[:, ks], kvc_ref[:, ks]], axis=0).astype(BF16)
    vv = jnp.concatenate([kvp_ref[:, vs], kvc_ref[:, vs]], axis=0).astype(BF16)
    return kk, vv


def _attn_fwd(qkv, sinks, *, name):
    rows = qkv.shape[0]
    n_q = sinks.shape[0]
    n_kv = n_q // Q_PER_KV
    scale = 1.0 / math.sqrt(HEAD_DIM)
    dq, _, in_specs = _attn_specs(rows, n_q)

    def body(sink_ref, q_ref, kvc_ref, kvp_ref, o_ref, lse_ref):
        valid = _attn_mask(pl.program_id(0))
        for kvh in range(n_kv):
            kk, vv = _kv_pair(kvc_ref, kvp_ref, kvh, n_kv)
            qs = _stack_heads(q_ref, kvh, BF16)
            s = lax.dot_general(qs, kk, (((1,), (1,)), ((), ())), preferred_element_type=F32)
            s = jnp.where(valid, s * scale, -jnp.inf)
            sk = _per_row_sink(sink_ref, kvh)
            m = jnp.maximum(jnp.max(s, axis=1, keepdims=True), sk)
            p = jnp.exp(s - m)
            l = jnp.sum(p, axis=1, keepdims=True) + jnp.exp(sk - m)
            o = jnp.dot((p / l).astype(BF16), vv, preferred_element_type=F32)
            lse = m + jnp.log(l)
            for g in range(Q_PER_KV):
                h = kvh * Q_PER_KV + g
                rs = slice(g * ATT_BLOCK, (g + 1) * ATT_BLOCK)
                o_ref[:, h * HEAD_DIM:(h + 1) * HEAD_DIM] = o[rs]
                lse_ref[:, h:h + 1] = lse[rs]

    out_specs = [pl.BlockSpec((ATT_BLOCK, dq), lambda n: (n, 0)), pl.BlockSpec((ATT_BLOCK, n_q), lambda n: (n, 0))]
    out_shape = [jax.ShapeDtypeStruct((rows, dq), F32), jax.ShapeDtypeStruct((rows, n_q), F32)]
    return pl.pallas_call(body, name=name, grid=(rows // ATT_BLOCK,), in_specs=in_specs, out_specs=out_specs,
                          out_shape=out_shape, compiler_params=_cp("parallel"))(sinks, qkv, qkv, qkv)


def _attn_bwd(qkv, dout, lse, sinks, *, name):
    rows = qkv.shape[0]
    n_q = sinks.shape[0]
    n_kv = n_q // Q_PER_KV
    scale = 1.0 / math.sqrt(HEAD_DIM)
    dq_w, dkv_w, in_specs = _attn_specs(rows, n_q)
    blk_q = pl.BlockSpec((ATT_BLOCK, dq_w), lambda n: (n, 0))
    blk_kv = pl.BlockSpec((ATT_BLOCK, dkv_w), lambda n: (n, 0))
    in_specs = in_specs + [blk_q, pl.BlockSpec((ATT_BLOCK, n_q), lambda n: (n, 0))]

    def body(sink_ref, q_ref, kvc_ref, kvp_ref, do_ref, lse_ref, dq_ref, dkc_ref, dkp_ref, dsink_ref):
        n = pl.program_id(0)

        @pl.when(n == 0)
        def _():
            dsink_ref[...] = jnp.zeros_like(dsink_ref)

        valid = _attn_mask(n)
        head_ids = lax.broadcasted_iota(jnp.int32, (1, n_q), 1)
        dsink = jnp.zeros((1, n_q), F32)
        for kvh in range(n_kv):
            kk, vv = _kv_pair(kvc_ref, kvp_ref, kvh, n_kv)
            qs = _stack_heads(q_ref, kvh, BF16)
            dos = _stack_heads(do_ref, kvh, BF16)
            lse = jnp.concatenate([lse_ref[:, kvh * Q_PER_KV + g:kvh * Q_PER_KV + g + 1] for g in range(Q_PER_KV)], axis=0)
            s = lax.dot_general(qs, kk, (((1,), (1,)), ((), ())), preferred_element_type=F32)
            s = jnp.where(valid, s * scale, -jnp.inf)
            p = jnp.exp(s - lse)
            dp = lax.dot_general(dos, vv, (((1,), (1,)), ((), ())), preferred_element_type=F32)
            delta = jnp.sum(p * dp, axis=1, keepdims=True)
            ds = (p * (dp - delta) * scale).astype(BF16)
            sink_term = jnp.exp(_per_row_sink(sink_ref, kvh) - lse) * delta
            dqs = jnp.dot(ds, kk, preferred_element_type=F32)
            for g in range(Q_PER_KV):
                h = kvh * Q_PER_KV + g
                rs = slice(g * ATT_BLOCK, (g + 1) * ATT_BLOCK)
                dsink = dsink + jnp.where(head_ids == h, -jnp.sum(sink_term[rs]), 0.0)
                dq_ref[:, h * HEAD_DIM:(h + 1) * HEAD_DIM] = dqs[rs]
            dk = lax.dot_general(ds, qs, (((0,), (0,)), ((), ())), preferred_element_type=F32)
            dv = lax.dot_general(p.astype(BF16), dos, (((0,), (0,)), ((), ())), preferred_element_type=F32)
            ks = slice(kvh * HEAD_DIM, (kvh + 1) * HEAD_DIM)
            vs = slice((n_kv + kvh) * HEAD_DIM, (n_kv + kvh + 1) * HEAD_DIM)
            dkp_ref[:, ks] = dk[0:ATT_BLOCK]
            dkc_ref[:, ks] = dk[ATT_BLOCK:]
            dkp_ref[:, vs] = dv[0:ATT_BLOCK]
            dkc_ref[:, vs] = dv[ATT_BLOCK:]
        dsink_ref[...] += dsink

    out_specs = [blk_q, blk_kv, blk_kv, pl.BlockSpec((1, n_q), lambda n: (0, 0))]
    out_shape = [jax.ShapeDtypeStruct((rows, dq_w), F32), jax.ShapeDtypeStruct((rows, dkv_w), F32),
                 jax.ShapeDtypeStruct((rows, dkv_w), F32), jax.ShapeDtypeStruct((1, n_q), F32)]
    return pl.pallas_call(body, name=name, grid=(rows // ATT_BLOCK,), in_specs=in_specs, out_specs=out_specs,
                          out_shape=out_shape, compiler_params=_cp("arbitrary"))(sinks, qkv, qkv, qkv, dout, lse)


def _dqkv_assemble(dq, dkc, dkp, *, name):
    rows, dq_w = dq.shape
    dkv_w = dkc.shape[1]
    nb = rows // ATT_BLOCK

    def body(dq_ref, dkc_ref, dkp_ref, o_ref, db_ref):
        n = pl.program_id(0)

        @pl.when(n == 0)
        def _():
            db_ref[...] = jnp.zeros_like(db_ref)

        dqv = dq_ref[...]
        dkv = dkc_ref[...] + jnp.where(n == nb - 1, 0.0, dkp_ref[...])
        o_ref[:, 0:dq_w] = dqv.astype(BF16)
        o_ref[:, dq_w:dq_w + dkv_w] = dkv.astype(BF16)
        db_ref[:, 0:dq_w] += _rowsum(dqv)
        db_ref[:, dq_w:dq_w + dkv_w] += _rowsum(dkv)

    width = dq_w + dkv_w
    in_specs = [pl.BlockSpec((ATT_BLOCK, dq_w), lambda n: (n, 0)), pl.BlockSpec((ATT_BLOCK, dkv_w), lambda n: (n, 0)),
                pl.BlockSpec((ATT_BLOCK, dkv_w), lambda n: (jnp.minimum(n + 1, nb - 1), 0))]
    out_specs = [pl.BlockSpec((ATT_BLOCK, width), lambda n: (n, 0)), pl.BlockSpec((1, width), lambda n: (0, 0))]
    out_shape = [jax.ShapeDtypeStruct((rows, width), BF16), jax.ShapeDtypeStruct((1, width), F32)]
    return pl.pallas_call(body, name=name, grid=(nb,), in_specs=in_specs, out_specs=out_specs, out_shape=out_shape,
                          compiler_params=_cp("arbitrary"))(dq, dkc, dkp)


def _row_tile(r, c):
    budget = 2 * 1024 * 1024 // (4 * c)
    for cand in (1024, 512, 256, 128, 64, 32, 16):
        if cand <= budget and r % cand == 0:
            return cand
    return r


def _octo_sum(own, recv, place, dest, lead, *, name):
    _, _, r, c = own.shape
    t = _row_tile(r, c)
    lead_idx, buf_shape = lead

    def body(place_ref, own_ref, *rest):
        o_ref = rest[7] if dest is None else rest[8]
        acc = own_ref[...].astype(F32)
        for k in range(7):
            acc = acc + rest[k][...].astype(F32)
        o_ref[...] = acc

    def peer(mask):
        return pl.BlockSpec((None, t, c), lambda i, pr: (pr[2] ^ mask, i, 0))

    if lead_idx is None:
        o_spec = pl.BlockSpec((None, t, c), lambda i, pr: (pr[1], i, 0))
    else:
        o_spec = pl.BlockSpec((None, None, t, c), lambda i, pr: (lead_idx, pr[1], i, 0))
    in_specs = [pl.BlockSpec((None, None, t, c), lambda i, pr: (pr[0], pr[1], i, 0))] + [peer(m) for m in range(1, 8)]
    args = [place, own] + [recv] * 7
    aliases = {}
    if dest is not None:
        in_specs.append(HBM)
        args.append(dest)
        aliases = {9: 0}
    grid_spec = pltpu.PrefetchScalarGridSpec(num_scalar_prefetch=1, grid=(r // t,), in_specs=in_specs, out_specs=o_spec)
    return pl.pallas_call(body, name=name, grid_spec=grid_spec, out_shape=jax.ShapeDtypeStruct(buf_shape, F32),
                          input_output_aliases=aliases, compiler_params=_cp("parallel"))(*args)


def _adamw_math(w, g, m, v):
    nm = ADAM_B1 * m + (1.0 - ADAM_B1) * g
    nv = ADAM_B2 * v + (1.0 - ADAM_B2) * (g * g)
    m_hat = nm / (1.0 - ADAM_B1 ** ADAM_STEP)
    v_hat = nv / (1.0 - ADAM_B2 ** ADAM_STEP)
    return -ADAM_LR * (m_hat / (jnp.sqrt(v_hat) + ADAM_EPS) + ADAM_WD * w), nm, nv


def _adamw(w, g, m, v, *, name):
    r, c = w.shape
    t = _row_tile(r, c)

    def body(w_ref, g_ref, m_ref, v_ref, d_ref, nm_ref, nv_ref, go_ref):
        gv = g_ref[...]
        d_ref[...], nm_ref[...], nv_ref[...] = _adamw_math(w_ref[...], gv, m_ref[...], v_ref[...])
        go_ref[...] = gv

    blk = pl.BlockSpec((t, c), lambda i: (i, 0))
    sds = jax.ShapeDtypeStruct((r, c), F32)
    return pl.pallas_call(body, name=name, grid=(r // t,), in_specs=[blk] * 4, out_specs=[blk] * 4,
                          out_shape=[sds] * 4, compiler_params=_cp("parallel"))(w, g, m, v)


HBM = pl.BlockSpec(memory_space=pl.ANY)


def _place():
    x, y, c = lax.axis_index("x"), lax.axis_index("y"), lax.axis_index("c")
    chips = [(1 - x, y), (x, 1 - y), (1 - x, 1 - y)]
    return x, y, c, 2 * x + y, (x, y, 1 - c), chips


def _rcopy(src, dst, ssem, rsem, dev):
    return pltpu.make_async_remote_copy(src_ref=src, dst_ref=dst, send_sem=ssem, recv_sem=rsem, device_id=dev,
                                        device_id_type=MESH)


HBM_ONLY = pl.BlockSpec(memory_space=pltpu.HBM)
SEM = pl.BlockSpec(memory_space=pltpu.SEMAPHORE)


def _peers():
    x, y, c = lax.axis_index("x"), lax.axis_index("y"), lax.axis_index("c")
    out = []
    for mask in range(1, 8):
        px = 1 - x if mask & 4 else x
        py = 1 - y if mask & 2 else y
        pc = 1 - c if mask & 1 else c
        out.append(((px, py, pc), 2 * px + py, pc, 4 * px + 2 * py + pc))
    return 4 * x + 2 * y + c, out


def _reduce_start(grads, lands, after, *, name, whole=False):
    nt = len(grads)

    def body(*refs):
        ssems, rsems = refs[2 * nt + 1:3 * nt + 1], refs[3 * nt + 1:4 * nt + 1]
        g_out, l_out, token = refs[4 * nt + 1:5 * nt + 1], refs[5 * nt + 1:6 * nt + 1], refs[6 * nt + 1]
        me, peers = _peers()
        for t in range(nt):
            for k, (dev, chip, core, _) in enumerate(peers):
                src = g_out[t] if whole else g_out[t].at[chip, core]
                _rcopy(src, l_out[t].at[me], ssems[t].at[k], rsems[t].at[k], dev).start()
        token[...] = jnp.zeros_like(token)

    sems = [pltpu.SemaphoreType.DMA((7,))] * (2 * nt)
    out_shape = (sems + [pltpu.HBM(g.shape, g.dtype) for g in grads] + [pltpu.HBM(l.shape, l.dtype) for l in lands]
                 + [jax.ShapeDtypeStruct((8, LANES), F32)])
    res = pl.pallas_call(
        body, name=name, in_specs=[HBM_ONLY] * (2 * nt + 1),
        out_specs=[SEM] * (2 * nt) + [HBM_ONLY] * (2 * nt) + [pl.BlockSpec(memory_space=pltpu.VMEM)], out_shape=out_shape,
        input_output_aliases={t: 2 * nt + t for t in range(2 * nt)},
        compiler_params=pltpu.CompilerParams(has_side_effects=DATAFLOW),
    )(*[pltpu.with_memory_space_constraint(a, pltpu.HBM) for a in list(grads) + list(lands) + [after]])
    return res[:nt], res[nt:2 * nt], res[2 * nt:3 * nt], res[3 * nt:4 * nt], res[4 * nt]


def _reduce_wait(grads, lands, ssems, rsems, after, *, name, whole=False):
    nt = len(grads)

    def body(*refs):
        ssem_refs, rsem_refs = refs[2 * nt:3 * nt], refs[3 * nt:4 * nt]
        g_out, l_out = refs[4 * nt + 1:5 * nt + 1], refs[5 * nt + 1:6 * nt + 1]
        me, peers = _peers()
        for t in range(nt):
            for k, (dev, chip, core, _) in enumerate(peers):
                src = g_out[t] if whole else g_out[t].at[chip, core]
                _rcopy(src, l_out[t].at[me], ssem_refs[t].at[k], rsem_refs[t].at[k], dev).wait_send()
        for t in range(nt):
            for k, (dev, _, _, idx) in enumerate(peers):
                slot = l_out[t].at[idx]
                _rcopy(slot, slot, ssem_refs[t].at[k], rsem_refs[t].at[k], dev).wait_recv()

    res = pl.pallas_call(
        body, name=name, in_specs=[HBM_ONLY] * (2 * nt) + [SEM] * (2 * nt) + [HBM_ONLY], out_specs=[HBM_ONLY] * (2 * nt),
        out_shape=[pltpu.HBM(a.shape, a.dtype) for a in list(grads) + list(lands)],
        input_output_aliases={t: t for t in range(2 * nt)},
        compiler_params=pltpu.CompilerParams(has_side_effects=DATAFLOW),
    )(*grads, *lands, *ssems, *rsems, pltpu.with_memory_space_constraint(after, pltpu.HBM))
    return list(res[:nt]), list(res[nt:])
DATAFLOW = pltpu.SideEffectType.DATAFLOW_SIDE_EFFECTING


def _gather_now(bufs, *, name):
    nt = len(bufs)

    def body(*refs):
        outs = refs[nt:2 * nt]
        ssem, rsem = refs[2 * nt:]
        x, y, c, q, sib, chips = _place()
        sends = []
        for t in range(nt):
            for j, (px, py) in enumerate(chips):
                mine = outs[t].at[q, c]
                cp = _rcopy(mine, mine, ssem.at[t, j], rsem.at[t, j], (px, py, c))
                cp.start()
                sends.append(cp)
        for t in range(nt):
            for j, (px, py) in enumerate(chips):
                landed = outs[t].at[2 * px + py, c]
                _rcopy(landed, landed, ssem.at[t, j], rsem.at[t, j], (px, py, c)).wait_recv()
                cp = _rcopy(landed, landed, ssem.at[t, 3 + j], rsem.at[t, 3 + j], sib)
                cp.start()
                sends.append(cp)
        for t in range(nt):
            for j, (px, py) in enumerate(chips):
                passed = outs[t].at[2 * px + py, 1 - c]
                _rcopy(passed, passed, ssem.at[t, 3 + j], rsem.at[t, 3 + j], sib).wait_recv()
        for cp in sends:
            cp.wait_send()

    out_shape = [jax.ShapeDtypeStruct(b.shape, b.dtype) for b in bufs]
    return pl.pallas_call(
        body, name=name, in_specs=[HBM] * nt, out_specs=[HBM] * nt, out_shape=out_shape,
        input_output_aliases={t: t for t in range(nt)},
        scratch_shapes=[pltpu.SemaphoreType.DMA((nt, 6)), pltpu.SemaphoreType.DMA((nt, 6))],
    )(*bufs)


def _gather_start(bufs, half, after, *, name):
    nt = len(bufs)

    def body(*refs):
        ssems, rsems, outs = refs[nt + 1:2 * nt + 1], refs[2 * nt + 1:3 * nt + 1], refs[3 * nt + 1:4 * nt + 1]
        x, y, c, q, sib, chips = _place()
        for t in range(nt):
            for j, (px, py) in enumerate(chips):
                mine = outs[t].at[q, c] if half[t] else outs[t].at[q]
                _rcopy(mine, mine, ssems[t].at[j], rsems[t].at[j], (px, py, c)).start()

    sems = [pltpu.SemaphoreType.DMA((3,))] * (2 * nt)
    out_shape = sems + [pltpu.HBM(b.shape, b.dtype) for b in bufs]
    res = pl.pallas_call(
        body, name=name, in_specs=[HBM_ONLY] * (nt + 1), out_specs=[SEM] * (2 * nt) + [HBM_ONLY] * nt, out_shape=out_shape,
        input_output_aliases={t: 2 * nt + t for t in range(nt)},
        compiler_params=pltpu.CompilerParams(has_side_effects=DATAFLOW),
    )(*[pltpu.with_memory_space_constraint(b, pltpu.HBM) for b in list(bufs) + [after]])
    return res[:nt], res[nt:2 * nt], res[2 * nt:]


def _gather_wait(bufs, half, ssems, rsems, after, *, name):
    nt = len(bufs)

    def body(*refs):
        ssem_refs, rsem_refs = refs[nt:2 * nt], refs[2 * nt:3 * nt]
        outs = refs[3 * nt + 1:]
        x, y, c, q, sib, chips = _place()
        for t in range(nt):
            for j, (px, py) in enumerate(chips):
                mine = outs[t].at[q, c] if half[t] else outs[t].at[q]
                _rcopy(mine, mine, ssem_refs[t].at[j], rsem_refs[t].at[j], (px, py, c)).wait_send()
        for t in range(nt):
            for j, (px, py) in enumerate(chips):
                theirs = outs[t].at[2 * px + py, c] if half[t] else outs[t].at[2 * px + py]
                _rcopy(theirs, theirs, ssem_refs[t].at[j], rsem_refs[t].at[j], (px, py, c)).wait_recv()

    res = pl.pallas_call(
        body, name=name, in_specs=[HBM_ONLY] * nt + [SEM] * (2 * nt) + [HBM], out_specs=[HBM_ONLY] * nt,
        out_shape=[pltpu.HBM(b.shape, b.dtype) for b in bufs], input_output_aliases={t: t for t in range(nt)},
        compiler_params=pltpu.CompilerParams(has_side_effects=DATAFLOW),
    )(*bufs, *ssems, *rsems, after)
    return list(res)


def _sibling_swap(bufs, *, name):
    nt = len(bufs)

    def body(*refs):
        outs = refs[nt:2 * nt]
        ssem, rsem = refs[2 * nt:]
        x, y, c, q, sib, chips = _place()
        sends = []
        for t in range(nt):
            for j, (px, py) in enumerate(chips):
                held = outs[t].at[2 * px + py, c]
                cp = _rcopy(held, held, ssem.at[t, j], rsem.at[t, j], sib)
                cp.start()
                sends.append(cp)
        for t in range(nt):
            for j, (px, py) in enumerate(chips):
                other = outs[t].at[2 * px + py, 1 - c]
                _rcopy(other, other, ssem.at[t, j], rsem.at[t, j], sib).wait_recv()
        for cp in sends:
            cp.wait_send()

    return pl.pallas_call(
        body, name=name, in_specs=[HBM] * nt, out_specs=[HBM] * nt,
        out_shape=[jax.ShapeDtypeStruct(b.shape, b.dtype) for b in bufs], input_output_aliases={t: t for t in range(nt)},
        scratch_shapes=[pltpu.SemaphoreType.DMA((nt, 3)), pltpu.SemaphoreType.DMA((nt, 3))],
    )(*bufs)


def _sibling_share(bufs, layout, *, name):
    no = len(bufs)
    nt = len(layout)

    def body(*refs):
        outs = refs[no:2 * no]
        ssem, rsem = refs[2 * no:]
        x, y, c, q, sib, chips = _place()

        def slot(t, half):
            o, lead = layout[t]
            return outs[o].at[half] if lead is None else outs[o].at[lead, half]

        sends = []
        for t in range(nt):
            cp = _rcopy(slot(t, c), slot(t, c), ssem.at[t], rsem.at[t], sib)
            cp.start()
            sends.append(cp)
        for t in range(nt):
            _rcopy(slot(t, 1 - c), slot(t, 1 - c), ssem.at[t], rsem.at[t], sib).wait_recv()
        for cp in sends:
            cp.wait_send()

    out_shape = [jax.ShapeDtypeStruct(b.shape, b.dtype) for b in bufs]
    return pl.pallas_call(
        body, name=name, in_specs=[HBM] * no, out_specs=[HBM] * no, out_shape=out_shape,
        input_output_aliases={o: o for o in range(no)},
        scratch_shapes=[pltpu.SemaphoreType.DMA((nt,)), pltpu.SemaphoreType.DMA((nt,))],
    )(*bufs)


def _small_tail(local, params, exchange, *, name):
    (cwa, cba, ga, ba, gb, bb, dws, dsb, dbq, dsk, cwf0, cbf0, cwf1, cbf1,
     g00, g01, g10, g11, b00, b01, b10, b11, err) = local
    n_local = len(local)
    kw, wa = cwa.shape
    ng = dws.shape[0]
    nqkv = dbq.shape[1]
    nsk = dsk.shape[1]
    f = cwf0.shape[2]
    dm = err.shape[1]
    row_vec = 8 * (-(-kw // 8))
    shapes = [(row_vec + 8, wa), (ng * B_CHUNK + 8, B_CHUNK), (8, nqkv), (2, 2, 8, f), (16, dm)]
    n_grp = len(shapes)
    flat_params = [a for triple in params for a in triple]
    n_par = len(params)

    def pack_body(*refs):
        loc = refs[:n_local]
        grp = refs[n_local:n_local + n_grp]

        for gr in grp:
            gr[...] = jnp.zeros_like(gr)
        a, b, c, dd, e = grp
        a[0:kw, :] = loc[0][...]
        for k in range(5):
            a[row_vec + k:row_vec + k + 1, :] = loc[1 + k][...]
        for g in range(ng):
            b[g * B_CHUNK:(g + 1) * B_CHUNK, :] = loc[6][g]
        b[ng * B_CHUNK:ng * B_CHUNK + ng, :] = loc[7][...]
        c[0:1, :] = loc[8][...]
        c[1:2, 0:nsk] = loc[9][...]
        for l in range(2):
            for s in range(2):
                dd[l, s, 0:3, :] = loc[10 + 2 * l][s]
                dd[l, s, 3:4, :] = loc[11 + 2 * l][s]
        for k in range(9):
            e[k:k + 1, :] = loc[14 + k][...]

    vm = pl.BlockSpec(memory_space=pltpu.VMEM)
    groups = pl.pallas_call(
        pack_body, name=name + "_pack", in_specs=[vm] * n_local, out_specs=[vm] * n_grp,
        out_shape=[jax.ShapeDtypeStruct(s, F32) for s in shapes],
        compiler_params=pltpu.CompilerParams(vmem_limit_bytes=VMEM_LIMIT),
    )(*local)
    groups, landed = exchange(groups)

    def adamw_body(*refs):
        own = refs[:n_grp]
        land = refs[n_grp:2 * n_grp]
        par = refs[2 * n_grp:2 * n_grp + 3 * n_par]
        outs = refs[2 * n_grp + 3 * n_par:2 * n_grp + 7 * n_par]
        loss_ref = refs[2 * n_grp + 7 * n_par]
        tot = refs[2 * n_grp + 7 * n_par + 1:]
        x, y = lax.axis_index("x"), lax.axis_index("y")
        q = 2 * x + y
        me = 4 * x + 2 * y + lax.axis_index("c")
        for gi in range(n_grp):
            acc = None
            for dv in range(8):
                term = jnp.where(me == dv, own[gi][...], land[gi][dv])
                acc = term if acc is None else acc + term
            tot[gi][...] = acc
        ta, tb, tc, td, te = tot

        def mine(piece):
            out = piece(0)
            for k in range(1, 4):
                out = jnp.where(q == k, piece(k), out)
            return out

        def update(p, grad, index=None):
            at = (lambda r: r[...]) if index is None else (lambda r: r[index])
            w_ref, m_ref, v_ref = par[3 * p:3 * p + 3]
            g_ref, d_ref, nm_ref, nv_ref = outs[4 * p:4 * p + 4]
            delta, nm, nv = _adamw_math(at(w_ref), grad, at(m_ref), at(v_ref))
            for r, val in ((g_ref, grad), (d_ref, delta), (nm_ref, nm), (nv_ref, nv)):
                if index is None:
                    r[...] = val
                else:
                    r[index] = val

        wq = wa // 4
        update(0, mine(lambda k: ta[0:kw, k * wq:(k + 1) * wq]), (0,))
        for k in range(5):
            update(1 + k, ta[row_vec + k:row_vec + k + 1, :])
        for g in range(ng):
            update(6, tb[g * B_CHUNK:(g + 1) * B_CHUNK, :], (0, g))
        update(7, tb[ng * B_CHUNK:ng * B_CHUNK + ng, :], (0,))
        nq4 = nqkv // 4
        update(8, mine(lambda k: tc[0:1, k * nq4:(k + 1) * nq4]))
        update(9, tc[1:2, 0:nsk])
        fh = f // 2
        for l in range(2):
            update(10, mine(lambda k: td[l, k // 2, 0:3, (k % 2) * fh:(k % 2 + 1) * fh]), (l,))
            update(11, jnp.concatenate([td[l, 0, 3:4, :], td[l, 1, 3:4, :]], axis=1), (slice(l, l + 1),))
        dq4 = dm // 4
        for i in range(2):
            for j in range(2):
                for p, base in ((12, 0), (13, 4)):
                    row = base + 2 * i + j
                    update(p, mine(lambda k: te[row:row + 1, k * dq4:(k + 1) * dq4]), (i, slice(j, j + 1)))
        loss_ref[...] = (0.5 / dm) * jnp.sum(te[8:9, :], axis=1, keepdims=True)

    out_shape = []
    for w, _, _ in params:
        out_shape += [jax.ShapeDtypeStruct(w.shape, F32)] * 4
    out_shape.append(jax.ShapeDtypeStruct((1, 1), F32))
    res = pl.pallas_call(
        adamw_body, name=name + "_adamw", in_specs=[vm] * (2 * n_grp + 3 * n_par), out_specs=[vm] * len(out_shape),
        out_shape=out_shape, scratch_shapes=[pltpu.VMEM(s, F32) for s in shapes],
        compiler_params=pltpu.CompilerParams(vmem_limit_bytes=VMEM_LIMIT),
    )(*groups, *landed, *flat_params)
    return [res[4 * p:4 * p + 4] for p in range(n_par)], res[-1]


def _pack(arrays, rows_multiple):
    flat = jnp.concatenate([a.reshape(-1) for a in arrays])
    rows = -(-flat.shape[0] // LANES)
    rows = -(-rows // rows_multiple) * rows_multiple
    flat = jnp.pad(flat, (0, rows * LANES - flat.shape[0]))
    return flat.reshape(rows, LANES)


def _unshard_cols(stacked):
    moved = jnp.moveaxis(stacked, 0, -2)
    return moved.reshape(moved.shape[:-2] + (4 * stacked.shape[-1],))


def kernel(x, ab_w_in, a_conv_w, a_conv_b, a_norm_g, a_norm_b, b_norm_g, b_norm_b, b_spatial_w, b_spatial_b, ab_w_out, c_w_qkv, c_b_qkv, c_sinks, c_w_o, ffn_w_up, ffn_conv_w, ffn_conv_b, ffn_w_down, ln_g, ln_b, loss_target, m_ab_w_in, m_a_conv_w, m_a_conv_b, m_a_norm_g, m_a_norm_b, m_b_norm_g, m_b_norm_b, m_b_spatial_w, m_b_spatial_b, m_ab_w_out, m_c_w_qkv, m_c_b_qkv, m_c_sinks, m_c_w_o, m_ffn_w_up, m_ffn_conv_w, m_ffn_conv_b, m_ffn_w_down, m_ln_g, m_ln_b, v_ab_w_in, v_a_conv_w, v_a_conv_b, v_a_norm_g, v_a_norm_b, v_b_norm_g, v_b_norm_b, v_b_spatial_w, v_b_spatial_b, v_ab_w_out, v_c_w_qkv, v_c_b_qkv, v_c_sinks, v_c_w_o, v_ffn_w_up, v_ffn_conv_w, v_ffn_conv_b, v_ffn_w_down, v_ln_g, v_ln_b):
    rows, d = x.shape[1], x.shape[2]
    depth = ln_g.shape[0]
    assert depth == 2 and x.shape[0] == 1
    alpha = (2.0 * depth) ** 0.25
    f = ffn_w_down.shape[1] * 4
    n_q = c_sinks.shape[1]
    q_idx = 2 * lax.axis_index("x") + lax.axis_index("y")
    c_idx = lax.axis_index("c")
    xs, tgt = x[0], loss_target[0]

    def own_slot(part):
        buf = lax.empty((4,) + part.shape, part.dtype)
        return lax.dynamic_update_slice(buf, part[None], (q_idx, 0, 0, 0))

    def halves(wm):
        return own_slot(wm.astype(BF16).reshape((2, wm.shape[0] // 2) + wm.shape[1:]))

    small_sharded = [a_conv_w[0], c_b_qkv[0], ffn_conv_w, ln_g, ln_b]
    small_pack = _pack(small_sharded, 16)
    bufs = [halves(ab_w_in[0]), own_slot(small_pack.reshape(2, small_pack.shape[0] // 2, LANES)), halves(ab_w_out[0]),
            halves(ffn_w_up[0]), halves(ffn_w_down[0]), halves(c_w_qkv[0]), halves(c_w_o[0]),
            halves(ffn_w_up[1]), halves(ffn_w_down[1])]
    whole = lambda g: g.reshape(4, 2 * g.shape[2], g.shape[3])
    n_now = 2
    first_two = _gather_now(bufs[:n_now], name="gather_now")
    w_in, small_all = [whole(g) for g in first_two]
    later = bufs[n_now:]
    half = [True, True] + [False] * (len(later) - 2)
    ssems, rsems, started = _gather_start(later, half, first_two[1], name="gather_start")

    def arrive(idx, after, tag):
        idx = [i - n_now for i in idx]
        halved = [half[i] for i in idx]
        got = _gather_wait([started[i] for i in idx], halved, [ssems[i] for i in idx], [rsems[i] for i in idx], after,
                           name=f"gather_wait_{tag}")
        if all(halved):
            got = _sibling_swap(got, name=f"gather_swap_{tag}")
        return [whole(g) for g in got]

    small_all = small_all.reshape(4, -1)
    sh_shapes = [s.shape for s in small_sharded]
    pieces, pos = [], 0
    for s in sh_shapes:
        n = math.prod(s)
        pieces.append(_unshard_cols(small_all[:, pos:pos + n].reshape((4,) + s)))
        pos += n
    conv_w_a, b_qkv, conv_w_f, ln_gf, ln_bf = pieces

    tril = jnp.tril(jnp.ones((B_CHUNK, B_CHUNK), F32))
    ws = (b_spatial_w[0] * tril).astype(BF16)
    wst = jnp.swapaxes(ws, 1, 2)
    sbb = jnp.broadcast_to(b_spatial_b[0][:, :, None], b_spatial_w[0].shape)
    mix_vecs = [a_conv_b, a_norm_g, a_norm_b, b_norm_g, b_norm_b]
    cw_f = [jnp.swapaxes(conv_w_f[l].reshape(3, 2, f), 0, 1) for l in range(depth)]
    cb_f = [ffn_conv_b[l].reshape(2, 1, f) for l in range(depth)]
    lng = lambda i, j: ln_gf[i, j].reshape(1, d)
    lnb = lambda i, j: ln_bf[i, j].reshape(1, d)
    sinks = c_sinks[0]

    w_up, w_down = [None, None], [None, None]

    def ffn_fwd(xin, l):
        w_up[l], = arrive([3 + 4 * l], xin, f"up{l}")
        hf, fact = _ffn_up_fwd(xin, w_up[l], cw_f[l], cb_f[l], name=f"ffn{l}_up")
        w_down[l] = arrive([4 + 4 * l], fact, f"down{l}")[0].reshape(-1, d)
        out = _matmul(fact, w_down[l], name=f"ffn{l}_down", tm=512, tn=1024, tk=2816)
        return hf, fact, out

    h0, ab, a2 = _mixer_fwd(xs, w_in, conv_w_a, *mix_vecs, ws, sbb, name="mix_fwd")
    w_out = arrive([2], ab, "out")[0].reshape(-1, d)
    mix = _matmul(ab, w_out, name="mix_out", tm=1024, tn=1024, tk=1024)
    x1 = _add_ln_fwd(xs, mix, lng(0, 0), lnb(0, 0), alpha, name="ln00")
    hf0, f0, ffn0 = ffn_fwd(x1, 0)
    x2 = _add_ln_fwd(x1, ffn0, lng(0, 1), lnb(0, 1), alpha, name="ln01")
    w_qkv = _unshard_cols(arrive([5], x2, "qkv")[0])
    qkv = _matmul(x2, w_qkv, name="att_qkv", tm=1024, tn=w_qkv.shape[1], tk=1024, bias=b_qkv.reshape(1, -1))
    ao, lse = _attn_fwd(qkv, sinks, name="att_core")
    w_o = arrive([6], ao, "o")[0].reshape(-1, d)
    att = _matmul(ao, w_o, name="att_out", tm=1024, tn=1024, tk=1024)
    x3 = _add_ln_fwd(x2, att, lng(1, 0), lnb(1, 0), alpha, name="ln10")
    hf1, f1, ffn1 = ffn_fwd(x3, 1)
    sq_err, dy = _add_ln_loss(x3, ffn1, lng(1, 1), lnb(1, 1), tgt, alpha, name="ln11_loss")

    def owner_view(g):
        if g.ndim == 3:
            return g.reshape(4, 2, g.shape[1] // 2, g.shape[2])
        return g.reshape(4, 2, g.shape[0] // 8, g.shape[1])

    in_flight = []

    def send_grads(tag, grads, after):
        lands = [lax.empty((8,) + g.shape[2:], BF16) for g in grads]
        ss, rs, g_thru, l_thru, token = _reduce_start(grads, lands, after, name=f"reduce_start_{tag}")
        in_flight.append((tag, g_thru, l_thru, ss, rs))
        return token[0:1, 0:1]

    def ffn_bwd(dz, xin, hf, fact, l):
        d_wdown = _matmul(fact, dz, name=f"ffn{l}_down_dw", ta=True, tm=1408, tn=1024, tk=2048, out_dtype=BF16)
        dfa = _matmul(dz, w_down[l], name=f"ffn{l}_down_dx", tb=True, tm=1024, tn=1408, tk=1024, out_dtype=BF16)
        dx_parts, d_wup, dcw, dcb = _ffn_up_bwd(hf, dfa, xin, w_up[l], cw_f[l], cb_f[l], name=f"ffn{l}_up_bwd")
        tok = send_grads(f"ffn{l}", [owner_view(d_wup), owner_view(d_wdown)], dcb)
        return [(dx_parts, 1.0), (dz, alpha)], dcw, dcb, tok

    dz, dg11, db11 = _add_ln_bwd([(dy, 1.0)], x3, ffn1, lng(1, 1), alpha, name="ln11_bwd")
    dx3, dcw1, dcb1, tok = ffn_bwd(dz, x3, hf1, f1, 1)
    dz, dg10, db10 = _add_ln_bwd(dx3, x2, att, lng(1, 0) + tok, alpha, name="ln10_bwd")
    d_wo = _matmul(ao, dz, name="att_out_dw", ta=True, tm=1024, tn=1024, tk=1024, out_dtype=BF16)
    dao = _matmul(dz, w_o, name="att_out_dx", tb=True, tm=1024, tn=1024, tk=1024)
    dq, dkc, dkp, d_sinks = _attn_bwd(qkv, dao, lse, sinks, name="att_core_bwd")
    dqkv, d_bqkv = _dqkv_assemble(dq, dkc, dkp, name="att_dqkv")
    d_wqkv = _matmul(x2, dqkv, name="att_qkv_dw", ta=True, tm=1024, tn=dqkv.shape[1], tk=1024, out_dtype=BF16)
    d_wqkv_st = jnp.moveaxis(d_wqkv.reshape(d_wqkv.shape[0], 4, -1), 1, 0)
    tok = send_grads("att", [owner_view(d_wqkv_st), owner_view(d_wo)], d_bqkv)
    dx2 = _matmul(dqkv, w_qkv, name="att_qkv_dx", tb=True, tm=1024, tn=1024, tk=dqkv.shape[1], addend=(dz, alpha))
    dz, dg01, db01 = _add_ln_bwd([(dx2, 1.0)], x1, ffn0, lng(0, 1) + tok, alpha, name="ln01_bwd")
    dx1, dcw0, dcb0, tok = ffn_bwd(dz, x1, hf0, f0, 0)
    dz, dg00, db00 = _add_ln_bwd(dx1, xs, mix, lng(0, 0) + tok, alpha, name="ln00_bwd")
    d_wout = _matmul(ab, dz, name="mix_out_dw", ta=True, tm=1024, tn=1024, tk=1024, out_dtype=BF16)
    dab = _matmul(dz, w_out, name="mix_out_dx", tb=True, tm=1024, tn=1024, tk=1024)
    grad_x, d_win, d_cwa, d_cba, d_ga, d_ba, d_gb, d_bb, d_ws, d_sb = _mixer_bwd(
        h0, a2, dab, xs, w_in, dz, alpha, conv_w_a, *mix_vecs[1:], ws, wst, sbb, tril, name="mix_bwd")

    small_w = [a_conv_w, a_conv_b, a_norm_g, a_norm_b, b_norm_g, b_norm_b, b_spatial_w, b_spatial_b, c_b_qkv,
               c_sinks, ffn_conv_w, ffn_conv_b, ln_g, ln_b]
    small_m = [m_a_conv_w, m_a_conv_b, m_a_norm_g, m_a_norm_b, m_b_norm_g, m_b_norm_b, m_b_spatial_w, m_b_spatial_b,
               m_c_b_qkv, m_c_sinks, m_ffn_conv_w, m_ffn_conv_b, m_ln_g, m_ln_b]
    small_v = [v_a_conv_w, v_a_conv_b, v_a_norm_g, v_a_norm_b, v_b_norm_g, v_b_norm_b, v_b_spatial_w, v_b_spatial_b,
               v_c_b_qkv, v_c_sinks, v_ffn_conv_w, v_ffn_conv_b, v_ln_g, v_ln_b]
    local = [d_cwa, d_cba, d_ga, d_ba, d_gb, d_bb, d_ws, d_sb, d_bqkv, d_sinks, dcw0, dcb0, dcw1, dcb1,
             dg00, dg01, dg10, dg11, db00, db01, db10, db11, sq_err]
    place = jnp.stack([q_idx, c_idx, 4 * lax.axis_index("x") + 2 * lax.axis_index("y") + c_idx]).astype(jnp.int32)
    where = {"mix": [(0, None), (1, None)], "att": [(2, None), (3, None)], "ffn0": [(4, 0), (5, 0)], "ffn1": [(4, 1), (5, 1)]}
    big_w = [ab_w_in, ab_w_out, c_w_qkv, c_w_o, ffn_w_up, ffn_w_down]
    big_m = [m_ab_w_in, m_ab_w_out, m_c_w_qkv, m_c_w_o, m_ffn_w_up, m_ffn_w_down]
    big_v = [v_ab_w_in, v_ab_w_out, v_c_w_qkv, v_c_w_o, v_ffn_w_up, v_ffn_w_down]
    big_out = [None] * 6

    def finish(tags, after, label):
        bufs, layout = {}, []
        for tag, g_thru, l_thru, ss, rs in in_flight:
            if tag not in tags:
                continue
            own, landed = _reduce_wait(g_thru, l_thru, ss, rs, after, name=f"reduce_wait_{tag}")
            for k, (o, lead) in enumerate(where[tag]):
                piece = own[k].shape[2:]
                shape = (2,) + piece if lead is None else (2, 2) + piece
                bufs[o] = _octo_sum(own[k], landed[k], place, bufs.get(o), (lead, shape), name=f"reduce_sum_{tag}{k}")
                layout.append((o, lead))
        order = sorted(bufs)
        shared = _sibling_share([bufs[o] for o in order], [(order.index(o), lead) for o, lead in layout],
                                name=f"reduce_share_{label}")
        for o, g in zip(order, shared):
            w = big_w[o]
            two_d = lambda a: a.reshape(-1, a.shape[-1])
            outs = _adamw(two_d(w), two_d(g), two_d(big_m[o]), two_d(big_v[o]), name=f"adamw_big{o}")
            big_out[o] = [r.reshape(w.shape) for r in outs]
        return big_out[order[-1]][0]

    def exchange(groups):
        lands = [lax.empty((8,) + g.shape, F32) for g in groups]
        ss, rs, g_thru, l_thru, token = _reduce_start(groups, lands, sq_err, name="small_start", whole=True)
        tok = send_grads("mix", [owner_view(d_win), owner_view(d_wout)], after=token)
        done = finish(("ffn1", "att", "ffn0"), d_ws + tok, "early")
        return _reduce_wait(g_thru, l_thru, ss, rs, done, name="small_wait", whole=True)

    small_out, loss = _small_tail(local, list(zip(small_w, small_m, small_v)), exchange, name="small_tail")
    loss = loss[0, 0]
    small_g = [o[0] for o in small_out]
    sm_delta = [o[1] for o in small_out]
    sm_m = [o[2] for o in small_out]
    sm_v = [o[3] for o in small_out]
    finish(("mix",), sm_delta[6], "mix")

    order_big = {0: 0, 9: 1, 10: 2, 13: 3, 14: 4, 17: 5}
    order_small = {1: 0, 2: 1, 3: 2, 4: 3, 5: 4, 6: 5, 7: 6, 8: 7, 11: 8, 12: 9, 15: 10, 16: 11, 18: 12, 19: 13}
    grads, deltas, new_m, new_v = [], [], [], []
    for pos_w in range(20):
        if pos_w in order_big:
            t = order_big[pos_w]
            grads.append(big_out[t][3])
            deltas.append(big_out[t][0])
            new_m.append(big_out[t][1])
            new_v.append(big_out[t][2])
        else:
            t = order_small[pos_w]
            grads.append(small_g[t])
            deltas.append(sm_delta[t])
            new_m.append(sm_m[t])
            new_v.append(sm_v[t])
    return (loss, grad_x[None], *grads, *deltas, *new_m, *new_v)
```

```python
import math

import jax
import jax.numpy as jnp
from jax import lax
from jax.experimental import pallas as pl
from jax.experimental.pallas import tpu as pltpu

F32 = jnp.float32
BF16 = jnp.bfloat16
MESH = pl.DeviceIdType.MESH

LN_EPS = 1e-5
HEAD_DIM = 64
ATT_BLOCK = 128
Q_PER_KV = 8
A_KERNEL = 31
CONV_HALO = 32
FFN_HALO = 8
BF16_ROWS = 16
B_CHUNK = 128
LANES = 128
MXU_WIDTH = 256
GELU_C = math.sqrt(2.0 / math.pi)
ADAM_LR = 0.001
ADAM_B1 = 0.9
ADAM_B2 = 0.999
ADAM_EPS = 1e-08
ADAM_WD = 0.01
ADAM_STEP = 10
VMEM_LIMIT = 56 * 1024 * 1024


def _cp(*dims):
    return pltpu.CompilerParams(dimension_semantics=dims, vmem_limit_bytes=VMEM_LIMIT)


def _pick(n, prefs):
    for p in prefs:
        if n % p == 0:
            return p
    return n


def _sig(x):
    return 1.0 / (1.0 + jnp.exp(-x))


def _gelu(x):
    t = jnp.tanh(GELU_C * (x + 0.044715 * (x * x * x)))
    return x * (0.5 * (1.0 + t)), t


def _gelu_grad(x, t):
    return 0.5 * (1.0 + t) + 0.5 * x * (1.0 - t * t) * (GELU_C * (1.0 + 3.0 * 0.044715 * x * x))


def _ln_stats(z):
    mu = jnp.mean(z, axis=-1, keepdims=True)
    zc = z - mu
    var = jnp.mean(zc * zc, axis=-1, keepdims=True)
    rstd = lax.rsqrt(var + LN_EPS)
    return zc * rstd, rstd


def _ln_bwd(dxh, xh, rstd):
    return rstd * (dxh - jnp.mean(dxh, axis=-1, keepdims=True) - xh * jnp.mean(dxh * xh, axis=-1, keepdims=True))


def _rowsum(a):
    return jnp.sum(a, axis=0, keepdims=True)


def _lshape(a):
    return (a.shape[0], a.shape[1]) if a.ndim == 2 else (a.shape[1], a.shape[0] * a.shape[2])


def _spec2(arr, blk_r, blk_c, ridx, cidx):
    if len(arr.shape) == 2:
        return pl.BlockSpec((blk_r, blk_c), lambda i, j, k: (ridx(i, j, k), cidx(i, j, k)))
    per = arr.shape[2] // blk_c
    assert arr.shape[2] % blk_c == 0
    return pl.BlockSpec((None, blk_r, blk_c), lambda i, j, k: (cidx(i, j, k) // per, ridx(i, j, k), cidx(i, j, k) % per))


def _matmul(a, b, *, name, ta=False, tb=False, tm, tn, tk, out_dtype=F32, out_stack=None, bias=None, addend=None):
    ar, ac = _lshape(a)
    br, bc = _lshape(b)
    m, kdim = (ac, ar) if ta else (ar, ac)
    n = br if tb else bc
    assert (bc if tb else br) == kdim
    tm, tn, tk = min(tm, m), min(tn, n), min(tk, kdim)
    assert m % tm == 0 and n % tn == 0 and kdim % tk == 0, (name, m, n, kdim, tm, tn, tk)
    nk = kdim // tk
    gi, gj, gk = (lambda i, j, k: i), (lambda i, j, k: j), (lambda i, j, k: k)
    a_spec = _spec2(a, tk, tm, gk, gi) if ta else _spec2(a, tm, tk, gi, gk)
    b_spec = _spec2(b, tn, tk, gj, gk) if tb else _spec2(b, tk, tn, gk, gj)
    if out_stack is None:
        out_sds = jax.ShapeDtypeStruct((m, n), out_dtype)
    else:
        out_sds = jax.ShapeDtypeStruct((out_stack, m, n // out_stack), out_dtype)
    o_spec = _spec2(out_sds, tm, tn, gi, gj)
    in_specs = [a_spec, b_spec]
    args = [a, b]
    if bias is not None:
        in_specs.append(pl.BlockSpec((1, tn), lambda i, j, k: (0, j)))
        args.append(bias)
    scale = None
    if addend is not None:
        add_arr, scale = addend
        in_specs.append(pl.BlockSpec((tm, tn), lambda i, j, k: (i, j)))
        args.append(add_arr)
    use_acc = nk > 1 and out_dtype != F32
    dn = (((0 if ta else 1,), (1 if tb else 0,)), ((), ()))

    def body(*refs):
        a_ref, b_ref = refs[0], refs[1]
        pos = 2
        bias_ref = add_ref = None
        if bias is not None:
            bias_ref = refs[pos]
            pos += 1
        if addend is not None:
            add_ref = refs[pos]
            pos += 1
        o_ref = refs[pos]
        acc_ref = refs[pos + 1] if use_acc else o_ref
        p = lax.dot_general(a_ref[...].astype(BF16), b_ref[...].astype(BF16), dn, preferred_element_type=F32)

        def finish(val):
            if bias_ref is not None:
                val = val + bias_ref[...]
            if add_ref is not None:
                val = val + scale * add_ref[...]
            return val.astype(out_dtype)

        if nk == 1:
            o_ref[...] = finish(p)
        else:
            k = pl.program_id(2)

            @pl.when(k == 0)
            def _():
                acc_ref[...] = p

            @pl.when(k > 0)
            def _():
                acc_ref[...] += p

            if use_acc or bias_ref is not None or add_ref is not None:
                @pl.when(k == nk - 1)
                def _():
                    o_ref[...] = finish(acc_ref[...])

    return pl.pallas_call(
        body, name=name, grid=(m // tm, n // tn, nk), in_specs=in_specs, out_specs=o_spec, out_shape=out_sds,
        scratch_shapes=[pltpu.VMEM((tm, tn), F32)] if use_acc else [],
        compiler_params=_cp("parallel", "parallel", "arbitrary"),
    )(*args)


def _add_ln_fwd(x, s, g, b, alpha, *, name):
    rows, d = x.shape
    t = _pick(rows, (512, 256))

    def body(x_ref, s_ref, g_ref, b_ref, y_ref):
        xh, _ = _ln_stats(alpha * x_ref[...] + s_ref[...])
        y_ref[...] = xh * g_ref[...] + b_ref[...]

    row = pl.BlockSpec((t, d), lambda i: (i, 0))
    vec = pl.BlockSpec((1, d), lambda i: (0, 0))
    return pl.pallas_call(body, name=name, grid=(rows // t,), in_specs=[row, row, vec, vec], out_specs=row,
                          out_shape=jax.ShapeDtypeStruct((rows, d), F32), compiler_params=_cp("parallel"))(x, s, g, b)


def _add_ln_bwd(dy_terms, x, s, g, alpha, *, name):
    rows, d = x.shape
    t = _pick(rows, (512, 256))
    nterm = len(dy_terms)
    scales = [sc for _, sc in dy_terms]
    ranks = [a.ndim for a, _ in dy_terms]

    def body(*refs):
        dy_refs = refs[:nterm]
        x_ref, s_ref, g_ref, dz_ref, dg_ref, db_ref = refs[nterm:]

        @pl.when(pl.program_id(0) == 0)
        def _():
            dg_ref[...] = jnp.zeros_like(dg_ref)
            db_ref[...] = jnp.zeros_like(db_ref)

        dyv = None
        for r, sc, rank in zip(dy_refs, scales, ranks):
            slabs = [r[...]] if rank == 2 else [r[p] for p in range(r.shape[0])]
            for v in slabs:
                v = v if sc == 1.0 else sc * v
                dyv = v if dyv is None else dyv + v
        xh, rstd = _ln_stats(alpha * x_ref[...] + s_ref[...])
        dz_ref[...] = _ln_bwd(dyv * g_ref[...], xh, rstd)
        dg_ref[...] += _rowsum(dyv * xh)
        db_ref[...] += _rowsum(dyv)

    row = pl.BlockSpec((t, d), lambda i: (i, 0))
    vec = pl.BlockSpec((1, d), lambda i: (0, 0))
    vsds = jax.ShapeDtypeStruct((1, d), F32)
    dy_specs = [row if a.ndim == 2 else pl.BlockSpec((a.shape[0], t, d), lambda i: (0, i, 0)) for a, _ in dy_terms]
    return pl.pallas_call(body, name=name, grid=(rows // t,), in_specs=dy_specs + [row, row, vec], out_specs=[row, vec, vec],
                          out_shape=[jax.ShapeDtypeStruct((rows, d), F32), vsds, vsds],
                          compiler_params=_cp("arbitrary"))(*[a for a, _ in dy_terms], x, s, g)


def _add_ln_loss(x, s, g, b, tgt, alpha, *, name):
    rows, d = x.shape
    t = _pick(rows, (512, 256))

    def body(x_ref, s_ref, g_ref, b_ref, t_ref, l_ref, dy_ref):
        @pl.when(pl.program_id(0) == 0)
        def _():
            l_ref[...] = jnp.zeros_like(l_ref)

        xh, _ = _ln_stats(alpha * x_ref[...] + s_ref[...])
        e = (xh * g_ref[...] + b_ref[...]) - t_ref[...]
        l_ref[...] += _rowsum(e * e)
        dy_ref[...] = e * (1.0 / d)

    row = pl.BlockSpec((t, d), lambda i: (i, 0))
    vec = pl.BlockSpec((1, d), lambda i: (0, 0))
    return pl.pallas_call(body, name=name, grid=(rows // t,), in_specs=[row, row, vec, vec, row], out_specs=[vec, row],
                          out_shape=[jax.ShapeDtypeStruct((1, d), F32), jax.ShapeDtypeStruct((rows, d), F32)],
                          compiler_params=_cp("arbitrary"))(x, s, g, b, tgt)


def _col_blocks(width, step):
    return [slice(pos, min(pos + step, width)) for pos in range(0, width, step)]


def _conv3(e, w, b):
    r1 = pltpu.roll(e, 1, 0)
    r2 = pltpu.roll(e, 2, 0)
    return w[0:1, :] * r2 + w[1:2, :] * r1 + w[2:3, :] * e + b, (r2, r1, e)


def _ffn_up_fwd(x, w_up, cw, cb, *, name):
    rows, d = x.shape
    nq, _, tc = w_up.shape
    nj = nq // 2
    f = tc * nj
    tm = _pick(rows, (512, 256))
    blocks = _col_blocks(tc, tc)

    def body(x_ref, wg_ref, wv_ref, cw_ref, cb_ref, hf_ref, f_ref, prev_ref):
        @pl.when(pl.program_id(1) == 0)
        def _():
            prev_ref[...] = jnp.zeros_like(prev_ref)

        xb = x_ref[...].astype(BF16)
        for cs in blocks:
            hc = []
            for s, w_ref in ((0, wg_ref), (1, wv_ref)):
                h = jnp.dot(xb, w_ref[:, cs], preferred_element_type=F32)
                hf_ref[s, :, cs] = h
                e = jnp.concatenate([prev_ref[s, :, cs], h], axis=0)
                prev_ref[s, :, cs] = h[tm - FFN_HALO:]
                y, _ = _conv3(e, cw_ref[s, :, cs], cb_ref[s, :, cs])
                hc.append(y[FFN_HALO:])
            gl, _ = _gelu(hc[0])
            f_ref[:, cs] = (gl * hc[1]).astype(BF16)

    in_specs = [
        pl.BlockSpec((tm, d), lambda j, i: (i, 0)),
        pl.BlockSpec((None, d, tc), lambda j, i: (j, 0, 0)),
        pl.BlockSpec((None, d, tc), lambda j, i: (nj + j, 0, 0)),
        pl.BlockSpec((2, 3, tc), lambda j, i: (0, 0, j)),
        pl.BlockSpec((2, 1, tc), lambda j, i: (0, 0, j)),
    ]
    out_specs = [pl.BlockSpec((2, tm, tc), lambda j, i: (0, i, j)), pl.BlockSpec((tm, tc), lambda j, i: (i, j))]
    out_shape = [jax.ShapeDtypeStruct((2, rows, f), F32), jax.ShapeDtypeStruct((rows, f), BF16)]
    return pl.pallas_call(body, name=name, grid=(nj, rows // tm), in_specs=in_specs, out_specs=out_specs, out_shape=out_shape,
                          scratch_shapes=[pltpu.VMEM((2, FFN_HALO, tc), F32)],
                          compiler_params=_cp("parallel", "arbitrary"))(x, w_up, w_up, cw, cb)


def _ffn_up_bwd(hf, df, x, w_up, cw, cb, *, name):
    _, rows, f = hf.shape
    d = x.shape[1]
    nq, _, tc = w_up.shape
    nj = nq // 2
    tm = _pick(rows, (512, 256))
    hb = tm // FFN_HALO
    once = pl.Buffered(1)
    ni = rows // tm
    last_blk = rows // FFN_HALO - 1
    ext = tm + 2 * FFN_HALO
    tile = slice(FFN_HALO, FFN_HALO + tm)
    blocks = _col_blocks(tc, MXU_WIDTH)

    def body(h_ref, hp_ref, hn_ref, d_ref, dn_ref, x_ref, wg_ref, wv_ref, cw_ref, cb_ref, dx_ref, dw_out_ref, dcw_ref, dcb_ref,
             dw_ref):
        i = pl.program_id(1)
        first = i == 0
        last = i == ni - 1

        @pl.when(first)
        def _():
            dw_ref[...] = jnp.zeros_like(dw_ref)
            dcw_ref[...] = jnp.zeros_like(dcw_ref)
            dcb_ref[...] = jnp.zeros_like(dcb_ref)

        xt = x_ref[...].astype(BF16).T
        dx = None
        for cs in blocks:
            wc = cs.stop - cs.start
            d_next = dn_ref[:, cs].astype(F32)[0:FFN_HALO]
            de = jnp.concatenate([jnp.zeros((FFN_HALO, wc), F32), d_ref[:, cs].astype(F32), jnp.where(last, 0.0, d_next)], axis=0)
            taps, hc = [], []
            for s in range(2):
                e = jnp.concatenate([jnp.where(first, 0.0, hp_ref[s, :, cs]), h_ref[s, :, cs], hn_ref[s, :, cs]], axis=0)
                y, tp = _conv3(e, cw_ref[s, :, cs], cb_ref[s, :, cs])
                hc.append(y)
                taps.append(tp)
            gl, th = _gelu(hc[0])
            dhc = (de * hc[1] * _gelu_grad(hc[0], th), de * gl)
            for s, w_ref in ((0, wg_ref), (1, wv_ref)):
                w = cw_ref[s, :, cs]
                g = dhc[s]
                dh = (w[2:3, :] * g + w[1:2, :] * pltpu.roll(g, ext - 1, 0) + w[0:1, :] * pltpu.roll(g, ext - 2, 0))[tile]
                gt = g[tile]
                for k in range(3):
                    dcw_ref[s, k:k + 1, cs] += _rowsum(gt * taps[s][k][tile])
                dcb_ref[s, :, cs] += _rowsum(gt)
                dhb = dh.astype(BF16)
                part = lax.dot_general(dhb, w_ref[:, cs], (((1,), (1,)), ((), ())), preferred_element_type=F32)
                dx = part if dx is None else dx + part
                dw_ref[s, :, cs] += jnp.dot(xt, dhb, preferred_element_type=F32)
        dx_ref[...] = dx

        @pl.when(last)
        def _():
            dw_out_ref[...] = dw_ref[...].astype(BF16)

    in_specs = [
        pl.BlockSpec((2, tm, tc), lambda j, i: (0, i, j)),
        pl.BlockSpec((2, FFN_HALO, tc), lambda j, i: (0, jnp.maximum(i * hb - 1, 0), j)),
        pl.BlockSpec((2, FFN_HALO, tc), lambda j, i: (0, jnp.minimum((i + 1) * hb, last_blk), j)),
        pl.BlockSpec((tm, tc), lambda j, i: (i, j)),
        pl.BlockSpec((BF16_ROWS, tc), lambda j, i: (jnp.minimum((i + 1) * (tm // BF16_ROWS), rows // BF16_ROWS - 1), j)),
        pl.BlockSpec((tm, d), lambda j, i: (i, 0)),
        pl.BlockSpec((None, d, tc), lambda j, i: (j, 0, 0), pipeline_mode=once),
        pl.BlockSpec((None, d, tc), lambda j, i: (nj + j, 0, 0), pipeline_mode=once),
        pl.BlockSpec((2, 3, tc), lambda j, i: (0, 0, j)),
        pl.BlockSpec((2, 1, tc), lambda j, i: (0, 0, j)),
    ]
    out_specs = [
        pl.BlockSpec((None, tm, d), lambda j, i: (j, i, 0)),
        pl.BlockSpec((2, None, d, tc), lambda j, i: (0, j, 0, 0), pipeline_mode=once),
        pl.BlockSpec((2, 3, tc), lambda j, i: (0, 0, j)),
        pl.BlockSpec((2, 1, tc), lambda j, i: (0, 0, j)),
    ]
    out_shape = [jax.ShapeDtypeStruct((nj, rows, d), F32), jax.ShapeDtypeStruct((2, nj, d, tc), BF16),
                 jax.ShapeDtypeStruct((2, 3, f), F32), jax.ShapeDtypeStruct((2, 1, f), F32)]
    dx, dw, dcw, dcb = pl.pallas_call(body, name=name, grid=(nj, ni), in_specs=in_specs, out_specs=out_specs,
                                      out_shape=out_shape, scratch_shapes=[pltpu.VMEM((2, d, tc), F32)],
                                      compiler_params=_cp("parallel", "arbitrary"))(
        hf, hf, hf, df, df, x, w_up, w_up, cw, cb)
    return dx, dw.reshape(nq, d, tc), dcw, dcb


def _mixer_fwd(x, w_in, cw, cb, ga, ba, gb, bb, ws, sbb, *, name):
    rows, d = x.shape
    _, _, w = w_in.shape
    t = _pick(rows, (256,))
    groups = w // B_CHUNK

    def body(x_ref, win_ref, cw_ref, cb_ref, ga_ref, ba_ref, gb_ref, bb_ref, ws_ref, sb_ref, h_ref, o_ref, a2_ref, prev_ref):
        @pl.when(pl.program_id(0) == 0)
        def _():
            prev_ref[...] = jnp.zeros_like(prev_ref)

        xb = x_ref[...].astype(BF16)
        for s in range(4):
            h_ref[s] = jnp.dot(xb, win_ref[s], preferred_element_type=F32)
        a1 = h_ref[0] * _sig(h_ref[1])
        e = jnp.concatenate([prev_ref[...], a1], axis=0)
        prev_ref[...] = a1[t - CONV_HALO:]
        acc = cw_ref[A_KERNEL - 1:A_KERNEL, :] * e
        for k in range(A_KERNEL - 1):
            acc = acc + cw_ref[k:k + 1, :] * pltpu.roll(e, A_KERNEL - 1 - k, 0)
        a2 = acc[CONV_HALO:] + cb_ref[...]
        a2_ref[...] = a2
        xh, _ = _ln_stats(a2)
        a3 = xh * ga_ref[...] + ba_ref[...]
        o_ref[:, 0:w] = (a3 * _sig(a3)).astype(BF16)

        u, _ = _gelu(h_ref[2])
        v1, _ = _gelu(h_ref[3])
        xh2, _ = _ln_stats(v1)
        v2 = (xh2 * gb_ref[...] + bb_ref[...]).astype(BF16)
        for c in range(t // B_CHUNK):
            rs = slice(c * B_CHUNK, (c + 1) * B_CHUNK)
            for g in range(groups):
                cs = slice(g * B_CHUNK, (g + 1) * B_CHUNK)
                mixed = jnp.dot(ws_ref[g], v2[rs, cs], preferred_element_type=F32) + sb_ref[g]
                o_ref[rs, w + g * B_CHUNK:w + (g + 1) * B_CHUNK] = (u[rs, cs] * mixed).astype(BF16)

    vec = pl.BlockSpec((1, w), lambda i: (0, 0))
    grp = pl.BlockSpec((groups, B_CHUNK, B_CHUNK), lambda i: (0, 0, 0))
    in_specs = [
        pl.BlockSpec((t, d), lambda i: (i, 0)),
        pl.BlockSpec((4, d, w), lambda i: (0, 0, 0)),
        pl.BlockSpec((A_KERNEL, w), lambda i: (0, 0)),
        vec, vec, vec, vec, vec, grp, grp,
    ]
    out_specs = [pl.BlockSpec((4, t, w), lambda i: (0, i, 0)), pl.BlockSpec((t, 2 * w), lambda i: (i, 0)),
                 pl.BlockSpec((t, w), lambda i: (i, 0))]
    out_shape = [jax.ShapeDtypeStruct((4, rows, w), F32), jax.ShapeDtypeStruct((rows, 2 * w), BF16),
                 jax.ShapeDtypeStruct((rows, w), F32)]
    return pl.pallas_call(body, name=name, grid=(rows // t,), in_specs=in_specs, out_specs=out_specs, out_shape=out_shape,
                          scratch_shapes=[pltpu.VMEM((CONV_HALO, w), F32)],
                          compiler_params=_cp("arbitrary"))(x, w_in, cw, cb, ga, ba, gb, bb, ws, sbb)


def _mixer_bwd(h0, a2, dab, x, w_in, res, res_scale, cw, ga, ba, gb, bb, ws, wst, sbb, tril, *, name):
    _, rows, w = h0.shape
    d = x.shape[1]
    once = pl.Buffered(1)
    t = _pick(rows, (256,))
    hb = t // CONV_HALO
    ni = rows // t
    last_blk = rows // CONV_HALO - 1
    ext = t + CONV_HALO
    tile = slice(0, t)
    groups = w // B_CHUNK
    taps = A_KERNEL - 1

    def body(h_ref, a2_ref, a2n_ref, d_ref, dn_ref, x_ref, win_ref, res_ref, cw_ref, ga_ref, ba_ref, gb_ref, bb_ref,
             ws_ref, wst_ref, sb_ref, tril_ref, dx_ref, dwin_ref, dcw_ref, dcb_ref, dga_ref, dba_ref, dgb_ref, dbb_ref,
             dws_ref, dsb_ref, dw_ref):
        i = pl.program_id(0)
        first = i == 0
        last = i == ni - 1

        @pl.when(first)
        def _():
            for r in (dw_ref, dcw_ref, dcb_ref, dga_ref, dba_ref, dgb_ref, dbb_ref, dws_ref, dsb_ref):
                r[...] = jnp.zeros_like(r)

        xt = x_ref[...].astype(BF16).T
        dx_terms = []

        def through_w_in(slot, dh):
            dhb = dh.astype(BF16)
            dx_terms.append(lax.dot_general(dhb, win_ref[slot], (((1,), (1,)), ((), ())), preferred_element_type=F32))
            dw_ref[slot] += jnp.dot(xt, dhb, preferred_element_type=F32)

        xh, rstd = _ln_stats(jnp.concatenate([a2_ref[...], a2n_ref[...]], axis=0))
        a3 = xh * ga_ref[...] + ba_ref[...]
        s3 = _sig(a3)
        da_e = jnp.concatenate([d_ref[:, 0:w], jnp.where(last, 0.0, dn_ref[...])], axis=0)
        da3 = da_e * (s3 * (1.0 + a3 * (1.0 - s3)))
        da2 = _ln_bwd(da3 * ga_ref[...], xh, rstd)
        dga_ref[...] += _rowsum(da3[tile] * xh[tile])
        dba_ref[...] += _rowsum(da3[tile])
        dcb_ref[...] += _rowsum(da2[tile])
        sgt = _sig(h_ref[1])
        a1t = h_ref[0] * sgt
        da1t = None
        for k in range(A_KERNEL):
            sh = taps - k
            fed = (da2 if sh == 0 else pltpu.roll(da2, ext - sh, 0))[tile]
            dcw_ref[k:k + 1, :] += _rowsum(a1t * fed)
            term = cw_ref[k:k + 1, :] * fed
            da1t = term if da1t is None else da1t + term
        through_w_in(0, da1t * sgt)
        through_w_in(1, da1t * h_ref[0] * sgt * (1.0 - sgt))

        bu = h_ref[2]
        bv = h_ref[3]
        u, tu = _gelu(bu)
        v1, tv = _gelu(bv)
        xh2, rstd2 = _ln_stats(v1)
        v2 = (xh2 * gb_ref[...] + bb_ref[...]).astype(BF16)
        db = d_ref[:, w:2 * w]
        dmx_all = db * u
        du_parts, dv2_parts = [], []
        for c in range(t // B_CHUNK):
            rs = slice(c * B_CHUNK, (c + 1) * B_CHUNK)
            du_row, dv2_row = [], []
            for g in range(groups):
                cs = slice(g * B_CHUNK, (g + 1) * B_CHUNK)
                v2cg = v2[rs, cs]
                mixed = jnp.dot(ws_ref[g], v2cg, preferred_element_type=F32) + sb_ref[g]
                dmx = dmx_all[rs, cs]
                dmxb = dmx.astype(BF16)
                du_row.append(db[rs, cs] * mixed)
                dv2_row.append(jnp.dot(wst_ref[g], dmxb, preferred_element_type=F32))
                dws_ref[g] += tril_ref[...] * lax.dot_general(dmxb, v2cg, (((1,), (1,)), ((), ())),
                                                               preferred_element_type=F32)
                dsb_ref[g:g + 1, :] += _rowsum(dmx.T)
            du_parts.append(jnp.concatenate(du_row, axis=1))
            dv2_parts.append(jnp.concatenate(dv2_row, axis=1))
        du = jnp.concatenate(du_parts, axis=0)
        dv2 = jnp.concatenate(dv2_parts, axis=0)
        dgb_ref[...] += _rowsum(dv2 * xh2)
        dbb_ref[...] += _rowsum(dv2)
        dv1 = _ln_bwd(dv2 * gb_ref[...], xh2, rstd2)
        through_w_in(2, du * _gelu_grad(bu, tu))
        through_w_in(3, dv1 * _gelu_grad(bv, tv))
        dx_ref[...] = res_scale * res_ref[...] + ((dx_terms[0] + dx_terms[1]) + (dx_terms[2] + dx_terms[3]))

        @pl.when(last)
        def _():
            dwin_ref[...] = dw_ref[...].astype(BF16)

    vec = pl.BlockSpec((1, w), lambda i: (0, 0))
    grp = pl.BlockSpec((groups, B_CHUNK, B_CHUNK), lambda i: (0, 0, 0))
    halo = pl.BlockSpec((CONV_HALO, w), lambda i: (jnp.minimum((i + 1) * hb, last_blk), 0))
    wide = pl.BlockSpec((t, d), lambda i: (i, 0))
    in_specs = [
        pl.BlockSpec((4, t, w), lambda i: (0, i, 0)),
        pl.BlockSpec((t, w), lambda i: (i, 0)),
        halo,
        pl.BlockSpec((t, 2 * w), lambda i: (i, 0)),
        halo,
        wide,
        pl.BlockSpec((4, d, w), lambda i: (0, 0, 0), pipeline_mode=once),
        wide,
        pl.BlockSpec((A_KERNEL, w), lambda i: (0, 0)),
        vec, vec, vec, vec, grp, grp, grp,
        pl.BlockSpec((B_CHUNK, B_CHUNK), lambda i: (0, 0)),
    ]
    vsds = jax.ShapeDtypeStruct((1, w), F32)
    out_specs = [
        wide,
        pl.BlockSpec((4, d, w), lambda i: (0, 0, 0), pipeline_mode=once),
        pl.BlockSpec((A_KERNEL, w), lambda i: (0, 0)),
        vec, vec, vec, vec, vec, grp,
        pl.BlockSpec((groups, B_CHUNK), lambda i: (0, 0)),
    ]
    out_shape = [jax.ShapeDtypeStruct((rows, d), F32), jax.ShapeDtypeStruct((4, d, w), BF16),
                 jax.ShapeDtypeStruct((A_KERNEL, w), F32),
                 vsds, vsds, vsds, vsds, vsds, jax.ShapeDtypeStruct((groups, B_CHUNK, B_CHUNK), F32),
                 jax.ShapeDtypeStruct((groups, B_CHUNK), F32)]
    return pl.pallas_call(body, name=name, grid=(ni,), in_specs=in_specs, out_specs=out_specs, out_shape=out_shape,
                          scratch_shapes=[pltpu.VMEM((4, d, w), F32)], compiler_params=_cp("arbitrary"))(
        h0, a2, a2, dab, dab, x, w_in, res, cw, ga, ba, gb, bb, ws, wst, sbb, tril)


GROUP_ROWS = Q_PER_KV * ATT_BLOCK


def _attn_mask(n):
    qi = lax.broadcasted_iota(jnp.int32, (GROUP_ROWS, 2 * ATT_BLOCK), 0) & (ATT_BLOCK - 1)
    sj = lax.broadcasted_iota(jnp.int32, (GROUP_ROWS, 2 * ATT_BLOCK), 1)
    diff = qi + ATT_BLOCK - sj
    return (diff >= 0) & (diff < ATT_BLOCK) & ((n > 0) | (sj >= ATT_BLOCK))


def _stack_heads(ref, kvh, dtype):
    heads = [ref[:, (kvh * Q_PER_KV + g) * HEAD_DIM:(kvh * Q_PER_KV + g + 1) * HEAD_DIM] for g in range(Q_PER_KV)]
    return jnp.concatenate(heads, axis=0).astype(dtype)


def _per_row_sink(sink_ref, kvh):
    head = lax.broadcasted_iota(jnp.int32, (GROUP_ROWS, 1), 0) // ATT_BLOCK
    out = jnp.zeros((GROUP_ROWS, 1), F32)
    for g in range(Q_PER_KV):
        out = jnp.where(head == g, sink_ref[kvh * Q_PER_KV + g], out)
    return out


def _attn_specs(rows, n_q):
    dq = n_q * HEAD_DIM
    dkv = 2 * (n_q // Q_PER_KV) * HEAD_DIM
    kv_blk = dq // dkv
    assert dq % dkv == 0
    return dq, dkv, [
        pl.BlockSpec(memory_space=pltpu.SMEM),
        pl.BlockSpec((ATT_BLOCK, dq), lambda n: (n, 0)),
        pl.BlockSpec((ATT_BLOCK, dkv), lambda n: (n, kv_blk)),
        pl.BlockSpec((ATT_BLOCK, dkv), lambda n: (jnp.maximum(n - 1, 0), kv_blk)),
    ]


def _kv_pair(kvc_ref, kvp_ref, kvh, n_kv):
    ks = slice(kvh * HEAD_DIM, (kvh + 1) * HEAD_DIM)
    vs = slice((n_kv + kvh) * HEAD_DIM, (n_kv + kvh + 1) * HEAD_DIM)
    kk = jnp.concatenate([kvp_ref[:, ks], kvc_ref[:, ks]], axis=0).astype(BF16)
    vv = jnp.concatenate([kvp_ref[:, vs], kvc_ref[:, vs]], axis=0).astype(BF16)
    return kk, vv


def _attn_fwd(qkv, sinks, *, name):
    rows = qkv.shape[0]
    n_q = sinks.shape[0]
    n_kv = n_q // Q_PER_KV
    scale = 1.0 / math.sqrt(HEAD_DIM)
    dq, _, in_specs = _attn_specs(rows, n_q)

    def body(sink_ref, q_ref, kvc_ref, kvp_ref, o_ref, lse_ref):
        valid = _attn_mask(pl.program_id(0))
        for kvh in range(n_kv):
            kk, vv = _kv_pair(kvc_ref, kvp_ref, kvh, n_kv)
            qs = _stack_heads(q_ref, kvh, BF16)
            s = lax.dot_general(qs, kk, (((1,), (1,)), ((), ())), preferred_element_type=F32)
            s = jnp.where(valid, s * scale, -jnp.inf)
            sk = _per_row_sink(sink_ref, kvh)
            m = jnp.maximum(jnp.max(s, axis=1, keepdims=True), sk)
            p = jnp.exp(s - m)
            l = jnp.sum(p, axis=1, keepdims=True) + jnp.exp(sk - m)
            o = jnp.dot((p / l).astype(BF16), vv, preferred_element_type=F32)
            lse = m + jnp.log(l)
            for g in range(Q_PER_KV):
                h = kvh * Q_PER_KV + g
                rs = slice(g * ATT_BLOCK, (g + 1) * ATT_BLOCK)
                o_ref[:, h * HEAD_DIM:(h + 1) * HEAD_DIM] = o[rs]
                lse_ref[:, h:h + 1] = lse[rs]

    out_specs = [pl.BlockSpec((ATT_BLOCK, dq), lambda n: (n, 0)), pl.BlockSpec((ATT_BLOCK, n_q), lambda n: (n, 0))]
    out_shape = [jax.ShapeDtypeStruct((rows, dq), F32), jax.ShapeDtypeStruct((rows, n_q), F32)]
    return pl.pallas_call(body, name=name, grid=(rows // ATT_BLOCK,), in_specs=in_specs, out_specs=out_specs,
                          out_shape=out_shape, compiler_params=_cp("parallel"))(sinks, qkv, qkv, qkv)


def _attn_bwd(qkv, dout, lse, sinks, *, name):
    rows = qkv.shape[0]
    n_q = sinks.shape[0]
    n_kv = n_q // Q_PER_KV
    scale = 1.0 / math.sqrt(HEAD_DIM)
    dq_w, dkv_w, in_specs = _attn_specs(rows, n_q)
    blk_q = pl.BlockSpec((ATT_BLOCK, dq_w), lambda n: (n, 0))
    blk_kv = pl.BlockSpec((ATT_BLOCK, dkv_w), lambda n: (n, 0))
    in_specs = in_specs + [blk_q, pl.BlockSpec((ATT_BLOCK, n_q), lambda n: (n, 0))]

    def body(sink_ref, q_ref, kvc_ref, kvp_ref, do_ref, lse_ref, dq_ref, dkc_ref, dkp_ref, dsink_ref):
        n = pl.program_id(0)

        @pl.when(n == 0)
        def _():
            dsink_ref[...] = jnp.zeros_like(dsink_ref)

        valid = _attn_mask(n)
        head_ids = lax.broadcasted_iota(jnp.int32, (1, n_q), 1)
        dsink = jnp.zeros((1, n_q), F32)
        for kvh in range(n_kv):
            kk, vv = _kv_pair(kvc_ref, kvp_ref, kvh, n_kv)
            qs = _stack_heads(q_ref, kvh, BF16)
            dos = _stack_heads(do_ref, kvh, BF16)
            lse = jnp.concatenate([lse_ref[:, kvh * Q_PER_KV + g:kvh * Q_PER_KV + g + 1] for g in range(Q_PER_KV)], axis=0)
            s = lax.dot_general(qs, kk, (((1,), (1,)), ((), ())), preferred_element_type=F32)
            s = jnp.where(valid, s * scale, -jnp.inf)
            p = jnp.exp(s - lse)
            dp = lax.dot_general(dos, vv, (((1,), (1,)), ((), ())), preferred_element_type=F32)
            delta = jnp.sum(p * dp, axis=1, keepdims=True)
            ds = (p * (dp - delta) * scale).astype(BF16)
            sink_term = jnp.exp(_per_row_sink(sink_ref, kvh) - lse) * delta
            dqs = jnp.dot(ds, kk, preferred_element_type=F32)
            for g in range(Q_PER_KV):
                h = kvh * Q_PER_KV + g
                rs = slice(g * ATT_BLOCK, (g + 1) * ATT_BLOCK)
                dsink = dsink + jnp.where(head_ids == h, -jnp.sum(sink_term[rs]), 0.0)
                dq_ref[:, h * HEAD_DIM:(h + 1) * HEAD_DIM] = dqs[rs]
            dk = lax.dot_general(ds, qs, (((0,), (0,)), ((), ())), preferred_element_type=F32)
            dv = lax.dot_general(p.astype(BF16), dos, (((0,), (0,)), ((), ())), preferred_element_type=F32)
            ks = slice(kvh * HEAD_DIM, (kvh + 1) * HEAD_DIM)
            vs = slice((n_kv + kvh) * HEAD_DIM, (n_kv + kvh + 1) * HEAD_DIM)
            dkp_ref[:, ks] = dk[0:ATT_BLOCK]
            dkc_ref[:, ks] = dk[ATT_BLOCK:]
            dkp_ref[:, vs] = dv[0:ATT_BLOCK]
            dkc_ref[:, vs] = dv[ATT_BLOCK:]
        dsink_ref[...] += dsink

    out_specs = [blk_q, blk_kv, blk_kv, pl.BlockSpec((1, n_q), lambda n: (0, 0))]
    out_shape = [jax.ShapeDtypeStruct((rows, dq_w), F32), jax.ShapeDtypeStruct((rows, dkv_w), F32),
                 jax.ShapeDtypeStruct((rows, dkv_w), F32), jax.ShapeDtypeStruct((1, n_q), F32)]
    return pl.pallas_call(body, name=name, grid=(rows // ATT_BLOCK,), in_specs=in_specs, out_specs=out_specs,
                          out_shape=out_shape, compiler_params=_cp("arbitrary"))(sinks, qkv, qkv, qkv, dout, lse)


def _dqkv_assemble(dq, dkc, dkp, *, name):
    rows, dq_w = dq.shape
    dkv_w = dkc.shape[1]
    nb = rows // ATT_BLOCK

    def body(dq_ref, dkc_ref, dkp_ref, o_ref, db_ref):
        n = pl.program_id(0)

        @pl.when(n == 0)
        def _():
            db_ref[...] = jnp.zeros_like(db_ref)

        dqv = dq_ref[...]
        dkv = dkc_ref[...] + jnp.where(n == nb - 1, 0.0, dkp_ref[...])
        o_ref[:, 0:dq_w] = dqv.astype(BF16)
        o_ref[:, dq_w:dq_w + dkv_w] = dkv.astype(BF16)
        db_ref[:, 0:dq_w] += _rowsum(dqv)
        db_ref[:, dq_w:dq_w + dkv_w] += _rowsum(dkv)

    width = dq_w + dkv_w
    in_specs = [pl.BlockSpec((ATT_BLOCK, dq_w), lambda n: (n, 0)), pl.BlockSpec((ATT_BLOCK, dkv_w), lambda n: (n, 0)),
                pl.BlockSpec((ATT_BLOCK, dkv_w), lambda n: (jnp.minimum(n + 1, nb - 1), 0))]
    out_specs = [pl.BlockSpec((ATT_BLOCK, width), lambda n: (n, 0)), pl.BlockSpec((1, width), lambda n: (0, 0))]
    out_shape = [jax.ShapeDtypeStruct((rows, width), BF16), jax.ShapeDtypeStruct((1, width), F32)]
    return pl.pallas_call(body, name=name, grid=(nb,), in_specs=in_specs, out_specs=out_specs, out_shape=out_shape,
                          compiler_params=_cp("arbitrary"))(dq, dkc, dkp)


def _row_tile(r, c):
    budget = 2 * 1024 * 1024 // (4 * c)
    for cand in (1024, 512, 256, 128, 64, 32, 16):
        if cand <= budget and r % cand == 0:
            return cand
    return r


def _octo_sum(own, recv, place, dest, lead, *, name):
    _, _, r, c = own.shape
    t = _row_tile(r, c)
    lead_idx, buf_shape = lead

    def body(place_ref, own_ref, *rest):
        o_ref = rest[7] if dest is None else rest[8]
        acc = own_ref[...].astype(F32)
        for k in range(7):
            acc = acc + rest[k][...].astype(F32)
        o_ref[...] = acc

    def peer(mask):
        return pl.BlockSpec((None, t, c), lambda i, pr: (pr[2] ^ mask, i, 0))

    if lead_idx is None:
        o_spec = pl.BlockSpec((None, t, c), lambda i, pr: (pr[1], i, 0))
    else:
        o_spec = pl.BlockSpec((None, None, t, c), lambda i, pr: (lead_idx, pr[1], i, 0))
    in_specs = [pl.BlockSpec((None, None, t, c), lambda i, pr: (pr[0], pr[1], i, 0))] + [peer(m) for m in range(1, 8)]
    args = [place, own] + [recv] * 7
    aliases = {}
    if dest is not None:
        in_specs.append(HBM)
        args.append(dest)
        aliases = {9: 0}
    grid_spec = pltpu.PrefetchScalarGridSpec(num_scalar_prefetch=1, grid=(r // t,), in_specs=in_specs, out_specs=o_spec)
    return pl.pallas_call(body, name=name, grid_spec=grid_spec, out_shape=jax.ShapeDtypeStruct(buf_shape, F32),
                          input_output_aliases=aliases, compiler_params=_cp("parallel"))(*args)


def _adamw_math(w, g, m, v):
    nm = ADAM_B1 * m + (1.0 - ADAM_B1) * g
    nv = ADAM_B2 * v + (1.0 - ADAM_B2) * (g * g)
    m_hat = nm / (1.0 - ADAM_B1 ** ADAM_STEP)
    v_hat = nv / (1.0 - ADAM_B2 ** ADAM_STEP)
    return -ADAM_LR * (m_hat / (jnp.sqrt(v_hat) + ADAM_EPS) + ADAM_WD * w), nm, nv


def _adamw(w, g, m, v, *, name):
    r, c = w.shape
    t = _row_tile(r, c)

    def body(w_ref, g_ref, m_ref, v_ref, d_ref, nm_ref, nv_ref, go_ref):
        gv = g_ref[...]
        d_ref[...], nm_ref[...], nv_ref[...] = _adamw_math(w_ref[...], gv, m_ref[...], v_ref[...])
        go_ref[...] = gv

    blk = pl.BlockSpec((t, c), lambda i: (i, 0))
    sds = jax.ShapeDtypeStruct((r, c), F32)
    return pl.pallas_call(body, name=name, grid=(r // t,), in_specs=[blk] * 4, out_specs=[blk] * 4,
                          out_shape=[sds] * 4, compiler_params=_cp("parallel"))(w, g, m, v)


HBM = pl.BlockSpec(memory_space=pl.ANY)


def _place():
    x, y, c = lax.axis_index("x"), lax.axis_index("y"), lax.axis_index("c")
    chips = [(1 - x, y), (x, 1 - y), (1 - x, 1 - y)]
    return x, y, c, 2 * x + y, (x, y, 1 - c), chips


def _rcopy(src, dst, ssem, rsem, dev):
    return pltpu.make_async_remote_copy(src_ref=src, dst_ref=dst, send_sem=ssem, recv_sem=rsem, device_id=dev,
                                        device_id_type=MESH)


HBM_ONLY = pl.BlockSpec(memory_space=pltpu.HBM)
SEM = pl.BlockSpec(memory_space=pltpu.SEMAPHORE)


def _peers():
    x, y, c = lax.axis_index("x"), lax.axis_index("y"), lax.axis_index("c")
    out = []
    for mask in range(1, 8):
        px = 1 - x if mask & 4 else x
        py = 1 - y if mask & 2 else y
        pc = 1 - c if mask & 1 else c
        out.append(((px, py, pc), 2 * px + py, pc, 4 * px + 2 * py + pc))
    return 4 * x + 2 * y + c, out


def _reduce_start(grads, lands, after, *, name, whole=False):
    nt = len(grads)

    def body(*refs):
        ssems, rsems = refs[2 * nt + 1:3 * nt + 1], refs[3 * nt + 1:4 * nt + 1]
        g_out, l_out, token = refs[4 * nt + 1:5 * nt + 1], refs[5 * nt + 1:6 * nt + 1], refs[6 * nt + 1]
        me, peers = _peers()
        for t in range(nt):
            for k, (dev, chip, core, _) in enumerate(peers):
                src = g_out[t] if whole else g_out[t].at[chip, core]
                _rcopy(src, l_out[t].at[me], ssems[t].at[k], rsems[t].at[k], dev).start()
        token[...] = jnp.zeros_like(token)

    sems = [pltpu.SemaphoreType.DMA((7,))] * (2 * nt)
    out_shape = (sems + [pltpu.HBM(g.shape, g.dtype) for g in grads] + [pltpu.HBM(l.shape, l.dtype) for l in lands]
                 + [jax.ShapeDtypeStruct((8, LANES), F32)])
    res = pl.pallas_call(
        body, name=name, in_specs=[HBM_ONLY] * (2 * nt + 1),
        out_specs=[SEM] * (2 * nt) + [HBM_ONLY] * (2 * nt) + [pl.BlockSpec(memory_space=pltpu.VMEM)], out_shape=out_shape,
        input_output_aliases={t: 2 * nt + t for t in range(2 * nt)},
        compiler_params=pltpu.CompilerParams(has_side_effects=DATAFLOW),
    )(*[pltpu.with_memory_space_constraint(a, pltpu.HBM) for a in list(grads) + list(lands) + [after]])
    return res[:nt], res[nt:2 * nt], res[2 * nt:3 * nt], res[3 * nt:4 * nt], res[4 * nt]


def _reduce_wait(grads, lands, ssems, rsems, after, *, name, whole=False):
    nt = len(grads)

    def body(*refs):
        ssem_refs, rsem_refs = refs[2 * nt:3 * nt], refs[3 * nt:4 * nt]
        g_out, l_out = refs[4 * nt + 1:5 * nt + 1], refs[5 * nt + 1:6 * nt + 1]
        me, peers = _peers()
        for t in range(nt):
            for k, (dev, chip, core, _) in enumerate(peers):
                src = g_out[t] if whole else g_out[t].at[chip, core]
                _rcopy(src, l_out[t].at[me], ssem_refs[t].at[k], rsem_refs[t].at[k], dev).wait_send()
        for t in range(nt):
            for k, (dev, _, _, idx) in enumerate(peers):
                slot = l_out[t].at[idx]
                _rcopy(slot, slot, ssem_refs[t].at[k], rsem_refs[t].at[k], dev).wait_recv()

    res = pl.pallas_call(
        body, name=name, in_specs=[HBM_ONLY] * (2 * nt) + [SEM] * (2 * nt) + [HBM_ONLY], out_specs=[HBM_ONLY] * (2 * nt),
        out_shape=[pltpu.HBM(a.shape, a.dtype) for a in list(grads) + list(lands)],
        input_output_aliases={t: t for t in range(2 * nt)},
        compiler_params=pltpu.CompilerParams(has_side_effects=DATAFLOW),
    )(*grads, *lands, *ssems, *rsems, pltpu.with_memory_space_constraint(after, pltpu.HBM))
    return list(res[:nt]), list(res[nt:])
DATAFLOW = pltpu.SideEffectType.DATAFLOW_SIDE_EFFECTING


def _gather_now(bufs, *, name):
    nt = len(bufs)

    def body(*refs):
        outs = refs[nt:2 * nt]
        ssem, rsem = refs[2 * nt:]
        x, y, c, q, sib, chips = _place()
        sends = []
        for t in range(nt):
            for j, (px, py) in enumerate(chips):
                mine = outs[t].at[q, c]
                cp = _rcopy(mine, mine, ssem.at[t, j], rsem.at[t, j], (px, py, c))
                cp.start()
                sends.append(cp)
        for t in range(nt):
            for j, (px, py) in enumerate(chips):
                landed = outs[t].at[2 * px + py, c]
                _rcopy(landed, landed, ssem.at[t, j], rsem.at[t, j], (px, py, c)).wait_recv()
                cp = _rcopy(landed, landed, ssem.at[t, 3 + j], rsem.at[t, 3 + j], sib)
                cp.start()
                sends.append(cp)
        for t in range(nt):
            for j, (px, py) in enumerate(chips):
                passed = outs[t].at[2 * px + py, 1 - c]
                _rcopy(passed, passed, ssem.at[t, 3 + j], rsem.at[t, 3 + j], sib).wait_recv()
        for cp in sends:
            cp.wait_send()

    out_shape = [jax.ShapeDtypeStruct(b.shape, b.dtype) for b in bufs]
    return pl.pallas_call(
        body, name=name, in_specs=[HBM] * nt, out_specs=[HBM] * nt, out_shape=out_shape,
        input_output_aliases={t: t for t in range(nt)},
        scratch_shapes=[pltpu.SemaphoreType.DMA((nt, 6)), pltpu.SemaphoreType.DMA((nt, 6))],
    )(*bufs)


def _gather_start(bufs, half, after, *, name):
    nt = len(bufs)

    def body(*refs):
        ssems, rsems, outs = refs[nt + 1:2 * nt + 1], refs[2 * nt + 1:3 * nt + 1], refs[3 * nt + 1:4 * nt + 1]
        x, y, c, q, sib, chips = _place()
        for t in range(nt):
            for j, (px, py) in enumerate(chips):
                mine = outs[t].at[q, c] if half[t] else outs[t].at[q]
                _rcopy(mine, mine, ssems[t].at[j], rsems[t].at[j], (px, py, c)).start()

    sems = [pltpu.SemaphoreType.DMA((3,))] * (2 * nt)
    out_shape = sems + [pltpu.HBM(b.shape, b.dtype) for b in bufs]
    res = pl.pallas_call(
        body, name=name, in_specs=[HBM_ONLY] * (nt + 1), out_specs=[SEM] * (2 * nt) + [HBM_ONLY] * nt, out_shape=out_shape,
        input_output_aliases={t: 2 * nt + t for t in range(nt)},
        compiler_params=pltpu.CompilerParams(has_side_effects=DATAFLOW),
    )(*[pltpu.with_memory_space_constraint(b, pltpu.HBM) for b in list(bufs) + [after]])
    return res[:nt], res[nt:2 * nt], res[2 * nt:]


def _gather_wait(bufs, half, ssems, rsems, after, *, name):
    nt = len(bufs)

    def body(*refs):
        ssem_refs, rsem_refs = refs[nt:2 * nt], refs[2 * nt:3 * nt]
        outs = refs[3 * nt + 1:]
        x, y, c, q, sib, chips = _place()
        for t in range(nt):
            for j, (px, py) in enumerate(chips):
                mine = outs[t].at[q, c] if half[t] else outs[t].at[q]
                _rcopy(mine, mine, ssem_refs[t].at[j], rsem_refs[t].at[j], (px, py, c)).wait_send()
        for t in range(nt):
            for j, (px, py) in enumerate(chips):
                theirs = outs[t].at[2 * px + py, c] if half[t] else outs[t].at[2 * px + py]
                _rcopy(theirs, theirs, ssem_refs[t].at[j], rsem_refs[t].at[j], (px, py, c)).wait_recv()

    res = pl.pallas_call(
        body, name=name, in_specs=[HBM_ONLY] * nt + [SEM] * (2 * nt) + [HBM], out_specs=[HBM_ONLY] * nt,
        out_shape=[pltpu.HBM(b.shape, b.dtype) for b in bufs], input_output_aliases={t: t for t in range(nt)},
        compiler_params=pltpu.CompilerParams(has_side_effects=DATAFLOW),
    )(*bufs, *ssems, *rsems, after)
    return list(res)


def _sibling_swap(bufs, *, name):
    nt = len(bufs)

    def body(*refs):
        outs = refs[nt:2 * nt]
        ssem, rsem = refs[2 * nt:]
        x, y, c, q, sib, chips = _place()
        sends = []
        for t in range(nt):
            for j, (px, py) in enumerate(chips):
                held = outs[t].at[2 * px + py, c]
                cp = _rcopy(held, held, ssem.at[t, j], rsem.at[t, j], sib)
                cp.start()
                sends.append(cp)
        for t in range(nt):
            for j, (px, py) in enumerate(chips):
                other = outs[t].at[2 * px + py, 1 - c]
                _rcopy(other, other, ssem.at[t, j], rsem.at[t, j], sib).wait_recv()
        for cp in sends:
            cp.wait_send()

    return pl.pallas_call(
        body, name=name, in_specs=[HBM] * nt, out_specs=[HBM] * nt,
        out_shape=[jax.ShapeDtypeStruct(b.shape, b.dtype) for b in bufs], input_output_aliases={t: t for t in range(nt)},
        scratch_shapes=[pltpu.SemaphoreType.DMA((nt, 3)), pltpu.SemaphoreType.DMA((nt, 3))],
    )(*bufs)


def _sibling_share(bufs, layout, *, name):
    no = len(bufs)
    nt = len(layout)

    def body(*refs):
        outs = refs[no:2 * no]
        ssem, rsem = refs[2 * no:]
        x, y, c, q, sib, chips = _place()

        def slot(t, half):
            o, lead = layout[t]
            return outs[o].at[half] if lead is None else outs[o].at[lead, half]

        sends = []
        for t in range(nt):
            cp = _rcopy(slot(t, c), slot(t, c), ssem.at[t], rsem.at[t], sib)
            cp.start()
            sends.append(cp)
        for t in range(nt):
            _rcopy(slot(t, 1 - c), slot(t, 1 - c), ssem.at[t], rsem.at[t], sib).wait_recv()
        for cp in sends:
            cp.wait_send()

    out_shape = [jax.ShapeDtypeStruct(b.shape, b.dtype) for b in bufs]
    return pl.pallas_call(
        body, name=name, in_specs=[HBM] * no, out_specs=[HBM] * no, out_shape=out_shape,
        input_output_aliases={o: o for o in range(no)},
        scratch_shapes=[pltpu.SemaphoreType.DMA((nt,)), pltpu.SemaphoreType.DMA((nt,))],
    )(*bufs)


SMALL_GROUP_OF = (0,) * 6 + (1,) * 2 + (2,) * 2 + (3,) * 4 + (4,) * 9


def _small_pack(local, dims, which, *, name):
    kw, wa, ng, nqkv, nsk, f, dm = dims
    shapes = _small_shapes(dims)
    row_vec = 8 * (-(-kw // 8))
    pos = sorted(local)
    assert all(SMALL_GROUP_OF[p] in which for p in pos)

    def pack_body(*refs):
        loc = dict(zip(pos, refs[:len(pos)]))
        grp = dict(zip(which, refs[len(pos):]))
        for gr in grp.values():
            gr[...] = jnp.zeros_like(gr)
        if 0 in which:
            grp[0][0:kw, :] = loc[0][...]
            for k in range(5):
                grp[0][row_vec + k:row_vec + k + 1, :] = loc[1 + k][...]
        if 1 in which:
            for g in range(ng):
                grp[1][g * B_CHUNK:(g + 1) * B_CHUNK, :] = loc[6][g]
            grp[1][ng * B_CHUNK:ng * B_CHUNK + ng, :] = loc[7][...]
        if 2 in which:
            grp[2][0:1, :] = loc[8][...]
            grp[2][1:2, 0:nsk] = loc[9][...]
        if 3 in which:
            for l in range(2):
                for s in range(2):
                    grp[3][l, s, 0:3, :] = loc[10 + 2 * l][s]
                    grp[3][l, s, 3:4, :] = loc[11 + 2 * l][s]
        if 4 in which:
            for k in range(9):
                grp[4][k:k + 1, :] = loc[14 + k][...]

    vm = pl.BlockSpec(memory_space=pltpu.VMEM)
    return pl.pallas_call(
        pack_body, name=name, in_specs=[vm] * len(pos), out_specs=[vm] * len(which),
        out_shape=[jax.ShapeDtypeStruct(shapes[g], F32) for g in which],
        compiler_params=pltpu.CompilerParams(vmem_limit_bytes=VMEM_LIMIT),
    )(*[local[p] for p in pos])


def _small_shapes(dims):
    kw, wa, ng, nqkv, nsk, f, dm = dims
    return [(8 * (-(-kw // 8)) + 8, wa), (ng * B_CHUNK + 8, B_CHUNK), (8, nqkv), (2, 2, 8, f), (16, dm)]


def _small_update(groups, landed, params, dims, *, name):
    kw, wa, ng, nqkv, nsk, f, dm = dims
    shapes = _small_shapes(dims)
    row_vec = 8 * (-(-kw // 8))
    n_grp = len(shapes)
    flat_params = [a for triple in params for a in triple]
    n_par = len(params)
    vm = pl.BlockSpec(memory_space=pltpu.VMEM)

    def adamw_body(*refs):
        own = refs[:n_grp]
        land = refs[n_grp:2 * n_grp]
        par = refs[2 * n_grp:2 * n_grp + 3 * n_par]
        outs = refs[2 * n_grp + 3 * n_par:2 * n_grp + 7 * n_par]
        loss_ref = refs[2 * n_grp + 7 * n_par]
        tot = refs[2 * n_grp + 7 * n_par + 1:]
        x, y = lax.axis_index("x"), lax.axis_index("y")
        q = 2 * x + y
        me = 4 * x + 2 * y + lax.axis_index("c")
        for gi in range(n_grp):
            acc = None
            for dv in range(8):
                term = jnp.where(me == dv, own[gi][...], land[gi][dv])
                acc = term if acc is None else acc + term
            tot[gi][...] = acc
        ta, tb, tc, td, te = tot

        def mine(piece):
            out = piece(0)
            for k in range(1, 4):
                out = jnp.where(q == k, piece(k), out)
            return out

        def update(p, grad, index=None):
            at = (lambda r: r[...]) if index is None else (lambda r: r[index])
            w_ref, m_ref, v_ref = par[3 * p:3 * p + 3]
            g_ref, d_ref, nm_ref, nv_ref = outs[4 * p:4 * p + 4]
            delta, nm, nv = _adamw_math(at(w_ref), grad, at(m_ref), at(v_ref))
            for r, val in ((g_ref, grad), (d_ref, delta), (nm_ref, nm), (nv_ref, nv)):
                if index is None:
                    r[...] = val
                else:
                    r[index] = val

        wq = wa // 4
        update(0, mine(lambda k: ta[0:kw, k * wq:(k + 1) * wq]), (0,))
        for k in range(5):
            update(1 + k, ta[row_vec + k:row_vec + k + 1, :])
        for g in range(ng):
            update(6, tb[g * B_CHUNK:(g + 1) * B_CHUNK, :], (0, g))
        update(7, tb[ng * B_CHUNK:ng * B_CHUNK + ng, :], (0,))
        nq4 = nqkv // 4
        update(8, mine(lambda k: tc[0:1, k * nq4:(k + 1) * nq4]))
        update(9, tc[1:2, 0:nsk])
        fh = f // 2
        for l in range(2):
            update(10, mine(lambda k: td[l, k // 2, 0:3, (k % 2) * fh:(k % 2 + 1) * fh]), (l,))
            update(11, jnp.concatenate([td[l, 0, 3:4, :], td[l, 1, 3:4, :]], axis=1), (slice(l, l + 1),))
        dq4 = dm // 4
        for i in range(2):
            for j in range(2):
                for p, base in ((12, 0), (13, 4)):
                    row = base + 2 * i + j
                    update(p, mine(lambda k: te[row:row + 1, k * dq4:(k + 1) * dq4]), (i, slice(j, j + 1)))
        loss_ref[...] = (0.5 / dm) * jnp.sum(te[8:9, :], axis=1, keepdims=True)

    out_shape = []
    for w, _, _ in params:
        out_shape += [jax.ShapeDtypeStruct(w.shape, F32)] * 4
    out_shape.append(jax.ShapeDtypeStruct((1, 1), F32))
    res = pl.pallas_call(
        adamw_body, name=name + "_adamw", in_specs=[vm] * (2 * n_grp + 3 * n_par), out_specs=[vm] * len(out_shape),
        out_shape=out_shape, scratch_shapes=[pltpu.VMEM(s, F32) for s in shapes],
        compiler_params=pltpu.CompilerParams(vmem_limit_bytes=VMEM_LIMIT),
    )(*groups, *landed, *flat_params)
    return [res[4 * p:4 * p + 4] for p in range(n_par)], res[-1]


def _pack(arrays, rows_multiple):
    flat = jnp.concatenate([a.reshape(-1) for a in arrays])
    rows = -(-flat.shape[0] // LANES)
    rows = -(-rows // rows_multiple) * rows_multiple
    flat = jnp.pad(flat, (0, rows * LANES - flat.shape[0]))
    return flat.reshape(rows, LANES)


def _unshard_cols(stacked):
    moved = jnp.moveaxis(stacked, 0, -2)
    return moved.reshape(moved.shape[:-2] + (4 * stacked.shape[-1],))


def kernel(x, ab_w_in, a_conv_w, a_conv_b, a_norm_g, a_norm_b, b_norm_g, b_norm_b, b_spatial_w, b_spatial_b, ab_w_out, c_w_qkv, c_b_qkv, c_sinks, c_w_o, ffn_w_up, ffn_conv_w, ffn_conv_b, ffn_w_down, ln_g, ln_b, loss_target, m_ab_w_in, m_a_conv_w, m_a_conv_b, m_a_norm_g, m_a_norm_b, m_b_norm_g, m_b_norm_b, m_b_spatial_w, m_b_spatial_b, m_ab_w_out, m_c_w_qkv, m_c_b_qkv, m_c_sinks, m_c_w_o, m_ffn_w_up, m_ffn_conv_w, m_ffn_conv_b, m_ffn_w_down, m_ln_g, m_ln_b, v_ab_w_in, v_a_conv_w, v_a_conv_b, v_a_norm_g, v_a_norm_b, v_b_norm_g, v_b_norm_b, v_b_spatial_w, v_b_spatial_b, v_ab_w_out, v_c_w_qkv, v_c_b_qkv, v_c_sinks, v_c_w_o, v_ffn_w_up, v_ffn_conv_w, v_ffn_conv_b, v_ffn_w_down, v_ln_g, v_ln_b):
    rows, d = x.shape[1], x.shape[2]
    depth = ln_g.shape[0]
    assert depth == 2 and x.shape[0] == 1
    alpha = (2.0 * depth) ** 0.25
    f = ffn_w_down.shape[1] * 4
    n_q = c_sinks.shape[1]
    q_idx = 2 * lax.axis_index("x") + lax.axis_index("y")
    c_idx = lax.axis_index("c")
    xs, tgt = x[0], loss_target[0]

    def own_slot(part):
        buf = lax.empty((4,) + part.shape, part.dtype)
        return lax.dynamic_update_slice(buf, part[None], (q_idx, 0, 0, 0))

    def halves(wm):
        return own_slot(wm.astype(BF16).reshape((2, wm.shape[0] // 2) + wm.shape[1:]))

    small_sharded = [a_conv_w[0], c_b_qkv[0], ffn_conv_w, ln_g, ln_b]
    small_pack = _pack(small_sharded, 16)
    bufs = [halves(ab_w_in[0]), own_slot(small_pack.reshape(2, small_pack.shape[0] // 2, LANES)), halves(ab_w_out[0]),
            halves(ffn_w_up[0]), halves(ffn_w_down[0]), halves(c_w_qkv[0]), halves(c_w_o[0]),
            halves(ffn_w_up[1]), halves(ffn_w_down[1])]
    whole = lambda g: g.reshape(4, 2 * g.shape[2], g.shape[3])
    n_now = 2
    first_two = _gather_now(bufs[:n_now], name="gather_now")
    w_in, small_all = [whole(g) for g in first_two]
    later = bufs[n_now:]
    half = [True, True] + [False] * (len(later) - 2)
    ssems, rsems, started = _gather_start(later, half, first_two[1], name="gather_start")

    def arrive(idx, after, tag):
        idx = [i - n_now for i in idx]
        halved = [half[i] for i in idx]
        got = _gather_wait([started[i] for i in idx], halved, [ssems[i] for i in idx], [rsems[i] for i in idx], after,
                           name=f"gather_wait_{tag}")
        if all(halved):
            got = _sibling_swap(got, name=f"gather_swap_{tag}")
        return [whole(g) for g in got]

    small_all = small_all.reshape(4, -1)
    sh_shapes = [s.shape for s in small_sharded]
    pieces, pos = [], 0
    for s in sh_shapes:
        n = math.prod(s)
        pieces.append(_unshard_cols(small_all[:, pos:pos + n].reshape((4,) + s)))
        pos += n
    conv_w_a, b_qkv, conv_w_f, ln_gf, ln_bf = pieces

    tril = jnp.tril(jnp.ones((B_CHUNK, B_CHUNK), F32))
    ws = (b_spatial_w[0] * tril).astype(BF16)
    wst = jnp.swapaxes(ws, 1, 2)
    sbb = jnp.broadcast_to(b_spatial_b[0][:, :, None], b_spatial_w[0].shape)
    mix_vecs = [a_conv_b, a_norm_g, a_norm_b, b_norm_g, b_norm_b]
    cw_f = [jnp.swapaxes(conv_w_f[l].reshape(3, 2, f), 0, 1) for l in range(depth)]
    cb_f = [ffn_conv_b[l].reshape(2, 1, f) for l in range(depth)]
    lng = lambda i, j: ln_gf[i, j].reshape(1, d)
    lnb = lambda i, j: ln_bf[i, j].reshape(1, d)
    sinks = c_sinks[0]

    w_up, w_down = [None, None], [None, None]

    def ffn_fwd(xin, l):
        w_up[l], = arrive([3 + 4 * l], xin, f"up{l}")
        hf, fact = _ffn_up_fwd(xin, w_up[l], cw_f[l], cb_f[l], name=f"ffn{l}_up")
        w_down[l] = arrive([4 + 4 * l], fact, f"down{l}")[0].reshape(-1, d)
        out = _matmul(fact, w_down[l], name=f"ffn{l}_down", tm=512, tn=1024, tk=2816)
        return hf, fact, out

    h0, ab, a2 = _mixer_fwd(xs, w_in, conv_w_a, *mix_vecs, ws, sbb, name="mix_fwd")
    w_out = arrive([2], ab, "out")[0].reshape(-1, d)
    mix = _matmul(ab, w_out, name="mix_out", tm=1024, tn=1024, tk=1024)
    x1 = _add_ln_fwd(xs, mix, lng(0, 0), lnb(0, 0), alpha, name="ln00")
    hf0, f0, ffn0 = ffn_fwd(x1, 0)
    x2 = _add_ln_fwd(x1, ffn0, lng(0, 1), lnb(0, 1), alpha, name="ln01")
    w_qkv = _unshard_cols(arrive([5], x2, "qkv")[0])
    qkv = _matmul(x2, w_qkv, name="att_qkv", tm=1024, tn=w_qkv.shape[1], tk=1024, bias=b_qkv.reshape(1, -1))
    ao, lse = _attn_fwd(qkv, sinks, name="att_core")
    w_o = arrive([6], ao, "o")[0].reshape(-1, d)
    att = _matmul(ao, w_o, name="att_out", tm=1024, tn=1024, tk=1024)
    x3 = _add_ln_fwd(x2, att, lng(1, 0), lnb(1, 0), alpha, name="ln10")
    hf1, f1, ffn1 = ffn_fwd(x3, 1)
    sq_err, dy = _add_ln_loss(x3, ffn1, lng(1, 1), lnb(1, 1), tgt, alpha, name="ln11_loss")

    def owner_view(g):
        if g.ndim == 3:
            return g.reshape(4, 2, g.shape[1] // 2, g.shape[2])
        return g.reshape(4, 2, g.shape[0] // 8, g.shape[1])

    in_flight = []

    def send_grads(tag, grads, after):
        lands = [lax.empty((8,) + g.shape[2:], BF16) for g in grads]
        ss, rs, g_thru, l_thru, token = _reduce_start(grads, lands, after, name=f"reduce_start_{tag}")
        in_flight.append((tag, g_thru, l_thru, ss, rs))
        return token[0:1, 0:1]

    def ffn_bwd(dz, xin, hf, fact, l):
        d_wdown = _matmul(fact, dz, name=f"ffn{l}_down_dw", ta=True, tm=1408, tn=1024, tk=2048, out_dtype=BF16)
        dfa = _matmul(dz, w_down[l], name=f"ffn{l}_down_dx", tb=True, tm=1024, tn=1408, tk=1024, out_dtype=BF16)
        dx_parts, d_wup, dcw, dcb = _ffn_up_bwd(hf, dfa, xin, w_up[l], cw_f[l], cb_f[l], name=f"ffn{l}_up_bwd")
        tok = send_grads(f"ffn{l}", [owner_view(d_wup), owner_view(d_wdown)], dcb)
        return [(dx_parts, 1.0), (dz, alpha)], dcw, dcb, tok

    dz, dg11, db11 = _add_ln_bwd([(dy, 1.0)], x3, ffn1, lng(1, 1), alpha, name="ln11_bwd")
    dx3, dcw1, dcb1, tok = ffn_bwd(dz, x3, hf1, f1, 1)
    dz, dg10, db10 = _add_ln_bwd(dx3, x2, att, lng(1, 0) + tok, alpha, name="ln10_bwd")
    d_wo = _matmul(ao, dz, name="att_out_dw", ta=True, tm=1024, tn=1024, tk=1024, out_dtype=BF16)
    dao = _matmul(dz, w_o, name="att_out_dx", tb=True, tm=1024, tn=1024, tk=1024)
    dq, dkc, dkp, d_sinks = _attn_bwd(qkv, dao, lse, sinks, name="att_core_bwd")
    dqkv, d_bqkv = _dqkv_assemble(dq, dkc, dkp, name="att_dqkv")
    d_wqkv = _matmul(x2, dqkv, name="att_qkv_dw", ta=True, tm=1024, tn=dqkv.shape[1], tk=1024, out_dtype=BF16)
    d_wqkv_st = jnp.moveaxis(d_wqkv.reshape(d_wqkv.shape[0], 4, -1), 1, 0)
    tok = send_grads("att", [owner_view(d_wqkv_st), owner_view(d_wo)], d_bqkv)
    dx2 = _matmul(dqkv, w_qkv, name="att_qkv_dx", tb=True, tm=1024, tn=1024, tk=dqkv.shape[1], addend=(dz, alpha))
    dz, dg01, db01 = _add_ln_bwd([(dx2, 1.0)], x1, ffn0, lng(0, 1) + tok, alpha, name="ln01_bwd")
    dx1, dcw0, dcb0, tok = ffn_bwd(dz, x1, hf0, f0, 0)
    dz, dg00, db00 = _add_ln_bwd(dx1, xs, mix, lng(0, 0) + tok, alpha, name="ln00_bwd")
    d_wout = _matmul(ab, dz, name="mix_out_dw", ta=True, tm=1024, tn=1024, tk=1024, out_dtype=BF16)
    dab = _matmul(dz, w_out, name="mix_out_dx", tb=True, tm=1024, tn=1024, tk=1024)
    small_dims = (a_conv_w.shape[1], a_conv_b.shape[1], b_spatial_w.shape[1], 4 * c_b_qkv.shape[1], n_q, f, d)

    def send_small(local_arrays, which, tag, after):
        groups = _small_pack(local_arrays, small_dims, which, name=f"small_pack_{tag}")
        lands = [lax.empty((8,) + g.shape, F32) for g in groups]
        return _reduce_start(groups, lands, after, name=f"small_start_{tag}", whole=True)

    ready = [d_bqkv, d_sinks, dcw0, dcb0, dcw1, dcb1, dg00, dg01, dg10, dg11, db00, db01, db10, db11, sq_err]
    ss_e, rs_e, g_e, l_e, token = send_small(dict(zip(range(8, 23), ready)), (2, 3, 4), "early", dab)
    grad_x, d_win, d_cwa, d_cba, d_ga, d_ba, d_gb, d_bb, d_ws, d_sb = _mixer_bwd(
        h0, a2, dab, xs, w_in, dz, alpha, conv_w_a, *mix_vecs[1:], ws, wst, sbb, tril + token[0:1, 0:1], name="mix_bwd")

    small_w = [a_conv_w, a_conv_b, a_norm_g, a_norm_b, b_norm_g, b_norm_b, b_spatial_w, b_spatial_b, c_b_qkv,
               c_sinks, ffn_conv_w, ffn_conv_b, ln_g, ln_b]
    small_m = [m_a_conv_w, m_a_conv_b, m_a_norm_g, m_a_norm_b, m_b_norm_g, m_b_norm_b, m_b_spatial_w, m_b_spatial_b,
               m_c_b_qkv, m_c_sinks, m_ffn_conv_w, m_ffn_conv_b, m_ln_g, m_ln_b]
    small_v = [v_a_conv_w, v_a_conv_b, v_a_norm_g, v_a_norm_b, v_b_norm_g, v_b_norm_b, v_b_spatial_w, v_b_spatial_b,
               v_c_b_qkv, v_c_sinks, v_ffn_conv_w, v_ffn_conv_b, v_ln_g, v_ln_b]
    place = jnp.stack([q_idx, c_idx, 4 * lax.axis_index("x") + 2 * lax.axis_index("y") + c_idx]).astype(jnp.int32)
    where = {"mix": [(0, None), (1, None)], "att": [(2, None), (3, None)], "ffn0": [(4, 0), (5, 0)], "ffn1": [(4, 1), (5, 1)]}
    big_w = [ab_w_in, ab_w_out, c_w_qkv, c_w_o, ffn_w_up, ffn_w_down]
    big_m = [m_ab_w_in, m_ab_w_out, m_c_w_qkv, m_c_w_o, m_ffn_w_up, m_ffn_w_down]
    big_v = [v_ab_w_in, v_ab_w_out, v_c_w_qkv, v_c_w_o, v_ffn_w_up, v_ffn_w_down]
    big_out = [None] * 6

    def finish(tags, after, label):
        bufs, layout = {}, []
        for tag, g_thru, l_thru, ss, rs in in_flight:
            if tag not in tags:
                continue
            own, landed = _reduce_wait(g_thru, l_thru, ss, rs, after, name=f"reduce_wait_{tag}")
            for k, (o, lead) in enumerate(where[tag]):
                piece = own[k].shape[2:]
                shape = (2,) + piece if lead is None else (2, 2) + piece
                bufs[o] = _octo_sum(own[k], landed[k], place, bufs.get(o), (lead, shape), name=f"reduce_sum_{tag}{k}")
                layout.append((o, lead))
        order = sorted(bufs)
        shared = _sibling_share([bufs[o] for o in order], [(order.index(o), lead) for o, lead in layout],
                                name=f"reduce_share_{label}")
        for o, g in zip(order, shared):
            w = big_w[o]
            two_d = lambda a: a.reshape(-1, a.shape[-1])
            outs = _adamw(two_d(w), two_d(g), two_d(big_m[o]), two_d(big_v[o]), name=f"adamw_big{o}")
            big_out[o] = [r.reshape(w.shape) for r in outs]
        return big_out[order[-1]][0]

    mixer_small = [d_cwa, d_cba, d_ga, d_ba, d_gb, d_bb, d_ws, d_sb]
    ss_l, rs_l, g_l, l_l, token = send_small(dict(zip(range(8), mixer_small)), (0, 1), "late", grad_x)
    tok = send_grads("mix", [owner_view(d_win), owner_view(d_wout)], after=token)
    done = finish(("ffn1", "att", "ffn0"), d_ws + tok, "early")
    g_l, l_l = _reduce_wait(g_l, l_l, ss_l, rs_l, done, name="small_wait_late", whole=True)
    g_e, l_e = _reduce_wait(g_e, l_e, ss_e, rs_e, done, name="small_wait_early", whole=True)
    small_out, loss = _small_update(g_l + g_e, l_l + l_e, list(zip(small_w, small_m, small_v)), small_dims,
                                    name="small_tail")
    loss = loss[0, 0]
    small_g = [o[0] for o in small_out]
    sm_delta = [o[1] for o in small_out]
    sm_m = [o[2] for o in small_out]
    sm_v = [o[3] for o in small_out]
    finish(("mix",), sm_delta[6], "mix")

    order_big = {0: 0, 9: 1, 10: 2, 13: 3, 14: 4, 17: 5}
    order_small = {1: 0, 2: 1, 3: 2, 4: 3, 5: 4, 6: 5, 7: 6, 8: 7, 11: 8, 12: 9, 15: 10, 16: 11, 18: 12, 19: 13}
    grads, deltas, new_m, new_v = [], [], [], []
    for pos_w in range(20):
        if pos_w in order_big:
            t = order_big[pos_w]
            grads.append(big_out[t][3])
            deltas.append(big_out[t][0])
            new_m.append(big_out[t][1])
            new_v.append(big_out[t][2])
        else:
            t = order_small[pos_w]
            grads.append(small_g[t])
            deltas.append(sm_delta[t])
            new_m.append(sm_m[t])
            new_v.append(sm_v[t])
    return (loss, grad_x[None], *grads, *deltas, *new_m, *new_v)
```

```python
import math

import jax
import jax.numpy as jnp
from jax import lax
from jax.experimental import pallas as pl
from jax.experimental.pallas import tpu as pltpu

F32 = jnp.float32
BF16 = jnp.bfloat16
MESH = pl.DeviceIdType.MESH

LN_EPS = 1e-5
HEAD_DIM = 64
ATT_BLOCK = 128
Q_PER_KV = 8
A_KERNEL = 31
CONV_HALO = 32
FFN_HALO = 8
BF16_ROWS = 16
B_CHUNK = 128
LANES = 128
MXU_WIDTH = 256
GELU_C = math.sqrt(2.0 / math.pi)
ADAM_LR = 0.001
ADAM_B1 = 0.9
ADAM_B2 = 0.999
ADAM_EPS = 1e-08
ADAM_WD = 0.01
ADAM_STEP = 10
VMEM_LIMIT = 56 * 1024 * 1024


def _cp(*dims):
    return pltpu.CompilerParams(dimension_semantics=dims, vmem_limit_bytes=VMEM_LIMIT)


def _pick(n, prefs):
    for p in prefs:
        if n % p == 0:
            return p
    return n


def _sig(x):
    return 1.0 / (1.0 + jnp.exp(-x))


def _gelu(x):
    t = jnp.tanh(GELU_C * (x + 0.044715 * (x * x * x)))
    return x * (0.5 * (1.0 + t)), t


def _gelu_grad(x, t):
    return 0.5 * (1.0 + t) + 0.5 * x * (1.0 - t * t) * (GELU_C * (1.0 + 3.0 * 0.044715 * x * x))


def _ln_stats(z):
    mu = jnp.mean(z, axis=-1, keepdims=True)
    zc = z - mu
    var = jnp.mean(zc * zc, axis=-1, keepdims=True)
    rstd = lax.rsqrt(var + LN_EPS)
    return zc * rstd, rstd


def _ln_bwd(dxh, xh, rstd):
    return rstd * (dxh - jnp.mean(dxh, axis=-1, keepdims=True) - xh * jnp.mean(dxh * xh, axis=-1, keepdims=True))


def _rowsum(a):
    return jnp.sum(a, axis=0, keepdims=True)


def _lshape(a):
    return (a.shape[0], a.shape[1]) if a.ndim == 2 else (a.shape[1], a.shape[0] * a.shape[2])


def _spec2(arr, blk_r, blk_c, ridx, cidx):
    if len(arr.shape) == 2:
        return pl.BlockSpec((blk_r, blk_c), lambda i, j, k: (ridx(i, j, k), cidx(i, j, k)))
    per = arr.shape[2] // blk_c
    assert arr.shape[2] % blk_c == 0
    return pl.BlockSpec((None, blk_r, blk_c), lambda i, j, k: (cidx(i, j, k) // per, ridx(i, j, k), cidx(i, j, k) % per))


def _matmul(a, b, *, name, ta=False, tb=False, tm, tn, tk, out_dtype=F32, out_stack=None, bias=None, addend=None):
    ar, ac = _lshape(a)
    br, bc = _lshape(b)
    m, kdim = (ac, ar) if ta else (ar, ac)
    n = br if tb else bc
    assert (bc if tb else br) == kdim
    tm, tn, tk = min(tm, m), min(tn, n), min(tk, kdim)
    assert m % tm == 0 and n % tn == 0 and kdim % tk == 0, (name, m, n, kdim, tm, tn, tk)
    nk = kdim // tk
    gi, gj, gk = (lambda i, j, k: i), (lambda i, j, k: j), (lambda i, j, k: k)
    a_spec = _spec2(a, tk, tm, gk, gi) if ta else _spec2(a, tm, tk, gi, gk)
    b_spec = _spec2(b, tn, tk, gj, gk) if tb else _spec2(b, tk, tn, gk, gj)
    if out_stack is None:
        out_sds = jax.ShapeDtypeStruct((m, n), out_dtype)
    else:
        out_sds = jax.ShapeDtypeStruct((out_stack, m, n // out_stack), out_dtype)
    o_spec = _spec2(out_sds, tm, tn, gi, gj)
    in_specs = [a_spec, b_spec]
    args = [a, b]
    if bias is not None:
        in_specs.append(pl.BlockSpec((1, tn), lambda i, j, k: (0, j)))
        args.append(bias)
    scale = None
    if addend is not None:
        add_arr, scale = addend
        in_specs.append(pl.BlockSpec((tm, tn), lambda i, j, k: (i, j)))
        args.append(add_arr)
    use_acc = nk > 1 and out_dtype != F32
    dn = (((0 if ta else 1,), (1 if tb else 0,)), ((), ()))

    def body(*refs):
        a_ref, b_ref = refs[0], refs[1]
        pos = 2
        bias_ref = add_ref = None
        if bias is not None:
            bias_ref = refs[pos]
            pos += 1
        if addend is not None:
            add_ref = refs[pos]
            pos += 1
        o_ref = refs[pos]
        acc_ref = refs[pos + 1] if use_acc else o_ref
        p = lax.dot_general(a_ref[...].astype(BF16), b_ref[...].astype(BF16), dn, preferred_element_type=F32)

        def finish(val):
            if bias_ref is not None:
                val = val + bias_ref[...]
            if add_ref is not None:
                val = val + scale * add_ref[...]
            return val.astype(out_dtype)

        if nk == 1:
            o_ref[...] = finish(p)
        else:
            k = pl.program_id(2)

            @pl.when(k == 0)
            def _():
                acc_ref[...] = p

            @pl.when(k > 0)
            def _():
                acc_ref[...] += p

            if use_acc or bias_ref is not None or add_ref is not None:
                @pl.when(k == nk - 1)
                def _():
                    o_ref[...] = finish(acc_ref[...])

    return pl.pallas_call(
        body, name=name, grid=(m // tm, n // tn, nk), in_specs=in_specs, out_specs=o_spec, out_shape=out_sds,
        scratch_shapes=[pltpu.VMEM((tm, tn), F32)] if use_acc else [],
        compiler_params=_cp("parallel", "parallel", "arbitrary"),
    )(*args)


def _add_ln_fwd(x, s, g, b, alpha, *, name):
    rows, d = x.shape
    t = _pick(rows, (512, 256))

    def body(x_ref, s_ref, g_ref, b_ref, y_ref):
        xh, _ = _ln_stats(alpha * x_ref[...] + s_ref[...])
        y_ref[...] = xh * g_ref[...] + b_ref[...]

    row = pl.BlockSpec((t, d), lambda i: (i, 0))
    vec = pl.BlockSpec((1, d), lambda i: (0, 0))
    return pl.pallas_call(body, name=name, grid=(rows // t,), in_specs=[row, row, vec, vec], out_specs=row,
                          out_shape=jax.ShapeDtypeStruct((rows, d), F32), compiler_params=_cp("parallel"))(x, s, g, b)


def _add_ln_bwd(dy_terms, x, s, g, alpha, *, name):
    rows, d = x.shape
    t = _pick(rows, (512, 256))
    nterm = len(dy_terms)
    scales = [sc for _, sc in dy_terms]
    ranks = [a.ndim for a, _ in dy_terms]

    def body(*refs):
        dy_refs = refs[:nterm]
        x_ref, s_ref, g_ref, dz_ref, dg_ref, db_ref = refs[nterm:]

        @pl.when(pl.program_id(0) == 0)
        def _():
            dg_ref[...] = jnp.zeros_like(dg_ref)
            db_ref[...] = jnp.zeros_like(db_ref)

        dyv = None
        for r, sc, rank in zip(dy_refs, scales, ranks):
            slabs = [r[...]] if rank == 2 else [r[p] for p in range(r.shape[0])]
            for v in slabs:
                v = v if sc == 1.0 else sc * v
                dyv = v if dyv is None else dyv + v
        xh, rstd = _ln_stats(alpha * x_ref[...] + s_ref[...])
        dz_ref[...] = _ln_bwd(dyv * g_ref[...], xh, rstd)
        dg_ref[...] += _rowsum(dyv * xh)
        db_ref[...] += _rowsum(dyv)

    row = pl.BlockSpec((t, d), lambda i: (i, 0))
    vec = pl.BlockSpec((1, d), lambda i: (0, 0))
    vsds = jax.ShapeDtypeStruct((1, d), F32)
    dy_specs = [row if a.ndim == 2 else pl.BlockSpec((a.shape[0], t, d), lambda i: (0, i, 0)) for a, _ in dy_terms]
    return pl.pallas_call(body, name=name, grid=(rows // t,), in_specs=dy_specs + [row, row, vec], out_specs=[row, vec, vec],
                          out_shape=[jax.ShapeDtypeStruct((rows, d), F32), vsds, vsds],
                          compiler_params=_cp("arbitrary"))(*[a for a, _ in dy_terms], x, s, g)


def _add_ln_loss(x, s, g, b, tgt, alpha, *, name):
    rows, d = x.shape
    t = _pick(rows, (512, 256))

    def body(x_ref, s_ref, g_ref, b_ref, t_ref, l_ref, dy_ref):
        @pl.when(pl.program_id(0) == 0)
        def _():
            l_ref[...] = jnp.zeros_like(l_ref)

        xh, _ = _ln_stats(alpha * x_ref[...] + s_ref[...])
        e = (xh * g_ref[...] + b_ref[...]) - t_ref[...]
        l_ref[...] += _rowsum(e * e)
        dy_ref[...] = e * (1.0 / d)

    row = pl.BlockSpec((t, d), lambda i: (i, 0))
    vec = pl.BlockSpec((1, d), lambda i: (0, 0))
    return pl.pallas_call(body, name=name, grid=(rows // t,), in_specs=[row, row, vec, vec, row], out_specs=[vec, row],
                          out_shape=[jax.ShapeDtypeStruct((1, d), F32), jax.ShapeDtypeStruct((rows, d), F32)],
                          compiler_params=_cp("arbitrary"))(x, s, g, b, tgt)


def _col_blocks(width, step):
    return [slice(pos, min(pos + step, width)) for pos in range(0, width, step)]


def _conv3(e, w, b):
    r1 = pltpu.roll(e, 1, 0)
    r2 = pltpu.roll(e, 2, 0)
    return w[0:1, :] * r2 + w[1:2, :] * r1 + w[2:3, :] * e + b, (r2, r1, e)


def _ffn_up_fwd(x, w_up, cw, cb, *, name):
    rows, d = x.shape
    nq, _, tc = w_up.shape
    nj = nq // 2
    f = tc * nj
    tm = _pick(rows, (512, 256))
    blocks = _col_blocks(tc, tc)

    def body(x_ref, wg_ref, wv_ref, cw_ref, cb_ref, hf_ref, f_ref, prev_ref):
        @pl.when(pl.program_id(1) == 0)
        def _():
            prev_ref[...] = jnp.zeros_like(prev_ref)

        xb = x_ref[...].astype(BF16)
        for cs in blocks:
            hc = []
            for s, w_ref in ((0, wg_ref), (1, wv_ref)):
                h = jnp.dot(xb, w_ref[:, cs], preferred_element_type=F32)
                hf_ref[s, :, cs] = h
                e = jnp.concatenate([prev_ref[s, :, cs], h], axis=0)
                prev_ref[s, :, cs] = h[tm - FFN_HALO:]
                y, _ = _conv3(e, cw_ref[s, :, cs], cb_ref[s, :, cs])
                hc.append(y[FFN_HALO:])
            gl, _ = _gelu(hc[0])
            f_ref[:, cs] = (gl * hc[1]).astype(BF16)

    in_specs = [
        pl.BlockSpec((tm, d), lambda j, i: (i, 0)),
        pl.BlockSpec((None, d, tc), lambda j, i: (j, 0, 0)),
        pl.BlockSpec((None, d, tc), lambda j, i: (nj + j, 0, 0)),
        pl.BlockSpec((2, 3, tc), lambda j, i: (0, 0, j)),
        pl.BlockSpec((2, 1, tc), lambda j, i: (0, 0, j)),
    ]
    out_specs = [pl.BlockSpec((2, tm, tc), lambda j, i: (0, i, j)), pl.BlockSpec((tm, tc), lambda j, i: (i, j))]
    out_shape = [jax.ShapeDtypeStruct((2, rows, f), F32), jax.ShapeDtypeStruct((rows, f), BF16)]
    return pl.pallas_call(body, name=name, grid=(nj, rows // tm), in_specs=in_specs, out_specs=out_specs, out_shape=out_shape,
                          scratch_shapes=[pltpu.VMEM((2, FFN_HALO, tc), F32)],
                          compiler_params=_cp("parallel", "arbitrary"))(x, w_up, w_up, cw, cb)


def _ffn_up_bwd(hf, df, x, w_up, cw, cb, *, name):
    _, rows, f = hf.shape
    d = x.shape[1]
    nq, _, tc = w_up.shape
    nj = nq // 2
    tm = _pick(rows, (512, 256))
    hb = tm // FFN_HALO
    once = pl.Buffered(1)
    ni = rows // tm
    last_blk = rows // FFN_HALO - 1
    ext = tm + 2 * FFN_HALO
    tile = slice(FFN_HALO, FFN_HALO + tm)
    blocks = _col_blocks(tc, MXU_WIDTH)

    def body(h_ref, hp_ref, hn_ref, d_ref, dn_ref, x_ref, wg_ref, wv_ref, cw_ref, cb_ref, dx_ref, dw_out_ref, dcw_ref, dcb_ref,
             dw_ref):
        i = pl.program_id(1)
        first = i == 0
        last = i == ni - 1

        @pl.when(first)
        def _():
            dw_ref[...] = jnp.zeros_like(dw_ref)
            dcw_ref[...] = jnp.zeros_like(dcw_ref)
            dcb_ref[...] = jnp.zeros_like(dcb_ref)

        xt = x_ref[...].astype(BF16).T
        dx = None
        for cs in blocks:
            wc = cs.stop - cs.start
            d_next = dn_ref[:, cs].astype(F32)[0:FFN_HALO]
            de = jnp.concatenate([jnp.zeros((FFN_HALO, wc), F32), d_ref[:, cs].astype(F32), jnp.where(last, 0.0, d_next)], axis=0)
            taps, hc = [], []
            for s in range(2):
                e = jnp.concatenate([jnp.where(first, 0.0, hp_ref[s, :, cs]), h_ref[s, :, cs], hn_ref[s, :, cs]], axis=0)
                y, tp = _conv3(e, cw_ref[s, :, cs], cb_ref[s, :, cs])
                hc.append(y)
                taps.append(tp)
            gl, th = _gelu(hc[0])
            dhc = (de * hc[1] * _gelu_grad(hc[0], th), de * gl)
            for s, w_ref in ((0, wg_ref), (1, wv_ref)):
                w = cw_ref[s, :, cs]
                g = dhc[s]
                dh = (w[2:3, :] * g + w[1:2, :] * pltpu.roll(g, ext - 1, 0) + w[0:1, :] * pltpu.roll(g, ext - 2, 0))[tile]
                gt = g[tile]
                for k in range(3):
                    dcw_ref[s, k:k + 1, cs] += _rowsum(gt * taps[s][k][tile])
                dcb_ref[s, :, cs] += _rowsum(gt)
                dhb = dh.astype(BF16)
                part = lax.dot_general(dhb, w_ref[:, cs], (((1,), (1,)), ((), ())), preferred_element_type=F32)
                dx = part if dx is None else dx + part
                dw_ref[s, :, cs] += jnp.dot(xt, dhb, preferred_element_type=F32)
        dx_ref[...] = dx

        @pl.when(last)
        def _():
            dw_out_ref[...] = dw_ref[...].astype(BF16)

    in_specs = [
        pl.BlockSpec((2, tm, tc), lambda j, i: (0, i, j)),
        pl.BlockSpec((2, FFN_HALO, tc), lambda j, i: (0, jnp.maximum(i * hb - 1, 0), j)),
        pl.BlockSpec((2, FFN_HALO, tc), lambda j, i: (0, jnp.minimum((i + 1) * hb, last_blk), j)),
        pl.BlockSpec((tm, tc), lambda j, i: (i, j)),
        pl.BlockSpec((BF16_ROWS, tc), lambda j, i: (jnp.minimum((i + 1) * (tm // BF16_ROWS), rows // BF16_ROWS - 1), j)),
        pl.BlockSpec((tm, d), lambda j, i: (i, 0)),
        pl.BlockSpec((None, d, tc), lambda j, i: (j, 0, 0), pipeline_mode=once),
        pl.BlockSpec((None, d, tc), lambda j, i: (nj + j, 0, 0), pipeline_mode=once),
        pl.BlockSpec((2, 3, tc), lambda j, i: (0, 0, j)),
        pl.BlockSpec((2, 1, tc), lambda j, i: (0, 0, j)),
    ]
    out_specs = [
        pl.BlockSpec((None, tm, d), lambda j, i: (j, i, 0)),
        pl.BlockSpec((2, None, d, tc), lambda j, i: (0, j, 0, 0), pipeline_mode=once),
        pl.BlockSpec((2, 3, tc), lambda j, i: (0, 0, j)),
        pl.BlockSpec((2, 1, tc), lambda j, i: (0, 0, j)),
    ]
    out_shape = [jax.ShapeDtypeStruct((nj, rows, d), F32), jax.ShapeDtypeStruct((2, nj, d, tc), BF16),
                 jax.ShapeDtypeStruct((2, 3, f), F32), jax.ShapeDtypeStruct((2, 1, f), F32)]
    dx, dw, dcw, dcb = pl.pallas_call(body, name=name, grid=(nj, ni), in_specs=in_specs, out_specs=out_specs,
                                      out_shape=out_shape, scratch_shapes=[pltpu.VMEM((2, d, tc), F32)],
                                      compiler_params=_cp("parallel", "arbitrary"))(
        hf, hf, hf, df, df, x, w_up, w_up, cw, cb)
    return dx, dw.reshape(nq, d, tc), dcw, dcb


def _mixer_fwd(x, w_in, cw, cb, ga, ba, gb, bb, ws, sbb, *, name):
    rows, d = x.shape
    _, _, w = w_in.shape
    t = _pick(rows, (256,))
    groups = w // B_CHUNK

    def body(x_ref, win_ref, cw_ref, cb_ref, ga_ref, ba_ref, gb_ref, bb_ref, ws_ref, sb_ref, h_ref, o_ref, a2_ref, prev_ref):
        @pl.when(pl.program_id(0) == 0)
        def _():
            prev_ref[...] = jnp.zeros_like(prev_ref)

        xb = x_ref[...].astype(BF16)
        for s in range(4):
            h_ref[s] = jnp.dot(xb, win_ref[s], preferred_element_type=F32)
        a1 = h_ref[0] * _sig(h_ref[1])
        e = jnp.concatenate([prev_ref[...], a1], axis=0)
        prev_ref[...] = a1[t - CONV_HALO:]
        acc = cw_ref[A_KERNEL - 1:A_KERNEL, :] * e
        for k in range(A_KERNEL - 1):
            acc = acc + cw_ref[k:k + 1, :] * pltpu.roll(e, A_KERNEL - 1 - k, 0)
        a2 = acc[CONV_HALO:] + cb_ref[...]
        a2_ref[...] = a2
        xh, _ = _ln_stats(a2)
        a3 = xh * ga_ref[...] + ba_ref[...]
        o_ref[:, 0:w] = (a3 * _sig(a3)).astype(BF16)

        u, _ = _gelu(h_ref[2])
        v1, _ = _gelu(h_ref[3])
        xh2, _ = _ln_stats(v1)
        v2 = (xh2 * gb_ref[...] + bb_ref[...]).astype(BF16)
        for c in range(t // B_CHUNK):
            rs = slice(c * B_CHUNK, (c + 1) * B_CHUNK)
            for g in range(groups):
                cs = slice(g * B_CHUNK, (g + 1) * B_CHUNK)
                mixed = jnp.dot(ws_ref[g], v2[rs, cs], preferred_element_type=F32) + sb_ref[g]
                o_ref[rs, w + g * B_CHUNK:w + (g + 1) * B_CHUNK] = (u[rs, cs] * mixed).astype(BF16)

    vec = pl.BlockSpec((1, w), lambda i: (0, 0))
    grp = pl.BlockSpec((groups, B_CHUNK, B_CHUNK), lambda i: (0, 0, 0))
    in_specs = [
        pl.BlockSpec((t, d), lambda i: (i, 0)),
        pl.BlockSpec((4, d, w), lambda i: (0, 0, 0)),
        pl.BlockSpec((A_KERNEL, w), lambda i: (0, 0)),
        vec, vec, vec, vec, vec, grp, grp,
    ]
    out_specs = [pl.BlockSpec((4, t, w), lambda i: (0, i, 0)), pl.BlockSpec((t, 2 * w), lambda i: (i, 0)),
                 pl.BlockSpec((t, w), lambda i: (i, 0))]
    out_shape = [jax.ShapeDtypeStruct((4, rows, w), F32), jax.ShapeDtypeStruct((rows, 2 * w), BF16),
                 jax.ShapeDtypeStruct((rows, w), F32)]
    return pl.pallas_call(body, name=name, grid=(rows // t,), in_specs=in_specs, out_specs=out_specs, out_shape=out_shape,
                          scratch_shapes=[pltpu.VMEM((CONV_HALO, w), F32)],
                          compiler_params=_cp("arbitrary"))(x, w_in, cw, cb, ga, ba, gb, bb, ws, sbb)


def _mixer_bwd(h0, a2, dab, x, w_in, res, res_scale, cw, ga, ba, gb, bb, ws, wst, sbb, tril, *, name):
    _, rows, w = h0.shape
    d = x.shape[1]
    once = pl.Buffered(1)
    t = _pick(rows, (256,))
    hb = t // CONV_HALO
    ni = rows // t
    last_blk = rows // CONV_HALO - 1
    ext = t + CONV_HALO
    tile = slice(0, t)
    groups = w // B_CHUNK
    taps = A_KERNEL - 1

    def body(h_ref, a2_ref, a2n_ref, d_ref, dn_ref, x_ref, win_ref, res_ref, cw_ref, ga_ref, ba_ref, gb_ref, bb_ref,
             ws_ref, wst_ref, sb_ref, tril_ref, dx_ref, dwin_ref, dcw_ref, dcb_ref, dga_ref, dba_ref, dgb_ref, dbb_ref,
             dws_ref, dsb_ref, dw_ref):
        i = pl.program_id(0)
        first = i == 0
        last = i == ni - 1

        @pl.when(first)
        def _():
            for r in (dw_ref, dcw_ref, dcb_ref, dga_ref, dba_ref, dgb_ref, dbb_ref, dws_ref, dsb_ref):
                r[...] = jnp.zeros_like(r)

        xt = x_ref[...].astype(BF16).T
        dx_terms = []

        def through_w_in(slot, dh):
            dhb = dh.astype(BF16)
            dx_terms.append(lax.dot_general(dhb, win_ref[slot], (((1,), (1,)), ((), ())), preferred_element_type=F32))
            dw_ref[slot] += jnp.dot(xt, dhb, preferred_element_type=F32)

        xh, rstd = _ln_stats(jnp.concatenate([a2_ref[...], a2n_ref[...]], axis=0))
        a3 = xh * ga_ref[...] + ba_ref[...]
        s3 = _sig(a3)
        da_e = jnp.concatenate([d_ref[:, 0:w], jnp.where(last, 0.0, dn_ref[...])], axis=0)
        da3 = da_e * (s3 * (1.0 + a3 * (1.0 - s3)))
        da2 = _ln_bwd(da3 * ga_ref[...], xh, rstd)
        dga_ref[...] += _rowsum(da3[tile] * xh[tile])
        dba_ref[...] += _rowsum(da3[tile])
        dcb_ref[...] += _rowsum(da2[tile])
        sgt = _sig(h_ref[1])
        a1t = h_ref[0] * sgt
        da1t = None
        for k in range(A_KERNEL):
            sh = taps - k
            fed = (da2 if sh == 0 else pltpu.roll(da2, ext - sh, 0))[tile]
            dcw_ref[k:k + 1, :] += _rowsum(a1t * fed)
            term = cw_ref[k:k + 1, :] * fed
            da1t = term if da1t is None else da1t + term
        through_w_in(0, da1t * sgt)
        through_w_in(1, da1t * h_ref[0] * sgt * (1.0 - sgt))

        bu = h_ref[2]
        bv = h_ref[3]
        u, tu = _gelu(bu)
        v1, tv = _gelu(bv)
        xh2, rstd2 = _ln_stats(v1)
        v2 = (xh2 * gb_ref[...] + bb_ref[...]).astype(BF16)
        db = d_ref[:, w:2 * w]
        dmx_all = db * u
        du_parts, dv2_parts = [], []
        for c in range(t // B_CHUNK):
            rs = slice(c * B_CHUNK, (c + 1) * B_CHUNK)
            du_row, dv2_row = [], []
            for g in range(groups):
                cs = slice(g * B_CHUNK, (g + 1) * B_CHUNK)
                v2cg = v2[rs, cs]
                mixed = jnp.dot(ws_ref[g], v2cg, preferred_element_type=F32) + sb_ref[g]
                dmx = dmx_all[rs, cs]
                dmxb = dmx.astype(BF16)
                du_row.append(db[rs, cs] * mixed)
                dv2_row.append(jnp.dot(wst_ref[g], dmxb, preferred_element_type=F32))
                dws_ref[g] += tril_ref[...] * lax.dot_general(dmxb, v2cg, (((1,), (1,)), ((), ())),
                                                               preferred_element_type=F32)
                dsb_ref[g:g + 1, :] += _rowsum(dmx.T)
            du_parts.append(jnp.concatenate(du_row, axis=1))
            dv2_parts.append(jnp.concatenate(dv2_row, axis=1))
        du = jnp.concatenate(du_parts, axis=0)
        dv2 = jnp.concatenate(dv2_parts, axis=0)
        dgb_ref[...] += _rowsum(dv2 * xh2)
        dbb_ref[...] += _rowsum(dv2)
        dv1 = _ln_bwd(dv2 * gb_ref[...], xh2, rstd2)
        through_w_in(2, du * _gelu_grad(bu, tu))
        through_w_in(3, dv1 * _gelu_grad(bv, tv))
        dx_ref[...] = res_scale * res_ref[...] + ((dx_terms[0] + dx_terms[1]) + (dx_terms[2] + dx_terms[3]))

        @pl.when(last)
        def _():
            dwin_ref[...] = dw_ref[...].astype(BF16)

    vec = pl.BlockSpec((1, w), lambda i: (0, 0))
    grp = pl.BlockSpec((groups, B_CHUNK, B_CHUNK), lambda i: (0, 0, 0))
    halo = pl.BlockSpec((CONV_HALO, w), lambda i: (jnp.minimum((i + 1) * hb, last_blk), 0))
    wide = pl.BlockSpec((t, d), lambda i: (i, 0))
    in_specs = [
        pl.BlockSpec((4, t, w), lambda i: (0, i, 0)),
        pl.BlockSpec((t, w), lambda i: (i, 0)),
        halo,
        pl.BlockSpec((t, 2 * w), lambda i: (i, 0)),
        halo,
        wide,
        pl.BlockSpec((4, d, w), lambda i: (0, 0, 0), pipeline_mode=once),
        wide,
        pl.BlockSpec((A_KERNEL, w), lambda i: (0, 0)),
        vec, vec, vec, vec, grp, grp, grp,
        pl.BlockSpec((B_CHUNK, B_CHUNK), lambda i: (0, 0)),
    ]
    vsds = jax.ShapeDtypeStruct((1, w), F32)
    out_specs = [
        wide,
        pl.BlockSpec((4, d, w), lambda i: (0, 0, 0), pipeline_mode=once),
        pl.BlockSpec((A_KERNEL, w), lambda i: (0, 0)),
        vec, vec, vec, vec, vec, grp,
        pl.BlockSpec((groups, B_CHUNK), lambda i: (0, 0)),
    ]
    out_shape = [jax.ShapeDtypeStruct((rows, d), F32), jax.ShapeDtypeStruct((4, d, w), BF16),
                 jax.ShapeDtypeStruct((A_KERNEL, w), F32),
                 vsds, vsds, vsds, vsds, vsds, jax.ShapeDtypeStruct((groups, B_CHUNK, B_CHUNK), F32),
                 jax.ShapeDtypeStruct((groups, B_CHUNK), F32)]
    return pl.pallas_call(body, name=name, grid=(ni,), in_specs=in_specs, out_specs=out_specs, out_shape=out_shape,
                          scratch_shapes=[pltpu.VMEM((4, d, w), F32)], compiler_params=_cp("arbitrary"))(
        h0, a2, a2, dab, dab, x, w_in, res, cw, ga, ba, gb, bb, ws, wst, sbb, tril)


GROUP_ROWS = Q_PER_KV * ATT_BLOCK


def _attn_mask(n):
    qi = lax.broadcasted_iota(jnp.int32, (GROUP_ROWS, 2 * ATT_BLOCK), 0) & (ATT_BLOCK - 1)
    sj = lax.broadcasted_iota(jnp.int32, (GROUP_ROWS, 2 * ATT_BLOCK), 1)
    diff = qi + ATT_BLOCK - sj
    return (diff >= 0) & (diff < ATT_BLOCK) & ((n > 0) | (sj >= ATT_BLOCK))


def _stack_heads(ref, kvh, dtype):
    heads = [ref[:, (kvh * Q_PER_KV + g) * HEAD_DIM:(kvh * Q_PER_KV + g + 1) * HEAD_DIM] for g in range(Q_PER_KV)]
    return jnp.concatenate(heads, axis=0).astype(dtype)


def _per_row_sink(sink_ref, kvh):
    head = lax.broadcasted_iota(jnp.int32, (GROUP_ROWS, 1), 0) // ATT_BLOCK
    out = jnp.zeros((GROUP_ROWS, 1), F32)
    for g in range(Q_PER_KV):
        out = jnp.where(head == g, sink_ref[kvh * Q_PER_KV + g], out)
    return out


def _attn_specs(rows, n_q):
    dq = n_q * HEAD_DIM
    dkv = 2 * (n_q // Q_PER_KV) * HEAD_DIM
    kv_blk = dq // dkv
    assert dq % dkv == 0
    return dq, dkv, [
        pl.BlockSpec(memory_space=pltpu.SMEM),
        pl.BlockSpec((ATT_BLOCK, dq), lambda n: (n, 0)),
        pl.BlockSpec((ATT_BLOCK, dkv), lambda n: (n, kv_blk)),
        pl.BlockSpec((ATT_BLOCK, dkv), lambda n: (jnp.maximum(n - 1, 0), kv_blk)),
    ]


def _kv_pair(kvc_ref, kvp_ref, kvh, n_kv):
    ks = slice(kvh * HEAD_DIM, (kvh + 1) * HEAD_DIM)
    vs = slice((n_kv + kvh) * HEAD_DIM, (n_kv + kvh + 1) * HEAD_DIM)
    kk = jnp.concatenate([kvp_ref[:, ks], kvc_ref[:, ks]], axis=0).astype(BF16)
    vv = jnp.concatenate([kvp_ref[:, vs], kvc_ref[:, vs]], axis=0).astype(BF16)
    return kk, vv


def _attn_fwd(qkv, sinks, *, name):
    rows = qkv.shape[0]
    n_q = sinks.shape[0]
    n_kv = n_q // Q_PER_KV
    scale = 1.0 / math.sqrt(HEAD_DIM)
    dq, _, in_specs = _attn_specs(rows, n_q)

    def body(sink_ref, q_ref, kvc_ref, kvp_ref, o_ref, lse_ref):
        valid = _attn_mask(pl.program_id(0))
        for kvh in range(n_kv):
            kk, vv = _kv_pair(kvc_ref, kvp_ref, kvh, n_kv)
            qs = _stack_heads(q_ref, kvh, BF16)
            s = lax.dot_general(qs, kk, (((1,), (1,)), ((), ())), preferred_element_type=F32)
            s = jnp.where(valid, s * scale, -jnp.inf)
            sk = _per_row_sink(sink_ref, kvh)
            m = jnp.maximum(jnp.max(s, axis=1, keepdims=True), sk)
            p = jnp.exp(s - m)
            l = jnp.sum(p, axis=1, keepdims=True) + jnp.exp(sk - m)
            o = jnp.dot((p / l).astype(BF16), vv, preferred_element_type=F32)
            lse = m + jnp.log(l)
            for g in range(Q_PER_KV):
                h = kvh * Q_PER_KV + g
                rs = slice(g * ATT_BLOCK, (g + 1) * ATT_BLOCK)
                o_ref[:, h * HEAD_DIM:(h + 1) * HEAD_DIM] = o[rs]
                lse_ref[:, h:h + 1] = lse[rs]

    out_specs = [pl.BlockSpec((ATT_BLOCK, dq), lambda n: (n, 0)), pl.BlockSpec((ATT_BLOCK, n_q), lambda n: (n, 0))]
    out_shape = [jax.ShapeDtypeStruct((rows, dq), F32), jax.ShapeDtypeStruct((rows, n_q), F32)]
    return pl.pallas_call(body, name=name, grid=(rows // ATT_BLOCK,), in_specs=in_specs, out_specs=out_specs,
                          out_shape=out_shape, compiler_params=_cp("parallel"))(sinks, qkv, qkv, qkv)


def _attn_bwd(qkv, dout, lse, sinks, *, name):
    rows = qkv.shape[0]
    n_q = sinks.shape[0]
    n_kv = n_q // Q_PER_KV
    scale = 1.0 / math.sqrt(HEAD_DIM)
    dq_w, dkv_w, in_specs = _attn_specs(rows, n_q)
    blk_q = pl.BlockSpec((ATT_BLOCK, dq_w), lambda n: (n, 0))
    blk_kv = pl.BlockSpec((ATT_BLOCK, dkv_w), lambda n: (n, 0))
    in_specs = in_specs + [blk_q, pl.BlockSpec((ATT_BLOCK, n_q), lambda n: (n, 0))]

    def body(sink_ref, q_ref, kvc_ref, kvp_ref, do_ref, lse_ref, dq_ref, dkc_ref, dkp_ref, dsink_ref):
        n = pl.program_id(0)

        @pl.when(n == 0)
        def _():
            dsink_ref[...] = jnp.zeros_like(dsink_ref)

        valid = _attn_mask(n)
        head_ids = lax.broadcasted_iota(jnp.int32, (1, n_q), 1)
        dsink = jnp.zeros((1, n_q), F32)
        for kvh in range(n_kv):
            kk, vv = _kv_pair(kvc_ref, kvp_ref, kvh, n_kv)
            qs = _stack_heads(q_ref, kvh, BF16)
            dos = _stack_heads(do_ref, kvh, BF16)
            lse = jnp.concatenate([lse_ref[:, kvh * Q_PER_KV + g:kvh * Q_PER_KV + g + 1] for g in range(Q_PER_KV)], axis=0)
            s = lax.dot_general(qs, kk, (((1,), (1,)), ((), ())), preferred_element_type=F32)
            s = jnp.where(valid, s * scale, -jnp.inf)
            p = jnp.exp(s - lse)
            dp = lax.dot_general(dos, vv, (((1,), (1,)), ((), ())), preferred_element_type=F32)
            delta = jnp.sum(p * dp, axis=1, keepdims=True)
            ds = (p * (dp - delta) * scale).astype(BF16)
            sink_term = jnp.exp(_per_row_sink(sink_ref, kvh) - lse) * delta
            dqs = jnp.dot(ds, kk, preferred_element_type=F32)
            for g in range(Q_PER_KV):
                h = kvh * Q_PER_KV + g
                rs = slice(g * ATT_BLOCK, (g + 1) * ATT_BLOCK)
                dsink = dsink + jnp.where(head_ids == h, -jnp.sum(sink_term[rs]), 0.0)
                dq_ref[:, h * HEAD_DIM:(h + 1) * HEAD_DIM] = dqs[rs]
            dk = lax.dot_general(ds, qs, (((0,), (0,)), ((), ())), preferred_element_type=F32)
            dv = lax.dot_general(p.astype(BF16), dos, (((0,), (0,)), ((), ())), preferred_element_type=F32)
            ks = slice(kvh * HEAD_DIM, (kvh + 1) * HEAD_DIM)
            vs = slice((n_kv + kvh) * HEAD_DIM, (n_kv + kvh + 1) * HEAD_DIM)
            dkp_ref[:, ks] = dk[0:ATT_BLOCK]
            dkc_ref[:, ks] = dk[ATT_BLOCK:]
            dkp_ref[:, vs] = dv[0:ATT_BLOCK]
            dkc_ref[:, vs] = dv[ATT_BLOCK:]
        dsink_ref[...] += dsink

    out_specs = [blk_q, blk_kv, blk_kv, pl.BlockSpec((1, n_q), lambda n: (0, 0))]
    out_shape = [jax.ShapeDtypeStruct((rows, dq_w), F32), jax.ShapeDtypeStruct((rows, dkv_w), F32),
                 jax.ShapeDtypeStruct((rows, dkv_w), F32), jax.ShapeDtypeStruct((1, n_q), F32)]
    return pl.pallas_call(body, name=name, grid=(rows // ATT_BLOCK,), in_specs=in_specs, out_specs=out_specs,
                          out_shape=out_shape, compiler_params=_cp("arbitrary"))(sinks, qkv, qkv, qkv, dout, lse)


def _dqkv_assemble(dq, dkc, dkp, *, name):
    rows, dq_w = dq.shape
    dkv_w = dkc.shape[1]
    nb = rows // ATT_BLOCK

    def body(dq_ref, dkc_ref, dkp_ref, o_ref, db_ref):
        n = pl.program_id(0)

        @pl.when(n == 0)
        def _():
            db_ref[...] = jnp.zeros_like(db_ref)

        dqv = dq_ref[...]
        dkv = dkc_ref[...] + jnp.where(n == nb - 1, 0.0, dkp_ref[...])
        o_ref[:, 0:dq_w] = dqv.astype(BF16)
        o_ref[:, dq_w:dq_w + dkv_w] = dkv.astype(BF16)
        db_ref[:, 0:dq_w] += _rowsum(dqv)
        db_ref[:, dq_w:dq_w + dkv_w] += _rowsum(dkv)

    width = dq_w + dkv_w
    in_specs = [pl.BlockSpec((ATT_BLOCK, dq_w), lambda n: (n, 0)), pl.BlockSpec((ATT_BLOCK, dkv_w), lambda n: (n, 0)),
                pl.BlockSpec((ATT_BLOCK, dkv_w), lambda n: (jnp.minimum(n + 1, nb - 1), 0))]
    out_specs = [pl.BlockSpec((ATT_BLOCK, width), lambda n: (n, 0)), pl.BlockSpec((1, width), lambda n: (0, 0))]
    out_shape = [jax.ShapeDtypeStruct((rows, width), BF16), jax.ShapeDtypeStruct((1, width), F32)]
    return pl.pallas_call(body, name=name, grid=(nb,), in_specs=in_specs, out_specs=out_specs, out_shape=out_shape,
                          compiler_params=_cp("arbitrary"))(dq, dkc, dkp)


def _row_tile(r, c):
    budget = 2 * 1024 * 1024 // (4 * c)
    for cand in (1024, 512, 256, 128, 64, 32, 16):
        if cand <= budget and r % cand == 0:
            return cand
    return r


def _octo_sum(own, recv, place, dest, lead, *, name):
    _, _, r, c = own.shape
    t = _row_tile(r, c)
    lead_idx, buf_shape = lead

    def body(place_ref, own_ref, *rest):
        o_ref = rest[7] if dest is None else rest[8]
        acc = own_ref[...].astype(F32)
        for k in range(7):
            acc = acc + rest[k][...].astype(F32)
        o_ref[...] = acc

    def peer(mask):
        return pl.BlockSpec((None, t, c), lambda i, pr: (pr[2] ^ mask, i, 0))

    if lead_idx is None:
        o_spec = pl.BlockSpec((None, t, c), lambda i, pr: (pr[1], i, 0))
    else:
        o_spec = pl.BlockSpec((None, None, t, c), lambda i, pr: (lead_idx, pr[1], i, 0))
    in_specs = [pl.BlockSpec((None, None, t, c), lambda i, pr: (pr[0], pr[1], i, 0))] + [peer(m) for m in range(1, 8)]
    args = [place, own] + [recv] * 7
    aliases = {}
    if dest is not None:
        in_specs.append(HBM)
        args.append(dest)
        aliases = {9: 0}
    grid_spec = pltpu.PrefetchScalarGridSpec(num_scalar_prefetch=1, grid=(r // t,), in_specs=in_specs, out_specs=o_spec)
    return pl.pallas_call(body, name=name, grid_spec=grid_spec, out_shape=jax.ShapeDtypeStruct(buf_shape, F32),
                          input_output_aliases=aliases, compiler_params=_cp("parallel"))(*args)


def _adamw_math(w, g, m, v):
    nm = ADAM_B1 * m + (1.0 - ADAM_B1) * g
    nv = ADAM_B2 * v + (1.0 - ADAM_B2) * (g * g)
    m_hat = nm / (1.0 - ADAM_B1 ** ADAM_STEP)
    v_hat = nv / (1.0 - ADAM_B2 ** ADAM_STEP)
    return -ADAM_LR * (m_hat / (jnp.sqrt(v_hat) + ADAM_EPS) + ADAM_WD * w), nm, nv


def _adamw(w, g, m, v, *, name):
    r, c = w.shape
    t = _row_tile(r, c)

    def body(w_ref, g_ref, m_ref, v_ref, d_ref, nm_ref, nv_ref, go_ref):
        gv = g_ref[...]
        d_ref[...], nm_ref[...], nv_ref[...] = _adamw_math(w_ref[...], gv, m_ref[...], v_ref[...])
        go_ref[...] = gv

    blk = pl.BlockSpec((t, c), lambda i: (i, 0))
    sds = jax.ShapeDtypeStruct((r, c), F32)
    return pl.pallas_call(body, name=name, grid=(r // t,), in_specs=[blk] * 4, out_specs=[blk] * 4,
                          out_shape=[sds] * 4, compiler_params=_cp("parallel"))(w, g, m, v)


HBM = pl.BlockSpec(memory_space=pl.ANY)


def _place():
    x, y, c = lax.axis_index("x"), lax.axis_index("y"), lax.axis_index("c")
    chips = [(1 - x, y), (x, 1 - y), (1 - x, 1 - y)]
    return x, y, c, 2 * x + y, (x, y, 1 - c), chips


def _rcopy(src, dst, ssem, rsem, dev):
    return pltpu.make_async_remote_copy(src_ref=src, dst_ref=dst, send_sem=ssem, recv_sem=rsem, device_id=dev,
                                        device_id_type=MESH)


HBM_ONLY = pl.BlockSpec(memory_space=pltpu.HBM)
SEM = pl.BlockSpec(memory_space=pltpu.SEMAPHORE)


def _peers():
    x, y, c = lax.axis_index("x"), lax.axis_index("y"), lax.axis_index("c")
    out = []
    for mask in range(1, 8):
        px = 1 - x if mask & 4 else x
        py = 1 - y if mask & 2 else y
        pc = 1 - c if mask & 1 else c
        out.append(((px, py, pc), 2 * px + py, pc, 4 * px + 2 * py + pc))
    return 4 * x + 2 * y + c, out


def _reduce_start(grads, lands, after, *, name, whole=False):
    nt = len(grads)

    def body(*refs):
        ssems, rsems = refs[2 * nt + 1:3 * nt + 1], refs[3 * nt + 1:4 * nt + 1]
        g_out, l_out, token = refs[4 * nt + 1:5 * nt + 1], refs[5 * nt + 1:6 * nt + 1], refs[6 * nt + 1]
        me, peers = _peers()
        for t in range(nt):
            for k, (dev, chip, core, _) in enumerate(peers):
                src = g_out[t] if whole else g_out[t].at[chip, core]
                _rcopy(src, l_out[t].at[me], ssems[t].at[k], rsems[t].at[k], dev).start()
        token[...] = jnp.zeros_like(token)

    sems = [pltpu.SemaphoreType.DMA((7,))] * (2 * nt)
    out_shape = (sems + [pltpu.HBM(g.shape, g.dtype) for g in grads] + [pltpu.HBM(l.shape, l.dtype) for l in lands]
                 + [jax.ShapeDtypeStruct((8, LANES), F32)])
    res = pl.pallas_call(
        body, name=name, in_specs=[HBM_ONLY] * (2 * nt + 1),
        out_specs=[SEM] * (2 * nt) + [HBM_ONLY] * (2 * nt) + [pl.BlockSpec(memory_space=pltpu.VMEM)], out_shape=out_shape,
        input_output_aliases={t: 2 * nt + t for t in range(2 * nt)},
        compiler_params=pltpu.CompilerParams(has_side_effects=DATAFLOW),
    )(*[pltpu.with_memory_space_constraint(a, pltpu.HBM) for a in list(grads) + list(lands) + [after]])
    return res[:nt], res[nt:2 * nt], res[2 * nt:3 * nt], res[3 * nt:4 * nt], res[4 * nt]


def _reduce_wait(grads, lands, ssems, rsems, after, *, name, whole=False):
    nt = len(grads)

    def body(*refs):
        ssem_refs, rsem_refs = refs[2 * nt:3 * nt], refs[3 * nt:4 * nt]
        g_out, l_out = refs[4 * nt + 1:5 * nt + 1], refs[5 * nt + 1:6 * nt + 1]
        me, peers = _peers()
        for t in range(nt):
            for k, (dev, chip, core, _) in enumerate(peers):
                src = g_out[t] if whole else g_out[t].at[chip, core]
                _rcopy(src, l_out[t].at[me], ssem_refs[t].at[k], rsem_refs[t].at[k], dev).wait_send()
        for t in range(nt):
            for k, (dev, _, _, idx) in enumerate(peers):
                slot = l_out[t].at[idx]
                _rcopy(slot, slot, ssem_refs[t].at[k], rsem_refs[t].at[k], dev).wait_recv()

    res = pl.pallas_call(
        body, name=name, in_specs=[HBM_ONLY] * (2 * nt) + [SEM] * (2 * nt) + [HBM_ONLY], out_specs=[HBM_ONLY] * (2 * nt),
        out_shape=[pltpu.HBM(a.shape, a.dtype) for a in list(grads) + list(lands)],
        input_output_aliases={t: t for t in range(2 * nt)},
        compiler_params=pltpu.CompilerParams(has_side_effects=DATAFLOW),
    )(*grads, *lands, *ssems, *rsems, pltpu.with_memory_space_constraint(after, pltpu.HBM))
    return list(res[:nt]), list(res[nt:])
DATAFLOW = pltpu.SideEffectType.DATAFLOW_SIDE_EFFECTING


def _gather_now(bufs, *, name):
    nt = len(bufs)

    def body(*refs):
        outs = refs[nt:2 * nt]
        ssem, rsem = refs[2 * nt:]
        x, y, c, q, sib, chips = _place()
        sends = []
        for t in range(nt):
            for j, (px, py) in enumerate(chips):
                mine = outs[t].at[q, c]
                cp = _rcopy(mine, mine, ssem.at[t, j], rsem.at[t, j], (px, py, c))
                cp.start()
                sends.append(cp)
        for t in range(nt):
            for j, (px, py) in enumerate(chips):
                landed = outs[t].at[2 * px + py, c]
                _rcopy(landed, landed, ssem.at[t, j], rsem.at[t, j], (px, py, c)).wait_recv()
                cp = _rcopy(landed, landed, ssem.at[t, 3 + j], rsem.at[t, 3 + j], sib)
                cp.start()
                sends.append(cp)
        for t in range(nt):
            for j, (px, py) in enumerate(chips):
                passed = outs[t].at[2 * px + py, 1 - c]
                _rcopy(passed, passed, ssem.at[t, 3 + j], rsem.at[t, 3 + j], sib).wait_recv()
        for cp in sends:
            cp.wait_send()

    out_shape = [jax.ShapeDtypeStruct(b.shape, b.dtype) for b in bufs]
    return pl.pallas_call(
        body, name=name, in_specs=[HBM] * nt, out_specs=[HBM] * nt, out_shape=out_shape,
        input_output_aliases={t: t for t in range(nt)},
        scratch_shapes=[pltpu.SemaphoreType.DMA((nt, 6)), pltpu.SemaphoreType.DMA((nt, 6))],
    )(*bufs)


def _gather_start(bufs, half, after, *, name):
    nt = len(bufs)

    def body(*refs):
        ssems, rsems, outs = refs[nt + 1:2 * nt + 1], refs[2 * nt + 1:3 * nt + 1], refs[3 * nt + 1:4 * nt + 1]
        x, y, c, q, sib, chips = _place()
        for t in range(nt):
            for j, (px, py) in enumerate(chips):
                mine = outs[t].at[q, c] if half[t] else outs[t].at[q]
                _rcopy(mine, mine, ssems[t].at[j], rsems[t].at[j], (px, py, c)).start()

    sems = [pltpu.SemaphoreType.DMA((3,))] * (2 * nt)
    out_shape = sems + [pltpu.HBM(b.shape, b.dtype) for b in bufs]
    res = pl.pallas_call(
        body, name=name, in_specs=[HBM_ONLY] * (nt + 1), out_specs=[SEM] * (2 * nt) + [HBM_ONLY] * nt, out_shape=out_shape,
        input_output_aliases={t: 2 * nt + t for t in range(nt)},
        compiler_params=pltpu.CompilerParams(has_side_effects=DATAFLOW),
    )(*[pltpu.with_memory_space_constraint(b, pltpu.HBM) for b in list(bufs) + [after]])
    return res[:nt], res[nt:2 * nt], res[2 * nt:]


def _gather_wait(bufs, half, ssems, rsems, after, *, name):
    nt = len(bufs)

    def body(*refs):
        ssem_refs, rsem_refs = refs[nt:2 * nt], refs[2 * nt:3 * nt]
        outs = refs[3 * nt + 1:]
        x, y, c, q, sib, chips = _place()
        for t in range(nt):
            for j, (px, py) in enumerate(chips):
                mine = outs[t].at[q, c] if half[t] else outs[t].at[q]
                _rcopy(mine, mine, ssem_refs[t].at[j], rsem_refs[t].at[j], (px, py, c)).wait_send()
        for t in range(nt):
            for j, (px, py) in enumerate(chips):
                theirs = outs[t].at[2 * px + py, c] if half[t] else outs[t].at[2 * px + py]
                _rcopy(theirs, theirs, ssem_refs[t].at[j], rsem_refs[t].at[j], (px, py, c)).wait_recv()

    res = pl.pallas_call(
        body, name=name, in_specs=[HBM_ONLY] * nt + [SEM] * (2 * nt) + [HBM], out_specs=[HBM_ONLY] * nt,
        out_shape=[pltpu.HBM(b.shape, b.dtype) for b in bufs], input_output_aliases={t: t for t in range(nt)},
        compiler_params=pltpu.CompilerParams(has_side_effects=DATAFLOW),
    )(*bufs, *ssems, *rsems, after)
    return list(res)


def _sibling_swap(bufs, *, name):
    nt = len(bufs)

    def body(*refs):
        outs = refs[nt:2 * nt]
        ssem, rsem = refs[2 * nt:]
        x, y, c, q, sib, chips = _place()
        sends = []
        for t in range(nt):
            for j, (px, py) in enumerate(chips):
                held = outs[t].at[2 * px + py, c]
                cp = _rcopy(held, held, ssem.at[t, j], rsem.at[t, j], sib)
                cp.start()
                sends.append(cp)
        for t in range(nt):
            for j, (px, py) in enumerate(chips):
                other = outs[t].at[2 * px + py, 1 - c]
                _rcopy(other, other, ssem.at[t, j], rsem.at[t, j], sib).wait_recv()
        for cp in sends:
            cp.wait_send()

    return pl.pallas_call(
        body, name=name, in_specs=[HBM] * nt, out_specs=[HBM] * nt,
        out_shape=[jax.ShapeDtypeStruct(b.shape, b.dtype) for b in bufs], input_output_aliases={t: t for t in range(nt)},
        scratch_shapes=[pltpu.SemaphoreType.DMA((nt, 3)), pltpu.SemaphoreType.DMA((nt, 3))],
    )(*bufs)


def _sibling_share(bufs, layout, *, name):
    no = len(bufs)
    nt = len(layout)

    def body(*refs):
        outs = refs[no:2 * no]
        ssem, rsem = refs[2 * no:]
        x, y, c, q, sib, chips = _place()

        def slot(t, half):
            o, lead = layout[t]
            return outs[o].at[half] if lead is None else outs[o].at[lead, half]

        sends = []
        for t in range(nt):
            cp = _rcopy(slot(t, c), slot(t, c), ssem.at[t], rsem.at[t], sib)
            cp.start()
            sends.append(cp)
        for t in range(nt):
            _rcopy(slot(t, 1 - c), slot(t, 1 - c), ssem.at[t], rsem.at[t], sib).wait_recv()
        for cp in sends:
            cp.wait_send()

    out_shape = [jax.ShapeDtypeStruct(b.shape, b.dtype) for b in bufs]
    return pl.pallas_call(
        body, name=name, in_specs=[HBM] * no, out_specs=[HBM] * no, out_shape=out_shape,
        input_output_aliases={o: o for o in range(no)},
        scratch_shapes=[pltpu.SemaphoreType.DMA((nt,)), pltpu.SemaphoreType.DMA((nt,))],
    )(*bufs)


SMALL_GROUP_OF = (0,) * 6 + (1,) * 2 + (2,) * 2 + (3,) * 4 + (4,) * 9


def _small_pack(local, dims, which, *, name):
    kw, wa, ng, nqkv, nsk, f, dm = dims
    shapes = _small_shapes(dims)
    row_vec = 8 * (-(-kw // 8))
    pos = sorted(local)
    assert all(SMALL_GROUP_OF[p] in which for p in pos)

    def pack_body(*refs):
        loc = dict(zip(pos, refs[:len(pos)]))
        grp = dict(zip(which, refs[len(pos):]))
        for gr in grp.values():
            gr[...] = jnp.zeros_like(gr)
        if 0 in which:
            grp[0][0:kw, :] = loc[0][...]
            for k in range(5):
                grp[0][row_vec + k:row_vec + k + 1, :] = loc[1 + k][...]
        if 1 in which:
            for g in range(ng):
                grp[1][g * B_CHUNK:(g + 1) * B_CHUNK, :] = loc[6][g]
            grp[1][ng * B_CHUNK:ng * B_CHUNK + ng, :] = loc[7][...]
        if 2 in which:
            grp[2][0:1, :] = loc[8][...]
            grp[2][1:2, 0:nsk] = loc[9][...]
        if 3 in which:
            for l in range(2):
                for s in range(2):
                    grp[3][l, s, 0:3, :] = loc[10 + 2 * l][s]
                    grp[3][l, s, 3:4, :] = loc[11 + 2 * l][s]
        if 4 in which:
            for k in range(9):
                grp[4][k:k + 1, :] = loc[14 + k][...]

    vm = pl.BlockSpec(memory_space=pltpu.VMEM)
    return pl.pallas_call(
        pack_body, name=name, in_specs=[vm] * len(pos), out_specs=[vm] * len(which),
        out_shape=[jax.ShapeDtypeStruct(shapes[g], F32) for g in which],
        compiler_params=pltpu.CompilerParams(vmem_limit_bytes=VMEM_LIMIT),
    )(*[local[p] for p in pos])


def _small_shapes(dims):
    kw, wa, ng, nqkv, nsk, f, dm = dims
    return [(8 * (-(-kw // 8)) + 8, wa), (ng * B_CHUNK + 8, B_CHUNK), (8, nqkv), (2, 2, 8, f), (16, dm)]


def _small_update(groups, landed, params, dims, *, name):
    kw, wa, ng, nqkv, nsk, f, dm = dims
    shapes = _small_shapes(dims)
    row_vec = 8 * (-(-kw // 8))
    n_grp = len(shapes)
    flat_params = [a for triple in params for a in triple]
    n_par = len(params)
    vm = pl.BlockSpec(memory_space=pltpu.VMEM)

    def adamw_body(*refs):
        own = refs[:n_grp]
        land = refs[n_grp:2 * n_grp]
        par = refs[2 * n_grp:2 * n_grp + 3 * n_par]
        outs = refs[2 * n_grp + 3 * n_par:2 * n_grp + 7 * n_par]
        loss_ref = refs[2 * n_grp + 7 * n_par]
        tot = refs[2 * n_grp + 7 * n_par + 1:]
        x, y = lax.axis_index("x"), lax.axis_index("y")
        q = 2 * x + y
        me = 4 * x + 2 * y + lax.axis_index("c")
        for gi in range(n_grp):
            acc = None
            for dv in range(8):
                term = jnp.where(me == dv, own[gi][...], land[gi][dv])
                acc = term if acc is None else acc + term
            tot[gi][...] = acc
        ta, tb, tc, td, te = tot

        def mine(piece):
            out = piece(0)
            for k in range(1, 4):
                out = jnp.where(q == k, piece(k), out)
            return out

        def update(p, grad, index=None):
            at = (lambda r: r[...]) if index is None else (lambda r: r[index])
            w_ref, m_ref, v_ref = par[3 * p:3 * p + 3]
            g_ref, d_ref, nm_ref, nv_ref = outs[4 * p:4 * p + 4]
            delta, nm, nv = _adamw_math(at(w_ref), grad, at(m_ref), at(v_ref))
            for r, val in ((g_ref, grad), (d_ref, delta), (nm_ref, nm), (nv_ref, nv)):
                if index is None:
                    r[...] = val
                else:
                    r[index] = val

        wq = wa // 4
        update(0, mine(lambda k: ta[0:kw, k * wq:(k + 1) * wq]), (0,))
        for k in range(5):
            update(1 + k, ta[row_vec + k:row_vec + k + 1, :])
        for g in range(ng):
            update(6, tb[g * B_CHUNK:(g + 1) * B_CHUNK, :], (0, g))
        update(7, tb[ng * B_CHUNK:ng * B_CHUNK + ng, :], (0,))
        nq4 = nqkv // 4
        update(8, mine(lambda k: tc[0:1, k * nq4:(k + 1) * nq4]))
        update(9, tc[1:2, 0:nsk])
        fh = f // 2
        for l in range(2):
            update(10, mine(lambda k: td[l, k // 2, 0:3, (k % 2) * fh:(k % 2 + 1) * fh]), (l,))
            update(11, jnp.concatenate([td[l, 0, 3:4, :], td[l, 1, 3:4, :]], axis=1), (slice(l, l + 1),))
        dq4 = dm // 4
        for i in range(2):
            for j in range(2):
                for p, base in ((12, 0), (13, 4)):
                    row = base + 2 * i + j
                    update(p, mine(lambda k: te[row:row + 1, k * dq4:(k + 1) * dq4]), (i, slice(j, j + 1)))
        loss_ref[...] = (0.5 / dm) * jnp.sum(te[8:9, :], axis=1, keepdims=True)

    out_shape = []
    for w, _, _ in params:
        out_shape += [jax.ShapeDtypeStruct(w.shape, F32)] * 4
    out_shape.append(jax.ShapeDtypeStruct((1, 1), F32))
    res = pl.pallas_call(
        adamw_body, name=name + "_adamw", in_specs=[vm] * (2 * n_grp + 3 * n_par), out_specs=[vm] * len(out_shape),
        out_shape=out_shape, scratch_shapes=[pltpu.VMEM(s, F32) for s in shapes],
        compiler_params=pltpu.CompilerParams(vmem_limit_bytes=VMEM_LIMIT),
    )(*groups, *landed, *flat_params)
    return [res[4 * p:4 * p + 4] for p in range(n_par)], res[-1]


def _pack(arrays, rows_multiple):
    flat = jnp.concatenate([a.reshape(-1) for a in arrays])
    rows = -(-flat.shape[0] // LANES)
    rows = -(-rows // rows_multiple) * rows_multiple
    flat = jnp.pad(flat, (0, rows * LANES - flat.shape[0]))
    return flat.reshape(rows, LANES)


def _unshard_cols(stacked):
    moved = jnp.moveaxis(stacked, 0, -2)
    return moved.reshape(moved.shape[:-2] + (4 * stacked.shape[-1],))


def kernel(x, ab_w_in, a_conv_w, a_conv_b, a_norm_g, a_norm_b, b_norm_g, b_norm_b, b_spatial_w, b_spatial_b, ab_w_out, c_w_qkv, c_b_qkv, c_sinks, c_w_o, ffn_w_up, ffn_conv_w, ffn_conv_b, ffn_w_down, ln_g, ln_b, loss_target, m_ab_w_in, m_a_conv_w, m_a_conv_b, m_a_norm_g, m_a_norm_b, m_b_norm_g, m_b_norm_b, m_b_spatial_w, m_b_spatial_b, m_ab_w_out, m_c_w_qkv, m_c_b_qkv, m_c_sinks, m_c_w_o, m_ffn_w_up, m_ffn_conv_w, m_ffn_conv_b, m_ffn_w_down, m_ln_g, m_ln_b, v_ab_w_in, v_a_conv_w, v_a_conv_b, v_a_norm_g, v_a_norm_b, v_b_norm_g, v_b_norm_b, v_b_spatial_w, v_b_spatial_b, v_ab_w_out, v_c_w_qkv, v_c_b_qkv, v_c_sinks, v_c_w_o, v_ffn_w_up, v_ffn_conv_w, v_ffn_conv_b, v_ffn_w_down, v_ln_g, v_ln_b):
    rows, d = x.shape[1], x.shape[2]
    depth = ln_g.shape[0]
    assert depth == 2 and x.shape[0] == 1
    alpha = (2.0 * depth) ** 0.25
    f = ffn_w_down.shape[1] * 4
    n_q = c_sinks.shape[1]
    q_idx = 2 * lax.axis_index("x") + lax.axis_index("y")
    c_idx = lax.axis_index("c")
    xs, tgt = x[0], loss_target[0]

    def own_slot(part):
        buf = lax.empty((4,) + part.shape, part.dtype)
        return lax.dynamic_update_slice(buf, part[None], (q_idx, 0, 0, 0))

    def halves(wm):
        return own_slot(wm.astype(BF16).reshape((2, wm.shape[0] // 2) + wm.shape[1:]))

    small_sharded = [a_conv_w[0], c_b_qkv[0], ffn_conv_w, ln_g, ln_b]
    small_pack = _pack(small_sharded, 16)
    bufs = [halves(ab_w_in[0]), own_slot(small_pack.reshape(2, small_pack.shape[0] // 2, LANES)), halves(ab_w_out[0]),
            halves(ffn_w_up[0]), halves(ffn_w_down[0]), halves(c_w_qkv[0]), halves(c_w_o[0]),
            halves(ffn_w_up[1]), halves(ffn_w_down[1])]
    whole = lambda g: g.reshape(4, 2 * g.shape[2], g.shape[3])
    n_now = 2
    first_two = _gather_now(bufs[:n_now], name="gather_now")
    w_in, small_all = [whole(g) for g in first_two]
    later = bufs[n_now:]
    half = [True, True] + [False] * (len(later) - 2)
    ssems, rsems, started = _gather_start(later, half, first_two[1], name="gather_start")

    def arrive(idx, after, tag):
        idx = [i - n_now for i in idx]
        halved = [half[i] for i in idx]
        got = _gather_wait([started[i] for i in idx], halved, [ssems[i] for i in idx], [rsems[i] for i in idx], after,
                           name=f"gather_wait_{tag}")
        if all(halved):
            got = _sibling_swap(got, name=f"gather_swap_{tag}")
        return [whole(g) for g in got]

    small_all = small_all.reshape(4, -1)
    sh_shapes = [s.shape for s in small_sharded]
    pieces, pos = [], 0
    for s in sh_shapes:
        n = math.prod(s)
        pieces.append(_unshard_cols(small_all[:, pos:pos + n].reshape((4,) + s)))
        pos += n
    conv_w_a, b_qkv, conv_w_f, ln_gf, ln_bf = pieces

    tril = jnp.tril(jnp.ones((B_CHUNK, B_CHUNK), F32))
    ws = (b_spatial_w[0] * tril).astype(BF16)
    wst = jnp.swapaxes(ws, 1, 2)
    sbb = jnp.broadcast_to(b_spatial_b[0][:, :, None], b_spatial_w[0].shape)
    mix_vecs = [a_conv_b, a_norm_g, a_norm_b, b_norm_g, b_norm_b]
    cw_f = [jnp.swapaxes(conv_w_f[l].reshape(3, 2, f), 0, 1) for l in range(depth)]
    cb_f = [ffn_conv_b[l].reshape(2, 1, f) for l in range(depth)]
    lng = lambda i, j: ln_gf[i, j].reshape(1, d)
    lnb = lambda i, j: ln_bf[i, j].reshape(1, d)
    sinks = c_sinks[0]

    w_up, w_down = [None, None], [None, None]

    def ffn_fwd(xin, l):
        w_up[l], = arrive([3 + 4 * l], xin, f"up{l}")
        hf, fact = _ffn_up_fwd(xin, w_up[l], cw_f[l], cb_f[l], name=f"ffn{l}_up")
        w_down[l] = arrive([4 + 4 * l], fact, f"down{l}")[0].reshape(-1, d)
        out = _matmul(fact, w_down[l], name=f"ffn{l}_down", tm=512, tn=1024, tk=2816)
        return hf, fact, out

    h0, ab, a2 = _mixer_fwd(xs, w_in, conv_w_a, *mix_vecs, ws, sbb, name="mix_fwd")
    w_out = arrive([2], ab, "out")[0].reshape(-1, d)
    mix = _matmul(ab, w_out, name="mix_out", tm=1024, tn=1024, tk=1024)
    x1 = _add_ln_fwd(xs, mix, lng(0, 0), lnb(0, 0), alpha, name="ln00")
    hf0, f0, ffn0 = ffn_fwd(x1, 0)
    x2 = _add_ln_fwd(x1, ffn0, lng(0, 1), lnb(0, 1), alpha, name="ln01")
    w_qkv = _unshard_cols(arrive([5], x2, "qkv")[0])
    qkv = _matmul(x2, w_qkv, name="att_qkv", tm=1024, tn=w_qkv.shape[1], tk=1024, bias=b_qkv.reshape(1, -1))
    ao, lse = _attn_fwd(qkv, sinks, name="att_core")
    w_o = arrive([6], ao, "o")[0].reshape(-1, d)
    att = _matmul(ao, w_o, name="att_out", tm=1024, tn=1024, tk=1024)
    x3 = _add_ln_fwd(x2, att, lng(1, 0), lnb(1, 0), alpha, name="ln10")
    hf1, f1, ffn1 = ffn_fwd(x3, 1)
    sq_err, dy = _add_ln_loss(x3, ffn1, lng(1, 1), lnb(1, 1), tgt, alpha, name="ln11_loss")

    def owner_view(g):
        if g.ndim == 3:
            return g.reshape(4, 2, g.shape[1] // 2, g.shape[2])
        return g.reshape(4, 2, g.shape[0] // 8, g.shape[1])

    in_flight = []

    def send_grads(tag, grads, after):
        lands = [lax.empty((8,) + g.shape[2:], BF16) for g in grads]
        ss, rs, g_thru, l_thru, token = _reduce_start(grads, lands, after, name=f"reduce_start_{tag}")
        in_flight.append((tag, g_thru, l_thru, ss, rs))
        return token[0:1, 0:1]

    def ffn_bwd(dz, xin, hf, fact, l):
        d_wdown = _matmul(fact, dz, name=f"ffn{l}_down_dw", ta=True, tm=1408, tn=1024, tk=2048, out_dtype=BF16)
        dfa = _matmul(dz, w_down[l], name=f"ffn{l}_down_dx", tb=True, tm=1024, tn=1408, tk=1024, out_dtype=BF16)
        dx_parts, d_wup, dcw, dcb = _ffn_up_bwd(hf, dfa, xin, w_up[l], cw_f[l], cb_f[l], name=f"ffn{l}_up_bwd")
        tok = send_grads(f"ffn{l}", [owner_view(d_wup), owner_view(d_wdown)], dcb)
        return [(dx_parts, 1.0), (dz, alpha)], dcw, dcb, tok

    dz, dg11, db11 = _add_ln_bwd([(dy, 1.0)], x3, ffn1, lng(1, 1), alpha, name="ln11_bwd")
    dx3, dcw1, dcb1, tok = ffn_bwd(dz, x3, hf1, f1, 1)
    dz, dg10, db10 = _add_ln_bwd(dx3, x2, att, lng(1, 0) + tok, alpha, name="ln10_bwd")
    d_wo = _matmul(ao, dz, name="att_out_dw", ta=True, tm=1024, tn=1024, tk=1024, out_dtype=BF16)
    dao = _matmul(dz, w_o, name="att_out_dx", tb=True, tm=1024, tn=1024, tk=1024)
    dq, dkc, dkp, d_sinks = _attn_bwd(qkv, dao, lse, sinks, name="att_core_bwd")
    dqkv, d_bqkv = _dqkv_assemble(dq, dkc, dkp, name="att_dqkv")
    d_wqkv = _matmul(x2, dqkv, name="att_qkv_dw", ta=True, tm=1024, tn=dqkv.shape[1], tk=1024, out_dtype=BF16)
    d_wqkv_st = jnp.moveaxis(d_wqkv.reshape(d_wqkv.shape[0], 4, -1), 1, 0)
    tok = send_grads("att", [owner_view(d_wqkv_st), owner_view(d_wo)], d_bqkv)
    dx2 = _matmul(dqkv, w_qkv, name="att_qkv_dx", tb=True, tm=1024, tn=1024, tk=dqkv.shape[1], addend=(dz, alpha))
    dz, dg01, db01 = _add_ln_bwd([(dx2, 1.0)], x1, ffn0, lng(0, 1) + tok, alpha, name="ln01_bwd")
    dx1, dcw0, dcb0, tok = ffn_bwd(dz, x1, hf0, f0, 0)
    dz, dg00, db00 = _add_ln_bwd(dx1, xs, mix, lng(0, 0) + tok, alpha, name="ln00_bwd")
    d_wout = _matmul(ab, dz, name="mix_out_dw", ta=True, tm=1024, tn=1024, tk=1024, out_dtype=BF16)
    dab = _matmul(dz, w_out, name="mix_out_dx", tb=True, tm=1024, tn=1024, tk=1024)
    small_dims = (a_conv_w.shape[1], a_conv_b.shape[1], b_spatial_w.shape[1], 4 * c_b_qkv.shape[1], n_q, f, d)

    def send_small(local_arrays, which, tag, after):
        groups = _small_pack(local_arrays, small_dims, which, name=f"small_pack_{tag}")
        lands = [lax.empty((8,) + g.shape, F32) for g in groups]
        return _reduce_start(groups, lands, after, name=f"small_start_{tag}", whole=True)

    ready = [d_bqkv, d_sinks, dcw0, dcb0, dcw1, dcb1, dg00, dg01, dg10, dg11, db00, db01, db10, db11, sq_err]
    ss_e, rs_e, g_e, l_e, token = send_small(dict(zip(range(8, 23), ready)), (2, 3, 4), "early", dab)
    tok = send_grads("out", [owner_view(d_wout)], token)
    grad_x, d_win, d_cwa, d_cba, d_ga, d_ba, d_gb, d_bb, d_ws, d_sb = _mixer_bwd(
        h0, a2, dab, xs, w_in, dz, alpha, conv_w_a, *mix_vecs[1:], ws, wst, sbb, tril + tok, name="mix_bwd")

    small_w = [a_conv_w, a_conv_b, a_norm_g, a_norm_b, b_norm_g, b_norm_b, b_spatial_w, b_spatial_b, c_b_qkv,
               c_sinks, ffn_conv_w, ffn_conv_b, ln_g, ln_b]
    small_m = [m_a_conv_w, m_a_conv_b, m_a_norm_g, m_a_norm_b, m_b_norm_g, m_b_norm_b, m_b_spatial_w, m_b_spatial_b,
               m_c_b_qkv, m_c_sinks, m_ffn_conv_w, m_ffn_conv_b, m_ln_g, m_ln_b]
    small_v = [v_a_conv_w, v_a_conv_b, v_a_norm_g, v_a_norm_b, v_b_norm_g, v_b_norm_b, v_b_spatial_w, v_b_spatial_b,
               v_c_b_qkv, v_c_sinks, v_ffn_conv_w, v_ffn_conv_b, v_ln_g, v_ln_b]
    place = jnp.stack([q_idx, c_idx, 4 * lax.axis_index("x") + 2 * lax.axis_index("y") + c_idx]).astype(jnp.int32)
    where = {"mix": [(0, None)], "out": [(1, None)], "att": [(2, None), (3, None)], "ffn0": [(4, 0), (5, 0)],
             "ffn1": [(4, 1), (5, 1)]}
    big_w = [ab_w_in, ab_w_out, c_w_qkv, c_w_o, ffn_w_up, ffn_w_down]
    big_m = [m_ab_w_in, m_ab_w_out, m_c_w_qkv, m_c_w_o, m_ffn_w_up, m_ffn_w_down]
    big_v = [v_ab_w_in, v_ab_w_out, v_c_w_qkv, v_c_w_o, v_ffn_w_up, v_ffn_w_down]
    big_out = [None] * 6

    def finish(tags, after, label):
        bufs, layout = {}, []
        for tag, g_thru, l_thru, ss, rs in in_flight:
            if tag not in tags:
                continue
            own, landed = _reduce_wait(g_thru, l_thru, ss, rs, after, name=f"reduce_wait_{tag}")
            for k, (o, lead) in enumerate(where[tag]):
                piece = own[k].shape[2:]
                shape = (2,) + piece if lead is None else (2, 2) + piece
                bufs[o] = _octo_sum(own[k], landed[k], place, bufs.get(o), (lead, shape), name=f"reduce_sum_{tag}{k}")
                layout.append((o, lead))
        order = sorted(bufs)
        shared = _sibling_share([bufs[o] for o in order], [(order.index(o), lead) for o, lead in layout],
                                name=f"reduce_share_{label}")
        for o, g in zip(order, shared):
            w = big_w[o]
            two_d = lambda a: a.reshape(-1, a.shape[-1])
            outs = _adamw(two_d(w), two_d(g), two_d(big_m[o]), two_d(big_v[o]), name=f"adamw_big{o}")
            big_out[o] = [r.reshape(w.shape) for r in outs]
        return big_out[order[-1]][0]

    mixer_small = [d_cwa, d_cba, d_ga, d_ba, d_gb, d_bb, d_ws, d_sb]
    ss_l, rs_l, g_l, l_l, token = send_small(dict(zip(range(8), mixer_small)), (0, 1), "late", grad_x)
    tok = send_grads("mix", [owner_view(d_win)], after=token)
    done = finish(("ffn1", "att", "ffn0", "out"), d_ws + tok, "early")
    g_l, l_l = _reduce_wait(g_l, l_l, ss_l, rs_l, done, name="small_wait_late", whole=True)
    g_e, l_e = _reduce_wait(g_e, l_e, ss_e, rs_e, done, name="small_wait_early", whole=True)
    small_out, loss = _small_update(g_l + g_e, l_l + l_e, list(zip(small_w, small_m, small_v)), small_dims,
                                    name="small_tail")
    loss = loss[0, 0]
    small_g = [o[0] for o in small_out]
    sm_delta = [o[1] for o in small_out]
    sm_m = [o[2] for o in small_out]
    sm_v = [o[3] for o in small_out]
    finish(("mix",), sm_delta[6], "mix")

    order_big = {0: 0, 9: 1, 10: 2, 13: 3, 14: 4, 17: 5}
    order_small = {1: 0, 2: 1, 3: 2, 4: 3, 5: 4, 6: 5, 7: 6, 8: 7, 11: 8, 12: 9, 15: 10, 16: 11, 18: 12, 19: 13}
    grads, deltas, new_m, new_v = [], [], [], []
    for pos_w in range(20):
        if pos_w in order_big:
            t = order_big[pos_w]
            grads.append(big_out[t][3])
            deltas.append(big_out[t][0])
            new_m.append(big_out[t][1])
            new_v.append(big_out[t][2])
        else:
            t = order_small[pos_w]
            grads.append(small_g[t])
            deltas.append(sm_delta[t])
            new_m.append(sm_m[t])
            new_v.append(sm_v[t])
    return (loss, grad_x[None], *grads, *deltas, *new_m, *new_v)
```

```python
import math

import jax
import jax.numpy as jnp
from jax import lax
from jax.experimental import pallas as pl
from jax.experimental.pallas import tpu as pltpu

F32 = jnp.float32
BF16 = jnp.bfloat16
MESH = pl.DeviceIdType.MESH

LN_EPS = 1e-5
HEAD_DIM = 64
ATT_BLOCK = 128
Q_PER_KV = 8
A_KERNEL = 31
CONV_HALO = 32
FFN_HALO = 8
BF16_ROWS = 16
B_CHUNK = 128
LANES = 128
MXU_WIDTH = 256
GELU_C = math.sqrt(2.0 / math.pi)
ADAM_LR = 0.001
ADAM_B1 = 0.9
ADAM_B2 = 0.999
ADAM_EPS = 1e-08
ADAM_WD = 0.01
ADAM_STEP = 10
VMEM_LIMIT = 56 * 1024 * 1024


def _cp(*dims):
    return pltpu.CompilerParams(dimension_semantics=dims, vmem_limit_bytes=VMEM_LIMIT)


def _pick(n, prefs):
    for p in prefs:
        if n % p == 0:
            return p
    return n


def _sig(x):
    return 1.0 / (1.0 + jnp.exp(-x))


def _gelu(x):
    t = jnp.tanh(GELU_C * (x + 0.044715 * (x * x * x)))
    return x * (0.5 * (1.0 + t)), t


def _gelu_grad(x, t):
    return 0.5 * (1.0 + t) + 0.5 * x * (1.0 - t * t) * (GELU_C * (1.0 + 3.0 * 0.044715 * x * x))


def _ln_stats(z):
    mu = jnp.mean(z, axis=-1, keepdims=True)
    zc = z - mu
    var = jnp.mean(zc * zc, axis=-1, keepdims=True)
    rstd = lax.rsqrt(var + LN_EPS)
    return zc * rstd, rstd


def _ln_bwd(dxh, xh, rstd):
    return rstd * (dxh - jnp.mean(dxh, axis=-1, keepdims=True) - xh * jnp.mean(dxh * xh, axis=-1, keepdims=True))


def _rowsum(a):
    return jnp.sum(a, axis=0, keepdims=True)


def _lshape(a):
    return (a.shape[0], a.shape[1]) if a.ndim == 2 else (a.shape[1], a.shape[0] * a.shape[2])


def _spec2(arr, blk_r, blk_c, ridx, cidx):
    if len(arr.shape) == 2:
        return pl.BlockSpec((blk_r, blk_c), lambda i, j, k: (ridx(i, j, k), cidx(i, j, k)))
    per = arr.shape[2] // blk_c
    assert arr.shape[2] % blk_c == 0
    return pl.BlockSpec((None, blk_r, blk_c), lambda i, j, k: (cidx(i, j, k) // per, ridx(i, j, k), cidx(i, j, k) % per))


def _matmul(a, b, *, name, ta=False, tb=False, tm, tn, tk, out_dtype=F32, out_stack=None, bias=None, addend=None):
    ar, ac = _lshape(a)
    br, bc = _lshape(b)
    m, kdim = (ac, ar) if ta else (ar, ac)
    n = br if tb else bc
    assert (bc if tb else br) == kdim
    tm, tn, tk = min(tm, m), min(tn, n), min(tk, kdim)
    assert m % tm == 0 and n % tn == 0 and kdim % tk == 0, (name, m, n, kdim, tm, tn, tk)
    nk = kdim // tk
    gi, gj, gk = (lambda i, j, k: i), (lambda i, j, k: j), (lambda i, j, k: k)
    a_spec = _spec2(a, tk, tm, gk, gi) if ta else _spec2(a, tm, tk, gi, gk)
    b_spec = _spec2(b, tn, tk, gj, gk) if tb else _spec2(b, tk, tn, gk, gj)
    if out_stack is None:
        out_sds = jax.ShapeDtypeStruct((m, n), out_dtype)
    else:
        out_sds = jax.ShapeDtypeStruct((out_stack, m, n // out_stack), out_dtype)
    o_spec = _spec2(out_sds, tm, tn, gi, gj)
    in_specs = [a_spec, b_spec]
    args = [a, b]
    if bias is not None:
        in_specs.append(pl.BlockSpec((1, tn), lambda i, j, k: (0, j)))
        args.append(bias)
    scale = None
    if addend is not None:
        add_arr, scale = addend
        in_specs.append(pl.BlockSpec((tm, tn), lambda i, j, k: (i, j)))
        args.append(add_arr)
    use_acc = nk > 1 and out_dtype != F32
    dn = (((0 if ta else 1,), (1 if tb else 0,)), ((), ()))

    def body(*refs):
        a_ref, b_ref = refs[0], refs[1]
        pos = 2
        bias_ref = add_ref = None
        if bias is not None:
            bias_ref = refs[pos]
            pos += 1
        if addend is not None:
            add_ref = refs[pos]
            pos += 1
        o_ref = refs[pos]
        acc_ref = refs[pos + 1] if use_acc else o_ref
        p = lax.dot_general(a_ref[...].astype(BF16), b_ref[...].astype(BF16), dn, preferred_element_type=F32)

        def finish(val):
            if bias_ref is not None:
                val = val + bias_ref[...]
            if add_ref is not None:
                val = val + scale * add_ref[...]
            return val.astype(out_dtype)

        if nk == 1:
            o_ref[...] = finish(p)
        else:
            k = pl.program_id(2)

            @pl.when(k == 0)
            def _():
                acc_ref[...] = p

            @pl.when(k > 0)
            def _():
                acc_ref[...] += p

            if use_acc or bias_ref is not None or add_ref is not None:
                @pl.when(k == nk - 1)
                def _():
                    o_ref[...] = finish(acc_ref[...])

    return pl.pallas_call(
        body, name=name, grid=(m // tm, n // tn, nk), in_specs=in_specs, out_specs=o_spec, out_shape=out_sds,
        scratch_shapes=[pltpu.VMEM((tm, tn), F32)] if use_acc else [],
        compiler_params=_cp("parallel", "parallel", "arbitrary"),
    )(*args)


def _proj_add_ln(a, w, x, g, b, alpha, *, name):
    rows, kdim = a.shape
    d = w.shape[1]
    tm = _pick(rows, (512, 256))

    def body(a_ref, w_ref, x_ref, g_ref, b_ref, s_ref, y_ref):
        s = jnp.dot(a_ref[...].astype(BF16), w_ref[...], preferred_element_type=F32)
        s_ref[...] = s
        xh, _ = _ln_stats(alpha * x_ref[...] + s)
        y_ref[...] = xh * g_ref[...] + b_ref[...]

    row = pl.BlockSpec((tm, d), lambda i: (i, 0))
    vec = pl.BlockSpec((1, d), lambda i: (0, 0))
    sds = jax.ShapeDtypeStruct((rows, d), F32)
    return pl.pallas_call(body, name=name, grid=(rows // tm,),
                          in_specs=[pl.BlockSpec((tm, kdim), lambda i: (i, 0)), pl.BlockSpec((kdim, d), lambda i: (0, 0)),
                                    row, vec, vec],
                          out_specs=[row, row], out_shape=[sds, sds], compiler_params=_cp("parallel"))(a, w, x, g, b)


def _add_ln_bwd(dy_terms, x, s, g, alpha, *, name):
    rows, d = x.shape
    t = _pick(rows, (512, 256))
    nterm = len(dy_terms)
    scales = [sc for _, sc in dy_terms]
    ranks = [a.ndim for a, _ in dy_terms]

    def body(*refs):
        dy_refs = refs[:nterm]
        x_ref, s_ref, g_ref, dz_ref, dg_ref, db_ref = refs[nterm:]

        @pl.when(pl.program_id(0) == 0)
        def _():
            dg_ref[...] = jnp.zeros_like(dg_ref)
            db_ref[...] = jnp.zeros_like(db_ref)

        dyv = None
        for r, sc, rank in zip(dy_refs, scales, ranks):
            slabs = [r[...]] if rank == 2 else [r[p] for p in range(r.shape[0])]
            for v in slabs:
                v = v if sc == 1.0 else sc * v
                dyv = v if dyv is None else dyv + v
        xh, rstd = _ln_stats(alpha * x_ref[...] + s_ref[...])
        dz_ref[...] = _ln_bwd(dyv * g_ref[...], xh, rstd)
        dg_ref[...] += _rowsum(dyv * xh)
        db_ref[...] += _rowsum(dyv)

    row = pl.BlockSpec((t, d), lambda i: (i, 0))
    vec = pl.BlockSpec((1, d), lambda i: (0, 0))
    vsds = jax.ShapeDtypeStruct((1, d), F32)
    dy_specs = [row if a.ndim == 2 else pl.BlockSpec((a.shape[0], t, d), lambda i: (0, i, 0)) for a, _ in dy_terms]
    return pl.pallas_call(body, name=name, grid=(rows // t,), in_specs=dy_specs + [row, row, vec], out_specs=[row, vec, vec],
                          out_shape=[jax.ShapeDtypeStruct((rows, d), F32), vsds, vsds],
                          compiler_params=_cp("arbitrary"))(*[a for a, _ in dy_terms], x, s, g)


def _add_ln_loss(x, s, g, b, tgt, alpha, *, name):
    rows, d = x.shape
    t = _pick(rows, (512, 256))

    def body(x_ref, s_ref, g_ref, b_ref, t_ref, l_ref, dy_ref):
        @pl.when(pl.program_id(0) == 0)
        def _():
            l_ref[...] = jnp.zeros_like(l_ref)

        xh, _ = _ln_stats(alpha * x_ref[...] + s_ref[...])
        e = (xh * g_ref[...] + b_ref[...]) - t_ref[...]
        l_ref[...] += _rowsum(e * e)
        dy_ref[...] = e * (1.0 / d)

    row = pl.BlockSpec((t, d), lambda i: (i, 0))
    vec = pl.BlockSpec((1, d), lambda i: (0, 0))
    return pl.pallas_call(body, name=name, grid=(rows // t,), in_specs=[row, row, vec, vec, row], out_specs=[vec, row],
                          out_shape=[jax.ShapeDtypeStruct((1, d), F32), jax.ShapeDtypeStruct((rows, d), F32)],
                          compiler_params=_cp("arbitrary"))(x, s, g, b, tgt)


def _col_blocks(width, step):
    return [slice(pos, min(pos + step, width)) for pos in range(0, width, step)]


def _conv3(e, w, b):
    r1 = pltpu.roll(e, 1, 0)
    r2 = pltpu.roll(e, 2, 0)
    return w[0:1, :] * r2 + w[1:2, :] * r1 + w[2:3, :] * e + b, (r2, r1, e)


def _ffn_up_fwd(x, w_up, cw, cb, *, name):
    rows, d = x.shape
    nq, _, tc = w_up.shape
    nj = nq // 2
    f = tc * nj
    tm = _pick(rows, (512, 256))
    blocks = _col_blocks(tc, tc)

    def body(x_ref, wg_ref, wv_ref, cw_ref, cb_ref, hf_ref, f_ref, prev_ref):
        @pl.when(pl.program_id(1) == 0)
        def _():
            prev_ref[...] = jnp.zeros_like(prev_ref)

        xb = x_ref[...].astype(BF16)
        for cs in blocks:
            hc = []
            for s, w_ref in ((0, wg_ref), (1, wv_ref)):
                h = jnp.dot(xb, w_ref[:, cs], preferred_element_type=F32)
                hf_ref[s, :, cs] = h
                e = jnp.concatenate([prev_ref[s, :, cs], h], axis=0)
                prev_ref[s, :, cs] = h[tm - FFN_HALO:]
                y, _ = _conv3(e, cw_ref[s, :, cs], cb_ref[s, :, cs])
                hc.append(y[FFN_HALO:])
            gl, _ = _gelu(hc[0])
            f_ref[:, cs] = (gl * hc[1]).astype(BF16)

    in_specs = [
        pl.BlockSpec((tm, d), lambda j, i: (i, 0)),
        pl.BlockSpec((None, d, tc), lambda j, i: (j, 0, 0)),
        pl.BlockSpec((None, d, tc), lambda j, i: (nj + j, 0, 0)),
        pl.BlockSpec((2, 3, tc), lambda j, i: (0, 0, j)),
        pl.BlockSpec((2, 1, tc), lambda j, i: (0, 0, j)),
    ]
    out_specs = [pl.BlockSpec((2, tm, tc), lambda j, i: (0, i, j)), pl.BlockSpec((tm, tc), lambda j, i: (i, j))]
    out_shape = [jax.ShapeDtypeStruct((2, rows, f), F32), jax.ShapeDtypeStruct((rows, f), BF16)]
    return pl.pallas_call(body, name=name, grid=(nj, rows // tm), in_specs=in_specs, out_specs=out_specs, out_shape=out_shape,
                          scratch_shapes=[pltpu.VMEM((2, FFN_HALO, tc), F32)],
                          compiler_params=_cp("parallel", "arbitrary"))(x, w_up, w_up, cw, cb)


def _ffn_up_bwd(hf, df, x, w_up, cw, cb, *, name):
    _, rows, f = hf.shape
    d = x.shape[1]
    nq, _, tc = w_up.shape
    nj = nq // 2
    tm = _pick(rows, (512, 256))
    hb = tm // FFN_HALO
    once = pl.Buffered(1)
    ni = rows // tm
    last_blk = rows // FFN_HALO - 1
    ext = tm + 2 * FFN_HALO
    tile = slice(FFN_HALO, FFN_HALO + tm)
    blocks = _col_blocks(tc, MXU_WIDTH)

    def body(h_ref, hp_ref, hn_ref, d_ref, dn_ref, x_ref, wg_ref, wv_ref, cw_ref, cb_ref, dx_ref, dw_out_ref, dcw_ref, dcb_ref,
             dw_ref):
        i = pl.program_id(1)
        first = i == 0
        last = i == ni - 1

        @pl.when(first)
        def _():
            dw_ref[...] = jnp.zeros_like(dw_ref)
            dcw_ref[...] = jnp.zeros_like(dcw_ref)
            dcb_ref[...] = jnp.zeros_like(dcb_ref)

        xt = x_ref[...].astype(BF16).T
        dx = None
        for cs in blocks:
            wc = cs.stop - cs.start
            d_next = dn_ref[:, cs].astype(F32)[0:FFN_HALO]
            de = jnp.concatenate([jnp.zeros((FFN_HALO, wc), F32), d_ref[:, cs].astype(F32), jnp.where(last, 0.0, d_next)], axis=0)
            taps, hc = [], []
            for s in range(2):
                e = jnp.concatenate([jnp.where(first, 0.0, hp_ref[s, :, cs]), h_ref[s, :, cs], hn_ref[s, :, cs]], axis=0)
                y, tp = _conv3(e, cw_ref[s, :, cs], cb_ref[s, :, cs])
                hc.append(y)
                taps.append(tp)
            gl, th = _gelu(hc[0])
            dhc = (de * hc[1] * _gelu_grad(hc[0], th), de * gl)
            for s, w_ref in ((0, wg_ref), (1, wv_ref)):
                w = cw_ref[s, :, cs]
                g = dhc[s]
                dh = (w[2:3, :] * g + w[1:2, :] * pltpu.roll(g, ext - 1, 0) + w[0:1, :] * pltpu.roll(g, ext - 2, 0))[tile]
                gt = g[tile]
                for k in range(3):
                    dcw_ref[s, k:k + 1, cs] += _rowsum(gt * taps[s][k][tile])
                dcb_ref[s, :, cs] += _rowsum(gt)
                dhb = dh.astype(BF16)
                part = lax.dot_general(dhb, w_ref[:, cs], (((1,), (1,)), ((), ())), preferred_element_type=F32)
                dx = part if dx is None else dx + part
                dw_ref[s, :, cs] += jnp.dot(xt, dhb, preferred_element_type=F32)
        dx_ref[...] = dx

        @pl.when(last)
        def _():
            dw_out_ref[...] = dw_ref[...].astype(BF16)

    in_specs = [
        pl.BlockSpec((2, tm, tc), lambda j, i: (0, i, j)),
        pl.BlockSpec((2, FFN_HALO, tc), lambda j, i: (0, jnp.maximum(i * hb - 1, 0), j)),
        pl.BlockSpec((2, FFN_HALO, tc), lambda j, i: (0, jnp.minimum((i + 1) * hb, last_blk), j)),
        pl.BlockSpec((tm, tc), lambda j, i: (i, j)),
        pl.BlockSpec((BF16_ROWS, tc), lambda j, i: (jnp.minimum((i + 1) * (tm // BF16_ROWS), rows // BF16_ROWS - 1), j)),
        pl.BlockSpec((tm, d), lambda j, i: (i, 0)),
        pl.BlockSpec((None, d, tc), lambda j, i: (j, 0, 0), pipeline_mode=once),
        pl.BlockSpec((None, d, tc), lambda j, i: (nj + j, 0, 0), pipeline_mode=once),
        pl.BlockSpec((2, 3, tc), lambda j, i: (0, 0, j)),
        pl.BlockSpec((2, 1, tc), lambda j, i: (0, 0, j)),
    ]
    out_specs = [
        pl.BlockSpec((None, tm, d), lambda j, i: (j, i, 0)),
        pl.BlockSpec((2, None, d, tc), lambda j, i: (0, j, 0, 0), pipeline_mode=once),
        pl.BlockSpec((2, 3, tc), lambda j, i: (0, 0, j)),
        pl.BlockSpec((2, 1, tc), lambda j, i: (0, 0, j)),
    ]
    out_shape = [jax.ShapeDtypeStruct((nj, rows, d), F32), jax.ShapeDtypeStruct((2, nj, d, tc), BF16),
                 jax.ShapeDtypeStruct((2, 3, f), F32), jax.ShapeDtypeStruct((2, 1, f), F32)]
    dx, dw, dcw, dcb = pl.pallas_call(body, name=name, grid=(nj, ni), in_specs=in_specs, out_specs=out_specs,
                                      out_shape=out_shape, scratch_shapes=[pltpu.VMEM((2, d, tc), F32)],
                                      compiler_params=_cp("parallel", "arbitrary"))(
        hf, hf, hf, df, df, x, w_up, w_up, cw, cb)
    return dx, dw.reshape(nq, d, tc), dcw, dcb


def _mixer_fwd(x, w_in, cw, cb, ga, ba, gb, bb, ws, sbb, *, name):
    rows, d = x.shape
    _, _, w = w_in.shape
    t = _pick(rows, (256,))
    groups = w // B_CHUNK

    def body(x_ref, win_ref, cw_ref, cb_ref, ga_ref, ba_ref, gb_ref, bb_ref, ws_ref, sb_ref, h_ref, o_ref, a2_ref, prev_ref):
        @pl.when(pl.program_id(0) == 0)
        def _():
            prev_ref[...] = jnp.zeros_like(prev_ref)

        xb = x_ref[...].astype(BF16)
        for s in range(4):
            h_ref[s] = jnp.dot(xb, win_ref[s], preferred_element_type=F32)
        a1 = h_ref[0] * _sig(h_ref[1])
        e = jnp.concatenate([prev_ref[...], a1], axis=0)
        prev_ref[...] = a1[t - CONV_HALO:]
        acc = cw_ref[A_KERNEL - 1:A_KERNEL, :] * e
        for k in range(A_KERNEL - 1):
            acc = acc + cw_ref[k:k + 1, :] * pltpu.roll(e, A_KERNEL - 1 - k, 0)
        a2 = acc[CONV_HALO:] + cb_ref[...]
        a2_ref[...] = a2
        xh, _ = _ln_stats(a2)
        a3 = xh * ga_ref[...] + ba_ref[...]
        o_ref[:, 0:w] = (a3 * _sig(a3)).astype(BF16)

        u, _ = _gelu(h_ref[2])
        v1, _ = _gelu(h_ref[3])
        xh2, _ = _ln_stats(v1)
        v2 = (xh2 * gb_ref[...] + bb_ref[...]).astype(BF16)
        for c in range(t // B_CHUNK):
            rs = slice(c * B_CHUNK, (c + 1) * B_CHUNK)
            for g in range(groups):
                cs = slice(g * B_CHUNK, (g + 1) * B_CHUNK)
                mixed = jnp.dot(ws_ref[g], v2[rs, cs], preferred_element_type=F32) + sb_ref[g]
                o_ref[rs, w + g * B_CHUNK:w + (g + 1) * B_CHUNK] = (u[rs, cs] * mixed).astype(BF16)

    vec = pl.BlockSpec((1, w), lambda i: (0, 0))
    grp = pl.BlockSpec((groups, B_CHUNK, B_CHUNK), lambda i: (0, 0, 0))
    in_specs = [
        pl.BlockSpec((t, d), lambda i: (i, 0)),
        pl.BlockSpec((4, d, w), lambda i: (0, 0, 0)),
        pl.BlockSpec((A_KERNEL, w), lambda i: (0, 0)),
        vec, vec, vec, vec, vec, grp, grp,
    ]
    out_specs = [pl.BlockSpec((4, t, w), lambda i: (0, i, 0)), pl.BlockSpec((t, 2 * w), lambda i: (i, 0)),
                 pl.BlockSpec((t, w), lambda i: (i, 0))]
    out_shape = [jax.ShapeDtypeStruct((4, rows, w), F32), jax.ShapeDtypeStruct((rows, 2 * w), BF16),
                 jax.ShapeDtypeStruct((rows, w), F32)]
    return pl.pallas_call(body, name=name, grid=(rows // t,), in_specs=in_specs, out_specs=out_specs, out_shape=out_shape,
                          scratch_shapes=[pltpu.VMEM((CONV_HALO, w), F32)],
                          compiler_params=_cp("arbitrary"))(x, w_in, cw, cb, ga, ba, gb, bb, ws, sbb)


def _mixer_bwd(h0, a2, dab, x, w_in, res, res_scale, cw, ga, ba, gb, bb, ws, wst, sbb, tril, *, name):
    _, rows, w = h0.shape
    d = x.shape[1]
    once = pl.Buffered(1)
    t = _pick(rows, (256,))
    hb = t // CONV_HALO
    ni = rows // t
    last_blk = rows // CONV_HALO - 1
    ext = t + CONV_HALO
    tile = slice(0, t)
    groups = w // B_CHUNK
    taps = A_KERNEL - 1

    def body(h_ref, a2_ref, a2n_ref, d_ref, dn_ref, x_ref, win_ref, res_ref, cw_ref, ga_ref, ba_ref, gb_ref, bb_ref,
             ws_ref, wst_ref, sb_ref, tril_ref, dx_ref, dwin_ref, dcw_ref, dcb_ref, dga_ref, dba_ref, dgb_ref, dbb_ref,
             dws_ref, dsb_ref, dw_ref):
        i = pl.program_id(0)
        first = i == 0
        last = i == ni - 1

        @pl.when(first)
        def _():
            for r in (dw_ref, dcw_ref, dcb_ref, dga_ref, dba_ref, dgb_ref, dbb_ref, dws_ref, dsb_ref):
                r[...] = jnp.zeros_like(r)

        xt = x_ref[...].astype(BF16).T
        dx_terms = []

        def through_w_in(slot, dh):
            dhb = dh.astype(BF16)
            dx_terms.append(lax.dot_general(dhb, win_ref[slot], (((1,), (1,)), ((), ())), preferred_element_type=F32))
            dw_ref[slot] += jnp.dot(xt, dhb, preferred_element_type=F32)

        xh, rstd = _ln_stats(jnp.concatenate([a2_ref[...], a2n_ref[...]], axis=0))
        a3 = xh * ga_ref[...] + ba_ref[...]
        s3 = _sig(a3)
        da_e = jnp.concatenate([d_ref[:, 0:w], jnp.where(last, 0.0, dn_ref[...])], axis=0)
        da3 = da_e * (s3 * (1.0 + a3 * (1.0 - s3)))
        da2 = _ln_bwd(da3 * ga_ref[...], xh, rstd)
        dga_ref[...] += _rowsum(da3[tile] * xh[tile])
        dba_ref[...] += _rowsum(da3[tile])
        dcb_ref[...] += _rowsum(da2[tile])
        sgt = _sig(h_ref[1])
        a1t = h_ref[0] * sgt
        da1t = None
        for k in range(A_KERNEL):
            sh = taps - k
            fed = (da2 if sh == 0 else pltpu.roll(da2, ext - sh, 0))[tile]
            dcw_ref[k:k + 1, :] += _rowsum(a1t * fed)
            term = cw_ref[k:k + 1, :] * fed
            da1t = term if da1t is None else da1t + term
        through_w_in(0, da1t * sgt)
        through_w_in(1, da1t * h_ref[0] * sgt * (1.0 - sgt))

        bu = h_ref[2]
        bv = h_ref[3]
        u, tu = _gelu(bu)
        v1, tv = _gelu(bv)
        xh2, rstd2 = _ln_stats(v1)
        v2 = (xh2 * gb_ref[...] + bb_ref[...]).astype(BF16)
        db = d_ref[:, w:2 * w]
        dmx_all = db * u
        du_parts, dv2_parts = [], []
        for c in range(t // B_CHUNK):
            rs = slice(c * B_CHUNK, (c + 1) * B_CHUNK)
            du_row, dv2_row = [], []
            for g in range(groups):
                cs = slice(g * B_CHUNK, (g + 1) * B_CHUNK)
                v2cg = v2[rs, cs]
                mixed = jnp.dot(ws_ref[g], v2cg, preferred_element_type=F32) + sb_ref[g]
                dmx = dmx_all[rs, cs]
                dmxb = dmx.astype(BF16)
                du_row.append(db[rs, cs] * mixed)
                dv2_row.append(jnp.dot(wst_ref[g], dmxb, preferred_element_type=F32))
                dws_ref[g] += tril_ref[...] * lax.dot_general(dmxb, v2cg, (((1,), (1,)), ((), ())),
                                                               preferred_element_type=F32)
                dsb_ref[g:g + 1, :] += _rowsum(dmx.T)
            du_parts.append(jnp.concatenate(du_row, axis=1))
            dv2_parts.append(jnp.concatenate(dv2_row, axis=1))
        du = jnp.concatenate(du_parts, axis=0)
        dv2 = jnp.concatenate(dv2_parts, axis=0)
        dgb_ref[...] += _rowsum(dv2 * xh2)
        dbb_ref[...] += _rowsum(dv2)
        dv1 = _ln_bwd(dv2 * gb_ref[...], xh2, rstd2)
        through_w_in(2, du * _gelu_grad(bu, tu))
        through_w_in(3, dv1 * _gelu_grad(bv, tv))
        dx_ref[...] = res_scale * res_ref[...] + ((dx_terms[0] + dx_terms[1]) + (dx_terms[2] + dx_terms[3]))

        @pl.when(last)
        def _():
            dwin_ref[...] = dw_ref[...].astype(BF16)

    vec = pl.BlockSpec((1, w), lambda i: (0, 0))
    grp = pl.BlockSpec((groups, B_CHUNK, B_CHUNK), lambda i: (0, 0, 0))
    halo = pl.BlockSpec((CONV_HALO, w), lambda i: (jnp.minimum((i + 1) * hb, last_blk), 0))
    wide = pl.BlockSpec((t, d), lambda i: (i, 0))
    in_specs = [
        pl.BlockSpec((4, t, w), lambda i: (0, i, 0)),
        pl.BlockSpec((t, w), lambda i: (i, 0)),
        halo,
        pl.BlockSpec((t, 2 * w), lambda i: (i, 0)),
        halo,
        wide,
        pl.BlockSpec((4, d, w), lambda i: (0, 0, 0), pipeline_mode=once),
        wide,
        pl.BlockSpec((A_KERNEL, w), lambda i: (0, 0)),
        vec, vec, vec, vec, grp, grp, grp,
        pl.BlockSpec((B_CHUNK, B_CHUNK), lambda i: (0, 0)),
    ]
    vsds = jax.ShapeDtypeStruct((1, w), F32)
    out_specs = [
        wide,
        pl.BlockSpec((4, d, w), lambda i: (0, 0, 0), pipeline_mode=once),
        pl.BlockSpec((A_KERNEL, w), lambda i: (0, 0)),
        vec, vec, vec, vec, vec, grp,
        pl.BlockSpec((groups, B_CHUNK), lambda i: (0, 0)),
    ]
    out_shape = [jax.ShapeDtypeStruct((rows, d), F32), jax.ShapeDtypeStruct((4, d, w), BF16),
                 jax.ShapeDtypeStruct((A_KERNEL, w), F32),
                 vsds, vsds, vsds, vsds, vsds, jax.ShapeDtypeStruct((groups, B_CHUNK, B_CHUNK), F32),
                 jax.ShapeDtypeStruct((groups, B_CHUNK), F32)]
    return pl.pallas_call(body, name=name, grid=(ni,), in_specs=in_specs, out_specs=out_specs, out_shape=out_shape,
                          scratch_shapes=[pltpu.VMEM((4, d, w), F32)], compiler_params=_cp("arbitrary"))(
        h0, a2, a2, dab, dab, x, w_in, res, cw, ga, ba, gb, bb, ws, wst, sbb, tril)


GROUP_ROWS = Q_PER_KV * ATT_BLOCK


def _attn_mask(n):
    qi = lax.broadcasted_iota(jnp.int32, (GROUP_ROWS, 2 * ATT_BLOCK), 0) & (ATT_BLOCK - 1)
    sj = lax.broadcasted_iota(jnp.int32, (GROUP_ROWS, 2 * ATT_BLOCK), 1)
    diff = qi + ATT_BLOCK - sj
    return (diff >= 0) & (diff < ATT_BLOCK) & ((n > 0) | (sj >= ATT_BLOCK))


def _stack_heads(ref, kvh, dtype):
    heads = [ref[:, (kvh * Q_PER_KV + g) * HEAD_DIM:(kvh * Q_PER_KV + g + 1) * HEAD_DIM] for g in range(Q_PER_KV)]
    return jnp.concatenate(heads, axis=0).astype(dtype)


def _per_row_sink(sink_ref, kvh):
    head = lax.broadcasted_iota(jnp.int32, (GROUP_ROWS, 1), 0) // ATT_BLOCK
    out = jnp.zeros((GROUP_ROWS, 1), F32)
    for g in range(Q_PER_KV):
        out = jnp.where(head == g, sink_ref[kvh * Q_PER_KV + g], out)
    return out


def _attn_specs(rows, n_q):
    dq = n_q * HEAD_DIM
    dkv = 2 * (n_q // Q_PER_KV) * HEAD_DIM
    kv_blk = dq // dkv
    assert dq % dkv == 0
    return dq, dkv, [
        pl.BlockSpec(memory_space=pltpu.SMEM),
        pl.BlockSpec((ATT_BLOCK, dq), lambda n: (n, 0)),
        pl.BlockSpec((ATT_BLOCK, dkv), lambda n: (n, kv_blk)),
        pl.BlockSpec((ATT_BLOCK, dkv), lambda n: (jnp.maximum(n - 1, 0), kv_blk)),
    ]


def _kv_pair(kvc_ref, kvp_ref, kvh, n_kv):
    ks = slice(kvh * HEAD_DIM, (kvh + 1) * HEAD_DIM)
    vs = slice((n_kv + kvh) * HEAD_DIM, (n_kv + kvh + 1) * HEAD_DIM)
    kk = jnp.concatenate([kvp_ref[:, ks], kvc_ref[:, ks]], axis=0).astype(BF16)
    vv = jnp.concatenate([kvp_ref[:, vs], kvc_ref[:, vs]], axis=0).astype(BF16)
    return kk, vv


def _attn_fwd(qkv, sinks, *, name):
    rows = qkv.shape[0]
    n_q = sinks.shape[0]
    n_kv = n_q // Q_PER_KV
    scale = 1.0 / math.sqrt(HEAD_DIM)
    dq, _, in_specs = _attn_specs(rows, n_q)

    def body(sink_ref, q_ref, kvc_ref, kvp_ref, o_ref, lse_ref):
        valid = _attn_mask(pl.program_id(0))
        for kvh in range(n_kv):
            kk, vv = _kv_pair(kvc_ref, kvp_ref, kvh, n_kv)
            qs = _stack_heads(q_ref, kvh, BF16)
            s = lax.dot_general(qs, kk, (((1,), (1,)), ((), ())), preferred_element_type=F32)
            s = jnp.where(valid, s * scale, -jnp.inf)
            sk = _per_row_sink(sink_ref, kvh)
            m = jnp.maximum(jnp.max(s, axis=1, keepdims=True), sk)
            p = jnp.exp(s - m)
            l = jnp.sum(p, axis=1, keepdims=True) + jnp.exp(sk - m)
            o = jnp.dot((p / l).astype(BF16), vv, preferred_element_type=F32)
            lse = m + jnp.log(l)
            for g in range(Q_PER_KV):
                h = kvh * Q_PER_KV + g
                rs = slice(g * ATT_BLOCK, (g + 1) * ATT_BLOCK)
                o_ref[:, h * HEAD_DIM:(h + 1) * HEAD_DIM] = o[rs]
                lse_ref[:, h:h + 1] = lse[rs]

    out_specs = [pl.BlockSpec((ATT_BLOCK, dq), lambda n: (n, 0)), pl.BlockSpec((ATT_BLOCK, n_q), lambda n: (n, 0))]
    out_shape = [jax.ShapeDtypeStruct((rows, dq), F32), jax.ShapeDtypeStruct((rows, n_q), F32)]
    return pl.pallas_call(body, name=name, grid=(rows // ATT_BLOCK,), in_specs=in_specs, out_specs=out_specs,
                          out_shape=out_shape, compiler_params=_cp("parallel"))(sinks, qkv, qkv, qkv)


def _attn_bwd(qkv, dout, lse, sinks, *, name):
    rows = qkv.shape[0]
    n_q = sinks.shape[0]
    n_kv = n_q // Q_PER_KV
    scale = 1.0 / math.sqrt(HEAD_DIM)
    dq_w, dkv_w, in_specs = _attn_specs(rows, n_q)
    blk_q = pl.BlockSpec((ATT_BLOCK, dq_w), lambda n: (n, 0))
    blk_kv = pl.BlockSpec((ATT_BLOCK, dkv_w), lambda n: (n, 0))
    in_specs = in_specs + [blk_q, pl.BlockSpec((ATT_BLOCK, n_q), lambda n: (n, 0))]

    def body(sink_ref, q_ref, kvc_ref, kvp_ref, do_ref, lse_ref, dq_ref, dkc_ref, dkp_ref, dsink_ref):
        n = pl.program_id(0)

        @pl.when(n == 0)
        def _():
            dsink_ref[...] = jnp.zeros_like(dsink_ref)

        valid = _attn_mask(n)
        head_ids = lax.broadcasted_iota(jnp.int32, (1, n_q), 1)
        dsink = jnp.zeros((1, n_q), F32)
        for kvh in range(n_kv):
            kk, vv = _kv_pair(kvc_ref, kvp_ref, kvh, n_kv)
            qs = _stack_heads(q_ref, kvh, BF16)
            dos = _stack_heads(do_ref, kvh, BF16)
            lse = jnp.concatenate([lse_ref[:, kvh * Q_PER_KV + g:kvh * Q_PER_KV + g + 1] for g in range(Q_PER_KV)], axis=0)
            s = lax.dot_general(qs, kk, (((1,), (1,)), ((), ())), preferred_element_type=F32)
            s = jnp.where(valid, s * scale, -jnp.inf)
            p = jnp.exp(s - lse)
            dp = lax.dot_general(dos, vv, (((1,), (1,)), ((), ())), preferred_element_type=F32)
            delta = jnp.sum(p * dp, axis=1, keepdims=True)
            ds = (p * (dp - delta) * scale).astype(BF16)
            sink_term = jnp.exp(_per_row_sink(sink_ref, kvh) - lse) * delta
            dqs = jnp.dot(ds, kk, preferred_element_type=F32)
            for g in range(Q_PER_KV):
                h = kvh * Q_PER_KV + g
                rs = slice(g * ATT_BLOCK, (g + 1) * ATT_BLOCK)
                dsink = dsink + jnp.where(head_ids == h, -jnp.sum(sink_term[rs]), 0.0)
                dq_ref[:, h * HEAD_DIM:(h + 1) * HEAD_DIM] = dqs[rs]
            dk = lax.dot_general(ds, qs, (((0,), (0,)), ((), ())), preferred_element_type=F32)
            dv = lax.dot_general(p.astype(BF16), dos, (((0,), (0,)), ((), ())), preferred_element_type=F32)
            ks = slice(kvh * HEAD_DIM, (kvh + 1) * HEAD_DIM)
            vs = slice((n_kv + kvh) * HEAD_DIM, (n_kv + kvh + 1) * HEAD_DIM)
            dkp_ref[:, ks] = dk[0:ATT_BLOCK]
            dkc_ref[:, ks] = dk[ATT_BLOCK:]
            dkp_ref[:, vs] = dv[0:ATT_BLOCK]
            dkc_ref[:, vs] = dv[ATT_BLOCK:]
        dsink_ref[...] += dsink

    out_specs = [blk_q, blk_kv, blk_kv, pl.BlockSpec((1, n_q), lambda n: (0, 0))]
    out_shape = [jax.ShapeDtypeStruct((rows, dq_w), F32), jax.ShapeDtypeStruct((rows, dkv_w), F32),
                 jax.ShapeDtypeStruct((rows, dkv_w), F32), jax.ShapeDtypeStruct((1, n_q), F32)]
    return pl.pallas_call(body, name=name, grid=(rows // ATT_BLOCK,), in_specs=in_specs, out_specs=out_specs,
                          out_shape=out_shape, compiler_params=_cp("arbitrary"))(sinks, qkv, qkv, qkv, dout, lse)


def _dqkv_assemble(dq, dkc, dkp, *, name):
    rows, dq_w = dq.shape
    dkv_w = dkc.shape[1]
    nb = rows // ATT_BLOCK

    def body(dq_ref, dkc_ref, dkp_ref, o_ref, db_ref):
        n = pl.program_id(0)

        @pl.when(n == 0)
        def _():
            db_ref[...] = jnp.zeros_like(db_ref)

        dqv = dq_ref[...]
        dkv = dkc_ref[...] + jnp.where(n == nb - 1, 0.0, dkp_ref[...])
        o_ref[:, 0:dq_w] = dqv.astype(BF16)
        o_ref[:, dq_w:dq_w + dkv_w] = dkv.astype(BF16)
        db_ref[:, 0:dq_w] += _rowsum(dqv)
        db_ref[:, dq_w:dq_w + dkv_w] += _rowsum(dkv)

    width = dq_w + dkv_w
    in_specs = [pl.BlockSpec((ATT_BLOCK, dq_w), lambda n: (n, 0)), pl.BlockSpec((ATT_BLOCK, dkv_w), lambda n: (n, 0)),
                pl.BlockSpec((ATT_BLOCK, dkv_w), lambda n: (jnp.minimum(n + 1, nb - 1), 0))]
    out_specs = [pl.BlockSpec((ATT_BLOCK, width), lambda n: (n, 0)), pl.BlockSpec((1, width), lambda n: (0, 0))]
    out_shape = [jax.ShapeDtypeStruct((rows, width), BF16), jax.ShapeDtypeStruct((1, width), F32)]
    return pl.pallas_call(body, name=name, grid=(nb,), in_specs=in_specs, out_specs=out_specs, out_shape=out_shape,
                          compiler_params=_cp("arbitrary"))(dq, dkc, dkp)


def _row_tile(r, c):
    budget = 2 * 1024 * 1024 // (4 * c)
    for cand in (1024, 512, 256, 128, 64, 32, 16):
        if cand <= budget and r % cand == 0:
            return cand
    return r


def _octo_sum(own, recv, place, dest, lead, *, name):
    _, _, r, c = own.shape
    t = _row_tile(r, c)
    lead_idx, buf_shape = lead

    def body(place_ref, own_ref, *rest):
        o_ref = rest[7] if dest is None else rest[8]
        acc = own_ref[...].astype(F32)
        for k in range(7):
            acc = acc + rest[k][...].astype(F32)
        o_ref[...] = acc

    def peer(mask):
        return pl.BlockSpec((None, t, c), lambda i, pr: (pr[2] ^ mask, i, 0))

    if lead_idx is None:
        o_spec = pl.BlockSpec((None, t, c), lambda i, pr: (pr[1], i, 0))
    else:
        o_spec = pl.BlockSpec((None, None, t, c), lambda i, pr: (lead_idx, pr[1], i, 0))
    in_specs = [pl.BlockSpec((None, None, t, c), lambda i, pr: (pr[0], pr[1], i, 0))] + [peer(m) for m in range(1, 8)]
    args = [place, own] + [recv] * 7
    aliases = {}
    if dest is not None:
        in_specs.append(HBM)
        args.append(dest)
        aliases = {9: 0}
    grid_spec = pltpu.PrefetchScalarGridSpec(num_scalar_prefetch=1, grid=(r // t,), in_specs=in_specs, out_specs=o_spec)
    return pl.pallas_call(body, name=name, grid_spec=grid_spec, out_shape=jax.ShapeDtypeStruct(buf_shape, F32),
                          input_output_aliases=aliases, compiler_params=_cp("parallel"))(*args)


def _adamw_math(w, g, m, v):
    nm = ADAM_B1 * m + (1.0 - ADAM_B1) * g
    nv = ADAM_B2 * v + (1.0 - ADAM_B2) * (g * g)
    m_hat = nm / (1.0 - ADAM_B1 ** ADAM_STEP)
    v_hat = nv / (1.0 - ADAM_B2 ** ADAM_STEP)
    return -ADAM_LR * (m_hat / (jnp.sqrt(v_hat) + ADAM_EPS) + ADAM_WD * w), nm, nv


def _adamw(w, g, m, v, *, name):
    r, c = w.shape
    t = _row_tile(r, c)

    def body(w_ref, g_ref, m_ref, v_ref, d_ref, nm_ref, nv_ref, go_ref):
        gv = g_ref[...]
        d_ref[...], nm_ref[...], nv_ref[...] = _adamw_math(w_ref[...], gv, m_ref[...], v_ref[...])
        go_ref[...] = gv

    blk = pl.BlockSpec((t, c), lambda i: (i, 0))
    sds = jax.ShapeDtypeStruct((r, c), F32)
    return pl.pallas_call(body, name=name, grid=(r // t,), in_specs=[blk] * 4, out_specs=[blk] * 4,
                          out_shape=[sds] * 4, compiler_params=_cp("parallel"))(w, g, m, v)


HBM = pl.BlockSpec(memory_space=pl.ANY)


def _place():
    x, y, c = lax.axis_index("x"), lax.axis_index("y"), lax.axis_index("c")
    chips = [(1 - x, y), (x, 1 - y), (1 - x, 1 - y)]
    return x, y, c, 2 * x + y, (x, y, 1 - c), chips


def _rcopy(src, dst, ssem, rsem, dev):
    return pltpu.make_async_remote_copy(src_ref=src, dst_ref=dst, send_sem=ssem, recv_sem=rsem, device_id=dev,
                                        device_id_type=MESH)


HBM_ONLY = pl.BlockSpec(memory_space=pltpu.HBM)
SEM = pl.BlockSpec(memory_space=pltpu.SEMAPHORE)


def _peers():
    x, y, c = lax.axis_index("x"), lax.axis_index("y"), lax.axis_index("c")
    out = []
    for mask in range(1, 8):
        px = 1 - x if mask & 4 else x
        py = 1 - y if mask & 2 else y
        pc = 1 - c if mask & 1 else c
        out.append(((px, py, pc), 2 * px + py, pc, 4 * px + 2 * py + pc))
    return 4 * x + 2 * y + c, out


def _reduce_start(grads, lands, after, *, name, whole=False):
    nt = len(grads)

    def body(*refs):
        ssems, rsems = refs[2 * nt + 1:3 * nt + 1], refs[3 * nt + 1:4 * nt + 1]
        g_out, l_out, token = refs[4 * nt + 1:5 * nt + 1], refs[5 * nt + 1:6 * nt + 1], refs[6 * nt + 1]
        me, peers = _peers()
        for t in range(nt):
            for k, (dev, chip, core, _) in enumerate(peers):
                src = g_out[t] if whole else g_out[t].at[chip, core]
                _rcopy(src, l_out[t].at[me], ssems[t].at[k], rsems[t].at[k], dev).start()
        token[...] = jnp.zeros_like(token)

    sems = [pltpu.SemaphoreType.DMA((7,))] * (2 * nt)
    out_shape = (sems + [pltpu.HBM(g.shape, g.dtype) for g in grads] + [pltpu.HBM(l.shape, l.dtype) for l in lands]
                 + [jax.ShapeDtypeStruct((8, LANES), F32)])
    res = pl.pallas_call(
        body, name=name, in_specs=[HBM_ONLY] * (2 * nt + 1),
        out_specs=[SEM] * (2 * nt) + [HBM_ONLY] * (2 * nt) + [pl.BlockSpec(memory_space=pltpu.VMEM)], out_shape=out_shape,
        input_output_aliases={t: 2 * nt + t for t in range(2 * nt)},
        compiler_params=pltpu.CompilerParams(has_side_effects=DATAFLOW),
    )(*[pltpu.with_memory_space_constraint(a, pltpu.HBM) for a in list(grads) + list(lands) + [after]])
    return res[:nt], res[nt:2 * nt], res[2 * nt:3 * nt], res[3 * nt:4 * nt], res[4 * nt]


def _reduce_wait(grads, lands, ssems, rsems, after, *, name, whole=False):
    nt = len(grads)

    def body(*refs):
        ssem_refs, rsem_refs = refs[2 * nt:3 * nt], refs[3 * nt:4 * nt]
        g_out, l_out = refs[4 * nt + 1:5 * nt + 1], refs[5 * nt + 1:6 * nt + 1]
        me, peers = _peers()
        for t in range(nt):
            for k, (dev, chip, core, _) in enumerate(peers):
                src = g_out[t] if whole else g_out[t].at[chip, core]
                _rcopy(src, l_out[t].at[me], ssem_refs[t].at[k], rsem_refs[t].at[k], dev).wait_send()
        for t in range(nt):
            for k, (dev, _, _, idx) in enumerate(peers):
                slot = l_out[t].at[idx]
                _rcopy(slot, slot, ssem_refs[t].at[k], rsem_refs[t].at[k], dev).wait_recv()

    res = pl.pallas_call(
        body, name=name, in_specs=[HBM_ONLY] * (2 * nt) + [SEM] * (2 * nt) + [HBM_ONLY], out_specs=[HBM_ONLY] * (2 * nt),
        out_shape=[pltpu.HBM(a.shape, a.dtype) for a in list(grads) + list(lands)],
        input_output_aliases={t: t for t in range(2 * nt)},
        compiler_params=pltpu.CompilerParams(has_side_effects=DATAFLOW),
    )(*grads, *lands, *ssems, *rsems, pltpu.with_memory_space_constraint(after, pltpu.HBM))
    return list(res[:nt]), list(res[nt:])
DATAFLOW = pltpu.SideEffectType.DATAFLOW_SIDE_EFFECTING


def _gather_now(bufs, *, name):
    nt = len(bufs)

    def body(*refs):
        outs = refs[nt:2 * nt]
        ssem, rsem = refs[2 * nt:]
        x, y, c, q, sib, chips = _place()
        sends = []
        for t in range(nt):
            for j, (px, py) in enumerate(chips):
                mine = outs[t].at[q, c]
                cp = _rcopy(mine, mine, ssem.at[t, j], rsem.at[t, j], (px, py, c))
                cp.start()
                sends.append(cp)
        for t in range(nt):
            for j, (px, py) in enumerate(chips):
                landed = outs[t].at[2 * px + py, c]
                _rcopy(landed, landed, ssem.at[t, j], rsem.at[t, j], (px, py, c)).wait_recv()
                cp = _rcopy(landed, landed, ssem.at[t, 3 + j], rsem.at[t, 3 + j], sib)
                cp.start()
                sends.append(cp)
        for t in range(nt):
            for j, (px, py) in enumerate(chips):
                passed = outs[t].at[2 * px + py, 1 - c]
                _rcopy(passed, passed, ssem.at[t, 3 + j], rsem.at[t, 3 + j], sib).wait_recv()
        for cp in sends:
            cp.wait_send()

    out_shape = [jax.ShapeDtypeStruct(b.shape, b.dtype) for b in bufs]
    return pl.pallas_call(
        body, name=name, in_specs=[HBM] * nt, out_specs=[HBM] * nt, out_shape=out_shape,
        input_output_aliases={t: t for t in range(nt)},
        scratch_shapes=[pltpu.SemaphoreType.DMA((nt, 6)), pltpu.SemaphoreType.DMA((nt, 6))],
    )(*bufs)


def _gather_start(bufs, half, after, *, name):
    nt = len(bufs)

    def body(*refs):
        ssems, rsems, outs = refs[nt + 1:2 * nt + 1], refs[2 * nt + 1:3 * nt + 1], refs[3 * nt + 1:4 * nt + 1]
        x, y, c, q, sib, chips = _place()
        for t in range(nt):
            for j, (px, py) in enumerate(chips):
                mine = outs[t].at[q, c] if half[t] else outs[t].at[q]
                _rcopy(mine, mine, ssems[t].at[j], rsems[t].at[j], (px, py, c)).start()

    sems = [pltpu.SemaphoreType.DMA((3,))] * (2 * nt)
    out_shape = sems + [pltpu.HBM(b.shape, b.dtype) for b in bufs]
    res = pl.pallas_call(
        body, name=name, in_specs=[HBM_ONLY] * (nt + 1), out_specs=[SEM] * (2 * nt) + [HBM_ONLY] * nt, out_shape=out_shape,
        input_output_aliases={t: 2 * nt + t for t in range(nt)},
        compiler_params=pltpu.CompilerParams(has_side_effects=DATAFLOW),
    )(*[pltpu.with_memory_space_constraint(b, pltpu.HBM) for b in list(bufs) + [after]])
    return res[:nt], res[nt:2 * nt], res[2 * nt:]


def _gather_wait(bufs, half, ssems, rsems, after, *, name):
    nt = len(bufs)

    def body(*refs):
        ssem_refs, rsem_refs = refs[nt:2 * nt], refs[2 * nt:3 * nt]
        outs = refs[3 * nt + 1:]
        x, y, c, q, sib, chips = _place()
        for t in range(nt):
            for j, (px, py) in enumerate(chips):
                mine = outs[t].at[q, c] if half[t] else outs[t].at[q]
                _rcopy(mine, mine, ssem_refs[t].at[j], rsem_refs[t].at[j], (px, py, c)).wait_send()
        for t in range(nt):
            for j, (px, py) in enumerate(chips):
                theirs = outs[t].at[2 * px + py, c] if half[t] else outs[t].at[2 * px + py]
                _rcopy(theirs, theirs, ssem_refs[t].at[j], rsem_refs[t].at[j], (px, py, c)).wait_recv()

    res = pl.pallas_call(
        body, name=name, in_specs=[HBM_ONLY] * nt + [SEM] * (2 * nt) + [HBM], out_specs=[HBM_ONLY] * nt,
        out_shape=[pltpu.HBM(b.shape, b.dtype) for b in bufs], input_output_aliases={t: t for t in range(nt)},
        compiler_params=pltpu.CompilerParams(has_side_effects=DATAFLOW),
    )(*bufs, *ssems, *rsems, after)
    return list(res)


def _sibling_swap(bufs, *, name):
    nt = len(bufs)

    def body(*refs):
        outs = refs[nt:2 * nt]
        ssem, rsem = refs[2 * nt:]
        x, y, c, q, sib, chips = _place()
        sends = []
        for t in range(nt):
            for j, (px, py) in enumerate(chips):
                held = outs[t].at[2 * px + py, c]
                cp = _rcopy(held, held, ssem.at[t, j], rsem.at[t, j], sib)
                cp.start()
                sends.append(cp)
        for t in range(nt):
            for j, (px, py) in enumerate(chips):
                other = outs[t].at[2 * px + py, 1 - c]
                _rcopy(other, other, ssem.at[t, j], rsem.at[t, j], sib).wait_recv()
        for cp in sends:
            cp.wait_send()

    return pl.pallas_call(
        body, name=name, in_specs=[HBM] * nt, out_specs=[HBM] * nt,
        out_shape=[jax.ShapeDtypeStruct(b.shape, b.dtype) for b in bufs], input_output_aliases={t: t for t in range(nt)},
        scratch_shapes=[pltpu.SemaphoreType.DMA((nt, 3)), pltpu.SemaphoreType.DMA((nt, 3))],
    )(*bufs)


def _sibling_share(bufs, layout, *, name):
    no = len(bufs)
    nt = len(layout)

    def body(*refs):
        outs = refs[no:2 * no]
        ssem, rsem = refs[2 * no:]
        x, y, c, q, sib, chips = _place()

        def slot(t, half):
            o, lead = layout[t]
            return outs[o].at[half] if lead is None else outs[o].at[lead, half]

        sends = []
        for t in range(nt):
            cp = _rcopy(slot(t, c), slot(t, c), ssem.at[t], rsem.at[t], sib)
            cp.start()
            sends.append(cp)
        for t in range(nt):
            _rcopy(slot(t, 1 - c), slot(t, 1 - c), ssem.at[t], rsem.at[t], sib).wait_recv()
        for cp in sends:
            cp.wait_send()

    out_shape = [jax.ShapeDtypeStruct(b.shape, b.dtype) for b in bufs]
    return pl.pallas_call(
        body, name=name, in_specs=[HBM] * no, out_specs=[HBM] * no, out_shape=out_shape,
        input_output_aliases={o: o for o in range(no)},
        scratch_shapes=[pltpu.SemaphoreType.DMA((nt,)), pltpu.SemaphoreType.DMA((nt,))],
    )(*bufs)


SMALL_GROUP_OF = (0,) * 6 + (1,) * 2 + (2,) * 2 + (3,) * 4 + (4,) * 9


def _small_pack(local, dims, which, *, name):
    kw, wa, ng, nqkv, nsk, f, dm = dims
    shapes = _small_shapes(dims)
    row_vec = 8 * (-(-kw // 8))
    pos = sorted(local)
    assert all(SMALL_GROUP_OF[p] in which for p in pos)

    def pack_body(*refs):
        loc = dict(zip(pos, refs[:len(pos)]))
        grp = dict(zip(which, refs[len(pos):]))
        for gr in grp.values():
            gr[...] = jnp.zeros_like(gr)
        if 0 in which:
            grp[0][0:kw, :] = loc[0][...]
            for k in range(5):
                grp[0][row_vec + k:row_vec + k + 1, :] = loc[1 + k][...]
        if 1 in which:
            for g in range(ng):
                grp[1][g * B_CHUNK:(g + 1) * B_CHUNK, :] = loc[6][g]
            grp[1][ng * B_CHUNK:ng * B_CHUNK + ng, :] = loc[7][...]
        if 2 in which:
            grp[2][0:1, :] = loc[8][...]
            grp[2][1:2, 0:nsk] = loc[9][...]
        if 3 in which:
            for l in range(2):
                for s in range(2):
                    grp[3][l, s, 0:3, :] = loc[10 + 2 * l][s]
                    grp[3][l, s, 3:4, :] = loc[11 + 2 * l][s]
        if 4 in which:
            for k in range(9):
                grp[4][k:k + 1, :] = loc[14 + k][...]

    vm = pl.BlockSpec(memory_space=pltpu.VMEM)
    return pl.pallas_call(
        pack_body, name=name, in_specs=[vm] * len(pos), out_specs=[vm] * len(which),
        out_shape=[jax.ShapeDtypeStruct(shapes[g], F32) for g in which],
        compiler_params=pltpu.CompilerParams(vmem_limit_bytes=VMEM_LIMIT),
    )(*[local[p] for p in pos])


def _small_shapes(dims):
    kw, wa, ng, nqkv, nsk, f, dm = dims
    return [(8 * (-(-kw // 8)) + 8, wa), (ng * B_CHUNK + 8, B_CHUNK), (8, nqkv), (2, 2, 8, f), (16, dm)]


def _small_update(groups, landed, params, dims, *, name):
    kw, wa, ng, nqkv, nsk, f, dm = dims
    shapes = _small_shapes(dims)
    row_vec = 8 * (-(-kw // 8))
    n_grp = len(shapes)
    flat_params = [a for triple in params for a in triple]
    n_par = len(params)
    vm = pl.BlockSpec(memory_space=pltpu.VMEM)

    def adamw_body(*refs):
        own = refs[:n_grp]
        land = refs[n_grp:2 * n_grp]
        par = refs[2 * n_grp:2 * n_grp + 3 * n_par]
        outs = refs[2 * n_grp + 3 * n_par:2 * n_grp + 7 * n_par]
        loss_ref = refs[2 * n_grp + 7 * n_par]
        tot = refs[2 * n_grp + 7 * n_par + 1:]
        x, y = lax.axis_index("x"), lax.axis_index("y")
        q = 2 * x + y
        me = 4 * x + 2 * y + lax.axis_index("c")
        for gi in range(n_grp):
            acc = None
            for dv in range(8):
                term = jnp.where(me == dv, own[gi][...], land[gi][dv])
                acc = term if acc is None else acc + term
            tot[gi][...] = acc
        ta, tb, tc, td, te = tot

        def mine(piece):
            out = piece(0)
            for k in range(1, 4):
                out = jnp.where(q == k, piece(k), out)
            return out

        def update(p, grad, index=None):
            at = (lambda r: r[...]) if index is None else (lambda r: r[index])
            w_ref, m_ref, v_ref = par[3 * p:3 * p + 3]
            g_ref, d_ref, nm_ref, nv_ref = outs[4 * p:4 * p + 4]
            delta, nm, nv = _adamw_math(at(w_ref), grad, at(m_ref), at(v_ref))
            for r, val in ((g_ref, grad), (d_ref, delta), (nm_ref, nm), (nv_ref, nv)):
                if index is None:
                    r[...] = val
                else:
                    r[index] = val

        wq = wa // 4
        update(0, mine(lambda k: ta[0:kw, k * wq:(k + 1) * wq]), (0,))
        for k in range(5):
            update(1 + k, ta[row_vec + k:row_vec + k + 1, :])
        for g in range(ng):
            update(6, tb[g * B_CHUNK:(g + 1) * B_CHUNK, :], (0, g))
        update(7, tb[ng * B_CHUNK:ng * B_CHUNK + ng, :], (0,))
        nq4 = nqkv // 4
        update(8, mine(lambda k: tc[0:1, k * nq4:(k + 1) * nq4]))
        update(9, tc[1:2, 0:nsk])
        fh = f // 2
        for l in range(2):
            update(10, mine(lambda k: td[l, k // 2, 0:3, (k % 2) * fh:(k % 2 + 1) * fh]), (l,))
            update(11, jnp.concatenate([td[l, 0, 3:4, :], td[l, 1, 3:4, :]], axis=1), (slice(l, l + 1),))
        dq4 = dm // 4
        for i in range(2):
            for j in range(2):
                for p, base in ((12, 0), (13, 4)):
                    row = base + 2 * i + j
                    update(p, mine(lambda k: te[row:row + 1, k * dq4:(k + 1) * dq4]), (i, slice(j, j + 1)))
        loss_ref[...] = (0.5 / dm) * jnp.sum(te[8:9, :], axis=1, keepdims=True)

    out_shape = []
    for w, _, _ in params:
        out_shape += [jax.ShapeDtypeStruct(w.shape, F32)] * 4
    out_shape.append(jax.ShapeDtypeStruct((1, 1), F32))
    res = pl.pallas_call(
        adamw_body, name=name + "_adamw", in_specs=[vm] * (2 * n_grp + 3 * n_par), out_specs=[vm] * len(out_shape),
        out_shape=out_shape, scratch_shapes=[pltpu.VMEM(s, F32) for s in shapes],
        compiler_params=pltpu.CompilerParams(vmem_limit_bytes=VMEM_LIMIT),
    )(*groups, *landed, *flat_params)
    return [res[4 * p:4 * p + 4] for p in range(n_par)], res[-1]


def _pack(arrays, rows_multiple):
    flat = jnp.concatenate([a.reshape(-1) for a in arrays])
    rows = -(-flat.shape[0] // LANES)
    rows = -(-rows // rows_multiple) * rows_multiple
    flat = jnp.pad(flat, (0, rows * LANES - flat.shape[0]))
    return flat.reshape(rows, LANES)


def _unshard_cols(stacked):
    moved = jnp.moveaxis(stacked, 0, -2)
    return moved.reshape(moved.shape[:-2] + (4 * stacked.shape[-1],))


def kernel(x, ab_w_in, a_conv_w, a_conv_b, a_norm_g, a_norm_b, b_norm_g, b_norm_b, b_spatial_w, b_spatial_b, ab_w_out, c_w_qkv, c_b_qkv, c_sinks, c_w_o, ffn_w_up, ffn_conv_w, ffn_conv_b, ffn_w_down, ln_g, ln_b, loss_target, m_ab_w_in, m_a_conv_w, m_a_conv_b, m_a_norm_g, m_a_norm_b, m_b_norm_g, m_b_norm_b, m_b_spatial_w, m_b_spatial_b, m_ab_w_out, m_c_w_qkv, m_c_b_qkv, m_c_sinks, m_c_w_o, m_ffn_w_up, m_ffn_conv_w, m_ffn_conv_b, m_ffn_w_down, m_ln_g, m_ln_b, v_ab_w_in, v_a_conv_w, v_a_conv_b, v_a_norm_g, v_a_norm_b, v_b_norm_g, v_b_norm_b, v_b_spatial_w, v_b_spatial_b, v_ab_w_out, v_c_w_qkv, v_c_b_qkv, v_c_sinks, v_c_w_o, v_ffn_w_up, v_ffn_conv_w, v_ffn_conv_b, v_ffn_w_down, v_ln_g, v_ln_b):
    rows, d = x.shape[1], x.shape[2]
    depth = ln_g.shape[0]
    assert depth == 2 and x.shape[0] == 1
    alpha = (2.0 * depth) ** 0.25
    f = ffn_w_down.shape[1] * 4
    n_q = c_sinks.shape[1]
    q_idx = 2 * lax.axis_index("x") + lax.axis_index("y")
    c_idx = lax.axis_index("c")
    xs, tgt = x[0], loss_target[0]

    def own_slot(part):
        buf = lax.empty((4,) + part.shape, part.dtype)
        return lax.dynamic_update_slice(buf, part[None], (q_idx, 0, 0, 0))

    def halves(wm):
        return own_slot(wm.astype(BF16).reshape((2, wm.shape[0] // 2) + wm.shape[1:]))

    small_sharded = [a_conv_w[0], c_b_qkv[0], ffn_conv_w, ln_g, ln_b]
    small_pack = _pack(small_sharded, 16)
    bufs = [halves(ab_w_in[0]), own_slot(small_pack.reshape(2, small_pack.shape[0] // 2, LANES)), halves(ab_w_out[0]),
            halves(ffn_w_up[0]), halves(ffn_w_down[0]), halves(c_w_qkv[0]), halves(c_w_o[0]),
            halves(ffn_w_up[1]), halves(ffn_w_down[1])]
    whole = lambda g: g.reshape(4, 2 * g.shape[2], g.shape[3])
    n_now = 2
    first_two = _gather_now(bufs[:n_now], name="gather_now")
    w_in, small_all = [whole(g) for g in first_two]
    later = bufs[n_now:]
    half = [True, True] + [False] * (len(later) - 2)
    ssems, rsems, started = _gather_start(later, half, first_two[1], name="gather_start")

    def arrive(idx, after, tag):
        idx = [i - n_now for i in idx]
        halved = [half[i] for i in idx]
        got = _gather_wait([started[i] for i in idx], halved, [ssems[i] for i in idx], [rsems[i] for i in idx], after,
                           name=f"gather_wait_{tag}")
        if all(halved):
            got = _sibling_swap(got, name=f"gather_swap_{tag}")
        return [whole(g) for g in got]

    small_all = small_all.reshape(4, -1)
    sh_shapes = [s.shape for s in small_sharded]
    pieces, pos = [], 0
    for s in sh_shapes:
        n = math.prod(s)
        pieces.append(_unshard_cols(small_all[:, pos:pos + n].reshape((4,) + s)))
        pos += n
    conv_w_a, b_qkv, conv_w_f, ln_gf, ln_bf = pieces

    tril = jnp.tril(jnp.ones((B_CHUNK, B_CHUNK), F32))
    ws = (b_spatial_w[0] * tril).astype(BF16)
    wst = jnp.swapaxes(ws, 1, 2)
    sbb = jnp.broadcast_to(b_spatial_b[0][:, :, None], b_spatial_w[0].shape)
    mix_vecs = [a_conv_b, a_norm_g, a_norm_b, b_norm_g, b_norm_b]
    cw_f = [jnp.swapaxes(conv_w_f[l].reshape(3, 2, f), 0, 1) for l in range(depth)]
    cb_f = [ffn_conv_b[l].reshape(2, 1, f) for l in range(depth)]
    lng = lambda i, j: ln_gf[i, j].reshape(1, d)
    lnb = lambda i, j: ln_bf[i, j].reshape(1, d)
    sinks = c_sinks[0]

    w_up, w_down = [None, None], [None, None]

    def ffn_fwd(xin, l):
        w_up[l], = arrive([3 + 4 * l], xin, f"up{l}")
        hf, fact = _ffn_up_fwd(xin, w_up[l], cw_f[l], cb_f[l], name=f"ffn{l}_up")
        w_down[l] = arrive([4 + 4 * l], fact, f"down{l}")[0].reshape(-1, d)
        return hf, fact

    h0, ab, a2 = _mixer_fwd(xs, w_in, conv_w_a, *mix_vecs, ws, sbb, name="mix_fwd")
    w_out = arrive([2], ab, "out")[0].reshape(-1, d)
    mix, x1 = _proj_add_ln(ab, w_out, xs, lng(0, 0), lnb(0, 0), alpha, name="mix_out_ln00")
    hf0, f0 = ffn_fwd(x1, 0)
    ffn0, x2 = _proj_add_ln(f0, w_down[0], x1, lng(0, 1), lnb(0, 1), alpha, name="ffn0_down_ln01")
    w_qkv = _unshard_cols(arrive([5], x2, "qkv")[0])
    qkv = _matmul(x2, w_qkv, name="att_qkv", tm=1024, tn=w_qkv.shape[1], tk=1024, bias=b_qkv.reshape(1, -1))
    ao, lse = _attn_fwd(qkv, sinks, name="att_core")
    w_o = arrive([6], ao, "o")[0].reshape(-1, d)
    att, x3 = _proj_add_ln(ao, w_o, x2, lng(1, 0), lnb(1, 0), alpha, name="att_out_ln10")
    hf1, f1 = ffn_fwd(x3, 1)
    ffn1 = _matmul(f1, w_down[1], name="ffn1_down", tm=512, tn=1024, tk=2816)
    sq_err, dy = _add_ln_loss(x3, ffn1, lng(1, 1), lnb(1, 1), tgt, alpha, name="ln11_loss")

    def owner_view(g):
        if g.ndim == 3:
            return g.reshape(4, 2, g.shape[1] // 2, g.shape[2])
        return g.reshape(4, 2, g.shape[0] // 8, g.shape[1])

    in_flight = []

    def send_grads(tag, grads, after):
        lands = [lax.empty((8,) + g.shape[2:], BF16) for g in grads]
        ss, rs, g_thru, l_thru, token = _reduce_start(grads, lands, after, name=f"reduce_start_{tag}")
        in_flight.append((tag, g_thru, l_thru, ss, rs))
        return token[0:1, 0:1]

    def ffn_bwd(dz, xin, hf, fact, l):
        d_wdown = _matmul(fact, dz, name=f"ffn{l}_down_dw", ta=True, tm=1408, tn=1024, tk=2048, out_dtype=BF16)
        dfa = _matmul(dz, w_down[l], name=f"ffn{l}_down_dx", tb=True, tm=1024, tn=1408, tk=1024, out_dtype=BF16)
        dx_parts, d_wup, dcw, dcb = _ffn_up_bwd(hf, dfa, xin, w_up[l], cw_f[l], cb_f[l], name=f"ffn{l}_up_bwd")
        tok = send_grads(f"ffn{l}", [owner_view(d_wup), owner_view(d_wdown)], dcb)
        return [(dx_parts, 1.0), (dz, alpha)], dcw, dcb, tok

    dz, dg11, db11 = _add_ln_bwd([(dy, 1.0)], x3, ffn1, lng(1, 1), alpha, name="ln11_bwd")
    dx3, dcw1, dcb1, tok = ffn_bwd(dz, x3, hf1, f1, 1)
    dz, dg10, db10 = _add_ln_bwd(dx3, x2, att, lng(1, 0) + tok, alpha, name="ln10_bwd")
    d_wo = _matmul(ao, dz, name="att_out_dw", ta=True, tm=1024, tn=1024, tk=1024, out_dtype=BF16)
    dao = _matmul(dz, w_o, name="att_out_dx", tb=True, tm=1024, tn=1024, tk=1024)
    dq, dkc, dkp, d_sinks = _attn_bwd(qkv, dao, lse, sinks, name="att_core_bwd")
    dqkv, d_bqkv = _dqkv_assemble(dq, dkc, dkp, name="att_dqkv")
    d_wqkv = _matmul(x2, dqkv, name="att_qkv_dw", ta=True, tm=1024, tn=dqkv.shape[1], tk=1024, out_dtype=BF16)
    d_wqkv_st = jnp.moveaxis(d_wqkv.reshape(d_wqkv.shape[0], 4, -1), 1, 0)
    tok = send_grads("att", [owner_view(d_wqkv_st), owner_view(d_wo)], d_bqkv)
    dx2 = _matmul(dqkv, w_qkv, name="att_qkv_dx", tb=True, tm=1024, tn=1024, tk=dqkv.shape[1], addend=(dz, alpha))
    dz, dg01, db01 = _add_ln_bwd([(dx2, 1.0)], x1, ffn0, lng(0, 1) + tok, alpha, name="ln01_bwd")
    dx1, dcw0, dcb0, tok = ffn_bwd(dz, x1, hf0, f0, 0)
    dz, dg00, db00 = _add_ln_bwd(dx1, xs, mix, lng(0, 0) + tok, alpha, name="ln00_bwd")
    d_wout = _matmul(ab, dz, name="mix_out_dw", ta=True, tm=1024, tn=1024, tk=1024, out_dtype=BF16)
    dab = _matmul(dz, w_out, name="mix_out_dx", tb=True, tm=1024, tn=1024, tk=1024)
    small_dims = (a_conv_w.shape[1], a_conv_b.shape[1], b_spatial_w.shape[1], 4 * c_b_qkv.shape[1], n_q, f, d)

    def send_small(local_arrays, which, tag, after):
        groups = _small_pack(local_arrays, small_dims, which, name=f"small_pack_{tag}")
        lands = [lax.empty((8,) + g.shape, F32) for g in groups]
        return _reduce_start(groups, lands, after, name=f"small_start_{tag}", whole=True)

    ready = [d_bqkv, d_sinks, dcw0, dcb0, dcw1, dcb1, dg00, dg01, dg10, dg11, db00, db01, db10, db11, sq_err]
    ss_e, rs_e, g_e, l_e, token = send_small(dict(zip(range(8, 23), ready)), (2, 3, 4), "early", dab)
    tok = send_grads("out", [owner_view(d_wout)], token)
    grad_x, d_win, d_cwa, d_cba, d_ga, d_ba, d_gb, d_bb, d_ws, d_sb = _mixer_bwd(
        h0, a2, dab, xs, w_in, dz, alpha, conv_w_a, *mix_vecs[1:], ws, wst, sbb, tril + tok, name="mix_bwd")

    small_w = [a_conv_w, a_conv_b, a_norm_g, a_norm_b, b_norm_g, b_norm_b, b_spatial_w, b_spatial_b, c_b_qkv,
               c_sinks, ffn_conv_w, ffn_conv_b, ln_g, ln_b]
    small_m = [m_a_conv_w, m_a_conv_b, m_a_norm_g, m_a_norm_b, m_b_norm_g, m_b_norm_b, m_b_spatial_w, m_b_spatial_b,
               m_c_b_qkv, m_c_sinks, m_ffn_conv_w, m_ffn_conv_b, m_ln_g, m_ln_b]
    small_v = [v_a_conv_w, v_a_conv_b, v_a_norm_g, v_a_norm_b, v_b_norm_g, v_b_norm_b, v_b_spatial_w, v_b_spatial_b,
               v_c_b_qkv, v_c_sinks, v_ffn_conv_w, v_ffn_conv_b, v_ln_g, v_ln_b]
    place = jnp.stack([q_idx, c_idx, 4 * lax.axis_index("x") + 2 * lax.axis_index("y") + c_idx]).astype(jnp.int32)
    where = {"mix": [(0, None)], "out": [(1, None)], "att": [(2, None), (3, None)], "ffn0": [(4, 0), (5, 0)],
             "ffn1": [(4, 1), (5, 1)]}
    big_w = [ab_w_in, ab_w_out, c_w_qkv, c_w_o, ffn_w_up, ffn_w_down]
    big_m = [m_ab_w_in, m_ab_w_out, m_c_w_qkv, m_c_w_o, m_ffn_w_up, m_ffn_w_down]
    big_v = [v_ab_w_in, v_ab_w_out, v_c_w_qkv, v_c_w_o, v_ffn_w_up, v_ffn_w_down]
    big_out = [None] * 6

    def finish(tags, after, label):
        bufs, layout = {}, []
        for tag, g_thru, l_thru, ss, rs in in_flight:
            if tag not in tags:
                continue
            own, landed = _reduce_wait(g_thru, l_thru, ss, rs, after, name=f"reduce_wait_{tag}")
            for k, (o, lead) in enumerate(where[tag]):
                piece = own[k].shape[2:]
                shape = (2,) + piece if lead is None else (2, 2) + piece
                bufs[o] = _octo_sum(own[k], landed[k], place, bufs.get(o), (lead, shape), name=f"reduce_sum_{tag}{k}")
                layout.append((o, lead))
        order = sorted(bufs)
        shared = _sibling_share([bufs[o] for o in order], [(order.index(o), lead) for o, lead in layout],
                                name=f"reduce_share_{label}")
        for o, g in zip(order, shared):
            w = big_w[o]
            two_d = lambda a: a.reshape(-1, a.shape[-1])
            outs = _adamw(two_d(w), two_d(g), two_d(big_m[o]), two_d(big_v[o]), name=f"adamw_big{o}")
            big_out[o] = [r.reshape(w.shape) for r in outs]
        return big_out[order[-1]][0]

    mixer_small = [d_cwa, d_cba, d_ga, d_ba, d_gb, d_bb, d_ws, d_sb]
    ss_l, rs_l, g_l, l_l, token = send_small(dict(zip(range(8), mixer_small)), (0, 1), "late", grad_x)
    tok = send_grads("mix", [owner_view(d_win)], after=token)
    done = finish(("ffn1", "att", "ffn0", "out"), d_ws + tok, "early")
    g_l, l_l = _reduce_wait(g_l, l_l, ss_l, rs_l, done, name="small_wait_late", whole=True)
    g_e, l_e = _reduce_wait(g_e, l_e, ss_e, rs_e, done, name="small_wait_early", whole=True)
    small_out, loss = _small_update(g_l + g_e, l_l + l_e, list(zip(small_w, small_m, small_v)), small_dims,
                                    name="small_tail")
    loss = loss[0, 0]
    small_g = [o[0] for o in small_out]
    sm_delta = [o[1] for o in small_out]
    sm_m = [o[2] for o in small_out]
    sm_v = [o[3] for o in small_out]
    finish(("mix",), sm_delta[6], "mix")

    order_big = {0: 0, 9: 1, 10: 2, 13: 3, 14: 4, 17: 5}
    order_small = {1: 0, 2: 1, 3: 2, 4: 3, 5: 4, 6: 5, 7: 6, 8: 7, 11: 8, 12: 9, 15: 10, 16: 11, 18: 12, 19: 13}
    grads, deltas, new_m, new_v = [], [], [], []
    for pos_w in range(20):
        if pos_w in order_big:
            t = order_big[pos_w]
            grads.append(big_out[t][3])
            deltas.append(big_out[t][0])
            new_m.append(big_out[t][1])
            new_v.append(big_out[t][2])
        else:
            t = order_small[pos_w]
            grads.append(small_g[t])
            deltas.append(sm_delta[t])
            new_m.append(sm_m[t])
            new_v.append(sm_v[t])
    return (loss, grad_x[None], *grads, *deltas, *new_m, *new_v)
```

```python
import math

import jax
import jax.numpy as jnp
from jax import lax
from jax.experimental import pallas as pl
from jax.experimental.pallas import tpu as pltpu

F32 = jnp.float32
BF16 = jnp.bfloat16
MESH = pl.DeviceIdType.MESH

LN_EPS = 1e-5
HEAD_DIM = 64
ATT_BLOCK = 128
Q_PER_KV = 8
A_KERNEL = 31
CONV_HALO = 32
FFN_HALO = 8
BF16_ROWS = 16
B_CHUNK = 128
LANES = 128
MXU_WIDTH = 256
GELU_C = math.sqrt(2.0 / math.pi)
ADAM_LR = 0.001
ADAM_B1 = 0.9
ADAM_B2 = 0.999
ADAM_EPS = 1e-08
ADAM_WD = 0.01
ADAM_STEP = 10
VMEM_LIMIT = 56 * 1024 * 1024


def _cp(*dims):
    return pltpu.CompilerParams(dimension_semantics=dims, vmem_limit_bytes=VMEM_LIMIT)


def _pick(n, prefs):
    for p in prefs:
        if n % p == 0:
            return p
    return n


def _sig(x):
    return 1.0 / (1.0 + jnp.exp(-x))


def _gelu(x):
    t = jnp.tanh(GELU_C * (x + 0.044715 * (x * x * x)))
    return x * (0.5 * (1.0 + t)), t


def _gelu_grad(x, t):
    return 0.5 * (1.0 + t) + 0.5 * x * (1.0 - t * t) * (GELU_C * (1.0 + 3.0 * 0.044715 * x * x))


def _ln_stats(z):
    mu = jnp.mean(z, axis=-1, keepdims=True)
    zc = z - mu
    var = jnp.mean(zc * zc, axis=-1, keepdims=True)
    rstd = lax.rsqrt(var + LN_EPS)
    return zc * rstd, rstd


def _ln_bwd(dxh, xh, rstd):
    return rstd * (dxh - jnp.mean(dxh, axis=-1, keepdims=True) - xh * jnp.mean(dxh * xh, axis=-1, keepdims=True))


def _rowsum(a):
    return jnp.sum(a, axis=0, keepdims=True)


def _lshape(a):
    return (a.shape[0], a.shape[1]) if a.ndim == 2 else (a.shape[1], a.shape[0] * a.shape[2])


def _spec2(arr, blk_r, blk_c, ridx, cidx):
    if len(arr.shape) == 2:
        return pl.BlockSpec((blk_r, blk_c), lambda i, j, k: (ridx(i, j, k), cidx(i, j, k)))
    per = arr.shape[2] // blk_c
    assert arr.shape[2] % blk_c == 0
    return pl.BlockSpec((None, blk_r, blk_c), lambda i, j, k: (cidx(i, j, k) // per, ridx(i, j, k), cidx(i, j, k) % per))


def _matmul(a, b, *, name, ta=False, tb=False, tm, tn, tk, out_dtype=F32, out_stack=None, bias=None, addend=None):
    ar, ac = _lshape(a)
    br, bc = _lshape(b)
    m, kdim = (ac, ar) if ta else (ar, ac)
    n = br if tb else bc
    assert (bc if tb else br) == kdim
    tm, tn, tk = min(tm, m), min(tn, n), min(tk, kdim)
    assert m % tm == 0 and n % tn == 0 and kdim % tk == 0, (name, m, n, kdim, tm, tn, tk)
    nk = kdim // tk
    gi, gj, gk = (lambda i, j, k: i), (lambda i, j, k: j), (lambda i, j, k: k)
    a_spec = _spec2(a, tk, tm, gk, gi) if ta else _spec2(a, tm, tk, gi, gk)
    b_spec = _spec2(b, tn, tk, gj, gk) if tb else _spec2(b, tk, tn, gk, gj)
    if out_stack is None:
        out_sds = jax.ShapeDtypeStruct((m, n), out_dtype)
    else:
        out_sds = jax.ShapeDtypeStruct((out_stack, m, n // out_stack), out_dtype)
    o_spec = _spec2(out_sds, tm, tn, gi, gj)
    in_specs = [a_spec, b_spec]
    args = [a, b]
    if bias is not None:
        in_specs.append(pl.BlockSpec((1, tn), lambda i, j, k: (0, j)))
        args.append(bias)
    scale = None
    if addend is not None:
        add_arr, scale = addend
        in_specs.append(pl.BlockSpec((tm, tn), lambda i, j, k: (i, j)))
        args.append(add_arr)
    use_acc = nk > 1 and out_dtype != F32
    dn = (((0 if ta else 1,), (1 if tb else 0,)), ((), ()))

    def body(*refs):
        a_ref, b_ref = refs[0], refs[1]
        pos = 2
        bias_ref = add_ref = None
        if bias is not None:
            bias_ref = refs[pos]
            pos += 1
        if addend is not None:
            add_ref = refs[pos]
            pos += 1
        o_ref = refs[pos]
        acc_ref = refs[pos + 1] if use_acc else o_ref
        p = lax.dot_general(a_ref[...].astype(BF16), b_ref[...].astype(BF16), dn, preferred_element_type=F32)

        def finish(val):
            if bias_ref is not None:
                val = val + bias_ref[...]
            if add_ref is not None:
                val = val + scale * add_ref[...]
            return val.astype(out_dtype)

        if nk == 1:
            o_ref[...] = finish(p)
        else:
            k = pl.program_id(2)

            @pl.when(k == 0)
            def _():
                acc_ref[...] = p

            @pl.when(k > 0)
            def _():
                acc_ref[...] += p

            if use_acc or bias_ref is not None or add_ref is not None:
                @pl.when(k == nk - 1)
                def _():
                    o_ref[...] = finish(acc_ref[...])

    return pl.pallas_call(
        body, name=name, grid=(m // tm, n // tn, nk), in_specs=in_specs, out_specs=o_spec, out_shape=out_sds,
        scratch_shapes=[pltpu.VMEM((tm, tn), F32)] if use_acc else [],
        compiler_params=_cp("parallel", "parallel", "arbitrary"),
    )(*args)


def _proj_add_ln(a, w, x, g, b, alpha, *, name):
    rows, kdim = a.shape
    d = w.shape[1]
    tm = _pick(rows, (512, 256))

    def body(a_ref, w_ref, x_ref, g_ref, b_ref, s_ref, y_ref):
        s = jnp.dot(a_ref[...].astype(BF16), w_ref[...], preferred_element_type=F32)
        s_ref[...] = s
        xh, _ = _ln_stats(alpha * x_ref[...] + s)
        y_ref[...] = xh * g_ref[...] + b_ref[...]

    row = pl.BlockSpec((tm, d), lambda i: (i, 0))
    vec = pl.BlockSpec((1, d), lambda i: (0, 0))
    sds = jax.ShapeDtypeStruct((rows, d), F32)
    return pl.pallas_call(body, name=name, grid=(rows // tm,),
                          in_specs=[pl.BlockSpec((tm, kdim), lambda i: (i, 0)), pl.BlockSpec((kdim, d), lambda i: (0, 0)),
                                    row, vec, vec],
                          out_specs=[row, row], out_shape=[sds, sds], compiler_params=_cp("parallel"))(a, w, x, g, b)


def _add_ln_bwd(dy_terms, x, s, g, alpha, *, name):
    rows, d = x.shape
    t = _pick(rows, (512, 256))
    nterm = len(dy_terms)
    scales = [sc for _, sc in dy_terms]
    ranks = [a.ndim for a, _ in dy_terms]

    def body(*refs):
        dy_refs = refs[:nterm]
        x_ref, s_ref, g_ref, dz_ref, dg_ref, db_ref = refs[nterm:]

        @pl.when(pl.program_id(0) == 0)
        def _():
            dg_ref[...] = jnp.zeros_like(dg_ref)
            db_ref[...] = jnp.zeros_like(db_ref)

        dyv = None
        for r, sc, rank in zip(dy_refs, scales, ranks):
            slabs = [r[...]] if rank == 2 else [r[p] for p in range(r.shape[0])]
            for v in slabs:
                v = v if sc == 1.0 else sc * v
                dyv = v if dyv is None else dyv + v
        xh, rstd = _ln_stats(alpha * x_ref[...] + s_ref[...])
        dz_ref[...] = _ln_bwd(dyv * g_ref[...], xh, rstd)
        dg_ref[...] += _rowsum(dyv * xh)
        db_ref[...] += _rowsum(dyv)

    row = pl.BlockSpec((t, d), lambda i: (i, 0))
    vec = pl.BlockSpec((1, d), lambda i: (0, 0))
    vsds = jax.ShapeDtypeStruct((1, d), F32)
    dy_specs = [row if a.ndim == 2 else pl.BlockSpec((a.shape[0], t, d), lambda i: (0, i, 0)) for a, _ in dy_terms]
    return pl.pallas_call(body, name=name, grid=(rows // t,), in_specs=dy_specs + [row, row, vec], out_specs=[row, vec, vec],
                          out_shape=[jax.ShapeDtypeStruct((rows, d), F32), vsds, vsds],
                          compiler_params=_cp("arbitrary"))(*[a for a, _ in dy_terms], x, s, g)


def _proj_add_ln_loss(a, w, x, g, b, tgt, alpha, *, name):
    rows, kdim = a.shape
    d = w.shape[1]
    tm = _pick(rows, (512, 256))

    def body(a_ref, w_ref, x_ref, g_ref, b_ref, t_ref, s_ref, l_ref, dy_ref):
        @pl.when(pl.program_id(0) == 0)
        def _():
            l_ref[...] = jnp.zeros_like(l_ref)

        s = jnp.dot(a_ref[...].astype(BF16), w_ref[...], preferred_element_type=F32)
        s_ref[...] = s
        xh, _ = _ln_stats(alpha * x_ref[...] + s)
        e = (xh * g_ref[...] + b_ref[...]) - t_ref[...]
        l_ref[...] += _rowsum(e * e)
        dy_ref[...] = e * (1.0 / d)

    row = pl.BlockSpec((tm, d), lambda i: (i, 0))
    vec = pl.BlockSpec((1, d), lambda i: (0, 0))
    sds = jax.ShapeDtypeStruct((rows, d), F32)
    return pl.pallas_call(body, name=name, grid=(rows // tm,),
                          in_specs=[pl.BlockSpec((tm, kdim), lambda i: (i, 0)), pl.BlockSpec((kdim, d), lambda i: (0, 0)),
                                    row, vec, vec, row],
                          out_specs=[row, vec, row], out_shape=[sds, jax.ShapeDtypeStruct((1, d), F32), sds],
                          compiler_params=_cp("arbitrary"))(a, w, x, g, b, tgt)


def _col_blocks(width, step):
    return [slice(pos, min(pos + step, width)) for pos in range(0, width, step)]


def _conv3(e, w, b):
    r1 = pltpu.roll(e, 1, 0)
    r2 = pltpu.roll(e, 2, 0)
    return w[0:1, :] * r2 + w[1:2, :] * r1 + w[2:3, :] * e + b, (r2, r1, e)


def _ffn_up_fwd(x, w_up, cw, cb, *, name):
    rows, d = x.shape
    nq, _, tc = w_up.shape
    nj = nq // 2
    f = tc * nj
    tm = _pick(rows, (512, 256))
    blocks = _col_blocks(tc, tc)

    def body(x_ref, wg_ref, wv_ref, cw_ref, cb_ref, hf_ref, f_ref, prev_ref):
        @pl.when(pl.program_id(1) == 0)
        def _():
            prev_ref[...] = jnp.zeros_like(prev_ref)

        xb = x_ref[...].astype(BF16)
        for cs in blocks:
            hc = []
            for s, w_ref in ((0, wg_ref), (1, wv_ref)):
                h = jnp.dot(xb, w_ref[:, cs], preferred_element_type=F32)
                hf_ref[s, :, cs] = h
                e = jnp.concatenate([prev_ref[s, :, cs], h], axis=0)
                prev_ref[s, :, cs] = h[tm - FFN_HALO:]
                y, _ = _conv3(e, cw_ref[s, :, cs], cb_ref[s, :, cs])
                hc.append(y[FFN_HALO:])
            gl, _ = _gelu(hc[0])
            f_ref[:, cs] = (gl * hc[1]).astype(BF16)

    in_specs = [
        pl.BlockSpec((tm, d), lambda j, i: (i, 0)),
        pl.BlockSpec((None, d, tc), lambda j, i: (j, 0, 0)),
        pl.BlockSpec((None, d, tc), lambda j, i: (nj + j, 0, 0)),
        pl.BlockSpec((2, 3, tc), lambda j, i: (0, 0, j)),
        pl.BlockSpec((2, 1, tc), lambda j, i: (0, 0, j)),
    ]
    out_specs = [pl.BlockSpec((2, tm, tc), lambda j, i: (0, i, j)), pl.BlockSpec((tm, tc), lambda j, i: (i, j))]
    out_shape = [jax.ShapeDtypeStruct((2, rows, f), F32), jax.ShapeDtypeStruct((rows, f), BF16)]
    return pl.pallas_call(body, name=name, grid=(nj, rows // tm), in_specs=in_specs, out_specs=out_specs, out_shape=out_shape,
                          scratch_shapes=[pltpu.VMEM((2, FFN_HALO, tc), F32)],
                          compiler_params=_cp("parallel", "arbitrary"))(x, w_up, w_up, cw, cb)


def _ffn_up_bwd(hf, df, x, w_up, cw, cb, *, name):
    _, rows, f = hf.shape
    d = x.shape[1]
    nq, _, tc = w_up.shape
    nj = nq // 2
    tm = _pick(rows, (512, 256))
    hb = tm // FFN_HALO
    once = pl.Buffered(1)
    ni = rows // tm
    last_blk = rows // FFN_HALO - 1
    ext = tm + 2 * FFN_HALO
    tile = slice(FFN_HALO, FFN_HALO + tm)
    blocks = _col_blocks(tc, MXU_WIDTH)

    def body(h_ref, hp_ref, hn_ref, d_ref, dn_ref, x_ref, wg_ref, wv_ref, cw_ref, cb_ref, dx_ref, dw_out_ref, dcw_ref, dcb_ref,
             dw_ref):
        i = pl.program_id(1)
        first = i == 0
        last = i == ni - 1

        @pl.when(first)
        def _():
            dw_ref[...] = jnp.zeros_like(dw_ref)
            dcw_ref[...] = jnp.zeros_like(dcw_ref)
            dcb_ref[...] = jnp.zeros_like(dcb_ref)

        xt = x_ref[...].astype(BF16).T
        dx = None
        for cs in blocks:
            wc = cs.stop - cs.start
            d_next = dn_ref[:, cs].astype(F32)[0:FFN_HALO]
            de = jnp.concatenate([jnp.zeros((FFN_HALO, wc), F32), d_ref[:, cs].astype(F32), jnp.where(last, 0.0, d_next)], axis=0)
            taps, hc = [], []
            for s in range(2):
                e = jnp.concatenate([jnp.where(first, 0.0, hp_ref[s, :, cs]), h_ref[s, :, cs], hn_ref[s, :, cs]], axis=0)
                y, tp = _conv3(e, cw_ref[s, :, cs], cb_ref[s, :, cs])
                hc.append(y)
                taps.append(tp)
            gl, th = _gelu(hc[0])
            dhc = (de * hc[1] * _gelu_grad(hc[0], th), de * gl)
            for s, w_ref in ((0, wg_ref), (1, wv_ref)):
                w = cw_ref[s, :, cs]
                g = dhc[s]
                dh = (w[2:3, :] * g + w[1:2, :] * pltpu.roll(g, ext - 1, 0) + w[0:1, :] * pltpu.roll(g, ext - 2, 0))[tile]
                gt = g[tile]
                for k in range(3):
                    dcw_ref[s, k:k + 1, cs] += _rowsum(gt * taps[s][k][tile])
                dcb_ref[s, :, cs] += _rowsum(gt)
                dhb = dh.astype(BF16)
                part = lax.dot_general(dhb, w_ref[:, cs], (((1,), (1,)), ((), ())), preferred_element_type=F32)
                dx = part if dx is None else dx + part
                dw_ref[s, :, cs] += jnp.dot(xt, dhb, preferred_element_type=F32)
        dx_ref[...] = dx

        @pl.when(last)
        def _():
            dw_out_ref[...] = dw_ref[...].astype(BF16)

    in_specs = [
        pl.BlockSpec((2, tm, tc), lambda j, i: (0, i, j)),
        pl.BlockSpec((2, FFN_HALO, tc), lambda j, i: (0, jnp.maximum(i * hb - 1, 0), j)),
        pl.BlockSpec((2, FFN_HALO, tc), lambda j, i: (0, jnp.minimum((i + 1) * hb, last_blk), j)),
        pl.BlockSpec((tm, tc), lambda j, i: (i, j)),
        pl.BlockSpec((BF16_ROWS, tc), lambda j, i: (jnp.minimum((i + 1) * (tm // BF16_ROWS), rows // BF16_ROWS - 1), j)),
        pl.BlockSpec((tm, d), lambda j, i: (i, 0)),
        pl.BlockSpec((None, d, tc), lambda j, i: (j, 0, 0), pipeline_mode=once),
        pl.BlockSpec((None, d, tc), lambda j, i: (nj + j, 0, 0), pipeline_mode=once),
        pl.BlockSpec((2, 3, tc), lambda j, i: (0, 0, j)),
        pl.BlockSpec((2, 1, tc), lambda j, i: (0, 0, j)),
    ]
    out_specs = [
        pl.BlockSpec((None, tm, d), lambda j, i: (j, i, 0)),
        pl.BlockSpec((2, None, d, tc), lambda j, i: (0, j, 0, 0), pipeline_mode=once),
        pl.BlockSpec((2, 3, tc), lambda j, i: (0, 0, j)),
        pl.BlockSpec((2, 1, tc), lambda j, i: (0, 0, j)),
    ]
    out_shape = [jax.ShapeDtypeStruct((nj, rows, d), F32), jax.ShapeDtypeStruct((2, nj, d, tc), BF16),
                 jax.ShapeDtypeStruct((2, 3, f), F32), jax.ShapeDtypeStruct((2, 1, f), F32)]
    dx, dw, dcw, dcb = pl.pallas_call(body, name=name, grid=(nj, ni), in_specs=in_specs, out_specs=out_specs,
                                      out_shape=out_shape, scratch_shapes=[pltpu.VMEM((2, d, tc), F32)],
                                      compiler_params=_cp("parallel", "arbitrary"))(
        hf, hf, hf, df, df, x, w_up, w_up, cw, cb)
    return dx, dw.reshape(nq, d, tc), dcw, dcb


def _mixer_fwd(x, w_in, cw, cb, ga, ba, gb, bb, ws, sbb, *, name):
    rows, d = x.shape
    _, _, w = w_in.shape
    t = _pick(rows, (256,))
    groups = w // B_CHUNK

    def body(x_ref, win_ref, cw_ref, cb_ref, ga_ref, ba_ref, gb_ref, bb_ref, ws_ref, sb_ref, h_ref, o_ref, a2_ref, prev_ref):
        @pl.when(pl.program_id(0) == 0)
        def _():
            prev_ref[...] = jnp.zeros_like(prev_ref)

        xb = x_ref[...].astype(BF16)
        for s in range(4):
            h_ref[s] = jnp.dot(xb, win_ref[s], preferred_element_type=F32)
        a1 = h_ref[0] * _sig(h_ref[1])
        e = jnp.concatenate([prev_ref[...], a1], axis=0)
        prev_ref[...] = a1[t - CONV_HALO:]
        acc = cw_ref[A_KERNEL - 1:A_KERNEL, :] * e
        for k in range(A_KERNEL - 1):
            acc = acc + cw_ref[k:k + 1, :] * pltpu.roll(e, A_KERNEL - 1 - k, 0)
        a2 = acc[CONV_HALO:] + cb_ref[...]
        a2_ref[...] = a2
        xh, _ = _ln_stats(a2)
        a3 = xh * ga_ref[...] + ba_ref[...]
        o_ref[:, 0:w] = (a3 * _sig(a3)).astype(BF16)

        u, _ = _gelu(h_ref[2])
        v1, _ = _gelu(h_ref[3])
        xh2, _ = _ln_stats(v1)
        v2 = (xh2 * gb_ref[...] + bb_ref[...]).astype(BF16)
        for c in range(t // B_CHUNK):
            rs = slice(c * B_CHUNK, (c + 1) * B_CHUNK)
            for g in range(groups):
                cs = slice(g * B_CHUNK, (g + 1) * B_CHUNK)
                mixed = jnp.dot(ws_ref[g], v2[rs, cs], preferred_element_type=F32) + sb_ref[g]
                o_ref[rs, w + g * B_CHUNK:w + (g + 1) * B_CHUNK] = (u[rs, cs] * mixed).astype(BF16)

    vec = pl.BlockSpec((1, w), lambda i: (0, 0))
    grp = pl.BlockSpec((groups, B_CHUNK, B_CHUNK), lambda i: (0, 0, 0))
    in_specs = [
        pl.BlockSpec((t, d), lambda i: (i, 0)),
        pl.BlockSpec((4, d, w), lambda i: (0, 0, 0)),
        pl.BlockSpec((A_KERNEL, w), lambda i: (0, 0)),
        vec, vec, vec, vec, vec, grp, grp,
    ]
    out_specs = [pl.BlockSpec((4, t, w), lambda i: (0, i, 0)), pl.BlockSpec((t, 2 * w), lambda i: (i, 0)),
                 pl.BlockSpec((t, w), lambda i: (i, 0))]
    out_shape = [jax.ShapeDtypeStruct((4, rows, w), F32), jax.ShapeDtypeStruct((rows, 2 * w), BF16),
                 jax.ShapeDtypeStruct((rows, w), F32)]
    return pl.pallas_call(body, name=name, grid=(rows // t,), in_specs=in_specs, out_specs=out_specs, out_shape=out_shape,
                          scratch_shapes=[pltpu.VMEM((CONV_HALO, w), F32)],
                          compiler_params=_cp("arbitrary"))(x, w_in, cw, cb, ga, ba, gb, bb, ws, sbb)


def _mixer_bwd(h0, a2, dab, x, w_in, res, res_scale, cw, ga, ba, gb, bb, ws, wst, sbb, tril, *, name):
    _, rows, w = h0.shape
    d = x.shape[1]
    once = pl.Buffered(1)
    t = _pick(rows, (256,))
    hb = t // CONV_HALO
    ni = rows // t
    last_blk = rows // CONV_HALO - 1
    ext = t + CONV_HALO
    tile = slice(0, t)
    groups = w // B_CHUNK
    taps = A_KERNEL - 1

    def body(h_ref, a2_ref, a2n_ref, d_ref, dn_ref, x_ref, win_ref, res_ref, cw_ref, ga_ref, ba_ref, gb_ref, bb_ref,
             ws_ref, wst_ref, sb_ref, tril_ref, dx_ref, dwin_ref, dcw_ref, dcb_ref, dga_ref, dba_ref, dgb_ref, dbb_ref,
             dws_ref, dsb_ref, dw_ref):
        i = pl.program_id(0)
        first = i == 0
        last = i == ni - 1

        @pl.when(first)
        def _():
            for r in (dw_ref, dcw_ref, dcb_ref, dga_ref, dba_ref, dgb_ref, dbb_ref, dws_ref, dsb_ref):
                r[...] = jnp.zeros_like(r)

        xt = x_ref[...].astype(BF16).T
        dx_terms = []

        def through_w_in(slot, dh):
            dhb = dh.astype(BF16)
            dx_terms.append(lax.dot_general(dhb, win_ref[slot], (((1,), (1,)), ((), ())), preferred_element_type=F32))
            dw_ref[slot] += jnp.dot(xt, dhb, preferred_element_type=F32)

        xh, rstd = _ln_stats(jnp.concatenate([a2_ref[...], a2n_ref[...]], axis=0))
        a3 = xh * ga_ref[...] + ba_ref[...]
        s3 = _sig(a3)
        da_e = jnp.concatenate([d_ref[:, 0:w], jnp.where(last, 0.0, dn_ref[...])], axis=0)
        da3 = da_e * (s3 * (1.0 + a3 * (1.0 - s3)))
        da2 = _ln_bwd(da3 * ga_ref[...], xh, rstd)
        dga_ref[...] += _rowsum(da3[tile] * xh[tile])
        dba_ref[...] += _rowsum(da3[tile])
        dcb_ref[...] += _rowsum(da2[tile])
        sgt = _sig(h_ref[1])
        a1t = h_ref[0] * sgt
        da1t = None
        for k in range(A_KERNEL):
            sh = taps - k
            fed = (da2 if sh == 0 else pltpu.roll(da2, ext - sh, 0))[tile]
            dcw_ref[k:k + 1, :] += _rowsum(a1t * fed)
            term = cw_ref[k:k + 1, :] * fed
            da1t = term if da1t is None else da1t + term
        through_w_in(0, da1t * sgt)
        through_w_in(1, da1t * h_ref[0] * sgt * (1.0 - sgt))

        bu = h_ref[2]
        bv = h_ref[3]
        u, tu = _gelu(bu)
        v1, tv = _gelu(bv)
        xh2, rstd2 = _ln_stats(v1)
        v2 = (xh2 * gb_ref[...] + bb_ref[...]).astype(BF16)
        db = d_ref[:, w:2 * w]
        dmx_all = db * u
        du_parts, dv2_parts = [], []
        for c in range(t // B_CHUNK):
            rs = slice(c * B_CHUNK, (c + 1) * B_CHUNK)
            du_row, dv2_row = [], []
            for g in range(groups):
                cs = slice(g * B_CHUNK, (g + 1) * B_CHUNK)
                v2cg = v2[rs, cs]
                mixed = jnp.dot(ws_ref[g], v2cg, preferred_element_type=F32) + sb_ref[g]
                dmx = dmx_all[rs, cs]
                dmxb = dmx.astype(BF16)
                du_row.append(db[rs, cs] * mixed)
                dv2_row.append(jnp.dot(wst_ref[g], dmxb, preferred_element_type=F32))
                dws_ref[g] += tril_ref[...] * lax.dot_general(dmxb, v2cg, (((1,), (1,)), ((), ())),
                                                               preferred_element_type=F32)
                dsb_ref[g:g + 1, :] += _rowsum(dmx.T)
            du_parts.append(jnp.concatenate(du_row, axis=1))
            dv2_parts.append(jnp.concatenate(dv2_row, axis=1))
        du = jnp.concatenate(du_parts, axis=0)
        dv2 = jnp.concatenate(dv2_parts, axis=0)
        dgb_ref[...] += _rowsum(dv2 * xh2)
        dbb_ref[...] += _rowsum(dv2)
        dv1 = _ln_bwd(dv2 * gb_ref[...], xh2, rstd2)
        through_w_in(2, du * _gelu_grad(bu, tu))
        through_w_in(3, dv1 * _gelu_grad(bv, tv))
        dx_ref[...] = res_scale * res_ref[...] + ((dx_terms[0] + dx_terms[1]) + (dx_terms[2] + dx_terms[3]))

        @pl.when(last)
        def _():
            dwin_ref[...] = dw_ref[...].astype(BF16)

    vec = pl.BlockSpec((1, w), lambda i: (0, 0))
    grp = pl.BlockSpec((groups, B_CHUNK, B_CHUNK), lambda i: (0, 0, 0))
    halo = pl.BlockSpec((CONV_HALO, w), lambda i: (jnp.minimum((i + 1) * hb, last_blk), 0))
    wide = pl.BlockSpec((t, d), lambda i: (i, 0))
    in_specs = [
        pl.BlockSpec((4, t, w), lambda i: (0, i, 0)),
        pl.BlockSpec((t, w), lambda i: (i, 0)),
        halo,
        pl.BlockSpec((t, 2 * w), lambda i: (i, 0)),
        halo,
        wide,
        pl.BlockSpec((4, d, w), lambda i: (0, 0, 0), pipeline_mode=once),
        wide,
        pl.BlockSpec((A_KERNEL, w), lambda i: (0, 0)),
        vec, vec, vec, vec, grp, grp, grp,
        pl.BlockSpec((B_CHUNK, B_CHUNK), lambda i: (0, 0)),
    ]
    vsds = jax.ShapeDtypeStruct((1, w), F32)
    out_specs = [
        wide,
        pl.BlockSpec((4, d, w), lambda i: (0, 0, 0), pipeline_mode=once),
        pl.BlockSpec((A_KERNEL, w), lambda i: (0, 0)),
        vec, vec, vec, vec, vec, grp,
        pl.BlockSpec((groups, B_CHUNK), lambda i: (0, 0)),
    ]
    out_shape = [jax.ShapeDtypeStruct((rows, d), F32), jax.ShapeDtypeStruct((4, d, w), BF16),
                 jax.ShapeDtypeStruct((A_KERNEL, w), F32),
                 vsds, vsds, vsds, vsds, vsds, jax.ShapeDtypeStruct((groups, B_CHUNK, B_CHUNK), F32),
                 jax.ShapeDtypeStruct((groups, B_CHUNK), F32)]
    return pl.pallas_call(body, name=name, grid=(ni,), in_specs=in_specs, out_specs=out_specs, out_shape=out_shape,
                          scratch_shapes=[pltpu.VMEM((4, d, w), F32)], compiler_params=_cp("arbitrary"))(
        h0, a2, a2, dab, dab, x, w_in, res, cw, ga, ba, gb, bb, ws, wst, sbb, tril)


GROUP_ROWS = Q_PER_KV * ATT_BLOCK


def _attn_mask(n):
    qi = lax.broadcasted_iota(jnp.int32, (GROUP_ROWS, 2 * ATT_BLOCK), 0) & (ATT_BLOCK - 1)
    sj = lax.broadcasted_iota(jnp.int32, (GROUP_ROWS, 2 * ATT_BLOCK), 1)
    diff = qi + ATT_BLOCK - sj
    return (diff >= 0) & (diff < ATT_BLOCK) & ((n > 0) | (sj >= ATT_BLOCK))


def _stack_heads(ref, kvh, dtype):
    heads = [ref[:, (kvh * Q_PER_KV + g) * HEAD_DIM:(kvh * Q_PER_KV + g + 1) * HEAD_DIM] for g in range(Q_PER_KV)]
    return jnp.concatenate(heads, axis=0).astype(dtype)


def _per_row_sink(sink_ref, kvh):
    head = lax.broadcasted_iota(jnp.int32, (GROUP_ROWS, 1), 0) // ATT_BLOCK
    out = jnp.zeros((GROUP_ROWS, 1), F32)
    for g in range(Q_PER_KV):
        out = jnp.where(head == g, sink_ref[kvh * Q_PER_KV + g], out)
    return out


def _attn_specs(rows, n_q):
    dq = n_q * HEAD_DIM
    dkv = 2 * (n_q // Q_PER_KV) * HEAD_DIM
    kv_blk = dq // dkv
    assert dq % dkv == 0
    return dq, dkv, [
        pl.BlockSpec(memory_space=pltpu.SMEM),
        pl.BlockSpec((ATT_BLOCK, dq), lambda n: (n, 0)),
        pl.BlockSpec((ATT_BLOCK, dkv), lambda n: (n, kv_blk)),
        pl.BlockSpec((ATT_BLOCK, dkv), lambda n: (jnp.maximum(n - 1, 0), kv_blk)),
    ]


def _kv_pair(kvc_ref, kvp_ref, kvh, n_kv):
    ks = slice(kvh * HEAD_DIM, (kvh + 1) * HEAD_DIM)
    vs = slice((n_kv + kvh) * HEAD_DIM, (n_kv + kvh + 1) * HEAD_DIM)
    kk = jnp.concatenate([kvp_ref[:, ks], kvc_ref[:, ks]], axis=0).astype(BF16)
    vv = jnp.concatenate([kvp_ref[:, vs], kvc_ref[:, vs]], axis=0).astype(BF16)
    return kk, vv


def _attn_fwd(qkv, sinks, *, name):
    rows = qkv.shape[0]
    n_q = sinks.shape[0]
    n_kv = n_q // Q_PER_KV
    scale = 1.0 / math.sqrt(HEAD_DIM)
    dq, _, in_specs = _attn_specs(rows, n_q)

    def body(sink_ref, q_ref, kvc_ref, kvp_ref, o_ref, lse_ref):
        valid = _attn_mask(pl.program_id(0))
        for kvh in range(n_kv):
            kk, vv = _kv_pair(kvc_ref, kvp_ref, kvh, n_kv)
            qs = _stack_heads(q_ref, kvh, BF16)
            s = lax.dot_general(qs, kk, (((1,), (1,)), ((), ())), preferred_element_type=F32)
            s = jnp.where(valid, s * scale, -jnp.inf)
            sk = _per_row_sink(sink_ref, kvh)
            m = jnp.maximum(jnp.max(s, axis=1, keepdims=True), sk)
            p = jnp.exp(s - m)
            l = jnp.sum(p, axis=1, keepdims=True) + jnp.exp(sk - m)
            o = jnp.dot((p / l).astype(BF16), vv, preferred_element_type=F32)
            lse = m + jnp.log(l)
            for g in range(Q_PER_KV):
                h = kvh * Q_PER_KV + g
                rs = slice(g * ATT_BLOCK, (g + 1) * ATT_BLOCK)
                o_ref[:, h * HEAD_DIM:(h + 1) * HEAD_DIM] = o[rs]
                lse_ref[:, h:h + 1] = lse[rs]

    out_specs = [pl.BlockSpec((ATT_BLOCK, dq), lambda n: (n, 0)), pl.BlockSpec((ATT_BLOCK, n_q), lambda n: (n, 0))]
    out_shape = [jax.ShapeDtypeStruct((rows, dq), F32), jax.ShapeDtypeStruct((rows, n_q), F32)]
    return pl.pallas_call(body, name=name, grid=(rows // ATT_BLOCK,), in_specs=in_specs, out_specs=out_specs,
                          out_shape=out_shape, compiler_params=_cp("parallel"))(sinks, qkv, qkv, qkv)


def _attn_bwd(qkv, dout, lse, sinks, *, name):
    rows = qkv.shape[0]
    n_q = sinks.shape[0]
    n_kv = n_q // Q_PER_KV
    scale = 1.0 / math.sqrt(HEAD_DIM)
    dq_w, dkv_w, in_specs = _attn_specs(rows, n_q)
    blk_q = pl.BlockSpec((ATT_BLOCK, dq_w), lambda n: (n, 0))
    blk_kv = pl.BlockSpec((ATT_BLOCK, dkv_w), lambda n: (n, 0))
    in_specs = in_specs + [blk_q, pl.BlockSpec((ATT_BLOCK, n_q), lambda n: (n, 0))]

    def body(sink_ref, q_ref, kvc_ref, kvp_ref, do_ref, lse_ref, dq_ref, dkc_ref, dkp_ref, dsink_ref):
        n = pl.program_id(0)

        @pl.when(n == 0)
        def _():
            dsink_ref[...] = jnp.zeros_like(dsink_ref)

        valid = _attn_mask(n)
        head_ids = lax.broadcasted_iota(jnp.int32, (1, n_q), 1)
        dsink = jnp.zeros((1, n_q), F32)
        for kvh in range(n_kv):
            kk, vv = _kv_pair(kvc_ref, kvp_ref, kvh, n_kv)
            qs = _stack_heads(q_ref, kvh, BF16)
            dos = _stack_heads(do_ref, kvh, BF16)
            lse = jnp.concatenate([lse_ref[:, kvh * Q_PER_KV + g:kvh * Q_PER_KV + g + 1] for g in range(Q_PER_KV)], axis=0)
            s = lax.dot_general(qs, kk, (((1,), (1,)), ((), ())), preferred_element_type=F32)
            s = jnp.where(valid, s * scale, -jnp.inf)
            p = jnp.exp(s - lse)
            dp = lax.dot_general(dos, vv, (((1,), (1,)), ((), ())), preferred_element_type=F32)
            delta = jnp.sum(p * dp, axis=1, keepdims=True)
            ds = (p * (dp - delta) * scale).astype(BF16)
            sink_term = jnp.exp(_per_row_sink(sink_ref, kvh) - lse) * delta
            dqs = jnp.dot(ds, kk, preferred_element_type=F32)
            for g in range(Q_PER_KV):
                h = kvh * Q_PER_KV + g
                rs = slice(g * ATT_BLOCK, (g + 1) * ATT_BLOCK)
                dsink = dsink + jnp.where(head_ids == h, -jnp.sum(sink_term[rs]), 0.0)
                dq_ref[:, h * HEAD_DIM:(h + 1) * HEAD_DIM] = dqs[rs]
            dk = lax.dot_general(ds, qs, (((0,), (0,)), ((), ())), preferred_element_type=F32)
            dv = lax.dot_general(p.astype(BF16), dos, (((0,), (0,)), ((), ())), preferred_element_type=F32)
            ks = slice(kvh * HEAD_DIM, (kvh + 1) * HEAD_DIM)
            vs = slice((n_kv + kvh) * HEAD_DIM, (n_kv + kvh + 1) * HEAD_DIM)
            dkp_ref[:, ks] = dk[0:ATT_BLOCK]
            dkc_ref[:, ks] = dk[ATT_BLOCK:]
            dkp_ref[:, vs] = dv[0:ATT_BLOCK]
            dkc_ref[:, vs] = dv[ATT_BLOCK:]
        dsink_ref[...] += dsink

    out_specs = [blk_q, blk_kv, blk_kv, pl.BlockSpec((1, n_q), lambda n: (0, 0))]
    out_shape = [jax.ShapeDtypeStruct((rows, dq_w), F32), jax.ShapeDtypeStruct((rows, dkv_w), F32),
                 jax.ShapeDtypeStruct((rows, dkv_w), F32), jax.ShapeDtypeStruct((1, n_q), F32)]
    return pl.pallas_call(body, name=name, grid=(rows // ATT_BLOCK,), in_specs=in_specs, out_specs=out_specs,
                          out_shape=out_shape, compiler_params=_cp("arbitrary"))(sinks, qkv, qkv, qkv, dout, lse)


def _dqkv_assemble(dq, dkc, dkp, *, name):
    rows, dq_w = dq.shape
    dkv_w = dkc.shape[1]
    nb = rows // ATT_BLOCK

    def body(dq_ref, dkc_ref, dkp_ref, o_ref, db_ref):
        n = pl.program_id(0)

        @pl.when(n == 0)
        def _():
            db_ref[...] = jnp.zeros_like(db_ref)

        dqv = dq_ref[...]
        dkv = dkc_ref[...] + jnp.where(n == nb - 1, 0.0, dkp_ref[...])
        o_ref[:, 0:dq_w] = dqv.astype(BF16)
        o_ref[:, dq_w:dq_w + dkv_w] = dkv.astype(BF16)
        db_ref[:, 0:dq_w] += _rowsum(dqv)
        db_ref[:, dq_w:dq_w + dkv_w] += _rowsum(dkv)

    width = dq_w + dkv_w
    in_specs = [pl.BlockSpec((ATT_BLOCK, dq_w), lambda n: (n, 0)), pl.BlockSpec((ATT_BLOCK, dkv_w), lambda n: (n, 0)),
                pl.BlockSpec((ATT_BLOCK, dkv_w), lambda n: (jnp.minimum(n + 1, nb - 1), 0))]
    out_specs = [pl.BlockSpec((ATT_BLOCK, width), lambda n: (n, 0)), pl.BlockSpec((1, width), lambda n: (0, 0))]
    out_shape = [jax.ShapeDtypeStruct((rows, width), BF16), jax.ShapeDtypeStruct((1, width), F32)]
    return pl.pallas_call(body, name=name, grid=(nb,), in_specs=in_specs, out_specs=out_specs, out_shape=out_shape,
                          compiler_params=_cp("arbitrary"))(dq, dkc, dkp)


def _row_tile(r, c):
    budget = 2 * 1024 * 1024 // (4 * c)
    for cand in (1024, 512, 256, 128, 64, 32, 16):
        if cand <= budget and r % cand == 0:
            return cand
    return r


def _octo_sum(own, recv, place, dest, lead, *, name):
    _, _, r, c = own.shape
    t = _row_tile(r, c)
    lead_idx, buf_shape = lead

    def body(place_ref, own_ref, *rest):
        o_ref = rest[7] if dest is None else rest[8]
        acc = own_ref[...].astype(F32)
        for k in range(7):
            acc = acc + rest[k][...].astype(F32)
        o_ref[...] = acc

    def peer(mask):
        return pl.BlockSpec((None, t, c), lambda i, pr: (pr[2] ^ mask, i, 0))

    if lead_idx is None:
        o_spec = pl.BlockSpec((None, t, c), lambda i, pr: (pr[1], i, 0))
    else:
        o_spec = pl.BlockSpec((None, None, t, c), lambda i, pr: (lead_idx, pr[1], i, 0))
    in_specs = [pl.BlockSpec((None, None, t, c), lambda i, pr: (pr[0], pr[1], i, 0))] + [peer(m) for m in range(1, 8)]
    args = [place, own] + [recv] * 7
    aliases = {}
    if dest is not None:
        in_specs.append(HBM)
        args.append(dest)
        aliases = {9: 0}
    grid_spec = pltpu.PrefetchScalarGridSpec(num_scalar_prefetch=1, grid=(r // t,), in_specs=in_specs, out_specs=o_spec)
    return pl.pallas_call(body, name=name, grid_spec=grid_spec, out_shape=jax.ShapeDtypeStruct(buf_shape, F32),
                          input_output_aliases=aliases, compiler_params=_cp("parallel"))(*args)


def _adamw_math(w, g, m, v):
    nm = ADAM_B1 * m + (1.0 - ADAM_B1) * g
    nv = ADAM_B2 * v + (1.0 - ADAM_B2) * (g * g)
    m_hat = nm / (1.0 - ADAM_B1 ** ADAM_STEP)
    v_hat = nv / (1.0 - ADAM_B2 ** ADAM_STEP)
    return -ADAM_LR * (m_hat / (jnp.sqrt(v_hat) + ADAM_EPS) + ADAM_WD * w), nm, nv


def _adamw(w, g, m, v, *, name):
    r, c = w.shape
    t = _row_tile(r, c)

    def body(w_ref, g_ref, m_ref, v_ref, d_ref, nm_ref, nv_ref, go_ref):
        gv = g_ref[...]
        d_ref[...], nm_ref[...], nv_ref[...] = _adamw_math(w_ref[...], gv, m_ref[...], v_ref[...])
        go_ref[...] = gv

    blk = pl.BlockSpec((t, c), lambda i: (i, 0))
    sds = jax.ShapeDtypeStruct((r, c), F32)
    return pl.pallas_call(body, name=name, grid=(r // t,), in_specs=[blk] * 4, out_specs=[blk] * 4,
                          out_shape=[sds] * 4, compiler_params=_cp("parallel"))(w, g, m, v)


HBM = pl.BlockSpec(memory_space=pl.ANY)


def _place():
    x, y, c = lax.axis_index("x"), lax.axis_index("y"), lax.axis_index("c")
    chips = [(1 - x, y), (x, 1 - y), (1 - x, 1 - y)]
    return x, y, c, 2 * x + y, (x, y, 1 - c), chips


def _rcopy(src, dst, ssem, rsem, dev):
    return pltpu.make_async_remote_copy(src_ref=src, dst_ref=dst, send_sem=ssem, recv_sem=rsem, device_id=dev,
                                        device_id_type=MESH)


HBM_ONLY = pl.BlockSpec(memory_space=pltpu.HBM)
SEM = pl.BlockSpec(memory_space=pltpu.SEMAPHORE)


def _peers():
    x, y, c = lax.axis_index("x"), lax.axis_index("y"), lax.axis_index("c")
    out = []
    for mask in range(1, 8):
        px = 1 - x if mask & 4 else x
        py = 1 - y if mask & 2 else y
        pc = 1 - c if mask & 1 else c
        out.append(((px, py, pc), 2 * px + py, pc, 4 * px + 2 * py + pc))
    return 4 * x + 2 * y + c, out


def _reduce_start(grads, lands, after, *, name, whole=False):
    nt = len(grads)

    def body(*refs):
        ssems, rsems = refs[2 * nt + 1:3 * nt + 1], refs[3 * nt + 1:4 * nt + 1]
        g_out, l_out, token = refs[4 * nt + 1:5 * nt + 1], refs[5 * nt + 1:6 * nt + 1], refs[6 * nt + 1]
        me, peers = _peers()
        for t in range(nt):
            for k, (dev, chip, core, _) in enumerate(peers):
                src = g_out[t] if whole else g_out[t].at[chip, core]
                _rcopy(src, l_out[t].at[me], ssems[t].at[k], rsems[t].at[k], dev).start()
        token[...] = jnp.zeros_like(token)

    sems = [pltpu.SemaphoreType.DMA((7,))] * (2 * nt)
    out_shape = (sems + [pltpu.HBM(g.shape, g.dtype) for g in grads] + [pltpu.HBM(l.shape, l.dtype) for l in lands]
                 + [jax.ShapeDtypeStruct((8, LANES), F32)])
    res = pl.pallas_call(
        body, name=name, in_specs=[HBM_ONLY] * (2 * nt + 1),
        out_specs=[SEM] * (2 * nt) + [HBM_ONLY] * (2 * nt) + [pl.BlockSpec(memory_space=pltpu.VMEM)], out_shape=out_shape,
        input_output_aliases={t: 2 * nt + t for t in range(2 * nt)},
        compiler_params=pltpu.CompilerParams(has_side_effects=DATAFLOW),
    )(*[pltpu.with_memory_space_constraint(a, pltpu.HBM) for a in list(grads) + list(lands) + [after]])
    return res[:nt], res[nt:2 * nt], res[2 * nt:3 * nt], res[3 * nt:4 * nt], res[4 * nt]


def _reduce_wait(grads, lands, ssems, rsems, after, *, name, whole=False):
    nt = len(grads)

    def body(*refs):
        ssem_refs, rsem_refs = refs[2 * nt:3 * nt], refs[3 * nt:4 * nt]
        g_out, l_out = refs[4 * nt + 1:5 * nt + 1], refs[5 * nt + 1:6 * nt + 1]
        me, peers = _peers()
        for t in range(nt):
            for k, (dev, chip, core, _) in enumerate(peers):
                src = g_out[t] if whole else g_out[t].at[chip, core]
                _rcopy(src, l_out[t].at[me], ssem_refs[t].at[k], rsem_refs[t].at[k], dev).wait_send()
        for t in range(nt):
            for k, (dev, _, _, idx) in enumerate(peers):
                slot = l_out[t].at[idx]
                _rcopy(slot, slot, ssem_refs[t].at[k], rsem_refs[t].at[k], dev).wait_recv()

    res = pl.pallas_call(
        body, name=name, in_specs=[HBM_ONLY] * (2 * nt) + [SEM] * (2 * nt) + [HBM_ONLY], out_specs=[HBM_ONLY] * (2 * nt),
        out_shape=[pltpu.HBM(a.shape, a.dtype) for a in list(grads) + list(lands)],
        input_output_aliases={t: t for t in range(2 * nt)},
        compiler_params=pltpu.CompilerParams(has_side_effects=DATAFLOW),
    )(*grads, *lands, *ssems, *rsems, pltpu.with_memory_space_constraint(after, pltpu.HBM))
    return list(res[:nt]), list(res[nt:])
DATAFLOW = pltpu.SideEffectType.DATAFLOW_SIDE_EFFECTING


def _gather_now(bufs, *, name):
    nt = len(bufs)

    def body(*refs):
        outs = refs[nt:2 * nt]
        ssem, rsem = refs[2 * nt:]
        x, y, c, q, sib, chips = _place()
        sends = []
        for t in range(nt):
            for j, (px, py) in enumerate(chips):
                mine = outs[t].at[q, c]
                cp = _rcopy(mine, mine, ssem.at[t, j], rsem.at[t, j], (px, py, c))
                cp.start()
                sends.append(cp)
        for t in range(nt):
            for j, (px, py) in enumerate(chips):
                landed = outs[t].at[2 * px + py, c]
                _rcopy(landed, landed, ssem.at[t, j], rsem.at[t, j], (px, py, c)).wait_recv()
                cp = _rcopy(landed, landed, ssem.at[t, 3 + j], rsem.at[t, 3 + j], sib)
                cp.start()
                sends.append(cp)
        for t in range(nt):
            for j, (px, py) in enumerate(chips):
                passed = outs[t].at[2 * px + py, 1 - c]
                _rcopy(passed, passed, ssem.at[t, 3 + j], rsem.at[t, 3 + j], sib).wait_recv()
        for cp in sends:
            cp.wait_send()

    out_shape = [jax.ShapeDtypeStruct(b.shape, b.dtype) for b in bufs]
    return pl.pallas_call(
        body, name=name, in_specs=[HBM] * nt, out_specs=[HBM] * nt, out_shape=out_shape,
        input_output_aliases={t: t for t in range(nt)},
        scratch_shapes=[pltpu.SemaphoreType.DMA((nt, 6)), pltpu.SemaphoreType.DMA((nt, 6))],
    )(*bufs)


def _gather_start(bufs, half, after, *, name):
    nt = len(bufs)

    def body(*refs):
        ssems, rsems, outs = refs[nt + 1:2 * nt + 1], refs[2 * nt + 1:3 * nt + 1], refs[3 * nt + 1:4 * nt + 1]
        x, y, c, q, sib, chips = _place()
        for t in range(nt):
            for j, (px, py) in enumerate(chips):
                mine = outs[t].at[q, c] if half[t] else outs[t].at[q]
                _rcopy(mine, mine, ssems[t].at[j], rsems[t].at[j], (px, py, c)).start()

    sems = [pltpu.SemaphoreType.DMA((3,))] * (2 * nt)
    out_shape = sems + [pltpu.HBM(b.shape, b.dtype) for b in bufs]
    res = pl.pallas_call(
        body, name=name, in_specs=[HBM_ONLY] * (nt + 1), out_specs=[SEM] * (2 * nt) + [HBM_ONLY] * nt, out_shape=out_shape,
        input_output_aliases={t: 2 * nt + t for t in range(nt)},
        compiler_params=pltpu.CompilerParams(has_side_effects=DATAFLOW),
    )(*[pltpu.with_memory_space_constraint(b, pltpu.HBM) for b in list(bufs) + [after]])
    return res[:nt], res[nt:2 * nt], res[2 * nt:]


def _gather_wait(bufs, half, ssems, rsems, after, *, name):
    nt = len(bufs)

    def body(*refs):
        ssem_refs, rsem_refs = refs[nt:2 * nt], refs[2 * nt:3 * nt]
        outs = refs[3 * nt + 1:]
        x, y, c, q, sib, chips = _place()
        for t in range(nt):
            for j, (px, py) in enumerate(chips):
                mine = outs[t].at[q, c] if half[t] else outs[t].at[q]
                _rcopy(mine, mine, ssem_refs[t].at[j], rsem_refs[t].at[j], (px, py, c)).wait_send()
        for t in range(nt):
            for j, (px, py) in enumerate(chips):
                theirs = outs[t].at[2 * px + py, c] if half[t] else outs[t].at[2 * px + py]
                _rcopy(theirs, theirs, ssem_refs[t].at[j], rsem_refs[t].at[j], (px, py, c)).wait_recv()

    res = pl.pallas_call(
        body, name=name, in_specs=[HBM_ONLY] * nt + [SEM] * (2 * nt) + [HBM], out_specs=[HBM_ONLY] * nt,
        out_shape=[pltpu.HBM(b.shape, b.dtype) for b in bufs], input_output_aliases={t: t for t in range(nt)},
        compiler_params=pltpu.CompilerParams(has_side_effects=DATAFLOW),
    )(*bufs, *ssems, *rsems, after)
    return list(res)


def _sibling_swap(bufs, *, name):
    nt = len(bufs)

    def body(*refs):
        outs = refs[nt:2 * nt]
        ssem, rsem = refs[2 * nt:]
        x, y, c, q, sib, chips = _place()
        sends = []
        for t in range(nt):
            for j, (px, py) in enumerate(chips):
                held = outs[t].at[2 * px + py, c]
                cp = _rcopy(held, held, ssem.at[t, j], rsem.at[t, j], sib)
                cp.start()
                sends.append(cp)
        for t in range(nt):
            for j, (px, py) in enumerate(chips):
                other = outs[t].at[2 * px + py, 1 - c]
                _rcopy(other, other, ssem.at[t, j], rsem.at[t, j], sib).wait_recv()
        for cp in sends:
            cp.wait_send()

    return pl.pallas_call(
        body, name=name, in_specs=[HBM] * nt, out_specs=[HBM] * nt,
        out_shape=[jax.ShapeDtypeStruct(b.shape, b.dtype) for b in bufs], input_output_aliases={t: t for t in range(nt)},
        scratch_shapes=[pltpu.SemaphoreType.DMA((nt, 3)), pltpu.SemaphoreType.DMA((nt, 3))],
    )(*bufs)


def _sibling_share(bufs, layout, *, name):
    no = len(bufs)
    nt = len(layout)

    def body(*refs):
        outs = refs[no:2 * no]
        ssem, rsem = refs[2 * no:]
        x, y, c, q, sib, chips = _place()

        def slot(t, half):
            o, lead = layout[t]
            return outs[o].at[half] if lead is None else outs[o].at[lead, half]

        sends = []
        for t in range(nt):
            cp = _rcopy(slot(t, c), slot(t, c), ssem.at[t], rsem.at[t], sib)
            cp.start()
            sends.append(cp)
        for t in range(nt):
            _rcopy(slot(t, 1 - c), slot(t, 1 - c), ssem.at[t], rsem.at[t], sib).wait_recv()
        for cp in sends:
            cp.wait_send()

    out_shape = [jax.ShapeDtypeStruct(b.shape, b.dtype) for b in bufs]
    return pl.pallas_call(
        body, name=name, in_specs=[HBM] * no, out_specs=[HBM] * no, out_shape=out_shape,
        input_output_aliases={o: o for o in range(no)},
        scratch_shapes=[pltpu.SemaphoreType.DMA((nt,)), pltpu.SemaphoreType.DMA((nt,))],
    )(*bufs)


SMALL_GROUP_OF = (0,) * 6 + (1,) * 2 + (2,) * 2 + (3,) * 4 + (4,) * 9


def _small_pack(local, dims, which, *, name):
    kw, wa, ng, nqkv, nsk, f, dm = dims
    shapes = _small_shapes(dims)
    row_vec = 8 * (-(-kw // 8))
    pos = sorted(local)
    assert all(SMALL_GROUP_OF[p] in which for p in pos)

    def pack_body(*refs):
        loc = dict(zip(pos, refs[:len(pos)]))
        grp = dict(zip(which, refs[len(pos):]))
        for gr in grp.values():
            gr[...] = jnp.zeros_like(gr)
        if 0 in which:
            grp[0][0:kw, :] = loc[0][...]
            for k in range(5):
                grp[0][row_vec + k:row_vec + k + 1, :] = loc[1 + k][...]
        if 1 in which:
            for g in range(ng):
                grp[1][g * B_CHUNK:(g + 1) * B_CHUNK, :] = loc[6][g]
            grp[1][ng * B_CHUNK:ng * B_CHUNK + ng, :] = loc[7][...]
        if 2 in which:
            grp[2][0:1, :] = loc[8][...]
            grp[2][1:2, 0:nsk] = loc[9][...]
        if 3 in which:
            for l in range(2):
                for s in range(2):
                    grp[3][l, s, 0:3, :] = loc[10 + 2 * l][s]
                    grp[3][l, s, 3:4, :] = loc[11 + 2 * l][s]
        if 4 in which:
            for k in range(9):
                grp[4][k:k + 1, :] = loc[14 + k][...]

    vm = pl.BlockSpec(memory_space=pltpu.VMEM)
    return pl.pallas_call(
        pack_body, name=name, in_specs=[vm] * len(pos), out_specs=[vm] * len(which),
        out_shape=[jax.ShapeDtypeStruct(shapes[g], F32) for g in which],
        compiler_params=pltpu.CompilerParams(vmem_limit_bytes=VMEM_LIMIT),
    )(*[local[p] for p in pos])


def _small_shapes(dims):
    kw, wa, ng, nqkv, nsk, f, dm = dims
    return [(8 * (-(-kw // 8)) + 8, wa), (ng * B_CHUNK + 8, B_CHUNK), (8, nqkv), (2, 2, 8, f), (16, dm)]


def _small_update(groups, landed, params, dims, *, name):
    kw, wa, ng, nqkv, nsk, f, dm = dims
    shapes = _small_shapes(dims)
    row_vec = 8 * (-(-kw // 8))
    n_grp = len(shapes)
    flat_params = [a for triple in params for a in triple]
    n_par = len(params)
    vm = pl.BlockSpec(memory_space=pltpu.VMEM)

    def adamw_body(*refs):
        own = refs[:n_grp]
        land = refs[n_grp:2 * n_grp]
        par = refs[2 * n_grp:2 * n_grp + 3 * n_par]
        outs = refs[2 * n_grp + 3 * n_par:2 * n_grp + 7 * n_par]
        loss_ref = refs[2 * n_grp + 7 * n_par]
        tot = refs[2 * n_grp + 7 * n_par + 1:]
        x, y = lax.axis_index("x"), lax.axis_index("y")
        q = 2 * x + y
        me = 4 * x + 2 * y + lax.axis_index("c")
        for gi in range(n_grp):
            acc = None
            for dv in range(8):
                term = jnp.where(me == dv, own[gi][...], land[gi][dv])
                acc = term if acc is None else acc + term
            tot[gi][...] = acc
        ta, tb, tc, td, te = tot

        def mine(piece):
            out = piece(0)
            for k in range(1, 4):
                out = jnp.where(q == k, piece(k), out)
            return out

        def update(p, grad, index=None):
            at = (lambda r: r[...]) if index is None else (lambda r: r[index])
            w_ref, m_ref, v_ref = par[3 * p:3 * p + 3]
            g_ref, d_ref, nm_ref, nv_ref = outs[4 * p:4 * p + 4]
            delta, nm, nv = _adamw_math(at(w_ref), grad, at(m_ref), at(v_ref))
            for r, val in ((g_ref, grad), (d_ref, delta), (nm_ref, nm), (nv_ref, nv)):
                if index is None:
                    r[...] = val
                else:
                    r[index] = val

        wq = wa // 4
        update(0, mine(lambda k: ta[0:kw, k * wq:(k + 1) * wq]), (0,))
        for k in range(5):
            update(1 + k, ta[row_vec + k:row_vec + k + 1, :])
        for g in range(ng):
            update(6, tb[g * B_CHUNK:(g + 1) * B_CHUNK, :], (0, g))
        update(7, tb[ng * B_CHUNK:ng * B_CHUNK + ng, :], (0,))
        nq4 = nqkv // 4
        update(8, mine(lambda k: tc[0:1, k * nq4:(k + 1) * nq4]))
        update(9, tc[1:2, 0:nsk])
        fh = f // 2
        for l in range(2):
            update(10, mine(lambda k: td[l, k // 2, 0:3, (k % 2) * fh:(k % 2 + 1) * fh]), (l,))
            update(11, jnp.concatenate([td[l, 0, 3:4, :], td[l, 1, 3:4, :]], axis=1), (slice(l, l + 1),))
        dq4 = dm // 4
        for i in range(2):
            for j in range(2):
                for p, base in ((12, 0), (13, 4)):
                    row = base + 2 * i + j
                    update(p, mine(lambda k: te[row:row + 1, k * dq4:(k + 1) * dq4]), (i, slice(j, j + 1)))
        loss_ref[...] = (0.5 / dm) * jnp.sum(te[8:9, :], axis=1, keepdims=True)

    out_shape = []
    for w, _, _ in params:
        out_shape += [jax.ShapeDtypeStruct(w.shape, F32)] * 4
    out_shape.append(jax.ShapeDtypeStruct((1, 1), F32))
    res = pl.pallas_call(
        adamw_body, name=name + "_adamw", in_specs=[vm] * (2 * n_grp + 3 * n_par), out_specs=[vm] * len(out_shape),
        out_shape=out_shape, scratch_shapes=[pltpu.VMEM(s, F32) for s in shapes],
        compiler_params=pltpu.CompilerParams(vmem_limit_bytes=VMEM_LIMIT),
    )(*groups, *landed, *flat_params)
    return [res[4 * p:4 * p + 4] for p in range(n_par)], res[-1]


def _pack(arrays, rows_multiple):
    flat = jnp.concatenate([a.reshape(-1) for a in arrays])
    rows = -(-flat.shape[0] // LANES)
    rows = -(-rows // rows_multiple) * rows_multiple
    flat = jnp.pad(flat, (0, rows * LANES - flat.shape[0]))
    return flat.reshape(rows, LANES)


def _unshard_cols(stacked):
    moved = jnp.moveaxis(stacked, 0, -2)
    return moved.reshape(moved.shape[:-2] + (4 * stacked.shape[-1],))


def kernel(x, ab_w_in, a_conv_w, a_conv_b, a_norm_g, a_norm_b, b_norm_g, b_norm_b, b_spatial_w, b_spatial_b, ab_w_out, c_w_qkv, c_b_qkv, c_sinks, c_w_o, ffn_w_up, ffn_conv_w, ffn_conv_b, ffn_w_down, ln_g, ln_b, loss_target, m_ab_w_in, m_a_conv_w, m_a_conv_b, m_a_norm_g, m_a_norm_b, m_b_norm_g, m_b_norm_b, m_b_spatial_w, m_b_spatial_b, m_ab_w_out, m_c_w_qkv, m_c_b_qkv, m_c_sinks, m_c_w_o, m_ffn_w_up, m_ffn_conv_w, m_ffn_conv_b, m_ffn_w_down, m_ln_g, m_ln_b, v_ab_w_in, v_a_conv_w, v_a_conv_b, v_a_norm_g, v_a_norm_b, v_b_norm_g, v_b_norm_b, v_b_spatial_w, v_b_spatial_b, v_ab_w_out, v_c_w_qkv, v_c_b_qkv, v_c_sinks, v_c_w_o, v_ffn_w_up, v_ffn_conv_w, v_ffn_conv_b, v_ffn_w_down, v_ln_g, v_ln_b):
    rows, d = x.shape[1], x.shape[2]
    depth = ln_g.shape[0]
    assert depth == 2 and x.shape[0] == 1
    alpha = (2.0 * depth) ** 0.25
    f = ffn_w_down.shape[1] * 4
    n_q = c_sinks.shape[1]
    q_idx = 2 * lax.axis_index("x") + lax.axis_index("y")
    c_idx = lax.axis_index("c")
    xs, tgt = x[0], loss_target[0]

    def own_slot(part):
        buf = lax.empty((4,) + part.shape, part.dtype)
        return lax.dynamic_update_slice(buf, part[None], (q_idx, 0, 0, 0))

    def halves(wm):
        return own_slot(wm.astype(BF16).reshape((2, wm.shape[0] // 2) + wm.shape[1:]))

    small_sharded = [a_conv_w[0], c_b_qkv[0], ffn_conv_w, ln_g, ln_b]
    small_pack = _pack(small_sharded, 16)
    bufs = [halves(ab_w_in[0]), own_slot(small_pack.reshape(2, small_pack.shape[0] // 2, LANES)), halves(ab_w_out[0]),
            halves(ffn_w_up[0]), halves(ffn_w_down[0]), halves(c_w_qkv[0]), halves(c_w_o[0]),
            halves(ffn_w_up[1]), halves(ffn_w_down[1])]
    whole = lambda g: g.reshape(4, 2 * g.shape[2], g.shape[3])
    n_now = 2
    first_two = _gather_now(bufs[:n_now], name="gather_now")
    w_in, small_all = [whole(g) for g in first_two]
    later = bufs[n_now:]
    half = [True, True] + [False] * (len(later) - 2)
    ssems, rsems, started = _gather_start(later, half, first_two[1], name="gather_start")

    def arrive(idx, after, tag):
        idx = [i - n_now for i in idx]
        halved = [half[i] for i in idx]
        got = _gather_wait([started[i] for i in idx], halved, [ssems[i] for i in idx], [rsems[i] for i in idx], after,
                           name=f"gather_wait_{tag}")
        if all(halved):
            got = _sibling_swap(got, name=f"gather_swap_{tag}")
        return [whole(g) for g in got]

    small_all = small_all.reshape(4, -1)
    sh_shapes = [s.shape for s in small_sharded]
    pieces, pos = [], 0
    for s in sh_shapes:
        n = math.prod(s)
        pieces.append(_unshard_cols(small_all[:, pos:pos + n].reshape((4,) + s)))
        pos += n
    conv_w_a, b_qkv, conv_w_f, ln_gf, ln_bf = pieces

    tril = jnp.tril(jnp.ones((B_CHUNK, B_CHUNK), F32))
    ws = (b_spatial_w[0] * tril).astype(BF16)
    wst = jnp.swapaxes(ws, 1, 2)
    sbb = jnp.broadcast_to(b_spatial_b[0][:, :, None], b_spatial_w[0].shape)
    mix_vecs = [a_conv_b, a_norm_g, a_norm_b, b_norm_g, b_norm_b]
    cw_f = [jnp.swapaxes(conv_w_f[l].reshape(3, 2, f), 0, 1) for l in range(depth)]
    cb_f = [ffn_conv_b[l].reshape(2, 1, f) for l in range(depth)]
    lng = lambda i, j: ln_gf[i, j].reshape(1, d)
    lnb = lambda i, j: ln_bf[i, j].reshape(1, d)
    sinks = c_sinks[0]

    w_up, w_down = [None, None], [None, None]

    def ffn_fwd(xin, l):
        w_up[l], = arrive([3 + 4 * l], xin, f"up{l}")
        hf, fact = _ffn_up_fwd(xin, w_up[l], cw_f[l], cb_f[l], name=f"ffn{l}_up")
        w_down[l] = arrive([4 + 4 * l], fact, f"down{l}")[0].reshape(-1, d)
        return hf, fact

    h0, ab, a2 = _mixer_fwd(xs, w_in, conv_w_a, *mix_vecs, ws, sbb, name="mix_fwd")
    w_out = arrive([2], ab, "out")[0].reshape(-1, d)
    mix, x1 = _proj_add_ln(ab, w_out, xs, lng(0, 0), lnb(0, 0), alpha, name="mix_out_ln00")
    hf0, f0 = ffn_fwd(x1, 0)
    ffn0, x2 = _proj_add_ln(f0, w_down[0], x1, lng(0, 1), lnb(0, 1), alpha, name="ffn0_down_ln01")
    w_qkv = _unshard_cols(arrive([5], x2, "qkv")[0])
    qkv = _matmul(x2, w_qkv, name="att_qkv", tm=1024, tn=w_qkv.shape[1], tk=1024, bias=b_qkv.reshape(1, -1))
    ao, lse = _attn_fwd(qkv, sinks, name="att_core")
    w_o = arrive([6], ao, "o")[0].reshape(-1, d)
    att, x3 = _proj_add_ln(ao, w_o, x2, lng(1, 0), lnb(1, 0), alpha, name="att_out_ln10")
    hf1, f1 = ffn_fwd(x3, 1)
    ffn1, sq_err, dy = _proj_add_ln_loss(f1, w_down[1], x3, lng(1, 1), lnb(1, 1), tgt, alpha, name="ffn1_down_ln11_loss")

    def owner_view(g):
        if g.ndim == 3:
            return g.reshape(4, 2, g.shape[1] // 2, g.shape[2])
        return g.reshape(4, 2, g.shape[0] // 8, g.shape[1])

    in_flight = []

    def send_grads(tag, grads, after):
        lands = [lax.empty((8,) + g.shape[2:], BF16) for g in grads]
        ss, rs, g_thru, l_thru, token = _reduce_start(grads, lands, after, name=f"reduce_start_{tag}")
        in_flight.append((tag, g_thru, l_thru, ss, rs))
        return token[0:1, 0:1]

    def ffn_bwd(dz, xin, hf, fact, l):
        d_wdown = _matmul(fact, dz, name=f"ffn{l}_down_dw", ta=True, tm=1408, tn=1024, tk=2048, out_dtype=BF16)
        dfa = _matmul(dz, w_down[l], name=f"ffn{l}_down_dx", tb=True, tm=1024, tn=1408, tk=1024, out_dtype=BF16)
        dx_parts, d_wup, dcw, dcb = _ffn_up_bwd(hf, dfa, xin, w_up[l], cw_f[l], cb_f[l], name=f"ffn{l}_up_bwd")
        tok = send_grads(f"ffn{l}", [owner_view(d_wup), owner_view(d_wdown)], dcb)
        return [(dx_parts, 1.0), (dz, alpha)], dcw, dcb, tok

    dz, dg11, db11 = _add_ln_bwd([(dy, 1.0)], x3, ffn1, lng(1, 1), alpha, name="ln11_bwd")
    dx3, dcw1, dcb1, tok = ffn_bwd(dz, x3, hf1, f1, 1)
    dz, dg10, db10 = _add_ln_bwd(dx3, x2, att, lng(1, 0) + tok, alpha, name="ln10_bwd")
    d_wo = _matmul(ao, dz, name="att_out_dw", ta=True, tm=1024, tn=1024, tk=1024, out_dtype=BF16)
    dao = _matmul(dz, w_o, name="att_out_dx", tb=True, tm=1024, tn=1024, tk=1024)
    dq, dkc, dkp, d_sinks = _attn_bwd(qkv, dao, lse, sinks, name="att_core_bwd")
    dqkv, d_bqkv = _dqkv_assemble(dq, dkc, dkp, name="att_dqkv")
    d_wqkv = _matmul(x2, dqkv, name="att_qkv_dw", ta=True, tm=1024, tn=dqkv.shape[1], tk=1024, out_dtype=BF16)
    d_wqkv_st = jnp.moveaxis(d_wqkv.reshape(d_wqkv.shape[0], 4, -1), 1, 0)
    tok = send_grads("att", [owner_view(d_wqkv_st), owner_view(d_wo)], d_bqkv)
    dx2 = _matmul(dqkv, w_qkv, name="att_qkv_dx", tb=True, tm=1024, tn=1024, tk=dqkv.shape[1], addend=(dz, alpha))
    dz, dg01, db01 = _add_ln_bwd([(dx2, 1.0)], x1, ffn0, lng(0, 1) + tok, alpha, name="ln01_bwd")
    dx1, dcw0, dcb0, tok = ffn_bwd(dz, x1, hf0, f0, 0)
    dz, dg00, db00 = _add_ln_bwd(dx1, xs, mix, lng(0, 0) + tok, alpha, name="ln00_bwd")
    d_wout = _matmul(ab, dz, name="mix_out_dw", ta=True, tm=1024, tn=1024, tk=1024, out_dtype=BF16)
    dab = _matmul(dz, w_out, name="mix_out_dx", tb=True, tm=1024, tn=1024, tk=1024)
    small_dims = (a_conv_w.shape[1], a_conv_b.shape[1], b_spatial_w.shape[1], 4 * c_b_qkv.shape[1], n_q, f, d)

    def send_small(local_arrays, which, tag, after):
        groups = _small_pack(local_arrays, small_dims, which, name=f"small_pack_{tag}")
        lands = [lax.empty((8,) + g.shape, F32) for g in groups]
        return _reduce_start(groups, lands, after, name=f"small_start_{tag}", whole=True)

    ready = [d_bqkv, d_sinks, dcw0, dcb0, dcw1, dcb1, dg00, dg01, dg10, dg11, db00, db01, db10, db11, sq_err]
    ss_e, rs_e, g_e, l_e, token = send_small(dict(zip(range(8, 23), ready)), (2, 3, 4), "early", dab)
    tok = send_grads("out", [owner_view(d_wout)], token)
    grad_x, d_win, d_cwa, d_cba, d_ga, d_ba, d_gb, d_bb, d_ws, d_sb = _mixer_bwd(
        h0, a2, dab, xs, w_in, dz, alpha, conv_w_a, *mix_vecs[1:], ws, wst, sbb, tril + tok, name="mix_bwd")

    small_w = [a_conv_w, a_conv_b, a_norm_g, a_norm_b, b_norm_g, b_norm_b, b_spatial_w, b_spatial_b, c_b_qkv,
               c_sinks, ffn_conv_w, ffn_conv_b, ln_g, ln_b]
    small_m = [m_a_conv_w, m_a_conv_b, m_a_norm_g, m_a_norm_b, m_b_norm_g, m_b_norm_b, m_b_spatial_w, m_b_spatial_b,
               m_c_b_qkv, m_c_sinks, m_ffn_conv_w, m_ffn_conv_b, m_ln_g, m_ln_b]
    small_v = [v_a_conv_w, v_a_conv_b, v_a_norm_g, v_a_norm_b, v_b_norm_g, v_b_norm_b, v_b_spatial_w, v_b_spatial_b,
               v_c_b_qkv, v_c_sinks, v_ffn_conv_w, v_ffn_conv_b, v_ln_g, v_ln_b]
    place = jnp.stack([q_idx, c_idx, 4 * lax.axis_index("x") + 2 * lax.axis_index("y") + c_idx]).astype(jnp.int32)
    where = {"mix": [(0, None)], "out": [(1, None)], "att": [(2, None), (3, None)], "ffn0": [(4, 0), (5, 0)],
             "ffn1": [(4, 1), (5, 1)]}
    big_w = [ab_w_in, ab_w_out, c_w_qkv, c_w_o, ffn_w_up, ffn_w_down]
    big_m = [m_ab_w_in, m_ab_w_out, m_c_w_qkv, m_c_w_o, m_ffn_w_up, m_ffn_w_down]
    big_v = [v_ab_w_in, v_ab_w_out, v_c_w_qkv, v_c_w_o, v_ffn_w_up, v_ffn_w_down]
    big_out = [None] * 6

    def finish(tags, after, label):
        bufs, layout = {}, []
        for tag, g_thru, l_thru, ss, rs in in_flight:
            if tag not in tags:
                continue
            own, landed = _reduce_wait(g_thru, l_thru, ss, rs, after, name=f"reduce_wait_{tag}")
            for k, (o, lead) in enumerate(where[tag]):
                piece = own[k].shape[2:]
                shape = (2,) + piece if lead is None else (2, 2) + piece
                bufs[o] = _octo_sum(own[k], landed[k], place, bufs.get(o), (lead, shape), name=f"reduce_sum_{tag}{k}")
                layout.append((o, lead))
        order = sorted(bufs)
        shared = _sibling_share([bufs[o] for o in order], [(order.index(o), lead) for o, lead in layout],
                                name=f"reduce_share_{label}")
        for o, g in zip(order, shared):
            w = big_w[o]
            two_d = lambda a: a.reshape(-1, a.shape[-1])
            outs = _adamw(two_d(w), two_d(g), two_d(big_m[o]), two_d(big_v[o]), name=f"adamw_big{o}")
            big_out[o] = [r.reshape(w.shape) for r in outs]
        return big_out[order[-1]][0]

    mixer_small = [d_cwa, d_cba, d_ga, d_ba, d_gb, d_bb, d_ws, d_sb]
    ss_l, rs_l, g_l, l_l, token = send_small(dict(zip(range(8), mixer_small)), (0, 1), "late", grad_x)
    tok = send_grads("mix", [owner_view(d_win)], after=token)
    done = finish(("ffn1", "att", "ffn0", "out"), d_ws + tok, "early")
    g_l, l_l = _reduce_wait(g_l, l_l, ss_l, rs_l, done, name="small_wait_late", whole=True)
    g_e, l_e = _reduce_wait(g_e, l_e, ss_e, rs_e, done, name="small_wait_early", whole=True)
    small_out, loss = _small_update(g_l + g_e, l_l + l_e, list(zip(small_w, small_m, small_v)), small_dims,
                                    name="small_tail")
    loss = loss[0, 0]
    small_g = [o[0] for o in small_out]
    sm_delta = [o[1] for o in small_out]
    sm_m = [o[2] for o in small_out]
    sm_v = [o[3] for o in small_out]
    finish(("mix",), sm_delta[6], "mix")

    order_big = {0: 0, 9: 1, 10: 2, 13: 3, 14: 4, 17: 5}
    order_small = {1: 0, 2: 1, 3: 2, 4: 3, 5: 4, 6: 5, 7: 6, 8: 7, 11: 8, 12: 9, 15: 10, 16: 11, 18: 12, 19: 13}
    grads, deltas, new_m, new_v = [], [], [], []
    for pos_w in range(20):
        if pos_w in order_big:
            t = order_big[pos_w]
            grads.append(big_out[t][3])
            deltas.append(big_out[t][0])
            new_m.append(big_out[t][1])
            new_v.append(big_out[t][2])
        else:
            t = order_small[pos_w]
            grads.append(small_g[t])
            deltas.append(sm_delta[t])
            new_m.append(sm_m[t])
            new_v.append(sm_v[t])
    return (loss, grad_x[None], *grads, *deltas, *new_m, *new_v)
```

```python
import math

import jax
import jax.numpy as jnp
from jax import lax
from jax.experimental import pallas as pl
from jax.experimental.pallas import tpu as pltpu

F32 = jnp.float32
BF16 = jnp.bfloat16
MESH = pl.DeviceIdType.MESH

LN_EPS = 1e-5
HEAD_DIM = 64
ATT_BLOCK = 128
Q_PER_KV = 8
A_KERNEL = 31
CONV_HALO = 32
FFN_HALO = 8
BF16_ROWS = 16
B_CHUNK = 128
LANES = 128
MXU_WIDTH = 256
GELU_C = math.sqrt(2.0 / math.pi)
ADAM_LR = 0.001
ADAM_B1 = 0.9
ADAM_B2 = 0.999
ADAM_EPS = 1e-08
ADAM_WD = 0.01
ADAM_STEP = 10
VMEM_LIMIT = 56 * 1024 * 1024


def _cp(*dims):
    return pltpu.CompilerParams(dimension_semantics=dims, vmem_limit_bytes=VMEM_LIMIT)


def _pick(n, prefs):
    for p in prefs:
        if n % p == 0:
            return p
    return n


def _sig(x):
    return 1.0 / (1.0 + jnp.exp(-x))


def _gelu(x):
    t = jnp.tanh(GELU_C * (x + 0.044715 * (x * x * x)))
    return x * (0.5 * (1.0 + t)), t


def _gelu_grad(x, t):
    return 0.5 * (1.0 + t) + 0.5 * x * (1.0 - t * t) * (GELU_C * (1.0 + 3.0 * 0.044715 * x * x))


def _ln_stats(z):
    mu = jnp.mean(z, axis=-1, keepdims=True)
    zc = z - mu
    var = jnp.mean(zc * zc, axis=-1, keepdims=True)
    rstd = lax.rsqrt(var + LN_EPS)
    return zc * rstd, rstd


def _ln_bwd(dxh, xh, rstd):
    return rstd * (dxh - jnp.mean(dxh, axis=-1, keepdims=True) - xh * jnp.mean(dxh * xh, axis=-1, keepdims=True))


def _rowsum(a):
    return jnp.sum(a, axis=0, keepdims=True)


def _lshape(a):
    return (a.shape[0], a.shape[1]) if a.ndim == 2 else (a.shape[1], a.shape[0] * a.shape[2])


def _spec2(arr, blk_r, blk_c, ridx, cidx):
    if len(arr.shape) == 2:
        return pl.BlockSpec((blk_r, blk_c), lambda i, j, k: (ridx(i, j, k), cidx(i, j, k)))
    per = arr.shape[2] // blk_c
    assert arr.shape[2] % blk_c == 0
    return pl.BlockSpec((None, blk_r, blk_c), lambda i, j, k: (cidx(i, j, k) // per, ridx(i, j, k), cidx(i, j, k) % per))


def _matmul(a, b, *, name, ta=False, tb=False, tm, tn, tk, out_dtype=F32, out_stack=None, bias=None, addend=None):
    ar, ac = _lshape(a)
    br, bc = _lshape(b)
    m, kdim = (ac, ar) if ta else (ar, ac)
    n = br if tb else bc
    assert (bc if tb else br) == kdim
    tm, tn, tk = min(tm, m), min(tn, n), min(tk, kdim)
    assert m % tm == 0 and n % tn == 0 and kdim % tk == 0, (name, m, n, kdim, tm, tn, tk)
    nk = kdim // tk
    gi, gj, gk = (lambda i, j, k: i), (lambda i, j, k: j), (lambda i, j, k: k)
    a_spec = _spec2(a, tk, tm, gk, gi) if ta else _spec2(a, tm, tk, gi, gk)
    b_spec = _spec2(b, tn, tk, gj, gk) if tb else _spec2(b, tk, tn, gk, gj)
    if out_stack is None:
        out_sds = jax.ShapeDtypeStruct((m, n), out_dtype)
    else:
        out_sds = jax.ShapeDtypeStruct((out_stack, m, n // out_stack), out_dtype)
    o_spec = _spec2(out_sds, tm, tn, gi, gj)
    in_specs = [a_spec, b_spec]
    args = [a, b]
    if bias is not None:
        in_specs.append(pl.BlockSpec((1, tn), lambda i, j, k: (0, j)))
        args.append(bias)
    scale = None
    if addend is not None:
        add_arr, scale = addend
        in_specs.append(pl.BlockSpec((tm, tn), lambda i, j, k: (i, j)))
        args.append(add_arr)
    use_acc = nk > 1 and out_dtype != F32
    dn = (((0 if ta else 1,), (1 if tb else 0,)), ((), ()))

    def body(*refs):
        a_ref, b_ref = refs[0], refs[1]
        pos = 2
        bias_ref = add_ref = None
        if bias is not None:
            bias_ref = refs[pos]
            pos += 1
        if addend is not None:
            add_ref = refs[pos]
            pos += 1
        o_ref = refs[pos]
        acc_ref = refs[pos + 1] if use_acc else o_ref
        p = lax.dot_general(a_ref[...].astype(BF16), b_ref[...].astype(BF16), dn, preferred_element_type=F32)

        def finish(val):
            if bias_ref is not None:
                val = val + bias_ref[...]
            if add_ref is not None:
                val = val + scale * add_ref[...]
            return val.astype(out_dtype)

        if nk == 1:
            o_ref[...] = finish(p)
        else:
            k = pl.program_id(2)

            @pl.when(k == 0)
            def _():
                acc_ref[...] = p

            @pl.when(k > 0)
            def _():
                acc_ref[...] += p

            if use_acc or bias_ref is not None or add_ref is not None:
                @pl.when(k == nk - 1)
                def _():
                    o_ref[...] = finish(acc_ref[...])

    return pl.pallas_call(
        body, name=name, grid=(m // tm, n // tn, nk), in_specs=in_specs, out_specs=o_spec, out_shape=out_sds,
        scratch_shapes=[pltpu.VMEM((tm, tn), F32)] if use_acc else [],
        compiler_params=_cp("parallel", "parallel", "arbitrary"),
    )(*args)


def _proj_add_ln(a, w, x, g, b, alpha, *, name):
    rows, kdim = a.shape
    d = w.shape[1]
    tm = _pick(rows, (512, 256))

    def body(a_ref, w_ref, x_ref, g_ref, b_ref, s_ref, y_ref):
        s = jnp.dot(a_ref[...].astype(BF16), w_ref[...], preferred_element_type=F32)
        s_ref[...] = s
        xh, _ = _ln_stats(alpha * x_ref[...] + s)
        y_ref[...] = xh * g_ref[...] + b_ref[...]

    row = pl.BlockSpec((tm, d), lambda i: (i, 0))
    vec = pl.BlockSpec((1, d), lambda i: (0, 0))
    sds = jax.ShapeDtypeStruct((rows, d), F32)
    return pl.pallas_call(body, name=name, grid=(rows // tm,),
                          in_specs=[pl.BlockSpec((tm, kdim), lambda i: (i, 0)), pl.BlockSpec((kdim, d), lambda i: (0, 0)),
                                    row, vec, vec],
                          out_specs=[row, row], out_shape=[sds, sds], compiler_params=_cp("parallel"))(a, w, x, g, b)


def _add_ln_bwd(dy_terms, x, s, g, alpha, *, name):
    rows, d = x.shape
    t = _pick(rows, (512, 256))
    nterm = len(dy_terms)
    scales = [sc for _, sc in dy_terms]
    ranks = [a.ndim for a, _ in dy_terms]

    def body(*refs):
        dy_refs = refs[:nterm]
        x_ref, s_ref, g_ref, dz_ref, dg_ref, db_ref = refs[nterm:]

        @pl.when(pl.program_id(0) == 0)
        def _():
            dg_ref[...] = jnp.zeros_like(dg_ref)
            db_ref[...] = jnp.zeros_like(db_ref)

        dyv = None
        for r, sc, rank in zip(dy_refs, scales, ranks):
            slabs = [r[...]] if rank == 2 else [r[p] for p in range(r.shape[0])]
            for v in slabs:
                v = v if sc == 1.0 else sc * v
                dyv = v if dyv is None else dyv + v
        xh, rstd = _ln_stats(alpha * x_ref[...] + s_ref[...])
        dz_ref[...] = _ln_bwd(dyv * g_ref[...], xh, rstd)
        dg_ref[...] += _rowsum(dyv * xh)
        db_ref[...] += _rowsum(dyv)

    row = pl.BlockSpec((t, d), lambda i: (i, 0))
    vec = pl.BlockSpec((1, d), lambda i: (0, 0))
    vsds = jax.ShapeDtypeStruct((1, d), F32)
    dy_specs = [row if a.ndim == 2 else pl.BlockSpec((a.shape[0], t, d), lambda i: (0, i, 0)) for a, _ in dy_terms]
    return pl.pallas_call(body, name=name, grid=(rows // t,), in_specs=dy_specs + [row, row, vec], out_specs=[row, vec, vec],
                          out_shape=[jax.ShapeDtypeStruct((rows, d), F32), vsds, vsds],
                          compiler_params=_cp("arbitrary"))(*[a for a, _ in dy_terms], x, s, g)


def _add_ln_loss(x, s, g, b, tgt, alpha, *, name):
    rows, d = x.shape
    t = _pick(rows, (512, 256))

    def body(x_ref, s_ref, g_ref, b_ref, t_ref, l_ref, dy_ref):
        @pl.when(pl.program_id(0) == 0)
        def _():
            l_ref[...] = jnp.zeros_like(l_ref)

        xh, _ = _ln_stats(alpha * x_ref[...] + s_ref[...])
        e = (xh * g_ref[...] + b_ref[...]) - t_ref[...]
        l_ref[...] += _rowsum(e * e)
        dy_ref[...] = e * (1.0 / d)

    row = pl.BlockSpec((t, d), lambda i: (i, 0))
    vec = pl.BlockSpec((1, d), lambda i: (0, 0))
    return pl.pallas_call(body, name=name, grid=(rows // t,), in_specs=[row, row, vec, vec, row], out_specs=[vec, row],
                          out_shape=[jax.ShapeDtypeStruct((1, d), F32), jax.ShapeDtypeStruct((rows, d), F32)],
                          compiler_params=_cp("arbitrary"))(x, s, g, b, tgt)


def _col_blocks(width, step):
    return [slice(pos, min(pos + step, width)) for pos in range(0, width, step)]


def _conv3(e, w, b):
    r1 = pltpu.roll(e, 1, 0)
    r2 = pltpu.roll(e, 2, 0)
    return w[0:1, :] * r2 + w[1:2, :] * r1 + w[2:3, :] * e + b, (r2, r1, e)


def _ffn_up_fwd(x, w_up, cw, cb, *, name):
    rows, d = x.shape
    nq, _, tc = w_up.shape
    nj = nq // 2
    f = tc * nj
    tm = _pick(rows, (512, 256))
    blocks = _col_blocks(tc, tc)

    def body(x_ref, wg_ref, wv_ref, cw_ref, cb_ref, hf_ref, f_ref, prev_ref):
        @pl.when(pl.program_id(1) == 0)
        def _():
            prev_ref[...] = jnp.zeros_like(prev_ref)

        xb = x_ref[...].astype(BF16)
        for cs in blocks:
            hc = []
            for s, w_ref in ((0, wg_ref), (1, wv_ref)):
                h = jnp.dot(xb, w_ref[:, cs], preferred_element_type=F32)
                hf_ref[s, :, cs] = h
                e = jnp.concatenate([prev_ref[s, :, cs], h], axis=0)
                prev_ref[s, :, cs] = h[tm - FFN_HALO:]
                y, _ = _conv3(e, cw_ref[s, :, cs], cb_ref[s, :, cs])
                hc.append(y[FFN_HALO:])
            gl, _ = _gelu(hc[0])
            f_ref[:, cs] = (gl * hc[1]).astype(BF16)

    in_specs = [
        pl.BlockSpec((tm, d), lambda j, i: (i, 0)),
        pl.BlockSpec((None, d, tc), lambda j, i: (j, 0, 0)),
        pl.BlockSpec((None, d, tc), lambda j, i: (nj + j, 0, 0)),
        pl.BlockSpec((2, 3, tc), lambda j, i: (0, 0, j)),
        pl.BlockSpec((2, 1, tc), lambda j, i: (0, 0, j)),
    ]
    out_specs = [pl.BlockSpec((2, tm, tc), lambda j, i: (0, i, j)), pl.BlockSpec((tm, tc), lambda j, i: (i, j))]
    out_shape = [jax.ShapeDtypeStruct((2, rows, f), F32), jax.ShapeDtypeStruct((rows, f), BF16)]
    return pl.pallas_call(body, name=name, grid=(nj, rows // tm), in_specs=in_specs, out_specs=out_specs, out_shape=out_shape,
                          scratch_shapes=[pltpu.VMEM((2, FFN_HALO, tc), F32)],
                          compiler_params=_cp("parallel", "arbitrary"))(x, w_up, w_up, cw, cb)


def _ffn_up_bwd(hf, df, x, w_up, cw, cb, *, name):
    _, rows, f = hf.shape
    d = x.shape[1]
    nq, _, tc = w_up.shape
    nj = nq // 2
    tm = _pick(rows, (512, 256))
    hb = tm // FFN_HALO
    once = pl.Buffered(1)
    ni = rows // tm
    last_blk = rows // FFN_HALO - 1
    ext = tm + 2 * FFN_HALO
    tile = slice(FFN_HALO, FFN_HALO + tm)
    blocks = _col_blocks(tc, MXU_WIDTH)

    def body(h_ref, hp_ref, hn_ref, d_ref, dn_ref, x_ref, wg_ref, wv_ref, cw_ref, cb_ref, dx_ref, dw_out_ref, dcw_ref, dcb_ref,
             dw_ref):
        i = pl.program_id(1)
        first = i == 0
        last = i == ni - 1

        @pl.when(first)
        def _():
            dw_ref[...] = jnp.zeros_like(dw_ref)
            dcw_ref[...] = jnp.zeros_like(dcw_ref)
            dcb_ref[...] = jnp.zeros_like(dcb_ref)

        xt = x_ref[...].astype(BF16).T
        dx = None
        for cs in blocks:
            wc = cs.stop - cs.start
            d_next = dn_ref[:, cs].astype(F32)[0:FFN_HALO]
            de = jnp.concatenate([jnp.zeros((FFN_HALO, wc), F32), d_ref[:, cs].astype(F32), jnp.where(last, 0.0, d_next)], axis=0)
            taps, hc = [], []
            for s in range(2):
                e = jnp.concatenate([jnp.where(first, 0.0, hp_ref[s, :, cs]), h_ref[s, :, cs], hn_ref[s, :, cs]], axis=0)
                y, tp = _conv3(e, cw_ref[s, :, cs], cb_ref[s, :, cs])
                hc.append(y)
                taps.append(tp)
            gl, th = _gelu(hc[0])
            dhc = (de * hc[1] * _gelu_grad(hc[0], th), de * gl)
            for s, w_ref in ((0, wg_ref), (1, wv_ref)):
                w = cw_ref[s, :, cs]
                g = dhc[s]
                dh = (w[2:3, :] * g + w[1:2, :] * pltpu.roll(g, ext - 1, 0) + w[0:1, :] * pltpu.roll(g, ext - 2, 0))[tile]
                gt = g[tile]
                for k in range(3):
                    dcw_ref[s, k:k + 1, cs] += _rowsum(gt * taps[s][k][tile])
                dcb_ref[s, :, cs] += _rowsum(gt)
                dhb = dh.astype(BF16)
                part = lax.dot_general(dhb, w_ref[:, cs], (((1,), (1,)), ((), ())), preferred_element_type=F32)
                dx = part if dx is None else dx + part
                dw_ref[s, :, cs] += jnp.dot(xt, dhb, preferred_element_type=F32)
        dx_ref[...] = dx

        @pl.when(last)
        def _():
            dw_out_ref[...] = dw_ref[...].astype(BF16)

    in_specs = [
        pl.BlockSpec((2, tm, tc), lambda j, i: (0, i, j)),
        pl.BlockSpec((2, FFN_HALO, tc), lambda j, i: (0, jnp.maximum(i * hb - 1, 0), j)),
        pl.BlockSpec((2, FFN_HALO, tc), lambda j, i: (0, jnp.minimum((i + 1) * hb, last_blk), j)),
        pl.BlockSpec((tm, tc), lambda j, i: (i, j)),
        pl.BlockSpec((BF16_ROWS, tc), lambda j, i: (jnp.minimum((i + 1) * (tm // BF16_ROWS), rows // BF16_ROWS - 1), j)),
        pl.BlockSpec((tm, d), lambda j, i: (i, 0)),
        pl.BlockSpec((None, d, tc), lambda j, i: (j, 0, 0), pipeline_mode=once),
        pl.BlockSpec((None, d, tc), lambda j, i: (nj + j, 0, 0), pipeline_mode=once),
        pl.BlockSpec((2, 3, tc), lambda j, i: (0, 0, j)),
        pl.BlockSpec((2, 1, tc), lambda j, i: (0, 0, j)),
    ]
    out_specs = [
        pl.BlockSpec((None, tm, d), lambda j, i: (j, i, 0)),
        pl.BlockSpec((2, None, d, tc), lambda j, i: (0, j, 0, 0), pipeline_mode=once),
        pl.BlockSpec((2, 3, tc), lambda j, i: (0, 0, j)),
        pl.BlockSpec((2, 1, tc), lambda j, i: (0, 0, j)),
    ]
    out_shape = [jax.ShapeDtypeStruct((nj, rows, d), F32), jax.ShapeDtypeStruct((2, nj, d, tc), BF16),
                 jax.ShapeDtypeStruct((2, 3, f), F32), jax.ShapeDtypeStruct((2, 1, f), F32)]
    dx, dw, dcw, dcb = pl.pallas_call(body, name=name, grid=(nj, ni), in_specs=in_specs, out_specs=out_specs,
                                      out_shape=out_shape, scratch_shapes=[pltpu.VMEM((2, d, tc), F32)],
                                      compiler_params=_cp("parallel", "arbitrary"))(
        hf, hf, hf, df, df, x, w_up, w_up, cw, cb)
    return dx, dw.reshape(nq, d, tc), dcw, dcb


def _mixer_fwd(x, w_in, cw, cb, ga, ba, gb, bb, ws, sbb, *, name):
    rows, d = x.shape
    _, _, w = w_in.shape
    t = _pick(rows, (256,))
    groups = w // B_CHUNK

    def body(x_ref, win_ref, cw_ref, cb_ref, ga_ref, ba_ref, gb_ref, bb_ref, ws_ref, sb_ref, h_ref, o_ref, a2_ref, prev_ref):
        @pl.when(pl.program_id(0) == 0)
        def _():
            prev_ref[...] = jnp.zeros_like(prev_ref)

        xb = x_ref[...].astype(BF16)
        for s in range(4):
            h_ref[s] = jnp.dot(xb, win_ref[s], preferred_element_type=F32)
        a1 = h_ref[0] * _sig(h_ref[1])
        e = jnp.concatenate([prev_ref[...], a1], axis=0)
        prev_ref[...] = a1[t - CONV_HALO:]
        acc = cw_ref[A_KERNEL - 1:A_KERNEL, :] * e
        for k in range(A_KERNEL - 1):
            acc = acc + cw_ref[k:k + 1, :] * pltpu.roll(e, A_KERNEL - 1 - k, 0)
        a2 = acc[CONV_HALO:] + cb_ref[...]
        a2_ref[...] = a2
        xh, _ = _ln_stats(a2)
        a3 = xh * ga_ref[...] + ba_ref[...]
        o_ref[:, 0:w] = (a3 * _sig(a3)).astype(BF16)

        u, _ = _gelu(h_ref[2])
        v1, _ = _gelu(h_ref[3])
        xh2, _ = _ln_stats(v1)
        v2 = (xh2 * gb_ref[...] + bb_ref[...]).astype(BF16)
        for c in range(t // B_CHUNK):
            rs = slice(c * B_CHUNK, (c + 1) * B_CHUNK)
            for g in range(groups):
                cs = slice(g * B_CHUNK, (g + 1) * B_CHUNK)
                mixed = jnp.dot(ws_ref[g], v2[rs, cs], preferred_element_type=F32) + sb_ref[g]
                o_ref[rs, w + g * B_CHUNK:w + (g + 1) * B_CHUNK] = (u[rs, cs] * mixed).astype(BF16)

    vec = pl.BlockSpec((1, w), lambda i: (0, 0))
    grp = pl.BlockSpec((groups, B_CHUNK, B_CHUNK), lambda i: (0, 0, 0))
    in_specs = [
        pl.BlockSpec((t, d), lambda i: (i, 0)),
        pl.BlockSpec((4, d, w), lambda i: (0, 0, 0)),
        pl.BlockSpec((A_KERNEL, w), lambda i: (0, 0)),
        vec, vec, vec, vec, vec, grp, grp,
    ]
    out_specs = [pl.BlockSpec((4, t, w), lambda i: (0, i, 0)), pl.BlockSpec((t, 2 * w), lambda i: (i, 0)),
                 pl.BlockSpec((t, w), lambda i: (i, 0))]
    out_shape = [jax.ShapeDtypeStruct((4, rows, w), F32), jax.ShapeDtypeStruct((rows, 2 * w), BF16),
                 jax.ShapeDtypeStruct((rows, w), F32)]
    return pl.pallas_call(body, name=name, grid=(rows // t,), in_specs=in_specs, out_specs=out_specs, out_shape=out_shape,
                          scratch_shapes=[pltpu.VMEM((CONV_HALO, w), F32)],
                          compiler_params=_cp("arbitrary"))(x, w_in, cw, cb, ga, ba, gb, bb, ws, sbb)


def _mixer_bwd(h0, a2, dab, x, w_in, res, res_scale, cw, ga, ba, gb, bb, ws, wst, sbb, tril, *, name):
    _, rows, w = h0.shape
    d = x.shape[1]
    once = pl.Buffered(1)
    t = _pick(rows, (256,))
    hb = t // CONV_HALO
    ni = rows // t
    last_blk = rows // CONV_HALO - 1
    ext = t + CONV_HALO
    tile = slice(0, t)
    groups = w // B_CHUNK
    taps = A_KERNEL - 1

    def body(h_ref, a2_ref, a2n_ref, d_ref, dn_ref, x_ref, win_ref, res_ref, cw_ref, ga_ref, ba_ref, gb_ref, bb_ref,
             ws_ref, wst_ref, sb_ref, tril_ref, dx_ref, dwin_ref, dcw_ref, dcb_ref, dga_ref, dba_ref, dgb_ref, dbb_ref,
             dws_ref, dsb_ref, dw_ref):
        i = pl.program_id(0)
        first = i == 0
        last = i == ni - 1

        @pl.when(first)
        def _():
            for r in (dw_ref, dcw_ref, dcb_ref, dga_ref, dba_ref, dgb_ref, dbb_ref, dws_ref, dsb_ref):
                r[...] = jnp.zeros_like(r)

        xt = x_ref[...].astype(BF16).T
        dx_terms = []

        def through_w_in(slot, dh):
            dhb = dh.astype(BF16)
            dx_terms.append(lax.dot_general(dhb, win_ref[slot], (((1,), (1,)), ((), ())), preferred_element_type=F32))
            dw_ref[slot] += jnp.dot(xt, dhb, preferred_element_type=F32)

        xh, rstd = _ln_stats(jnp.concatenate([a2_ref[...], a2n_ref[...]], axis=0))
        a3 = xh * ga_ref[...] + ba_ref[...]
        s3 = _sig(a3)
        da_e = jnp.concatenate([d_ref[:, 0:w], jnp.where(last, 0.0, dn_ref[...])], axis=0)
        da3 = da_e * (s3 * (1.0 + a3 * (1.0 - s3)))
        da2 = _ln_bwd(da3 * ga_ref[...], xh, rstd)
        dga_ref[...] += _rowsum(da3[tile] * xh[tile])
        dba_ref[...] += _rowsum(da3[tile])
        dcb_ref[...] += _rowsum(da2[tile])
        sgt = _sig(h_ref[1])
        a1t = h_ref[0] * sgt
        da1t = None
        for k in range(A_KERNEL):
            sh = taps - k
            fed = (da2 if sh == 0 else pltpu.roll(da2, ext - sh, 0))[tile]
            dcw_ref[k:k + 1, :] += _rowsum(a1t * fed)
            term = cw_ref[k:k + 1, :] * fed
            da1t = term if da1t is None else da1t + term
        through_w_in(0, da1t * sgt)
        through_w_in(1, da1t * h_ref[0] * sgt * (1.0 - sgt))

        bu = h_ref[2]
        bv = h_ref[3]
        u, tu = _gelu(bu)
        v1, tv = _gelu(bv)
        xh2, rstd2 = _ln_stats(v1)
        v2 = (xh2 * gb_ref[...] + bb_ref[...]).astype(BF16)
        db = d_ref[:, w:2 * w]
        dmx_all = db * u
        du_parts, dv2_parts = [], []
        for c in range(t // B_CHUNK):
            rs = slice(c * B_CHUNK, (c + 1) * B_CHUNK)
            du_row, dv2_row = [], []
            for g in range(groups):
                cs = slice(g * B_CHUNK, (g + 1) * B_CHUNK)
                v2cg = v2[rs, cs]
                mixed = jnp.dot(ws_ref[g], v2cg, preferred_element_type=F32) + sb_ref[g]
                dmx = dmx_all[rs, cs]
                dmxb = dmx.astype(BF16)
                du_row.append(db[rs, cs] * mixed)
                dv2_row.append(jnp.dot(wst_ref[g], dmxb, preferred_element_type=F32))
                dws_ref[g] += tril_ref[...] * lax.dot_general(dmxb, v2cg, (((1,), (1,)), ((), ())),
                                                               preferred_element_type=F32)
                dsb_ref[g:g + 1, :] += _rowsum(dmx.T)
            du_parts.append(jnp.concatenate(du_row, axis=1))
            dv2_parts.append(jnp.concatenate(dv2_row, axis=1))
        du = jnp.concatenate(du_parts, axis=0)
        dv2 = jnp.concatenate(dv2_parts, axis=0)
        dgb_ref[...] += _rowsum(dv2 * xh2)
        dbb_ref[...] += _rowsum(dv2)
        dv1 = _ln_bwd(dv2 * gb_ref[...], xh2, rstd2)
        through_w_in(2, du * _gelu_grad(bu, tu))
        through_w_in(3, dv1 * _gelu_grad(bv, tv))
        dx_ref[...] = res_scale * res_ref[...] + ((dx_terms[0] + dx_terms[1]) + (dx_terms[2] + dx_terms[3]))

        @pl.when(last)
        def _():
            dwin_ref[...] = dw_ref[...].astype(BF16)

    vec = pl.BlockSpec((1, w), lambda i: (0, 0))
    grp = pl.BlockSpec((groups, B_CHUNK, B_CHUNK), lambda i: (0, 0, 0))
    halo = pl.BlockSpec((CONV_HALO, w), lambda i: (jnp.minimum((i + 1) * hb, last_blk), 0))
    wide = pl.BlockSpec((t, d), lambda i: (i, 0))
    in_specs = [
        pl.BlockSpec((4, t, w), lambda i: (0, i, 0)),
        pl.BlockSpec((t, w), lambda i: (i, 0)),
        halo,
        pl.BlockSpec((t, 2 * w), lambda i: (i, 0)),
        halo,
        wide,
        pl.BlockSpec((4, d, w), lambda i: (0, 0, 0), pipeline_mode=once),
        wide,
        pl.BlockSpec((A_KERNEL, w), lambda i: (0, 0)),
        vec, vec, vec, vec, grp, grp, grp,
        pl.BlockSpec((B_CHUNK, B_CHUNK), lambda i: (0, 0)),
    ]
    vsds = jax.ShapeDtypeStruct((1, w), F32)
    out_specs = [
        wide,
        pl.BlockSpec((4, d, w), lambda i: (0, 0, 0), pipeline_mode=once),
        pl.BlockSpec((A_KERNEL, w), lambda i: (0, 0)),
        vec, vec, vec, vec, vec, grp,
        pl.BlockSpec((groups, B_CHUNK), lambda i: (0, 0)),
    ]
    out_shape = [jax.ShapeDtypeStruct((rows, d), F32), jax.ShapeDtypeStruct((4, d, w), BF16),
                 jax.ShapeDtypeStruct((A_KERNEL, w), F32),
                 vsds, vsds, vsds, vsds, vsds, jax.ShapeDtypeStruct((groups, B_CHUNK, B_CHUNK), F32),
                 jax.ShapeDtypeStruct((groups, B_CHUNK), F32)]
    return pl.pallas_call(body, name=name, grid=(ni,), in_specs=in_specs, out_specs=out_specs, out_shape=out_shape,
                          scratch_shapes=[pltpu.VMEM((4, d, w), F32)], compiler_params=_cp("arbitrary"))(
        h0, a2, a2, dab, dab, x, w_in, res, cw, ga, ba, gb, bb, ws, wst, sbb, tril)


GROUP_ROWS = Q_PER_KV * ATT_BLOCK


def _attn_mask(n):
    qi = lax.broadcasted_iota(jnp.int32, (GROUP_ROWS, 2 * ATT_BLOCK), 0) & (ATT_BLOCK - 1)
    sj = lax.broadcasted_iota(jnp.int32, (GROUP_ROWS, 2 * ATT_BLOCK), 1)
    diff = qi + ATT_BLOCK - sj
    return (diff >= 0) & (diff < ATT_BLOCK) & ((n > 0) | (sj >= ATT_BLOCK))


def _stack_heads(ref, kvh, dtype):
    heads = [ref[:, (kvh * Q_PER_KV + g) * HEAD_DIM:(kvh * Q_PER_KV + g + 1) * HEAD_DIM] for g in range(Q_PER_KV)]
    return jnp.concatenate(heads, axis=0).astype(dtype)


def _per_row_sink(sink_ref, kvh):
    head = lax.broadcasted_iota(jnp.int32, (GROUP_ROWS, 1), 0) // ATT_BLOCK
    out = jnp.zeros((GROUP_ROWS, 1), F32)
    for g in range(Q_PER_KV):
        out = jnp.where(head == g, sink_ref[kvh * Q_PER_KV + g], out)
    return out


def _attn_specs(rows, n_q):
    dq = n_q * HEAD_DIM
    dkv = 2 * (n_q // Q_PER_KV) * HEAD_DIM
    kv_blk = dq // dkv
    assert dq % dkv == 0
    return dq, dkv, [
        pl.BlockSpec(memory_space=pltpu.SMEM),
        pl.BlockSpec((ATT_BLOCK, dq), lambda n: (n, 0)),
        pl.BlockSpec((ATT_BLOCK, dkv), lambda n: (n, kv_blk)),
        pl.BlockSpec((ATT_BLOCK, dkv), lambda n: (jnp.maximum(n - 1, 0), kv_blk)),
    ]


def _kv_pair(kvc_ref, kvp_ref, kvh, n_kv):
    ks = slice(kvh * HEAD_DIM, (kvh + 1) * HEAD_DIM)
    vs = slice((n_kv + kvh) * HEAD_DIM, (n_kv + kvh + 1) * HEAD_DIM)
    kk = jnp.concatenate([kvp_ref[:, ks], kvc_ref[:, ks]], axis=0).astype(BF16)
    vv = jnp.concatenate([kvp_ref[:, vs], kvc_ref[:, vs]], axis=0).astype(BF16)
    return kk, vv


def _attn_fwd(qkv, sinks, *, name):
    rows = qkv.shape[0]
    n_q = sinks.shape[0]
    n_kv = n_q // Q_PER_KV
    scale = 1.0 / math.sqrt(HEAD_DIM)
    dq, _, in_specs = _attn_specs(rows, n_q)

    def body(sink_ref, q_ref, kvc_ref, kvp_ref, o_ref, lse_ref):
        valid = _attn_mask(pl.program_id(0))
        for kvh in range(n_kv):
            kk, vv = _kv_pair(kvc_ref, kvp_ref, kvh, n_kv)
            qs = _stack_heads(q_ref, kvh, BF16)
            s = lax.dot_general(qs, kk, (((1,), (1,)), ((), ())), preferred_element_type=F32)
            s = jnp.where(valid, s * scale, -jnp.inf)
            sk = _per_row_sink(sink_ref, kvh)
            m = jnp.maximum(jnp.max(s, axis=1, keepdims=True), sk)
            p = jnp.exp(s - m)
            l = jnp.sum(p, axis=1, keepdims=True) + jnp.exp(sk - m)
            o = jnp.dot((p / l).astype(BF16), vv, preferred_element_type=F32)
            lse = m + jnp.log(l)
            for g in range(Q_PER_KV):
                h = kvh * Q_PER_KV + g
                rs = slice(g * ATT_BLOCK, (g + 1) * ATT_BLOCK)
                o_ref[:, h * HEAD_DIM:(h + 1) * HEAD_DIM] = o[rs]
                lse_ref[:, h:h + 1] = lse[rs]

    out_specs = [pl.BlockSpec((ATT_BLOCK, dq), lambda n: (n, 0)), pl.BlockSpec((ATT_BLOCK, n_q), lambda n: (n, 0))]
    out_shape = [jax.ShapeDtypeStruct((rows, dq), F32), jax.ShapeDtypeStruct((rows, n_q), F32)]
    return pl.pallas_call(body, name=name, grid=(rows // ATT_BLOCK,), in_specs=in_specs, out_specs=out_specs,
                          out_shape=out_shape, compiler_params=_cp("parallel"))(sinks, qkv, qkv, qkv)


def _attn_bwd(qkv, dout, lse, sinks, *, name):
    rows = qkv.shape[0]
    n_q = sinks.shape[0]
    n_kv = n_q // Q_PER_KV
    scale = 1.0 / math.sqrt(HEAD_DIM)
    dq_w, dkv_w, in_specs = _attn_specs(rows, n_q)
    blk_q = pl.BlockSpec((ATT_BLOCK, dq_w), lambda n: (n, 0))
    blk_kv = pl.BlockSpec((ATT_BLOCK, dkv_w), lambda n: (n, 0))
    in_specs = in_specs + [blk_q, pl.BlockSpec((ATT_BLOCK, n_q), lambda n: (n, 0))]

    def body(sink_ref, q_ref, kvc_ref, kvp_ref, do_ref, lse_ref, dq_ref, dkc_ref, dkp_ref, dsink_ref):
        n = pl.program_id(0)

        @pl.when(n == 0)
        def _():
            dsink_ref[...] = jnp.zeros_like(dsink_ref)

        valid = _attn_mask(n)
        head_ids = lax.broadcasted_iota(jnp.int32, (1, n_q), 1)
        dsink = jnp.zeros((1, n_q), F32)
        for kvh in range(n_kv):
            kk, vv = _kv_pair(kvc_ref, kvp_ref, kvh, n_kv)
            qs = _stack_heads(q_ref, kvh, BF16)
            dos = _stack_heads(do_ref, kvh, BF16)
            lse = jnp.concatenate([lse_ref[:, kvh * Q_PER_KV + g:kvh * Q_PER_KV + g + 1] for g in range(Q_PER_KV)], axis=0)
            s = lax.dot_general(qs, kk, (((1,), (1,)), ((), ())), preferred_element_type=F32)
            s = jnp.where(valid, s * scale, -jnp.inf)
            p = jnp.exp(s - lse)
            dp = lax.dot_general(dos, vv, (((1,), (1,)), ((), ())), preferred_element_type=F32)
            delta = jnp.sum(p * dp, axis=1, keepdims=True)
            ds = (p * (dp - delta) * scale).astype(BF16)
            sink_term = jnp.exp(_per_row_sink(sink_ref, kvh) - lse) * delta
            dqs = jnp.dot(ds, kk, preferred_element_type=F32)
            for g in range(Q_PER_KV):
                h = kvh * Q_PER_KV + g
                rs = slice(g * ATT_BLOCK, (g + 1) * ATT_BLOCK)
                dsink = dsink + jnp.where(head_ids == h, -jnp.sum(sink_term[rs]), 0.0)
                dq_ref[:, h * HEAD_DIM:(h + 1) * HEAD_DIM] = dqs[rs]
            dk = lax.dot_general(ds, qs, (((0,), (0,)), ((), ())), preferred_element_type=F32)
            dv = lax.dot_general(p.astype(BF16), dos, (((0,), (0,)), ((), ())), preferred_element_type=F32)
            ks = slice(kvh * HEAD_DIM, (kvh + 1) * HEAD_DIM)
            vs = slice((n_kv + kvh) * HEAD_DIM, (n_kv + kvh + 1) * HEAD_DIM)
            dkp_ref[:, ks] = dk[0:ATT_BLOCK]
            dkc_ref[:, ks] = dk[ATT_BLOCK:]
            dkp_ref[:, vs] = dv[0:ATT_BLOCK]
            dkc_ref[:, vs] = dv[ATT_BLOCK:]
        dsink_ref[...] += dsink

    out_specs = [blk_q, blk_kv, blk_kv, pl.BlockSpec((1, n_q), lambda n: (0, 0))]
    out_shape = [jax.ShapeDtypeStruct((rows, dq_w), F32), jax.ShapeDtypeStruct((rows, dkv_w), F32),
                 jax.ShapeDtypeStruct((rows, dkv_w), F32), jax.ShapeDtypeStruct((1, n_q), F32)]
    return pl.pallas_call(body, name=name, grid=(rows // ATT_BLOCK,), in_specs=in_specs, out_specs=out_specs,
                          out_shape=out_shape, compiler_params=_cp("arbitrary"))(sinks, qkv, qkv, qkv, dout, lse)


def _dqkv_assemble(dq, dkc, dkp, *, name):
    rows, dq_w = dq.shape
    dkv_w = dkc.shape[1]
    nb = rows // ATT_BLOCK

    def body(dq_ref, dkc_ref, dkp_ref, o_ref, db_ref):
        n = pl.program_id(0)

        @pl.when(n == 0)
        def _():
            db_ref[...] = jnp.zeros_like(db_ref)

        dqv = dq_ref[...]
        dkv = dkc_ref[...] + jnp.where(n == nb - 1, 0.0, dkp_ref[...])
        o_ref[:, 0:dq_w] = dqv.astype(BF16)
        o_ref[:, dq_w:dq_w + dkv_w] = dkv.astype(BF16)
        db_ref[:, 0:dq_w] += _rowsum(dqv)
        db_ref[:, dq_w:dq_w + dkv_w] += _rowsum(dkv)

    width = dq_w + dkv_w
    in_specs = [pl.BlockSpec((ATT_BLOCK, dq_w), lambda n: (n, 0)), pl.BlockSpec((ATT_BLOCK, dkv_w), lambda n: (n, 0)),
                pl.BlockSpec((ATT_BLOCK, dkv_w), lambda n: (jnp.minimum(n + 1, nb - 1), 0))]
    out_specs = [pl.BlockSpec((ATT_BLOCK, width), lambda n: (n, 0)), pl.BlockSpec((1, width), lambda n: (0, 0))]
    out_shape = [jax.ShapeDtypeStruct((rows, width), BF16), jax.ShapeDtypeStruct((1, width), F32)]
    return pl.pallas_call(body, name=name, grid=(nb,), in_specs=in_specs, out_specs=out_specs, out_shape=out_shape,
                          compiler_params=_cp("arbitrary"))(dq, dkc, dkp)


def _row_tile(r, c):
    budget = 2 * 1024 * 1024 // (4 * c)
    for cand in (1024, 512, 256, 128, 64, 32, 16):
        if cand <= budget and r % cand == 0:
            return cand
    return r


def _octo_sum(own, recv, place, dest, lead, *, name):
    _, _, r, c = own.shape
    t = _row_tile(r, c)
    lead_idx, buf_shape = lead

    def body(place_ref, own_ref, *rest):
        o_ref = rest[7] if dest is None else rest[8]
        acc = own_ref[...].astype(F32)
        for k in range(7):
            acc = acc + rest[k][...].astype(F32)
        o_ref[...] = acc

    def peer(mask):
        return pl.BlockSpec((None, t, c), lambda i, pr: (pr[2] ^ mask, i, 0))

    if lead_idx is None:
        o_spec = pl.BlockSpec((None, t, c), lambda i, pr: (pr[1], i, 0))
    else:
        o_spec = pl.BlockSpec((None, None, t, c), lambda i, pr: (lead_idx, pr[1], i, 0))
    in_specs = [pl.BlockSpec((None, None, t, c), lambda i, pr: (pr[0], pr[1], i, 0))] + [peer(m) for m in range(1, 8)]
    args = [place, own] + [recv] * 7
    aliases = {}
    if dest is not None:
        in_specs.append(HBM)
        args.append(dest)
        aliases = {9: 0}
    grid_spec = pltpu.PrefetchScalarGridSpec(num_scalar_prefetch=1, grid=(r // t,), in_specs=in_specs, out_specs=o_spec)
    return pl.pallas_call(body, name=name, grid_spec=grid_spec, out_shape=jax.ShapeDtypeStruct(buf_shape, F32),
                          input_output_aliases=aliases, compiler_params=_cp("parallel"))(*args)


def _adamw_math(w, g, m, v):
    nm = ADAM_B1 * m + (1.0 - ADAM_B1) * g
    nv = ADAM_B2 * v + (1.0 - ADAM_B2) * (g * g)
    m_hat = nm / (1.0 - ADAM_B1 ** ADAM_STEP)
    v_hat = nv / (1.0 - ADAM_B2 ** ADAM_STEP)
    return -ADAM_LR * (m_hat / (jnp.sqrt(v_hat) + ADAM_EPS) + ADAM_WD * w), nm, nv


def _adamw(w, g, m, v, *, name):
    r, c = w.shape
    t = _row_tile(r, c)

    def body(w_ref, g_ref, m_ref, v_ref, d_ref, nm_ref, nv_ref, go_ref):
        gv = g_ref[...]
        d_ref[...], nm_ref[...], nv_ref[...] = _adamw_math(w_ref[...], gv, m_ref[...], v_ref[...])
        go_ref[...] = gv

    blk = pl.BlockSpec((t, c), lambda i: (i, 0))
    sds = jax.ShapeDtypeStruct((r, c), F32)
    return pl.pallas_call(body, name=name, grid=(r // t,), in_specs=[blk] * 4, out_specs=[blk] * 4,
                          out_shape=[sds] * 4, compiler_params=_cp("parallel"))(w, g, m, v)


HBM = pl.BlockSpec(memory_space=pl.ANY)


def _place():
    x, y, c = lax.axis_index("x"), lax.axis_index("y"), lax.axis_index("c")
    chips = [(1 - x, y), (x, 1 - y), (1 - x, 1 - y)]
    return x, y, c, 2 * x + y, (x, y, 1 - c), chips


def _rcopy(src, dst, ssem, rsem, dev):
    return pltpu.make_async_remote_copy(src_ref=src, dst_ref=dst, send_sem=ssem, recv_sem=rsem, device_id=dev,
                                        device_id_type=MESH)


HBM_ONLY = pl.BlockSpec(memory_space=pltpu.HBM)
SEM = pl.BlockSpec(memory_space=pltpu.SEMAPHORE)


def _peers():
    x, y, c = lax.axis_index("x"), lax.axis_index("y"), lax.axis_index("c")
    out = []
    for mask in range(1, 8):
        px = 1 - x if mask & 4 else x
        py = 1 - y if mask & 2 else y
        pc = 1 - c if mask & 1 else c
        out.append(((px, py, pc), 2 * px + py, pc, 4 * px + 2 * py + pc))
    return 4 * x + 2 * y + c, out


def _reduce_start(grads, lands, after, *, name, whole=False):
    nt = len(grads)

    def body(*refs):
        ssems, rsems = refs[2 * nt + 1:3 * nt + 1], refs[3 * nt + 1:4 * nt + 1]
        g_out, l_out, token = refs[4 * nt + 1:5 * nt + 1], refs[5 * nt + 1:6 * nt + 1], refs[6 * nt + 1]
        me, peers = _peers()
        for t in range(nt):
            for k, (dev, chip, core, _) in enumerate(peers):
                src = g_out[t] if whole else g_out[t].at[chip, core]
                _rcopy(src, l_out[t].at[me], ssems[t].at[k], rsems[t].at[k], dev).start()
        token[...] = jnp.zeros_like(token)

    sems = [pltpu.SemaphoreType.DMA((7,))] * (2 * nt)
    out_shape = (sems + [pltpu.HBM(g.shape, g.dtype) for g in grads] + [pltpu.HBM(l.shape, l.dtype) for l in lands]
                 + [jax.ShapeDtypeStruct((8, LANES), F32)])
    res = pl.pallas_call(
        body, name=name, in_specs=[HBM_ONLY] * (2 * nt + 1),
        out_specs=[SEM] * (2 * nt) + [HBM_ONLY] * (2 * nt) + [pl.BlockSpec(memory_space=pltpu.VMEM)], out_shape=out_shape,
        input_output_aliases={t: 2 * nt + t for t in range(2 * nt)},
        compiler_params=pltpu.CompilerParams(has_side_effects=DATAFLOW),
    )(*[pltpu.with_memory_space_constraint(a, pltpu.HBM) for a in list(grads) + list(lands) + [after]])
    return res[:nt], res[nt:2 * nt], res[2 * nt:3 * nt], res[3 * nt:4 * nt], res[4 * nt]


def _reduce_wait(grads, lands, ssems, rsems, after, *, name, whole=False):
    nt = len(grads)

    def body(*refs):
        ssem_refs, rsem_refs = refs[2 * nt:3 * nt], refs[3 * nt:4 * nt]
        g_out, l_out = refs[4 * nt + 1:5 * nt + 1], refs[5 * nt + 1:6 * nt + 1]
        me, peers = _peers()
        for t in range(nt):
            for k, (dev, chip, core, _) in enumerate(peers):
                src = g_out[t] if whole else g_out[t].at[chip, core]
                _rcopy(src, l_out[t].at[me], ssem_refs[t].at[k], rsem_refs[t].at[k], dev).wait_send()
        for t in range(nt):
            for k, (dev, _, _, idx) in enumerate(peers):
                slot = l_out[t].at[idx]
                _rcopy(slot, slot, ssem_refs[t].at[k], rsem_refs[t].at[k], dev).wait_recv()

    res = pl.pallas_call(
        body, name=name, in_specs=[HBM_ONLY] * (2 * nt) + [SEM] * (2 * nt) + [HBM_ONLY], out_specs=[HBM_ONLY] * (2 * nt),
        out_shape=[pltpu.HBM(a.shape, a.dtype) for a in list(grads) + list(lands)],
        input_output_aliases={t: t for t in range(2 * nt)},
        compiler_params=pltpu.CompilerParams(has_side_effects=DATAFLOW),
    )(*grads, *lands, *ssems, *rsems, pltpu.with_memory_space_constraint(after, pltpu.HBM))
    return list(res[:nt]), list(res[nt:])
DATAFLOW = pltpu.SideEffectType.DATAFLOW_SIDE_EFFECTING


def _gather_now(bufs, *, name):
    nt = len(bufs)

    def body(*refs):
        outs = refs[nt:2 * nt]
        ssem, rsem = refs[2 * nt:]
        x, y, c, q, sib, chips = _place()
        sends = []
        for t in range(nt):
            for j, (px, py) in enumerate(chips):
                mine = outs[t].at[q, c]
                cp = _rcopy(mine, mine, ssem.at[t, j], rsem.at[t, j], (px, py, c))
                cp.start()
                sends.append(cp)
        for t in range(nt):
            for j, (px, py) in enumerate(chips):
                landed = outs[t].at[2 * px + py, c]
                _rcopy(landed, landed, ssem.at[t, j], rsem.at[t, j], (px, py, c)).wait_recv()
                cp = _rcopy(landed, landed, ssem.at[t, 3 + j], rsem.at[t, 3 + j], sib)
                cp.start()
                sends.append(cp)
        for t in range(nt):
            for j, (px, py) in enumerate(chips):
                passed = outs[t].at[2 * px + py, 1 - c]
                _rcopy(passed, passed, ssem.at[t, 3 + j], rsem.at[t, 3 + j], sib).wait_recv()
        for cp in sends:
            cp.wait_send()

    out_shape = [jax.ShapeDtypeStruct(b.shape, b.dtype) for b in bufs]
    return pl.pallas_call(
        body, name=name, in_specs=[HBM] * nt, out_specs=[HBM] * nt, out_shape=out_shape,
        input_output_aliases={t: t for t in range(nt)},
        scratch_shapes=[pltpu.SemaphoreType.DMA((nt, 6)), pltpu.SemaphoreType.DMA((nt, 6))],
    )(*bufs)


def _gather_start(bufs, half, after, *, name):
    nt = len(bufs)

    def body(*refs):
        ssems, rsems, outs = refs[nt + 1:2 * nt + 1], refs[2 * nt + 1:3 * nt + 1], refs[3 * nt + 1:4 * nt + 1]
        x, y, c, q, sib, chips = _place()
        for t in range(nt):
            for j, (px, py) in enumerate(chips):
                mine = outs[t].at[q, c] if half[t] else outs[t].at[q]
                _rcopy(mine, mine, ssems[t].at[j], rsems[t].at[j], (px, py, c)).start()

    sems = [pltpu.SemaphoreType.DMA((3,))] * (2 * nt)
    out_shape = sems + [pltpu.HBM(b.shape, b.dtype) for b in bufs]
    res = pl.pallas_call(
        body, name=name, in_specs=[HBM_ONLY] * (nt + 1), out_specs=[SEM] * (2 * nt) + [HBM_ONLY] * nt, out_shape=out_shape,
        input_output_aliases={t: 2 * nt + t for t in range(nt)},
        compiler_params=pltpu.CompilerParams(has_side_effects=DATAFLOW),
    )(*[pltpu.with_memory_space_constraint(b, pltpu.HBM) for b in list(bufs) + [after]])
    return res[:nt], res[nt:2 * nt], res[2 * nt:]


def _gather_wait(bufs, half, ssems, rsems, after, *, name):
    nt = len(bufs)

    def body(*refs):
        ssem_refs, rsem_refs = refs[nt:2 * nt], refs[2 * nt:3 * nt]
        outs = refs[3 * nt + 1:]
        x, y, c, q, sib, chips = _place()
        for t in range(nt):
            for j, (px, py) in enumerate(chips):
                mine = outs[t].at[q, c] if half[t] else outs[t].at[q]
                _rcopy(mine, mine, ssem_refs[t].at[j], rsem_refs[t].at[j], (px, py, c)).wait_send()
        for t in range(nt):
            for j, (px, py) in enumerate(chips):
                theirs = outs[t].at[2 * px + py, c] if half[t] else outs[t].at[2 * px + py]
                _rcopy(theirs, theirs, ssem_refs[t].at[j], rsem_refs[t].at[j], (px, py, c)).wait_recv()

    res = pl.pallas_call(
        body, name=name, in_specs=[HBM_ONLY] * nt + [SEM] * (2 * nt) + [HBM], out_specs=[HBM_ONLY] * nt,
        out_shape=[pltpu.HBM(b.shape, b.dtype) for b in bufs], input_output_aliases={t: t for t in range(nt)},
        compiler_params=pltpu.CompilerParams(has_side_effects=DATAFLOW),
    )(*bufs, *ssems, *rsems, after)
    return list(res)


def _sibling_swap(bufs, *, name):
    nt = len(bufs)

    def body(*refs):
        outs = refs[nt:2 * nt]
        ssem, rsem = refs[2 * nt:]
        x, y, c, q, sib, chips = _place()
        sends = []
        for t in range(nt):
            for j, (px, py) in enumerate(chips):
                held = outs[t].at[2 * px + py, c]
                cp = _rcopy(held, held, ssem.at[t, j], rsem.at[t, j], sib)
                cp.start()
                sends.append(cp)
        for t in range(nt):
            for j, (px, py) in enumerate(chips):
                other = outs[t].at[2 * px + py, 1 - c]
                _rcopy(other, other, ssem.at[t, j], rsem.at[t, j], sib).wait_recv()
        for cp in sends:
            cp.wait_send()

    return pl.pallas_call(
        body, name=name, in_specs=[HBM] * nt, out_specs=[HBM] * nt,
        out_shape=[jax.ShapeDtypeStruct(b.shape, b.dtype) for b in bufs], input_output_aliases={t: t for t in range(nt)},
        scratch_shapes=[pltpu.SemaphoreType.DMA((nt, 3)), pltpu.SemaphoreType.DMA((nt, 3))],
    )(*bufs)


def _sibling_share(bufs, layout, *, name):
    no = len(bufs)
    nt = len(layout)

    def body(*refs):
        outs = refs[no:2 * no]
        ssem, rsem = refs[2 * no:]
        x, y, c, q, sib, chips = _place()

        def slot(t, half):
            o, lead = layout[t]
            return outs[o].at[half] if lead is None else outs[o].at[lead, half]

        sends = []
        for t in range(nt):
            cp = _rcopy(slot(t, c), slot(t, c), ssem.at[t], rsem.at[t], sib)
            cp.start()
            sends.append(cp)
        for t in range(nt):
            _rcopy(slot(t, 1 - c), slot(t, 1 - c), ssem.at[t], rsem.at[t], sib).wait_recv()
        for cp in sends:
            cp.wait_send()

    out_shape = [jax.ShapeDtypeStruct(b.shape, b.dtype) for b in bufs]
    return pl.pallas_call(
        body, name=name, in_specs=[HBM] * no, out_specs=[HBM] * no, out_shape=out_shape,
        input_output_aliases={o: o for o in range(no)},
        scratch_shapes=[pltpu.SemaphoreType.DMA((nt,)), pltpu.SemaphoreType.DMA((nt,))],
    )(*bufs)


SMALL_GROUP_OF = (0,) * 6 + (1,) * 2 + (2,) * 2 + (3,) * 4 + (4,) * 9


def _small_pack(local, dims, which, *, name):
    kw, wa, ng, nqkv, nsk, f, dm = dims
    shapes = _small_shapes(dims)
    row_vec = 8 * (-(-kw // 8))
    pos = sorted(local)
    assert all(SMALL_GROUP_OF[p] in which for p in pos)

    def pack_body(*refs):
        loc = dict(zip(pos, refs[:len(pos)]))
        grp = dict(zip(which, refs[len(pos):]))
        for gr in grp.values():
            gr[...] = jnp.zeros_like(gr)
        if 0 in which:
            grp[0][0:kw, :] = loc[0][...]
            for k in range(5):
                grp[0][row_vec + k:row_vec + k + 1, :] = loc[1 + k][...]
        if 1 in which:
            for g in range(ng):
                grp[1][g * B_CHUNK:(g + 1) * B_CHUNK, :] = loc[6][g]
            grp[1][ng * B_CHUNK:ng * B_CHUNK + ng, :] = loc[7][...]
        if 2 in which:
            grp[2][0:1, :] = loc[8][...]
            grp[2][1:2, 0:nsk] = loc[9][...]
        if 3 in which:
            for l in range(2):
                for s in range(2):
                    grp[3][l, s, 0:3, :] = loc[10 + 2 * l][s]
                    grp[3][l, s, 3:4, :] = loc[11 + 2 * l][s]
        if 4 in which:
            for k in range(9):
                grp[4][k:k + 1, :] = loc[14 + k][...]

    vm = pl.BlockSpec(memory_space=pltpu.VMEM)
    return pl.pallas_call(
        pack_body, name=name, in_specs=[vm] * len(pos), out_specs=[vm] * len(which),
        out_shape=[jax.ShapeDtypeStruct(shapes[g], F32) for g in which],
        compiler_params=pltpu.CompilerParams(vmem_limit_bytes=VMEM_LIMIT),
    )(*[local[p] for p in pos])


def _small_shapes(dims):
    kw, wa, ng, nqkv, nsk, f, dm = dims
    return [(8 * (-(-kw // 8)) + 8, wa), (ng * B_CHUNK + 8, B_CHUNK), (8, nqkv), (2, 2, 8, f), (16, dm)]


def _small_update(groups, landed, params, dims, *, name):
    kw, wa, ng, nqkv, nsk, f, dm = dims
    shapes = _small_shapes(dims)
    row_vec = 8 * (-(-kw // 8))
    n_grp = len(shapes)
    flat_params = [a for triple in params for a in triple]
    n_par = len(params)
    vm = pl.BlockSpec(memory_space=pltpu.VMEM)

    def adamw_body(*refs):
        own = refs[:n_grp]
        land = refs[n_grp:2 * n_grp]
        par = refs[2 * n_grp:2 * n_grp + 3 * n_par]
        outs = refs[2 * n_grp + 3 * n_par:2 * n_grp + 7 * n_par]
        loss_ref = refs[2 * n_grp + 7 * n_par]
        tot = refs[2 * n_grp + 7 * n_par + 1:]
        x, y = lax.axis_index("x"), lax.axis_index("y")
        q = 2 * x + y
        me = 4 * x + 2 * y + lax.axis_index("c")
        for gi in range(n_grp):
            acc = None
            for dv in range(8):
                term = jnp.where(me == dv, own[gi][...], land[gi][dv])
                acc = term if acc is None else acc + term
            tot[gi][...] = acc
        ta, tb, tc, td, te = tot

        def mine(piece):
            out = piece(0)
            for k in range(1, 4):
                out = jnp.where(q == k, piece(k), out)
            return out

        def update(p, grad, index=None):
            at = (lambda r: r[...]) if index is None else (lambda r: r[index])
            w_ref, m_ref, v_ref = par[3 * p:3 * p + 3]
            g_ref, d_ref, nm_ref, nv_ref = outs[4 * p:4 * p + 4]
            delta, nm, nv = _adamw_math(at(w_ref), grad, at(m_ref), at(v_ref))
            for r, val in ((g_ref, grad), (d_ref, delta), (nm_ref, nm), (nv_ref, nv)):
                if index is None:
                    r[...] = val
                else:
                    r[index] = val

        wq = wa // 4
        update(0, mine(lambda k: ta[0:kw, k * wq:(k + 1) * wq]), (0,))
        for k in range(5):
            update(1 + k, ta[row_vec + k:row_vec + k + 1, :])
        for g in range(ng):
            update(6, tb[g * B_CHUNK:(g + 1) * B_CHUNK, :], (0, g))
        update(7, tb[ng * B_CHUNK:ng * B_CHUNK + ng, :], (0,))
        nq4 = nqkv // 4
        update(8, mine(lambda k: tc[0:1, k * nq4:(k + 1) * nq4]))
        update(9, tc[1:2, 0:nsk])
        fh = f // 2
        for l in range(2):
            update(10, mine(lambda k: td[l, k // 2, 0:3, (k % 2) * fh:(k % 2 + 1) * fh]), (l,))
            update(11, jnp.concatenate([td[l, 0, 3:4, :], td[l, 1, 3:4, :]], axis=1), (slice(l, l + 1),))
        dq4 = dm // 4
        for i in range(2):
            for j in range(2):
                for p, base in ((12, 0), (13, 4)):
                    row = base + 2 * i + j
                    update(p, mine(lambda k: te[row:row + 1, k * dq4:(k + 1) * dq4]), (i, slice(j, j + 1)))
        loss_ref[...] = (0.5 / dm) * jnp.sum(te[8:9, :], axis=1, keepdims=True)

    out_shape = []
    for w, _, _ in params:
        out_shape += [jax.ShapeDtypeStruct(w.shape, F32)] * 4
    out_shape.append(jax.ShapeDtypeStruct((1, 1), F32))
    res = pl.pallas_call(
        adamw_body, name=name + "_adamw", in_specs=[vm] * (2 * n_grp + 3 * n_par), out_specs=[vm] * len(out_shape),
        out_shape=out_shape, scratch_shapes=[pltpu.VMEM(s, F32) for s in shapes],
        compiler_params=pltpu.CompilerParams(vmem_limit_bytes=VMEM_LIMIT),
    )(*groups, *landed, *flat_params)
    return [res[4 * p:4 * p + 4] for p in range(n_par)], res[-1]


def _pack(arrays, rows_multiple):
    flat = jnp.concatenate([a.reshape(-1) for a in arrays])
    rows = -(-flat.shape[0] // LANES)
    rows = -(-rows // rows_multiple) * rows_multiple
    flat = jnp.pad(flat, (0, rows * LANES - flat.shape[0]))
    return flat.reshape(rows, LANES)


def _unshard_cols(stacked):
    moved = jnp.moveaxis(stacked, 0, -2)
    return moved.reshape(moved.shape[:-2] + (4 * stacked.shape[-1],))


def kernel(x, ab_w_in, a_conv_w, a_conv_b, a_norm_g, a_norm_b, b_norm_g, b_norm_b, b_spatial_w, b_spatial_b, ab_w_out, c_w_qkv, c_b_qkv, c_sinks, c_w_o, ffn_w_up, ffn_conv_w, ffn_conv_b, ffn_w_down, ln_g, ln_b, loss_target, m_ab_w_in, m_a_conv_w, m_a_conv_b, m_a_norm_g, m_a_norm_b, m_b_norm_g, m_b_norm_b, m_b_spatial_w, m_b_spatial_b, m_ab_w_out, m_c_w_qkv, m_c_b_qkv, m_c_sinks, m_c_w_o, m_ffn_w_up, m_ffn_conv_w, m_ffn_conv_b, m_ffn_w_down, m_ln_g, m_ln_b, v_ab_w_in, v_a_conv_w, v_a_conv_b, v_a_norm_g, v_a_norm_b, v_b_norm_g, v_b_norm_b, v_b_spatial_w, v_b_spatial_b, v_ab_w_out, v_c_w_qkv, v_c_b_qkv, v_c_sinks, v_c_w_o, v_ffn_w_up, v_ffn_conv_w, v_ffn_conv_b, v_ffn_w_down, v_ln_g, v_ln_b):
    rows, d = x.shape[1], x.shape[2]
    depth = ln_g.shape[0]
    assert depth == 2 and x.shape[0] == 1
    alpha = (2.0 * depth) ** 0.25
    f = ffn_w_down.shape[1] * 4
    n_q = c_sinks.shape[1]
    q_idx = 2 * lax.axis_index("x") + lax.axis_index("y")
    c_idx = lax.axis_index("c")
    xs, tgt = x[0], loss_target[0]

    def own_slot(part):
        buf = lax.empty((4,) + part.shape, part.dtype)
        return lax.dynamic_update_slice(buf, part[None], (q_idx, 0, 0, 0))

    def halves(wm):
        return own_slot(wm.astype(BF16).reshape((2, wm.shape[0] // 2) + wm.shape[1:]))

    small_sharded = [a_conv_w[0], c_b_qkv[0], ffn_conv_w, ln_g, ln_b]
    small_pack = _pack(small_sharded, 16)
    bufs = [halves(ab_w_in[0]), own_slot(small_pack.reshape(2, small_pack.shape[0] // 2, LANES)), halves(ab_w_out[0]),
            halves(ffn_w_up[0]), halves(ffn_w_down[0]), halves(c_w_qkv[0]), halves(c_w_o[0]),
            halves(ffn_w_up[1]), halves(ffn_w_down[1])]
    whole = lambda g: g.reshape(4, 2 * g.shape[2], g.shape[3])
    n_now = 2
    first_two = _gather_now(bufs[:n_now], name="gather_now")
    w_in, small_all = [whole(g) for g in first_two]
    later = bufs[n_now:]
    half = [True, True] + [False] * (len(later) - 2)
    ssems, rsems, started = _gather_start(later, half, first_two[1], name="gather_start")

    def arrive(idx, after, tag):
        idx = [i - n_now for i in idx]
        halved = [half[i] for i in idx]
        got = _gather_wait([started[i] for i in idx], halved, [ssems[i] for i in idx], [rsems[i] for i in idx], after,
                           name=f"gather_wait_{tag}")
        if all(halved):
            got = _sibling_swap(got, name=f"gather_swap_{tag}")
        return [whole(g) for g in got]

    small_all = small_all.reshape(4, -1)
    sh_shapes = [s.shape for s in small_sharded]
    pieces, pos = [], 0
    for s in sh_shapes:
        n = math.prod(s)
        pieces.append(_unshard_cols(small_all[:, pos:pos + n].reshape((4,) + s)))
        pos += n
    conv_w_a, b_qkv, conv_w_f, ln_gf, ln_bf = pieces

    tril = jnp.tril(jnp.ones((B_CHUNK, B_CHUNK), F32))
    ws = (b_spatial_w[0] * tril).astype(BF16)
    wst = jnp.swapaxes(ws, 1, 2)
    sbb = jnp.broadcast_to(b_spatial_b[0][:, :, None], b_spatial_w[0].shape)
    mix_vecs = [a_conv_b, a_norm_g, a_norm_b, b_norm_g, b_norm_b]
    cw_f = [jnp.swapaxes(conv_w_f[l].reshape(3, 2, f), 0, 1) for l in range(depth)]
    cb_f = [ffn_conv_b[l].reshape(2, 1, f) for l in range(depth)]
    lng = lambda i, j: ln_gf[i, j].reshape(1, d)
    lnb = lambda i, j: ln_bf[i, j].reshape(1, d)
    sinks = c_sinks[0]

    w_up, w_down = [None, None], [None, None]

    def ffn_fwd(xin, l):
        w_up[l], = arrive([3 + 4 * l], xin, f"up{l}")
        hf, fact = _ffn_up_fwd(xin, w_up[l], cw_f[l], cb_f[l], name=f"ffn{l}_up")
        w_down[l] = arrive([4 + 4 * l], fact, f"down{l}")[0].reshape(-1, d)
        return hf, fact

    h0, ab, a2 = _mixer_fwd(xs, w_in, conv_w_a, *mix_vecs, ws, sbb, name="mix_fwd")
    w_out = arrive([2], ab, "out")[0].reshape(-1, d)
    mix, x1 = _proj_add_ln(ab, w_out, xs, lng(0, 0), lnb(0, 0), alpha, name="mix_out_ln00")
    hf0, f0 = ffn_fwd(x1, 0)
    ffn0, x2 = _proj_add_ln(f0, w_down[0], x1, lng(0, 1), lnb(0, 1), alpha, name="ffn0_down_ln01")
    w_qkv = _unshard_cols(arrive([5], x2, "qkv")[0])
    qkv = _matmul(x2, w_qkv, name="att_qkv", tm=1024, tn=w_qkv.shape[1], tk=1024, bias=b_qkv.reshape(1, -1))
    ao, lse = _attn_fwd(qkv, sinks, name="att_core")
    w_o = arrive([6], ao, "o")[0].reshape(-1, d)
    att, x3 = _proj_add_ln(ao, w_o, x2, lng(1, 0), lnb(1, 0), alpha, name="att_out_ln10")
    hf1, f1 = ffn_fwd(x3, 1)
    ffn1 = _matmul(f1, w_down[1], name="ffn1_down", tm=512, tn=1024, tk=2816)
    sq_err, dy = _add_ln_loss(x3, ffn1, lng(1, 1), lnb(1, 1), tgt, alpha, name="ln11_loss")

    def owner_view(g):
        if g.ndim == 3:
            return g.reshape(4, 2, g.shape[1] // 2, g.shape[2])
        return g.reshape(4, 2, g.shape[0] // 8, g.shape[1])

    in_flight = []

    def send_grads(tag, grads, after):
        lands = [lax.empty((8,) + g.shape[2:], BF16) for g in grads]
        ss, rs, g_thru, l_thru, token = _reduce_start(grads, lands, after, name=f"reduce_start_{tag}")
        in_flight.append((tag, g_thru, l_thru, ss, rs))
        return token[0:1, 0:1]

    def ffn_bwd(dz, xin, hf, fact, l):
        d_wdown = _matmul(fact, dz, name=f"ffn{l}_down_dw", ta=True, tm=1408, tn=1024, tk=2048, out_dtype=BF16)
        dfa = _matmul(dz, w_down[l], name=f"ffn{l}_down_dx", tb=True, tm=512, tn=2816, tk=1024, out_dtype=BF16)
        dx_parts, d_wup, dcw, dcb = _ffn_up_bwd(hf, dfa, xin, w_up[l], cw_f[l], cb_f[l], name=f"ffn{l}_up_bwd")
        tok = send_grads(f"ffn{l}", [owner_view(d_wup), owner_view(d_wdown)], dcb)
        return [(dx_parts, 1.0), (dz, alpha)], dcw, dcb, tok

    dz, dg11, db11 = _add_ln_bwd([(dy, 1.0)], x3, ffn1, lng(1, 1), alpha, name="ln11_bwd")
    dx3, dcw1, dcb1, tok = ffn_bwd(dz, x3, hf1, f1, 1)
    dz, dg10, db10 = _add_ln_bwd(dx3, x2, att, lng(1, 0) + tok, alpha, name="ln10_bwd")
    d_wo = _matmul(ao, dz, name="att_out_dw", ta=True, tm=1024, tn=1024, tk=2048, out_dtype=BF16)
    dao = _matmul(dz, w_o, name="att_out_dx", tb=True, tm=1024, tn=1024, tk=1024)
    dq, dkc, dkp, d_sinks = _attn_bwd(qkv, dao, lse, sinks, name="att_core_bwd")
    dqkv, d_bqkv = _dqkv_assemble(dq, dkc, dkp, name="att_dqkv")
    d_wqkv = _matmul(x2, dqkv, name="att_qkv_dw", ta=True, tm=1024, tn=dqkv.shape[1], tk=1024, out_dtype=BF16)
    d_wqkv_st = jnp.moveaxis(d_wqkv.reshape(d_wqkv.shape[0], 4, -1), 1, 0)
    tok = send_grads("att", [owner_view(d_wqkv_st), owner_view(d_wo)], d_bqkv)
    dx2 = _matmul(dqkv, w_qkv, name="att_qkv_dx", tb=True, tm=1024, tn=1024, tk=dqkv.shape[1], addend=(dz, alpha))
    dz, dg01, db01 = _add_ln_bwd([(dx2, 1.0)], x1, ffn0, lng(0, 1) + tok, alpha, name="ln01_bwd")
    dx1, dcw0, dcb0, tok = ffn_bwd(dz, x1, hf0, f0, 0)
    dz, dg00, db00 = _add_ln_bwd(dx1, xs, mix, lng(0, 0) + tok, alpha, name="ln00_bwd")
    d_wout = _matmul(ab, dz, name="mix_out_dw", ta=True, tm=1024, tn=1024, tk=2048, out_dtype=BF16)
    dab = _matmul(dz, w_out, name="mix_out_dx", tb=True, tm=1024, tn=1024, tk=1024)
    small_dims = (a_conv_w.shape[1], a_conv_b.shape[1], b_spatial_w.shape[1], 4 * c_b_qkv.shape[1], n_q, f, d)

    def send_small(local_arrays, which, tag, after):
        groups = _small_pack(local_arrays, small_dims, which, name=f"small_pack_{tag}")
        lands = [lax.empty((8,) + g.shape, F32) for g in groups]
        return _reduce_start(groups, lands, after, name=f"small_start_{tag}", whole=True)

    ready = [d_bqkv, d_sinks, dcw0, dcb0, dcw1, dcb1, dg00, dg01, dg10, dg11, db00, db01, db10, db11, sq_err]
    ss_e, rs_e, g_e, l_e, token = send_small(dict(zip(range(8, 23), ready)), (2, 3, 4), "early", dab)
    tok = send_grads("out", [owner_view(d_wout)], token)
    grad_x, d_win, d_cwa, d_cba, d_ga, d_ba, d_gb, d_bb, d_ws, d_sb = _mixer_bwd(
        h0, a2, dab, xs, w_in, dz, alpha, conv_w_a, *mix_vecs[1:], ws, wst, sbb, tril + tok, name="mix_bwd")

    small_w = [a_conv_w, a_conv_b, a_norm_g, a_norm_b, b_norm_g, b_norm_b, b_spatial_w, b_spatial_b, c_b_qkv,
               c_sinks, ffn_conv_w, ffn_conv_b, ln_g, ln_b]
    small_m = [m_a_conv_w, m_a_conv_b, m_a_norm_g, m_a_norm_b, m_b_norm_g, m_b_norm_b, m_b_spatial_w, m_b_spatial_b,
               m_c_b_qkv, m_c_sinks, m_ffn_conv_w, m_ffn_conv_b, m_ln_g, m_ln_b]
    small_v = [v_a_conv_w, v_a_conv_b, v_a_norm_g, v_a_norm_b, v_b_norm_g, v_b_norm_b, v_b_spatial_w, v_b_spatial_b,
               v_c_b_qkv, v_c_sinks, v_ffn_conv_w, v_ffn_conv_b, v_ln_g, v_ln_b]
    place = jnp.stack([q_idx, c_idx, 4 * lax.axis_index("x") + 2 * lax.axis_index("y") + c_idx]).astype(jnp.int32)
    where = {"mix": [(0, None)], "out": [(1, None)], "att": [(2, None), (3, None)], "ffn0": [(4, 0), (5, 0)],
             "ffn1": [(4, 1), (5, 1)]}
    big_w = [ab_w_in, ab_w_out, c_w_qkv, c_w_o, ffn_w_up, ffn_w_down]
    big_m = [m_ab_w_in, m_ab_w_out, m_c_w_qkv, m_c_w_o, m_ffn_w_up, m_ffn_w_down]
    big_v = [v_ab_w_in, v_ab_w_out, v_c_w_qkv, v_c_w_o, v_ffn_w_up, v_ffn_w_down]
    big_out = [None] * 6

    def finish(tags, after, label):
        bufs, layout = {}, []
        for tag, g_thru, l_thru, ss, rs in in_flight:
            if tag not in tags:
                continue
            own, landed = _reduce_wait(g_thru, l_thru, ss, rs, after, name=f"reduce_wait_{tag}")
            for k, (o, lead) in enumerate(where[tag]):
                piece = own[k].shape[2:]
                shape = (2,) + piece if lead is None else (2, 2) + piece
                bufs[o] = _octo_sum(own[k], landed[k], place, bufs.get(o), (lead, shape), name=f"reduce_sum_{tag}{k}")
                layout.append((o, lead))
        order = sorted(bufs)
        shared = _sibling_share([bufs[o] for o in order], [(order.index(o), lead) for o, lead in layout],
                                name=f"reduce_share_{label}")
        for o, g in zip(order, shared):
            w = big_w[o]
            two_d = lambda a: a.reshape(-1, a.shape[-1])
            outs = _adamw(two_d(w), two_d(g), two_d(big_m[o]), two_d(big_v[o]), name=f"adamw_big{o}")
            big_out[o] = [r.reshape(w.shape) for r in outs]
        return big_out[order[-1]][0]

    mixer_small = [d_cwa, d_cba, d_ga, d_ba, d_gb, d_bb, d_ws, d_sb]
    ss_l, rs_l, g_l, l_l, token = send_small(dict(zip(range(8), mixer_small)), (0, 1), "late", grad_x)
    tok = send_grads("mix", [owner_view(d_win)], after=token)
    done = finish(("ffn1", "att", "ffn0", "out"), d_ws + tok, "early")
    g_l, l_l = _reduce_wait(g_l, l_l, ss_l, rs_l, done, name="small_wait_late", whole=True)
    g_e, l_e = _reduce_wait(g_e, l_e, ss_e, rs_e, done, name="small_wait_early", whole=True)
    small_out, loss = _small_update(g_l + g_e, l_l + l_e, list(zip(small_w, small_m, small_v)), small_dims,
                                    name="small_tail")
    loss = loss[0, 0]
    small_g = [o[0] for o in small_out]
    sm_delta = [o[1] for o in small_out]
    sm_m = [o[2] for o in small_out]
    sm_v = [o[3] for o in small_out]
    finish(("mix",), sm_delta[6], "mix")

    order_big = {0: 0, 9: 1, 10: 2, 13: 3, 14: 4, 17: 5}
    order_small = {1: 0, 2: 1, 3: 2, 4: 3, 5: 4, 6: 5, 7: 6, 8: 7, 11: 8, 12: 9, 15: 10, 16: 11, 18: 12, 19: 13}
    grads, deltas, new_m, new_v = [], [], [], []
    for pos_w in range(20):
        if pos_w in order_big:
            t = order_big[pos_w]
            grads.append(big_out[t][3])
            deltas.append(big_out[t][0])
            new_m.append(big_out[t][1])
            new_v.append(big_out[t][2])
        else:
            t = order_small[pos_w]
            grads.append(small_g[t])
            deltas.append(sm_delta[t])
            new_m.append(sm_m[t])
            new_v.append(sm_v[t])
    return (loss, grad_x[None], *grads, *deltas, *new_m, *new_v)
```
